```python
import math
import jax
import jax.numpy as jnp
from jax import lax
import numpy as np

D_MODEL = 1024
BATCH = 8
SEQ = 2048
DEPTH = 2
DEC_BATCH = 128
DEC_SEQ = 4
PAST_LEN = 16384
PAGE_SIZE = 128

GROUP_W = D_MODEL // 4
SSD_HD = 64
SSD_H = GROUP_W // SSD_HD
SSD_G = 2
SSD_N = 128
SSD_CONV = 4
SSD_CONV_CH = GROUP_W + 2 * SSD_G * SSD_N
SSD_CHUNK = 64
RET_HD = 64
RET_H = GROUP_W // RET_HD
RET_CHUNK = 64
ROPE_BASE = 10000.0
RG_BLOCKS = 4
RG_BD = GROUP_W // RG_BLOCKS
RG_CONV = 4
RG_C = 8.0
HG_HD = 64
HG_H = GROUP_W // HG_HD
HG_CHUNK = 16
N_MEM = 256
XA_H = 4
XA_HD = D_MODEL // XA_H
D_FF = -(-8 * D_MODEL // (3 * 256)) * 256
IN_SIZES = (GROUP_W, SSD_CONV_CH, SSD_H, GROUP_W, GROUP_W, GROUP_W, GROUP_W, GROUP_W, GROUP_W, GROUP_W, GROUP_W, GROUP_W, GROUP_W)
D_IN = sum(IN_SIZES)
EPS = 1e-6

kernel_name = 'hybrid_ssd_ret_rglru_hgrn2_decode_step'


def _rmsnorm(x, g):
    xf = x.astype(jnp.float32)
    y = xf * lax.rsqrt(jnp.mean(xf * xf, axis=-1, keepdims=True) + EPS)
    return (y * g).astype(x.dtype)


def _headnorm(y, g, center):
    yf = y.astype(jnp.float32)
    if center:
        yf = yf - jnp.mean(yf, axis=-1, keepdims=True)
    yf = yf * lax.rsqrt(jnp.mean(yf * yf, axis=-1, keepdims=True) + EPS)
    return (yf.reshape(y.shape[:-2] + (-1,)) * g).astype(y.dtype)


def _causal_conv(x, buf, w, b):
    W = w.shape[0]
    T = x.shape[1]
    xx = jnp.concatenate([buf, x], axis=1)
    y = b + sum(xx[:, j:j + T] * w[j] for j in range(W))
    return y, xx[:, T:]


def _rope(x, pos):
    half = x.shape[-1] // 2
    inv = ROPE_BASE ** (-jnp.arange(half, dtype=jnp.float32) / half)
    ang = pos[:, None] * inv
    cos = jnp.cos(ang)[None, :, None, :]
    sin = jnp.sin(ang)[None, :, None, :]
    xf = x.astype(jnp.float32)
    x1, x2 = xf[..., :half], xf[..., half:]
    return jnp.concatenate([x1 * cos - x2 * sin, x2 * cos + x1 * sin], axis=-1).astype(x.dtype)


def _chunked_scalar_decay(q, k, v, log_a, s0, chunk):
    B, T, H, K = q.shape
    V = v.shape[-1]
    C = math.gcd(T, chunk)
    N = T // C
    f32 = jnp.float32
    qc = q.reshape(B, N, C, H, K).astype(f32)
    kc = k.reshape(B, N, C, H, K).astype(f32)
    vc = v.reshape(B, N, C, H, V).astype(f32)
    cum = jnp.cumsum(log_a.astype(f32).reshape(B, N, C, H), axis=2)
    cum_h = jnp.moveaxis(cum, 3, 2)
    mask = jnp.tril(jnp.ones((C, C), dtype=bool))
    dec = jnp.exp(jnp.where(mask, cum_h[..., :, None] - cum_h[..., None, :], -jnp.inf))
    scores = jnp.einsum('bnthk,bnshk->bnhts', qc, kc) * dec
    y = jnp.einsum('bnhts,bnshv->bnthv', scores, vc)
    last = cum[:, :, -1]
    k_end = kc * jnp.exp(last[:, :, None] - cum)[..., None]
    ds = jnp.einsum('bnshk,bnshv->bnhkv', k_end, vc)

    def step(S, inp):
        a, d = inp
        return a[..., None, None] * S + d, S

    s_fin, s_in = lax.scan(step, s0.astype(f32), (jnp.moveaxis(jnp.exp(last), 1, 0), jnp.moveaxis(ds, 1, 0)))
    s_in = jnp.moveaxis(s_in, 0, 1)
    y = y + jnp.einsum('bnthk,bnhkv->bnthv', qc * jnp.exp(cum)[..., None], s_in)
    return y.reshape(B, T, H, V).astype(v.dtype), s_fin.astype(s0.dtype)


def _chunked_vector_decay(q, k, v, log_f, s0, chunk):
    B, T, H, K = q.shape
    V = v.shape[-1]
    C = math.gcd(T, chunk)
    N = T // C
    f32 = jnp.float32
    qc = q.reshape(B, N, C, H, K).astype(f32)
    kc = k.reshape(B, N, C, H, K).astype(f32)
    vc = v.reshape(B, N, C, H, V).astype(f32)
    cum = jnp.cumsum(log_f.astype(f32).reshape(B, N, C, H, K), axis=2)
    cum_h = jnp.moveaxis(cum, 3, 2)
    mask = jnp.tril(jnp.ones((C, C), dtype=bool))[:, :, None]
    dec = jnp.exp(jnp.where(mask, cum_h[:, :, :, :, None, :] - cum_h[:, :, :, None, :, :], -jnp.inf))
    A = jnp.einsum('bnthk,bnhtsk,bnshk->bnhts', qc, dec, kc)
    y = jnp.einsum('bnhts,bnshv->bnthv', A, vc)
    last = cum[:, :, -1]
    k_end = kc * jnp.exp(last[:, :, None] - cum)
    ds = jnp.einsum('bnshk,bnshv->bnhkv', k_end, vc)

    def step(S, inp):
        a, d = inp
        return a[..., None] * S + d, S

    s_fin, s_in = lax.scan(step, s0.astype(f32), (jnp.moveaxis(jnp.exp(last), 1, 0), jnp.moveaxis(ds, 1, 0)))
    s_in = jnp.moveaxis(s_in, 0, 1)
    y = y + jnp.einsum('bnthk,bnhkv->bnthv', qc * jnp.exp(cum), s_in)
    return y.reshape(B, T, H, V).astype(v.dtype), s_fin.astype(s0.dtype)


def _linear_scan(a, b, h0):
    b = b.at[:, 0].add(a[:, 0] * h0)

    def comb(l, r):
        return (l[0] * r[0], r[0] * l[1] + r[1])

    _, h = lax.associative_scan(comb, (a, b), axis=1)
    return h


def _mixer(h, st, P, l, pos0):
    ssd_s, ssd_buf, ret_s, rg_h, rg_buf, hg_s = st
    B, T, _ = h.shape
    f32 = jnp.float32
    u = h @ P['w_in'][l]
    offs = np.cumsum(IN_SIZES)[:-1].tolist()
    (s_z, s_xbc, s_dt, r_q, r_k, r_v, r_g, g_x, g_y, h_q, h_f, h_i, h_g) = jnp.split(u, offs, axis=-1)

    xbc, ssd_buf_new = _causal_conv(s_xbc, ssd_buf, P['ssd_conv_w'][l], P['ssd_conv_b'][l])
    xbc = jax.nn.silu(xbc)
    xs, bm, cm = jnp.split(xbc, [GROUP_W, GROUP_W + SSD_G * SSD_N], axis=-1)
    xs = xs.reshape(B, T, SSD_H, SSD_HD)
    rep = SSD_H // SSD_G
    bm = jnp.repeat(bm.reshape(B, T, SSD_G, SSD_N), rep, axis=2)
    cm = jnp.repeat(cm.reshape(B, T, SSD_G, SSD_N), rep, axis=2)
    dt = jax.nn.softplus(s_dt + P['ssd_dt_bias'][l])
    log_a = -jnp.exp(P['ssd_A_log'][l]) * dt
    y, ssd_s_new = _chunked_scalar_decay(cm, bm, xs * dt[..., None], log_a, ssd_s, SSD_CHUNK)
    y = y + P['ssd_D'][l][:, None] * xs
    y_ssd = _rmsnorm(y.reshape(B, T, GROUP_W) * jax.nn.silu(s_z), P['ssd_norm'][l])

    pos = pos0 + jnp.arange(T, dtype=f32)
    rq = _rope(r_q.reshape(B, T, RET_H, RET_HD), pos)
    rk = _rope(r_k.reshape(B, T, RET_H, RET_HD), pos) * RET_HD ** -0.5
    rv = r_v.reshape(B, T, RET_H, RET_HD)
    log_gamma = jnp.log(1.0 - jnp.exp2(-5.0 - jnp.arange(RET_H, dtype=f32)))
    y, ret_s_new = _chunked_scalar_decay(rq, rk, rv, jnp.broadcast_to(log_gamma, (B, T, RET_H)), ret_s, RET_CHUNK)
    y_ret = jax.nn.silu(r_g) * _headnorm(y, P['ret_norm'][l], True)

    xr, rg_buf_new = _causal_conv(g_x, rg_buf, P['rg_conv_w'][l], P['rg_conv_b'][l])
    xb = xr.reshape(B, T, RG_BLOCKS, RG_BD)
    r_gate = jax.nn.sigmoid(jnp.einsum('bthi,hij->bthj', xb, P['rg_wa'][l]).reshape(B, T, GROUP_W) + P['rg_ba'][l])
    i_gate = jax.nn.sigmoid(jnp.einsum('bthi,hij->bthj', xb, P['rg_wx'][l]).reshape(B, T, GROUP_W) + P['rg_bx'][l])
    log_ar = (-RG_C * jax.nn.softplus(-P['rg_lambda'][l]) * r_gate).astype(f32)
    a_r = jnp.exp(log_ar)
    b_r = jnp.sqrt(-jnp.expm1(2.0 * log_ar)) * (i_gate * xr).astype(f32)
    hseq = _linear_scan(a_r, b_r, rg_h.astype(f32))
    rg_h_new = hseq[:, -1].astype(rg_h.dtype)
    y_rg = (jax.nn.gelu(g_y) * hseq).astype(h.dtype)

    lb_sm = jax.nn.softmax(P['hg_lb'].astype(f32), axis=0)
    lb = (jnp.cumsum(lb_sm, axis=0) - lb_sm[0])[l]
    fg = lb + (1.0 - lb) * jax.nn.sigmoid(h_f.astype(f32))
    shp = (B, T, HG_H, HG_HD)
    y, hg_s_new = _chunked_vector_decay(jax.nn.silu(h_q).reshape(shp), (1.0 - fg).reshape(shp),
                                        h_i.reshape(shp), jnp.log(fg).reshape(shp), hg_s, HG_CHUNK)
    y_hg = jax.nn.silu(h_g) * _headnorm(y, P['hg_norm'][l], False)

    mix = jnp.concatenate([y_ssd, y_ret, y_rg, y_hg], axis=-1) @ P['w_out'][l]
    return mix, (ssd_s_new, ssd_buf_new, ret_s_new, rg_h_new, rg_buf_new, hg_s_new)


def _mem_kv(mem, P, l):
    B, M, _ = mem.shape
    kv = _rmsnorm(mem, P['ln_mem'][l]) @ P['w_xkv'][l]
    k, v = jnp.split(kv, 2, axis=-1)
    return k.reshape(B, M, XA_H, XA_HD), v.reshape(B, M, XA_H, XA_HD)


def _xattn(h, k, v, P, l):
    B, T, _ = h.shape
    q = (h @ P['w_xq'][l]).reshape(B, T, XA_H, XA_HD)
    s = jnp.einsum('bthd,bmhd->bhtm', q, k).astype(jnp.float32) * XA_HD ** -0.5
    p = jax.nn.softmax(s, axis=-1).astype(v.dtype)
    o = jnp.einsum('bhtm,bmhd->bthd', p, v).reshape(B, T, D_MODEL)
    return o @ P['w_xo'][l]


def _ffn(h, P, l):
    g, u = jnp.split(h @ P['w_gu'][l], 2, axis=-1)
    return (jax.nn.silu(g) * u) @ P['w_down'][l]


def _layer(x, st, k_mem, v_mem, P, l, pos0):
    m, st_new = _mixer(_rmsnorm(x, P['ln_mix_pre'][l]), st, P, l, pos0)
    x = x + _rmsnorm(m, P['ln_mix_post'][l])
    a = _xattn(_rmsnorm(x, P['ln_xa_pre'][l]), k_mem, v_mem, P, l)
    x = x + _rmsnorm(a, P['ln_xa_post'][l])
    f = _ffn(_rmsnorm(x, P['ln_ffn_pre'][l]), P, l)
    x = x + _rmsnorm(f, P['ln_ffn_post'][l])
    return x, st_new


def _stack_field(states, i):
    return jnp.stack([s[i] for s in states], axis=0)


def setup_inputs(seed: int = 0) -> dict:
    key = jax.random.key(seed)
    keys = iter(jax.random.split(key, 64))
    f32 = jnp.float32
    L = DEPTH
    D = D_MODEL

    def nrm(shape, scale=1.0):
        return scale * jax.random.normal(next(keys), shape, f32)

    def gain(shape):
        return 1.0 + 0.02 * nrm(shape)

    def unif(shape, lo, hi):
        return jax.random.uniform(next(keys), shape, f32, lo, hi)

    dt0 = jnp.exp(unif((L, SSD_H), math.log(1e-3), math.log(1e-1)))
    a_rg = unif((L, GROUP_W), 0.9, 0.999) ** (1.0 / RG_C)
    return {
        'x_prompt': nrm((BATCH, SEQ, D)),
        'x_sample': nrm((DEC_BATCH, DEC_SEQ, D)),
        'state_ssd': nrm((L, DEC_BATCH, SSD_H, SSD_N, SSD_HD), 0.1),
        'state_ssd_conv': nrm((L, DEC_BATCH, SSD_CONV - 1, SSD_CONV_CH)),
        'state_ret': nrm((L, DEC_BATCH, RET_H, RET_HD, RET_HD)),
        'state_rglru': nrm((L, DEC_BATCH, GROUP_W), 0.5),
        'state_rglru_conv': nrm((L, DEC_BATCH, RG_CONV - 1, GROUP_W)),
        'state_hgrn': nrm((L, DEC_BATCH, HG_H, HG_HD, HG_HD), 0.5),
        'cache_mem_k': nrm((L, DEC_BATCH, N_MEM, XA_H, XA_HD)),
        'cache_mem_v': nrm((L, DEC_BATCH, N_MEM, XA_H, XA_HD)),
        'mem_prompt': nrm((BATCH, N_MEM, D)),
        'ln_mix_pre': gain((L, D)),
        'ln_mix_post': gain((L, D)),
        'ln_xa_pre': gain((L, D)),
        'ln_xa_post': gain((L, D)),
        'ln_ffn_pre': gain((L, D)),
        'ln_ffn_post': gain((L, D)),
        'w_in': nrm((L, D, D_IN), D ** -0.5),
        'ssd_conv_w': nrm((L, SSD_CONV, SSD_CONV_CH), SSD_CONV ** -0.5),
        'ssd_conv_b': nrm((L, SSD_CONV_CH), 0.01),
        'ssd_dt_bias': dt0 + jnp.log(-jnp.expm1(-dt0)),
        'ssd_A_log': jnp.log(unif((L, SSD_H), 1.0, 16.0)),
        'ssd_D': gain((L, SSD_H)),
        'ssd_norm': gain((L, GROUP_W)),
        'ret_norm': gain((L, GROUP_W)),
        'rg_conv_w': nrm((L, RG_CONV, GROUP_W), RG_CONV ** -0.5),
        'rg_conv_b': nrm((L, GROUP_W), 0.01),
        'rg_wa': nrm((L, RG_BLOCKS, RG_BD, RG_BD), RG_BD ** -0.5),
        'rg_ba': nrm((L, GROUP_W), 0.01),
        'rg_wx': nrm((L, RG_BLOCKS, RG_BD, RG_BD), RG_BD ** -0.5),
        'rg_bx': nrm((L, GROUP_W), 0.01),
        'rg_lambda': jnp.log(a_rg) - jnp.log1p(-a_rg),
        'hg_lb': nrm((L, GROUP_W), 0.5),
        'hg_norm': gain((L, GROUP_W)),
        'w_out': nrm((L, D, D), D ** -0.5),
        'ln_mem': gain((L, D)),
        'w_xq': nrm((L, D, D), D ** -0.5),
        'w_xkv': nrm((L, D, 2 * D), D ** -0.5),
        'w_xo': nrm((L, D, D), D ** -0.5),
        'w_gu': nrm((L, D, 2 * D_FF), D ** -0.5),
        'w_down': nrm((L, D_FF, D), D_FF ** -0.5),
    }


def reference(x_prompt, x_sample, state_ssd, state_ssd_conv, state_ret, state_rglru, state_rglru_conv,
              state_hgrn, cache_mem_k, cache_mem_v, mem_prompt, ln_mix_pre, ln_mix_post, ln_xa_pre,
              ln_xa_post, ln_ffn_pre, ln_ffn_post, w_in, ssd_conv_w, ssd_conv_b, ssd_dt_bias, ssd_A_log,
              ssd_D, ssd_norm, ret_norm, rg_conv_w, rg_conv_b, rg_wa, rg_ba, rg_wx, rg_bx, rg_lambda,
              hg_lb, hg_norm, w_out, ln_mem, w_xq, w_xkv, w_xo, w_gu, w_down):
    P = dict(ln_mix_pre=ln_mix_pre, ln_mix_post=ln_mix_post, ln_xa_pre=ln_xa_pre, ln_xa_post=ln_xa_post,
             ln_ffn_pre=ln_ffn_pre, ln_ffn_post=ln_ffn_post, w_in=w_in, ssd_conv_w=ssd_conv_w,
             ssd_conv_b=ssd_conv_b, ssd_dt_bias=ssd_dt_bias, ssd_A_log=ssd_A_log, ssd_D=ssd_D,
             ssd_norm=ssd_norm, ret_norm=ret_norm, rg_conv_w=rg_conv_w, rg_conv_b=rg_conv_b,
             rg_wa=rg_wa, rg_ba=rg_ba, rg_wx=rg_wx, rg_bx=rg_bx, rg_lambda=rg_lambda, hg_lb=hg_lb,
             hg_norm=hg_norm, w_out=w_out, ln_mem=ln_mem, w_xq=w_xq, w_xkv=w_xkv, w_xo=w_xo,
             w_gu=w_gu, w_down=w_down)
    B = x_prompt.shape[0]
    dt = x_prompt.dtype
    zero_state = (jnp.zeros((B, SSD_H, SSD_N, SSD_HD), dt), jnp.zeros((B, SSD_CONV - 1, SSD_CONV_CH), dt),
                  jnp.zeros((B, RET_H, RET_HD, RET_HD), dt), jnp.zeros((B, GROUP_W), dt),
                  jnp.zeros((B, RG_CONV - 1, GROUP_W), dt), jnp.zeros((B, HG_H, HG_HD, HG_HD), dt))
    y_p = x_prompt
    y_s = x_sample
    p_st, s_st, p_mk, p_mv = [], [], [], []
    for l in range(DEPTH):
        mk, mv = _mem_kv(mem_prompt, P, l)
        y_p, st_p = _layer(y_p, zero_state, mk, mv, P, l, 0)
        p_st.append(st_p)
        p_mk.append(mk)
        p_mv.append(mv)
        st_l = (state_ssd[l], state_ssd_conv[l], state_ret[l], state_rglru[l], state_rglru_conv[l], state_hgrn[l])
        y_s, st_s = _layer(y_s, st_l, cache_mem_k[l], cache_mem_v[l], P, l, PAST_LEN)
        s_st.append(st_s)
    return (y_p, y_s,
            _stack_field(p_st, 0), _stack_field(p_st, 1), _stack_field(p_st, 2),
            _stack_field(p_st, 3), _stack_field(p_st, 4), _stack_field(p_st, 5),
            jnp.stack(p_mk, axis=0), jnp.stack(p_mv, axis=0),
            _stack_field(s_st, 0), _stack_field(s_st, 1), _stack_field(s_st, 2),
            _stack_field(s_st, 3), _stack_field(s_st, 4), _stack_field(s_st, 5))
```

```python
import functools
import math

import jax
import jax.numpy as jnp
from jax import lax
from jax.experimental import pallas as pl
from jax.experimental.pallas import tpu as pltpu

F32 = jnp.float32
BF16 = jnp.bfloat16
EPS = 1e-6

GROUP_W = 256
HEAD = 64
N_HEADS = 4
SSD_N = 128
SSD_G = 2
CONV_W = 4
XA_H = 4
XA_HD = 256
ROPE_BASE = 10000.0
RG_C = 8.0
CHUNK = 64
HG_CHUNK = 16
PAST_LEN = 16384

ROW_TILE = 512
MIX_TILE = 256
SAMPLE_SEQS = 16
ATT_TILE = 512
ATT_SEQS = 4
FF_TILE = 256
VMEM_LIMIT = 56 * 1024 * 1024


def _bdot(a, b):
    return jnp.dot(a.astype(BF16), b.astype(BF16), preferred_element_type=F32)


def _bdot_nt(a, b):
    return lax.dot_general(a.astype(BF16), b.astype(BF16), (((1,), (1,)), ((), ())),
                           preferred_element_type=F32)


def _bdot_tn(a, b):
    return lax.dot_general(a.astype(BF16), b.astype(BF16), (((0,), (0,)), ((), ())),
                           preferred_element_type=F32)


def _split3(x):
    hi = x.astype(BF16)
    r = x - hi.astype(F32)
    mid = r.astype(BF16)
    lo = (r - mid.astype(F32)).astype(BF16)
    return hi, mid, lo


def _sel_dot(sel, x):
    hi, mid, lo = _split3(x)
    d = lambda y: jnp.dot(sel, y, preferred_element_type=F32)
    return (d(hi) + d(mid)) + d(lo)


def _dot_sel(x, sel):
    hi, mid, lo = _split3(x)
    d = lambda y: jnp.dot(y, sel, preferred_element_type=F32)
    return (d(hi) + d(mid)) + d(lo)


def _sel_dot_nt(sel, x):
    hi, mid, lo = _split3(x)
    d = lambda y: lax.dot_general(sel, y, (((1,), (1,)), ((), ())), preferred_element_type=F32)
    return (d(hi) + d(mid)) + d(lo)


def _rms(x, g):
    return x * lax.rsqrt(jnp.mean(x * x, axis=-1, keepdims=True) + EPS) * g


def _sigmoid(x):
    return jax.nn.sigmoid(x)


def _silu(x):
    return x * jax.nn.sigmoid(x)


def _softplus(x):
    return jnp.maximum(x, 0.0) + jnp.log1p(jnp.exp(-jnp.abs(x)))


def _gelu_tanh(x):
    c = math.sqrt(2.0 / math.pi)
    return 0.5 * x * (1.0 + jnp.tanh(c * (x + 0.044715 * (x * x * x))))


def _iota(shape, dim):
    return lax.broadcasted_iota(jnp.int32, shape, dim)


def _seg_causal_mask(n, seg_shift):
    r = _iota((n, n), 0)
    c = _iota((n, n), 1)
    return ((r >> seg_shift) == (c >> seg_shift)) & (c <= r)


def _head_expand_mat(width_in=128):
    r = _iota((width_in, GROUP_W), 0)
    c = _iota((width_in, GROUP_W), 1)
    return ((c >> 6) == r).astype(BF16)


def _head_stat(y, center):
    parts = []
    for h in range(N_HEADS):
        yh = y[:, h * HEAD:(h + 1) * HEAD]
        if center:
            yh = yh - jnp.mean(yh, axis=-1, keepdims=True)
        yh = yh * lax.rsqrt(jnp.mean(yh * yh, axis=-1, keepdims=True) + EPS)
        parts.append(yh)
    return jnp.concatenate(parts, axis=-1)


def _conv_taps_carry(x, xx_ref, w_ref, b_ref, first):
    rows = x.shape[0]

    @pl.when(first)
    def _():
        xx_ref[0:8, :] = jnp.zeros((8, x.shape[1]), F32)

    xx_ref[8:8 + rows, :] = x
    y = b_ref[...] + w_ref[CONV_W - 1:CONV_W, :] * x
    for j in range(CONV_W - 1):
        y = y + w_ref[j:j + 1, :] * xx_ref[pl.ds(8 - (CONV_W - 1) + j, rows), :]
    xx_ref[0:8, :] = xx_ref[rows:rows + 8, :]
    return y


def _conv_taps_seq4(x, buf4, w_ref, b_ref):
    rows = x.shape[0]
    r = _iota((rows, rows), 0)
    c = _iota((rows, rows), 1)
    t = r & 3
    y = b_ref[...] + w_ref[CONV_W - 1:CONV_W, :] * x
    for d in range(1, CONV_W):
        shift = ((c == r - d) & (t >= d)).astype(BF16)
        hist = ((t < d) & (c == (r - t) + (3 + t - d))).astype(BF16)
        y = y + w_ref[CONV_W - 1 - d:CONV_W - d, :] * (_sel_dot(shift, x) + _sel_dot(hist, buf4))
    new_sel = (((r & 3) < 3) & (c == r + 1)).astype(BF16)
    return y, _sel_dot(new_sel, x)


def _norm_matmul_kernel(x_ref, g_ref, w_ref, *o_refs, widths):
    h = _rms(x_ref[...], g_ref[...]).astype(BF16)
    off = 0
    for o_ref, width in zip(o_refs, widths):
        o_ref[...] = jnp.dot(h, w_ref[:, off:off + width], preferred_element_type=F32)
        off += width


def _const_spec(shape):
    nd = len(shape)
    return pl.BlockSpec(shape, lambda *_: (0,) * nd)


def _params(sem):
    return pltpu.CompilerParams(dimension_semantics=sem, vmem_limit_bytes=VMEM_LIMIT)


def _norm_matmul(x, g, w, widths, name):
    n, d = x.shape
    tm = min(ROW_TILE, n)
    return pl.pallas_call(
        functools.partial(_norm_matmul_kernel, widths=widths),
        grid=(n // tm,),
        in_specs=[pl.BlockSpec((tm, d), lambda i: (i, 0)), _const_spec((1, d)), _const_spec(w.shape)],
        out_specs=[pl.BlockSpec((tm, wd), lambda i: (i, 0)) for wd in widths],
        out_shape=[jax.ShapeDtypeStruct((n, wd), F32) for wd in widths],
        compiler_params=_params(("parallel",)),
        name=name,
    )(x, g, w)


def _out_proj_kernel(ys_ref, yr_ref, yg_ref, yh_ref, x_ref, wo_ref, gpost_ref, gpre_ref, wq_ref,
                     x1_ref, q_ref):
    mix = _bdot(ys_ref[...], wo_ref[0:GROUP_W, :])
    mix = mix + _bdot(yr_ref[...], wo_ref[GROUP_W:2 * GROUP_W, :])
    mix = mix + _bdot(yg_ref[...], wo_ref[2 * GROUP_W:3 * GROUP_W, :])
    mix = mix + _bdot(yh_ref[...], wo_ref[3 * GROUP_W:4 * GROUP_W, :])
    x1 = x_ref[...] + _rms(mix, gpost_ref[...])
    x1_ref[...] = x1
    q_ref[...] = _bdot(_rms(x1, gpre_ref[...]), wq_ref[...])


def _out_proj(ys, x, w_out, g_post, g_pre, w_xq):
    n, d = x.shape
    tm = min(ROW_TILE, n)
    row = lambda wd: pl.BlockSpec((tm, wd), lambda i: (i, 0))
    return pl.pallas_call(
        _out_proj_kernel,
        grid=(n // tm,),
        in_specs=[row(GROUP_W)] * 4 + [row(d), _const_spec(w_out.shape), _const_spec((1, d)),
                                       _const_spec((1, d)), _const_spec(w_xq.shape)],
        out_specs=[row(d), row(d)],
        out_shape=[jax.ShapeDtypeStruct((n, d), F32)] * 2,
        compiler_params=_params(("parallel",)),
        name="out_proj",
    )(*ys, x, w_out, g_post, g_pre, w_xq)


def _ffn_kernel(o_ref, x1_ref, wxo_ref, gxa_ref, gpre_ref, wgu_ref, wdn_ref, gpost_ref, x3_ref, *, d_ff):
    a = _bdot(o_ref[...], wxo_ref[...])
    x2 = x1_ref[...] + _rms(a, gxa_ref[...])
    h = _rms(x2, gpre_ref[...]).astype(BF16)
    acc = jnp.zeros(x2.shape, F32)
    for j in range(0, d_ff, FF_TILE):
        g = jnp.dot(h, wgu_ref[:, j:j + FF_TILE], preferred_element_type=F32)
        u = jnp.dot(h, wgu_ref[:, d_ff + j:d_ff + j + FF_TILE], preferred_element_type=F32)
        act = (_silu(g) * u).astype(BF16)
        acc = acc + jnp.dot(act, wdn_ref[j:j + FF_TILE, :], preferred_element_type=F32)
    x3_ref[...] = x2 + _rms(acc, gpost_ref[...])


def _ffn(o, x1, w_xo, g_xa, g_pre, w_gu, w_down, g_post):
    n, d = x1.shape
    d_ff = w_down.shape[0]
    tm = min(ROW_TILE, n)
    row = pl.BlockSpec((tm, d), lambda i: (i, 0))
    vec = _const_spec((1, d))
    return pl.pallas_call(
        functools.partial(_ffn_kernel, d_ff=d_ff),
        grid=(n // tm,),
        in_specs=[row, row, _const_spec(w_xo.shape), vec, vec, _const_spec(w_gu.shape),
                  _const_spec(w_down.shape), vec],
        out_specs=row,
        out_shape=jax.ShapeDtypeStruct((n, d), F32),
        compiler_params=_params(("parallel",)),
        name="attn_out_ffn",
    )(o, x1, w_xo, g_xa, g_pre, w_gu, w_down, g_post)


def _softmax_pv(s, v):
    m = jnp.max(s, axis=-1, keepdims=True)
    p = jnp.exp(s - m)
    return _bdot(p, v) / jnp.sum(p, axis=-1, keepdims=True)


def _attn_prompt_kernel(q_ref, k_ref, v_ref, o_ref):
    scale = XA_HD ** -0.5
    for h in range(XA_H):
        sl = slice(h * XA_HD, (h + 1) * XA_HD)
        s = _bdot_nt(q_ref[:, sl], k_ref[0, :, sl]) * scale
        o_ref[:, sl] = _softmax_pv(s, v_ref[0, :, sl])


def _attn_sample_kernel(q_ref, k_ref, v_ref, o_ref, *, seqs, seg):
    scale = XA_HD ** -0.5
    rows = seqs * seg
    row_seq = _iota((rows, XA_HD), 0) >> (seg.bit_length() - 1)
    for h in range(XA_H):
        sl = slice(h * XA_HD, (h + 1) * XA_HD)
        q = q_ref[:, sl]
        acc = jnp.zeros((rows, XA_HD), F32)
        for i in range(seqs):
            s = _bdot_nt(q, k_ref[i, :, sl]) * scale
            acc = jnp.where(row_seq == i, _softmax_pv(s, v_ref[i, :, sl]), acc)
        o_ref[:, sl] = acc


def _attention(q, k, v, seq_len):
    n, d = q.shape
    n_seq, n_mem, _ = k.shape
    if seq_len >= ATT_TILE:
        nt = seq_len // ATT_TILE
        kv = pl.BlockSpec((1, n_mem, d), lambda b, t: (b, 0, 0))
        row = pl.BlockSpec((ATT_TILE, d), lambda b, t: (b * nt + t, 0))
        return pl.pallas_call(
            _attn_prompt_kernel,
            grid=(n_seq, nt),
            in_specs=[row, kv, kv],
            out_specs=row,
            out_shape=jax.ShapeDtypeStruct((n, d), F32),
            compiler_params=_params(("parallel", "parallel")),
            name="xattn_prompt",
        )(q, k, v)
    kv = pl.BlockSpec((ATT_SEQS, n_mem, d), lambda i: (i, 0, 0))
    row = pl.BlockSpec((ATT_SEQS * seq_len, d), lambda i: (i, 0))
    return pl.pallas_call(
        functools.partial(_attn_sample_kernel, seqs=ATT_SEQS, seg=seq_len),
        grid=(n_seq // ATT_SEQS,),
        in_specs=[row, kv, kv],
        out_specs=row,
        out_shape=jax.ShapeDtypeStruct((n, d), F32),
        compiler_params=_params(("parallel",)),
        name="xattn_sample",
    )(q, k, v)


def _ssd_intra(q, k, v, cum_x, cum_t, mask):
    sc = _bdot_nt(q, k)
    outs = []
    for hh in range(2):
        col = cum_x[:, hh * HEAD:(hh + 1) * HEAD]
        dec = jnp.exp(jnp.where(mask, col - cum_t[hh:hh + 1, :], -jnp.inf))
        outs.append(_bdot(sc * dec, v[:, hh * HEAD:(hh + 1) * HEAD]))
    return jnp.concatenate(outs, axis=-1)


def _ssd_common(u, dtraw_ref, dtb_ref, alog_ref, conv, mask_cum):
    rows = u.shape[0]
    xbc = _silu(conv)
    xs = xbc[:, 0:GROUP_W]
    bm = xbc[:, GROUP_W:2 * GROUP_W]
    cm = xbc[:, 2 * GROUP_W:3 * GROUP_W]
    dt = _softplus(dtraw_ref[...] + dtb_ref[...])
    la = -jnp.exp(alog_ref[...]) * dt
    cum = _sel_dot(mask_cum, la)
    expand = _head_expand_mat()
    dt_x = _dot_sel(dt, expand)
    cum_x = _dot_sel(cum, expand)
    sel8 = (_iota((8, 128), 0) == _iota((8, 128), 1)).astype(BF16)
    cum_t = _sel_dot_nt(sel8, cum)
    return xs, bm, cm, dt_x, cum_x, cum_t


def _ssd_finish(y, xs, z, d_ref, g_ref):
    y = y + d_ref[...] * xs
    return _rms(y * _silu(z), g_ref[...])


def _ssd_prompt_kernel(u_ref, dtraw_ref, cw_ref, cb_ref, dtb_ref, alog_ref, d_ref, g_ref,
                       y_ref, s_out_ref, buf_out_ref, s_ref, xx_ref, y_scr):
    t = pl.program_id(1)
    rows = u_ref.shape[0]

    @pl.when(t == 0)
    def _():
        s_ref[...] = jnp.zeros(s_ref.shape, F32)

    u = u_ref[...]
    conv = _conv_taps_carry(u[:, GROUP_W:], xx_ref, cw_ref, cb_ref, t == 0)
    mask_cum = _seg_causal_mask(rows, 6).astype(BF16)
    xs, bm, cm, dt_x, cum_x, cum_t = _ssd_common(u, dtraw_ref, dtb_ref, alog_ref, conv, mask_cum)
    v = xs * dt_x
    ecum_x = jnp.exp(cum_x)
    tri = _seg_causal_mask(CHUNK, 6)
    for c in range(rows // CHUNK):
        rs = slice(c * CHUNK, (c + 1) * CHUNK)
        last_x = cum_x[c * CHUNK + CHUNK - 1:(c + 1) * CHUNK, :]
        vend = v[rs] * jnp.exp(last_x - cum_x[rs])
        for g in range(SSD_G):
            ls = slice(g * 128, (g + 1) * 128)
            q = cm[rs, ls]
            k = bm[rs, ls]
            s_prev = s_ref[:, ls]
            y_intra = _ssd_intra(q, k, v[rs, ls], cum_x[rs, ls], cum_t[2 * g:2 * g + 2, rs], tri)
            y_scr[rs, ls] = y_intra + _bdot(q, s_prev) * ecum_x[rs, ls]
            s_ref[:, ls] = jnp.exp(last_x[:, ls]) * s_prev + _bdot_tn(k, vend[:, ls])
    y_ref[...] = _ssd_finish(y_scr[...], xs, u[:, 0:GROUP_W], d_ref, g_ref)

    @pl.when(t == pl.num_programs(1) - 1)
    def _():
        for h in range(N_HEADS):
            s_out_ref[0, h] = s_ref[:, h * HEAD:(h + 1) * HEAD]
        buf_out_ref[0] = xx_ref[pl.ds(8 - (CONV_W - 1), CONV_W - 1), :]


def _ssd_sample_kernel(u_ref, dtraw_ref, s0_ref, buf_ref, cw_ref, cb_ref, dtb_ref, alog_ref, d_ref,
                       g_ref, y_ref, s_out_ref, buf_out_ref, *, seqs):
    rows = u_ref.shape[0]
    u = u_ref[...]
    conv, new_buf = _conv_taps_seq4(u[:, GROUP_W:], buf_ref[...], cw_ref, cb_ref)
    buf_out_ref[...] = new_buf
    mask = _seg_causal_mask(rows, 2)
    xs, bm, cm, dt_x, cum_x, cum_t = _ssd_common(u, dtraw_ref, dtb_ref, alog_ref, conv,
                                                 mask.astype(BF16))
    v = xs * dt_x
    ecum_x = jnp.exp(cum_x)
    pick_last = (_iota((rows, rows), 1) == (_iota((rows, rows), 0) | 3)).astype(BF16)
    last_x = _sel_dot(pick_last, cum_x)
    vend = v * jnp.exp(last_x - cum_x)
    row_seq = _iota((rows, 128), 0) >> 2
    ys = []
    for g in range(SSD_G):
        ls = slice(g * 128, (g + 1) * 128)
        q = cm[:, ls]
        k = bm[:, ls]
        y_g = _ssd_intra(q, k, v[:, ls], cum_x[:, ls], cum_t[2 * g:2 * g + 2, :], mask)
        y_int = jnp.zeros((rows, 128), F32)
        for i in range(seqs):
            in_seq = row_seq == i
            s_prev = jnp.concatenate([s0_ref[i, 2 * g], s0_ref[i, 2 * g + 1]], axis=-1)
            y_int = jnp.where(in_seq, _bdot(q, s_prev), y_int)
            ds = _bdot_tn(k, jnp.where(in_seq, vend[:, ls], 0.0))
            s_new = jnp.exp(last_x[4 * i:4 * i + 1, ls]) * s_prev + ds
            s_out_ref[i, 2 * g] = s_new[:, 0:HEAD]
            s_out_ref[i, 2 * g + 1] = s_new[:, HEAD:2 * HEAD]
        ys.append(y_g + y_int * ecum_x[:, ls])
    y_ref[...] = _ssd_finish(jnp.concatenate(ys, axis=-1), xs, u[:, 0:GROUP_W], d_ref, g_ref)


def _ssd(u, dtraw, state, buf, cw, cb, dtb, alog, d_x, gain, n_seq, seq_len):
    n = u.shape[0]
    cch = cw.shape[1]
    vec = lambda wd: _const_spec((1, wd))
    common_in = [_const_spec(cw.shape), vec(cch), vec(128), vec(128), vec(GROUP_W), vec(GROUP_W)]
    if state is None:
        nt = seq_len // MIX_TILE
        row = lambda wd: pl.BlockSpec((MIX_TILE, wd), lambda b, t: (b * nt + t, 0))
        return pl.pallas_call(
            _ssd_prompt_kernel,
            grid=(n_seq, nt),
            in_specs=[row(u.shape[1]), row(128)] + common_in,
            out_specs=[row(GROUP_W),
                       pl.BlockSpec((1, N_HEADS, SSD_N, HEAD), lambda b, t: (b, 0, 0, 0)),
                       pl.BlockSpec((1, CONV_W - 1, cch), lambda b, t: (b, 0, 0))],
            out_shape=[jax.ShapeDtypeStruct((n, GROUP_W), F32),
                       jax.ShapeDtypeStruct((n_seq, N_HEADS, SSD_N, HEAD), F32),
                       jax.ShapeDtypeStruct((n_seq, CONV_W - 1, cch), F32)],
            scratch_shapes=[pltpu.VMEM((SSD_N, GROUP_W), F32), pltpu.VMEM((MIX_TILE + 8, cch), F32),
                            pltpu.VMEM((MIX_TILE, GROUP_W), F32)],
            compiler_params=_params(("parallel", "arbitrary")),
            name="ssd_prompt",
        )(u, dtraw, cw, cb, dtb, alog, d_x, gain)
    rows = SAMPLE_SEQS * seq_len
    row = lambda wd: pl.BlockSpec((rows, wd), lambda i: (i, 0))
    st = pl.BlockSpec((SAMPLE_SEQS, N_HEADS, SSD_N, HEAD), lambda i: (i, 0, 0, 0))
    return pl.pallas_call(
        functools.partial(_ssd_sample_kernel, seqs=SAMPLE_SEQS),
        grid=(n_seq // SAMPLE_SEQS,),
        in_specs=[row(u.shape[1]), row(128), st, row(cch)] + common_in,
        out_specs=[row(GROUP_W), st, row(cch)],
        out_shape=[jax.ShapeDtypeStruct((n, GROUP_W), F32),
                   jax.ShapeDtypeStruct(state.shape, F32),
                   jax.ShapeDtypeStruct((n, cch), F32)],
        compiler_params=_params(("parallel",)),
        name="ssd_sample",
    )(u, dtraw, state, buf, cw, cb, dtb, alog, d_x, gain)


def _rope(x, cos, sin_signed):
    lane = _iota(x.shape, 1)
    swapped = jnp.where((lane & 63) < 32, pltpu.roll(x, GROUP_W - 32, 1), pltpu.roll(x, 32, 1))
    return x * cos + swapped * sin_signed


def _ret_qkv(u_ref, cos_ref, sin_ref):
    u = u_ref[...]
    q = _rope(u[:, 0:GROUP_W], cos_ref[...], sin_ref[...])
    k = _rope(u[:, GROUP_W:2 * GROUP_W], cos_ref[...], sin_ref[...]) * (HEAD ** -0.5)
    return q, k, u[:, 2 * GROUP_W:3 * GROUP_W], u[:, 3 * GROUP_W:4 * GROUP_W]


def _ret_prompt_kernel(u_ref, cos_ref, sin_ref, ecum_ref, eend_ref, elast_ref, dec_ref, g_ref,
                       y_ref, s_out_ref, s_ref, y_scr):
    t = pl.program_id(1)
    rows = u_ref.shape[0]

    @pl.when(t == 0)
    def _():
        s_ref[...] = jnp.zeros(s_ref.shape, F32)

    q, k, v, gate = _ret_qkv(u_ref, cos_ref, sin_ref)
    for c in range(rows // CHUNK):
        rs = slice(c * CHUNK, (c + 1) * CHUNK)
        vend = v[rs] * eend_ref[...]
        for h in range(N_HEADS):
            ls = slice(h * HEAD, (h + 1) * HEAD)
            qh = q[rs, ls]
            kh = k[rs, ls]
            s_prev = s_ref[:, ls]
            y_intra = _bdot(_bdot_nt(qh, kh) * dec_ref[h], v[rs, ls])
            y_scr[rs, ls] = y_intra + _bdot(qh, s_prev) * ecum_ref[:, ls]
            s_ref[:, ls] = elast_ref[:, ls] * s_prev + _bdot_tn(kh, vend[:, ls])
    y_ref[...] = _silu(gate) * (_head_stat(y_scr[...], True) * g_ref[...])

    @pl.when(t == pl.num_programs(1) - 1)
    def _():
        for h in range(N_HEADS):
            s_out_ref[0, h] = s_ref[:, h * HEAD:(h + 1) * HEAD]


def _ret_sample_kernel(u_ref, cos_ref, sin_ref, s0_ref, ecum_ref, eend_ref, elast_ref, dec_ref, g_ref,
                       y_ref, s_out_ref, *, seqs):
    rows = u_ref.shape[0]
    q, k, v, gate = _ret_qkv(u_ref, cos_ref, sin_ref)
    vend = v * eend_ref[...]
    row_seq = _iota((rows, HEAD), 0) >> 2
    ys = []
    for h in range(N_HEADS):
        ls = slice(h * HEAD, (h + 1) * HEAD)
        qh = q[:, ls]
        kh = k[:, ls]
        y_h = _bdot(_bdot_nt(qh, kh) * dec_ref[h], v[:, ls])
        y_int = jnp.zeros((rows, HEAD), F32)
        for i in range(seqs):
            in_seq = row_seq == i
            s_prev = s0_ref[i, h]
            y_int = jnp.where(in_seq, _bdot(qh, s_prev), y_int)
            ds = _bdot_tn(kh, jnp.where(in_seq, vend[:, ls], 0.0))
            s_out_ref[i, h] = elast_ref[:, ls] * s_prev + ds
        ys.append(y_h + y_int * ecum_ref[:, ls])
    y_ref[...] = _silu(gate) * (_head_stat(jnp.concatenate(ys, axis=-1), True) * g_ref[...])


def _ret_tables(block, seg):
    log_gamma = jnp.log(1.0 - jnp.exp2(-5.0 - jnp.arange(N_HEADS, dtype=F32)))
    pos = (jnp.arange(block) % seg).astype(F32)
    cum = (pos[:, None] + 1.0) * log_gamma[None, :]
    last = seg * log_gamma
    rep = lambda a: jnp.repeat(a, HEAD, axis=-1)
    same = (jnp.arange(block)[:, None] // seg) == (jnp.arange(block)[None, :] // seg)
    mask = same & (jnp.arange(block)[None, :] <= jnp.arange(block)[:, None])
    dec = jnp.exp(jnp.where(mask[None], cum.T[:, :, None] - cum.T[:, None, :], -jnp.inf))
    return (rep(jnp.exp(cum)), rep(jnp.exp(last[None, :] - cum)), rep(jnp.exp(last)[None, :]), dec)


def _ret(u, cos, sin_signed, state, gain, n_seq, seq_len):
    n = u.shape[0]
    vec = _const_spec((1, GROUP_W))
    if state is None:
        nt = seq_len // MIX_TILE
        ecum, eend, elast, dec = _ret_tables(CHUNK, CHUNK)
        row = lambda wd: pl.BlockSpec((MIX_TILE, wd), lambda b, t: (b * nt + t, 0))
        pos = pl.BlockSpec((MIX_TILE, GROUP_W), lambda b, t: (t, 0))
        return pl.pallas_call(
            _ret_prompt_kernel,
            grid=(n_seq, nt),
            in_specs=[row(u.shape[1]), pos, pos, _const_spec(ecum.shape), _const_spec(eend.shape), vec,
                      _const_spec(dec.shape), vec],
            out_specs=[row(GROUP_W), pl.BlockSpec((1, N_HEADS, HEAD, HEAD), lambda b, t: (b, 0, 0, 0))],
            out_shape=[jax.ShapeDtypeStruct((n, GROUP_W), F32),
                       jax.ShapeDtypeStruct((n_seq, N_HEADS, HEAD, HEAD), F32)],
            scratch_shapes=[pltpu.VMEM((HEAD, GROUP_W), F32), pltpu.VMEM((MIX_TILE, GROUP_W), F32)],
            compiler_params=_params(("parallel", "arbitrary")),
            name="ret_prompt",
        )(u, cos, sin_signed, ecum, eend, elast, dec, gain)
    rows = SAMPLE_SEQS * seq_len
    ecum, eend, elast, dec = _ret_tables(rows, seq_len)
    row = lambda wd: pl.BlockSpec((rows, wd), lambda i: (i, 0))
    st = pl.BlockSpec((SAMPLE_SEQS, N_HEADS, HEAD, HEAD), lambda i: (i, 0, 0, 0))
    cos_t = jnp.tile(cos, (SAMPLE_SEQS, 1))
    sin_t = jnp.tile(sin_signed, (SAMPLE_SEQS, 1))
    return pl.pallas_call(
        functools.partial(_ret_sample_kernel, seqs=SAMPLE_SEQS),
        grid=(n_seq // SAMPLE_SEQS,),
        in_specs=[row(u.shape[1]), _const_spec(cos_t.shape), _const_spec(sin_t.shape), st,
                  _const_spec(ecum.shape), _const_spec(eend.shape), vec, _const_spec(dec.shape), vec],
        out_specs=[row(GROUP_W), st],
        out_shape=[jax.ShapeDtypeStruct((n, GROUP_W), F32), jax.ShapeDtypeStruct(state.shape, F32)],
        compiler_params=_params(("parallel",)),
        name="ret_sample",
    )(u, cos_t, sin_t, state, ecum, eend, elast, dec, gain)


def _rg_gates(xr, wa_ref, ba_ref, wx_ref, bx_ref, lam_ref):
    r_gate = _sigmoid(_bdot(xr, wa_ref[...]) + ba_ref[...])
    i_gate = _sigmoid(_bdot(xr, wx_ref[...]) + bx_ref[...])
    log_a = (-RG_C * _softplus(-lam_ref[...])) * r_gate
    a = jnp.exp(log_a)
    b = jnp.sqrt(-jnp.tanh(log_a) * (jnp.exp(2.0 * log_a) + 1.0)) * (i_gate * xr)
    return a, b


def _rg_scan(a, b, seg):
    pos = _iota(a.shape, 0) & (seg - 1)
    d = 1
    while d < seg:
        ok = pos >= d
        a_sh = jnp.where(ok, pltpu.roll(a, d, 0), 1.0)
        b_sh = jnp.where(ok, pltpu.roll(b, d, 0), 0.0)
        b = a * b_sh + b
        a = a * a_sh
        d *= 2
    return a, b


def _rg_prompt_kernel(u_ref, cw_ref, cb_ref, wa_ref, ba_ref, wx_ref, bx_ref, lam_ref,
                      y_ref, h_out_ref, buf_out_ref, h_ref, xx_ref):
    t = pl.program_id(1)
    rows = u_ref.shape[0]

    @pl.when(t == 0)
    def _():
        h_ref[...] = jnp.zeros(h_ref.shape, F32)

    u = u_ref[...]
    xr = _conv_taps_carry(u[:, 0:GROUP_W], xx_ref, cw_ref, cb_ref, t == 0)
    a, b = _rg_gates(xr, wa_ref, ba_ref, wx_ref, bx_ref, lam_ref)
    a_cum, h_loc = _rg_scan(a, b, rows)
    hseq = h_loc + a_cum * h_ref[0:1, :]
    h_ref[...] = jnp.broadcast_to(hseq[rows - 1:rows, :], h_ref.shape)
    y_ref[...] = _gelu_tanh(u[:, GROUP_W:]) * hseq

    @pl.when(t == pl.num_programs(1) - 1)
    def _():
        h_out_ref[0] = hseq[rows - 1:rows, :]
        buf_out_ref[0] = xx_ref[pl.ds(8 - (CONV_W - 1), CONV_W - 1), :]


def _rg_sample_kernel(u_ref, h0_ref, buf_ref, cw_ref, cb_ref, wa_ref, ba_ref, wx_ref, bx_ref, lam_ref,
                      y_ref, h_out_ref, buf_out_ref):
    u = u_ref[...]
    xr, new_buf = _conv_taps_seq4(u[:, 0:GROUP_W], buf_ref[...], cw_ref, cb_ref)
    buf_out_ref[...] = new_buf
    a, b = _rg_gates(xr, wa_ref, ba_ref, wx_ref, bx_ref, lam_ref)
    a_cum, h_loc = _rg_scan(a, b, 4)
    hseq = h_loc + a_cum * h0_ref[...]
    h_out_ref[...] = hseq
    y_ref[...] = _gelu_tanh(u[:, GROUP_W:]) * hseq


def _rg(u, h0_rows, buf, cw, cb, wa, ba, wx, bx, lam, n_seq, seq_len):
    n = u.shape[0]
    vec = _const_spec((1, GROUP_W))
    common_in = [_const_spec(cw.shape), vec, _const_spec(wa.shape), vec, _const_spec(wx.shape), vec, vec]
    if h0_rows is None:
        nt = seq_len // MIX_TILE
        row = lambda wd: pl.BlockSpec((MIX_TILE, wd), lambda b, t: (b * nt + t, 0))
        return pl.pallas_call(
            _rg_prompt_kernel,
            grid=(n_seq, nt),
            in_specs=[row(u.shape[1])] + common_in,
            out_specs=[row(GROUP_W), pl.BlockSpec((1, 1, GROUP_W), lambda b, t: (b, 0, 0)),
                       pl.BlockSpec((1, CONV_W - 1, GROUP_W), lambda b, t: (b, 0, 0))],
            out_shape=[jax.ShapeDtypeStruct((n, GROUP_W), F32),
                       jax.ShapeDtypeStruct((n_seq, 1, GROUP_W), F32),
                       jax.ShapeDtypeStruct((n_seq, CONV_W - 1, GROUP_W), F32)],
            scratch_shapes=[pltpu.VMEM((8, GROUP_W), F32), pltpu.VMEM((MIX_TILE + 8, GROUP_W), F32)],
            compiler_params=_params(("parallel", "arbitrary")),
            name="rglru_prompt",
        )(u, cw, cb, wa, ba, wx, bx, lam)
    rows = SAMPLE_SEQS * seq_len
    row = lambda wd: pl.BlockSpec((rows, wd), lambda i: (i, 0))
    return pl.pallas_call(
        _rg_sample_kernel,
        grid=(n_seq // SAMPLE_SEQS,),
        in_specs=[row(u.shape[1]), row(GROUP_W), row(GROUP_W)] + common_in,
        out_specs=[row(GROUP_W)] * 3,
        out_shape=[jax.ShapeDtypeStruct((n, GROUP_W), F32)] * 3,
        compiler_params=_params(("parallel",)),
        name="rglru_sample",
    )(u, h0_rows, buf, cw, cb, wa, ba, wx, bx, lam)


def _hg_inputs(u_ref, lb_ref, seg_shift):
    u = u_ref[...]
    rows = u.shape[0]
    lb = lb_ref[...]
    fg = lb + (1.0 - lb) * _sigmoid(u[:, GROUP_W:2 * GROUP_W])
    q = _silu(u[:, 0:GROUP_W])
    k = 1.0 - fg
    v = u[:, 2 * GROUP_W:3 * GROUP_W]
    cum = _sel_dot(_seg_causal_mask(rows, seg_shift).astype(BF16), jnp.log(fg))
    return q, k, v, cum, u[:, 3 * GROUP_W:4 * GROUP_W]


def _hg_intra(q, k, v, cum, kk_ref, cc_ref, vv_ref, seg):
    rows = q.shape[0]
    pad = kk_ref.shape[0] - rows
    for ref, val in ((kk_ref, k), (cc_ref, cum), (vv_ref, v)):
        ref[0:pad, :] = jnp.zeros((pad, GROUP_W), F32)
        ref[pad:pad + rows, :] = val
    pos = _iota((rows, GROUP_W), 0) & (seg - 1)
    ones_bd = ((_iota((GROUP_W, GROUP_W), 0) >> 6) == (_iota((GROUP_W, GROUP_W), 1) >> 6)).astype(BF16)
    y = jnp.dot((q * k).astype(BF16), ones_bd, preferred_element_type=F32) * v
    for d in range(1, seg):
        k_d = kk_ref[pl.ds(pad - d, rows), :]
        c_d = cc_ref[pl.ds(pad - d, rows), :]
        v_d = vv_ref[pl.ds(pad - d, rows), :]
        p = jnp.where(pos >= d, q * k_d * jnp.exp(cum - c_d), 0.0)
        y = y + jnp.dot(p.astype(BF16), ones_bd, preferred_element_type=F32) * v_d
    return y


def _hg_prompt_kernel(u_ref, lb_ref, g_ref, y_ref, s_out_ref, st_ref, kk_ref, cc_ref, vv_ref, y_scr):
    t = pl.program_id(1)
    rows = u_ref.shape[0]

    @pl.when(t == 0)
    def _():
        st_ref[...] = jnp.zeros(st_ref.shape, F32)

    q, k, v, cum, gate = _hg_inputs(u_ref, lb_ref, 4)
    y_scr[...] = _hg_intra(q, k, v, cum, kk_ref, cc_ref, vv_ref, HG_CHUNK)
    qn = q * jnp.exp(cum)
    for c in range(rows // HG_CHUNK):
        rs = slice(c * HG_CHUNK, (c + 1) * HG_CHUNK)
        last = cum[(c + 1) * HG_CHUNK - 1:(c + 1) * HG_CHUNK, :]
        kend = k[rs] * jnp.exp(last - cum[rs])
        y_int = []
        ds_t = []
        for h in range(N_HEADS):
            ls = slice(h * HEAD, (h + 1) * HEAD)
            y_int.append(_bdot_nt(qn[rs, ls], st_ref[:, ls]))
            ds_t.append(_bdot_tn(v[rs, ls], kend[:, ls]))
        y_scr[rs, :] = y_scr[rs, :] + jnp.concatenate(y_int, axis=-1)
        st_ref[...] = jnp.exp(last) * st_ref[...] + jnp.concatenate(ds_t, axis=-1)
    y_ref[...] = _silu(gate) * (_head_stat(y_scr[...], False) * g_ref[...])

    @pl.when(t == pl.num_programs(1) - 1)
    def _():
        eye = (_iota((HEAD, HEAD), 0) == _iota((HEAD, HEAD), 1)).astype(BF16)
        for h in range(N_HEADS):
            s_out_ref[0, h] = _sel_dot_nt(eye, st_ref[:, h * HEAD:(h + 1) * HEAD])


def _hg_sample_kernel(u_ref, s0_ref, lb_ref, g_ref, y_ref, s_out_ref, kk_ref, cc_ref, vv_ref, *, seqs):
    rows = u_ref.shape[0]
    q, k, v, cum, gate = _hg_inputs(u_ref, lb_ref, 2)
    y = _hg_intra(q, k, v, cum, kk_ref, cc_ref, vv_ref, 4)
    pick_last = (_iota((rows, rows), 1) == (_iota((rows, rows), 0) | 3)).astype(BF16)
    last = _sel_dot(pick_last, cum)
    qn = q * jnp.exp(cum)
    kend = k * jnp.exp(last - cum)
    elast = jnp.exp(last)
    row_seq = _iota((rows, HEAD), 0) >> 2
    eye = _iota((HEAD, HEAD), 0) == _iota((HEAD, HEAD), 1)
    y_int = []
    for h in range(N_HEADS):
        ls = slice(h * HEAD, (h + 1) * HEAD)
        acc = jnp.zeros((rows, HEAD), F32)
        for i in range(seqs):
            in_seq = row_seq == i
            s_prev = s0_ref[i, h]
            acc = jnp.where(in_seq, _bdot(qn[:, ls], s_prev), acc)
            ds = _bdot_tn(jnp.where(in_seq, kend[:, ls], 0.0), v[:, ls])
            a_col = jnp.sum(jnp.where(eye, elast[4 * i:4 * i + 1, ls], 0.0), axis=-1, keepdims=True)
            s_out_ref[i, h] = a_col * s_prev + ds
        y_int.append(acc)
    y = y + jnp.concatenate(y_int, axis=-1)
    y_ref[...] = _silu(gate) * (_head_stat(y, False) * g_ref[...])


def _hg(u, state, lb, gain, n_seq, seq_len):
    n = u.shape[0]
    vec = _const_spec((1, GROUP_W))
    if state is None:
        nt = seq_len // MIX_TILE
        row = lambda wd: pl.BlockSpec((MIX_TILE, wd), lambda b, t: (b * nt + t, 0))
        shifted = pltpu.VMEM((MIX_TILE + HG_CHUNK, GROUP_W), F32)
        return pl.pallas_call(
            _hg_prompt_kernel,
            grid=(n_seq, nt),
            in_specs=[row(u.shape[1]), vec, vec],
            out_specs=[row(GROUP_W), pl.BlockSpec((1, N_HEADS, HEAD, HEAD), lambda b, t: (b, 0, 0, 0))],
            out_shape=[jax.ShapeDtypeStruct((n, GROUP_W), F32),
                       jax.ShapeDtypeStruct((n_seq, N_HEADS, HEAD, HEAD), F32)],
            scratch_shapes=[pltpu.VMEM((HEAD, GROUP_W), F32), shifted, shifted, shifted,
                            pltpu.VMEM((MIX_TILE, GROUP_W), F32)],
            compiler_params=_params(("parallel", "arbitrary")),
            name="hgrn_prompt",
        )(u, lb, gain)
    rows = SAMPLE_SEQS * seq_len
    row = lambda wd: pl.BlockSpec((rows, wd), lambda i: (i, 0))
    st = pl.BlockSpec((SAMPLE_SEQS, N_HEADS, HEAD, HEAD), lambda i: (i, 0, 0, 0))
    shifted = pltpu.VMEM((rows + 8, GROUP_W), F32)
    return pl.pallas_call(
        functools.partial(_hg_sample_kernel, seqs=SAMPLE_SEQS),
        grid=(n_seq // SAMPLE_SEQS,),
        in_specs=[row(u.shape[1]), st, vec, vec],
        out_specs=[row(GROUP_W), st],
        out_shape=[jax.ShapeDtypeStruct((n, GROUP_W), F32), jax.ShapeDtypeStruct(state.shape, F32)],
        scratch_shapes=[shifted, shifted, shifted],
        compiler_params=_params(("parallel",)),
        name="hgrn_sample",
    )(u, state, lb, gain)


def _block_diag(w):
    h, i, j = w.shape
    eye = jnp.eye(h, dtype=w.dtype)
    return (eye[:, None, :, None] * w[:, :, None, :]).reshape(h * i, h * j)


def _rope_tables(pos0, seq_len):
    half = HEAD // 2
    pos = pos0 + jnp.arange(seq_len, dtype=F32)
    inv = ROPE_BASE ** (-jnp.arange(half, dtype=F32) / half)
    ang = pos[:, None] * inv
    cos = jnp.tile(jnp.cos(ang), (1, 2 * N_HEADS))
    sin = jnp.sin(ang)
    sin_signed = jnp.tile(jnp.concatenate([-sin, sin], axis=-1), (1, N_HEADS))
    return cos, sin_signed


def _layer_weights(P, l):
    w_in = P['w_in'][l]
    z0 = GROUP_W + (GROUP_W + 2 * SSD_G * SSD_N)
    d0 = z0 + N_HEADS
    ret_end = d0 + 4 * GROUP_W
    rg_end = ret_end + 2 * GROUP_W
    dt_cols = jnp.pad(w_in[:, z0:d0], ((0, 0), (0, 128 - N_HEADS)))
    w_in_r = jnp.concatenate([w_in[:, :z0], w_in[:, d0:ret_end], w_in[:, ret_end:rg_end],
                              w_in[:, rg_end:], dt_cols], axis=1).astype(BF16)
    row = lambda a: a.reshape(1, -1)
    pad128 = lambda a: jnp.pad(a, (0, 128 - a.shape[0])).reshape(1, 128)
    lb_sm = jax.nn.softmax(P['hg_lb'].astype(F32), axis=0)
    lb = (jnp.cumsum(lb_sm, axis=0) - lb_sm[0])[l]
    return dict(
        w_in=w_in_r,
        in_widths=(z0, 4 * GROUP_W, 2 * GROUP_W, 4 * GROUP_W, 128),
        ln_mix_pre=row(P['ln_mix_pre'][l]), ln_mix_post=row(P['ln_mix_post'][l]),
        ln_xa_pre=row(P['ln_xa_pre'][l]), ln_xa_post=row(P['ln_xa_post'][l]),
        ln_ffn_pre=row(P['ln_ffn_pre'][l]), ln_ffn_post=row(P['ln_ffn_post'][l]),
        ssd_cw=P['ssd_conv_w'][l], ssd_cb=row(P['ssd_conv_b'][l]),
        ssd_dtb=pad128(P['ssd_dt_bias'][l]), ssd_alog=pad128(P['ssd_A_log'][l]),
        ssd_d=row(jnp.repeat(P['ssd_D'][l], HEAD)), ssd_norm=row(P['ssd_norm'][l]),
        ret_norm=row(P['ret_norm'][l]),
        rg_cw=P['rg_conv_w'][l], rg_cb=row(P['rg_conv_b'][l]),
        rg_wa=_block_diag(P['rg_wa'][l]).astype(BF16), rg_ba=row(P['rg_ba'][l]),
        rg_wx=_block_diag(P['rg_wx'][l]).astype(BF16), rg_bx=row(P['rg_bx'][l]),
        rg_lam=row(P['rg_lambda'][l]),
        hg_lb=row(lb), hg_norm=row(P['hg_norm'][l]),
        w_out=P['w_out'][l].astype(BF16), w_xq=P['w_xq'][l].astype(BF16),
        w_xo=P['w_xo'][l].astype(BF16), w_gu=P['w_gu'][l].astype(BF16),
        w_down=P['w_down'][l].astype(BF16),
        ln_mem=row(P['ln_mem'][l]), w_xkv=P['w_xkv'][l].astype(BF16),
    )


def _layer(x, W, k_mem, v_mem, st, n_seq, seq_len, pos0):
    u_ssd, u_ret, u_rg, u_hg, u_dt = _norm_matmul(x, W['ln_mix_pre'], W['w_in'], W['in_widths'], "in_proj")
    cos, sin_signed = _rope_tables(pos0, seq_len)
    if st is None:
        y_ssd, s_ssd, b_ssd = _ssd(u_ssd, u_dt, None, None, W['ssd_cw'], W['ssd_cb'], W['ssd_dtb'],
                                   W['ssd_alog'], W['ssd_d'], W['ssd_norm'], n_seq, seq_len)
        y_ret, s_ret = _ret(u_ret, cos, sin_signed, None, W['ret_norm'], n_seq, seq_len)
        y_rg, h_rg, b_rg = _rg(u_rg, None, None, W['rg_cw'], W['rg_cb'], W['rg_wa'], W['rg_ba'],
                               W['rg_wx'], W['rg_bx'], W['rg_lam'], n_seq, seq_len)
        h_rg = h_rg.reshape(n_seq, GROUP_W)
        y_hg, s_hg = _hg(u_hg, None, W['hg_lb'], W['hg_norm'], n_seq, seq_len)
    else:
        ssd_s, ssd_buf, ret_s, rg_h, rg_buf, hg_s = st
        pad_rows = lambda b: jnp.pad(b, ((0, 0), (0, 1), (0, 0))).reshape(n_seq * 4, b.shape[-1])
        y_ssd, s_ssd, b_ssd = _ssd(u_ssd, u_dt, ssd_s, pad_rows(ssd_buf), W['ssd_cw'], W['ssd_cb'],
                                   W['ssd_dtb'], W['ssd_alog'], W['ssd_d'], W['ssd_norm'], n_seq, seq_len)
        b_ssd = b_ssd.reshape(n_seq, 4, -1)[:, :CONV_W - 1]
        y_ret, s_ret = _ret(u_ret, cos, sin_signed, ret_s, W['ret_norm'], n_seq, seq_len)
        y_rg, h_rows, b_rg = _rg(u_rg, jnp.repeat(rg_h, seq_len, axis=0), pad_rows(rg_buf), W['rg_cw'],
                                 W['rg_cb'], W['rg_wa'], W['rg_ba'], W['rg_wx'], W['rg_bx'], W['rg_lam'],
                                 n_seq, seq_len)
        h_rg = h_rows.reshape(n_seq, seq_len, GROUP_W)[:, seq_len - 1]
        b_rg = b_rg.reshape(n_seq, 4, -1)[:, :CONV_W - 1]
        y_hg, s_hg = _hg(u_hg, hg_s, W['hg_lb'], W['hg_norm'], n_seq, seq_len)
    x1, q = _out_proj((y_ssd, y_ret, y_rg, y_hg), x, W['w_out'], W['ln_mix_post'], W['ln_xa_pre'], W['w_xq'])
    o = _attention(q, k_mem, v_mem, seq_len)
    x3 = _ffn(o, x1, W['w_xo'], W['ln_xa_post'], W['ln_ffn_pre'], W['w_gu'], W['w_down'], W['ln_ffn_post'])
    return x3, (s_ssd, b_ssd, s_ret, h_rg, b_rg, s_hg)


def kernel(x_prompt, x_sample, state_ssd, state_ssd_conv, state_ret, state_rglru, state_rglru_conv, state_hgrn, cache_mem_k, cache_mem_v, mem_prompt, ln_mix_pre, ln_mix_post, ln_xa_pre, ln_xa_post, ln_ffn_pre, ln_ffn_post, w_in, ssd_conv_w, ssd_conv_b, ssd_dt_bias, ssd_A_log, ssd_D, ssd_norm, ret_norm, rg_conv_w, rg_conv_b, rg_wa, rg_ba, rg_wx, rg_bx, rg_lambda, hg_lb, hg_norm, w_out, ln_mem, w_xq, w_xkv, w_xo, w_gu, w_down):
    P = dict(ln_mix_pre=ln_mix_pre, ln_mix_post=ln_mix_post, ln_xa_pre=ln_xa_pre, ln_xa_post=ln_xa_post,
             ln_ffn_pre=ln_ffn_pre, ln_ffn_post=ln_ffn_post, w_in=w_in, ssd_conv_w=ssd_conv_w,
             ssd_conv_b=ssd_conv_b, ssd_dt_bias=ssd_dt_bias, ssd_A_log=ssd_A_log, ssd_D=ssd_D,
             ssd_norm=ssd_norm, ret_norm=ret_norm, rg_conv_w=rg_conv_w, rg_conv_b=rg_conv_b,
             rg_wa=rg_wa, rg_ba=rg_ba, rg_wx=rg_wx, rg_bx=rg_bx, rg_lambda=rg_lambda, hg_lb=hg_lb,
             hg_norm=hg_norm, w_out=w_out, ln_mem=ln_mem, w_xq=w_xq, w_xkv=w_xkv, w_xo=w_xo,
             w_gu=w_gu, w_down=w_down)
    depth = w_in.shape[0]
    bp, tp, d = x_prompt.shape
    bs, ts, _ = x_sample.shape
    n_mem = mem_prompt.shape[1]
    y_p = x_prompt.reshape(bp * tp, d)
    y_s = x_sample.reshape(bs * ts, d)
    mem = mem_prompt.reshape(bp * n_mem, d)
    p_st, s_st, p_mk, p_mv = [], [], [], []
    for l in range(depth):
        W = _layer_weights(P, l)
        mk, mv = _norm_matmul(mem, W['ln_mem'], W['w_xkv'], (d, d), "mem_kv")
        mk = mk.reshape(bp, n_mem, d)
        mv = mv.reshape(bp, n_mem, d)
        y_p, st_p = _layer(y_p, W, mk, mv, None, bp, tp, 0.0)
        p_st.append(st_p)
        p_mk.append(mk.reshape(bp, n_mem, XA_H, XA_HD))
        p_mv.append(mv.reshape(bp, n_mem, XA_H, XA_HD))
        st_l = (state_ssd[l], state_ssd_conv[l], state_ret[l], state_rglru[l], state_rglru_conv[l],
                state_hgrn[l])
        y_s, st_s = _layer(y_s, W, cache_mem_k[l].reshape(bs, n_mem, d), cache_mem_v[l].reshape(bs, n_mem, d),
                           st_l, bs, ts, float(PAST_LEN))
        s_st.append(st_s)
    stack = lambda sts, i: jnp.stack([s[i] for s in sts], axis=0)
    return (y_p.reshape(bp, tp, d), y_s.reshape(bs, ts, d),
            stack(p_st, 0), stack(p_st, 1), stack(p_st, 2), stack(p_st, 3), stack(p_st, 4), stack(p_st, 5),
            jnp.stack(p_mk, axis=0), jnp.stack(p_mv, axis=0),
            stack(s_st, 0), stack(s_st, 1), stack(s_st, 2), stack(s_st, 3), stack(s_st, 4), stack(s_st, 5))
```

```python
import functools
import math

import jax
import jax.numpy as jnp
from jax import lax
from jax.experimental import pallas as pl
from jax.experimental.pallas import tpu as pltpu

F32 = jnp.float32
BF16 = jnp.bfloat16
EPS = 1e-6

GROUP_W = 256
HEAD = 64
N_HEADS = 4
SSD_N = 128
SSD_G = 2
CONV_W = 4
XA_H = 4
XA_HD = 256
ROPE_BASE = 10000.0
RG_C = 8.0
CHUNK = 64
HG_CHUNK = 16
PAST_LEN = 16384

ROW_TILE = 512
MIX_TILE = 256
SAMPLE_SEQS = 16
ATT_TILE = 512
ATT_SEQS = 4
FF_TILE = 256
VMEM_LIMIT = 56 * 1024 * 1024


def _bdot(a, b):
    return jnp.dot(a.astype(BF16), b.astype(BF16), preferred_element_type=F32)


def _bdot_nt(a, b):
    return lax.dot_general(a.astype(BF16), b.astype(BF16), (((1,), (1,)), ((), ())),
                           preferred_element_type=F32)


def _bdot_tn(a, b):
    return lax.dot_general(a.astype(BF16), b.astype(BF16), (((0,), (0,)), ((), ())),
                           preferred_element_type=F32)


def _split3(x):
    hi = x.astype(BF16)
    r = x - hi.astype(F32)
    mid = r.astype(BF16)
    lo = (r - mid.astype(F32)).astype(BF16)
    return hi, mid, lo


def _sel_dot(sel, x):
    hi, mid, lo = _split3(x)
    d = lambda y: jnp.dot(sel, y, preferred_element_type=F32)
    return (d(hi) + d(mid)) + d(lo)


def _dot_sel(x, sel):
    hi, mid, lo = _split3(x)
    d = lambda y: jnp.dot(y, sel, preferred_element_type=F32)
    return (d(hi) + d(mid)) + d(lo)


def _sel_dot_nt(sel, x):
    hi, mid, lo = _split3(x)
    d = lambda y: lax.dot_general(sel, y, (((1,), (1,)), ((), ())), preferred_element_type=F32)
    return (d(hi) + d(mid)) + d(lo)


def _rms(x, g):
    return x * lax.rsqrt(jnp.mean(x * x, axis=-1, keepdims=True) + EPS) * g


def _sigmoid(x):
    return jax.nn.sigmoid(x)


def _silu(x):
    return x * jax.nn.sigmoid(x)


def _softplus(x):
    return jnp.maximum(x, 0.0) + jnp.log1p(jnp.exp(-jnp.abs(x)))


def _gelu_tanh(x):
    c = math.sqrt(2.0 / math.pi)
    return 0.5 * x * (1.0 + jnp.tanh(c * (x + 0.044715 * (x * x * x))))


def _iota(shape, dim):
    return lax.broadcasted_iota(jnp.int32, shape, dim)


def _seg_causal_mask(n, seg_shift):
    r = _iota((n, n), 0)
    c = _iota((n, n), 1)
    return ((r >> seg_shift) == (c >> seg_shift)) & (c <= r)


def _head_expand_mat(width_in=128):
    r = _iota((width_in, GROUP_W), 0)
    c = _iota((width_in, GROUP_W), 1)
    return ((c >> 6) == r).astype(BF16)


def _head_stat(y, center):
    parts = []
    for h in range(N_HEADS):
        yh = y[:, h * HEAD:(h + 1) * HEAD]
        if center:
            yh = yh - jnp.mean(yh, axis=-1, keepdims=True)
        yh = yh * lax.rsqrt(jnp.mean(yh * yh, axis=-1, keepdims=True) + EPS)
        parts.append(yh)
    return jnp.concatenate(parts, axis=-1)


def _conv_taps_carry(x, xx_ref, w_ref, b_ref, first):
    rows = x.shape[0]

    @pl.when(first)
    def _():
        xx_ref[0:8, :] = jnp.zeros((8, x.shape[1]), F32)

    xx_ref[8:8 + rows, :] = x
    y = b_ref[...] + w_ref[CONV_W - 1:CONV_W, :] * x
    for j in range(CONV_W - 1):
        y = y + w_ref[j:j + 1, :] * xx_ref[pl.ds(8 - (CONV_W - 1) + j, rows), :]
    xx_ref[0:8, :] = xx_ref[rows:rows + 8, :]
    return y


def _conv_taps_seq4(x, buf4, w_ref, b_ref):
    rows = x.shape[0]
    r = _iota((rows, rows), 0)
    c = _iota((rows, rows), 1)
    t = r & 3
    y = b_ref[...] + w_ref[CONV_W - 1:CONV_W, :] * x
    for d in range(1, CONV_W):
        shift = ((c == r - d) & (t >= d)).astype(BF16)
        hist = ((t < d) & (c == (r - t) + (3 + t - d))).astype(BF16)
        y = y + w_ref[CONV_W - 1 - d:CONV_W - d, :] * (_sel_dot(shift, x) + _sel_dot(hist, buf4))
    new_sel = (((r & 3) < 3) & (c == r + 1)).astype(BF16)
    return y, _sel_dot(new_sel, x)


def _norm_matmul_kernel(x_ref, g_ref, w_ref, *o_refs, widths):
    h = _rms(x_ref[...], g_ref[...]).astype(BF16)
    off = 0
    for o_ref, width in zip(o_refs, widths):
        o_ref[...] = jnp.dot(h, w_ref[:, off:off + width], preferred_element_type=F32)
        off += width


def _const_spec(shape):
    nd = len(shape)
    return pl.BlockSpec(shape, lambda *_: (0,) * nd)


def _params(sem):
    return pltpu.CompilerParams(dimension_semantics=sem, vmem_limit_bytes=VMEM_LIMIT)


def _norm_matmul(x, g, w, widths, name):
    n, d = x.shape
    tm = min(ROW_TILE, n)
    return pl.pallas_call(
        functools.partial(_norm_matmul_kernel, widths=widths),
        grid=(n // tm,),
        in_specs=[pl.BlockSpec((tm, d), lambda i: (i, 0)), _const_spec((1, d)), _const_spec(w.shape)],
        out_specs=[pl.BlockSpec((tm, wd), lambda i: (i, 0)) for wd in widths],
        out_shape=[jax.ShapeDtypeStruct((n, wd), F32) for wd in widths],
        compiler_params=_params(("parallel",)),
        name=name,
    )(x, g, w)


def _out_proj_kernel(ys_ref, yr_ref, yg_ref, yh_ref, x_ref, wo_ref, gpost_ref, gpre_ref, wq_ref,
                     x1_ref, q_ref):
    mix = _bdot(ys_ref[...], wo_ref[0:GROUP_W, :])
    mix = mix + _bdot(yr_ref[...], wo_ref[GROUP_W:2 * GROUP_W, :])
    mix = mix + _bdot(yg_ref[...], wo_ref[2 * GROUP_W:3 * GROUP_W, :])
    mix = mix + _bdot(yh_ref[...], wo_ref[3 * GROUP_W:4 * GROUP_W, :])
    x1 = x_ref[...] + _rms(mix, gpost_ref[...])
    x1_ref[...] = x1
    q_ref[...] = _bdot(_rms(x1, gpre_ref[...]), wq_ref[...])


def _out_proj(ys, x, w_out, g_post, g_pre, w_xq):
    n, d = x.shape
    tm = min(ROW_TILE, n)
    row = lambda wd: pl.BlockSpec((tm, wd), lambda i: (i, 0))
    return pl.pallas_call(
        _out_proj_kernel,
        grid=(n // tm,),
        in_specs=[row(GROUP_W)] * 4 + [row(d), _const_spec(w_out.shape), _const_spec((1, d)),
                                       _const_spec((1, d)), _const_spec(w_xq.shape)],
        out_specs=[row(d), row(d)],
        out_shape=[jax.ShapeDtypeStruct((n, d), F32)] * 2,
        compiler_params=_params(("parallel",)),
        name="out_proj",
    )(*ys, x, w_out, g_post, g_pre, w_xq)


def _ffn_kernel(o_ref, x1_ref, wxo_ref, gxa_ref, gpre_ref, wgu_ref, wdn_ref, gpost_ref, x3_ref, *, d_ff):
    a = _bdot(o_ref[...], wxo_ref[...])
    x2 = x1_ref[...] + _rms(a, gxa_ref[...])
    h = _rms(x2, gpre_ref[...]).astype(BF16)
    acc = jnp.zeros(x2.shape, F32)
    for j in range(0, d_ff, FF_TILE):
        g = jnp.dot(h, wgu_ref[:, j:j + FF_TILE], preferred_element_type=F32)
        u = jnp.dot(h, wgu_ref[:, d_ff + j:d_ff + j + FF_TILE], preferred_element_type=F32)
        act = (_silu(g) * u).astype(BF16)
        acc = acc + jnp.dot(act, wdn_ref[j:j + FF_TILE, :], preferred_element_type=F32)
    x3_ref[...] = x2 + _rms(acc, gpost_ref[...])


def _ffn(o, x1, w_xo, g_xa, g_pre, w_gu, w_down, g_post):
    n, d = x1.shape
    d_ff = w_down.shape[0]
    tm = min(ROW_TILE, n)
    row = pl.BlockSpec((tm, d), lambda i: (i, 0))
    vec = _const_spec((1, d))
    return pl.pallas_call(
        functools.partial(_ffn_kernel, d_ff=d_ff),
        grid=(n // tm,),
        in_specs=[row, row, _const_spec(w_xo.shape), vec, vec, _const_spec(w_gu.shape),
                  _const_spec(w_down.shape), vec],
        out_specs=row,
        out_shape=jax.ShapeDtypeStruct((n, d), F32),
        compiler_params=_params(("parallel",)),
        name="attn_out_ffn",
    )(o, x1, w_xo, g_xa, g_pre, w_gu, w_down, g_post)


def _softmax_pv(s, v):
    m = jnp.max(s, axis=-1, keepdims=True)
    p = jnp.exp(s - m)
    return _bdot(p, v) / jnp.sum(p, axis=-1, keepdims=True)


def _attn_prompt_kernel(q_ref, k_ref, v_ref, o_ref):
    scale = XA_HD ** -0.5
    for h in range(XA_H):
        sl = slice(h * XA_HD, (h + 1) * XA_HD)
        s = _bdot_nt(q_ref[:, sl], k_ref[0, :, sl]) * scale
        o_ref[:, sl] = _softmax_pv(s, v_ref[0, :, sl])


def _attn_sample_kernel(q_ref, k_ref, v_ref, o_ref, *, seqs, seg):
    scale = XA_HD ** -0.5
    rows = seqs * seg
    n_mem = k_ref.shape[2]
    wq = jnp.concatenate([q_ref[:, h * XA_HD:(h + 1) * XA_HD] for h in range(XA_H)], axis=0)
    wshape = (XA_H * rows, n_mem * XA_H)
    valid = (_iota(wshape, 0) >> (rows.bit_length() - 1)) == (_iota(wshape, 1) & (XA_H - 1))
    row_seq = _iota((rows, XA_HD), 0) >> (seg.bit_length() - 1)
    accs = [jnp.zeros((rows, XA_HD), F32) for _ in range(XA_H)]
    for i in range(seqs):
        k2 = k_ref[0, i].reshape(n_mem * XA_H, XA_HD)
        v2 = v_ref[0, i].reshape(n_mem * XA_H, XA_HD)
        s = jnp.where(valid, _bdot_nt(wq, k2) * scale, -jnp.inf)
        o = _softmax_pv(s, v2)
        for h in range(XA_H):
            accs[h] = jnp.where(row_seq == i, o[h * rows:(h + 1) * rows], accs[h])
    for h in range(XA_H):
        o_ref[:, h * XA_HD:(h + 1) * XA_HD] = accs[h]


def _attention_prompt(q, k, v, seq_len):
    n, d = q.shape
    n_seq, n_mem, _ = k.shape
    nt = seq_len // ATT_TILE
    kv = pl.BlockSpec((1, n_mem, d), lambda b, t: (b, 0, 0))
    row = pl.BlockSpec((ATT_TILE, d), lambda b, t: (b * nt + t, 0))
    return pl.pallas_call(
        _attn_prompt_kernel,
        grid=(n_seq, nt),
        in_specs=[row, kv, kv],
        out_specs=row,
        out_shape=jax.ShapeDtypeStruct((n, d), F32),
        compiler_params=_params(("parallel", "parallel")),
        name="xattn_prompt",
    )(q, k, v)


def _attention_sample(q, cache_k, cache_v, layer, seq_len):
    n, d = q.shape
    _, n_seq, n_mem, heads, hd = cache_k.shape
    kv = pl.BlockSpec((1, ATT_SEQS, n_mem, heads, hd), lambda i: (layer, i, 0, 0, 0))
    row = pl.BlockSpec((ATT_SEQS * seq_len, d), lambda i: (i, 0))
    return pl.pallas_call(
        functools.partial(_attn_sample_kernel, seqs=ATT_SEQS, seg=seq_len),
        grid=(n_seq // ATT_SEQS,),
        in_specs=[row, kv, kv],
        out_specs=row,
        out_shape=jax.ShapeDtypeStruct((n, d), F32),
        compiler_params=_params(("parallel",)),
        name="xattn_sample",
    )(q, cache_k, cache_v)


def _ssd_intra(q, k, v, cum_x, cum_t, mask):
    sc = _bdot_nt(q, k)
    outs = []
    for hh in range(2):
        col = cum_x[:, hh * HEAD:(hh + 1) * HEAD]
        dec = jnp.exp(jnp.where(mask, col - cum_t[hh:hh + 1, :], -jnp.inf))
        outs.append(_bdot(sc * dec, v[:, hh * HEAD:(hh + 1) * HEAD]))
    return jnp.concatenate(outs, axis=-1)


def _ssd_common(u, dtraw_ref, dtb_ref, alog_ref, conv, mask_cum):
    rows = u.shape[0]
    xbc = _silu(conv)
    xs = xbc[:, 0:GROUP_W]
    bm = xbc[:, GROUP_W:2 * GROUP_W]
    cm = xbc[:, 2 * GROUP_W:3 * GROUP_W]
    dt = _softplus(dtraw_ref[...] + dtb_ref[...])
    la = -jnp.exp(alog_ref[...]) * dt
    cum = _sel_dot(mask_cum, la)
    expand = _head_expand_mat()
    dt_x = _dot_sel(dt, expand)
    cum_x = _dot_sel(cum, expand)
    sel8 = (_iota((8, 128), 0) == _iota((8, 128), 1)).astype(BF16)
    cum_t = _sel_dot_nt(sel8, cum)
    return xs, bm, cm, dt_x, cum_x, cum_t


def _ssd_finish(y, xs, z, d_ref, g_ref):
    y = y + d_ref[...] * xs
    return _rms(y * _silu(z), g_ref[...])


def _ssd_prompt_kernel(u_ref, dtraw_ref, cw_ref, cb_ref, dtb_ref, alog_ref, d_ref, g_ref,
                       y_ref, s_out_ref, buf_out_ref, s_ref, xx_ref, y_scr):
    t = pl.program_id(1)
    rows = u_ref.shape[0]

    @pl.when(t == 0)
    def _():
        s_ref[...] = jnp.zeros(s_ref.shape, F32)

    u = u_ref[...]
    conv = _conv_taps_carry(u[:, GROUP_W:], xx_ref, cw_ref, cb_ref, t == 0)
    mask_cum = _seg_causal_mask(rows, 6).astype(BF16)
    xs, bm, cm, dt_x, cum_x, cum_t = _ssd_common(u, dtraw_ref, dtb_ref, alog_ref, conv, mask_cum)
    v = xs * dt_x
    ecum_x = jnp.exp(cum_x)
    hmask = _head_block_mask()
    gmask = (_iota((GROUP_W, GROUP_W), 0) >> 7) == (_iota((GROUP_W, GROUP_W), 1) >> 7)
    wrow = _iota((CHUNK, GROUP_W), 0)
    wcol = _iota((CHUNK, GROUP_W), 1) & (CHUNK - 1)
    for c in range(rows // CHUNK):
        rs = slice(c * CHUNK, (c + 1) * CHUNK)
        cum_c = cum_x[rs]
        last_x = cum_x[c * CHUNK + CHUNK - 1:(c + 1) * CHUNK, :]
        vend = v[rs] * jnp.exp(last_x - cum_c)
        cum_row = jnp.sum(jnp.where(wrow == wcol, cum_c, 0.0), axis=0, keepdims=True)
        dec = jnp.exp(jnp.where(wcol <= wrow, cum_c - cum_row, -jnp.inf))
        b_stack = jnp.where(gmask, jnp.concatenate([bm[rs]] * N_HEADS, axis=0), 0.0)
        y_intra = _bdot(_bdot_nt(cm[rs], b_stack) * dec, _block_diag_rows(v[rs], hmask))
        y_inter = []
        for g in range(SSD_G):
            ls = slice(g * 128, (g + 1) * 128)
            s_prev = s_ref[:, ls]
            y_inter.append(_bdot(cm[rs, ls], s_prev))
            s_ref[:, ls] = jnp.exp(last_x[:, ls]) * s_prev + _bdot_tn(bm[rs, ls], vend[:, ls])
        y_scr[rs, :] = y_intra + jnp.concatenate(y_inter, axis=-1) * ecum_x[rs]
    y_ref[...] = _ssd_finish(y_scr[...], xs, u[:, 0:GROUP_W], d_ref, g_ref)

    @pl.when(t == pl.num_programs(1) - 1)
    def _():
        for h in range(N_HEADS):
            s_out_ref[0, h] = s_ref[:, h * HEAD:(h + 1) * HEAD]
        buf_out_ref[0] = xx_ref[pl.ds(8 - (CONV_W - 1), CONV_W - 1), :]


def _ssd_sample_kernel(u_ref, dtraw_ref, s0_ref, buf_ref, cw_ref, cb_ref, dtb_ref, alog_ref, d_ref,
                       g_ref, y_ref, s_out_ref, buf_out_ref, *, seqs):
    rows = u_ref.shape[0]
    u = u_ref[...]
    conv, new_buf = _conv_taps_seq4(u[:, GROUP_W:], buf_ref[...], cw_ref, cb_ref)
    buf_out_ref[...] = new_buf
    mask = _seg_causal_mask(rows, 2)
    xs, bm, cm, dt_x, cum_x, cum_t = _ssd_common(u, dtraw_ref, dtb_ref, alog_ref, conv,
                                                 mask.astype(BF16))
    v = xs * dt_x
    ecum_x = jnp.exp(cum_x)
    pick_last = (_iota((rows, rows), 1) == (_iota((rows, rows), 0) | 3)).astype(BF16)
    last_x = _sel_dot(pick_last, cum_x)
    vend = v * jnp.exp(last_x - cum_x)
    row_seq = _iota((rows, 128), 0) >> 2
    ys = []
    for g in range(SSD_G):
        ls = slice(g * 128, (g + 1) * 128)
        q = cm[:, ls]
        k = bm[:, ls]
        y_g = _ssd_intra(q, k, v[:, ls], cum_x[:, ls], cum_t[2 * g:2 * g + 2, :], mask)
        y_int = jnp.zeros((rows, 128), F32)
        for i in range(seqs):
            in_seq = row_seq == i
            s_prev = jnp.concatenate([s0_ref[i, 2 * g], s0_ref[i, 2 * g + 1]], axis=-1)
            y_int = jnp.where(in_seq, _bdot(q, s_prev), y_int)
            ds = _bdot_tn(k, jnp.where(in_seq, vend[:, ls], 0.0))
            s_new = jnp.exp(last_x[4 * i:4 * i + 1, ls]) * s_prev + ds
            s_out_ref[i, 2 * g] = s_new[:, 0:HEAD]
            s_out_ref[i, 2 * g + 1] = s_new[:, HEAD:2 * HEAD]
        ys.append(y_g + y_int * ecum_x[:, ls])
    y_ref[...] = _ssd_finish(jnp.concatenate(ys, axis=-1), xs, u[:, 0:GROUP_W], d_ref, g_ref)


def _ssd(u, dtraw, state, buf, cw, cb, dtb, alog, d_x, gain, n_seq, seq_len):
    n = u.shape[0]
    cch = cw.shape[1]
    vec = lambda wd: _const_spec((1, wd))
    common_in = [_const_spec(cw.shape), vec(cch), vec(128), vec(128), vec(GROUP_W), vec(GROUP_W)]
    if state is None:
        nt = seq_len // MIX_TILE
        row = lambda wd: pl.BlockSpec((MIX_TILE, wd), lambda b, t: (b * nt + t, 0))
        return pl.pallas_call(
            _ssd_prompt_kernel,
            grid=(n_seq, nt),
            in_specs=[row(u.shape[1]), row(128)] + common_in,
            out_specs=[row(GROUP_W),
                       pl.BlockSpec((1, N_HEADS, SSD_N, HEAD), lambda b, t: (b, 0, 0, 0)),
                       pl.BlockSpec((1, CONV_W - 1, cch), lambda b, t: (b, 0, 0))],
            out_shape=[jax.ShapeDtypeStruct((n, GROUP_W), F32),
                       jax.ShapeDtypeStruct((n_seq, N_HEADS, SSD_N, HEAD), F32),
                       jax.ShapeDtypeStruct((n_seq, CONV_W - 1, cch), F32)],
            scratch_shapes=[pltpu.VMEM((SSD_N, GROUP_W), F32), pltpu.VMEM((MIX_TILE + 8, cch), F32),
                            pltpu.VMEM((MIX_TILE, GROUP_W), F32)],
            compiler_params=_params(("parallel", "arbitrary")),
            name="ssd_prompt",
        )(u, dtraw, cw, cb, dtb, alog, d_x, gain)
    rows = SAMPLE_SEQS * seq_len
    row = lambda wd: pl.BlockSpec((rows, wd), lambda i: (i, 0))
    st = pl.BlockSpec((SAMPLE_SEQS, N_HEADS, SSD_N, HEAD), lambda i: (i, 0, 0, 0))
    return pl.pallas_call(
        functools.partial(_ssd_sample_kernel, seqs=SAMPLE_SEQS),
        grid=(n_seq // SAMPLE_SEQS,),
        in_specs=[row(u.shape[1]), row(128), st, row(cch)] + common_in,
        out_specs=[row(GROUP_W), st, row(cch)],
        out_shape=[jax.ShapeDtypeStruct((n, GROUP_W), F32),
                   jax.ShapeDtypeStruct(state.shape, F32),
                   jax.ShapeDtypeStruct((n, cch), F32)],
        compiler_params=_params(("parallel",)),
        name="ssd_sample",
    )(u, dtraw, state, buf, cw, cb, dtb, alog, d_x, gain)


def _rope(x, cos, sin_signed):
    lane = _iota(x.shape, 1)
    swapped = jnp.where((lane & 63) < 32, pltpu.roll(x, GROUP_W - 32, 1), pltpu.roll(x, 32, 1))
    return x * cos + swapped * sin_signed


def _ret_qkv(u_ref, cos_ref, sin_ref):
    u = u_ref[...]
    q = _rope(u[:, 0:GROUP_W], cos_ref[...], sin_ref[...])
    k = _rope(u[:, GROUP_W:2 * GROUP_W], cos_ref[...], sin_ref[...]) * (HEAD ** -0.5)
    return q, k, u[:, 2 * GROUP_W:3 * GROUP_W], u[:, 3 * GROUP_W:4 * GROUP_W]


def _ret_prompt_kernel(u_ref, cos_ref, sin_ref, ecum_ref, eend_ref, elast_ref, dec_ref, g_ref,
                       y_ref, s_out_ref, s_ref, y_scr):
    t = pl.program_id(1)
    rows = u_ref.shape[0]

    @pl.when(t == 0)
    def _():
        s_ref[...] = jnp.zeros(s_ref.shape, F32)

    q, k, v, gate = _ret_qkv(u_ref, cos_ref, sin_ref)
    hmask = _head_block_mask()
    for c in range(rows // CHUNK):
        rs = slice(c * CHUNK, (c + 1) * CHUNK)
        a_wide = _bdot_nt(q[rs], _block_diag_rows(k[rs], hmask)) * dec_ref[...]
        y_intra = _bdot(a_wide, _block_diag_rows(v[rs], hmask))
        y_scr[rs, :] = y_intra + _bdot(q[rs], s_ref[...]) * ecum_ref[...]
        ds = jnp.where(hmask, _bdot_tn(k[rs], v[rs] * eend_ref[...]), 0.0)
        s_ref[...] = elast_ref[...] * s_ref[...] + ds
    y_ref[...] = _silu(gate) * (_head_stat(y_scr[...], True) * g_ref[...])

    @pl.when(t == pl.num_programs(1) - 1)
    def _():
        for h in range(N_HEADS):
            s_out_ref[0, h] = s_ref[h * HEAD:(h + 1) * HEAD, h * HEAD:(h + 1) * HEAD]


def _ret_sample_kernel(u_ref, cos_ref, sin_ref, s0_ref, ecum_ref, eend_ref, elast_ref, dec_ref, g_ref,
                       y_ref, s_out_ref, *, seqs):
    rows = u_ref.shape[0]
    q, k, v, gate = _ret_qkv(u_ref, cos_ref, sin_ref)
    vend = v * eend_ref[...]
    row_seq = _iota((rows, HEAD), 0) >> 2
    ys = []
    for h in range(N_HEADS):
        ls = slice(h * HEAD, (h + 1) * HEAD)
        qh = q[:, ls]
        kh = k[:, ls]
        y_h = _bdot(_bdot_nt(qh, kh) * dec_ref[h], v[:, ls])
        y_int = jnp.zeros((rows, HEAD), F32)
        for i in range(seqs):
            in_seq = row_seq == i
            s_prev = s0_ref[i, h]
            y_int = jnp.where(in_seq, _bdot(qh, s_prev), y_int)
            ds = _bdot_tn(kh, jnp.where(in_seq, vend[:, ls], 0.0))
            s_out_ref[i, h] = elast_ref[:, ls] * s_prev + ds
        ys.append(y_h + y_int * ecum_ref[:, ls])
    y_ref[...] = _silu(gate) * (_head_stat(jnp.concatenate(ys, axis=-1), True) * g_ref[...])


def _ret_tables(block, seg):
    log_gamma = jnp.log(1.0 - jnp.exp2(-5.0 - jnp.arange(N_HEADS, dtype=F32)))
    pos = (jnp.arange(block) % seg).astype(F32)
    cum = (pos[:, None] + 1.0) * log_gamma[None, :]
    last = seg * log_gamma
    rep = lambda a: jnp.repeat(a, HEAD, axis=-1)
    same = (jnp.arange(block)[:, None] // seg) == (jnp.arange(block)[None, :] // seg)
    mask = same & (jnp.arange(block)[None, :] <= jnp.arange(block)[:, None])
    dec = jnp.exp(jnp.where(mask[None], cum.T[:, :, None] - cum.T[:, None, :], -jnp.inf))
    return (rep(jnp.exp(cum)), rep(jnp.exp(last[None, :] - cum)), rep(jnp.exp(last)[None, :]), dec)


def _ret(u, cos, sin_signed, state, gain, n_seq, seq_len):
    n = u.shape[0]
    vec = _const_spec((1, GROUP_W))
    if state is None:
        nt = seq_len // MIX_TILE
        ecum, eend, elast, dec = _ret_tables(CHUNK, CHUNK)
        dec = dec.transpose(1, 0, 2).reshape(CHUNK, N_HEADS * CHUNK)
        row = lambda wd: pl.BlockSpec((MIX_TILE, wd), lambda b, t: (b * nt + t, 0))
        pos = pl.BlockSpec((MIX_TILE, GROUP_W), lambda b, t: (t, 0))
        return pl.pallas_call(
            _ret_prompt_kernel,
            grid=(n_seq, nt),
            in_specs=[row(u.shape[1]), pos, pos, _const_spec(ecum.shape), _const_spec(eend.shape), vec,
                      _const_spec(dec.shape), vec],
            out_specs=[row(GROUP_W), pl.BlockSpec((1, N_HEADS, HEAD, HEAD), lambda b, t: (b, 0, 0, 0))],
            out_shape=[jax.ShapeDtypeStruct((n, GROUP_W), F32),
                       jax.ShapeDtypeStruct((n_seq, N_HEADS, HEAD, HEAD), F32)],
            scratch_shapes=[pltpu.VMEM((GROUP_W, GROUP_W), F32), pltpu.VMEM((MIX_TILE, GROUP_W), F32)],
            compiler_params=_params(("parallel", "arbitrary")),
            name="ret_prompt",
        )(u, cos, sin_signed, ecum, eend, elast, dec, gain)
    rows = SAMPLE_SEQS * seq_len
    ecum, eend, elast, dec = _ret_tables(rows, seq_len)
    row = lambda wd: pl.BlockSpec((rows, wd), lambda i: (i, 0))
    st = pl.BlockSpec((SAMPLE_SEQS, N_HEADS, HEAD, HEAD), lambda i: (i, 0, 0, 0))
    cos_t = jnp.tile(cos, (SAMPLE_SEQS, 1))
    sin_t = jnp.tile(sin_signed, (SAMPLE_SEQS, 1))
    return pl.pallas_call(
        functools.partial(_ret_sample_kernel, seqs=SAMPLE_SEQS),
        grid=(n_seq // SAMPLE_SEQS,),
        in_specs=[row(u.shape[1]), _const_spec(cos_t.shape), _const_spec(sin_t.shape), st,
                  _const_spec(ecum.shape), _const_spec(eend.shape), vec, _const_spec(dec.shape), vec],
        out_specs=[row(GROUP_W), st],
        out_shape=[jax.ShapeDtypeStruct((n, GROUP_W), F32), jax.ShapeDtypeStruct(state.shape, F32)],
        compiler_params=_params(("parallel",)),
        name="ret_sample",
    )(u, cos_t, sin_t, state, ecum, eend, elast, dec, gain)


def _rg_gates(xr, wa_ref, ba_ref, wx_ref, bx_ref, lam_ref):
    r_gate = _sigmoid(_bdot(xr, wa_ref[...]) + ba_ref[...])
    i_gate = _sigmoid(_bdot(xr, wx_ref[...]) + bx_ref[...])
    log_a = (-RG_C * _softplus(-lam_ref[...])) * r_gate
    a = jnp.exp(log_a)
    b = jnp.sqrt(-jnp.tanh(log_a) * (jnp.exp(2.0 * log_a) + 1.0)) * (i_gate * xr)
    return a, b


def _rg_scan(a, b, seg):
    pos = _iota(a.shape, 0) & (seg - 1)
    d = 1
    while d < seg:
        ok = pos >= d
        a_sh = jnp.where(ok, pltpu.roll(a, d, 0), 1.0)
        b_sh = jnp.where(ok, pltpu.roll(b, d, 0), 0.0)
        b = a * b_sh + b
        a = a * a_sh
        d *= 2
    return a, b


def _rg_prompt_kernel(u_ref, cw_ref, cb_ref, wa_ref, ba_ref, wx_ref, bx_ref, lam_ref,
                      y_ref, h_out_ref, buf_out_ref, h_ref, xx_ref):
    t = pl.program_id(1)
    rows = u_ref.shape[0]

    @pl.when(t == 0)
    def _():
        h_ref[...] = jnp.zeros(h_ref.shape, F32)

    u = u_ref[...]
    xr = _conv_taps_carry(u[:, 0:GROUP_W], xx_ref, cw_ref, cb_ref, t == 0)
    a, b = _rg_gates(xr, wa_ref, ba_ref, wx_ref, bx_ref, lam_ref)
    a_cum, h_loc = _rg_scan(a, b, rows)
    hseq = h_loc + a_cum * h_ref[0:1, :]
    h_ref[...] = jnp.broadcast_to(hseq[rows - 1:rows, :], h_ref.shape)
    y_ref[...] = _gelu_tanh(u[:, GROUP_W:]) * hseq

    @pl.when(t == pl.num_programs(1) - 1)
    def _():
        h_out_ref[0] = hseq[rows - 1:rows, :]
        buf_out_ref[0] = xx_ref[pl.ds(8 - (CONV_W - 1), CONV_W - 1), :]


def _rg_sample_kernel(u_ref, h0_ref, buf_ref, cw_ref, cb_ref, wa_ref, ba_ref, wx_ref, bx_ref, lam_ref,
                      y_ref, h_out_ref, buf_out_ref):
    u = u_ref[...]
    xr, new_buf = _conv_taps_seq4(u[:, 0:GROUP_W], buf_ref[...], cw_ref, cb_ref)
    buf_out_ref[...] = new_buf
    a, b = _rg_gates(xr, wa_ref, ba_ref, wx_ref, bx_ref, lam_ref)
    a_cum, h_loc = _rg_scan(a, b, 4)
    hseq = h_loc + a_cum * h0_ref[...]
    h_out_ref[...] = hseq
    y_ref[...] = _gelu_tanh(u[:, GROUP_W:]) * hseq


def _rg(u, h0_rows, buf, cw, cb, wa, ba, wx, bx, lam, n_seq, seq_len):
    n = u.shape[0]
    vec = _const_spec((1, GROUP_W))
    common_in = [_const_spec(cw.shape), vec, _const_spec(wa.shape), vec, _const_spec(wx.shape), vec, vec]
    if h0_rows is None:
        nt = seq_len // MIX_TILE
        row = lambda wd: pl.BlockSpec((MIX_TILE, wd), lambda b, t: (b * nt + t, 0))
        return pl.pallas_call(
            _rg_prompt_kernel,
            grid=(n_seq, nt),
            in_specs=[row(u.shape[1])] + common_in,
            out_specs=[row(GROUP_W), pl.BlockSpec((1, 1, GROUP_W), lambda b, t: (b, 0, 0)),
                       pl.BlockSpec((1, CONV_W - 1, GROUP_W), lambda b, t: (b, 0, 0))],
            out_shape=[jax.ShapeDtypeStruct((n, GROUP_W), F32),
                       jax.ShapeDtypeStruct((n_seq, 1, GROUP_W), F32),
                       jax.ShapeDtypeStruct((n_seq, CONV_W - 1, GROUP_W), F32)],
            scratch_shapes=[pltpu.VMEM((8, GROUP_W), F32), pltpu.VMEM((MIX_TILE + 8, GROUP_W), F32)],
            compiler_params=_params(("parallel", "arbitrary")),
            name="rglru_prompt",
        )(u, cw, cb, wa, ba, wx, bx, lam)
    rows = SAMPLE_SEQS * seq_len
    row = lambda wd: pl.BlockSpec((rows, wd), lambda i: (i, 0))
    return pl.pallas_call(
        _rg_sample_kernel,
        grid=(n_seq // SAMPLE_SEQS,),
        in_specs=[row(u.shape[1]), row(GROUP_W), row(GROUP_W)] + common_in,
        out_specs=[row(GROUP_W)] * 3,
        out_shape=[jax.ShapeDtypeStruct((n, GROUP_W), F32)] * 3,
        compiler_params=_params(("parallel",)),
        name="rglru_sample",
    )(u, h0_rows, buf, cw, cb, wa, ba, wx, bx, lam)


def _hg_inputs(u_ref, lb_ref, seg_shift):
    u = u_ref[...]
    rows = u.shape[0]
    lb = lb_ref[...]
    fg = lb + (1.0 - lb) * _sigmoid(u[:, GROUP_W:2 * GROUP_W])
    q = _silu(u[:, 0:GROUP_W])
    k = 1.0 - fg
    v = u[:, 2 * GROUP_W:3 * GROUP_W]
    cum = _sel_dot(_seg_causal_mask(rows, seg_shift).astype(BF16), jnp.log(fg))
    return q, k, v, cum, u[:, 3 * GROUP_W:4 * GROUP_W]


def _hg_intra(q, k, v, cum, kk_ref, cc_ref, vv_ref, seg):
    rows = q.shape[0]
    pad = kk_ref.shape[0] - rows
    for ref, val in ((kk_ref, k), (cc_ref, cum), (vv_ref, v)):
        ref[0:pad, :] = jnp.zeros((pad, GROUP_W), F32)
        ref[pad:pad + rows, :] = val
    pos = _iota((rows, GROUP_W), 0) & (seg - 1)
    ones_bd = ((_iota((GROUP_W, GROUP_W), 0) >> 6) == (_iota((GROUP_W, GROUP_W), 1) >> 6)).astype(BF16)
    y = jnp.dot((q * k).astype(BF16), ones_bd, preferred_element_type=F32) * v
    for d in range(1, seg):
        k_d = kk_ref[pl.ds(pad - d, rows), :]
        c_d = cc_ref[pl.ds(pad - d, rows), :]
        v_d = vv_ref[pl.ds(pad - d, rows), :]
        p = jnp.where(pos >= d, q * k_d * jnp.exp(cum - c_d), 0.0)
        y = y + jnp.dot(p.astype(BF16), ones_bd, preferred_element_type=F32) * v_d
    return y


def _head_block_mask():
    return (_iota((GROUP_W, GROUP_W), 0) >> 6) == (_iota((GROUP_W, GROUP_W), 1) >> 6)


def _block_diag_rows(x, mask):
    return jnp.where(mask, jnp.concatenate([x] * N_HEADS, axis=0), 0.0)


def _hg_prompt_kernel(u_ref, lb_ref, g_ref, y_ref, s_out_ref, st_ref, y_scr):
    t = pl.program_id(1)
    rows = u_ref.shape[0]

    @pl.when(t == 0)
    def _():
        st_ref[...] = jnp.zeros(st_ref.shape, F32)

    q, k, v, cum, gate = _hg_inputs(u_ref, lb_ref, 6)
    hmask = _head_block_mask()
    ones_bd = hmask.astype(BF16)
    y_scr[...] = jnp.dot((q * k).astype(BF16), ones_bd, preferred_element_type=F32) * v
    row = _iota((rows, GROUP_W), 0)
    levels = []
    blk_last = cum
    h = 1
    while h < CHUNK:
        upper = (row & h) != 0
        qn = jnp.where(upper, q * jnp.exp(cum - pltpu.roll(blk_last, h, 0)), 0.0)
        kn = jnp.where(upper, 0.0, k * jnp.exp(blk_last - cum))
        levels.append((h, qn, kn))
        blk_last = jnp.where(upper, blk_last, pltpu.roll(blk_last, rows - h, 0))
        h *= 2
    q_in = q * jnp.exp(cum)
    k_end = k * jnp.exp(blk_last - cum)
    trow = _iota((CHUNK, GROUP_W), 0)
    tcol = _iota((CHUNK, GROUP_W), 1) & (CHUNK - 1)
    for c in range(rows // CHUNK):
        rs = slice(c * CHUNK, (c + 1) * CHUNK)
        a_wide = jnp.zeros((CHUNK, GROUP_W), F32)
        for h, qn, kn in levels:
            g = _bdot_nt(qn[rs], _block_diag_rows(kn[rs], hmask))
            sh = (2 * h).bit_length() - 1
            a_wide = a_wide + jnp.where((trow >> sh) == (tcol >> sh), g, 0.0)
        y_c = _bdot(a_wide, _block_diag_rows(v[rs], hmask)) + _bdot_nt(q_in[rs], st_ref[...])
        y_scr[rs, :] = y_scr[rs, :] + y_c
        ds_t = jnp.where(hmask, _bdot_tn(v[rs], k_end[rs]), 0.0)
        st_ref[...] = jnp.exp(blk_last[c * CHUNK:c * CHUNK + 1, :]) * st_ref[...] + ds_t
    y_ref[...] = _silu(gate) * (_head_stat(y_scr[...], False) * g_ref[...])

    @pl.when(t == pl.num_programs(1) - 1)
    def _():
        eye = (_iota((HEAD, HEAD), 0) == _iota((HEAD, HEAD), 1)).astype(BF16)
        for h in range(N_HEADS):
            s_out_ref[0, h] = _sel_dot_nt(eye, st_ref[h * HEAD:(h + 1) * HEAD, h * HEAD:(h + 1) * HEAD])


def _hg_sample_kernel(u_ref, s0_ref, lb_ref, g_ref, y_ref, s_out_ref, kk_ref, cc_ref, vv_ref, *, seqs):
    rows = u_ref.shape[0]
    q, k, v, cum, gate = _hg_inputs(u_ref, lb_ref, 2)
    y = _hg_intra(q, k, v, cum, kk_ref, cc_ref, vv_ref, 4)
    pick_last = (_iota((rows, rows), 1) == (_iota((rows, rows), 0) | 3)).astype(BF16)
    last = _sel_dot(pick_last, cum)
    qn = q * jnp.exp(cum)
    kend = k * jnp.exp(last - cum)
    elast = jnp.exp(last)
    row_seq = _iota((rows, HEAD), 0) >> 2
    eye = _iota((HEAD, HEAD), 0) == _iota((HEAD, HEAD), 1)
    y_int = []
    for h in range(N_HEADS):
        ls = slice(h * HEAD, (h + 1) * HEAD)
        acc = jnp.zeros((rows, HEAD), F32)
        for i in range(seqs):
            in_seq = row_seq == i
            s_prev = s0_ref[i, h]
            acc = jnp.where(in_seq, _bdot(qn[:, ls], s_prev), acc)
            ds = _bdot_tn(jnp.where(in_seq, kend[:, ls], 0.0), v[:, ls])
            a_col = jnp.sum(jnp.where(eye, elast[4 * i:4 * i + 1, ls], 0.0), axis=-1, keepdims=True)
            s_out_ref[i, h] = a_col * s_prev + ds
        y_int.append(acc)
    y = y + jnp.concatenate(y_int, axis=-1)
    y_ref[...] = _silu(gate) * (_head_stat(y, False) * g_ref[...])


def _hg(u, state, lb, gain, n_seq, seq_len):
    n = u.shape[0]
    vec = _const_spec((1, GROUP_W))
    if state is None:
        nt = seq_len // MIX_TILE
        row = lambda wd: pl.BlockSpec((MIX_TILE, wd), lambda b, t: (b * nt + t, 0))
        return pl.pallas_call(
            _hg_prompt_kernel,
            grid=(n_seq, nt),
            in_specs=[row(u.shape[1]), vec, vec],
            out_specs=[row(GROUP_W), pl.BlockSpec((1, N_HEADS, HEAD, HEAD), lambda b, t: (b, 0, 0, 0))],
            out_shape=[jax.ShapeDtypeStruct((n, GROUP_W), F32),
                       jax.ShapeDtypeStruct((n_seq, N_HEADS, HEAD, HEAD), F32)],
            scratch_shapes=[pltpu.VMEM((GROUP_W, GROUP_W), F32), pltpu.VMEM((MIX_TILE, GROUP_W), F32)],
            compiler_params=_params(("parallel", "arbitrary")),
            name="hgrn_prompt",
        )(u, lb, gain)
    rows = SAMPLE_SEQS * seq_len
    row = lambda wd: pl.BlockSpec((rows, wd), lambda i: (i, 0))
    st = pl.BlockSpec((SAMPLE_SEQS, N_HEADS, HEAD, HEAD), lambda i: (i, 0, 0, 0))
    shifted = pltpu.VMEM((rows + 8, GROUP_W), F32)
    return pl.pallas_call(
        functools.partial(_hg_sample_kernel, seqs=SAMPLE_SEQS),
        grid=(n_seq // SAMPLE_SEQS,),
        in_specs=[row(u.shape[1]), st, vec, vec],
        out_specs=[row(GROUP_W), st],
        out_shape=[jax.ShapeDtypeStruct((n, GROUP_W), F32), jax.ShapeDtypeStruct(state.shape, F32)],
        scratch_shapes=[shifted, shifted, shifted],
        compiler_params=_params(("parallel",)),
        name="hgrn_sample",
    )(u, state, lb, gain)


def _block_diag(w):
    h, i, j = w.shape
    eye = jnp.eye(h, dtype=w.dtype)
    return (eye[:, None, :, None] * w[:, :, None, :]).reshape(h * i, h * j)


def _rope_tables(pos0, seq_len):
    half = HEAD // 2
    pos = pos0 + jnp.arange(seq_len, dtype=F32)
    inv = ROPE_BASE ** (-jnp.arange(half, dtype=F32) / half)
    ang = pos[:, None] * inv
    cos = jnp.tile(jnp.cos(ang), (1, 2 * N_HEADS))
    sin = jnp.sin(ang)
    sin_signed = jnp.tile(jnp.concatenate([-sin, sin], axis=-1), (1, N_HEADS))
    return cos, sin_signed


def _layer_weights(P, l):
    w_in = P['w_in'][l]
    z0 = GROUP_W + (GROUP_W + 2 * SSD_G * SSD_N)
    d0 = z0 + N_HEADS
    ret_end = d0 + 4 * GROUP_W
    rg_end = ret_end + 2 * GROUP_W
    dt_cols = jnp.pad(w_in[:, z0:d0], ((0, 0), (0, 128 - N_HEADS)))
    w_in_r = jnp.concatenate([w_in[:, :z0], w_in[:, d0:ret_end], w_in[:, ret_end:rg_end],
                              w_in[:, rg_end:], dt_cols], axis=1).astype(BF16)
    row = lambda a: a.reshape(1, -1)
    pad128 = lambda a: jnp.pad(a, (0, 128 - a.shape[0])).reshape(1, 128)
    lb_sm = jax.nn.softmax(P['hg_lb'].astype(F32), axis=0)
    lb = (jnp.cumsum(lb_sm, axis=0) - lb_sm[0])[l]
    return dict(
        w_in=w_in_r,
        in_widths=(z0, 4 * GROUP_W, 2 * GROUP_W, 4 * GROUP_W, 128),
        ln_mix_pre=row(P['ln_mix_pre'][l]), ln_mix_post=row(P['ln_mix_post'][l]),
        ln_xa_pre=row(P['ln_xa_pre'][l]), ln_xa_post=row(P['ln_xa_post'][l]),
        ln_ffn_pre=row(P['ln_ffn_pre'][l]), ln_ffn_post=row(P['ln_ffn_post'][l]),
        ssd_cw=P['ssd_conv_w'][l], ssd_cb=row(P['ssd_conv_b'][l]),
        ssd_dtb=pad128(P['ssd_dt_bias'][l]), ssd_alog=pad128(P['ssd_A_log'][l]),
        ssd_d=row(jnp.repeat(P['ssd_D'][l], HEAD)), ssd_norm=row(P['ssd_norm'][l]),
        ret_norm=row(P['ret_norm'][l]),
        rg_cw=P['rg_conv_w'][l], rg_cb=row(P['rg_conv_b'][l]),
        rg_wa=_block_diag(P['rg_wa'][l]).astype(BF16), rg_ba=row(P['rg_ba'][l]),
        rg_wx=_block_diag(P['rg_wx'][l]).astype(BF16), rg_bx=row(P['rg_bx'][l]),
        rg_lam=row(P['rg_lambda'][l]),
        hg_lb=row(lb), hg_norm=row(P['hg_norm'][l]),
        w_out=P['w_out'][l].astype(BF16), w_xq=P['w_xq'][l].astype(BF16),
        w_xo=P['w_xo'][l].astype(BF16), w_gu=P['w_gu'][l].astype(BF16),
        w_down=P['w_down'][l].astype(BF16),
        ln_mem=row(P['ln_mem'][l]), w_xkv=P['w_xkv'][l].astype(BF16),
    )


def _layer(x, W, k_mem, v_mem, st, n_seq, seq_len, pos0, layer):
    u_ssd, u_ret, u_rg, u_hg, u_dt = _norm_matmul(x, W['ln_mix_pre'], W['w_in'], W['in_widths'], "in_proj")
    cos, sin_signed = _rope_tables(pos0, seq_len)
    if st is None:
        y_ssd, s_ssd, b_ssd = _ssd(u_ssd, u_dt, None, None, W['ssd_cw'], W['ssd_cb'], W['ssd_dtb'],
                                   W['ssd_alog'], W['ssd_d'], W['ssd_norm'], n_seq, seq_len)
        y_ret, s_ret = _ret(u_ret, cos, sin_signed, None, W['ret_norm'], n_seq, seq_len)
        y_rg, h_rg, b_rg = _rg(u_rg, None, None, W['rg_cw'], W['rg_cb'], W['rg_wa'], W['rg_ba'],
                               W['rg_wx'], W['rg_bx'], W['rg_lam'], n_seq, seq_len)
        h_rg = h_rg.reshape(n_seq, GROUP_W)
        y_hg, s_hg = _hg(u_hg, None, W['hg_lb'], W['hg_norm'], n_seq, seq_len)
    else:
        ssd_s, ssd_buf, ret_s, rg_h, rg_buf, hg_s = st
        pad_rows = lambda b: jnp.pad(b, ((0, 0), (0, 1), (0, 0))).reshape(n_seq * 4, b.shape[-1])
        y_ssd, s_ssd, b_ssd = _ssd(u_ssd, u_dt, ssd_s, pad_rows(ssd_buf), W['ssd_cw'], W['ssd_cb'],
                                   W['ssd_dtb'], W['ssd_alog'], W['ssd_d'], W['ssd_norm'], n_seq, seq_len)
        b_ssd = b_ssd.reshape(n_seq, 4, -1)[:, :CONV_W - 1]
        y_ret, s_ret = _ret(u_ret, cos, sin_signed, ret_s, W['ret_norm'], n_seq, seq_len)
        y_rg, h_rows, b_rg = _rg(u_rg, jnp.repeat(rg_h, seq_len, axis=0), pad_rows(rg_buf), W['rg_cw'],
                                 W['rg_cb'], W['rg_wa'], W['rg_ba'], W['rg_wx'], W['rg_bx'], W['rg_lam'],
                                 n_seq, seq_len)
        h_rg = h_rows.reshape(n_seq, seq_len, GROUP_W)[:, seq_len - 1]
        b_rg = b_rg.reshape(n_seq, 4, -1)[:, :CONV_W - 1]
        y_hg, s_hg = _hg(u_hg, hg_s, W['hg_lb'], W['hg_norm'], n_seq, seq_len)
    x1, q = _out_proj((y_ssd, y_ret, y_rg, y_hg), x, W['w_out'], W['ln_mix_post'], W['ln_xa_pre'], W['w_xq'])
    if st is None:
        o = _attention_prompt(q, k_mem, v_mem, seq_len)
    else:
        o = _attention_sample(q, k_mem, v_mem, layer, seq_len)
    x3 = _ffn(o, x1, W['w_xo'], W['ln_xa_post'], W['ln_ffn_pre'], W['w_gu'], W['w_down'], W['ln_ffn_post'])
    return x3, (s_ssd, b_ssd, s_ret, h_rg, b_rg, s_hg)


def kernel(x_prompt, x_sample, state_ssd, state_ssd_conv, state_ret, state_rglru, state_rglru_conv, state_hgrn, cache_mem_k, cache_mem_v, mem_prompt, ln_mix_pre, ln_mix_post, ln_xa_pre, ln_xa_post, ln_ffn_pre, ln_ffn_post, w_in, ssd_conv_w, ssd_conv_b, ssd_dt_bias, ssd_A_log, ssd_D, ssd_norm, ret_norm, rg_conv_w, rg_conv_b, rg_wa, rg_ba, rg_wx, rg_bx, rg_lambda, hg_lb, hg_norm, w_out, ln_mem, w_xq, w_xkv, w_xo, w_gu, w_down):
    P = dict(ln_mix_pre=ln_mix_pre, ln_mix_post=ln_mix_post, ln_xa_pre=ln_xa_pre, ln_xa_post=ln_xa_post,
             ln_ffn_pre=ln_ffn_pre, ln_ffn_post=ln_ffn_post, w_in=w_in, ssd_conv_w=ssd_conv_w,
             ssd_conv_b=ssd_conv_b, ssd_dt_bias=ssd_dt_bias, ssd_A_log=ssd_A_log, ssd_D=ssd_D,
             ssd_norm=ssd_norm, ret_norm=ret_norm, rg_conv_w=rg_conv_w, rg_conv_b=rg_conv_b,
             rg_wa=rg_wa, rg_ba=rg_ba, rg_wx=rg_wx, rg_bx=rg_bx, rg_lambda=rg_lambda, hg_lb=hg_lb,
             hg_norm=hg_norm, w_out=w_out, ln_mem=ln_mem, w_xq=w_xq, w_xkv=w_xkv, w_xo=w_xo,
             w_gu=w_gu, w_down=w_down)
    depth = w_in.shape[0]
    bp, tp, d = x_prompt.shape
    bs, ts, _ = x_sample.shape
    n_mem = mem_prompt.shape[1]
    y_p = x_prompt.reshape(bp * tp, d)
    y_s = x_sample.reshape(bs * ts, d)
    mem = mem_prompt.reshape(bp * n_mem, d)
    p_st, s_st, p_mk, p_mv = [], [], [], []
    for l in range(depth):
        W = _layer_weights(P, l)
        mk, mv = _norm_matmul(mem, W['ln_mem'], W['w_xkv'], (d, d), "mem_kv")
        mk = mk.reshape(bp, n_mem, d)
        mv = mv.reshape(bp, n_mem, d)
        y_p, st_p = _layer(y_p, W, mk, mv, None, bp, tp, 0.0, l)
        p_st.append(st_p)
        p_mk.append(mk.reshape(bp, n_mem, XA_H, XA_HD))
        p_mv.append(mv.reshape(bp, n_mem, XA_H, XA_HD))
        st_l = (state_ssd[l], state_ssd_conv[l], state_ret[l], state_rglru[l], state_rglru_conv[l],
                state_hgrn[l])
        y_s, st_s = _layer(y_s, W, cache_mem_k, cache_mem_v, st_l, bs, ts, float(PAST_LEN), l)
        s_st.append(st_s)
    stack = lambda sts, i: jnp.stack([s[i] for s in sts], axis=0)
    return (y_p.reshape(bp, tp, d), y_s.reshape(bs, ts, d),
            stack(p_st, 0), stack(p_st, 1), stack(p_st, 2), stack(p_st, 3), stack(p_st, 4), stack(p_st, 5),
            jnp.stack(p_mk, axis=0), jnp.stack(p_mv, axis=0),
            stack(s_st, 0), stack(s_st, 1), stack(s_st, 2), stack(s_st, 3), stack(s_st, 4), stack(s_st, 5))
```

```python
import functools
import math

import jax
import jax.numpy as jnp
from jax import lax
from jax.experimental import pallas as pl
from jax.experimental.pallas import tpu as pltpu

F32 = jnp.float32
BF16 = jnp.bfloat16
EPS = 1e-6

GROUP_W = 256
HEAD = 64
N_HEADS = 4
SSD_N = 128
SSD_G = 2
CONV_W = 4
XA_H = 4
XA_HD = 256
ROPE_BASE = 10000.0
RG_C = 8.0
CHUNK = 64
HG_CHUNK = 16
PAST_LEN = 16384

ROW_TILE = 512
MIX_TILE = 256
SAMPLE_SEQS = 16
ATT_TILE = 512
ATT_SEQS = 4
FF_TILE = 256
VMEM_LIMIT = 56 * 1024 * 1024


def _bdot(a, b):
    return jnp.dot(a.astype(BF16), b.astype(BF16), preferred_element_type=F32)


def _bdot_nt(a, b):
    return lax.dot_general(a.astype(BF16), b.astype(BF16), (((1,), (1,)), ((), ())),
                           preferred_element_type=F32)


def _bdot_tn(a, b):
    return lax.dot_general(a.astype(BF16), b.astype(BF16), (((0,), (0,)), ((), ())),
                           preferred_element_type=F32)


def _split3(x):
    hi = x.astype(BF16)
    r = x - hi.astype(F32)
    mid = r.astype(BF16)
    lo = (r - mid.astype(F32)).astype(BF16)
    return hi, mid, lo


def _sel_dot(sel, x):
    hi, mid, lo = _split3(x)
    d = lambda y: jnp.dot(sel, y, preferred_element_type=F32)
    return (d(hi) + d(mid)) + d(lo)


def _dot_sel(x, sel):
    hi, mid, lo = _split3(x)
    d = lambda y: jnp.dot(y, sel, preferred_element_type=F32)
    return (d(hi) + d(mid)) + d(lo)


def _sel_dot_nt(sel, x):
    hi, mid, lo = _split3(x)
    d = lambda y: lax.dot_general(sel, y, (((1,), (1,)), ((), ())), preferred_element_type=F32)
    return (d(hi) + d(mid)) + d(lo)


def _rms(x, g):
    return x * lax.rsqrt(jnp.mean(x * x, axis=-1, keepdims=True) + EPS) * g


def _sigmoid(x):
    return jax.nn.sigmoid(x)


def _silu(x):
    return x * jax.nn.sigmoid(x)


def _softplus(x):
    return jnp.maximum(x, 0.0) + jnp.log1p(jnp.exp(-jnp.abs(x)))


def _gelu_tanh(x):
    c = math.sqrt(2.0 / math.pi)
    return 0.5 * x * (1.0 + jnp.tanh(c * (x + 0.044715 * (x * x * x))))


def _iota(shape, dim):
    return lax.broadcasted_iota(jnp.int32, shape, dim)


def _seg_causal_mask(n, seg_shift):
    r = _iota((n, n), 0)
    c = _iota((n, n), 1)
    return ((r >> seg_shift) == (c >> seg_shift)) & (c <= r)


def _head_expand_mat(width_in=128):
    r = _iota((width_in, GROUP_W), 0)
    c = _iota((width_in, GROUP_W), 1)
    return ((c >> 6) == r).astype(BF16)


def _head_stat(y, center):
    parts = []
    for h in range(N_HEADS):
        yh = y[:, h * HEAD:(h + 1) * HEAD]
        if center:
            yh = yh - jnp.mean(yh, axis=-1, keepdims=True)
        yh = yh * lax.rsqrt(jnp.mean(yh * yh, axis=-1, keepdims=True) + EPS)
        parts.append(yh)
    return jnp.concatenate(parts, axis=-1)


def _conv_taps_carry(x, xx_ref, w_ref, b_ref, first):
    rows = x.shape[0]

    @pl.when(first)
    def _():
        xx_ref[0:8, :] = jnp.zeros((8, x.shape[1]), F32)

    xx_ref[8:8 + rows, :] = x
    y = b_ref[...] + w_ref[CONV_W - 1:CONV_W, :] * x
    for j in range(CONV_W - 1):
        y = y + w_ref[j:j + 1, :] * xx_ref[pl.ds(8 - (CONV_W - 1) + j, rows), :]
    xx_ref[0:8, :] = xx_ref[rows:rows + 8, :]
    return y


def _conv_taps_seq4(x, buf4, w_ref, b_ref):
    rows = x.shape[0]
    r = _iota((rows, rows), 0)
    c = _iota((rows, rows), 1)
    t = r & 3
    y = b_ref[...] + w_ref[CONV_W - 1:CONV_W, :] * x
    for d in range(1, CONV_W):
        shift = ((c == r - d) & (t >= d)).astype(BF16)
        hist = ((t < d) & (c == (r - t) + (3 + t - d))).astype(BF16)
        y = y + w_ref[CONV_W - 1 - d:CONV_W - d, :] * (_sel_dot(shift, x) + _sel_dot(hist, buf4))
    new_sel = (((r & 3) < 3) & (c == r + 1)).astype(BF16)
    return y, _sel_dot(new_sel, x)


def _norm_matmul_kernel(x_ref, g_ref, w_ref, *o_refs, widths):
    h = _rms(x_ref[...], g_ref[...]).astype(BF16)
    off = 0
    for o_ref, width in zip(o_refs, widths):
        o_ref[...] = jnp.dot(h, w_ref[:, off:off + width], preferred_element_type=F32)
        off += width


def _const_spec(shape):
    nd = len(shape)
    return pl.BlockSpec(shape, lambda *_: (0,) * nd, pipeline_mode=pl.Buffered(1))


def _layer_spec(shape, layer):
    nd = len(shape)
    return pl.BlockSpec((None,) + tuple(shape[1:]), lambda *_: (layer,) + (0,) * (nd - 1),
                        pipeline_mode=pl.Buffered(1))


def _params(sem):
    return pltpu.CompilerParams(dimension_semantics=sem, vmem_limit_bytes=VMEM_LIMIT)


def _in_weight_prep_kernel(wt_ref, o_ref, *, src_rows, dt_row, dt_n):
    for j, src in enumerate(src_rows):
        o_ref[:, j * 128:(j + 1) * 128] = wt_ref[pl.ds(src, 128), :].T.astype(BF16)
    dt = wt_ref[pl.ds(dt_row, 8), :]
    dt = jnp.where(_iota(dt.shape, 0) < dt_n, dt, 0.0)
    dt = jnp.concatenate([dt, jnp.zeros((120, dt.shape[1]), F32)], axis=0)
    j = len(src_rows)
    o_ref[:, j * 128:(j + 1) * 128] = dt.T.astype(BF16)


def _in_weight_prep(w_in):
    depth, d, d_in = w_in.shape
    z0 = GROUP_W + (GROUP_W + 2 * SSD_G * SSD_N)
    d0 = z0 + N_HEADS
    src_rows = tuple(range(0, z0, 128)) + tuple(range(d0, d_in, 128))
    n_out = (len(src_rows) + 1) * 128
    wt = jnp.swapaxes(w_in, 1, 2)
    return pl.pallas_call(
        functools.partial(_in_weight_prep_kernel, src_rows=src_rows, dt_row=z0, dt_n=N_HEADS),
        grid=(depth,),
        in_specs=[pl.BlockSpec((None, d_in, d), lambda l: (l, 0, 0))],
        out_specs=pl.BlockSpec((None, d, n_out), lambda l: (l, 0, 0)),
        out_shape=jax.ShapeDtypeStruct((depth, d, n_out), BF16),
        compiler_params=_params(("parallel",)),
        name="in_weight_prep",
    )(wt)


def _norm_matmul(x, g, w, widths, name, layer=None):
    n, d = x.shape
    tm = min(ROW_TILE, n)
    w_spec = _const_spec(w.shape) if layer is None else _layer_spec(w.shape, layer)
    return pl.pallas_call(
        functools.partial(_norm_matmul_kernel, widths=widths),
        grid=(n // tm,),
        in_specs=[pl.BlockSpec((tm, d), lambda i: (i, 0)), _const_spec((1, d)), w_spec],
        out_specs=[pl.BlockSpec((tm, wd), lambda i: (i, 0)) for wd in widths],
        out_shape=[jax.ShapeDtypeStruct((n, wd), F32) for wd in widths],
        compiler_params=_params(("parallel",)),
        name=name,
    )(x, g, w)


def _out_proj_kernel(ys_ref, yr_ref, yg_ref, yh_ref, x_ref, wo_ref, gpost_ref, *rest, with_q):
    mix = _bdot(ys_ref[...], wo_ref[0:GROUP_W, :])
    mix = mix + _bdot(yr_ref[...], wo_ref[GROUP_W:2 * GROUP_W, :])
    mix = mix + _bdot(yg_ref[...], wo_ref[2 * GROUP_W:3 * GROUP_W, :])
    mix = mix + _bdot(yh_ref[...], wo_ref[3 * GROUP_W:4 * GROUP_W, :])
    x1 = x_ref[...] + _rms(mix, gpost_ref[...])
    if with_q:
        gpre_ref, wq_ref, x1_ref, q_ref = rest
        q_ref[...] = _bdot(_rms(x1, gpre_ref[...]), wq_ref[...])
    else:
        (x1_ref,) = rest
    x1_ref[...] = x1


def _out_proj(ys, x, w_out, g_post, g_pre=None, w_xq=None):
    n, d = x.shape
    tm = min(ROW_TILE, n)
    with_q = w_xq is not None
    row = lambda wd: pl.BlockSpec((tm, wd), lambda i: (i, 0))
    extra_in = [_const_spec((1, d)), _const_spec(w_xq.shape)] if with_q else []
    extra_args = (g_pre, w_xq) if with_q else ()
    n_out = 2 if with_q else 1
    return pl.pallas_call(
        functools.partial(_out_proj_kernel, with_q=with_q),
        grid=(n // tm,),
        in_specs=[row(GROUP_W)] * 4 + [row(d), _const_spec(w_out.shape), _const_spec((1, d))] + extra_in,
        out_specs=[row(d)] * n_out,
        out_shape=[jax.ShapeDtypeStruct((n, d), F32)] * n_out,
        compiler_params=_params(("parallel",)),
        name="out_proj",
    )(*ys, x, w_out, g_post, *extra_args)


def _ffn_tail(o, x1, wxo_ref, gxa_ref, gpre_ref, wgu_ref, wdn_ref, gpost_ref, d_ff):
    a = _bdot(o, wxo_ref[...])
    x2 = x1 + _rms(a, gxa_ref[...])
    h = _rms(x2, gpre_ref[...]).astype(BF16)
    acc = jnp.zeros(x2.shape, F32)
    for j in range(0, d_ff, FF_TILE):
        g = jnp.dot(h, wgu_ref[:, j:j + FF_TILE], preferred_element_type=F32)
        u = jnp.dot(h, wgu_ref[:, d_ff + j:d_ff + j + FF_TILE], preferred_element_type=F32)
        act = (_silu(g) * u).astype(BF16)
        acc = acc + jnp.dot(act, wdn_ref[j:j + FF_TILE, :], preferred_element_type=F32)
    return x2 + _rms(acc, gpost_ref[...])


def _ffn_kernel(o_ref, x1_ref, wxo_ref, gxa_ref, gpre_ref, wgu_ref, wdn_ref, gpost_ref, x3_ref, *, d_ff):
    x3_ref[...] = _ffn_tail(o_ref[...], x1_ref[...], wxo_ref, gxa_ref, gpre_ref, wgu_ref, wdn_ref, gpost_ref,
                            d_ff)


def _ffn(o, x1, w_xo, g_xa, g_pre, w_gu, w_down, g_post):
    n, d = x1.shape
    d_ff = w_down.shape[0]
    tm = min(ROW_TILE, n)
    row = pl.BlockSpec((tm, d), lambda i: (i, 0))
    vec = _const_spec((1, d))
    return pl.pallas_call(
        functools.partial(_ffn_kernel, d_ff=d_ff),
        grid=(n // tm,),
        in_specs=[row, row, _const_spec(w_xo.shape), vec, vec, _const_spec(w_gu.shape),
                  _const_spec(w_down.shape), vec],
        out_specs=row,
        out_shape=jax.ShapeDtypeStruct((n, d), F32),
        compiler_params=_params(("parallel",)),
        name="attn_out_ffn",
    )(o, x1, w_xo, g_xa, g_pre, w_gu, w_down, g_post)


def _xattn_ffn_kernel(x1_ref, k_ref, v_ref, gq_ref, wq_ref, wxo_ref, gxa_ref, gpre_ref, wgu_ref, wdn_ref,
                      gpost_ref, x3_ref, o_scr, *, d_ff):
    x1 = x1_ref[...]
    q = _bdot(_rms(x1, gq_ref[...]), wq_ref[...])
    scale = XA_HD ** -0.5
    for h in range(XA_H):
        sl = slice(h * XA_HD, (h + 1) * XA_HD)
        s = _bdot_nt(q[:, sl], k_ref[0, :, sl]) * scale
        o_scr[:, sl] = _softmax_pv(s, v_ref[0, :, sl])
    x3_ref[...] = _ffn_tail(o_scr[...], x1, wxo_ref, gxa_ref, gpre_ref, wgu_ref, wdn_ref, gpost_ref, d_ff)


def _xattn_ffn(x1, k, v, g_q, w_xq, w_xo, g_xa, g_pre, w_gu, w_down, g_post, seq_len):
    n, d = x1.shape
    n_seq, n_mem, _ = k.shape
    d_ff = w_down.shape[0]
    nt = seq_len // ROW_TILE
    row = pl.BlockSpec((ROW_TILE, d), lambda b, t: (b * nt + t, 0))
    kv = pl.BlockSpec((1, n_mem, d), lambda b, t: (b, 0, 0))
    vec = _const_spec((1, d))
    return pl.pallas_call(
        functools.partial(_xattn_ffn_kernel, d_ff=d_ff),
        grid=(n_seq, nt),
        in_specs=[row, kv, kv, vec, _const_spec(w_xq.shape), _const_spec(w_xo.shape), vec, vec,
                  _const_spec(w_gu.shape), _const_spec(w_down.shape), vec],
        out_specs=row,
        out_shape=jax.ShapeDtypeStruct((n, d), F32),
        scratch_shapes=[pltpu.VMEM((ROW_TILE, d), F32)],
        compiler_params=_params(("parallel", "parallel")),
        name="xattn_ffn",
    )(x1, k, v, g_q, w_xq, w_xo, g_xa, g_pre, w_gu, w_down, g_post)


def _softmax_pv(s, v):
    m = jnp.max(s, axis=-1, keepdims=True)
    p = jnp.exp(s - m)
    return _bdot(p, v) / jnp.sum(p, axis=-1, keepdims=True)


def _attn_sample_kernel(q_ref, k_ref, v_ref, o_ref, *, seqs, seg):
    scale = XA_HD ** -0.5
    rows = seqs * seg
    n_mem = k_ref.shape[2]
    wq = jnp.concatenate([q_ref[:, h * XA_HD:(h + 1) * XA_HD] for h in range(XA_H)], axis=0)
    wshape = (XA_H * rows, n_mem * XA_H)
    valid = (_iota(wshape, 0) >> (rows.bit_length() - 1)) == (_iota(wshape, 1) & (XA_H - 1))
    row_seq = _iota((rows, XA_HD), 0) >> (seg.bit_length() - 1)
    accs = [jnp.zeros((rows, XA_HD), F32) for _ in range(XA_H)]
    for i in range(seqs):
        k2 = k_ref[0, i].reshape(n_mem * XA_H, XA_HD)
        v2 = v_ref[0, i].reshape(n_mem * XA_H, XA_HD)
        s = jnp.where(valid, _bdot_nt(wq, k2) * scale, -jnp.inf)
        o = _softmax_pv(s, v2)
        for h in range(XA_H):
            accs[h] = jnp.where(row_seq == i, o[h * rows:(h + 1) * rows], accs[h])
    for h in range(XA_H):
        o_ref[:, h * XA_HD:(h + 1) * XA_HD] = accs[h]


def _attention_sample(q, cache_k, cache_v, layer, seq_len):
    n, d = q.shape
    _, n_seq, n_mem, heads, hd = cache_k.shape
    kv = pl.BlockSpec((1, ATT_SEQS, n_mem, heads, hd), lambda i: (layer, i, 0, 0, 0))
    row = pl.BlockSpec((ATT_SEQS * seq_len, d), lambda i: (i, 0))
    return pl.pallas_call(
        functools.partial(_attn_sample_kernel, seqs=ATT_SEQS, seg=seq_len),
        grid=(n_seq // ATT_SEQS,),
        in_specs=[row, kv, kv],
        out_specs=row,
        out_shape=jax.ShapeDtypeStruct((n, d), F32),
        compiler_params=_params(("parallel",)),
        name="xattn_sample",
    )(q, cache_k, cache_v)


def _ssd_intra(q, k, v, cum_x, cum_t, mask):
    sc = _bdot_nt(q, k)
    outs = []
    for hh in range(2):
        col = cum_x[:, hh * HEAD:(hh + 1) * HEAD]
        dec = jnp.exp(jnp.where(mask, col - cum_t[hh:hh + 1, :], -jnp.inf))
        outs.append(_bdot(sc * dec, v[:, hh * HEAD:(hh + 1) * HEAD]))
    return jnp.concatenate(outs, axis=-1)


def _ssd_common(u, dtraw_ref, dtb_ref, alog_ref, conv, mask_cum):
    rows = u.shape[0]
    xbc = _silu(conv)
    xs = xbc[:, 0:GROUP_W]
    bm = xbc[:, GROUP_W:2 * GROUP_W]
    cm = xbc[:, 2 * GROUP_W:3 * GROUP_W]
    dt = _softplus(dtraw_ref[...] + dtb_ref[...])
    la = -jnp.exp(alog_ref[...]) * dt
    cum = _sel_dot(mask_cum, la)
    expand = _head_expand_mat()
    dt_x = _dot_sel(dt, expand)
    cum_x = _dot_sel(cum, expand)
    sel8 = (_iota((8, 128), 0) == _iota((8, 128), 1)).astype(BF16)
    cum_t = _sel_dot_nt(sel8, cum)
    expand_full = ((_iota((128, 4 * 128), 1) >> 7) == _iota((128, 4 * 128), 0)).astype(BF16)
    cum_full = _dot_sel(cum, expand_full)
    return xs, bm, cm, dt_x, cum_x, cum_t, cum_full


def _ssd_finish(y, xs, z, d_ref, g_ref):
    y = y + d_ref[...] * xs
    return _rms(y * _silu(z), g_ref[...])


def _ssd_prompt_kernel(u_ref, dtraw_ref, cw_ref, cb_ref, dtb_ref, alog_ref, d_ref, g_ref,
                       y_ref, s_out_ref, buf_out_ref, s_ref, xx_ref, y_scr):
    t = pl.program_id(1)
    rows = u_ref.shape[0]

    @pl.when(t == 0)
    def _():
        s_ref[...] = jnp.zeros(s_ref.shape, F32)

    u = u_ref[...]
    conv = _conv_taps_carry(u[:, GROUP_W:], xx_ref, cw_ref, cb_ref, t == 0)
    mask_cum = _seg_causal_mask(rows, 6).astype(BF16)
    xs, bm, cm, dt_x, cum_x, _, cum_full = _ssd_common(u, dtraw_ref, dtb_ref, alog_ref, conv, mask_cum)
    v = xs * dt_x
    ecum_x = jnp.exp(cum_x)
    hmask = _head_block_mask()
    gmask = (_iota((GROUP_W, GROUP_W), 0) >> 7) == (_iota((GROUP_W, GROUP_W), 1) >> 7)
    wrow = _iota((CHUNK, GROUP_W), 0)
    wcol = _iota((CHUNK, GROUP_W), 1) & (CHUNK - 1)
    for c in range(rows // CHUNK):
        rs = slice(c * CHUNK, (c + 1) * CHUNK)
        cum_c = cum_x[rs]
        last_x = cum_x[c * CHUNK + CHUNK - 1:(c + 1) * CHUNK, :]
        vend = v[rs] * jnp.exp(last_x - cum_c)
        cum_row = jnp.sum(jnp.where(wrow == wcol, cum_c, 0.0), axis=0, keepdims=True)
        dec = jnp.exp(jnp.where(wcol <= wrow, cum_c - cum_row, -jnp.inf))
        b_stack = jnp.where(gmask, jnp.concatenate([bm[rs]] * N_HEADS, axis=0), 0.0)
        y_intra = _bdot(_bdot_nt(cm[rs], b_stack) * dec, _block_diag_rows(v[rs], hmask))
        last_full = cum_full[c * CHUNK + CHUNK - 1:(c + 1) * CHUNK, :]
        y_inter = []
        for g in range(SSD_G):
            ls = slice(g * 128, (g + 1) * 128)
            s_prev = s_ref[ls, :]
            y_inter.append(_bdot_nt(cm[rs, ls], s_prev))
            decay = jnp.concatenate(
                [jnp.broadcast_to(jnp.exp(last_full[:, h * 128:(h + 1) * 128]), (HEAD, 128))
                 for h in (2 * g, 2 * g + 1)], axis=0)
            s_ref[ls, :] = decay * s_prev + _bdot_tn(vend[:, ls], bm[rs, ls])
        y_scr[rs, :] = y_intra + jnp.concatenate(y_inter, axis=-1) * ecum_x[rs]
    y_ref[...] = _ssd_finish(y_scr[...], xs, u[:, 0:GROUP_W], d_ref, g_ref)

    @pl.when(t == pl.num_programs(1) - 1)
    def _():
        for h in range(N_HEADS):
            s_out_ref[0, h] = s_ref[h * HEAD:(h + 1) * HEAD, :]
        buf_out_ref[0] = xx_ref[pl.ds(8 - (CONV_W - 1), CONV_W - 1), :]


def _ssd_sample_kernel(u_ref, dtraw_ref, s0_ref, buf_ref, cw_ref, cb_ref, dtb_ref, alog_ref, d_ref,
                       g_ref, y_ref, s_out_ref, buf_out_ref, *, seqs):
    rows = u_ref.shape[0]
    u = u_ref[...]
    conv, new_buf = _conv_taps_seq4(u[:, GROUP_W:], buf_ref[...], cw_ref, cb_ref)
    buf_out_ref[...] = new_buf
    mask = _seg_causal_mask(rows, 2)
    xs, bm, cm, dt_x, cum_x, cum_t, cum_full = _ssd_common(u, dtraw_ref, dtb_ref, alog_ref, conv,
                                                           mask.astype(BF16))
    v = xs * dt_x
    ecum_x = jnp.exp(cum_x)
    pick_last = (_iota((rows, rows), 1) == (_iota((rows, rows), 0) | 3)).astype(BF16)
    last_x = _sel_dot(pick_last, cum_x)
    vend = v * jnp.exp(last_x - cum_x)
    row_seq = _iota((rows, 128), 0) >> 2
    ys = []
    for g in range(SSD_G):
        ls = slice(g * 128, (g + 1) * 128)
        q = cm[:, ls]
        k = bm[:, ls]
        y_g = _ssd_intra(q, k, v[:, ls], cum_x[:, ls], cum_t[2 * g:2 * g + 2, :], mask)
        y_int = jnp.zeros((rows, 128), F32)
        for i in range(seqs):
            in_seq = row_seq == i
            s_prev = jnp.concatenate([s0_ref[i, 2 * g], s0_ref[i, 2 * g + 1]], axis=0)
            y_int = jnp.where(in_seq, _bdot_nt(q, s_prev), y_int)
            ds = _bdot_tn(jnp.where(in_seq, vend[:, ls], 0.0), k)
            for hh in range(2):
                h = 2 * g + hh
                decay = jnp.exp(cum_full[4 * i + 3:4 * i + 4, h * 128:(h + 1) * 128])
                s_out_ref[i, h] = decay * s0_ref[i, h] + ds[hh * HEAD:(hh + 1) * HEAD, :]
        ys.append(y_g + y_int * ecum_x[:, ls])
    y_ref[...] = _ssd_finish(jnp.concatenate(ys, axis=-1), xs, u[:, 0:GROUP_W], d_ref, g_ref)


def _ssd(u, dtraw, state, buf, cw, cb, dtb, alog, d_x, gain, n_seq, seq_len, layer):
    n = u.shape[0]
    cch = cw.shape[1]
    vec = lambda wd: _const_spec((1, wd))
    common_in = [_const_spec(cw.shape), vec(cch), vec(128), vec(128), vec(GROUP_W), vec(GROUP_W)]
    if state is None:
        nt = seq_len // MIX_TILE
        row = lambda wd: pl.BlockSpec((MIX_TILE, wd), lambda b, t: (b * nt + t, 0))
        return pl.pallas_call(
            _ssd_prompt_kernel,
            grid=(n_seq, nt),
            in_specs=[row(u.shape[1]), row(128)] + common_in,
            out_specs=[row(GROUP_W),
                       pl.BlockSpec((1, N_HEADS, HEAD, SSD_N), lambda b, t: (b, 0, 0, 0)),
                       pl.BlockSpec((1, CONV_W - 1, cch), lambda b, t: (b, 0, 0))],
            out_shape=[jax.ShapeDtypeStruct((n, GROUP_W), F32),
                       jax.ShapeDtypeStruct((n_seq, N_HEADS, HEAD, SSD_N), F32),
                       jax.ShapeDtypeStruct((n_seq, CONV_W - 1, cch), F32)],
            scratch_shapes=[pltpu.VMEM((GROUP_W, SSD_N), F32), pltpu.VMEM((MIX_TILE + 8, cch), F32),
                            pltpu.VMEM((MIX_TILE, GROUP_W), F32)],
            compiler_params=_params(("parallel", "arbitrary")),
            name="ssd_prompt",
        )(u, dtraw, cw, cb, dtb, alog, d_x, gain)
    rows = SAMPLE_SEQS * seq_len
    row = lambda wd: pl.BlockSpec((rows, wd), lambda i: (i, 0))
    st = pl.BlockSpec((SAMPLE_SEQS, N_HEADS, HEAD, SSD_N), lambda i: (i, 0, 0, 0))
    st_in = pl.BlockSpec((None, SAMPLE_SEQS, N_HEADS, HEAD, SSD_N), lambda i: (layer, i, 0, 0, 0))
    return pl.pallas_call(
        functools.partial(_ssd_sample_kernel, seqs=SAMPLE_SEQS),
        grid=(n_seq // SAMPLE_SEQS,),
        in_specs=[row(u.shape[1]), row(128), st_in, row(cch)] + common_in,
        out_specs=[row(GROUP_W), st, row(cch)],
        out_shape=[jax.ShapeDtypeStruct((n, GROUP_W), F32),
                   jax.ShapeDtypeStruct(state.shape[1:], F32),
                   jax.ShapeDtypeStruct((n, cch), F32)],
        compiler_params=_params(("parallel",)),
        name="ssd_sample",
    )(u, dtraw, state, buf, cw, cb, dtb, alog, d_x, gain)


def _rope(x, cos, sin_signed):
    lane = _iota(x.shape, 1)
    swapped = jnp.where((lane & 63) < 32, pltpu.roll(x, GROUP_W - 32, 1), pltpu.roll(x, 32, 1))
    return x * cos + swapped * sin_signed


def _ret_qkv(u_ref, cos_ref, sin_ref):
    u = u_ref[...]
    q = _rope(u[:, 0:GROUP_W], cos_ref[...], sin_ref[...])
    k = _rope(u[:, GROUP_W:2 * GROUP_W], cos_ref[...], sin_ref[...]) * (HEAD ** -0.5)
    return q, k, u[:, 2 * GROUP_W:3 * GROUP_W], u[:, 3 * GROUP_W:4 * GROUP_W]


def _ret_prompt_kernel(u_ref, cos_ref, sin_ref, ecum_ref, eend_ref, elast_ref, dec_ref, g_ref,
                       y_ref, s_out_ref, s_ref, y_scr):
    t = pl.program_id(1)
    rows = u_ref.shape[0]

    @pl.when(t == 0)
    def _():
        s_ref[...] = jnp.zeros(s_ref.shape, F32)

    q, k, v, gate = _ret_qkv(u_ref, cos_ref, sin_ref)
    hmask = _head_block_mask()
    for c in range(rows // CHUNK):
        rs = slice(c * CHUNK, (c + 1) * CHUNK)
        a_wide = _bdot_nt(q[rs], _block_diag_rows(k[rs], hmask)) * dec_ref[...]
        y_intra = _bdot(a_wide, _block_diag_rows(v[rs], hmask))
        y_scr[rs, :] = y_intra + _bdot(q[rs], s_ref[...]) * ecum_ref[...]
        ds = jnp.where(hmask, _bdot_tn(k[rs], v[rs] * eend_ref[...]), 0.0)
        s_ref[...] = elast_ref[...] * s_ref[...] + ds
    y_ref[...] = _silu(gate) * (_head_stat(y_scr[...], True) * g_ref[...])

    @pl.when(t == pl.num_programs(1) - 1)
    def _():
        for h in range(N_HEADS):
            s_out_ref[0, h] = s_ref[h * HEAD:(h + 1) * HEAD, h * HEAD:(h + 1) * HEAD]


def _ret_sample_kernel(u_ref, cos_ref, sin_ref, s0_ref, ecum_ref, eend_ref, elast_ref, dec_ref, g_ref,
                       y_ref, s_out_ref, *, seqs):
    rows = u_ref.shape[0]
    q, k, v, gate = _ret_qkv(u_ref, cos_ref, sin_ref)
    vend = v * eend_ref[...]
    row_seq = _iota((rows, HEAD), 0) >> 2
    ys = []
    for h in range(N_HEADS):
        ls = slice(h * HEAD, (h + 1) * HEAD)
        qh = q[:, ls]
        kh = k[:, ls]
        y_h = _bdot(_bdot_nt(qh, kh) * dec_ref[h], v[:, ls])
        y_int = jnp.zeros((rows, HEAD), F32)
        for i in range(seqs):
            in_seq = row_seq == i
            s_prev = s0_ref[i, h]
            y_int = jnp.where(in_seq, _bdot(qh, s_prev), y_int)
            ds = _bdot_tn(kh, jnp.where(in_seq, vend[:, ls], 0.0))
            s_out_ref[i, h] = elast_ref[:, ls] * s_prev + ds
        ys.append(y_h + y_int * ecum_ref[:, ls])
    y_ref[...] = _silu(gate) * (_head_stat(jnp.concatenate(ys, axis=-1), True) * g_ref[...])


def _ret_tables(block, seg):
    log_gamma = jnp.log(1.0 - jnp.exp2(-5.0 - jnp.arange(N_HEADS, dtype=F32)))
    pos = (jnp.arange(block) % seg).astype(F32)
    cum = (pos[:, None] + 1.0) * log_gamma[None, :]
    last = seg * log_gamma
    rep = lambda a: jnp.repeat(a, HEAD, axis=-1)
    same = (jnp.arange(block)[:, None] // seg) == (jnp.arange(block)[None, :] // seg)
    mask = same & (jnp.arange(block)[None, :] <= jnp.arange(block)[:, None])
    dec = jnp.exp(jnp.where(mask[None], cum.T[:, :, None] - cum.T[:, None, :], -jnp.inf))
    return (rep(jnp.exp(cum)), rep(jnp.exp(last[None, :] - cum)), rep(jnp.exp(last)[None, :]), dec)


def _ret(u, cos, sin_signed, state, gain, n_seq, seq_len, layer):
    n = u.shape[0]
    vec = _const_spec((1, GROUP_W))
    if state is None:
        nt = seq_len // MIX_TILE
        ecum, eend, elast, dec = _ret_tables(CHUNK, CHUNK)
        dec = dec.transpose(1, 0, 2).reshape(CHUNK, N_HEADS * CHUNK)
        row = lambda wd: pl.BlockSpec((MIX_TILE, wd), lambda b, t: (b * nt + t, 0))
        pos = pl.BlockSpec((MIX_TILE, GROUP_W), lambda b, t: (t, 0))
        return pl.pallas_call(
            _ret_prompt_kernel,
            grid=(n_seq, nt),
            in_specs=[row(u.shape[1]), pos, pos, _const_spec(ecum.shape), _const_spec(eend.shape), vec,
                      _const_spec(dec.shape), vec],
            out_specs=[row(GROUP_W), pl.BlockSpec((1, N_HEADS, HEAD, HEAD), lambda b, t: (b, 0, 0, 0))],
            out_shape=[jax.ShapeDtypeStruct((n, GROUP_W), F32),
                       jax.ShapeDtypeStruct((n_seq, N_HEADS, HEAD, HEAD), F32)],
            scratch_shapes=[pltpu.VMEM((GROUP_W, GROUP_W), F32), pltpu.VMEM((MIX_TILE, GROUP_W), F32)],
            compiler_params=_params(("parallel", "arbitrary")),
            name="ret_prompt",
        )(u, cos, sin_signed, ecum, eend, elast, dec, gain)
    rows = SAMPLE_SEQS * seq_len
    ecum, eend, elast, dec = _ret_tables(rows, seq_len)
    row = lambda wd: pl.BlockSpec((rows, wd), lambda i: (i, 0))
    st = pl.BlockSpec((SAMPLE_SEQS, N_HEADS, HEAD, HEAD), lambda i: (i, 0, 0, 0))
    cos_t = jnp.tile(cos, (SAMPLE_SEQS, 1))
    sin_t = jnp.tile(sin_signed, (SAMPLE_SEQS, 1))
    return pl.pallas_call(
        functools.partial(_ret_sample_kernel, seqs=SAMPLE_SEQS),
        grid=(n_seq // SAMPLE_SEQS,),
        in_specs=[row(u.shape[1]), _const_spec(cos_t.shape), _const_spec(sin_t.shape),
                  pl.BlockSpec((None, SAMPLE_SEQS, N_HEADS, HEAD, HEAD), lambda i: (layer, i, 0, 0, 0)),
                  _const_spec(ecum.shape), _const_spec(eend.shape), vec, _const_spec(dec.shape), vec],
        out_specs=[row(GROUP_W), st],
        out_shape=[jax.ShapeDtypeStruct((n, GROUP_W), F32), jax.ShapeDtypeStruct(state.shape[1:], F32)],
        compiler_params=_params(("parallel",)),
        name="ret_sample",
    )(u, cos_t, sin_t, state, ecum, eend, elast, dec, gain)


def _rg_gates(xr, wa_ref, ba_ref, wx_ref, bx_ref, lam_ref):
    r_gate = _sigmoid(_bdot(xr, wa_ref[...]) + ba_ref[...])
    i_gate = _sigmoid(_bdot(xr, wx_ref[...]) + bx_ref[...])
    log_a = (-RG_C * _softplus(-lam_ref[...])) * r_gate
    a = jnp.exp(log_a)
    b = jnp.sqrt(-jnp.tanh(log_a) * (jnp.exp(2.0 * log_a) + 1.0)) * (i_gate * xr)
    return a, b


def _rg_scan(a, b, seg):
    pos = _iota(a.shape, 0) & (seg - 1)
    d = 1
    while d < seg:
        ok = pos >= d
        a_sh = jnp.where(ok, pltpu.roll(a, d, 0), 1.0)
        b_sh = jnp.where(ok, pltpu.roll(b, d, 0), 0.0)
        b = a * b_sh + b
        a = a * a_sh
        d *= 2
    return a, b


def _rg_prompt_kernel(u_ref, cw_ref, cb_ref, wa_ref, ba_ref, wx_ref, bx_ref, lam_ref,
                      y_ref, h_out_ref, buf_out_ref, h_ref, xx_ref):
    t = pl.program_id(1)
    rows = u_ref.shape[0]

    @pl.when(t == 0)
    def _():
        h_ref[...] = jnp.zeros(h_ref.shape, F32)

    u = u_ref[...]
    xr = _conv_taps_carry(u[:, 0:GROUP_W], xx_ref, cw_ref, cb_ref, t == 0)
    a, b = _rg_gates(xr, wa_ref, ba_ref, wx_ref, bx_ref, lam_ref)
    a_cum, h_loc = _rg_scan(a, b, rows)
    hseq = h_loc + a_cum * h_ref[0:1, :]
    h_ref[...] = jnp.broadcast_to(hseq[rows - 1:rows, :], h_ref.shape)
    y_ref[...] = _gelu_tanh(u[:, GROUP_W:]) * hseq

    @pl.when(t == pl.num_programs(1) - 1)
    def _():
        h_out_ref[0] = hseq[rows - 1:rows, :]
        buf_out_ref[0] = xx_ref[pl.ds(8 - (CONV_W - 1), CONV_W - 1), :]


def _rg_sample_kernel(u_ref, h0_ref, buf_ref, cw_ref, cb_ref, wa_ref, ba_ref, wx_ref, bx_ref, lam_ref,
                      y_ref, h_out_ref, buf_out_ref):
    u = u_ref[...]
    xr, new_buf = _conv_taps_seq4(u[:, 0:GROUP_W], buf_ref[...], cw_ref, cb_ref)
    buf_out_ref[...] = new_buf
    a, b = _rg_gates(xr, wa_ref, ba_ref, wx_ref, bx_ref, lam_ref)
    a_cum, h_loc = _rg_scan(a, b, 4)
    hseq = h_loc + a_cum * h0_ref[...]
    h_out_ref[...] = hseq
    y_ref[...] = _gelu_tanh(u[:, GROUP_W:]) * hseq


def _rg(u, h0_rows, buf, cw, cb, wa, ba, wx, bx, lam, n_seq, seq_len):
    n = u.shape[0]
    vec = _const_spec((1, GROUP_W))
    common_in = [_const_spec(cw.shape), vec, _const_spec(wa.shape), vec, _const_spec(wx.shape), vec, vec]
    if h0_rows is None:
        nt = seq_len // MIX_TILE
        row = lambda wd: pl.BlockSpec((MIX_TILE, wd), lambda b, t: (b * nt + t, 0))
        return pl.pallas_call(
            _rg_prompt_kernel,
            grid=(n_seq, nt),
            in_specs=[row(u.shape[1])] + common_in,
            out_specs=[row(GROUP_W), pl.BlockSpec((1, 1, GROUP_W), lambda b, t: (b, 0, 0)),
                       pl.BlockSpec((1, CONV_W - 1, GROUP_W), lambda b, t: (b, 0, 0))],
            out_shape=[jax.ShapeDtypeStruct((n, GROUP_W), F32),
                       jax.ShapeDtypeStruct((n_seq, 1, GROUP_W), F32),
                       jax.ShapeDtypeStruct((n_seq, CONV_W - 1, GROUP_W), F32)],
            scratch_shapes=[pltpu.VMEM((8, GROUP_W), F32), pltpu.VMEM((MIX_TILE + 8, GROUP_W), F32)],
            compiler_params=_params(("parallel", "arbitrary")),
            name="rglru_prompt",
        )(u, cw, cb, wa, ba, wx, bx, lam)
    rows = SAMPLE_SEQS * seq_len
    row = lambda wd: pl.BlockSpec((rows, wd), lambda i: (i, 0))
    return pl.pallas_call(
        _rg_sample_kernel,
        grid=(n_seq // SAMPLE_SEQS,),
        in_specs=[row(u.shape[1]), row(GROUP_W), row(GROUP_W)] + common_in,
        out_specs=[row(GROUP_W)] * 3,
        out_shape=[jax.ShapeDtypeStruct((n, GROUP_W), F32)] * 3,
        compiler_params=_params(("parallel",)),
        name="rglru_sample",
    )(u, h0_rows, buf, cw, cb, wa, ba, wx, bx, lam)


def _hg_inputs(u_ref, lb_ref, seg_shift):
    u = u_ref[...]
    rows = u.shape[0]
    lb = lb_ref[...]
    fg = lb + (1.0 - lb) * _sigmoid(u[:, GROUP_W:2 * GROUP_W])
    q = _silu(u[:, 0:GROUP_W])
    k = 1.0 - fg
    v = u[:, 2 * GROUP_W:3 * GROUP_W]
    cum = _sel_dot(_seg_causal_mask(rows, seg_shift).astype(BF16), jnp.log(fg))
    return q, k, v, cum, u[:, 3 * GROUP_W:4 * GROUP_W]


def _hg_intra(q, k, v, cum, kk_ref, cc_ref, vv_ref, seg):
    rows = q.shape[0]
    pad = kk_ref.shape[0] - rows
    for ref, val in ((kk_ref, k), (cc_ref, cum), (vv_ref, v)):
        ref[0:pad, :] = jnp.zeros((pad, GROUP_W), F32)
        ref[pad:pad + rows, :] = val
    pos = _iota((rows, GROUP_W), 0) & (seg - 1)
    ones_bd = ((_iota((GROUP_W, GROUP_W), 0) >> 6) == (_iota((GROUP_W, GROUP_W), 1) >> 6)).astype(BF16)
    y = jnp.dot((q * k).astype(BF16), ones_bd, preferred_element_type=F32) * v
    for d in range(1, seg):
        k_d = kk_ref[pl.ds(pad - d, rows), :]
        c_d = cc_ref[pl.ds(pad - d, rows), :]
        v_d = vv_ref[pl.ds(pad - d, rows), :]
        p = jnp.where(pos >= d, q * k_d * jnp.exp(cum - c_d), 0.0)
        y = y + jnp.dot(p.astype(BF16), ones_bd, preferred_element_type=F32) * v_d
    return y


def _head_block_mask():
    return (_iota((GROUP_W, GROUP_W), 0) >> 6) == (_iota((GROUP_W, GROUP_W), 1) >> 6)


def _block_diag_rows(x, mask):
    return jnp.where(mask, jnp.concatenate([x] * N_HEADS, axis=0), 0.0)


def _hg_prompt_kernel(u_ref, lb_ref, g_ref, y_ref, s_out_ref, st_ref, y_scr):
    t = pl.program_id(1)
    rows = u_ref.shape[0]

    @pl.when(t == 0)
    def _():
        st_ref[...] = jnp.zeros(st_ref.shape, F32)

    q, k, v, cum, gate = _hg_inputs(u_ref, lb_ref, 6)
    hmask = _head_block_mask()
    ones_bd = hmask.astype(BF16)
    y_scr[...] = jnp.dot((q * k).astype(BF16), ones_bd, preferred_element_type=F32) * v
    row = _iota((rows, GROUP_W), 0)
    levels = []
    blk_last = cum
    h = 1
    while h < CHUNK:
        upper = (row & h) != 0
        qn = jnp.where(upper, q * jnp.exp(cum - pltpu.roll(blk_last, h, 0)), 0.0)
        kn = jnp.where(upper, 0.0, k * jnp.exp(blk_last - cum))
        levels.append((h, qn, kn))
        blk_last = jnp.where(upper, blk_last, pltpu.roll(blk_last, rows - h, 0))
        h *= 2
    q_in = q * jnp.exp(cum)
    k_end = k * jnp.exp(blk_last - cum)
    trow = _iota((CHUNK, GROUP_W), 0)
    tcol = _iota((CHUNK, GROUP_W), 1) & (CHUNK - 1)
    for c in range(rows // CHUNK):
        rs = slice(c * CHUNK, (c + 1) * CHUNK)
        a_wide = jnp.zeros((CHUNK, GROUP_W), F32)
        for h, qn, kn in levels:
            g = _bdot_nt(qn[rs], _block_diag_rows(kn[rs], hmask))
            sh = (2 * h).bit_length() - 1
            a_wide = a_wide + jnp.where((trow >> sh) == (tcol >> sh), g, 0.0)
        y_c = _bdot(a_wide, _block_diag_rows(v[rs], hmask)) + _bdot_nt(q_in[rs], st_ref[...])
        y_scr[rs, :] = y_scr[rs, :] + y_c
        ds_t = jnp.where(hmask, _bdot_tn(v[rs], k_end[rs]), 0.0)
        st_ref[...] = jnp.exp(blk_last[c * CHUNK:c * CHUNK + 1, :]) * st_ref[...] + ds_t
    y_ref[...] = _silu(gate) * (_head_stat(y_scr[...], False) * g_ref[...])

    @pl.when(t == pl.num_programs(1) - 1)
    def _():
        eye = (_iota((HEAD, HEAD), 0) == _iota((HEAD, HEAD), 1)).astype(BF16)
        for h in range(N_HEADS):
            s_out_ref[0, h] = _sel_dot_nt(eye, st_ref[h * HEAD:(h + 1) * HEAD, h * HEAD:(h + 1) * HEAD])


def _hg_sample_kernel(u_ref, s0_ref, lb_ref, g_ref, y_ref, s_out_ref, kk_ref, cc_ref, vv_ref, *, seqs):
    rows = u_ref.shape[0]
    q, k, v, cum, gate = _hg_inputs(u_ref, lb_ref, 2)
    y = _hg_intra(q, k, v, cum, kk_ref, cc_ref, vv_ref, 4)
    pick_last = (_iota((rows, rows), 1) == (_iota((rows, rows), 0) | 3)).astype(BF16)
    last = _sel_dot(pick_last, cum)
    qn = q * jnp.exp(cum)
    kend = k * jnp.exp(last - cum)
    elast = jnp.exp(last)
    row_seq = _iota((rows, HEAD), 0) >> 2
    eye = _iota((HEAD, HEAD), 0) == _iota((HEAD, HEAD), 1)
    y_int = []
    for h in range(N_HEADS):
        ls = slice(h * HEAD, (h + 1) * HEAD)
        acc = jnp.zeros((rows, HEAD), F32)
        for i in range(seqs):
            in_seq = row_seq == i
            s_prev = s0_ref[i, h]
            acc = jnp.where(in_seq, _bdot(qn[:, ls], s_prev), acc)
            ds = _bdot_tn(jnp.where(in_seq, kend[:, ls], 0.0), v[:, ls])
            a_col = jnp.sum(jnp.where(eye, elast[4 * i:4 * i + 1, ls], 0.0), axis=-1, keepdims=True)
            s_out_ref[i, h] = a_col * s_prev + ds
        y_int.append(acc)
    y = y + jnp.concatenate(y_int, axis=-1)
    y_ref[...] = _silu(gate) * (_head_stat(y, False) * g_ref[...])


def _hg(u, state, lb, gain, n_seq, seq_len, layer):
    n = u.shape[0]
    vec = _const_spec((1, GROUP_W))
    if state is None:
        nt = seq_len // MIX_TILE
        row = lambda wd: pl.BlockSpec((MIX_TILE, wd), lambda b, t: (b * nt + t, 0))
        return pl.pallas_call(
            _hg_prompt_kernel,
            grid=(n_seq, nt),
            in_specs=[row(u.shape[1]), vec, vec],
            out_specs=[row(GROUP_W), pl.BlockSpec((1, N_HEADS, HEAD, HEAD), lambda b, t: (b, 0, 0, 0))],
            out_shape=[jax.ShapeDtypeStruct((n, GROUP_W), F32),
                       jax.ShapeDtypeStruct((n_seq, N_HEADS, HEAD, HEAD), F32)],
            scratch_shapes=[pltpu.VMEM((GROUP_W, GROUP_W), F32), pltpu.VMEM((MIX_TILE, GROUP_W), F32)],
            compiler_params=_params(("parallel", "arbitrary")),
            name="hgrn_prompt",
        )(u, lb, gain)
    rows = SAMPLE_SEQS * seq_len
    row = lambda wd: pl.BlockSpec((rows, wd), lambda i: (i, 0))
    st = pl.BlockSpec((SAMPLE_SEQS, N_HEADS, HEAD, HEAD), lambda i: (i, 0, 0, 0))
    shifted = pltpu.VMEM((rows + 8, GROUP_W), F32)
    return pl.pallas_call(
        functools.partial(_hg_sample_kernel, seqs=SAMPLE_SEQS),
        grid=(n_seq // SAMPLE_SEQS,),
        in_specs=[row(u.shape[1]),
                  pl.BlockSpec((None, SAMPLE_SEQS, N_HEADS, HEAD, HEAD), lambda i: (layer, i, 0, 0, 0)), vec, vec],
        out_specs=[row(GROUP_W), st],
        out_shape=[jax.ShapeDtypeStruct((n, GROUP_W), F32), jax.ShapeDtypeStruct(state.shape[1:], F32)],
        scratch_shapes=[shifted, shifted, shifted],
        compiler_params=_params(("parallel",)),
        name="hgrn_sample",
    )(u, state, lb, gain)


def _block_diag(w):
    h, i, j = w.shape
    eye = jnp.eye(h, dtype=w.dtype)
    return (eye[:, None, :, None] * w[:, :, None, :]).reshape(h * i, h * j)


def _rope_tables(pos0, seq_len):
    half = HEAD // 2
    pos = pos0 + jnp.arange(seq_len, dtype=F32)
    inv = ROPE_BASE ** (-jnp.arange(half, dtype=F32) / half)
    ang = pos[:, None] * inv
    cos = jnp.tile(jnp.cos(ang), (1, 2 * N_HEADS))
    sin = jnp.sin(ang)
    sin_signed = jnp.tile(jnp.concatenate([-sin, sin], axis=-1), (1, N_HEADS))
    return cos, sin_signed


def _layer_weights(P, l):
    z0 = GROUP_W + (GROUP_W + 2 * SSD_G * SSD_N)
    row = lambda a: a.reshape(1, -1)
    pad128 = lambda a: jnp.pad(a, (0, 128 - a.shape[0])).reshape(1, 128)
    lb_sm = jax.nn.softmax(P['hg_lb'].astype(F32), axis=0)
    lb = (jnp.cumsum(lb_sm, axis=0) - lb_sm[0])[l]
    return dict(
        w_in=P['w_in_prepped'],
        in_widths=(z0, 4 * GROUP_W, 2 * GROUP_W, 4 * GROUP_W, 128),
        ln_mix_pre=row(P['ln_mix_pre'][l]), ln_mix_post=row(P['ln_mix_post'][l]),
        ln_xa_pre=row(P['ln_xa_pre'][l]), ln_xa_post=row(P['ln_xa_post'][l]),
        ln_ffn_pre=row(P['ln_ffn_pre'][l]), ln_ffn_post=row(P['ln_ffn_post'][l]),
        ssd_cw=P['ssd_conv_w'][l], ssd_cb=row(P['ssd_conv_b'][l]),
        ssd_dtb=pad128(P['ssd_dt_bias'][l]), ssd_alog=pad128(P['ssd_A_log'][l]),
        ssd_d=row(jnp.repeat(P['ssd_D'][l], HEAD)), ssd_norm=row(P['ssd_norm'][l]),
        ret_norm=row(P['ret_norm'][l]),
        rg_cw=P['rg_conv_w'][l], rg_cb=row(P['rg_conv_b'][l]),
        rg_wa=_block_diag(P['rg_wa'][l]).astype(BF16), rg_ba=row(P['rg_ba'][l]),
        rg_wx=_block_diag(P['rg_wx'][l]).astype(BF16), rg_bx=row(P['rg_bx'][l]),
        rg_lam=row(P['rg_lambda'][l]),
        hg_lb=row(lb), hg_norm=row(P['hg_norm'][l]),
        w_out=P['w_out'][l].astype(BF16), w_xq=P['w_xq'][l].astype(BF16),
        w_xo=P['w_xo'][l].astype(BF16), w_gu=P['w_gu'][l].astype(BF16),
        w_down=P['w_down'][l].astype(BF16),
        ln_mem=row(P['ln_mem'][l]), w_xkv=P['w_xkv'][l].astype(BF16),
    )


def _layer(x, W, k_mem, v_mem, st, n_seq, seq_len, pos0, layer):
    u_ssd, u_ret, u_rg, u_hg, u_dt = _norm_matmul(x, W['ln_mix_pre'], W['w_in'], W['in_widths'], "in_proj",
                                                  layer=layer)
    cos, sin_signed = _rope_tables(pos0, seq_len)
    if st is None:
        y_ssd, s_ssd, b_ssd = _ssd(u_ssd, u_dt, None, None, W['ssd_cw'], W['ssd_cb'], W['ssd_dtb'],
                                   W['ssd_alog'], W['ssd_d'], W['ssd_norm'], n_seq, seq_len, layer)
        y_ret, s_ret = _ret(u_ret, cos, sin_signed, None, W['ret_norm'], n_seq, seq_len, layer)
        y_rg, h_rg, b_rg = _rg(u_rg, None, None, W['rg_cw'], W['rg_cb'], W['rg_wa'], W['rg_ba'],
                               W['rg_wx'], W['rg_bx'], W['rg_lam'], n_seq, seq_len)
        h_rg = h_rg.reshape(n_seq, GROUP_W)
        y_hg, s_hg = _hg(u_hg, None, W['hg_lb'], W['hg_norm'], n_seq, seq_len, layer)
    else:
        ssd_s, ssd_buf, ret_s, rg_h, rg_buf, hg_s = st
        pad_rows = lambda b: jnp.pad(b, ((0, 0), (0, 1), (0, 0))).reshape(n_seq * 4, b.shape[-1])
        y_ssd, s_ssd, b_ssd = _ssd(u_ssd, u_dt, ssd_s, pad_rows(ssd_buf), W['ssd_cw'], W['ssd_cb'],
                                   W['ssd_dtb'], W['ssd_alog'], W['ssd_d'], W['ssd_norm'], n_seq, seq_len, layer)
        b_ssd = b_ssd.reshape(n_seq, 4, -1)[:, :CONV_W - 1]
        y_ret, s_ret = _ret(u_ret, cos, sin_signed, ret_s, W['ret_norm'], n_seq, seq_len, layer)
        y_rg, h_rows, b_rg = _rg(u_rg, jnp.repeat(rg_h, seq_len, axis=0), pad_rows(rg_buf), W['rg_cw'],
                                 W['rg_cb'], W['rg_wa'], W['rg_ba'], W['rg_wx'], W['rg_bx'], W['rg_lam'],
                                 n_seq, seq_len)
        h_rg = h_rows.reshape(n_seq, seq_len, GROUP_W)[:, seq_len - 1]
        b_rg = b_rg.reshape(n_seq, 4, -1)[:, :CONV_W - 1]
        y_hg, s_hg = _hg(u_hg, hg_s, W['hg_lb'], W['hg_norm'], n_seq, seq_len, layer)
    ys = (y_ssd, y_ret, y_rg, y_hg)
    if st is None:
        (x1,) = _out_proj(ys, x, W['w_out'], W['ln_mix_post'])
        x3 = _xattn_ffn(x1, k_mem, v_mem, W['ln_xa_pre'], W['w_xq'], W['w_xo'], W['ln_xa_post'],
                        W['ln_ffn_pre'], W['w_gu'], W['w_down'], W['ln_ffn_post'], seq_len)
    else:
        x1, q = _out_proj(ys, x, W['w_out'], W['ln_mix_post'], W['ln_xa_pre'], W['w_xq'])
        o = _attention_sample(q, k_mem, v_mem, layer, seq_len)
        x3 = _ffn(o, x1, W['w_xo'], W['ln_xa_post'], W['ln_ffn_pre'], W['w_gu'], W['w_down'],
                  W['ln_ffn_post'])
    return x3, (s_ssd, b_ssd, s_ret, h_rg, b_rg, s_hg)


def kernel(x_prompt, x_sample, state_ssd, state_ssd_conv, state_ret, state_rglru, state_rglru_conv, state_hgrn, cache_mem_k, cache_mem_v, mem_prompt, ln_mix_pre, ln_mix_post, ln_xa_pre, ln_xa_post, ln_ffn_pre, ln_ffn_post, w_in, ssd_conv_w, ssd_conv_b, ssd_dt_bias, ssd_A_log, ssd_D, ssd_norm, ret_norm, rg_conv_w, rg_conv_b, rg_wa, rg_ba, rg_wx, rg_bx, rg_lambda, hg_lb, hg_norm, w_out, ln_mem, w_xq, w_xkv, w_xo, w_gu, w_down):
    P = dict(ln_mix_pre=ln_mix_pre, ln_mix_post=ln_mix_post, ln_xa_pre=ln_xa_pre, ln_xa_post=ln_xa_post,
             ln_ffn_pre=ln_ffn_pre, ln_ffn_post=ln_ffn_post, w_in=w_in, ssd_conv_w=ssd_conv_w,
             ssd_conv_b=ssd_conv_b, ssd_dt_bias=ssd_dt_bias, ssd_A_log=ssd_A_log, ssd_D=ssd_D,
             ssd_norm=ssd_norm, ret_norm=ret_norm, rg_conv_w=rg_conv_w, rg_conv_b=rg_conv_b,
             rg_wa=rg_wa, rg_ba=rg_ba, rg_wx=rg_wx, rg_bx=rg_bx, rg_lambda=rg_lambda, hg_lb=hg_lb,
             hg_norm=hg_norm, w_out=w_out, ln_mem=ln_mem, w_xq=w_xq, w_xkv=w_xkv, w_xo=w_xo,
             w_gu=w_gu, w_down=w_down)
    P['w_in_prepped'] = _in_weight_prep(w_in)
    ssd_t = jnp.swapaxes(state_ssd, -1, -2)
    depth = w_in.shape[0]
    bp, tp, d = x_prompt.shape
    bs, ts, _ = x_sample.shape
    n_mem = mem_prompt.shape[1]
    y_p = x_prompt.reshape(bp * tp, d)
    y_s = x_sample.reshape(bs * ts, d)
    mem = mem_prompt.reshape(bp * n_mem, d)
    p_st, s_st, p_mk, p_mv = [], [], [], []
    for l in range(depth):
        W = _layer_weights(P, l)
        mk, mv = _norm_matmul(mem, W['ln_mem'], W['w_xkv'], (d, d), "mem_kv")
        mk = mk.reshape(bp, n_mem, d)
        mv = mv.reshape(bp, n_mem, d)
        y_p, st_p = _layer(y_p, W, mk, mv, None, bp, tp, 0.0, l)
        p_st.append(st_p)
        p_mk.append(mk.reshape(bp, n_mem, XA_H, XA_HD))
        p_mv.append(mv.reshape(bp, n_mem, XA_H, XA_HD))
        st_l = (ssd_t, state_ssd_conv[l], state_ret, state_rglru[l], state_rglru_conv[l], state_hgrn)
        y_s, st_s = _layer(y_s, W, cache_mem_k, cache_mem_v, st_l, bs, ts, float(PAST_LEN), l)
        s_st.append(st_s)
    stack = lambda sts, i: jnp.stack([s[i] for s in sts], axis=0)
    return (y_p.reshape(bp, tp, d), y_s.reshape(bs, ts, d),
            jnp.swapaxes(stack(p_st, 0), -1, -2), stack(p_st, 1), stack(p_st, 2), stack(p_st, 3), stack(p_st, 4),
            stack(p_st, 5),
            jnp.stack(p_mk, axis=0), jnp.stack(p_mv, axis=0),
            jnp.swapaxes(stack(s_st, 0), -1, -2), stack(s_st, 1), stack(s_st, 2), stack(s_st, 3), stack(s_st, 4),
            stack(s_st, 5))
```

```python
import functools
import math

import jax
import jax.numpy as jnp
from jax import lax
from jax.experimental import pallas as pl
from jax.experimental.pallas import tpu as pltpu

F32 = jnp.float32
BF16 = jnp.bfloat16
EPS = 1e-6

GROUP_W = 256
HEAD = 64
N_HEADS = 4
SSD_N = 128
SSD_G = 2
CONV_W = 4
XA_H = 4
XA_HD = 256
ROPE_BASE = 10000.0
RG_C = 8.0
CHUNK = 64
HG_CHUNK = 16
PAST_LEN = 16384

ROW_TILE = 512
MIX_TILE = 256
SAMPLE_SEQS = 16
ATT_TILE = 512
ATT_SEQS = 4
FF_TILE = 256
VMEM_LIMIT = 56 * 1024 * 1024


def _bdot(a, b):
    return jnp.dot(a.astype(BF16), b.astype(BF16), preferred_element_type=F32)


def _bdot_nt(a, b):
    return lax.dot_general(a.astype(BF16), b.astype(BF16), (((1,), (1,)), ((), ())),
                           preferred_element_type=F32)


def _bdot_tn(a, b):
    return lax.dot_general(a.astype(BF16), b.astype(BF16), (((0,), (0,)), ((), ())),
                           preferred_element_type=F32)


def _split3(x):
    hi = x.astype(BF16)
    r = x - hi.astype(F32)
    mid = r.astype(BF16)
    lo = (r - mid.astype(F32)).astype(BF16)
    return hi, mid, lo


def _sel_dot(sel, x):
    hi, mid, lo = _split3(x)
    d = lambda y: jnp.dot(sel, y, preferred_element_type=F32)
    return (d(hi) + d(mid)) + d(lo)


def _dot_sel(x, sel):
    hi, mid, lo = _split3(x)
    d = lambda y: jnp.dot(y, sel, preferred_element_type=F32)
    return (d(hi) + d(mid)) + d(lo)


def _sel_dot_nt(sel, x):
    hi, mid, lo = _split3(x)
    d = lambda y: lax.dot_general(sel, y, (((1,), (1,)), ((), ())), preferred_element_type=F32)
    return (d(hi) + d(mid)) + d(lo)


def _rms(x, g):
    return x * lax.rsqrt(jnp.mean(x * x, axis=-1, keepdims=True) + EPS) * g


def _sigmoid(x):
    return jax.nn.sigmoid(x)


def _silu(x):
    return x * jax.nn.sigmoid(x)


def _softplus(x):
    return jnp.maximum(x, 0.0) + jnp.log1p(jnp.exp(-jnp.abs(x)))


def _gelu_tanh(x):
    c = math.sqrt(2.0 / math.pi)
    return 0.5 * x * (1.0 + jnp.tanh(c * (x + 0.044715 * (x * x * x))))


def _iota(shape, dim):
    return lax.broadcasted_iota(jnp.int32, shape, dim)


def _seg_causal_mask(n, seg_shift):
    r = _iota((n, n), 0)
    c = _iota((n, n), 1)
    return ((r >> seg_shift) == (c >> seg_shift)) & (c <= r)


def _head_expand_mat(width_in=128):
    r = _iota((width_in, GROUP_W), 0)
    c = _iota((width_in, GROUP_W), 1)
    return ((c >> 6) == r).astype(BF16)


def _head_block_mask():
    return (_iota((GROUP_W, GROUP_W), 0) >> 6) == (_iota((GROUP_W, GROUP_W), 1) >> 6)


def _block_diag_rows(x, mask01):
    return jnp.concatenate([x.astype(BF16)] * N_HEADS, axis=0) * mask01


def _head_stat(y, center):
    ones_bd = _head_block_mask().astype(BF16)
    if center:
        y = y - _dot_sel(y, ones_bd) * (1.0 / HEAD)
    return y * lax.rsqrt(_dot_sel(y * y, ones_bd) * (1.0 / HEAD) + EPS)


def _conv_taps_carry(x, xx_ref, w_ref, b_ref, first):
    rows = x.shape[0]

    @pl.when(first)
    def _():
        xx_ref[0:8, :] = jnp.zeros((8, x.shape[1]), F32)

    xx_ref[8:8 + rows, :] = x
    y = b_ref[...] + w_ref[CONV_W - 1:CONV_W, :] * x
    for j in range(CONV_W - 1):
        y = y + w_ref[j:j + 1, :] * xx_ref[pl.ds(8 - (CONV_W - 1) + j, rows), :]
    xx_ref[0:8, :] = xx_ref[rows:rows + 8, :]
    return y


def _conv_taps_seq4(x, buf4, w_ref, b_ref):
    rows = x.shape[0]
    r = _iota((rows, rows), 0)
    c = _iota((rows, rows), 1)
    t = r & 3
    y = b_ref[...] + w_ref[CONV_W - 1:CONV_W, :] * x
    for d in range(1, CONV_W):
        shift = ((c == r - d) & (t >= d)).astype(BF16)
        hist = ((t < d) & (c == (r - t) + (3 + t - d))).astype(BF16)
        y = y + w_ref[CONV_W - 1 - d:CONV_W - d, :] * (_sel_dot(shift, x) + _sel_dot(hist, buf4))
    new_sel = (((r & 3) < 3) & (c == r + 1)).astype(BF16)
    return y, _sel_dot(new_sel, x)


def _norm_matmul_kernel(x_ref, g_ref, w_ref, *o_refs, col_ranges):
    h = _rms(x_ref[...], g_ref[...]).astype(BF16)
    for o_ref, pieces in zip(o_refs, col_ranges):
        off = 0
        for a, b in pieces:
            o_ref[:, off:off + b - a] = jnp.dot(h, w_ref[:, a:b], preferred_element_type=F32)
            off += b - a


def _const_spec(shape):
    nd = len(shape)
    return pl.BlockSpec(shape, lambda *_: (0,) * nd, pipeline_mode=pl.Buffered(1))


def _layer_spec(shape, layer):
    nd = len(shape)
    return pl.BlockSpec((None,) + tuple(shape[1:]), lambda *_: (layer,) + (0,) * (nd - 1),
                        pipeline_mode=pl.Buffered(1))


def _params(sem):
    return pltpu.CompilerParams(dimension_semantics=sem, vmem_limit_bytes=VMEM_LIMIT)


def _in_weight_prep_kernel(wt_ref, o_ref, *, blocks, dt_block, dt_row, dt_n):
    for j, pieces in enumerate(blocks):
        if j == dt_block:
            dt = wt_ref[pl.ds(dt_row, 8), :]
            dt = jnp.where(_iota(dt.shape, 0) < dt_n, dt, 0.0)
            blk = jnp.concatenate([dt, jnp.zeros((120, dt.shape[1]), F32)], axis=0)
        else:
            blk = jnp.concatenate([wt_ref[pl.ds(src, n), :] for src, n in pieces], axis=0)
        o_ref[:, j * 128:(j + 1) * 128] = blk.T.astype(BF16)


IN_COLS_SSD = ((0, 1024),)
IN_COLS_RET = ((1024, 2048),)
IN_COLS_RG = ((2048, 2560),)
IN_COLS_HG = ((2560, 3584),)
IN_COLS_DT = ((3584, 3712),)
IN_COLS_RET_SPLIT = ((3712, 4224), (1536, 2048))


def _in_weight_prep(w_in):
    depth, d, d_in = w_in.shape
    z0 = GROUP_W + (GROUP_W + 2 * SSD_G * SSD_N)
    d0 = z0 + N_HEADS
    blocks = [((src, 128),) for src in tuple(range(0, z0, 128)) + tuple(range(d0, d_in, 128))]
    dt_block = len(blocks)
    blocks.append(())
    half = HEAD // 2
    for base in (d0, d0 + GROUP_W):
        for j in range(2):
            blocks.append(tuple((base + h * HEAD + j * half, half) for h in range(N_HEADS)))
    n_out = len(blocks) * 128
    wt = jnp.swapaxes(w_in, 1, 2)
    return pl.pallas_call(
        functools.partial(_in_weight_prep_kernel, blocks=tuple(blocks), dt_block=dt_block, dt_row=z0,
                          dt_n=N_HEADS),
        grid=(depth,),
        in_specs=[pl.BlockSpec((None, d_in, d), lambda l: (l, 0, 0))],
        out_specs=pl.BlockSpec((None, d, n_out), lambda l: (l, 0, 0)),
        out_shape=jax.ShapeDtypeStruct((depth, d, n_out), BF16),
        compiler_params=_params(("parallel",)),
        name="in_weight_prep",
    )(wt)


def _wspec(w, layer):
    return _layer_spec(w.shape, layer) if w.ndim == 3 else _const_spec(w.shape)


def _norm_matmul(x, g, w, col_ranges, name, layer=None):
    n, d = x.shape
    tm = min(ROW_TILE, n)
    w_spec = _wspec(w, layer)
    widths = [sum(b - a for a, b in pieces) for pieces in col_ranges]
    return pl.pallas_call(
        functools.partial(_norm_matmul_kernel, col_ranges=col_ranges),
        grid=(n // tm,),
        in_specs=[pl.BlockSpec((tm, d), lambda i: (i, 0)), _const_spec((1, d)), w_spec],
        out_specs=[pl.BlockSpec((tm, wd), lambda i: (i, 0)) for wd in widths],
        out_shape=[jax.ShapeDtypeStruct((n, wd), F32) for wd in widths],
        compiler_params=_params(("parallel",)),
        name=name,
    )(x, g, w)


def _out_proj_kernel(ys_ref, yr_ref, yg_ref, yh_ref, x_ref, wo_ref, gpost_ref, gpre_ref, wq_ref, x1_ref, q_ref):
    x1 = _mix_out(ys_ref, yr_ref, yg_ref, yh_ref, x_ref, wo_ref, gpost_ref)
    x1_ref[...] = x1
    q_ref[...] = _bdot(_rms(x1, gpre_ref[...]), wq_ref[...])


def _out_proj(ys, x, w_out, g_post, g_pre, w_xq, layer):
    n, d = x.shape
    tm = min(ROW_TILE, n)
    row = lambda wd: pl.BlockSpec((tm, wd), lambda i: (i, 0))
    vec = _const_spec((1, d))
    return pl.pallas_call(
        _out_proj_kernel,
        grid=(n // tm,),
        in_specs=[row(GROUP_W)] * 4 + [row(d), _wspec(w_out, layer), vec, vec, _wspec(w_xq, layer)],
        out_specs=[row(d)] * 2,
        out_shape=[jax.ShapeDtypeStruct((n, d), F32)] * 2,
        compiler_params=_params(("parallel",)),
        name="out_proj",
    )(*ys, x, w_out, g_post, g_pre, w_xq)


def _ffn_tail(o, x1, wxo_ref, gxa_ref, gpre_ref, wgu_ref, wdn_ref, gpost_ref, d_ff):
    a = _bdot(o, wxo_ref[...])
    x2 = x1 + _rms(a, gxa_ref[...])
    h = _rms(x2, gpre_ref[...]).astype(BF16)
    acc = jnp.zeros(x2.shape, F32)
    for j in range(0, d_ff, FF_TILE):
        g = jnp.dot(h, wgu_ref[:, j:j + FF_TILE], preferred_element_type=F32)
        u = jnp.dot(h, wgu_ref[:, d_ff + j:d_ff + j + FF_TILE], preferred_element_type=F32)
        act = (_silu(g) * u).astype(BF16)
        acc = acc + jnp.dot(act, wdn_ref[j:j + FF_TILE, :], preferred_element_type=F32)
    return x2 + _rms(acc, gpost_ref[...])


def _ffn_kernel(o_ref, x1_ref, wxo_ref, gxa_ref, gpre_ref, wgu_ref, wdn_ref, gpost_ref, x3_ref, *, d_ff):
    x3_ref[...] = _ffn_tail(o_ref[...], x1_ref[...], wxo_ref, gxa_ref, gpre_ref, wgu_ref, wdn_ref, gpost_ref,
                            d_ff)


def _ffn(o, x1, w_xo, g_xa, g_pre, w_gu, w_down, g_post, layer):
    n, d = x1.shape
    d_ff = w_down.shape[-2]
    tm = min(ROW_TILE, n)
    row = pl.BlockSpec((tm, d), lambda i: (i, 0))
    vec = _const_spec((1, d))
    return pl.pallas_call(
        functools.partial(_ffn_kernel, d_ff=d_ff),
        grid=(n // tm,),
        in_specs=[row, row, _wspec(w_xo, layer), vec, vec, _wspec(w_gu, layer), _wspec(w_down, layer), vec],
        out_specs=row,
        out_shape=jax.ShapeDtypeStruct((n, d), F32),
        compiler_params=_params(("parallel",)),
        name="attn_out_ffn",
    )(o, x1, w_xo, g_xa, g_pre, w_gu, w_down, g_post)


def _mix_out(ys_ref, yr_ref, yg_ref, yh_ref, x_ref, wo_ref, gpost_ref):
    mix = _bdot(ys_ref[...], wo_ref[0:GROUP_W, :])
    mix = mix + _bdot(yr_ref[...], wo_ref[GROUP_W:2 * GROUP_W, :])
    mix = mix + _bdot(yg_ref[...], wo_ref[2 * GROUP_W:3 * GROUP_W, :])
    mix = mix + _bdot(yh_ref[...], wo_ref[3 * GROUP_W:4 * GROUP_W, :])
    return x_ref[...] + _rms(mix, gpost_ref[...])


def _xattn_ffn_kernel(ys_ref, yr_ref, yg_ref, yh_ref, x_ref, wo_ref, gmix_ref, k_ref, v_ref, gq_ref, wq_ref,
                      wxo_ref, gxa_ref, gpre_ref, wgu_ref, wdn_ref, gpost_ref, x3_ref, o_scr, *, d_ff):
    x1 = _mix_out(ys_ref, yr_ref, yg_ref, yh_ref, x_ref, wo_ref, gmix_ref)
    q = _bdot(_rms(x1, gq_ref[...]), wq_ref[...])
    scale = XA_HD ** -0.5
    for h in range(XA_H):
        sl = slice(h * XA_HD, (h + 1) * XA_HD)
        s = _bdot_nt(q[:, sl], k_ref[0, :, sl]) * scale
        o_scr[:, sl] = _softmax_pv(s, v_ref[0, :, sl])
    x3_ref[...] = _ffn_tail(o_scr[...], x1, wxo_ref, gxa_ref, gpre_ref, wgu_ref, wdn_ref, gpost_ref, d_ff)


def _xattn_ffn(ys, x, w_out, g_mix, k, v, g_q, w_xq, w_xo, g_xa, g_pre, w_gu, w_down, g_post, seq_len, layer):
    n, d = x.shape
    n_seq, n_mem, _ = k.shape
    d_ff = w_down.shape[-2]
    nt = seq_len // ROW_TILE
    row = lambda wd: pl.BlockSpec((ROW_TILE, wd), lambda b, t: (b * nt + t, 0))
    kv = pl.BlockSpec((1, n_mem, d), lambda b, t: (b, 0, 0))
    vec = _const_spec((1, d))
    return pl.pallas_call(
        functools.partial(_xattn_ffn_kernel, d_ff=d_ff),
        grid=(n_seq, nt),
        in_specs=[row(GROUP_W)] * 4 + [row(d), _wspec(w_out, layer), vec, kv, kv, vec, _wspec(w_xq, layer),
                                       _wspec(w_xo, layer), vec, vec, _wspec(w_gu, layer),
                                       _wspec(w_down, layer), vec],
        out_specs=row(d),
        out_shape=jax.ShapeDtypeStruct((n, d), F32),
        scratch_shapes=[pltpu.VMEM((ROW_TILE, d), F32)],
        compiler_params=_params(("parallel", "parallel")),
        name="xattn_ffn",
    )(*ys, x, w_out, g_mix, k, v, g_q, w_xq, w_xo, g_xa, g_pre, w_gu, w_down, g_post)


def _softmax_pv(s, v):
    m = jnp.max(s, axis=-1, keepdims=True)
    p = jnp.exp(s - m)
    return _bdot(p, v) / jnp.sum(p, axis=-1, keepdims=True)


def _attn_sample_kernel(q_ref, k_ref, v_ref, o_ref, *, seqs, seg):
    scale = XA_HD ** -0.5
    rows = seqs * seg
    n_mem = k_ref.shape[2]
    wq = jnp.concatenate([q_ref[:, h * XA_HD:(h + 1) * XA_HD] for h in range(XA_H)], axis=0)
    wshape = (XA_H * rows, n_mem * XA_H)
    valid = (_iota(wshape, 0) >> (rows.bit_length() - 1)) == (_iota(wshape, 1) & (XA_H - 1))
    row_seq = _iota((rows, XA_HD), 0) >> (seg.bit_length() - 1)
    accs = [jnp.zeros((rows, XA_HD), F32) for _ in range(XA_H)]
    for i in range(seqs):
        k2 = k_ref[0, i].reshape(n_mem * XA_H, XA_HD)
        v2 = v_ref[0, i].reshape(n_mem * XA_H, XA_HD)
        s = jnp.where(valid, _bdot_nt(wq, k2) * scale, -jnp.inf)
        o = _softmax_pv(s, v2)
        for h in range(XA_H):
            accs[h] = jnp.where(row_seq == i, o[h * rows:(h + 1) * rows], accs[h])
    for h in range(XA_H):
        o_ref[:, h * XA_HD:(h + 1) * XA_HD] = accs[h]


def _attention_sample(q, cache_k, cache_v, layer, seq_len):
    n, d = q.shape
    _, n_seq, n_mem, heads, hd = cache_k.shape
    kv = pl.BlockSpec((1, ATT_SEQS, n_mem, heads, hd), lambda i: (layer, i, 0, 0, 0))
    row = pl.BlockSpec((ATT_SEQS * seq_len, d), lambda i: (i, 0))
    return pl.pallas_call(
        functools.partial(_attn_sample_kernel, seqs=ATT_SEQS, seg=seq_len),
        grid=(n_seq // ATT_SEQS,),
        in_specs=[row, kv, kv],
        out_specs=row,
        out_shape=jax.ShapeDtypeStruct((n, d), F32),
        compiler_params=_params(("parallel",)),
        name="xattn_sample",
    )(q, cache_k, cache_v)


def _ssd_intra(q, k, v, cum_x, cum_t, mask):
    sc = _bdot_nt(q, k)
    outs = []
    for hh in range(2):
        col = cum_x[:, hh * HEAD:(hh + 1) * HEAD]
        dec = jnp.exp(jnp.where(mask, col - cum_t[hh:hh + 1, :], -jnp.inf))
        outs.append(_bdot(sc * dec, v[:, hh * HEAD:(hh + 1) * HEAD]))
    return jnp.concatenate(outs, axis=-1)


def _ssd_common(u, dtraw_ref, dtb_ref, alog_ref, conv, mask_cum):
    rows = u.shape[0]
    xbc = _silu(conv)
    xs = xbc[:, 0:GROUP_W]
    bm = xbc[:, GROUP_W:2 * GROUP_W]
    cm = xbc[:, 2 * GROUP_W:3 * GROUP_W]
    dt = _softplus(dtraw_ref[...] + dtb_ref[...])
    la = -jnp.exp(alog_ref[...]) * dt
    cum = _sel_dot(mask_cum, la)
    expand = _head_expand_mat()
    dt_x = _dot_sel(dt, expand)
    cum_x = _dot_sel(cum, expand)
    sel8 = (_iota((8, 128), 0) == _iota((8, 128), 1)).astype(BF16)
    cum_t = _sel_dot_nt(sel8, cum)
    expand_full = ((_iota((128, 4 * 128), 1) >> 7) == _iota((128, 4 * 128), 0)).astype(BF16)
    cum_full = _dot_sel(cum, expand_full)
    return xs, bm, cm, dt_x, cum_x, cum_t, cum_full


def _ssd_finish(y, xs, z, d_ref, g_ref):
    y = y + d_ref[...] * xs
    return _rms(y * _silu(z), g_ref[...])


def _ssd_prompt_kernel(u_ref, dtraw_ref, cw_ref, cb_ref, dtb_ref, alog_ref, d_ref, g_ref,
                       y_ref, s_out_ref, buf_out_ref, s_ref, xx_ref, y_scr):
    t = pl.program_id(1)
    rows = u_ref.shape[0]

    @pl.when(t == 0)
    def _():
        s_ref[...] = jnp.zeros(s_ref.shape, F32)

    u = u_ref[...]
    conv = _conv_taps_carry(u[:, GROUP_W:], xx_ref, cw_ref, cb_ref, t == 0)
    mask_cum = _seg_causal_mask(rows, 6).astype(BF16)
    xs, bm, cm, dt_x, cum_x, _, cum_full = _ssd_common(u, dtraw_ref, dtb_ref, alog_ref, conv, mask_cum)
    v = xs * dt_x
    ecum_x = jnp.exp(cum_x)
    hmask01 = _head_block_mask().astype(BF16)
    gmask01 = ((_iota((GROUP_W, GROUP_W), 0) >> 7) == (_iota((GROUP_W, GROUP_W), 1) >> 7)).astype(BF16)
    wrow = _iota((CHUNK, GROUP_W), 0)
    wcol = _iota((CHUNK, GROUP_W), 1) & (CHUNK - 1)
    for c in range(rows // CHUNK):
        rs = slice(c * CHUNK, (c + 1) * CHUNK)
        cum_c = cum_x[rs]
        last_x = cum_x[c * CHUNK + CHUNK - 1:(c + 1) * CHUNK, :]
        vend = v[rs] * jnp.exp(last_x - cum_c)
        cum_row = jnp.sum(jnp.where(wrow == wcol, cum_c, 0.0), axis=0, keepdims=True)
        dec = jnp.exp(jnp.where(wcol <= wrow, cum_c - cum_row, -jnp.inf))
        y_intra = _bdot(_bdot_nt(cm[rs], _block_diag_rows(bm[rs], gmask01)) * dec,
                        _block_diag_rows(v[rs], hmask01))
        last_full = cum_full[c * CHUNK + CHUNK - 1:(c + 1) * CHUNK, :]
        y_inter = []
        for g in range(SSD_G):
            ls = slice(g * 128, (g + 1) * 128)
            s_prev = s_ref[ls, :]
            y_inter.append(_bdot_nt(cm[rs, ls], s_prev))
            decay = jnp.concatenate(
                [jnp.broadcast_to(jnp.exp(last_full[:, h * 128:(h + 1) * 128]), (HEAD, 128))
                 for h in (2 * g, 2 * g + 1)], axis=0)
            s_ref[ls, :] = decay * s_prev + _bdot_tn(vend[:, ls], bm[rs, ls])
        y_scr[rs, :] = y_intra + jnp.concatenate(y_inter, axis=-1) * ecum_x[rs]
    y_ref[...] = _ssd_finish(y_scr[...], xs, u[:, 0:GROUP_W], d_ref, g_ref)

    @pl.when(t == pl.num_programs(1) - 1)
    def _():
        for h in range(N_HEADS):
            s_out_ref[0, h] = s_ref[h * HEAD:(h + 1) * HEAD, :]
        buf_out_ref[0] = xx_ref[pl.ds(8 - (CONV_W - 1), CONV_W - 1), :]


def _ssd_sample_kernel(u_ref, dtraw_ref, s0_ref, buf_ref, cw_ref, cb_ref, dtb_ref, alog_ref, d_ref,
                       g_ref, y_ref, s_out_ref, buf_out_ref, *, seqs):
    rows = u_ref.shape[0]
    u = u_ref[...]
    conv, new_buf = _conv_taps_seq4(u[:, GROUP_W:], buf_ref[...], cw_ref, cb_ref)
    buf_out_ref[...] = new_buf
    mask = _seg_causal_mask(rows, 2)
    xs, bm, cm, dt_x, cum_x, cum_t, cum_full = _ssd_common(u, dtraw_ref, dtb_ref, alog_ref, conv,
                                                           mask.astype(BF16))
    v = xs * dt_x
    ecum_x = jnp.exp(cum_x)
    pick_last = (_iota((rows, rows), 1) == (_iota((rows, rows), 0) | 3)).astype(BF16)
    last_x = _sel_dot(pick_last, cum_x)
    vend = v * jnp.exp(last_x - cum_x)
    row_seq = _iota((rows, 128), 0) >> 2
    ys = []
    for g in range(SSD_G):
        ls = slice(g * 128, (g + 1) * 128)
        q = cm[:, ls]
        k = bm[:, ls]
        y_g = _ssd_intra(q, k, v[:, ls], cum_x[:, ls], cum_t[2 * g:2 * g + 2, :], mask)
        y_int = jnp.zeros((rows, 128), F32)
        for i in range(seqs):
            in_seq = row_seq == i
            s_prev = jnp.concatenate([s0_ref[i, 2 * g], s0_ref[i, 2 * g + 1]], axis=0)
            y_int = jnp.where(in_seq, _bdot_nt(q, s_prev), y_int)
            ds = _bdot_tn(jnp.where(in_seq, vend[:, ls], 0.0), k)
            for hh in range(2):
                h = 2 * g + hh
                decay = jnp.exp(cum_full[4 * i + 3:4 * i + 4, h * 128:(h + 1) * 128])
                s_out_ref[i, h] = decay * s0_ref[i, h] + ds[hh * HEAD:(hh + 1) * HEAD, :]
        ys.append(y_g + y_int * ecum_x[:, ls])
    y_ref[...] = _ssd_finish(jnp.concatenate(ys, axis=-1), xs, u[:, 0:GROUP_W], d_ref, g_ref)


def _ssd(u, dtraw, state, buf, cw, cb, dtb, alog, d_x, gain, n_seq, seq_len, layer):
    n = u.shape[0]
    cch = cw.shape[1]
    vec = lambda wd: _const_spec((1, wd))
    common_in = [_const_spec(cw.shape), vec(cch), vec(128), vec(128), vec(GROUP_W), vec(GROUP_W)]
    if state is None:
        nt = seq_len // MIX_TILE
        row = lambda wd: pl.BlockSpec((MIX_TILE, wd), lambda b, t: (b * nt + t, 0))
        return pl.pallas_call(
            _ssd_prompt_kernel,
            grid=(n_seq, nt),
            in_specs=[row(u.shape[1]), row(128)] + common_in,
            out_specs=[row(GROUP_W),
                       pl.BlockSpec((1, N_HEADS, HEAD, SSD_N), lambda b, t: (b, 0, 0, 0)),
                       pl.BlockSpec((1, CONV_W - 1, cch), lambda b, t: (b, 0, 0))],
            out_shape=[jax.ShapeDtypeStruct((n, GROUP_W), F32),
                       jax.ShapeDtypeStruct((n_seq, N_HEADS, HEAD, SSD_N), F32),
                       jax.ShapeDtypeStruct((n_seq, CONV_W - 1, cch), F32)],
            scratch_shapes=[pltpu.VMEM((GROUP_W, SSD_N), F32), pltpu.VMEM((MIX_TILE + 8, cch), F32),
                            pltpu.VMEM((MIX_TILE, GROUP_W), F32)],
            compiler_params=_params(("parallel", "arbitrary")),
            name="ssd_prompt",
        )(u, dtraw, cw, cb, dtb, alog, d_x, gain)
    rows = SAMPLE_SEQS * seq_len
    row = lambda wd: pl.BlockSpec((rows, wd), lambda i: (i, 0))
    st = pl.BlockSpec((SAMPLE_SEQS, N_HEADS, HEAD, SSD_N), lambda i: (i, 0, 0, 0))
    st_in = pl.BlockSpec((None, SAMPLE_SEQS, N_HEADS, HEAD, SSD_N), lambda i: (layer, i, 0, 0, 0))
    return pl.pallas_call(
        functools.partial(_ssd_sample_kernel, seqs=SAMPLE_SEQS),
        grid=(n_seq // SAMPLE_SEQS,),
        in_specs=[row(u.shape[1]), row(128), st_in, row(cch)] + common_in,
        out_specs=[row(GROUP_W), st, row(cch)],
        out_shape=[jax.ShapeDtypeStruct((n, GROUP_W), F32),
                   jax.ShapeDtypeStruct(state.shape[1:], F32),
                   jax.ShapeDtypeStruct((n, cch), F32)],
        compiler_params=_params(("parallel",)),
        name="ssd_sample",
    )(u, dtraw, state, buf, cw, cb, dtb, alog, d_x, gain)


def _rope(x, cos, sin_signed):
    lane = _iota(x.shape, 1)
    swapped = jnp.where((lane & 63) < 32, pltpu.roll(x, GROUP_W - 32, 1), pltpu.roll(x, 32, 1))
    return x * cos + swapped * sin_signed


def _rope_split(x, cos, sin_signed):
    swapped = jnp.concatenate([x[:, 128:], x[:, :128]], axis=1)
    return x * cos + swapped * sin_signed


def _ret_qkv(u_ref, cos_ref, sin_ref, rope=_rope):
    u = u_ref[...]
    q = rope(u[:, 0:GROUP_W], cos_ref[...], sin_ref[...])
    k = rope(u[:, GROUP_W:2 * GROUP_W], cos_ref[...], sin_ref[...]) * (HEAD ** -0.5)
    return q, k, u[:, 2 * GROUP_W:3 * GROUP_W], u[:, 3 * GROUP_W:4 * GROUP_W]


def _ret_prompt_kernel(u_ref, cos_ref, sin_ref, ecum_ref, eend_ref, elast_ref, dec_ref, g_ref,
                       y_ref, s_out_ref, s_ref, y_scr):
    t = pl.program_id(1)
    rows = u_ref.shape[0]

    @pl.when(t == 0)
    def _():
        s_ref[...] = jnp.zeros(s_ref.shape, F32)

    q, k, v, gate = _ret_qkv(u_ref, cos_ref, sin_ref, _rope_split)
    r_i = _iota((GROUP_W, GROUP_W), 0)
    c_i = _iota((GROUP_W, GROUP_W), 1)
    kmask01 = ((r_i >> 6) == ((c_i & 127) >> 5)).astype(BF16)
    vmask01 = _head_block_mask().astype(BF16)
    smask = ((r_i & 127) >> 5) == (c_i >> 6)
    for c in range(rows // CHUNK):
        rs = slice(c * CHUNK, (c + 1) * CHUNK)
        a_wide = _bdot_nt(q[rs], _block_diag_rows(k[rs], kmask01)) * dec_ref[...]
        y_intra = _bdot(a_wide, _block_diag_rows(v[rs], vmask01))
        y_scr[rs, :] = y_intra + _bdot(q[rs], s_ref[...]) * ecum_ref[...]
        ds = jnp.where(smask, _bdot_tn(k[rs], v[rs] * eend_ref[...]), 0.0)
        s_ref[...] = elast_ref[...] * s_ref[...] + ds
    y_ref[...] = _silu(gate) * (_head_stat(y_scr[...], True) * g_ref[...])

    @pl.when(t == pl.num_programs(1) - 1)
    def _():
        half = HEAD // 2
        for h in range(N_HEADS):
            vs = slice(h * HEAD, (h + 1) * HEAD)
            s_out_ref[0, h, 0:half, :] = s_ref[h * half:(h + 1) * half, vs]
            s_out_ref[0, h, half:HEAD, :] = s_ref[128 + h * half:128 + (h + 1) * half, vs]


def _ret_sample_kernel(u_ref, cos_ref, sin_ref, s0_ref, ecum_ref, eend_ref, elast_ref, dec_ref, g_ref,
                       y_ref, s_out_ref, *, seqs):
    rows = u_ref.shape[0]
    q, k, v, gate = _ret_qkv(u_ref, cos_ref, sin_ref)
    vend = v * eend_ref[...]
    row_seq = _iota((rows, HEAD), 0) >> 2
    ys = []
    for h in range(N_HEADS):
        ls = slice(h * HEAD, (h + 1) * HEAD)
        qh = q[:, ls]
        kh = k[:, ls]
        y_h = _bdot(_bdot_nt(qh, kh) * dec_ref[h], v[:, ls])
        y_int = jnp.zeros((rows, HEAD), F32)
        for i in range(seqs):
            in_seq = row_seq == i
            s_prev = s0_ref[i, h]
            y_int = jnp.where(in_seq, _bdot(qh, s_prev), y_int)
            ds = _bdot_tn(kh, jnp.where(in_seq, vend[:, ls], 0.0))
            s_out_ref[i, h] = elast_ref[:, ls] * s_prev + ds
        ys.append(y_h + y_int * ecum_ref[:, ls])
    y_ref[...] = _silu(gate) * (_head_stat(jnp.concatenate(ys, axis=-1), True) * g_ref[...])


def _ret_tables(block, seg):
    log_gamma = jnp.log(1.0 - jnp.exp2(-5.0 - jnp.arange(N_HEADS, dtype=F32)))
    pos = (jnp.arange(block) % seg).astype(F32)
    cum = (pos[:, None] + 1.0) * log_gamma[None, :]
    last = seg * log_gamma
    rep = lambda a: jnp.repeat(a, HEAD, axis=-1)
    same = (jnp.arange(block)[:, None] // seg) == (jnp.arange(block)[None, :] // seg)
    mask = same & (jnp.arange(block)[None, :] <= jnp.arange(block)[:, None])
    dec = jnp.exp(jnp.where(mask[None], cum.T[:, :, None] - cum.T[:, None, :], -jnp.inf))
    return (rep(jnp.exp(cum)), rep(jnp.exp(last[None, :] - cum)), rep(jnp.exp(last)[None, :]), dec)


def _ret(u, cos, sin_signed, state, gain, n_seq, seq_len, layer):
    n = u.shape[0]
    vec = _const_spec((1, GROUP_W))
    if state is None:
        nt = seq_len // MIX_TILE
        ecum, eend, elast, dec = _ret_tables(CHUNK, CHUNK)
        dec = dec.transpose(1, 0, 2).reshape(CHUNK, N_HEADS * CHUNK)
        row = lambda wd: pl.BlockSpec((MIX_TILE, wd), lambda b, t: (b * nt + t, 0))
        pos = pl.BlockSpec((MIX_TILE, GROUP_W), lambda b, t: (t, 0))
        return pl.pallas_call(
            _ret_prompt_kernel,
            grid=(n_seq, nt),
            in_specs=[row(u.shape[1]), pos, pos, _const_spec(ecum.shape), _const_spec(eend.shape), vec,
                      _const_spec(dec.shape), vec],
            out_specs=[row(GROUP_W), pl.BlockSpec((1, N_HEADS, HEAD, HEAD), lambda b, t: (b, 0, 0, 0))],
            out_shape=[jax.ShapeDtypeStruct((n, GROUP_W), F32),
                       jax.ShapeDtypeStruct((n_seq, N_HEADS, HEAD, HEAD), F32)],
            scratch_shapes=[pltpu.VMEM((GROUP_W, GROUP_W), F32), pltpu.VMEM((MIX_TILE, GROUP_W), F32)],
            compiler_params=_params(("parallel", "arbitrary")),
            name="ret_prompt",
        )(u, cos, sin_signed, ecum, eend, elast, dec, gain)
    rows = SAMPLE_SEQS * seq_len
    ecum, eend, elast, dec = _ret_tables(rows, seq_len)
    row = lambda wd: pl.BlockSpec((rows, wd), lambda i: (i, 0))
    st = pl.BlockSpec((SAMPLE_SEQS, N_HEADS, HEAD, HEAD), lambda i: (i, 0, 0, 0))
    cos_t = jnp.tile(cos, (SAMPLE_SEQS, 1))
    sin_t = jnp.tile(sin_signed, (SAMPLE_SEQS, 1))
    return pl.pallas_call(
        functools.partial(_ret_sample_kernel, seqs=SAMPLE_SEQS),
        grid=(n_seq // SAMPLE_SEQS,),
        in_specs=[row(u.shape[1]), _const_spec(cos_t.shape), _const_spec(sin_t.shape),
                  pl.BlockSpec((None, SAMPLE_SEQS, N_HEADS, HEAD, HEAD), lambda i: (layer, i, 0, 0, 0)),
                  _const_spec(ecum.shape), _const_spec(eend.shape), vec, _const_spec(dec.shape), vec],
        out_specs=[row(GROUP_W), st],
        out_shape=[jax.ShapeDtypeStruct((n, GROUP_W), F32), jax.ShapeDtypeStruct(state.shape[1:], F32)],
        compiler_params=_params(("parallel",)),
        name="ret_sample",
    )(u, cos_t, sin_t, state, ecum, eend, elast, dec, gain)


def _rg_gates(xr, wa_ref, ba_ref, wx_ref, bx_ref, lam_ref):
    r_gate = _sigmoid(_bdot(xr, wa_ref[...]) + ba_ref[...])
    i_gate = _sigmoid(_bdot(xr, wx_ref[...]) + bx_ref[...])
    log_a = (-RG_C * _softplus(-lam_ref[...])) * r_gate
    a = jnp.exp(log_a)
    b = jnp.sqrt(-jnp.tanh(log_a) * (jnp.exp(2.0 * log_a) + 1.0)) * (i_gate * xr)
    return a, b


def _rg_scan(a, b, seg):
    pos = _iota(a.shape, 0) & (seg - 1)
    d = 1
    while d < seg:
        ok = pos >= d
        a_sh = jnp.where(ok, pltpu.roll(a, d, 0), 1.0)
        b_sh = jnp.where(ok, pltpu.roll(b, d, 0), 0.0)
        b = a * b_sh + b
        a = a * a_sh
        d *= 2
    return a, b


def _rg_prompt_kernel(u_ref, cw_ref, cb_ref, wa_ref, ba_ref, wx_ref, bx_ref, lam_ref,
                      y_ref, h_out_ref, buf_out_ref, h_ref, xx_ref):
    t = pl.program_id(1)
    rows = u_ref.shape[0]

    @pl.when(t == 0)
    def _():
        h_ref[...] = jnp.zeros(h_ref.shape, F32)

    u = u_ref[...]
    xr = _conv_taps_carry(u[:, 0:GROUP_W], xx_ref, cw_ref, cb_ref, t == 0)
    a, b = _rg_gates(xr, wa_ref, ba_ref, wx_ref, bx_ref, lam_ref)
    a_cum, h_loc = _rg_scan(a, b, rows)
    hseq = h_loc + a_cum * h_ref[0:1, :]
    h_ref[...] = jnp.broadcast_to(hseq[rows - 1:rows, :], h_ref.shape)
    y_ref[...] = _gelu_tanh(u[:, GROUP_W:]) * hseq

    @pl.when(t == pl.num_programs(1) - 1)
    def _():
        h_out_ref[0] = hseq[rows - 1:rows, :]
        buf_out_ref[0] = xx_ref[pl.ds(8 - (CONV_W - 1), CONV_W - 1), :]


def _rg_sample_kernel(u_ref, h0_ref, buf_ref, cw_ref, cb_ref, wa_ref, ba_ref, wx_ref, bx_ref, lam_ref,
                      y_ref, h_out_ref, buf_out_ref):
    u = u_ref[...]
    xr, new_buf = _conv_taps_seq4(u[:, 0:GROUP_W], buf_ref[...], cw_ref, cb_ref)
    buf_out_ref[...] = new_buf
    a, b = _rg_gates(xr, wa_ref, ba_ref, wx_ref, bx_ref, lam_ref)
    a_cum, h_loc = _rg_scan(a, b, 4)
    hseq = h_loc + a_cum * h0_ref[...]
    h_out_ref[...] = hseq
    y_ref[...] = _gelu_tanh(u[:, GROUP_W:]) * hseq


def _rg(u, h0_rows, buf, cw, cb, wa, ba, wx, bx, lam, n_seq, seq_len):
    n = u.shape[0]
    vec = _const_spec((1, GROUP_W))
    common_in = [_const_spec(cw.shape), vec, _const_spec(wa.shape), vec, _const_spec(wx.shape), vec, vec]
    if h0_rows is None:
        nt = seq_len // MIX_TILE
        row = lambda wd: pl.BlockSpec((MIX_TILE, wd), lambda b, t: (b * nt + t, 0))
        return pl.pallas_call(
            _rg_prompt_kernel,
            grid=(n_seq, nt),
            in_specs=[row(u.shape[1])] + common_in,
            out_specs=[row(GROUP_W), pl.BlockSpec((1, 1, GROUP_W), lambda b, t: (b, 0, 0)),
                       pl.BlockSpec((1, CONV_W - 1, GROUP_W), lambda b, t: (b, 0, 0))],
            out_shape=[jax.ShapeDtypeStruct((n, GROUP_W), F32),
                       jax.ShapeDtypeStruct((n_seq, 1, GROUP_W), F32),
                       jax.ShapeDtypeStruct((n_seq, CONV_W - 1, GROUP_W), F32)],
            scratch_shapes=[pltpu.VMEM((8, GROUP_W), F32), pltpu.VMEM((MIX_TILE + 8, GROUP_W), F32)],
            compiler_params=_params(("parallel", "arbitrary")),
            name="rglru_prompt",
        )(u, cw, cb, wa, ba, wx, bx, lam)
    rows = SAMPLE_SEQS * seq_len
    row = lambda wd: pl.BlockSpec((rows, wd), lambda i: (i, 0))
    return pl.pallas_call(
        _rg_sample_kernel,
        grid=(n_seq // SAMPLE_SEQS,),
        in_specs=[row(u.shape[1]), row(GROUP_W), row(GROUP_W)] + common_in,
        out_specs=[row(GROUP_W)] * 3,
        out_shape=[jax.ShapeDtypeStruct((n, GROUP_W), F32)] * 3,
        compiler_params=_params(("parallel",)),
        name="rglru_sample",
    )(u, h0_rows, buf, cw, cb, wa, ba, wx, bx, lam)


def _hg_inputs(u_ref, lb_ref, seg_shift):
    u = u_ref[...]
    rows = u.shape[0]
    lb = lb_ref[...]
    fg = lb + (1.0 - lb) * _sigmoid(u[:, GROUP_W:2 * GROUP_W])
    q = _silu(u[:, 0:GROUP_W])
    k = 1.0 - fg
    v = u[:, 2 * GROUP_W:3 * GROUP_W]
    cum = _sel_dot(_seg_causal_mask(rows, seg_shift).astype(BF16), jnp.log(fg))
    return q, k, v, cum, u[:, 3 * GROUP_W:4 * GROUP_W]


def _hg_intra(q, k, v, cum, kk_ref, cc_ref, vv_ref, seg):
    rows = q.shape[0]
    pad = kk_ref.shape[0] - rows
    for ref, val in ((kk_ref, k), (cc_ref, cum), (vv_ref, v)):
        ref[0:pad, :] = jnp.zeros((pad, GROUP_W), F32)
        ref[pad:pad + rows, :] = val
    pos = _iota((rows, GROUP_W), 0) & (seg - 1)
    ones_bd = ((_iota((GROUP_W, GROUP_W), 0) >> 6) == (_iota((GROUP_W, GROUP_W), 1) >> 6)).astype(BF16)
    y = jnp.dot((q * k).astype(BF16), ones_bd, preferred_element_type=F32) * v
    for d in range(1, seg):
        k_d = kk_ref[pl.ds(pad - d, rows), :]
        c_d = cc_ref[pl.ds(pad - d, rows), :]
        v_d = vv_ref[pl.ds(pad - d, rows), :]
        p = jnp.where(pos >= d, q * k_d * jnp.exp(cum - c_d), 0.0)
        y = y + jnp.dot(p.astype(BF16), ones_bd, preferred_element_type=F32) * v_d
    return y


def _hg_prompt_kernel(u_ref, lb_ref, g_ref, y_ref, s_out_ref, st_ref, y_scr):
    t = pl.program_id(1)
    rows = u_ref.shape[0]

    @pl.when(t == 0)
    def _():
        st_ref[...] = jnp.zeros(st_ref.shape, F32)

    q, k, v, cum, gate = _hg_inputs(u_ref, lb_ref, 6)
    hmask = _head_block_mask()
    hmask01 = hmask.astype(BF16)
    ones_bd = hmask01
    y_scr[...] = jnp.dot((q * k).astype(BF16), ones_bd, preferred_element_type=F32) * v
    row = _iota((rows, GROUP_W), 0)
    levels = []
    blk_last = cum
    h = 1
    while h < CHUNK:
        upper = (row & h) != 0
        qn = jnp.where(upper, q * jnp.exp(cum - pltpu.roll(blk_last, h, 0)), 0.0)
        kn = jnp.where(upper, 0.0, k * jnp.exp(blk_last - cum))
        levels.append((h, qn, kn))
        blk_last = jnp.where(upper, blk_last, pltpu.roll(blk_last, rows - h, 0))
        h *= 2
    q_in = q * jnp.exp(cum)
    k_end = k * jnp.exp(blk_last - cum)
    trow = _iota((CHUNK, GROUP_W), 0)
    tcol = _iota((CHUNK, GROUP_W), 1) & (CHUNK - 1)
    for c in range(rows // CHUNK):
        rs = slice(c * CHUNK, (c + 1) * CHUNK)
        a_wide = jnp.zeros((CHUNK, GROUP_W), F32)
        for h, qn, kn in levels:
            g = _bdot_nt(qn[rs], _block_diag_rows(kn[rs], hmask01))
            sh = (2 * h).bit_length() - 1
            a_wide = a_wide + jnp.where((trow >> sh) == (tcol >> sh), g, 0.0)
        y_c = _bdot(a_wide, _block_diag_rows(v[rs], hmask01)) + _bdot_nt(q_in[rs], st_ref[...])
        y_scr[rs, :] = y_scr[rs, :] + y_c
        ds_t = jnp.where(hmask, _bdot_tn(v[rs], k_end[rs]), 0.0)
        st_ref[...] = jnp.exp(blk_last[c * CHUNK:c * CHUNK + 1, :]) * st_ref[...] + ds_t
    y_ref[...] = _silu(gate) * (_head_stat(y_scr[...], False) * g_ref[...])

    @pl.when(t == pl.num_programs(1) - 1)
    def _():
        eye = (_iota((HEAD, HEAD), 0) == _iota((HEAD, HEAD), 1)).astype(BF16)
        for h in range(N_HEADS):
            s_out_ref[0, h] = _sel_dot_nt(eye, st_ref[h * HEAD:(h + 1) * HEAD, h * HEAD:(h + 1) * HEAD])


def _hg_sample_kernel(u_ref, s0_ref, lb_ref, g_ref, y_ref, s_out_ref, kk_ref, cc_ref, vv_ref, *, seqs):
    rows = u_ref.shape[0]
    q, k, v, cum, gate = _hg_inputs(u_ref, lb_ref, 2)
    y = _hg_intra(q, k, v, cum, kk_ref, cc_ref, vv_ref, 4)
    pick_last = (_iota((rows, rows), 1) == (_iota((rows, rows), 0) | 3)).astype(BF16)
    last = _sel_dot(pick_last, cum)
    qn = q * jnp.exp(cum)
    kend = k * jnp.exp(last - cum)
    elast = jnp.exp(last)
    row_seq = _iota((rows, HEAD), 0) >> 2
    eye = _iota((HEAD, HEAD), 0) == _iota((HEAD, HEAD), 1)
    y_int = []
    for h in range(N_HEADS):
        ls = slice(h * HEAD, (h + 1) * HEAD)
        acc = jnp.zeros((rows, HEAD), F32)
        for i in range(seqs):
            in_seq = row_seq == i
            s_prev = s0_ref[i, h]
            acc = jnp.where(in_seq, _bdot(qn[:, ls], s_prev), acc)
            ds = _bdot_tn(jnp.where(in_seq, kend[:, ls], 0.0), v[:, ls])
            a_col = jnp.sum(jnp.where(eye, elast[4 * i:4 * i + 1, ls], 0.0), axis=-1, keepdims=True)
            s_out_ref[i, h] = a_col * s_prev + ds
        y_int.append(acc)
    y = y + jnp.concatenate(y_int, axis=-1)
    y_ref[...] = _silu(gate) * (_head_stat(y, False) * g_ref[...])


def _hg(u, state, lb, gain, n_seq, seq_len, layer):
    n = u.shape[0]
    vec = _const_spec((1, GROUP_W))
    if state is None:
        nt = seq_len // MIX_TILE
        row = lambda wd: pl.BlockSpec((MIX_TILE, wd), lambda b, t: (b * nt + t, 0))
        return pl.pallas_call(
            _hg_prompt_kernel,
            grid=(n_seq, nt),
            in_specs=[row(u.shape[1]), vec, vec],
            out_specs=[row(GROUP_W), pl.BlockSpec((1, N_HEADS, HEAD, HEAD), lambda b, t: (b, 0, 0, 0))],
            out_shape=[jax.ShapeDtypeStruct((n, GROUP_W), F32),
                       jax.ShapeDtypeStruct((n_seq, N_HEADS, HEAD, HEAD), F32)],
            scratch_shapes=[pltpu.VMEM((GROUP_W, GROUP_W), F32), pltpu.VMEM((MIX_TILE, GROUP_W), F32)],
            compiler_params=_params(("parallel", "arbitrary")),
            name="hgrn_prompt",
        )(u, lb, gain)
    rows = SAMPLE_SEQS * seq_len
    row = lambda wd: pl.BlockSpec((rows, wd), lambda i: (i, 0))
    st = pl.BlockSpec((SAMPLE_SEQS, N_HEADS, HEAD, HEAD), lambda i: (i, 0, 0, 0))
    shifted = pltpu.VMEM((rows + 8, GROUP_W), F32)
    return pl.pallas_call(
        functools.partial(_hg_sample_kernel, seqs=SAMPLE_SEQS),
        grid=(n_seq // SAMPLE_SEQS,),
        in_specs=[row(u.shape[1]),
                  pl.BlockSpec((None, SAMPLE_SEQS, N_HEADS, HEAD, HEAD), lambda i: (layer, i, 0, 0, 0)), vec, vec],
        out_specs=[row(GROUP_W), st],
        out_shape=[jax.ShapeDtypeStruct((n, GROUP_W), F32), jax.ShapeDtypeStruct(state.shape[1:], F32)],
        scratch_shapes=[shifted, shifted, shifted],
        compiler_params=_params(("parallel",)),
        name="hgrn_sample",
    )(u, state, lb, gain)


def _block_diag(w):
    h, i, j = w.shape
    eye = jnp.eye(h, dtype=w.dtype)
    return (eye[:, None, :, None] * w[:, :, None, :]).reshape(h * i, h * j)


def _rope_tables(pos0, seq_len, split_halves):
    half = HEAD // 2
    pos = pos0 + jnp.arange(seq_len, dtype=F32)
    inv = ROPE_BASE ** (-jnp.arange(half, dtype=F32) / half)
    ang = pos[:, None] * inv
    cos = jnp.tile(jnp.cos(ang), (1, 2 * N_HEADS))
    sin = jnp.sin(ang)
    if split_halves:
        sin_signed = jnp.concatenate([jnp.tile(-sin, (1, N_HEADS)), jnp.tile(sin, (1, N_HEADS))], axis=-1)
    else:
        sin_signed = jnp.tile(jnp.concatenate([-sin, sin], axis=-1), (1, N_HEADS))
    return cos, sin_signed


def _layer_weights(P, l):
    row = lambda a: a.reshape(1, -1)
    pad128 = lambda a: jnp.pad(a, (0, 128 - a.shape[0])).reshape(1, 128)
    lb_sm = jax.nn.softmax(P['hg_lb'].astype(F32), axis=0)
    lb = (jnp.cumsum(lb_sm, axis=0) - lb_sm[0])[l]
    return dict(
        w_in=P['w_in_prepped'],
        ln_mix_pre=row(P['ln_mix_pre'][l]), ln_mix_post=row(P['ln_mix_post'][l]),
        ln_xa_pre=row(P['ln_xa_pre'][l]), ln_xa_post=row(P['ln_xa_post'][l]),
        ln_ffn_pre=row(P['ln_ffn_pre'][l]), ln_ffn_post=row(P['ln_ffn_post'][l]),
        ssd_cw=P['ssd_conv_w'][l], ssd_cb=row(P['ssd_conv_b'][l]),
        ssd_dtb=pad128(P['ssd_dt_bias'][l]), ssd_alog=pad128(P['ssd_A_log'][l]),
        ssd_d=row(jnp.repeat(P['ssd_D'][l], HEAD)), ssd_norm=row(P['ssd_norm'][l]),
        ret_norm=row(P['ret_norm'][l]),
        rg_cw=P['rg_conv_w'][l], rg_cb=row(P['rg_conv_b'][l]),
        rg_wa=_block_diag(P['rg_wa'][l]).astype(BF16), rg_ba=row(P['rg_ba'][l]),
        rg_wx=_block_diag(P['rg_wx'][l]).astype(BF16), rg_bx=row(P['rg_bx'][l]),
        rg_lam=row(P['rg_lambda'][l]),
        hg_lb=row(lb), hg_norm=row(P['hg_norm'][l]),
        w_out=P['w_out_bf16'], w_xq=P['w_xq_bf16'], w_xo=P['w_xo_bf16'], w_gu=P['w_gu_bf16'],
        w_down=P['w_down_bf16'], ln_mem=row(P['ln_mem'][l]), w_xkv=P['w_xkv_bf16'],
    )


def _layer(x, W, k_mem, v_mem, st, n_seq, seq_len, pos0, layer):
    ret_cols = IN_COLS_RET_SPLIT if st is None else IN_COLS_RET
    u_ssd, u_ret, u_rg, u_hg, u_dt = _norm_matmul(
        x, W['ln_mix_pre'], W['w_in'], (IN_COLS_SSD, ret_cols, IN_COLS_RG, IN_COLS_HG, IN_COLS_DT), "in_proj",
        layer=layer)
    cos, sin_signed = _rope_tables(pos0, seq_len, split_halves=st is None)
    if st is None:
        y_ssd, s_ssd, b_ssd = _ssd(u_ssd, u_dt, None, None, W['ssd_cw'], W['ssd_cb'], W['ssd_dtb'],
                                   W['ssd_alog'], W['ssd_d'], W['ssd_norm'], n_seq, seq_len, layer)
        y_ret, s_ret = _ret(u_ret, cos, sin_signed, None, W['ret_norm'], n_seq, seq_len, layer)
        y_rg, h_rg, b_rg = _rg(u_rg, None, None, W['rg_cw'], W['rg_cb'], W['rg_wa'], W['rg_ba'],
                               W['rg_wx'], W['rg_bx'], W['rg_lam'], n_seq, seq_len)
        h_rg = h_rg.reshape(n_seq, GROUP_W)
        y_hg, s_hg = _hg(u_hg, None, W['hg_lb'], W['hg_norm'], n_seq, seq_len, layer)
    else:
        ssd_s, ssd_buf, ret_s, rg_h, rg_buf, hg_s = st
        pad_rows = lambda b: jnp.pad(b, ((0, 0), (0, 1), (0, 0))).reshape(n_seq * 4, b.shape[-1])
        y_ssd, s_ssd, b_ssd = _ssd(u_ssd, u_dt, ssd_s, pad_rows(ssd_buf), W['ssd_cw'], W['ssd_cb'],
                                   W['ssd_dtb'], W['ssd_alog'], W['ssd_d'], W['ssd_norm'], n_seq, seq_len, layer)
        b_ssd = b_ssd.reshape(n_seq, 4, -1)[:, :CONV_W - 1]
        y_ret, s_ret = _ret(u_ret, cos, sin_signed, ret_s, W['ret_norm'], n_seq, seq_len, layer)
        y_rg, h_rows, b_rg = _rg(u_rg, jnp.repeat(rg_h, seq_len, axis=0), pad_rows(rg_buf), W['rg_cw'],
                                 W['rg_cb'], W['rg_wa'], W['rg_ba'], W['rg_wx'], W['rg_bx'], W['rg_lam'],
                                 n_seq, seq_len)
        h_rg = h_rows.reshape(n_seq, seq_len, GROUP_W)[:, seq_len - 1]
        b_rg = b_rg.reshape(n_seq, 4, -1)[:, :CONV_W - 1]
        y_hg, s_hg = _hg(u_hg, hg_s, W['hg_lb'], W['hg_norm'], n_seq, seq_len, layer)
    ys = (y_ssd, y_ret, y_rg, y_hg)
    if st is None:
        x3 = _xattn_ffn(ys, x, W['w_out'], W['ln_mix_post'], k_mem, v_mem, W['ln_xa_pre'], W['w_xq'], W['w_xo'],
                        W['ln_xa_post'], W['ln_ffn_pre'], W['w_gu'], W['w_down'], W['ln_ffn_post'], seq_len,
                        layer)
    else:
        x1, q = _out_proj(ys, x, W['w_out'], W['ln_mix_post'], W['ln_xa_pre'], W['w_xq'], layer)
        o = _attention_sample(q, k_mem, v_mem, layer, seq_len)
        x3 = _ffn(o, x1, W['w_xo'], W['ln_xa_post'], W['ln_ffn_pre'], W['w_gu'], W['w_down'],
                  W['ln_ffn_post'], layer)
    return x3, (s_ssd, b_ssd, s_ret, h_rg, b_rg, s_hg)


def kernel(x_prompt, x_sample, state_ssd, state_ssd_conv, state_ret, state_rglru, state_rglru_conv, state_hgrn, cache_mem_k, cache_mem_v, mem_prompt, ln_mix_pre, ln_mix_post, ln_xa_pre, ln_xa_post, ln_ffn_pre, ln_ffn_post, w_in, ssd_conv_w, ssd_conv_b, ssd_dt_bias, ssd_A_log, ssd_D, ssd_norm, ret_norm, rg_conv_w, rg_conv_b, rg_wa, rg_ba, rg_wx, rg_bx, rg_lambda, hg_lb, hg_norm, w_out, ln_mem, w_xq, w_xkv, w_xo, w_gu, w_down):
    P = dict(ln_mix_pre=ln_mix_pre, ln_mix_post=ln_mix_post, ln_xa_pre=ln_xa_pre, ln_xa_post=ln_xa_post,
             ln_ffn_pre=ln_ffn_pre, ln_ffn_post=ln_ffn_post, w_in=w_in, ssd_conv_w=ssd_conv_w,
             ssd_conv_b=ssd_conv_b, ssd_dt_bias=ssd_dt_bias, ssd_A_log=ssd_A_log, ssd_D=ssd_D,
             ssd_norm=ssd_norm, ret_norm=ret_norm, rg_conv_w=rg_conv_w, rg_conv_b=rg_conv_b,
             rg_wa=rg_wa, rg_ba=rg_ba, rg_wx=rg_wx, rg_bx=rg_bx, rg_lambda=rg_lambda, hg_lb=hg_lb,
             hg_norm=hg_norm, w_out=w_out, ln_mem=ln_mem, w_xq=w_xq, w_xkv=w_xkv, w_xo=w_xo,
             w_gu=w_gu, w_down=w_down)
    P['w_in_prepped'] = _in_weight_prep(w_in)
    for name in ('w_out', 'w_xq', 'w_xo', 'w_gu', 'w_down', 'w_xkv'):
        P[name + '_bf16'] = P[name].astype(BF16)
    ssd_t = jnp.swapaxes(state_ssd, -1, -2)
    depth = w_in.shape[0]
    bp, tp, d = x_prompt.shape
    bs, ts, _ = x_sample.shape
    n_mem = mem_prompt.shape[1]
    y_p = x_prompt.reshape(bp * tp, d)
    y_s = x_sample.reshape(bs * ts, d)
    mem = mem_prompt.reshape(bp * n_mem, d)
    p_st, s_st, p_mk, p_mv = [], [], [], []
    for l in range(depth):
        W = _layer_weights(P, l)
        mk, mv = _norm_matmul(mem, W['ln_mem'], W['w_xkv'], (((0, d),), ((d, 2 * d),)), "mem_kv", layer=l)
        mk = mk.reshape(bp, n_mem, d)
        mv = mv.reshape(bp, n_mem, d)
        y_p, st_p = _layer(y_p, W, mk, mv, None, bp, tp, 0.0, l)
        p_st.append(st_p)
        p_mk.append(mk.reshape(bp, n_mem, XA_H, XA_HD))
        p_mv.append(mv.reshape(bp, n_mem, XA_H, XA_HD))
        st_l = (ssd_t, state_ssd_conv[l], state_ret, state_rglru[l], state_rglru_conv[l], state_hgrn)
        y_s, st_s = _layer(y_s, W, cache_mem_k, cache_mem_v, st_l, bs, ts, float(PAST_LEN), l)
        s_st.append(st_s)
    stack = lambda sts, i: jnp.stack([s[i] for s in sts], axis=0)
    return (y_p.reshape(bp, tp, d), y_s.reshape(bs, ts, d),
            jnp.swapaxes(stack(p_st, 0), -1, -2), stack(p_st, 1), stack(p_st, 2), stack(p_st, 3), stack(p_st, 4),
            stack(p_st, 5),
            jnp.stack(p_mk, axis=0), jnp.stack(p_mv, axis=0),
            jnp.swapaxes(stack(s_st, 0), -1, -2), stack(s_st, 1), stack(s_st, 2), stack(s_st, 3), stack(s_st, 4),
            stack(s_st, 5))
```

```python
import functools
import math

import jax
import jax.numpy as jnp
from jax import lax
from jax.experimental import pallas as pl
from jax.experimental.pallas import tpu as pltpu

F32 = jnp.float32
BF16 = jnp.bfloat16
EPS = 1e-6

GROUP_W = 256
HEAD = 64
N_HEADS = 4
SSD_N = 128
SSD_G = 2
CONV_W = 4
XA_H = 4
XA_HD = 256
ROPE_BASE = 10000.0
RG_C = 8.0
CHUNK = 64
HG_CHUNK = 16
PAST_LEN = 16384

ROW_TILE = 512
MIX_TILE = 512
RG_TILE = 256
SAMPLE_SEQS = 16
ATT_TILE = 512
ATT_SEQS = 4
FF_TILE = 256
VMEM_LIMIT = 56 * 1024 * 1024


def _bdot(a, b):
    return jnp.dot(a.astype(BF16), b.astype(BF16), preferred_element_type=F32)


def _bdot_nt(a, b):
    return lax.dot_general(a.astype(BF16), b.astype(BF16), (((1,), (1,)), ((), ())),
                           preferred_element_type=F32)


def _bdot_tn(a, b):
    return lax.dot_general(a.astype(BF16), b.astype(BF16), (((0,), (0,)), ((), ())),
                           preferred_element_type=F32)


def _split3(x):
    hi = x.astype(BF16)
    r = x - hi.astype(F32)
    mid = r.astype(BF16)
    lo = (r - mid.astype(F32)).astype(BF16)
    return hi, mid, lo


def _sel_dot(sel, x):
    hi, mid, lo = _split3(x)
    d = lambda y: jnp.dot(sel, y, preferred_element_type=F32)
    return (d(hi) + d(mid)) + d(lo)


def _dot_sel(x, sel):
    hi, mid, lo = _split3(x)
    d = lambda y: jnp.dot(y, sel, preferred_element_type=F32)
    return (d(hi) + d(mid)) + d(lo)


def _sel_dot_nt(sel, x):
    hi, mid, lo = _split3(x)
    d = lambda y: lax.dot_general(sel, y, (((1,), (1,)), ((), ())), preferred_element_type=F32)
    return (d(hi) + d(mid)) + d(lo)


def _rms(x, g):
    return x * lax.rsqrt(jnp.mean(x * x, axis=-1, keepdims=True) + EPS) * g


def _sigmoid(x):
    return jax.nn.sigmoid(x)


def _silu(x):
    return x * jax.nn.sigmoid(x)


def _softplus(x):
    return jnp.maximum(x, 0.0) + jnp.log1p(jnp.exp(-jnp.abs(x)))


def _gelu_tanh(x):
    c = math.sqrt(2.0 / math.pi)
    return 0.5 * x * (1.0 + jnp.tanh(c * (x + 0.044715 * (x * x * x))))


def _iota(shape, dim):
    return lax.broadcasted_iota(jnp.int32, shape, dim)


def _seg_causal_mask(n, seg_shift):
    r = _iota((n, n), 0)
    c = _iota((n, n), 1)
    return ((r >> seg_shift) == (c >> seg_shift)) & (c <= r)


def _seg_cumsum(x, seg_shift):
    rows = x.shape[0]
    blk = min(rows, 256)
    mask = _seg_causal_mask(blk, seg_shift).astype(BF16)
    return jnp.concatenate([_sel_dot(mask, x[i:i + blk]) for i in range(0, rows, blk)], axis=0)


def _head_expand_mat(width_in=128):
    r = _iota((width_in, GROUP_W), 0)
    c = _iota((width_in, GROUP_W), 1)
    return ((c >> 6) == r).astype(BF16)


def _head_block_mask():
    return (_iota((GROUP_W, GROUP_W), 0) >> 6) == (_iota((GROUP_W, GROUP_W), 1) >> 6)


def _block_diag_rows(x, mask01):
    return jnp.concatenate([x.astype(BF16)] * N_HEADS, axis=0) * mask01


def _head_stat(y, center):
    ones_bd = _head_block_mask().astype(BF16)
    if center:
        y = y - _dot_sel(y, ones_bd) * (1.0 / HEAD)
    return y * lax.rsqrt(_dot_sel(y * y, ones_bd) * (1.0 / HEAD) + EPS)


def _conv_taps_carry(x, xx_ref, w_ref, b_ref, first):
    rows = x.shape[0]

    @pl.when(first)
    def _():
        xx_ref[0:8, :] = jnp.zeros((8, x.shape[1]), F32)

    xx_ref[8:8 + rows, :] = x
    y = b_ref[...] + w_ref[CONV_W - 1:CONV_W, :] * x
    for j in range(CONV_W - 1):
        y = y + w_ref[j:j + 1, :] * xx_ref[pl.ds(8 - (CONV_W - 1) + j, rows), :]
    xx_ref[0:8, :] = xx_ref[rows:rows + 8, :]
    return y


def _conv_taps_seq4(x, buf4, w_ref, b_ref):
    rows = x.shape[0]
    r = _iota((rows, rows), 0)
    c = _iota((rows, rows), 1)
    t = r & 3
    y = b_ref[...] + w_ref[CONV_W - 1:CONV_W, :] * x
    for d in range(1, CONV_W):
        shift = ((c == r - d) & (t >= d)).astype(BF16)
        hist = ((t < d) & (c == (r - t) + (3 + t - d))).astype(BF16)
        y = y + w_ref[CONV_W - 1 - d:CONV_W - d, :] * (_sel_dot(shift, x) + _sel_dot(hist, buf4))
    new_sel = (((r & 3) < 3) & (c == r + 1)).astype(BF16)
    return y, _sel_dot(new_sel, x)


def _norm_matmul_kernel(x_ref, g_ref, w_ref, *o_refs, col_ranges):
    h = _rms(x_ref[...], g_ref[...]).astype(BF16)
    for o_ref, pieces in zip(o_refs, col_ranges):
        off = 0
        for a, b in pieces:
            o_ref[:, off:off + b - a] = jnp.dot(h, w_ref[:, a:b], preferred_element_type=F32)
            off += b - a


def _const_spec(shape):
    nd = len(shape)
    return pl.BlockSpec(shape, lambda *_: (0,) * nd, pipeline_mode=pl.Buffered(1))


def _layer_spec(shape, layer):
    nd = len(shape)
    return pl.BlockSpec((None,) + tuple(shape[1:]), lambda *_: (layer,) + (0,) * (nd - 1),
                        pipeline_mode=pl.Buffered(1))


def _params(sem):
    return pltpu.CompilerParams(dimension_semantics=sem, vmem_limit_bytes=VMEM_LIMIT)


def _in_weight_prep_kernel(wt_ref, o_ref, *, blocks, dt_block, dt_row, dt_n):
    for j, pieces in enumerate(blocks):
        if j == dt_block:
            dt = wt_ref[pl.ds(dt_row, 8), :]
            dt = jnp.where(_iota(dt.shape, 0) < dt_n, dt, 0.0)
            blk = jnp.concatenate([dt, jnp.zeros((120, dt.shape[1]), F32)], axis=0)
        else:
            blk = jnp.concatenate([wt_ref[pl.ds(src, n), :] for src, n in pieces], axis=0)
        o_ref[:, j * 128:(j + 1) * 128] = blk.T.astype(BF16)


IN_COLS_SSD = ((0, 1024),)
IN_COLS_RET = ((1024, 2048),)
IN_COLS_RG = ((2048, 2560),)
IN_COLS_HG = ((2560, 3584),)
IN_COLS_DT = ((3584, 3712),)
IN_COLS_RET_SPLIT = ((3712, 4224), (1536, 2048))


def _in_weight_prep(w_in):
    depth, d, d_in = w_in.shape
    z0 = GROUP_W + (GROUP_W + 2 * SSD_G * SSD_N)
    d0 = z0 + N_HEADS
    blocks = [((src, 128),) for src in tuple(range(0, z0, 128)) + tuple(range(d0, d_in, 128))]
    dt_block = len(blocks)
    blocks.append(())
    half = HEAD // 2
    for base in (d0, d0 + GROUP_W):
        for j in range(2):
            blocks.append(tuple((base + h * HEAD + j * half, half) for h in range(N_HEADS)))
    n_out = len(blocks) * 128
    wt = jnp.swapaxes(w_in, 1, 2)
    return pl.pallas_call(
        functools.partial(_in_weight_prep_kernel, blocks=tuple(blocks), dt_block=dt_block, dt_row=z0,
                          dt_n=N_HEADS),
        grid=(depth,),
        in_specs=[pl.BlockSpec((None, d_in, d), lambda l: (l, 0, 0))],
        out_specs=pl.BlockSpec((None, d, n_out), lambda l: (l, 0, 0)),
        out_shape=jax.ShapeDtypeStruct((depth, d, n_out), BF16),
        compiler_params=_params(("parallel",)),
        name="in_weight_prep",
    )(wt)


def _wspec(w, layer):
    return _layer_spec(w.shape, layer) if w.ndim == 3 else _const_spec(w.shape)


def _norm_matmul(x, g, w, col_ranges, name, layer=None):
    n, d = x.shape
    tm = min(ROW_TILE, n)
    w_spec = _wspec(w, layer)
    widths = [sum(b - a for a, b in pieces) for pieces in col_ranges]
    return pl.pallas_call(
        functools.partial(_norm_matmul_kernel, col_ranges=col_ranges),
        grid=(n // tm,),
        in_specs=[pl.BlockSpec((tm, d), lambda i: (i, 0)), _const_spec((1, d)), w_spec],
        out_specs=[pl.BlockSpec((tm, wd), lambda i: (i, 0)) for wd in widths],
        out_shape=[jax.ShapeDtypeStruct((n, wd), F32) for wd in widths],
        compiler_params=_params(("parallel",)),
        name=name,
    )(x, g, w)


def _out_proj_kernel(ys_ref, yr_ref, yg_ref, yh_ref, x_ref, wo_ref, gpost_ref, gpre_ref, wq_ref, x1_ref, q_ref):
    x1 = _mix_out(ys_ref, yr_ref, yg_ref, yh_ref, x_ref, wo_ref, gpost_ref)
    x1_ref[...] = x1
    q_ref[...] = _bdot(_rms(x1, gpre_ref[...]), wq_ref[...])


def _out_proj(ys, x, w_out, g_post, g_pre, w_xq, layer):
    n, d = x.shape
    tm = min(ROW_TILE, n)
    row = lambda wd: pl.BlockSpec((tm, wd), lambda i: (i, 0))
    vec = _const_spec((1, d))
    return pl.pallas_call(
        _out_proj_kernel,
        grid=(n // tm,),
        in_specs=[row(GROUP_W)] * 4 + [row(d), _wspec(w_out, layer), vec, vec, _wspec(w_xq, layer)],
        out_specs=[row(d)] * 2,
        out_shape=[jax.ShapeDtypeStruct((n, d), F32)] * 2,
        compiler_params=_params(("parallel",)),
        name="out_proj",
    )(*ys, x, w_out, g_post, g_pre, w_xq)


def _ffn_tail(o, x1, wxo_ref, gxa_ref, gpre_ref, wgu_ref, wdn_ref, gpost_ref, d_ff):
    a = _bdot(o, wxo_ref[...])
    x2 = x1 + _rms(a, gxa_ref[...])
    h = _rms(x2, gpre_ref[...]).astype(BF16)
    acc = jnp.zeros(x2.shape, F32)
    for j in range(0, d_ff, FF_TILE):
        g = jnp.dot(h, wgu_ref[:, j:j + FF_TILE], preferred_element_type=F32)
        u = jnp.dot(h, wgu_ref[:, d_ff + j:d_ff + j + FF_TILE], preferred_element_type=F32)
        act = (_silu(g) * u).astype(BF16)
        acc = acc + jnp.dot(act, wdn_ref[j:j + FF_TILE, :], preferred_element_type=F32)
    return x2 + _rms(acc, gpost_ref[...])


def _ffn_kernel(o_ref, x1_ref, wxo_ref, gxa_ref, gpre_ref, wgu_ref, wdn_ref, gpost_ref, x3_ref, *, d_ff):
    x3_ref[...] = _ffn_tail(o_ref[...], x1_ref[...], wxo_ref, gxa_ref, gpre_ref, wgu_ref, wdn_ref, gpost_ref,
                            d_ff)


def _ffn(o, x1, w_xo, g_xa, g_pre, w_gu, w_down, g_post, layer):
    n, d = x1.shape
    d_ff = w_down.shape[-2]
    tm = min(ROW_TILE, n)
    row = pl.BlockSpec((tm, d), lambda i: (i, 0))
    vec = _const_spec((1, d))
    return pl.pallas_call(
        functools.partial(_ffn_kernel, d_ff=d_ff),
        grid=(n // tm,),
        in_specs=[row, row, _wspec(w_xo, layer), vec, vec, _wspec(w_gu, layer), _wspec(w_down, layer), vec],
        out_specs=row,
        out_shape=jax.ShapeDtypeStruct((n, d), F32),
        compiler_params=_params(("parallel",)),
        name="attn_out_ffn",
    )(o, x1, w_xo, g_xa, g_pre, w_gu, w_down, g_post)


def _mix_out(ys_ref, yr_ref, yg_ref, yh_ref, x_ref, wo_ref, gpost_ref):
    mix = _bdot(ys_ref[...], wo_ref[0:GROUP_W, :])
    mix = mix + _bdot(yr_ref[...], wo_ref[GROUP_W:2 * GROUP_W, :])
    mix = mix + _bdot(yg_ref[...], wo_ref[2 * GROUP_W:3 * GROUP_W, :])
    mix = mix + _bdot(yh_ref[...], wo_ref[3 * GROUP_W:4 * GROUP_W, :])
    return x_ref[...] + _rms(mix, gpost_ref[...])


def _xattn_ffn_kernel(ys_ref, yr_ref, yg_ref, yh_ref, x_ref, wo_ref, gmix_ref, k_ref, v_ref, gq_ref, wq_ref,
                      wxo_ref, gxa_ref, gpre_ref, wgu_ref, wdn_ref, gpost_ref, x3_ref, o_scr, *, d_ff):
    x1 = _mix_out(ys_ref, yr_ref, yg_ref, yh_ref, x_ref, wo_ref, gmix_ref)
    q = _bdot(_rms(x1, gq_ref[...]), wq_ref[...])
    scale = XA_HD ** -0.5
    for h in range(XA_H):
        sl = slice(h * XA_HD, (h + 1) * XA_HD)
        s = _bdot_nt(q[:, sl], k_ref[0, :, sl]) * scale
        o_scr[:, sl] = _softmax_pv(s, v_ref[0, :, sl])
    x3_ref[...] = _ffn_tail(o_scr[...], x1, wxo_ref, gxa_ref, gpre_ref, wgu_ref, wdn_ref, gpost_ref, d_ff)


def _xattn_ffn(ys, x, w_out, g_mix, k, v, g_q, w_xq, w_xo, g_xa, g_pre, w_gu, w_down, g_post, seq_len, layer):
    n, d = x.shape
    n_seq, n_mem, _ = k.shape
    d_ff = w_down.shape[-2]
    nt = seq_len // ROW_TILE
    row = lambda wd: pl.BlockSpec((ROW_TILE, wd), lambda b, t: (b * nt + t, 0))
    kv = pl.BlockSpec((1, n_mem, d), lambda b, t: (b, 0, 0))
    vec = _const_spec((1, d))
    return pl.pallas_call(
        functools.partial(_xattn_ffn_kernel, d_ff=d_ff),
        grid=(n_seq, nt),
        in_specs=[row(GROUP_W)] * 4 + [row(d), _wspec(w_out, layer), vec, kv, kv, vec, _wspec(w_xq, layer),
                                       _wspec(w_xo, layer), vec, vec, _wspec(w_gu, layer),
                                       _wspec(w_down, layer), vec],
        out_specs=row(d),
        out_shape=jax.ShapeDtypeStruct((n, d), F32),
        scratch_shapes=[pltpu.VMEM((ROW_TILE, d), F32)],
        compiler_params=_params(("parallel", "parallel")),
        name="xattn_ffn",
    )(*ys, x, w_out, g_mix, k, v, g_q, w_xq, w_xo, g_xa, g_pre, w_gu, w_down, g_post)


def _softmax_pv(s, v):
    m = jnp.max(s, axis=-1, keepdims=True)
    p = jnp.exp(s - m)
    return _bdot(p, v) / jnp.sum(p, axis=-1, keepdims=True)


def _attn_sample_kernel(q_ref, k_ref, v_ref, o_ref, *, seqs, seg):
    scale = XA_HD ** -0.5
    rows = seqs * seg
    n_mem = k_ref.shape[2]
    wq = jnp.concatenate([q_ref[:, h * XA_HD:(h + 1) * XA_HD] for h in range(XA_H)], axis=0)
    wshape = (XA_H * rows, n_mem * XA_H)
    valid = (_iota(wshape, 0) >> (rows.bit_length() - 1)) == (_iota(wshape, 1) & (XA_H - 1))
    row_seq = _iota((rows, XA_HD), 0) >> (seg.bit_length() - 1)
    accs = [jnp.zeros((rows, XA_HD), F32) for _ in range(XA_H)]
    for i in range(seqs):
        k2 = k_ref[0, i].reshape(n_mem * XA_H, XA_HD)
        v2 = v_ref[0, i].reshape(n_mem * XA_H, XA_HD)
        s = jnp.where(valid, _bdot_nt(wq, k2) * scale, -jnp.inf)
        o = _softmax_pv(s, v2)
        for h in range(XA_H):
            accs[h] = jnp.where(row_seq == i, o[h * rows:(h + 1) * rows], accs[h])
    for h in range(XA_H):
        o_ref[:, h * XA_HD:(h + 1) * XA_HD] = accs[h]


def _attention_sample(q, cache_k, cache_v, layer, seq_len):
    n, d = q.shape
    _, n_seq, n_mem, heads, hd = cache_k.shape
    kv = pl.BlockSpec((1, ATT_SEQS, n_mem, heads, hd), lambda i: (layer, i, 0, 0, 0))
    row = pl.BlockSpec((ATT_SEQS * seq_len, d), lambda i: (i, 0))
    return pl.pallas_call(
        functools.partial(_attn_sample_kernel, seqs=ATT_SEQS, seg=seq_len),
        grid=(n_seq // ATT_SEQS,),
        in_specs=[row, kv, kv],
        out_specs=row,
        out_shape=jax.ShapeDtypeStruct((n, d), F32),
        compiler_params=_params(("parallel",)),
        name="xattn_sample",
    )(q, cache_k, cache_v)


def _ssd_intra(q, k, v, cum_x, cum_t, mask):
    sc = _bdot_nt(q, k)
    outs = []
    for hh in range(2):
        col = cum_x[:, hh * HEAD:(hh + 1) * HEAD]
        dec = jnp.exp(jnp.where(mask, col - cum_t[hh:hh + 1, :], -jnp.inf))
        outs.append(_bdot(sc * dec, v[:, hh * HEAD:(hh + 1) * HEAD]))
    return jnp.concatenate(outs, axis=-1)


def _ssd_common(u, dtraw_ref, dtb_ref, alog_ref, conv, seg_shift):
    rows = u.shape[0]
    xbc = _silu(conv)
    xs = xbc[:, 0:GROUP_W]
    bm = xbc[:, GROUP_W:2 * GROUP_W]
    cm = xbc[:, 2 * GROUP_W:3 * GROUP_W]
    dt = _softplus(dtraw_ref[...] + dtb_ref[...])
    la = -jnp.exp(alog_ref[...]) * dt
    cum = _seg_cumsum(la, seg_shift)
    expand = _head_expand_mat()
    dt_x = _dot_sel(dt, expand)
    cum_x = _dot_sel(cum, expand)
    sel8 = (_iota((8, 128), 0) == _iota((8, 128), 1)).astype(BF16)
    cum_t = _sel_dot_nt(sel8, cum)
    expand_full = ((_iota((128, 4 * 128), 1) >> 7) == _iota((128, 4 * 128), 0)).astype(BF16)
    cum_full = _dot_sel(cum, expand_full)
    return xs, bm, cm, dt_x, cum_x, cum_t, cum_full


def _ssd_finish(y, xs, z, d_ref, g_ref):
    y = y + d_ref[...] * xs
    return _rms(y * _silu(z), g_ref[...])


def _ssd_prompt_kernel(u_ref, dtraw_ref, cw_ref, cb_ref, dtb_ref, alog_ref, d_ref, g_ref,
                       y_ref, s_out_ref, buf_out_ref, s_ref, xx_ref, y_scr):
    t = pl.program_id(1)
    rows = u_ref.shape[0]

    @pl.when(t == 0)
    def _():
        s_ref[...] = jnp.zeros(s_ref.shape, F32)

    u = u_ref[...]
    conv = _conv_taps_carry(u[:, GROUP_W:], xx_ref, cw_ref, cb_ref, t == 0)
    xs, bm, cm, dt_x, cum_x, _, cum_full = _ssd_common(u, dtraw_ref, dtb_ref, alog_ref, conv, 6)
    v = xs * dt_x
    ecum_x = jnp.exp(cum_x)
    hmask01 = _head_block_mask().astype(BF16)
    gmask01 = ((_iota((GROUP_W, GROUP_W), 0) >> 7) == (_iota((GROUP_W, GROUP_W), 1) >> 7)).astype(BF16)
    wrow = _iota((CHUNK, GROUP_W), 0)
    wcol = _iota((CHUNK, GROUP_W), 1) & (CHUNK - 1)
    n_chunks = rows // CHUNK
    groups = [slice(g * 128, (g + 1) * 128) for g in range(SSD_G)]
    a_wide, ds_t = [], []
    for c in range(n_chunks):
        rs = slice(c * CHUNK, (c + 1) * CHUNK)
        cum_c = cum_x[rs]
        last_x = cum_x[c * CHUNK + CHUNK - 1:(c + 1) * CHUNK, :]
        vend = v[rs] * jnp.exp(last_x - cum_c)
        cum_row = jnp.sum(jnp.where(wrow == wcol, cum_c, 0.0), axis=0, keepdims=True)
        dec = jnp.exp(jnp.where(wcol <= wrow, cum_c - cum_row, -jnp.inf))
        a_wide.append(_bdot_nt(cm[rs], _block_diag_rows(bm[rs], gmask01)) * dec)
        ds_t.append([_bdot_tn(vend[:, ls], bm[rs, ls]) for ls in groups])
    s_in = []
    s_cur = [s_ref[ls, :] for ls in groups]
    for c in range(n_chunks):
        last_full = cum_full[c * CHUNK + CHUNK - 1:(c + 1) * CHUNK, :]
        s_in.append([x.astype(BF16) for x in s_cur])
        for g in range(SSD_G):
            decay = jnp.concatenate(
                [jnp.broadcast_to(jnp.exp(last_full[:, h * 128:(h + 1) * 128]), (HEAD, 128))
                 for h in (2 * g, 2 * g + 1)], axis=0)
            s_cur[g] = decay * s_cur[g] + ds_t[c][g]
    for g, ls in enumerate(groups):
        s_ref[ls, :] = s_cur[g]
    for c in range(n_chunks):
        rs = slice(c * CHUNK, (c + 1) * CHUNK)
        y_intra = _bdot(a_wide[c], _block_diag_rows(v[rs], hmask01))
        y_inter = [_bdot_nt(cm[rs, ls], s_in[c][g]) for g, ls in enumerate(groups)]
        y_scr[rs, :] = y_intra + jnp.concatenate(y_inter, axis=-1) * ecum_x[rs]
    y_ref[...] = _ssd_finish(y_scr[...], xs, u[:, 0:GROUP_W], d_ref, g_ref)

    @pl.when(t == pl.num_programs(1) - 1)
    def _():
        for h in range(N_HEADS):
            s_out_ref[0, h] = s_ref[h * HEAD:(h + 1) * HEAD, :]
        buf_out_ref[0] = xx_ref[pl.ds(8 - (CONV_W - 1), CONV_W - 1), :]


def _ssd_sample_kernel(u_ref, dtraw_ref, s0_ref, buf_ref, cw_ref, cb_ref, dtb_ref, alog_ref, d_ref,
                       g_ref, y_ref, s_out_ref, buf_out_ref, *, seqs):
    rows = u_ref.shape[0]
    u = u_ref[...]
    conv, new_buf = _conv_taps_seq4(u[:, GROUP_W:], buf_ref[...], cw_ref, cb_ref)
    buf_out_ref[...] = new_buf
    mask = _seg_causal_mask(rows, 2)
    xs, bm, cm, dt_x, cum_x, cum_t, cum_full = _ssd_common(u, dtraw_ref, dtb_ref, alog_ref, conv, 2)
    v = xs * dt_x
    ecum_x = jnp.exp(cum_x)
    pick_last = (_iota((rows, rows), 1) == (_iota((rows, rows), 0) | 3)).astype(BF16)
    last_x = _sel_dot(pick_last, cum_x)
    vend = v * jnp.exp(last_x - cum_x)
    row_seq = _iota((rows, 128), 0) >> 2
    ys = []
    for g in range(SSD_G):
        ls = slice(g * 128, (g + 1) * 128)
        q = cm[:, ls]
        k = bm[:, ls]
        y_g = _ssd_intra(q, k, v[:, ls], cum_x[:, ls], cum_t[2 * g:2 * g + 2, :], mask)
        y_int = jnp.zeros((rows, 128), F32)
        for i in range(seqs):
            in_seq = row_seq == i
            s_prev = jnp.concatenate([s0_ref[i, 2 * g], s0_ref[i, 2 * g + 1]], axis=0)
            y_int = jnp.where(in_seq, _bdot_nt(q, s_prev), y_int)
            ds = _bdot_tn(jnp.where(in_seq, vend[:, ls], 0.0), k)
            for hh in range(2):
                h = 2 * g + hh
                decay = jnp.exp(cum_full[4 * i + 3:4 * i + 4, h * 128:(h + 1) * 128])
                s_out_ref[i, h] = decay * s0_ref[i, h] + ds[hh * HEAD:(hh + 1) * HEAD, :]
        ys.append(y_g + y_int * ecum_x[:, ls])
    y_ref[...] = _ssd_finish(jnp.concatenate(ys, axis=-1), xs, u[:, 0:GROUP_W], d_ref, g_ref)


def _ssd(u, dtraw, state, buf, cw, cb, dtb, alog, d_x, gain, n_seq, seq_len, layer):
    n = u.shape[0]
    cch = cw.shape[1]
    vec = lambda wd: _const_spec((1, wd))
    common_in = [_const_spec(cw.shape), vec(cch), vec(128), vec(128), vec(GROUP_W), vec(GROUP_W)]
    if state is None:
        nt = seq_len // MIX_TILE
        row = lambda wd: pl.BlockSpec((MIX_TILE, wd), lambda b, t: (b * nt + t, 0))
        return pl.pallas_call(
            _ssd_prompt_kernel,
            grid=(n_seq, nt),
            in_specs=[row(u.shape[1]), row(128)] + common_in,
            out_specs=[row(GROUP_W),
                       pl.BlockSpec((1, N_HEADS, HEAD, SSD_N), lambda b, t: (b, 0, 0, 0)),
                       pl.BlockSpec((1, CONV_W - 1, cch), lambda b, t: (b, 0, 0))],
            out_shape=[jax.ShapeDtypeStruct((n, GROUP_W), F32),
                       jax.ShapeDtypeStruct((n_seq, N_HEADS, HEAD, SSD_N), F32),
                       jax.ShapeDtypeStruct((n_seq, CONV_W - 1, cch), F32)],
            scratch_shapes=[pltpu.VMEM((GROUP_W, SSD_N), F32), pltpu.VMEM((MIX_TILE + 8, cch), F32),
                            pltpu.VMEM((MIX_TILE, GROUP_W), F32)],
            compiler_params=_params(("parallel", "arbitrary")),
            name="ssd_prompt",
        )(u, dtraw, cw, cb, dtb, alog, d_x, gain)
    rows = SAMPLE_SEQS * seq_len
    row = lambda wd: pl.BlockSpec((rows, wd), lambda i: (i, 0))
    st = pl.BlockSpec((SAMPLE_SEQS, N_HEADS, HEAD, SSD_N), lambda i: (i, 0, 0, 0))
    st_in = pl.BlockSpec((None, SAMPLE_SEQS, N_HEADS, HEAD, SSD_N), lambda i: (layer, i, 0, 0, 0))
    return pl.pallas_call(
        functools.partial(_ssd_sample_kernel, seqs=SAMPLE_SEQS),
        grid=(n_seq // SAMPLE_SEQS,),
        in_specs=[row(u.shape[1]), row(128), st_in, row(cch)] + common_in,
        out_specs=[row(GROUP_W), st, row(cch)],
        out_shape=[jax.ShapeDtypeStruct((n, GROUP_W), F32),
                   jax.ShapeDtypeStruct(state.shape[1:], F32),
                   jax.ShapeDtypeStruct((n, cch), F32)],
        compiler_params=_params(("parallel",)),
        name="ssd_sample",
    )(u, dtraw, state, buf, cw, cb, dtb, alog, d_x, gain)


def _rope(x, cos, sin_signed):
    lane = _iota(x.shape, 1)
    swapped = jnp.where((lane & 63) < 32, pltpu.roll(x, GROUP_W - 32, 1), pltpu.roll(x, 32, 1))
    return x * cos + swapped * sin_signed


def _rope_split(x, cos, sin_signed):
    swapped = jnp.concatenate([x[:, 128:], x[:, :128]], axis=1)
    return x * cos + swapped * sin_signed


def _ret_qkv(u_ref, cos_ref, sin_ref, rope=_rope):
    u = u_ref[...]
    q = rope(u[:, 0:GROUP_W], cos_ref[...], sin_ref[...])
    k = rope(u[:, GROUP_W:2 * GROUP_W], cos_ref[...], sin_ref[...]) * (HEAD ** -0.5)
    return q, k, u[:, 2 * GROUP_W:3 * GROUP_W], u[:, 3 * GROUP_W:4 * GROUP_W]


def _ret_prompt_kernel(u_ref, cos_ref, sin_ref, ecum_ref, eend_ref, elast_ref, dec_ref, g_ref,
                       y_ref, s_out_ref, s_ref, y_scr):
    t = pl.program_id(1)
    rows = u_ref.shape[0]

    @pl.when(t == 0)
    def _():
        s_ref[...] = jnp.zeros(s_ref.shape, F32)

    q, k, v, gate = _ret_qkv(u_ref, cos_ref, sin_ref, _rope_split)
    r_i = _iota((GROUP_W, GROUP_W), 0)
    c_i = _iota((GROUP_W, GROUP_W), 1)
    kmask01 = ((r_i >> 6) == ((c_i & 127) >> 5)).astype(BF16)
    vmask01 = _head_block_mask().astype(BF16)
    smask = ((r_i & 127) >> 5) == (c_i >> 6)
    chunks = [slice(c * CHUNK, (c + 1) * CHUNK) for c in range(rows // CHUNK)]
    a_wide = [_bdot_nt(q[rs], _block_diag_rows(k[rs], kmask01)) * dec_ref[...] for rs in chunks]
    ds = [jnp.where(smask, _bdot_tn(k[rs], v[rs] * eend_ref[...]), 0.0) for rs in chunks]
    s = s_ref[...]
    s_in = []
    for d in ds:
        s_in.append(s.astype(BF16))
        s = elast_ref[...] * s + d
    s_ref[...] = s
    for rs, a, s_c in zip(chunks, a_wide, s_in):
        y_intra = _bdot(a, _block_diag_rows(v[rs], vmask01))
        y_scr[rs, :] = y_intra + _bdot(q[rs], s_c) * ecum_ref[...]
    y_ref[...] = _silu(gate) * (_head_stat(y_scr[...], True) * g_ref[...])

    @pl.when(t == pl.num_programs(1) - 1)
    def _():
        half = HEAD // 2
        for h in range(N_HEADS):
            vs = slice(h * HEAD, (h + 1) * HEAD)
            s_out_ref[0, h, 0:half, :] = s_ref[h * half:(h + 1) * half, vs]
            s_out_ref[0, h, half:HEAD, :] = s_ref[128 + h * half:128 + (h + 1) * half, vs]


def _ret_sample_kernel(u_ref, cos_ref, sin_ref, s0_ref, ecum_ref, eend_ref, elast_ref, dec_ref, g_ref,
                       y_ref, s_out_ref, *, seqs):
    rows = u_ref.shape[0]
    q, k, v, gate = _ret_qkv(u_ref, cos_ref, sin_ref)
    vend = v * eend_ref[...]
    row_seq = _iota((rows, HEAD), 0) >> 2
    ys = []
    for h in range(N_HEADS):
        ls = slice(h * HEAD, (h + 1) * HEAD)
        qh = q[:, ls]
        kh = k[:, ls]
        y_h = _bdot(_bdot_nt(qh, kh) * dec_ref[h], v[:, ls])
        y_int = jnp.zeros((rows, HEAD), F32)
        for i in range(seqs):
            in_seq = row_seq == i
            s_prev = s0_ref[i, h]
            y_int = jnp.where(in_seq, _bdot(qh, s_prev), y_int)
            ds = _bdot_tn(kh, jnp.where(in_seq, vend[:, ls], 0.0))
            s_out_ref[i, h] = elast_ref[:, ls] * s_prev + ds
        ys.append(y_h + y_int * ecum_ref[:, ls])
    y_ref[...] = _silu(gate) * (_head_stat(jnp.concatenate(ys, axis=-1), True) * g_ref[...])


def _ret_tables(block, seg):
    log_gamma = jnp.log(1.0 - jnp.exp2(-5.0 - jnp.arange(N_HEADS, dtype=F32)))
    pos = (jnp.arange(block) % seg).astype(F32)
    cum = (pos[:, None] + 1.0) * log_gamma[None, :]
    last = seg * log_gamma
    rep = lambda a: jnp.repeat(a, HEAD, axis=-1)
    same = (jnp.arange(block)[:, None] // seg) == (jnp.arange(block)[None, :] // seg)
    mask = same & (jnp.arange(block)[None, :] <= jnp.arange(block)[:, None])
    dec = jnp.exp(jnp.where(mask[None], cum.T[:, :, None] - cum.T[:, None, :], -jnp.inf))
    return (rep(jnp.exp(cum)), rep(jnp.exp(last[None, :] - cum)), rep(jnp.exp(last)[None, :]), dec)


def _ret(u, cos, sin_signed, state, gain, n_seq, seq_len, layer):
    n = u.shape[0]
    vec = _const_spec((1, GROUP_W))
    if state is None:
        nt = seq_len // MIX_TILE
        ecum, eend, elast, dec = _ret_tables(CHUNK, CHUNK)
        dec = dec.transpose(1, 0, 2).reshape(CHUNK, N_HEADS * CHUNK)
        row = lambda wd: pl.BlockSpec((MIX_TILE, wd), lambda b, t: (b * nt + t, 0))
        pos = pl.BlockSpec((MIX_TILE, GROUP_W), lambda b, t: (t, 0))
        return pl.pallas_call(
            _ret_prompt_kernel,
            grid=(n_seq, nt),
            in_specs=[row(u.shape[1]), pos, pos, _const_spec(ecum.shape), _const_spec(eend.shape), vec,
                      _const_spec(dec.shape), vec],
            out_specs=[row(GROUP_W), pl.BlockSpec((1, N_HEADS, HEAD, HEAD), lambda b, t: (b, 0, 0, 0))],
            out_shape=[jax.ShapeDtypeStruct((n, GROUP_W), F32),
                       jax.ShapeDtypeStruct((n_seq, N_HEADS, HEAD, HEAD), F32)],
            scratch_shapes=[pltpu.VMEM((GROUP_W, GROUP_W), F32), pltpu.VMEM((MIX_TILE, GROUP_W), F32)],
            compiler_params=_params(("parallel", "arbitrary")),
            name="ret_prompt",
        )(u, cos, sin_signed, ecum, eend, elast, dec, gain)
    rows = SAMPLE_SEQS * seq_len
    ecum, eend, elast, dec = _ret_tables(rows, seq_len)
    row = lambda wd: pl.BlockSpec((rows, wd), lambda i: (i, 0))
    st = pl.BlockSpec((SAMPLE_SEQS, N_HEADS, HEAD, HEAD), lambda i: (i, 0, 0, 0))
    cos_t = jnp.tile(cos, (SAMPLE_SEQS, 1))
    sin_t = jnp.tile(sin_signed, (SAMPLE_SEQS, 1))
    return pl.pallas_call(
        functools.partial(_ret_sample_kernel, seqs=SAMPLE_SEQS),
        grid=(n_seq // SAMPLE_SEQS,),
        in_specs=[row(u.shape[1]), _const_spec(cos_t.shape), _const_spec(sin_t.shape),
                  pl.BlockSpec((None, SAMPLE_SEQS, N_HEADS, HEAD, HEAD), lambda i: (layer, i, 0, 0, 0)),
                  _const_spec(ecum.shape), _const_spec(eend.shape), vec, _const_spec(dec.shape), vec],
        out_specs=[row(GROUP_W), st],
        out_shape=[jax.ShapeDtypeStruct((n, GROUP_W), F32), jax.ShapeDtypeStruct(state.shape[1:], F32)],
        compiler_params=_params(("parallel",)),
        name="ret_sample",
    )(u, cos_t, sin_t, state, ecum, eend, elast, dec, gain)


def _rg_gates(xr, wa_ref, ba_ref, wx_ref, bx_ref, lam_ref):
    r_gate = _sigmoid(_bdot(xr, wa_ref[...]) + ba_ref[...])
    i_gate = _sigmoid(_bdot(xr, wx_ref[...]) + bx_ref[...])
    log_a = (-RG_C * _softplus(-lam_ref[...])) * r_gate
    a = jnp.exp(log_a)
    b = jnp.sqrt(-jnp.tanh(log_a) * (jnp.exp(2.0 * log_a) + 1.0)) * (i_gate * xr)
    return a, b


def _rg_scan(a, b, seg):
    pos = _iota(a.shape, 0) & (seg - 1)
    d = 1
    while d < seg:
        ok = pos >= d
        a_sh = jnp.where(ok, pltpu.roll(a, d, 0), 1.0)
        b_sh = jnp.where(ok, pltpu.roll(b, d, 0), 0.0)
        b = a * b_sh + b
        a = a * a_sh
        d *= 2
    return a, b


def _rg_prompt_kernel(u_ref, cw_ref, cb_ref, wa_ref, ba_ref, wx_ref, bx_ref, lam_ref,
                      y_ref, h_out_ref, buf_out_ref, h_ref, xx_ref):
    t = pl.program_id(1)
    rows = u_ref.shape[0]

    @pl.when(t == 0)
    def _():
        h_ref[...] = jnp.zeros(h_ref.shape, F32)

    u = u_ref[...]
    xr = _conv_taps_carry(u[:, 0:GROUP_W], xx_ref, cw_ref, cb_ref, t == 0)
    a, b = _rg_gates(xr, wa_ref, ba_ref, wx_ref, bx_ref, lam_ref)
    a_cum, h_loc = _rg_scan(a, b, rows)
    hseq = h_loc + a_cum * h_ref[0:1, :]
    h_ref[...] = jnp.broadcast_to(hseq[rows - 1:rows, :], h_ref.shape)
    y_ref[...] = _gelu_tanh(u[:, GROUP_W:]) * hseq

    @pl.when(t == pl.num_programs(1) - 1)
    def _():
        h_out_ref[0] = hseq[rows - 1:rows, :]
        buf_out_ref[0] = xx_ref[pl.ds(8 - (CONV_W - 1), CONV_W - 1), :]


def _rg_sample_kernel(u_ref, h0_ref, buf_ref, cw_ref, cb_ref, wa_ref, ba_ref, wx_ref, bx_ref, lam_ref,
                      y_ref, h_out_ref, buf_out_ref):
    u = u_ref[...]
    xr, new_buf = _conv_taps_seq4(u[:, 0:GROUP_W], buf_ref[...], cw_ref, cb_ref)
    buf_out_ref[...] = new_buf
    a, b = _rg_gates(xr, wa_ref, ba_ref, wx_ref, bx_ref, lam_ref)
    a_cum, h_loc = _rg_scan(a, b, 4)
    hseq = h_loc + a_cum * h0_ref[...]
    h_out_ref[...] = hseq
    y_ref[...] = _gelu_tanh(u[:, GROUP_W:]) * hseq


def _rg(u, h0_rows, buf, cw, cb, wa, ba, wx, bx, lam, n_seq, seq_len):
    n = u.shape[0]
    vec = _const_spec((1, GROUP_W))
    common_in = [_const_spec(cw.shape), vec, _const_spec(wa.shape), vec, _const_spec(wx.shape), vec, vec]
    if h0_rows is None:
        nt = seq_len // RG_TILE
        row = lambda wd: pl.BlockSpec((RG_TILE, wd), lambda b, t: (b * nt + t, 0))
        return pl.pallas_call(
            _rg_prompt_kernel,
            grid=(n_seq, nt),
            in_specs=[row(u.shape[1])] + common_in,
            out_specs=[row(GROUP_W), pl.BlockSpec((1, 1, GROUP_W), lambda b, t: (b, 0, 0)),
                       pl.BlockSpec((1, CONV_W - 1, GROUP_W), lambda b, t: (b, 0, 0))],
            out_shape=[jax.ShapeDtypeStruct((n, GROUP_W), F32),
                       jax.ShapeDtypeStruct((n_seq, 1, GROUP_W), F32),
                       jax.ShapeDtypeStruct((n_seq, CONV_W - 1, GROUP_W), F32)],
            scratch_shapes=[pltpu.VMEM((8, GROUP_W), F32), pltpu.VMEM((RG_TILE + 8, GROUP_W), F32)],
            compiler_params=_params(("parallel", "arbitrary")),
            name="rglru_prompt",
        )(u, cw, cb, wa, ba, wx, bx, lam)
    rows = SAMPLE_SEQS * seq_len
    row = lambda wd: pl.BlockSpec((rows, wd), lambda i: (i, 0))
    return pl.pallas_call(
        _rg_sample_kernel,
        grid=(n_seq // SAMPLE_SEQS,),
        in_specs=[row(u.shape[1]), row(GROUP_W), row(GROUP_W)] + common_in,
        out_specs=[row(GROUP_W)] * 3,
        out_shape=[jax.ShapeDtypeStruct((n, GROUP_W), F32)] * 3,
        compiler_params=_params(("parallel",)),
        name="rglru_sample",
    )(u, h0_rows, buf, cw, cb, wa, ba, wx, bx, lam)


def _hg_inputs(u_ref, lb_ref, seg_shift):
    u = u_ref[...]
    rows = u.shape[0]
    lb = lb_ref[...]
    fg = lb + (1.0 - lb) * _sigmoid(u[:, GROUP_W:2 * GROUP_W])
    q = _silu(u[:, 0:GROUP_W])
    k = 1.0 - fg
    v = u[:, 2 * GROUP_W:3 * GROUP_W]
    cum = _seg_cumsum(jnp.log(fg), seg_shift)
    return q, k, v, cum, u[:, 3 * GROUP_W:4 * GROUP_W]


def _hg_intra(q, k, v, cum, kk_ref, cc_ref, vv_ref, seg):
    rows = q.shape[0]
    pad = kk_ref.shape[0] - rows
    for ref, val in ((kk_ref, k), (cc_ref, cum), (vv_ref, v)):
        ref[0:pad, :] = jnp.zeros((pad, GROUP_W), F32)
        ref[pad:pad + rows, :] = val
    pos = _iota((rows, GROUP_W), 0) & (seg - 1)
    ones_bd = ((_iota((GROUP_W, GROUP_W), 0) >> 6) == (_iota((GROUP_W, GROUP_W), 1) >> 6)).astype(BF16)
    y = jnp.dot((q * k).astype(BF16), ones_bd, preferred_element_type=F32) * v
    for d in range(1, seg):
        k_d = kk_ref[pl.ds(pad - d, rows), :]
        c_d = cc_ref[pl.ds(pad - d, rows), :]
        v_d = vv_ref[pl.ds(pad - d, rows), :]
        p = jnp.where(pos >= d, q * k_d * jnp.exp(cum - c_d), 0.0)
        y = y + jnp.dot(p.astype(BF16), ones_bd, preferred_element_type=F32) * v_d
    return y


def _hg_prompt_kernel(u_ref, lb_ref, g_ref, y_ref, s_out_ref, st_ref, y_scr):
    t = pl.program_id(1)
    rows = u_ref.shape[0]

    @pl.when(t == 0)
    def _():
        st_ref[...] = jnp.zeros(st_ref.shape, F32)

    q, k, v, cum, gate = _hg_inputs(u_ref, lb_ref, 6)
    hmask = _head_block_mask()
    hmask01 = hmask.astype(BF16)
    ones_bd = hmask01
    y_scr[...] = jnp.dot((q * k).astype(BF16), ones_bd, preferred_element_type=F32) * v
    row = _iota((rows, GROUP_W), 0)
    levels = []
    blk_last = cum
    h = 1
    while h < CHUNK:
        upper = (row & h) != 0
        qn = jnp.where(upper, q * jnp.exp(cum - pltpu.roll(blk_last, h, 0)), 0.0)
        kn = jnp.where(upper, 0.0, k * jnp.exp(blk_last - cum))
        levels.append((h, qn, kn))
        blk_last = jnp.where(upper, blk_last, pltpu.roll(blk_last, rows - h, 0))
        h *= 2
    q_in = q * jnp.exp(cum)
    k_end = k * jnp.exp(blk_last - cum)
    trow = _iota((CHUNK, GROUP_W), 0)
    tcol = _iota((CHUNK, GROUP_W), 1) & (CHUNK - 1)
    n_chunks = rows // CHUNK
    a_wide, ds_t = [], []
    for c in range(n_chunks):
        rs = slice(c * CHUNK, (c + 1) * CHUNK)
        a_c = jnp.zeros((CHUNK, GROUP_W), F32)
        for h, qn, kn in levels:
            g = _bdot_nt(qn[rs], _block_diag_rows(kn[rs], hmask01))
            sh = (2 * h).bit_length() - 1
            a_c = a_c + jnp.where((trow >> sh) == (tcol >> sh), g, 0.0)
        a_wide.append(a_c)
        ds_t.append(jnp.where(hmask, _bdot_tn(v[rs], k_end[rs]), 0.0))
    st = st_ref[...]
    st_in = []
    for c in range(n_chunks):
        st_in.append(st.astype(BF16))
        st = jnp.exp(blk_last[c * CHUNK:c * CHUNK + 1, :]) * st + ds_t[c]
    st_ref[...] = st
    for c in range(n_chunks):
        rs = slice(c * CHUNK, (c + 1) * CHUNK)
        y_c = _bdot(a_wide[c], _block_diag_rows(v[rs], hmask01)) + _bdot_nt(q_in[rs], st_in[c])
        y_scr[rs, :] = y_scr[rs, :] + y_c
    y_ref[...] = _silu(gate) * (_head_stat(y_scr[...], False) * g_ref[...])

    @pl.when(t == pl.num_programs(1) - 1)
    def _():
        eye = (_iota((HEAD, HEAD), 0) == _iota((HEAD, HEAD), 1)).astype(BF16)
        for h in range(N_HEADS):
            s_out_ref[0, h] = _sel_dot_nt(eye, st_ref[h * HEAD:(h + 1) * HEAD, h * HEAD:(h + 1) * HEAD])


def _hg_sample_kernel(u_ref, s0_ref, lb_ref, g_ref, y_ref, s_out_ref, kk_ref, cc_ref, vv_ref, *, seqs):
    rows = u_ref.shape[0]
    q, k, v, cum, gate = _hg_inputs(u_ref, lb_ref, 2)
    y = _hg_intra(q, k, v, cum, kk_ref, cc_ref, vv_ref, 4)
    pick_last = (_iota((rows, rows), 1) == (_iota((rows, rows), 0) | 3)).astype(BF16)
    last = _sel_dot(pick_last, cum)
    qn = q * jnp.exp(cum)
    kend = k * jnp.exp(last - cum)
    elast = jnp.exp(last)
    row_seq = _iota((rows, HEAD), 0) >> 2
    eye = _iota((HEAD, HEAD), 0) == _iota((HEAD, HEAD), 1)
    y_int = []
    for h in range(N_HEADS):
        ls = slice(h * HEAD, (h + 1) * HEAD)
        acc = jnp.zeros((rows, HEAD), F32)
        for i in range(seqs):
            in_seq = row_seq == i
            s_prev = s0_ref[i, h]
            acc = jnp.where(in_seq, _bdot(qn[:, ls], s_prev), acc)
            ds = _bdot_tn(jnp.where(in_seq, kend[:, ls], 0.0), v[:, ls])
            a_col = jnp.sum(jnp.where(eye, elast[4 * i:4 * i + 1, ls], 0.0), axis=-1, keepdims=True)
            s_out_ref[i, h] = a_col * s_prev + ds
        y_int.append(acc)
    y = y + jnp.concatenate(y_int, axis=-1)
    y_ref[...] = _silu(gate) * (_head_stat(y, False) * g_ref[...])


def _hg(u, state, lb, gain, n_seq, seq_len, layer):
    n = u.shape[0]
    vec = _const_spec((1, GROUP_W))
    if state is None:
        nt = seq_len // MIX_TILE
        row = lambda wd: pl.BlockSpec((MIX_TILE, wd), lambda b, t: (b * nt + t, 0))
        return pl.pallas_call(
            _hg_prompt_kernel,
            grid=(n_seq, nt),
            in_specs=[row(u.shape[1]), vec, vec],
            out_specs=[row(GROUP_W), pl.BlockSpec((1, N_HEADS, HEAD, HEAD), lambda b, t: (b, 0, 0, 0))],
            out_shape=[jax.ShapeDtypeStruct((n, GROUP_W), F32),
                       jax.ShapeDtypeStruct((n_seq, N_HEADS, HEAD, HEAD), F32)],
            scratch_shapes=[pltpu.VMEM((GROUP_W, GROUP_W), F32), pltpu.VMEM((MIX_TILE, GROUP_W), F32)],
            compiler_params=_params(("parallel", "arbitrary")),
            name="hgrn_prompt",
        )(u, lb, gain)
    rows = SAMPLE_SEQS * seq_len
    row = lambda wd: pl.BlockSpec((rows, wd), lambda i: (i, 0))
    st = pl.BlockSpec((SAMPLE_SEQS, N_HEADS, HEAD, HEAD), lambda i: (i, 0, 0, 0))
    shifted = pltpu.VMEM((rows + 8, GROUP_W), F32)
    return pl.pallas_call(
        functools.partial(_hg_sample_kernel, seqs=SAMPLE_SEQS),
        grid=(n_seq // SAMPLE_SEQS,),
        in_specs=[row(u.shape[1]),
                  pl.BlockSpec((None, SAMPLE_SEQS, N_HEADS, HEAD, HEAD), lambda i: (layer, i, 0, 0, 0)), vec, vec],
        out_specs=[row(GROUP_W), st],
        out_shape=[jax.ShapeDtypeStruct((n, GROUP_W), F32), jax.ShapeDtypeStruct(state.shape[1:], F32)],
        scratch_shapes=[shifted, shifted, shifted],
        compiler_params=_params(("parallel",)),
        name="hgrn_sample",
    )(u, state, lb, gain)


def _block_diag(w):
    h, i, j = w.shape
    eye = jnp.eye(h, dtype=w.dtype)
    return (eye[:, None, :, None] * w[:, :, None, :]).reshape(h * i, h * j)


def _rope_tables(pos0, seq_len, split_halves):
    half = HEAD // 2
    pos = pos0 + jnp.arange(seq_len, dtype=F32)
    inv = ROPE_BASE ** (-jnp.arange(half, dtype=F32) / half)
    ang = pos[:, None] * inv
    cos = jnp.tile(jnp.cos(ang), (1, 2 * N_HEADS))
    sin = jnp.sin(ang)
    if split_halves:
        sin_signed = jnp.concatenate([jnp.tile(-sin, (1, N_HEADS)), jnp.tile(sin, (1, N_HEADS))], axis=-1)
    else:
        sin_signed = jnp.tile(jnp.concatenate([-sin, sin], axis=-1), (1, N_HEADS))
    return cos, sin_signed


def _layer_weights(P, l):
    row = lambda a: a.reshape(1, -1)
    pad128 = lambda a: jnp.pad(a, (0, 128 - a.shape[0])).reshape(1, 128)
    lb_sm = jax.nn.softmax(P['hg_lb'].astype(F32), axis=0)
    lb = (jnp.cumsum(lb_sm, axis=0) - lb_sm[0])[l]
    return dict(
        w_in=P['w_in_prepped'],
        ln_mix_pre=row(P['ln_mix_pre'][l]), ln_mix_post=row(P['ln_mix_post'][l]),
        ln_xa_pre=row(P['ln_xa_pre'][l]), ln_xa_post=row(P['ln_xa_post'][l]),
        ln_ffn_pre=row(P['ln_ffn_pre'][l]), ln_ffn_post=row(P['ln_ffn_post'][l]),
        ssd_cw=P['ssd_conv_w'][l], ssd_cb=row(P['ssd_conv_b'][l]),
        ssd_dtb=pad128(P['ssd_dt_bias'][l]), ssd_alog=pad128(P['ssd_A_log'][l]),
        ssd_d=row(jnp.repeat(P['ssd_D'][l], HEAD)), ssd_norm=row(P['ssd_norm'][l]),
        ret_norm=row(P['ret_norm'][l]),
        rg_cw=P['rg_conv_w'][l], rg_cb=row(P['rg_conv_b'][l]),
        rg_wa=_block_diag(P['rg_wa'][l]).astype(BF16), rg_ba=row(P['rg_ba'][l]),
        rg_wx=_block_diag(P['rg_wx'][l]).astype(BF16), rg_bx=row(P['rg_bx'][l]),
        rg_lam=row(P['rg_lambda'][l]),
        hg_lb=row(lb), hg_norm=row(P['hg_norm'][l]),
        w_out=P['w_out_bf16'], w_xq=P['w_xq_bf16'], w_xo=P['w_xo_bf16'], w_gu=P['w_gu_bf16'],
        w_down=P['w_down_bf16'], ln_mem=row(P['ln_mem'][l]), w_xkv=P['w_xkv_bf16'],
    )


def _layer(x, W, k_mem, v_mem, st, n_seq, seq_len, pos0, layer):
    ret_cols = IN_COLS_RET_SPLIT if st is None else IN_COLS_RET
    u_ssd, u_ret, u_rg, u_hg, u_dt = _norm_matmul(
        x, W['ln_mix_pre'], W['w_in'], (IN_COLS_SSD, ret_cols, IN_COLS_RG, IN_COLS_HG, IN_COLS_DT), "in_proj",
        layer=layer)
    cos, sin_signed = _rope_tables(pos0, seq_len, split_halves=st is None)
    if st is None:
        y_ssd, s_ssd, b_ssd = _ssd(u_ssd, u_dt, None, None, W['ssd_cw'], W['ssd_cb'], W['ssd_dtb'],
                                   W['ssd_alog'], W['ssd_d'], W['ssd_norm'], n_seq, seq_len, layer)
        y_ret, s_ret = _ret(u_ret, cos, sin_signed, None, W['ret_norm'], n_seq, seq_len, layer)
        y_rg, h_rg, b_rg = _rg(u_rg, None, None, W['rg_cw'], W['rg_cb'], W['rg_wa'], W['rg_ba'],
                               W['rg_wx'], W['rg_bx'], W['rg_lam'], n_seq, seq_len)
        h_rg = h_rg.reshape(n_seq, GROUP_W)
        y_hg, s_hg = _hg(u_hg, None, W['hg_lb'], W['hg_norm'], n_seq, seq_len, layer)
    else:
        ssd_s, ssd_buf, ret_s, rg_h, rg_buf, hg_s = st
        pad_rows = lambda b: jnp.pad(b, ((0, 0), (0, 1), (0, 0))).reshape(n_seq * 4, b.shape[-1])
        y_ssd, s_ssd, b_ssd = _ssd(u_ssd, u_dt, ssd_s, pad_rows(ssd_buf), W['ssd_cw'], W['ssd_cb'],
                                   W['ssd_dtb'], W['ssd_alog'], W['ssd_d'], W['ssd_norm'], n_seq, seq_len, layer)
        b_ssd = b_ssd.reshape(n_seq, 4, -1)[:, :CONV_W - 1]
        y_ret, s_ret = _ret(u_ret, cos, sin_signed, ret_s, W['ret_norm'], n_seq, seq_len, layer)
        y_rg, h_rows, b_rg = _rg(u_rg, jnp.repeat(rg_h, seq_len, axis=0), pad_rows(rg_buf), W['rg_cw'],
                                 W['rg_cb'], W['rg_wa'], W['rg_ba'], W['rg_wx'], W['rg_bx'], W['rg_lam'],
                                 n_seq, seq_len)
        h_rg = h_rows.reshape(n_seq, seq_len, GROUP_W)[:, seq_len - 1]
        b_rg = b_rg.reshape(n_seq, 4, -1)[:, :CONV_W - 1]
        y_hg, s_hg = _hg(u_hg, hg_s, W['hg_lb'], W['hg_norm'], n_seq, seq_len, layer)
    ys = (y_ssd, y_ret, y_rg, y_hg)
    if st is None:
        x3 = _xattn_ffn(ys, x, W['w_out'], W['ln_mix_post'], k_mem, v_mem, W['ln_xa_pre'], W['w_xq'], W['w_xo'],
                        W['ln_xa_post'], W['ln_ffn_pre'], W['w_gu'], W['w_down'], W['ln_ffn_post'], seq_len,
                        layer)
    else:
        x1, q = _out_proj(ys, x, W['w_out'], W['ln_mix_post'], W['ln_xa_pre'], W['w_xq'], layer)
        o = _attention_sample(q, k_mem, v_mem, layer, seq_len)
        x3 = _ffn(o, x1, W['w_xo'], W['ln_xa_post'], W['ln_ffn_pre'], W['w_gu'], W['w_down'],
                  W['ln_ffn_post'], layer)
    return x3, (s_ssd, b_ssd, s_ret, h_rg, b_rg, s_hg)


def kernel(x_prompt, x_sample, state_ssd, state_ssd_conv, state_ret, state_rglru, state_rglru_conv, state_hgrn, cache_mem_k, cache_mem_v, mem_prompt, ln_mix_pre, ln_mix_post, ln_xa_pre, ln_xa_post, ln_ffn_pre, ln_ffn_post, w_in, ssd_conv_w, ssd_conv_b, ssd_dt_bias, ssd_A_log, ssd_D, ssd_norm, ret_norm, rg_conv_w, rg_conv_b, rg_wa, rg_ba, rg_wx, rg_bx, rg_lambda, hg_lb, hg_norm, w_out, ln_mem, w_xq, w_xkv, w_xo, w_gu, w_down):
    P = dict(ln_mix_pre=ln_mix_pre, ln_mix_post=ln_mix_post, ln_xa_pre=ln_xa_pre, ln_xa_post=ln_xa_post,
             ln_ffn_pre=ln_ffn_pre, ln_ffn_post=ln_ffn_post, w_in=w_in, ssd_conv_w=ssd_conv_w,
             ssd_conv_b=ssd_conv_b, ssd_dt_bias=ssd_dt_bias, ssd_A_log=ssd_A_log, ssd_D=ssd_D,
             ssd_norm=ssd_norm, ret_norm=ret_norm, rg_conv_w=rg_conv_w, rg_conv_b=rg_conv_b,
             rg_wa=rg_wa, rg_ba=rg_ba, rg_wx=rg_wx, rg_bx=rg_bx, rg_lambda=rg_lambda, hg_lb=hg_lb,
             hg_norm=hg_norm, w_out=w_out, ln_mem=ln_mem, w_xq=w_xq, w_xkv=w_xkv, w_xo=w_xo,
             w_gu=w_gu, w_down=w_down)
    P['w_in_prepped'] = _in_weight_prep(w_in)
    for name in ('w_out', 'w_xq', 'w_xo', 'w_gu', 'w_down', 'w_xkv'):
        P[name + '_bf16'] = P[name].astype(BF16)
    ssd_t = jnp.swapaxes(state_ssd, -1, -2)
    depth = w_in.shape[0]
    bp, tp, d = x_prompt.shape
    bs, ts, _ = x_sample.shape
    n_mem = mem_prompt.shape[1]
    y_p = x_prompt.reshape(bp * tp, d)
    y_s = x_sample.reshape(bs * ts, d)
    mem = mem_prompt.reshape(bp * n_mem, d)
    p_st, s_st, p_mk, p_mv = [], [], [], []
    for l in range(depth):
        W = _layer_weights(P, l)
        mk, mv = _norm_matmul(mem, W['ln_mem'], W['w_xkv'], (((0, d),), ((d, 2 * d),)), "mem_kv", layer=l)
        mk = mk.reshape(bp, n_mem, d)
        mv = mv.reshape(bp, n_mem, d)
        y_p, st_p = _layer(y_p, W, mk, mv, None, bp, tp, 0.0, l)
        p_st.append(st_p)
        p_mk.append(mk.reshape(bp, n_mem, XA_H, XA_HD))
        p_mv.append(mv.reshape(bp, n_mem, XA_H, XA_HD))
        st_l = (ssd_t, state_ssd_conv[l], state_ret, state_rglru[l], state_rglru_conv[l], state_hgrn)
        y_s, st_s = _layer(y_s, W, cache_mem_k, cache_mem_v, st_l, bs, ts, float(PAST_LEN), l)
        s_st.append(st_s)
    stack = lambda sts, i: jnp.stack([s[i] for s in sts], axis=0)
    return (y_p.reshape(bp, tp, d), y_s.reshape(bs, ts, d),
            jnp.swapaxes(stack(p_st, 0), -1, -2), stack(p_st, 1), stack(p_st, 2), stack(p_st, 3), stack(p_st, 4),
            stack(p_st, 5),
            jnp.stack(p_mk, axis=0), jnp.stack(p_mv, axis=0),
            jnp.swapaxes(stack(s_st, 0), -1, -2), stack(s_st, 1), stack(s_st, 2), stack(s_st, 3), stack(s_st, 4),
            stack(s_st, 5))
```

```python
import functools
import math

import jax
import jax.numpy as jnp
from jax import lax
from jax.experimental import pallas as pl
from jax.experimental.pallas import tpu as pltpu

F32 = jnp.float32
BF16 = jnp.bfloat16
EPS = 1e-6

GROUP_W = 256
HEAD = 64
N_HEADS = 4
SSD_N = 128
SSD_G = 2
CONV_W = 4
XA_H = 4
XA_HD = 256
ROPE_BASE = 10000.0
RG_C = 8.0
CHUNK = 64
HG_CHUNK = 16
PAST_LEN = 16384

ROW_TILE = 512
MIX_TILE = 1024
RG_TILE = 256
SAMPLE_SEQS = 16
ATT_TILE = 512
ATT_SEQS = 4
FF_TILE = 256
VMEM_LIMIT = 56 * 1024 * 1024


def _bdot(a, b):
    return jnp.dot(a.astype(BF16), b.astype(BF16), preferred_element_type=F32)


def _bdot_nt(a, b):
    return lax.dot_general(a.astype(BF16), b.astype(BF16), (((1,), (1,)), ((), ())),
                           preferred_element_type=F32)


def _bdot_tn(a, b):
    return lax.dot_general(a.astype(BF16), b.astype(BF16), (((0,), (0,)), ((), ())),
                           preferred_element_type=F32)


def _split3(x):
    hi = x.astype(BF16)
    r = x - hi.astype(F32)
    mid = r.astype(BF16)
    lo = (r - mid.astype(F32)).astype(BF16)
    return hi, mid, lo


def _sel_dot(sel, x):
    hi, mid, lo = _split3(x)
    d = lambda y: jnp.dot(sel, y, preferred_element_type=F32)
    return (d(hi) + d(mid)) + d(lo)


def _dot_sel(x, sel):
    hi, mid, lo = _split3(x)
    d = lambda y: jnp.dot(y, sel, preferred_element_type=F32)
    return (d(hi) + d(mid)) + d(lo)


def _sel_dot_nt(sel, x):
    hi, mid, lo = _split3(x)
    d = lambda y: lax.dot_general(sel, y, (((1,), (1,)), ((), ())), preferred_element_type=F32)
    return (d(hi) + d(mid)) + d(lo)


def _rms(x, g):
    return x * lax.rsqrt(jnp.mean(x * x, axis=-1, keepdims=True) + EPS) * g


def _sigmoid(x):
    return jax.nn.sigmoid(x)


def _silu(x):
    return x * jax.nn.sigmoid(x)


def _softplus(x):
    return jnp.maximum(x, 0.0) + jnp.log1p(jnp.exp(-jnp.abs(x)))


def _gelu_tanh(x):
    c = math.sqrt(2.0 / math.pi)
    return 0.5 * x * (1.0 + jnp.tanh(c * (x + 0.044715 * (x * x * x))))


def _iota(shape, dim):
    return lax.broadcasted_iota(jnp.int32, shape, dim)


def _seg_causal_mask(n, seg_shift):
    r = _iota((n, n), 0)
    c = _iota((n, n), 1)
    return ((r >> seg_shift) == (c >> seg_shift)) & (c <= r)


def _seg_cumsum(x, seg_shift):
    rows = x.shape[0]
    blk = min(rows, 256)
    mask = _seg_causal_mask(blk, seg_shift).astype(BF16)
    return jnp.concatenate([_sel_dot(mask, x[i:i + blk]) for i in range(0, rows, blk)], axis=0)


def _head_expand_mat(width_in=128):
    r = _iota((width_in, GROUP_W), 0)
    c = _iota((width_in, GROUP_W), 1)
    return ((c >> 6) == r).astype(BF16)


def _head_block_mask():
    return (_iota((GROUP_W, GROUP_W), 0) >> 6) == (_iota((GROUP_W, GROUP_W), 1) >> 6)


def _block_diag_rows(x, mask01):
    return jnp.concatenate([x.astype(BF16)] * N_HEADS, axis=0) * mask01


def _head_stat(y, center):
    ones_bd = _head_block_mask().astype(BF16)
    if center:
        y = y - _dot_sel(y, ones_bd) * (1.0 / HEAD)
    return y * lax.rsqrt(_dot_sel(y * y, ones_bd) * (1.0 / HEAD) + EPS)


def _conv_taps_carry(x, xx_ref, w_ref, b_ref, first):
    rows = x.shape[0]

    @pl.when(first)
    def _():
        xx_ref[0:8, :] = jnp.zeros((8, x.shape[1]), F32)

    xx_ref[8:8 + rows, :] = x
    y = b_ref[...] + w_ref[CONV_W - 1:CONV_W, :] * x
    for j in range(CONV_W - 1):
        y = y + w_ref[j:j + 1, :] * xx_ref[pl.ds(8 - (CONV_W - 1) + j, rows), :]
    xx_ref[0:8, :] = xx_ref[rows:rows + 8, :]
    return y


def _conv_taps_seq4(x, buf4, w_ref, b_ref):
    rows = x.shape[0]
    r = _iota((rows, rows), 0)
    c = _iota((rows, rows), 1)
    t = r & 3
    y = b_ref[...] + w_ref[CONV_W - 1:CONV_W, :] * x
    for d in range(1, CONV_W):
        shift = ((c == r - d) & (t >= d)).astype(BF16)
        hist = ((t < d) & (c == (r - t) + (3 + t - d))).astype(BF16)
        y = y + w_ref[CONV_W - 1 - d:CONV_W - d, :] * (_sel_dot(shift, x) + _sel_dot(hist, buf4))
    new_sel = (((r & 3) < 3) & (c == r + 1)).astype(BF16)
    return y, _sel_dot(new_sel, x)


def _norm_matmul_kernel(x_ref, g_ref, w_ref, *o_refs, col_ranges):
    h = _rms(x_ref[...], g_ref[...]).astype(BF16)
    for o_ref, pieces in zip(o_refs, col_ranges):
        off = 0
        for a, b in pieces:
            o_ref[:, off:off + b - a] = jnp.dot(h, w_ref[:, a:b], preferred_element_type=F32)
            off += b - a


def _const_spec(shape):
    nd = len(shape)
    return pl.BlockSpec(shape, lambda *_: (0,) * nd, pipeline_mode=pl.Buffered(1))


def _layer_spec(shape, layer):
    nd = len(shape)
    return pl.BlockSpec((None,) + tuple(shape[1:]), lambda *_: (layer,) + (0,) * (nd - 1),
                        pipeline_mode=pl.Buffered(1))


def _params(sem):
    return pltpu.CompilerParams(dimension_semantics=sem, vmem_limit_bytes=VMEM_LIMIT)


def _in_weight_prep_kernel(wt_ref, o_ref, ot_ref, *, blocks, dt_block, dt_row, dt_n, t_rows):
    off = 0
    for src, n in t_rows:
        ot_ref[off:off + n, :] = wt_ref[pl.ds(src, n), :].astype(BF16)
        off += n
    for j, pieces in enumerate(blocks):
        if j == dt_block:
            dt = wt_ref[pl.ds(dt_row, 8), :]
            dt = jnp.where(_iota(dt.shape, 0) < dt_n, dt, 0.0)
            blk = jnp.concatenate([dt, jnp.zeros((120, dt.shape[1]), F32)], axis=0)
        else:
            blk = jnp.concatenate([wt_ref[pl.ds(src, n), :] for src, n in pieces], axis=0)
        o_ref[:, j * 128:(j + 1) * 128] = blk.T.astype(BF16)


IN_COLS_SSD = ((0, 1024),)
IN_COLS_RET = ((1024, 2048),)
IN_COLS_RG = ((2048, 2560),)
IN_COLS_HG = ((2560, 3584),)
IN_COLS_DT = ((3584, 3712),)
IN_COLS_RET_SPLIT = ((3712, 4224), (1536, 2048))


def _in_weight_prep(w_in):
    depth, d, d_in = w_in.shape
    z0 = GROUP_W + (GROUP_W + 2 * SSD_G * SSD_N)
    d0 = z0 + N_HEADS
    blocks = [((src, 128),) for src in tuple(range(0, z0, 128)) + tuple(range(d0, d_in, 128))]
    dt_block = len(blocks)
    blocks.append(())
    half = HEAD // 2
    for base in (d0, d0 + GROUP_W):
        for j in range(2):
            blocks.append(tuple((base + h * HEAD + j * half, half) for h in range(N_HEADS)))
    n_out = len(blocks) * 128
    t_rows = ((d0, 4 * GROUP_W), (d0 + 6 * GROUP_W, 4 * GROUP_W))
    n_t = sum(n for _, n in t_rows)
    wt = jnp.swapaxes(w_in, 1, 2)
    return pl.pallas_call(
        functools.partial(_in_weight_prep_kernel, blocks=tuple(blocks), dt_block=dt_block, dt_row=z0,
                          dt_n=N_HEADS, t_rows=t_rows),
        grid=(depth,),
        in_specs=[pl.BlockSpec((None, d_in, d), lambda l: (l, 0, 0), pipeline_mode=pl.Buffered(1))],
        out_specs=[pl.BlockSpec((None, d, n_out), lambda l: (l, 0, 0)),
                   pl.BlockSpec((None, n_t, d), lambda l: (l, 0, 0))],
        out_shape=[jax.ShapeDtypeStruct((depth, d, n_out), BF16), jax.ShapeDtypeStruct((depth, n_t, d), BF16)],
        compiler_params=_params(("parallel",)),
        name="in_weight_prep",
    )(wt)


def _in_proj_t_kernel(x_ref, g_ref, wt_ref, *o_refs):
    h = _rms(x_ref[...], g_ref[...]).astype(BF16)
    off = 0
    for o_ref in o_refs:
        f = o_ref.shape[0]
        o_ref[...] = lax.dot_general(wt_ref[off:off + f, :], h, (((1,), (1,)), ((), ())),
                                     preferred_element_type=F32)
        off += f


def _in_proj_t(x, g, wt, layer):
    n, d = x.shape
    f = wt.shape[1] // 2
    return pl.pallas_call(
        _in_proj_t_kernel,
        grid=(1,),
        in_specs=[_const_spec((n, d)), _const_spec((1, d)), _layer_spec(wt.shape, layer)],
        out_specs=[_const_spec((f, n))] * 2,
        out_shape=[jax.ShapeDtypeStruct((f, n), F32)] * 2,
        compiler_params=_params(("arbitrary",)),
        name="in_proj_t",
    )(x, g, wt)


def _wspec(w, layer):
    return _layer_spec(w.shape, layer) if w.ndim == 3 else _const_spec(w.shape)


def _norm_matmul(x, g, w, col_ranges, name, layer=None):
    n, d = x.shape
    tm = min(ROW_TILE, n)
    w_spec = _wspec(w, layer)
    widths = [sum(b - a for a, b in pieces) for pieces in col_ranges]
    return pl.pallas_call(
        functools.partial(_norm_matmul_kernel, col_ranges=col_ranges),
        grid=(n // tm,),
        in_specs=[pl.BlockSpec((tm, d), lambda i: (i, 0)), _const_spec((1, d)), w_spec],
        out_specs=[pl.BlockSpec((tm, wd), lambda i: (i, 0)) for wd in widths],
        out_shape=[jax.ShapeDtypeStruct((n, wd), F32) for wd in widths],
        compiler_params=_params(("parallel",)),
        name=name,
    )(x, g, w)


def _out_proj_kernel(ys_ref, yr_ref, yg_ref, yh_ref, x_ref, wo_ref, gpost_ref, gpre_ref, wq_ref, x1_ref, q_ref):
    x1 = _mix_out(ys_ref, yr_ref, yg_ref, yh_ref, x_ref, wo_ref, gpost_ref)
    x1_ref[...] = x1
    q_ref[...] = _bdot(_rms(x1, gpre_ref[...]), wq_ref[...])


def _out_proj(ys, x, w_out, g_post, g_pre, w_xq, layer):
    n, d = x.shape
    tm = min(ROW_TILE, n)
    row = lambda wd: pl.BlockSpec((tm, wd), lambda i: (i, 0))
    vec = _const_spec((1, d))
    return pl.pallas_call(
        _out_proj_kernel,
        grid=(n // tm,),
        in_specs=[row(GROUP_W)] * 4 + [row(d), _wspec(w_out, layer), vec, vec, _wspec(w_xq, layer)],
        out_specs=[row(d)] * 2,
        out_shape=[jax.ShapeDtypeStruct((n, d), F32)] * 2,
        compiler_params=_params(("parallel",)),
        name="out_proj",
    )(*ys, x, w_out, g_post, g_pre, w_xq)


def _ffn_tail(o, x1, wxo_ref, gxa_ref, gpre_ref, wgu_ref, wdn_ref, gpost_ref, d_ff):
    a = _bdot(o, wxo_ref[...])
    x2 = x1 + _rms(a, gxa_ref[...])
    h = _rms(x2, gpre_ref[...]).astype(BF16)
    acc = jnp.zeros(x2.shape, F32)
    for j in range(0, d_ff, FF_TILE):
        g = jnp.dot(h, wgu_ref[:, j:j + FF_TILE], preferred_element_type=F32)
        u = jnp.dot(h, wgu_ref[:, d_ff + j:d_ff + j + FF_TILE], preferred_element_type=F32)
        act = (_silu(g) * u).astype(BF16)
        acc = acc + jnp.dot(act, wdn_ref[j:j + FF_TILE, :], preferred_element_type=F32)
    return x2 + _rms(acc, gpost_ref[...])


def _ffn_kernel(o_ref, x1_ref, wxo_ref, gxa_ref, gpre_ref, wgu_ref, wdn_ref, gpost_ref, x3_ref, *, d_ff):
    x3_ref[...] = _ffn_tail(o_ref[...], x1_ref[...], wxo_ref, gxa_ref, gpre_ref, wgu_ref, wdn_ref, gpost_ref,
                            d_ff)


def _ffn(o, x1, w_xo, g_xa, g_pre, w_gu, w_down, g_post, layer):
    n, d = x1.shape
    d_ff = w_down.shape[-2]
    tm = min(ROW_TILE, n)
    row = pl.BlockSpec((tm, d), lambda i: (i, 0))
    vec = _const_spec((1, d))
    return pl.pallas_call(
        functools.partial(_ffn_kernel, d_ff=d_ff),
        grid=(n // tm,),
        in_specs=[row, row, _wspec(w_xo, layer), vec, vec, _wspec(w_gu, layer), _wspec(w_down, layer), vec],
        out_specs=row,
        out_shape=jax.ShapeDtypeStruct((n, d), F32),
        compiler_params=_params(("parallel",)),
        name="attn_out_ffn",
    )(o, x1, w_xo, g_xa, g_pre, w_gu, w_down, g_post)


def _mix_out(ys_ref, yr_ref, yg_ref, yh_ref, x_ref, wo_ref, gpost_ref):
    mix = _bdot(ys_ref[...], wo_ref[0:GROUP_W, :])
    mix = mix + _bdot(yr_ref[...], wo_ref[GROUP_W:2 * GROUP_W, :])
    mix = mix + _bdot(yg_ref[...], wo_ref[2 * GROUP_W:3 * GROUP_W, :])
    mix = mix + _bdot(yh_ref[...], wo_ref[3 * GROUP_W:4 * GROUP_W, :])
    return x_ref[...] + _rms(mix, gpost_ref[...])


def _xattn_ffn_kernel(ys_ref, yr_ref, yg_ref, yh_ref, x_ref, wo_ref, gmix_ref, k_ref, v_ref, gq_ref, wq_ref,
                      wxo_ref, gxa_ref, gpre_ref, wgu_ref, wdn_ref, gpost_ref, x3_ref, o_scr, *, d_ff):
    x1 = _mix_out(ys_ref, yr_ref, yg_ref, yh_ref, x_ref, wo_ref, gmix_ref)
    q = _bdot(_rms(x1, gq_ref[...]), wq_ref[...])
    scale = XA_HD ** -0.5
    for h in range(XA_H):
        sl = slice(h * XA_HD, (h + 1) * XA_HD)
        s = _bdot_nt(q[:, sl], k_ref[0, :, sl]) * scale
        o_scr[:, sl] = _softmax_pv(s, v_ref[0, :, sl])
    x3_ref[...] = _ffn_tail(o_scr[...], x1, wxo_ref, gxa_ref, gpre_ref, wgu_ref, wdn_ref, gpost_ref, d_ff)


def _xattn_ffn(ys, x, w_out, g_mix, k, v, g_q, w_xq, w_xo, g_xa, g_pre, w_gu, w_down, g_post, seq_len, layer):
    n, d = x.shape
    n_seq, n_mem, _ = k.shape
    d_ff = w_down.shape[-2]
    nt = seq_len // ROW_TILE
    row = lambda wd: pl.BlockSpec((ROW_TILE, wd), lambda b, t: (b * nt + t, 0))
    kv = pl.BlockSpec((1, n_mem, d), lambda b, t: (b, 0, 0))
    vec = _const_spec((1, d))
    return pl.pallas_call(
        functools.partial(_xattn_ffn_kernel, d_ff=d_ff),
        grid=(n_seq, nt),
        in_specs=[row(GROUP_W)] * 4 + [row(d), _wspec(w_out, layer), vec, kv, kv, vec, _wspec(w_xq, layer),
                                       _wspec(w_xo, layer), vec, vec, _wspec(w_gu, layer),
                                       _wspec(w_down, layer), vec],
        out_specs=row(d),
        out_shape=jax.ShapeDtypeStruct((n, d), F32),
        scratch_shapes=[pltpu.VMEM((ROW_TILE, d), F32)],
        compiler_params=_params(("parallel", "parallel")),
        name="xattn_ffn",
    )(*ys, x, w_out, g_mix, k, v, g_q, w_xq, w_xo, g_xa, g_pre, w_gu, w_down, g_post)


def _softmax_pv(s, v):
    m = jnp.max(s, axis=-1, keepdims=True)
    p = jnp.exp(s - m)
    return _bdot(p, v) / jnp.sum(p, axis=-1, keepdims=True)


def _attn_sample_kernel(q_ref, k_ref, v_ref, o_ref, *, seqs, seg):
    scale = XA_HD ** -0.5
    rows = seqs * seg
    n_mem = k_ref.shape[2]
    wq = jnp.concatenate([q_ref[:, h * XA_HD:(h + 1) * XA_HD] for h in range(XA_H)], axis=0)
    wshape = (XA_H * rows, n_mem * XA_H)
    valid = (_iota(wshape, 0) >> (rows.bit_length() - 1)) == (_iota(wshape, 1) & (XA_H - 1))
    row_seq = _iota((rows, XA_HD), 0) >> (seg.bit_length() - 1)
    accs = [jnp.zeros((rows, XA_HD), F32) for _ in range(XA_H)]
    for i in range(seqs):
        k2 = k_ref[0, i].reshape(n_mem * XA_H, XA_HD)
        v2 = v_ref[0, i].reshape(n_mem * XA_H, XA_HD)
        s = jnp.where(valid, _bdot_nt(wq, k2) * scale, -jnp.inf)
        o = _softmax_pv(s, v2)
        for h in range(XA_H):
            accs[h] = jnp.where(row_seq == i, o[h * rows:(h + 1) * rows], accs[h])
    for h in range(XA_H):
        o_ref[:, h * XA_HD:(h + 1) * XA_HD] = accs[h]


def _attention_sample(q, cache_k, cache_v, layer, seq_len):
    n, d = q.shape
    _, n_seq, n_mem, heads, hd = cache_k.shape
    kv = pl.BlockSpec((1, ATT_SEQS, n_mem, heads, hd), lambda i: (layer, i, 0, 0, 0))
    row = pl.BlockSpec((ATT_SEQS * seq_len, d), lambda i: (i, 0))
    return pl.pallas_call(
        functools.partial(_attn_sample_kernel, seqs=ATT_SEQS, seg=seq_len),
        grid=(n_seq // ATT_SEQS,),
        in_specs=[row, kv, kv],
        out_specs=row,
        out_shape=jax.ShapeDtypeStruct((n, d), F32),
        compiler_params=_params(("parallel",)),
        name="xattn_sample",
    )(q, cache_k, cache_v)


def _ssd_intra(q, k, v, cum_x, cum_t, mask):
    sc = _bdot_nt(q, k)
    outs = []
    for hh in range(2):
        col = cum_x[:, hh * HEAD:(hh + 1) * HEAD]
        dec = jnp.exp(jnp.where(mask, col - cum_t[hh:hh + 1, :], -jnp.inf))
        outs.append(_bdot(sc * dec, v[:, hh * HEAD:(hh + 1) * HEAD]))
    return jnp.concatenate(outs, axis=-1)


def _ssd_common(u, dtraw_ref, dtb_ref, alog_ref, conv, seg_shift):
    rows = u.shape[0]
    xbc = _silu(conv)
    xs = xbc[:, 0:GROUP_W]
    bm = xbc[:, GROUP_W:2 * GROUP_W]
    cm = xbc[:, 2 * GROUP_W:3 * GROUP_W]
    dt = _softplus(dtraw_ref[...] + dtb_ref[...])
    la = -jnp.exp(alog_ref[...]) * dt
    cum = _seg_cumsum(la, seg_shift)
    expand = _head_expand_mat()
    dt_x = _dot_sel(dt, expand)
    cum_x = _dot_sel(cum, expand)
    sel8 = (_iota((8, 128), 0) == _iota((8, 128), 1)).astype(BF16)
    cum_t = _sel_dot_nt(sel8, cum)
    expand_full = ((_iota((128, 4 * 128), 1) >> 7) == _iota((128, 4 * 128), 0)).astype(BF16)
    cum_full = _dot_sel(cum, expand_full)
    return xs, bm, cm, dt_x, cum_x, cum_t, cum_full


def _ssd_finish(y, xs, z, d_ref, g_ref):
    y = y + d_ref[...] * xs
    return _rms(y * _silu(z), g_ref[...])


def _ssd_prompt_kernel(u_ref, dtraw_ref, cw_ref, cb_ref, dtb_ref, alog_ref, d_ref, g_ref,
                       y_ref, s_out_ref, buf_out_ref, s_ref, xx_ref, y_scr):
    t = pl.program_id(1)
    rows = u_ref.shape[0]

    @pl.when(t == 0)
    def _():
        s_ref[...] = jnp.zeros(s_ref.shape, F32)

    u = u_ref[...]
    conv = _conv_taps_carry(u[:, GROUP_W:], xx_ref, cw_ref, cb_ref, t == 0)
    xs, bm, cm, dt_x, cum_x, _, cum_full = _ssd_common(u, dtraw_ref, dtb_ref, alog_ref, conv, 6)
    v = xs * dt_x
    ecum_x = jnp.exp(cum_x)
    hmask01 = _head_block_mask().astype(BF16)
    gmask01 = ((_iota((GROUP_W, GROUP_W), 0) >> 7) == (_iota((GROUP_W, GROUP_W), 1) >> 7)).astype(BF16)
    wrow = _iota((CHUNK, GROUP_W), 0)
    wcol = _iota((CHUNK, GROUP_W), 1) & (CHUNK - 1)
    n_chunks = rows // CHUNK
    groups = [slice(g * 128, (g + 1) * 128) for g in range(SSD_G)]
    a_wide, ds_t = [], []
    for c in range(n_chunks):
        rs = slice(c * CHUNK, (c + 1) * CHUNK)
        cum_c = cum_x[rs]
        last_x = cum_x[c * CHUNK + CHUNK - 1:(c + 1) * CHUNK, :]
        vend = v[rs] * jnp.exp(last_x - cum_c)
        cum_row = jnp.sum(jnp.where(wrow == wcol, cum_c, 0.0), axis=0, keepdims=True)
        dec = jnp.exp(jnp.where(wcol <= wrow, cum_c - cum_row, -jnp.inf))
        a_wide.append(_bdot_nt(cm[rs], _block_diag_rows(bm[rs], gmask01)) * dec)
        ds_t.append([_bdot_tn(vend[:, ls], bm[rs, ls]) for ls in groups])
    s_in = []
    s_cur = [s_ref[ls, :] for ls in groups]
    for c in range(n_chunks):
        last_full = cum_full[c * CHUNK + CHUNK - 1:(c + 1) * CHUNK, :]
        s_in.append([x.astype(BF16) for x in s_cur])
        for g in range(SSD_G):
            decay = jnp.concatenate(
                [jnp.broadcast_to(jnp.exp(last_full[:, h * 128:(h + 1) * 128]), (HEAD, 128))
                 for h in (2 * g, 2 * g + 1)], axis=0)
            s_cur[g] = decay * s_cur[g] + ds_t[c][g]
    for g, ls in enumerate(groups):
        s_ref[ls, :] = s_cur[g]
    for c in range(n_chunks):
        rs = slice(c * CHUNK, (c + 1) * CHUNK)
        y_intra = _bdot(a_wide[c], _block_diag_rows(v[rs], hmask01))
        y_inter = [_bdot_nt(cm[rs, ls], s_in[c][g]) for g, ls in enumerate(groups)]
        y_scr[rs, :] = y_intra + jnp.concatenate(y_inter, axis=-1) * ecum_x[rs]
    y_ref[...] = _ssd_finish(y_scr[...], xs, u[:, 0:GROUP_W], d_ref, g_ref)

    @pl.when(t == pl.num_programs(1) - 1)
    def _():
        for h in range(N_HEADS):
            s_out_ref[0, h] = s_ref[h * HEAD:(h + 1) * HEAD, :]
        buf_out_ref[0] = xx_ref[pl.ds(8 - (CONV_W - 1), CONV_W - 1), :]


def _ssd_sample_kernel(u_ref, dtraw_ref, s0_ref, buf_ref, cw_ref, cb_ref, dtb_ref, alog_ref, d_ref,
                       g_ref, y_ref, s_out_ref, buf_out_ref, *, seqs):
    rows = u_ref.shape[0]
    u = u_ref[...]
    conv, new_buf = _conv_taps_seq4(u[:, GROUP_W:], buf_ref[...], cw_ref, cb_ref)
    buf_out_ref[...] = new_buf
    mask = _seg_causal_mask(rows, 2)
    xs, bm, cm, dt_x, cum_x, cum_t, cum_full = _ssd_common(u, dtraw_ref, dtb_ref, alog_ref, conv, 2)
    v = xs * dt_x
    ecum_x = jnp.exp(cum_x)
    pick_last = (_iota((rows, rows), 1) == (_iota((rows, rows), 0) | 3)).astype(BF16)
    last_x = _sel_dot(pick_last, cum_x)
    vend = v * jnp.exp(last_x - cum_x)
    row_seq = _iota((rows, 128), 0) >> 2
    ys = []
    for g in range(SSD_G):
        ls = slice(g * 128, (g + 1) * 128)
        q = cm[:, ls]
        k = bm[:, ls]
        y_g = _ssd_intra(q, k, v[:, ls], cum_x[:, ls], cum_t[2 * g:2 * g + 2, :], mask)
        y_int = jnp.zeros((rows, 128), F32)
        for i in range(seqs):
            in_seq = row_seq == i
            s_prev = jnp.concatenate([s0_ref[i, 2 * g], s0_ref[i, 2 * g + 1]], axis=0)
            y_int = jnp.where(in_seq, _bdot_nt(q, s_prev), y_int)
            ds = _bdot_tn(jnp.where(in_seq, vend[:, ls], 0.0), k)
            for hh in range(2):
                h = 2 * g + hh
                decay = jnp.exp(cum_full[4 * i + 3:4 * i + 4, h * 128:(h + 1) * 128])
                s_out_ref[i, h] = decay * s0_ref[i, h] + ds[hh * HEAD:(hh + 1) * HEAD, :]
        ys.append(y_g + y_int * ecum_x[:, ls])
    y_ref[...] = _ssd_finish(jnp.concatenate(ys, axis=-1), xs, u[:, 0:GROUP_W], d_ref, g_ref)


def _ssd(u, dtraw, state, buf, cw, cb, dtb, alog, d_x, gain, n_seq, seq_len, layer):
    n = u.shape[0]
    cch = cw.shape[1]
    vec = lambda wd: _const_spec((1, wd))
    common_in = [_const_spec(cw.shape), vec(cch), vec(128), vec(128), vec(GROUP_W), vec(GROUP_W)]
    if state is None:
        nt = seq_len // MIX_TILE
        row = lambda wd: pl.BlockSpec((MIX_TILE, wd), lambda b, t: (b * nt + t, 0))
        return pl.pallas_call(
            _ssd_prompt_kernel,
            grid=(n_seq, nt),
            in_specs=[row(u.shape[1]), row(128)] + common_in,
            out_specs=[row(GROUP_W),
                       pl.BlockSpec((1, N_HEADS, HEAD, SSD_N), lambda b, t: (b, 0, 0, 0)),
                       pl.BlockSpec((1, CONV_W - 1, cch), lambda b, t: (b, 0, 0))],
            out_shape=[jax.ShapeDtypeStruct((n, GROUP_W), F32),
                       jax.ShapeDtypeStruct((n_seq, N_HEADS, HEAD, SSD_N), F32),
                       jax.ShapeDtypeStruct((n_seq, CONV_W - 1, cch), F32)],
            scratch_shapes=[pltpu.VMEM((GROUP_W, SSD_N), F32), pltpu.VMEM((MIX_TILE + 8, cch), F32),
                            pltpu.VMEM((MIX_TILE, GROUP_W), F32)],
            compiler_params=_params(("parallel", "arbitrary")),
            name="ssd_prompt",
        )(u, dtraw, cw, cb, dtb, alog, d_x, gain)
    rows = SAMPLE_SEQS * seq_len
    row = lambda wd: pl.BlockSpec((rows, wd), lambda i: (i, 0))
    st = pl.BlockSpec((SAMPLE_SEQS, N_HEADS, HEAD, SSD_N), lambda i: (i, 0, 0, 0))
    st_in = pl.BlockSpec((None, SAMPLE_SEQS, N_HEADS, HEAD, SSD_N), lambda i: (layer, i, 0, 0, 0))
    return pl.pallas_call(
        functools.partial(_ssd_sample_kernel, seqs=SAMPLE_SEQS),
        grid=(n_seq // SAMPLE_SEQS,),
        in_specs=[row(u.shape[1]), row(128), st_in, row(cch)] + common_in,
        out_specs=[row(GROUP_W), st, row(cch)],
        out_shape=[jax.ShapeDtypeStruct((n, GROUP_W), F32),
                   jax.ShapeDtypeStruct(state.shape[1:], F32),
                   jax.ShapeDtypeStruct((n, cch), F32)],
        compiler_params=_params(("parallel",)),
        name="ssd_sample",
    )(u, dtraw, state, buf, cw, cb, dtb, alog, d_x, gain)


def _rope(x, cos, sin_signed):
    lane = _iota(x.shape, 1)
    swapped = jnp.where((lane & 63) < 32, pltpu.roll(x, GROUP_W - 32, 1), pltpu.roll(x, 32, 1))
    return x * cos + swapped * sin_signed


def _rope_split(x, cos, sin_signed):
    swapped = jnp.concatenate([x[:, 128:], x[:, :128]], axis=1)
    return x * cos + swapped * sin_signed


def _ret_qkv(u_ref, cos_ref, sin_ref, rope=_rope):
    u = u_ref[...]
    q = rope(u[:, 0:GROUP_W], cos_ref[...], sin_ref[...])
    k = rope(u[:, GROUP_W:2 * GROUP_W], cos_ref[...], sin_ref[...]) * (HEAD ** -0.5)
    return q, k, u[:, 2 * GROUP_W:3 * GROUP_W], u[:, 3 * GROUP_W:4 * GROUP_W]


def _ret_prompt_kernel(u_ref, cos_ref, sin_ref, ecum_ref, eend_ref, elast_ref, dec_ref, g_ref,
                       y_ref, s_out_ref, s_ref, y_scr):
    t = pl.program_id(1)
    rows = u_ref.shape[0]

    @pl.when(t == 0)
    def _():
        s_ref[...] = jnp.zeros(s_ref.shape, F32)

    q, k, v, gate = _ret_qkv(u_ref, cos_ref, sin_ref, _rope_split)
    r_i = _iota((GROUP_W, GROUP_W), 0)
    c_i = _iota((GROUP_W, GROUP_W), 1)
    kmask01 = ((r_i >> 6) == ((c_i & 127) >> 5)).astype(BF16)
    vmask01 = _head_block_mask().astype(BF16)
    smask = ((r_i & 127) >> 5) == (c_i >> 6)
    chunks = [slice(c * CHUNK, (c + 1) * CHUNK) for c in range(rows // CHUNK)]
    a_wide = [_bdot_nt(q[rs], _block_diag_rows(k[rs], kmask01)) * dec_ref[...] for rs in chunks]
    ds = [jnp.where(smask, _bdot_tn(k[rs], v[rs] * eend_ref[...]), 0.0) for rs in chunks]
    s = s_ref[...]
    s_in = []
    for d in ds:
        s_in.append(s.astype(BF16))
        s = elast_ref[...] * s + d
    s_ref[...] = s
    for rs, a, s_c in zip(chunks, a_wide, s_in):
        y_intra = _bdot(a, _block_diag_rows(v[rs], vmask01))
        y_scr[rs, :] = y_intra + _bdot(q[rs], s_c) * ecum_ref[...]
    y_ref[...] = _silu(gate) * (_head_stat(y_scr[...], True) * g_ref[...])

    @pl.when(t == pl.num_programs(1) - 1)
    def _():
        half = HEAD // 2
        for h in range(N_HEADS):
            vs = slice(h * HEAD, (h + 1) * HEAD)
            s_out_ref[0, h, 0:half, :] = s_ref[h * half:(h + 1) * half, vs]
            s_out_ref[0, h, half:HEAD, :] = s_ref[128 + h * half:128 + (h + 1) * half, vs]


def _ret_sample_kernel(u_ref, cos_ref, sin_ref, s0_ref, ecum_ref, eend_ref, elast_ref, dec_ref, g_ref,
                       y_ref, s_out_ref, *, seqs):
    rows = u_ref.shape[0]
    q, k, v, gate = _ret_qkv(u_ref, cos_ref, sin_ref)
    vend = v * eend_ref[...]
    row_seq = _iota((rows, HEAD), 0) >> 2
    ys = []
    for h in range(N_HEADS):
        ls = slice(h * HEAD, (h + 1) * HEAD)
        qh = q[:, ls]
        kh = k[:, ls]
        y_h = _bdot(_bdot_nt(qh, kh) * dec_ref[h], v[:, ls])
        y_int = jnp.zeros((rows, HEAD), F32)
        for i in range(seqs):
            in_seq = row_seq == i
            s_prev = s0_ref[i, h]
            y_int = jnp.where(in_seq, _bdot(qh, s_prev), y_int)
            ds = _bdot_tn(kh, jnp.where(in_seq, vend[:, ls], 0.0))
            s_out_ref[i, h] = elast_ref[:, ls] * s_prev + ds
        ys.append(y_h + y_int * ecum_ref[:, ls])
    y_ref[...] = _silu(gate) * (_head_stat(jnp.concatenate(ys, axis=-1), True) * g_ref[...])


def _ret_tables(block, seg):
    log_gamma = jnp.log(1.0 - jnp.exp2(-5.0 - jnp.arange(N_HEADS, dtype=F32)))
    pos = (jnp.arange(block) % seg).astype(F32)
    cum = (pos[:, None] + 1.0) * log_gamma[None, :]
    last = seg * log_gamma
    rep = lambda a: jnp.repeat(a, HEAD, axis=-1)
    same = (jnp.arange(block)[:, None] // seg) == (jnp.arange(block)[None, :] // seg)
    mask = same & (jnp.arange(block)[None, :] <= jnp.arange(block)[:, None])
    dec = jnp.exp(jnp.where(mask[None], cum.T[:, :, None] - cum.T[:, None, :], -jnp.inf))
    return (rep(jnp.exp(cum)), rep(jnp.exp(last[None, :] - cum)), rep(jnp.exp(last)[None, :]), dec)


def _ret(u, cos, sin_signed, state, gain, n_seq, seq_len, layer):
    n = u.shape[0]
    vec = _const_spec((1, GROUP_W))
    if state is None:
        nt = seq_len // MIX_TILE
        ecum, eend, elast, dec = _ret_tables(CHUNK, CHUNK)
        dec = dec.transpose(1, 0, 2).reshape(CHUNK, N_HEADS * CHUNK)
        row = lambda wd: pl.BlockSpec((MIX_TILE, wd), lambda b, t: (b * nt + t, 0))
        pos = pl.BlockSpec((MIX_TILE, GROUP_W), lambda b, t: (t, 0))
        return pl.pallas_call(
            _ret_prompt_kernel,
            grid=(n_seq, nt),
            in_specs=[row(u.shape[1]), pos, pos, _const_spec(ecum.shape), _const_spec(eend.shape), vec,
                      _const_spec(dec.shape), vec],
            out_specs=[row(GROUP_W), pl.BlockSpec((1, N_HEADS, HEAD, HEAD), lambda b, t: (b, 0, 0, 0))],
            out_shape=[jax.ShapeDtypeStruct((n, GROUP_W), F32),
                       jax.ShapeDtypeStruct((n_seq, N_HEADS, HEAD, HEAD), F32)],
            scratch_shapes=[pltpu.VMEM((GROUP_W, GROUP_W), F32), pltpu.VMEM((MIX_TILE, GROUP_W), F32)],
            compiler_params=_params(("parallel", "arbitrary")),
            name="ret_prompt",
        )(u, cos, sin_signed, ecum, eend, elast, dec, gain)
    rows = SAMPLE_SEQS * seq_len
    ecum, eend, elast, dec = _ret_tables(rows, seq_len)
    row = lambda wd: pl.BlockSpec((rows, wd), lambda i: (i, 0))
    st = pl.BlockSpec((SAMPLE_SEQS, N_HEADS, HEAD, HEAD), lambda i: (i, 0, 0, 0))
    cos_t = jnp.tile(cos, (SAMPLE_SEQS, 1))
    sin_t = jnp.tile(sin_signed, (SAMPLE_SEQS, 1))
    return pl.pallas_call(
        functools.partial(_ret_sample_kernel, seqs=SAMPLE_SEQS),
        grid=(n_seq // SAMPLE_SEQS,),
        in_specs=[row(u.shape[1]), _const_spec(cos_t.shape), _const_spec(sin_t.shape),
                  pl.BlockSpec((None, SAMPLE_SEQS, N_HEADS, HEAD, HEAD), lambda i: (layer, i, 0, 0, 0)),
                  _const_spec(ecum.shape), _const_spec(eend.shape), vec, _const_spec(dec.shape), vec],
        out_specs=[row(GROUP_W), st],
        out_shape=[jax.ShapeDtypeStruct((n, GROUP_W), F32), jax.ShapeDtypeStruct(state.shape[1:], F32)],
        compiler_params=_params(("parallel",)),
        name="ret_sample",
    )(u, cos_t, sin_t, state, ecum, eend, elast, dec, gain)


def _rg_gates(xr, wa_ref, ba_ref, wx_ref, bx_ref, lam_ref):
    r_gate = _sigmoid(_bdot(xr, wa_ref[...]) + ba_ref[...])
    i_gate = _sigmoid(_bdot(xr, wx_ref[...]) + bx_ref[...])
    log_a = (-RG_C * _softplus(-lam_ref[...])) * r_gate
    a = jnp.exp(log_a)
    b = jnp.sqrt(-jnp.tanh(log_a) * (jnp.exp(2.0 * log_a) + 1.0)) * (i_gate * xr)
    return a, b


def _rg_scan(a, b, seg):
    pos = _iota(a.shape, 0) & (seg - 1)
    d = 1
    while d < seg:
        ok = pos >= d
        a_sh = jnp.where(ok, pltpu.roll(a, d, 0), 1.0)
        b_sh = jnp.where(ok, pltpu.roll(b, d, 0), 0.0)
        b = a * b_sh + b
        a = a * a_sh
        d *= 2
    return a, b


def _rg_prompt_kernel(u_ref, cw_ref, cb_ref, wa_ref, ba_ref, wx_ref, bx_ref, lam_ref,
                      y_ref, h_out_ref, buf_out_ref, h_ref, xx_ref):
    t = pl.program_id(1)
    rows = u_ref.shape[0]

    @pl.when(t == 0)
    def _():
        h_ref[...] = jnp.zeros(h_ref.shape, F32)

    u = u_ref[...]
    xr = _conv_taps_carry(u[:, 0:GROUP_W], xx_ref, cw_ref, cb_ref, t == 0)
    a, b = _rg_gates(xr, wa_ref, ba_ref, wx_ref, bx_ref, lam_ref)
    a_cum, h_loc = _rg_scan(a, b, rows)
    hseq = h_loc + a_cum * h_ref[0:1, :]
    h_ref[...] = jnp.broadcast_to(hseq[rows - 1:rows, :], h_ref.shape)
    y_ref[...] = _gelu_tanh(u[:, GROUP_W:]) * hseq

    @pl.when(t == pl.num_programs(1) - 1)
    def _():
        h_out_ref[0] = hseq[rows - 1:rows, :]
        buf_out_ref[0] = xx_ref[pl.ds(8 - (CONV_W - 1), CONV_W - 1), :]


def _rg_sample_kernel(u_ref, h0_ref, buf_ref, cw_ref, cb_ref, wa_ref, ba_ref, wx_ref, bx_ref, lam_ref,
                      y_ref, h_out_ref, buf_out_ref):
    u = u_ref[...]
    xr, new_buf = _conv_taps_seq4(u[:, 0:GROUP_W], buf_ref[...], cw_ref, cb_ref)
    buf_out_ref[...] = new_buf
    a, b = _rg_gates(xr, wa_ref, ba_ref, wx_ref, bx_ref, lam_ref)
    a_cum, h_loc = _rg_scan(a, b, 4)
    hseq = h_loc + a_cum * h0_ref[...]
    h_out_ref[...] = hseq
    y_ref[...] = _gelu_tanh(u[:, GROUP_W:]) * hseq


def _rg(u, h0_rows, buf, cw, cb, wa, ba, wx, bx, lam, n_seq, seq_len):
    n = u.shape[0]
    vec = _const_spec((1, GROUP_W))
    common_in = [_const_spec(cw.shape), vec, _const_spec(wa.shape), vec, _const_spec(wx.shape), vec, vec]
    if h0_rows is None:
        nt = seq_len // RG_TILE
        row = lambda wd: pl.BlockSpec((RG_TILE, wd), lambda b, t: (b * nt + t, 0))
        return pl.pallas_call(
            _rg_prompt_kernel,
            grid=(n_seq, nt),
            in_specs=[row(u.shape[1])] + common_in,
            out_specs=[row(GROUP_W), pl.BlockSpec((1, 1, GROUP_W), lambda b, t: (b, 0, 0)),
                       pl.BlockSpec((1, CONV_W - 1, GROUP_W), lambda b, t: (b, 0, 0))],
            out_shape=[jax.ShapeDtypeStruct((n, GROUP_W), F32),
                       jax.ShapeDtypeStruct((n_seq, 1, GROUP_W), F32),
                       jax.ShapeDtypeStruct((n_seq, CONV_W - 1, GROUP_W), F32)],
            scratch_shapes=[pltpu.VMEM((8, GROUP_W), F32), pltpu.VMEM((RG_TILE + 8, GROUP_W), F32)],
            compiler_params=_params(("parallel", "arbitrary")),
            name="rglru_prompt",
        )(u, cw, cb, wa, ba, wx, bx, lam)
    rows = SAMPLE_SEQS * seq_len
    row = lambda wd: pl.BlockSpec((rows, wd), lambda i: (i, 0))
    return pl.pallas_call(
        _rg_sample_kernel,
        grid=(n_seq // SAMPLE_SEQS,),
        in_specs=[row(u.shape[1]), row(GROUP_W), row(GROUP_W)] + common_in,
        out_specs=[row(GROUP_W)] * 3,
        out_shape=[jax.ShapeDtypeStruct((n, GROUP_W), F32)] * 3,
        compiler_params=_params(("parallel",)),
        name="rglru_sample",
    )(u, h0_rows, buf, cw, cb, wa, ba, wx, bx, lam)


def _hg_inputs(u_ref, lb_ref, seg_shift):
    u = u_ref[...]
    rows = u.shape[0]
    lb = lb_ref[...]
    fg = lb + (1.0 - lb) * _sigmoid(u[:, GROUP_W:2 * GROUP_W])
    q = _silu(u[:, 0:GROUP_W])
    k = 1.0 - fg
    v = u[:, 2 * GROUP_W:3 * GROUP_W]
    cum = _seg_cumsum(jnp.log(fg), seg_shift)
    return q, k, v, cum, u[:, 3 * GROUP_W:4 * GROUP_W]


def _hg_intra(q, k, v, cum, kk_ref, cc_ref, vv_ref, seg):
    rows = q.shape[0]
    pad = kk_ref.shape[0] - rows
    for ref, val in ((kk_ref, k), (cc_ref, cum), (vv_ref, v)):
        ref[0:pad, :] = jnp.zeros((pad, GROUP_W), F32)
        ref[pad:pad + rows, :] = val
    pos = _iota((rows, GROUP_W), 0) & (seg - 1)
    ones_bd = ((_iota((GROUP_W, GROUP_W), 0) >> 6) == (_iota((GROUP_W, GROUP_W), 1) >> 6)).astype(BF16)
    y = jnp.dot((q * k).astype(BF16), ones_bd, preferred_element_type=F32) * v
    for d in range(1, seg):
        k_d = kk_ref[pl.ds(pad - d, rows), :]
        c_d = cc_ref[pl.ds(pad - d, rows), :]
        v_d = vv_ref[pl.ds(pad - d, rows), :]
        p = jnp.where(pos >= d, q * k_d * jnp.exp(cum - c_d), 0.0)
        y = y + jnp.dot(p.astype(BF16), ones_bd, preferred_element_type=F32) * v_d
    return y


def _hg_prompt_kernel(u_ref, lb_ref, g_ref, y_ref, s_out_ref, st_ref, y_scr):
    t = pl.program_id(1)
    rows = u_ref.shape[0]

    @pl.when(t == 0)
    def _():
        st_ref[...] = jnp.zeros(st_ref.shape, F32)

    q, k, v, cum, gate = _hg_inputs(u_ref, lb_ref, 6)
    hmask = _head_block_mask()
    hmask01 = hmask.astype(BF16)
    ones_bd = hmask01
    y_scr[...] = jnp.dot((q * k).astype(BF16), ones_bd, preferred_element_type=F32) * v
    row = _iota((rows, GROUP_W), 0)
    levels = []
    blk_last = cum
    h = 1
    while h < CHUNK:
        upper = (row & h) != 0
        qn = jnp.where(upper, q * jnp.exp(cum - pltpu.roll(blk_last, h, 0)), 0.0)
        kn = jnp.where(upper, 0.0, k * jnp.exp(blk_last - cum))
        levels.append((h, qn, kn))
        blk_last = jnp.where(upper, blk_last, pltpu.roll(blk_last, rows - h, 0))
        h *= 2
    q_in = q * jnp.exp(cum)
    k_end = k * jnp.exp(blk_last - cum)
    trow = _iota((CHUNK, GROUP_W), 0)
    tcol = _iota((CHUNK, GROUP_W), 1) & (CHUNK - 1)
    n_chunks = rows // CHUNK
    a_wide, ds_t = [], []
    for c in range(n_chunks):
        rs = slice(c * CHUNK, (c + 1) * CHUNK)
        a_c = jnp.zeros((CHUNK, GROUP_W), F32)
        for h, qn, kn in levels:
            g = _bdot_nt(qn[rs], _block_diag_rows(kn[rs], hmask01))
            sh = (2 * h).bit_length() - 1
            a_c = a_c + jnp.where((trow >> sh) == (tcol >> sh), g, 0.0)
        a_wide.append(a_c)
        ds_t.append(jnp.where(hmask, _bdot_tn(v[rs], k_end[rs]), 0.0))
    st = st_ref[...]
    st_in = []
    for c in range(n_chunks):
        st_in.append(st.astype(BF16))
        st = jnp.exp(blk_last[c * CHUNK:c * CHUNK + 1, :]) * st + ds_t[c]
    st_ref[...] = st
    for c in range(n_chunks):
        rs = slice(c * CHUNK, (c + 1) * CHUNK)
        y_c = _bdot(a_wide[c], _block_diag_rows(v[rs], hmask01)) + _bdot_nt(q_in[rs], st_in[c])
        y_scr[rs, :] = y_scr[rs, :] + y_c
    y_ref[...] = _silu(gate) * (_head_stat(y_scr[...], False) * g_ref[...])

    @pl.when(t == pl.num_programs(1) - 1)
    def _():
        eye = (_iota((HEAD, HEAD), 0) == _iota((HEAD, HEAD), 1)).astype(BF16)
        for h in range(N_HEADS):
            s_out_ref[0, h] = _sel_dot_nt(eye, st_ref[h * HEAD:(h + 1) * HEAD, h * HEAD:(h + 1) * HEAD])


def _hg_sample_kernel(u_ref, s0_ref, lb_ref, g_ref, y_ref, s_out_ref, kk_ref, cc_ref, vv_ref, *, seqs):
    rows = u_ref.shape[0]
    q, k, v, cum, gate = _hg_inputs(u_ref, lb_ref, 2)
    y = _hg_intra(q, k, v, cum, kk_ref, cc_ref, vv_ref, 4)
    pick_last = (_iota((rows, rows), 1) == (_iota((rows, rows), 0) | 3)).astype(BF16)
    last = _sel_dot(pick_last, cum)
    qn = q * jnp.exp(cum)
    kend = k * jnp.exp(last - cum)
    elast = jnp.exp(last)
    row_seq = _iota((rows, HEAD), 0) >> 2
    eye = _iota((HEAD, HEAD), 0) == _iota((HEAD, HEAD), 1)
    y_int = []
    for h in range(N_HEADS):
        ls = slice(h * HEAD, (h + 1) * HEAD)
        acc = jnp.zeros((rows, HEAD), F32)
        for i in range(seqs):
            in_seq = row_seq == i
            s_prev = s0_ref[i, h]
            acc = jnp.where(in_seq, _bdot(qn[:, ls], s_prev), acc)
            ds = _bdot_tn(jnp.where(in_seq, kend[:, ls], 0.0), v[:, ls])
            a_col = jnp.sum(jnp.where(eye, elast[4 * i:4 * i + 1, ls], 0.0), axis=-1, keepdims=True)
            s_out_ref[i, h] = a_col * s_prev + ds
        y_int.append(acc)
    y = y + jnp.concatenate(y_int, axis=-1)
    y_ref[...] = _silu(gate) * (_head_stat(y, False) * g_ref[...])


def _hg(u, state, lb, gain, n_seq, seq_len, layer):
    n = u.shape[0]
    vec = _const_spec((1, GROUP_W))
    if state is None:
        nt = seq_len // MIX_TILE
        row = lambda wd: pl.BlockSpec((MIX_TILE, wd), lambda b, t: (b * nt + t, 0))
        return pl.pallas_call(
            _hg_prompt_kernel,
            grid=(n_seq, nt),
            in_specs=[row(u.shape[1]), vec, vec],
            out_specs=[row(GROUP_W), pl.BlockSpec((1, N_HEADS, HEAD, HEAD), lambda b, t: (b, 0, 0, 0))],
            out_shape=[jax.ShapeDtypeStruct((n, GROUP_W), F32),
                       jax.ShapeDtypeStruct((n_seq, N_HEADS, HEAD, HEAD), F32)],
            scratch_shapes=[pltpu.VMEM((GROUP_W, GROUP_W), F32), pltpu.VMEM((MIX_TILE, GROUP_W), F32)],
            compiler_params=_params(("parallel", "arbitrary")),
            name="hgrn_prompt",
        )(u, lb, gain)
    rows = SAMPLE_SEQS * seq_len
    row = lambda wd: pl.BlockSpec((rows, wd), lambda i: (i, 0))
    st = pl.BlockSpec((SAMPLE_SEQS, N_HEADS, HEAD, HEAD), lambda i: (i, 0, 0, 0))
    shifted = pltpu.VMEM((rows + 8, GROUP_W), F32)
    return pl.pallas_call(
        functools.partial(_hg_sample_kernel, seqs=SAMPLE_SEQS),
        grid=(n_seq // SAMPLE_SEQS,),
        in_specs=[row(u.shape[1]),
                  pl.BlockSpec((None, SAMPLE_SEQS, N_HEADS, HEAD, HEAD), lambda i: (layer, i, 0, 0, 0)), vec, vec],
        out_specs=[row(GROUP_W), st],
        out_shape=[jax.ShapeDtypeStruct((n, GROUP_W), F32), jax.ShapeDtypeStruct(state.shape[1:], F32)],
        scratch_shapes=[shifted, shifted, shifted],
        compiler_params=_params(("parallel",)),
        name="hgrn_sample",
    )(u, state, lb, gain)


def _t_recurrence(q_scr, k_scr, v, dec_row, dec_scr, s_ref, n_seq, seq_len):
    cols = [slice(t * n_seq, (t + 1) * n_seq) for t in range(seq_len)]
    v_t = [v[:, c] for c in cols]
    outs = [jnp.zeros((HEAD, n_seq), F32) for _ in cols]
    for k in range(HEAD):
        s = s_ref[k]
        for t, c in enumerate(cols):
            dec = dec_row if dec_scr is None else dec_scr[k:k + 1, c]
            s = dec * s + k_scr[k:k + 1, c] * v_t[t]
            outs[t] = outs[t] + q_scr[k:k + 1, c] * s
        s_ref[k] = s
    return jnp.concatenate(outs, axis=1)


def _t_head_norm(o, center):
    if center:
        o = o - jnp.mean(o, axis=0, keepdims=True)
    return o * lax.rsqrt(jnp.mean(o * o, axis=0, keepdims=True) + EPS)


def _ret_t_kernel(q_ref, k_ref, v_ref, g_ref, cos_ref, sin_ref, gam_ref, gain_ref, s0_ref,
                  y_ref, s_ref, q_scr, k_scr, *, n_seq, seq_len):
    half = HEAD // 2

    def rope(x):
        swapped = jnp.concatenate([x[half:], x[:half]], axis=0)
        return x * cos_ref[...] + swapped * sin_ref[...]

    q_scr[...] = rope(q_ref[...])
    k_scr[...] = rope(k_ref[...]) * (HEAD ** -0.5)
    s_ref[0] = s0_ref[0]
    o = _t_recurrence(q_scr, k_scr, v_ref[...], gam_ref[0], None, s_ref.at[0], n_seq, seq_len)
    gain = jnp.concatenate([gain_ref[...]] * seq_len, axis=1)
    parts = [_t_head_norm(o[:, t * n_seq:(t + 1) * n_seq], True) for t in range(seq_len)]
    y_ref[...] = _silu(g_ref[...]) * (jnp.concatenate(parts, axis=1) * gain)


def _hg_t_kernel(q_ref, f_ref, i_ref, g_ref, lb_ref, gain_ref, s0_ref, y_ref, s_ref, q_scr, k_scr, f_scr,
                 *, n_seq, seq_len):
    lb = jnp.concatenate([lb_ref[...]] * seq_len, axis=1)
    fg = lb + (1.0 - lb) * _sigmoid(f_ref[...])
    q_scr[...] = _silu(q_ref[...])
    k_scr[...] = 1.0 - fg
    f_scr[...] = fg
    s_ref[0] = s0_ref[0]
    o = _t_recurrence(q_scr, k_scr, i_ref[...], None, f_scr, s_ref.at[0], n_seq, seq_len)
    gain = jnp.concatenate([gain_ref[...]] * seq_len, axis=1)
    parts = [_t_head_norm(o[:, t * n_seq:(t + 1) * n_seq], False) for t in range(seq_len)]
    y_ref[...] = _silu(g_ref[...]) * (jnp.concatenate(parts, axis=1) * gain)


def _t_specs(n, n_seq):
    head_rows = lambda section: pl.BlockSpec((HEAD, n), lambda h, s=section: (N_HEADS * s + h, 0))
    per_head = pl.BlockSpec((HEAD, n_seq), lambda h: (h, 0))
    state_in = lambda layer: pl.BlockSpec((None, 1, HEAD, HEAD, n_seq), lambda h: (layer, h, 0, 0, 0))
    state_out = pl.BlockSpec((1, HEAD, HEAD, n_seq), lambda h: (h, 0, 0, 0))
    return head_rows, per_head, state_in, state_out


def _ret_t(u_t, cos_t, sin_t, gam, gain_b, state_t, n_seq, seq_len, layer):
    n = u_t.shape[1]
    head_rows, per_head, state_in, state_out = _t_specs(n, n_seq)
    return pl.pallas_call(
        functools.partial(_ret_t_kernel, n_seq=n_seq, seq_len=seq_len),
        grid=(N_HEADS,),
        in_specs=[head_rows(0), head_rows(1), head_rows(2), head_rows(3), _const_spec(cos_t.shape),
                  _const_spec(sin_t.shape), pl.BlockSpec((1, 1, n_seq), lambda h: (h, 0, 0)), per_head,
                  state_in(layer)],
        out_specs=[pl.BlockSpec((HEAD, n), lambda h: (h, 0)), state_out],
        out_shape=[jax.ShapeDtypeStruct((GROUP_W, n), F32), jax.ShapeDtypeStruct(state_t.shape[1:], F32)],
        scratch_shapes=[pltpu.VMEM((HEAD, n), F32)] * 2,
        compiler_params=_params(("parallel",)),
        name="ret_sample_t",
    )(u_t, u_t, u_t, u_t, cos_t, sin_t, gam, gain_b, state_t)


def _hg_t(u_t, lb_b, gain_b, state_t, n_seq, seq_len, layer):
    n = u_t.shape[1]
    head_rows, per_head, state_in, state_out = _t_specs(n, n_seq)
    return pl.pallas_call(
        functools.partial(_hg_t_kernel, n_seq=n_seq, seq_len=seq_len),
        grid=(N_HEADS,),
        in_specs=[head_rows(0), head_rows(1), head_rows(2), head_rows(3), per_head, per_head, state_in(layer)],
        out_specs=[pl.BlockSpec((HEAD, n), lambda h: (h, 0)), state_out],
        out_shape=[jax.ShapeDtypeStruct((GROUP_W, n), F32), jax.ShapeDtypeStruct(state_t.shape[1:], F32)],
        scratch_shapes=[pltpu.VMEM((HEAD, n), F32)] * 3,
        compiler_params=_params(("parallel",)),
        name="hgrn_sample_t",
    )(u_t, u_t, u_t, u_t, lb_b, gain_b, state_t)


def _rope_tables_t(pos0, seq_len, n_seq):
    half = HEAD // 2
    pos = pos0 + jnp.arange(seq_len, dtype=F32)
    inv = ROPE_BASE ** (-jnp.arange(half, dtype=F32) / half)
    ang = pos[:, None] * inv
    cos = jnp.tile(jnp.cos(ang).T, (2, 1))
    sin = jnp.sin(ang).T
    sin_signed = jnp.concatenate([-sin, sin], axis=0)
    return jnp.repeat(cos, n_seq, axis=1), jnp.repeat(sin_signed, n_seq, axis=1)


def _block_diag(w):
    h, i, j = w.shape
    eye = jnp.eye(h, dtype=w.dtype)
    return (eye[:, None, :, None] * w[:, :, None, :]).reshape(h * i, h * j)


def _rope_tables(pos0, seq_len, split_halves):
    half = HEAD // 2
    pos = pos0 + jnp.arange(seq_len, dtype=F32)
    inv = ROPE_BASE ** (-jnp.arange(half, dtype=F32) / half)
    ang = pos[:, None] * inv
    cos = jnp.tile(jnp.cos(ang), (1, 2 * N_HEADS))
    sin = jnp.sin(ang)
    if split_halves:
        sin_signed = jnp.concatenate([jnp.tile(-sin, (1, N_HEADS)), jnp.tile(sin, (1, N_HEADS))], axis=-1)
    else:
        sin_signed = jnp.tile(jnp.concatenate([-sin, sin], axis=-1), (1, N_HEADS))
    return cos, sin_signed


def _layer_weights(P, l):
    row = lambda a: a.reshape(1, -1)
    pad128 = lambda a: jnp.pad(a, (0, 128 - a.shape[0])).reshape(1, 128)
    lb_sm = jax.nn.softmax(P['hg_lb'].astype(F32), axis=0)
    lb = (jnp.cumsum(lb_sm, axis=0) - lb_sm[0])[l]
    return dict(
        w_in=P['w_in_prepped'], w_in_t=P['w_in_t'],
        ln_mix_pre=row(P['ln_mix_pre'][l]), ln_mix_post=row(P['ln_mix_post'][l]),
        ln_xa_pre=row(P['ln_xa_pre'][l]), ln_xa_post=row(P['ln_xa_post'][l]),
        ln_ffn_pre=row(P['ln_ffn_pre'][l]), ln_ffn_post=row(P['ln_ffn_post'][l]),
        ssd_cw=P['ssd_conv_w'][l], ssd_cb=row(P['ssd_conv_b'][l]),
        ssd_dtb=pad128(P['ssd_dt_bias'][l]), ssd_alog=pad128(P['ssd_A_log'][l]),
        ssd_d=row(jnp.repeat(P['ssd_D'][l], HEAD)), ssd_norm=row(P['ssd_norm'][l]),
        ret_norm=row(P['ret_norm'][l]),
        rg_cw=P['rg_conv_w'][l], rg_cb=row(P['rg_conv_b'][l]),
        rg_wa=_block_diag(P['rg_wa'][l]).astype(BF16), rg_ba=row(P['rg_ba'][l]),
        rg_wx=_block_diag(P['rg_wx'][l]).astype(BF16), rg_bx=row(P['rg_bx'][l]),
        rg_lam=row(P['rg_lambda'][l]),
        hg_lb=row(lb), hg_norm=row(P['hg_norm'][l]),
        w_out=P['w_out_bf16'], w_xq=P['w_xq_bf16'], w_xo=P['w_xo_bf16'], w_gu=P['w_gu_bf16'],
        w_down=P['w_down_bf16'], ln_mem=row(P['ln_mem'][l]), w_xkv=P['w_xkv_bf16'],
    )


def _layer(x, W, k_mem, v_mem, st, n_seq, seq_len, pos0, layer):
    if st is None:
        u_ssd, u_ret, u_rg, u_hg, u_dt = _norm_matmul(
            x, W['ln_mix_pre'], W['w_in'], (IN_COLS_SSD, IN_COLS_RET_SPLIT, IN_COLS_RG, IN_COLS_HG, IN_COLS_DT),
            "in_proj", layer=layer)
        cos, sin_signed = _rope_tables(pos0, seq_len, split_halves=True)
        y_ssd, s_ssd, b_ssd = _ssd(u_ssd, u_dt, None, None, W['ssd_cw'], W['ssd_cb'], W['ssd_dtb'],
                                   W['ssd_alog'], W['ssd_d'], W['ssd_norm'], n_seq, seq_len, layer)
        y_ret, s_ret = _ret(u_ret, cos, sin_signed, None, W['ret_norm'], n_seq, seq_len, layer)
        y_rg, h_rg, b_rg = _rg(u_rg, None, None, W['rg_cw'], W['rg_cb'], W['rg_wa'], W['rg_ba'],
                               W['rg_wx'], W['rg_bx'], W['rg_lam'], n_seq, seq_len)
        h_rg = h_rg.reshape(n_seq, GROUP_W)
        y_hg, s_hg = _hg(u_hg, None, W['hg_lb'], W['hg_norm'], n_seq, seq_len, layer)
    else:
        ssd_s, ssd_buf, ret_s, rg_h, rg_buf, hg_s = st
        u_ssd, u_rg, u_dt = _norm_matmul(x, W['ln_mix_pre'], W['w_in'], (IN_COLS_SSD, IN_COLS_RG, IN_COLS_DT),
                                         "in_proj", layer=layer)
        to_tb = lambda a: a.reshape(n_seq, seq_len, -1).transpose(1, 0, 2).reshape(n_seq * seq_len, -1)
        from_t = lambda a: a.reshape(-1, seq_len, n_seq).transpose(2, 1, 0).reshape(n_seq * seq_len, -1)
        bcast = lambda g: jnp.broadcast_to(g.reshape(-1, 1), (g.size, n_seq))
        ut_ret, ut_hg = _in_proj_t(to_tb(x), W['ln_mix_pre'], W['w_in_t'], layer)
        cos_t, sin_t = _rope_tables_t(pos0, seq_len, n_seq)
        log_gamma = jnp.log(1.0 - jnp.exp2(-5.0 - jnp.arange(N_HEADS, dtype=F32)))
        gam = jnp.broadcast_to(jnp.exp(log_gamma)[:, None, None], (N_HEADS, 1, n_seq))
        yt_ret, s_ret = _ret_t(ut_ret, cos_t, sin_t, gam, bcast(W['ret_norm']), ret_s, n_seq, seq_len, layer)
        yt_hg, s_hg = _hg_t(ut_hg, bcast(W['hg_lb']), bcast(W['hg_norm']), hg_s, n_seq, seq_len, layer)
        y_ret, y_hg = from_t(yt_ret), from_t(yt_hg)
        pad_rows = lambda b: jnp.pad(b, ((0, 0), (0, 1), (0, 0))).reshape(n_seq * 4, b.shape[-1])
        y_ssd, s_ssd, b_ssd = _ssd(u_ssd, u_dt, ssd_s, pad_rows(ssd_buf), W['ssd_cw'], W['ssd_cb'],
                                   W['ssd_dtb'], W['ssd_alog'], W['ssd_d'], W['ssd_norm'], n_seq, seq_len, layer)
        b_ssd = b_ssd.reshape(n_seq, 4, -1)[:, :CONV_W - 1]
        y_rg, h_rows, b_rg = _rg(u_rg, jnp.repeat(rg_h, seq_len, axis=0), pad_rows(rg_buf), W['rg_cw'],
                                 W['rg_cb'], W['rg_wa'], W['rg_ba'], W['rg_wx'], W['rg_bx'], W['rg_lam'],
                                 n_seq, seq_len)
        h_rg = h_rows.reshape(n_seq, seq_len, GROUP_W)[:, seq_len - 1]
        b_rg = b_rg.reshape(n_seq, 4, -1)[:, :CONV_W - 1]
    ys = (y_ssd, y_ret, y_rg, y_hg)
    if st is None:
        x3 = _xattn_ffn(ys, x, W['w_out'], W['ln_mix_post'], k_mem, v_mem, W['ln_xa_pre'], W['w_xq'], W['w_xo'],
                        W['ln_xa_post'], W['ln_ffn_pre'], W['w_gu'], W['w_down'], W['ln_ffn_post'], seq_len,
                        layer)
    else:
        x1, q = _out_proj(ys, x, W['w_out'], W['ln_mix_post'], W['ln_xa_pre'], W['w_xq'], layer)
        o = _attention_sample(q, k_mem, v_mem, layer, seq_len)
        x3 = _ffn(o, x1, W['w_xo'], W['ln_xa_post'], W['ln_ffn_pre'], W['w_gu'], W['w_down'],
                  W['ln_ffn_post'], layer)
    return x3, (s_ssd, b_ssd, s_ret, h_rg, b_rg, s_hg)


def kernel(x_prompt, x_sample, state_ssd, state_ssd_conv, state_ret, state_rglru, state_rglru_conv, state_hgrn, cache_mem_k, cache_mem_v, mem_prompt, ln_mix_pre, ln_mix_post, ln_xa_pre, ln_xa_post, ln_ffn_pre, ln_ffn_post, w_in, ssd_conv_w, ssd_conv_b, ssd_dt_bias, ssd_A_log, ssd_D, ssd_norm, ret_norm, rg_conv_w, rg_conv_b, rg_wa, rg_ba, rg_wx, rg_bx, rg_lambda, hg_lb, hg_norm, w_out, ln_mem, w_xq, w_xkv, w_xo, w_gu, w_down):
    P = dict(ln_mix_pre=ln_mix_pre, ln_mix_post=ln_mix_post, ln_xa_pre=ln_xa_pre, ln_xa_post=ln_xa_post,
             ln_ffn_pre=ln_ffn_pre, ln_ffn_post=ln_ffn_post, w_in=w_in, ssd_conv_w=ssd_conv_w,
             ssd_conv_b=ssd_conv_b, ssd_dt_bias=ssd_dt_bias, ssd_A_log=ssd_A_log, ssd_D=ssd_D,
             ssd_norm=ssd_norm, ret_norm=ret_norm, rg_conv_w=rg_conv_w, rg_conv_b=rg_conv_b,
             rg_wa=rg_wa, rg_ba=rg_ba, rg_wx=rg_wx, rg_bx=rg_bx, rg_lambda=rg_lambda, hg_lb=hg_lb,
             hg_norm=hg_norm, w_out=w_out, ln_mem=ln_mem, w_xq=w_xq, w_xkv=w_xkv, w_xo=w_xo,
             w_gu=w_gu, w_down=w_down)
    P['w_in_prepped'], P['w_in_t'] = _in_weight_prep(w_in)
    for name in ('w_out', 'w_xq', 'w_xo', 'w_gu', 'w_down', 'w_xkv'):
        P[name + '_bf16'] = P[name].astype(BF16)
    ssd_t = jnp.swapaxes(state_ssd, -1, -2)
    ret_t = jnp.transpose(state_ret, (0, 2, 3, 4, 1))
    hgrn_t = jnp.transpose(state_hgrn, (0, 2, 3, 4, 1))
    depth = w_in.shape[0]
    bp, tp, d = x_prompt.shape
    bs, ts, _ = x_sample.shape
    n_mem = mem_prompt.shape[1]
    y_p = x_prompt.reshape(bp * tp, d)
    y_s = x_sample.reshape(bs * ts, d)
    mem = mem_prompt.reshape(bp * n_mem, d)
    p_st, s_st, p_mk, p_mv = [], [], [], []
    for l in range(depth):
        W = _layer_weights(P, l)
        mk, mv = _norm_matmul(mem, W['ln_mem'], W['w_xkv'], (((0, d),), ((d, 2 * d),)), "mem_kv", layer=l)
        mk = mk.reshape(bp, n_mem, d)
        mv = mv.reshape(bp, n_mem, d)
        y_p, st_p = _layer(y_p, W, mk, mv, None, bp, tp, 0.0, l)
        p_st.append(st_p)
        p_mk.append(mk.reshape(bp, n_mem, XA_H, XA_HD))
        p_mv.append(mv.reshape(bp, n_mem, XA_H, XA_HD))
        st_l = (ssd_t, state_ssd_conv[l], ret_t, state_rglru[l], state_rglru_conv[l], hgrn_t)
        y_s, st_s = _layer(y_s, W, cache_mem_k, cache_mem_v, st_l, bs, ts, float(PAST_LEN), l)
        s_st.append(st_s)
    stack = lambda sts, i: jnp.stack([s[i] for s in sts], axis=0)
    return (y_p.reshape(bp, tp, d), y_s.reshape(bs, ts, d),
            jnp.swapaxes(stack(p_st, 0), -1, -2), stack(p_st, 1), stack(p_st, 2), stack(p_st, 3), stack(p_st, 4),
            stack(p_st, 5),
            jnp.stack(p_mk, axis=0), jnp.stack(p_mv, axis=0),
            jnp.swapaxes(stack(s_st, 0), -1, -2), stack(s_st, 1), jnp.transpose(stack(s_st, 2), (0, 4, 1, 2, 3)),
            stack(s_st, 3), stack(s_st, 4), jnp.transpose(stack(s_st, 5), (0, 4, 1, 2, 3)))
```

```python
import functools
import math

import jax
import jax.numpy as jnp
from jax import lax
from jax.experimental import pallas as pl
from jax.experimental.pallas import tpu as pltpu

F32 = jnp.float32
BF16 = jnp.bfloat16
EPS = 1e-6
LOG2_E = 1.4426950408889634

GROUP_W = 256
HEAD = 64
N_HEADS = 4
SSD_N = 128
SSD_G = 2
CONV_W = 4
XA_H = 4
XA_HD = 256
ROPE_BASE = 10000.0
RG_C = 8.0
CHUNK = 64
PAST_LEN = 16384

ROW_TILE = 512
MIX_TILE = 1024
RG_TILE = 256
SAMPLE_SEQS = 16
ATT_SEQS = 8
FF_TILE = 256
VMEM_LIMIT = 56 * 1024 * 1024


def _bdot(a, b):
    return jnp.dot(a.astype(BF16), b.astype(BF16), preferred_element_type=F32)


def _bdot_nt(a, b):
    return lax.dot_general(a.astype(BF16), b.astype(BF16), (((1,), (1,)), ((), ())),
                           preferred_element_type=F32)


def _bdot_tn(a, b):
    return lax.dot_general(a.astype(BF16), b.astype(BF16), (((0,), (0,)), ((), ())),
                           preferred_element_type=F32)


def _split3(x):
    hi = x.astype(BF16)
    r = x - hi.astype(F32)
    mid = r.astype(BF16)
    lo = (r - mid.astype(F32)).astype(BF16)
    return hi, mid, lo


def _sel_dot(sel, x):
    hi, mid, lo = _split3(x)
    d = lambda y: jnp.dot(sel, y, preferred_element_type=F32)
    return (d(hi) + d(mid)) + d(lo)


def _dot_sel(x, sel):
    hi, mid, lo = _split3(x)
    d = lambda y: jnp.dot(y, sel, preferred_element_type=F32)
    return (d(hi) + d(mid)) + d(lo)


def _sel_dot_nt(sel, x):
    hi, mid, lo = _split3(x)
    d = lambda y: lax.dot_general(sel, y, (((1,), (1,)), ((), ())), preferred_element_type=F32)
    return (d(hi) + d(mid)) + d(lo)


def _rms(x, g):
    return x * lax.rsqrt(jnp.mean(x * x, axis=-1, keepdims=True) + EPS) * g


def _sigmoid(x):
    return jax.nn.sigmoid(x)


def _silu(x):
    return x * jax.nn.sigmoid(x)


def _softplus(x):
    return jnp.maximum(x, 0.0) + jnp.log1p(jnp.exp(-jnp.abs(x)))


def _gelu_tanh(x):
    c = math.sqrt(2.0 / math.pi)
    return 0.5 * x * (1.0 + jnp.tanh(c * (x + 0.044715 * (x * x * x))))


def _iota(shape, dim):
    return lax.broadcasted_iota(jnp.int32, shape, dim)


def _seg_causal_mask(n, seg_shift):
    r = _iota((n, n), 0)
    c = _iota((n, n), 1)
    return ((r >> seg_shift) == (c >> seg_shift)) & (c <= r)


def _seg_cumsum(x, seg_shift):
    rows = x.shape[0]
    blk = min(rows, 256)
    mask = _seg_causal_mask(blk, seg_shift).astype(BF16)
    return jnp.concatenate([_sel_dot(mask, x[i:i + blk]) for i in range(0, rows, blk)], axis=0)


def _head_expand_mat(width_in=128):
    r = _iota((width_in, GROUP_W), 0)
    c = _iota((width_in, GROUP_W), 1)
    return ((c >> 6) == r).astype(BF16)


def _head_block_mask():
    return (_iota((GROUP_W, GROUP_W), 0) >> 6) == (_iota((GROUP_W, GROUP_W), 1) >> 6)


def _block_diag_rows(x, mask01):
    return jnp.concatenate([x.astype(BF16)] * N_HEADS, axis=0) * mask01


def _head_stat(y, center):
    ones_bd = _head_block_mask().astype(BF16)
    if center:
        y = y - _dot_sel(y, ones_bd) * (1.0 / HEAD)
    return y * lax.rsqrt(_dot_sel(y * y, ones_bd) * (1.0 / HEAD) + EPS)


def _conv_taps_carry(x, xx_ref, w_ref, b_ref, first):
    rows = x.shape[0]

    @pl.when(first)
    def _():
        xx_ref[0:8, :] = jnp.zeros((8, x.shape[1]), F32)

    xx_ref[8:8 + rows, :] = x
    y = b_ref[...] + w_ref[CONV_W - 1:CONV_W, :] * x
    for j in range(CONV_W - 1):
        y = y + w_ref[j:j + 1, :] * xx_ref[pl.ds(8 - (CONV_W - 1) + j, rows), :]
    xx_ref[0:8, :] = xx_ref[rows:rows + 8, :]
    return y


def _conv_taps_seq4(x, buf4, w_ref, b_ref):
    rows = x.shape[0]
    r = _iota((rows, rows), 0)
    c = _iota((rows, rows), 1)
    t = r & 3
    y = b_ref[...] + w_ref[CONV_W - 1:CONV_W, :] * x
    for d in range(1, CONV_W):
        shift = ((c == r - d) & (t >= d)).astype(BF16)
        hist = ((t < d) & (c == (r - t) + (3 + t - d))).astype(BF16)
        y = y + w_ref[CONV_W - 1 - d:CONV_W - d, :] * (_sel_dot(shift, x) + _sel_dot(hist, buf4))
    new_sel = (((r & 3) < 3) & (c == r + 1)).astype(BF16)
    return y, _sel_dot(new_sel, x)


def _norm_matmul_kernel(x_ref, g_ref, w_ref, *o_refs, col_ranges):
    h = _rms(x_ref[...], g_ref[...]).astype(BF16)
    for o_ref, pieces in zip(o_refs, col_ranges):
        off = 0
        for a, b in pieces:
            o_ref[:, off:off + b - a] = jnp.dot(h, w_ref[:, a:b], preferred_element_type=F32)
            off += b - a


def _const_spec(shape):
    nd = len(shape)
    return pl.BlockSpec(shape, lambda *_: (0,) * nd, pipeline_mode=pl.Buffered(1))


def _layer_spec(shape, layer):
    nd = len(shape)
    return pl.BlockSpec((None,) + tuple(shape[1:]), lambda *_: (layer,) + (0,) * (nd - 1),
                        pipeline_mode=pl.Buffered(1))


def _params(sem):
    return pltpu.CompilerParams(dimension_semantics=sem, vmem_limit_bytes=VMEM_LIMIT)


def _in_weight_prep_kernel(wt_ref, o_ref, ot_ref, *, blocks, dt_block, dt_row, dt_n, t_rows):
    off = 0
    for src, n in t_rows:
        ot_ref[off:off + n, :] = wt_ref[pl.ds(src, n), :].astype(BF16)
        off += n
    for j, pieces in enumerate(blocks):
        if j == dt_block:
            dt = wt_ref[pl.ds(dt_row, 8), :]
            dt = jnp.where(_iota(dt.shape, 0) < dt_n, dt, 0.0)
            blk = jnp.concatenate([dt, jnp.zeros((120, dt.shape[1]), F32)], axis=0)
        else:
            blk = jnp.concatenate([wt_ref[pl.ds(src, n), :] for src, n in pieces], axis=0)
        o_ref[:, j * 128:(j + 1) * 128] = blk.T.astype(BF16)


IN_COLS_SSD = ((0, 1024),)
IN_COLS_RG = ((2048, 2560),)
IN_COLS_HG = ((2560, 3584),)
IN_COLS_DT = ((3584, 3712),)
IN_COLS_RET_SPLIT = ((3712, 4224), (1536, 2048))


def _in_weight_prep(w_in):
    depth, d, d_in = w_in.shape
    z0 = GROUP_W + (GROUP_W + 2 * SSD_G * SSD_N)
    d0 = z0 + N_HEADS
    blocks = [((src, 128),) for src in tuple(range(0, z0, 128)) + tuple(range(d0, d_in, 128))]
    dt_block = len(blocks)
    blocks.append(())
    half = HEAD // 2
    for base in (d0, d0 + GROUP_W):
        for j in range(2):
            blocks.append(tuple((base + h * HEAD + j * half, half) for h in range(N_HEADS)))
    n_out = len(blocks) * 128
    t_rows = ((d0, 4 * GROUP_W), (d0 + 6 * GROUP_W, 4 * GROUP_W))
    n_t = sum(n for _, n in t_rows)
    wt = jnp.swapaxes(w_in, 1, 2)
    return pl.pallas_call(
        functools.partial(_in_weight_prep_kernel, blocks=tuple(blocks), dt_block=dt_block, dt_row=z0,
                          dt_n=N_HEADS, t_rows=t_rows),
        grid=(depth,),
        in_specs=[pl.BlockSpec((None, d_in, d), lambda l: (l, 0, 0), pipeline_mode=pl.Buffered(1))],
        out_specs=[pl.BlockSpec((None, d, n_out), lambda l: (l, 0, 0)),
                   pl.BlockSpec((None, n_t, d), lambda l: (l, 0, 0))],
        out_shape=[jax.ShapeDtypeStruct((depth, d, n_out), BF16), jax.ShapeDtypeStruct((depth, n_t, d), BF16)],
        compiler_params=_params(("parallel",)),
        name="in_weight_prep",
    )(wt)


def _in_proj_t_kernel(x_ref, g_ref, wt_ref, *o_refs):
    h = _rms(x_ref[...], g_ref[...]).astype(BF16)
    off = 0
    for o_ref in o_refs:
        f = o_ref.shape[0]
        o_ref[...] = lax.dot_general(wt_ref[off:off + f, :], h, (((1,), (1,)), ((), ())),
                                     preferred_element_type=F32)
        off += f


def _in_proj_t(x, g, wt, layer):
    n, d = x.shape
    f = wt.shape[1] // 2
    return pl.pallas_call(
        _in_proj_t_kernel,
        grid=(1,),
        in_specs=[_const_spec((n, d)), _const_spec((1, d)), _layer_spec(wt.shape, layer)],
        out_specs=[_const_spec((f, n))] * 2,
        out_shape=[jax.ShapeDtypeStruct((f, n), F32)] * 2,
        compiler_params=_params(("arbitrary",)),
        name="in_proj_t",
    )(x, g, wt)


def _mem_kv_kernel(x_ref, g_ref, w_ref, k_ref, v_ref, k5_ref, v5_ref):
    h = _rms(x_ref[...], g_ref[...]).astype(BF16)
    d = x_ref.shape[1]
    k = jnp.dot(h, w_ref[:, 0:d], preferred_element_type=F32)
    v = jnp.dot(h, w_ref[:, d:2 * d], preferred_element_type=F32)
    k_ref[...] = k
    v_ref[...] = v
    k5_ref[...] = k.reshape(k5_ref.shape)
    v5_ref[...] = v.reshape(v5_ref.shape)


def _mem_kv(mem, g, w, layer, n_mem):
    n, d = mem.shape
    tm = min(ROW_TILE, n)
    seqs = tm // n_mem
    row = pl.BlockSpec((tm, d), lambda i: (i, 0))
    blk5 = pl.BlockSpec((seqs, n_mem, XA_H, XA_HD), lambda i: (i, 0, 0, 0))
    shape5 = jax.ShapeDtypeStruct((n // n_mem, n_mem, XA_H, XA_HD), F32)
    return pl.pallas_call(
        _mem_kv_kernel,
        grid=(n // tm,),
        in_specs=[row, _const_spec((1, d)), _wspec(w, layer)],
        out_specs=[row, row, blk5, blk5],
        out_shape=[jax.ShapeDtypeStruct((n, d), F32)] * 2 + [shape5] * 2,
        compiler_params=_params(("parallel",)),
        name="mem_kv",
    )(mem, g, w)


def _wspec(w, layer):
    return _layer_spec(w.shape, layer) if w.ndim == 3 else _const_spec(w.shape)


def _norm_matmul(x, g, w, col_ranges, name, layer=None):
    n, d = x.shape
    tm = min(ROW_TILE, n)
    w_spec = _wspec(w, layer)
    widths = [sum(b - a for a, b in pieces) for pieces in col_ranges]
    return pl.pallas_call(
        functools.partial(_norm_matmul_kernel, col_ranges=col_ranges),
        grid=(n // tm,),
        in_specs=[pl.BlockSpec((tm, d), lambda i: (i, 0)), _const_spec((1, d)), w_spec],
        out_specs=[pl.BlockSpec((tm, wd), lambda i: (i, 0)) for wd in widths],
        out_shape=[jax.ShapeDtypeStruct((n, wd), F32) for wd in widths],
        compiler_params=_params(("parallel",)),
        name=name,
    )(x, g, w)


def _out_proj_kernel(ys_ref, yr_ref, yg_ref, yh_ref, x_ref, wo_ref, gpost_ref, gpre_ref, wq_ref, x1_ref, q_ref):
    x1 = _mix_out(ys_ref, yr_ref, yg_ref, yh_ref, x_ref, wo_ref, gpost_ref)
    x1_ref[...] = x1
    q_ref[...] = _bdot(_rms(x1, gpre_ref[...]), wq_ref[...])


def _out_proj(ys, x, w_out, g_post, g_pre, w_xq, layer):
    n, d = x.shape
    tm = min(ROW_TILE, n)
    row = lambda wd: pl.BlockSpec((tm, wd), lambda i: (i, 0))
    vec = _const_spec((1, d))
    return pl.pallas_call(
        _out_proj_kernel,
        grid=(n // tm,),
        in_specs=[row(GROUP_W)] * 4 + [row(d), _wspec(w_out, layer), vec, vec, _wspec(w_xq, layer)],
        out_specs=[row(d)] * 2,
        out_shape=[jax.ShapeDtypeStruct((n, d), F32)] * 2,
        compiler_params=_params(("parallel",)),
        name="out_proj",
    )(*ys, x, w_out, g_post, g_pre, w_xq)


def _ffn_tail(o, x1, wxo_ref, gxa_ref, gpre_ref, wgu_ref, wdn_ref, gpost_ref, d_ff):
    a = _bdot(o, wxo_ref[...])
    x2 = x1 + _rms(a, gxa_ref[...])
    h = _rms(x2, gpre_ref[...]).astype(BF16)
    acc = jnp.zeros(x2.shape, F32)
    for j in range(0, d_ff, FF_TILE):
        g = jnp.dot(h, wgu_ref[:, j:j + FF_TILE], preferred_element_type=F32)
        u = jnp.dot(h, wgu_ref[:, d_ff + j:d_ff + j + FF_TILE], preferred_element_type=F32)
        act = (_silu(g) * u).astype(BF16)
        acc = acc + jnp.dot(act, wdn_ref[j:j + FF_TILE, :], preferred_element_type=F32)
    return x2 + _rms(acc, gpost_ref[...])


def _ffn_kernel(o_ref, x1_ref, wxo_ref, gxa_ref, gpre_ref, wgu_ref, wdn_ref, gpost_ref, x3_ref, *, d_ff):
    x3_ref[...] = _ffn_tail(o_ref[...], x1_ref[...], wxo_ref, gxa_ref, gpre_ref, wgu_ref, wdn_ref, gpost_ref,
                            d_ff)


def _ffn(o, x1, w_xo, g_xa, g_pre, w_gu, w_down, g_post, layer):
    n, d = x1.shape
    d_ff = w_down.shape[-2]
    tm = min(ROW_TILE, n)
    row = pl.BlockSpec((tm, d), lambda i: (i, 0))
    vec = _const_spec((1, d))
    return pl.pallas_call(
        functools.partial(_ffn_kernel, d_ff=d_ff),
        grid=(n // tm,),
        in_specs=[row, row, _wspec(w_xo, layer), vec, vec, _wspec(w_gu, layer), _wspec(w_down, layer), vec],
        out_specs=row,
        out_shape=jax.ShapeDtypeStruct((n, d), F32),
        compiler_params=_params(("parallel",)),
        name="attn_out_ffn",
    )(o, x1, w_xo, g_xa, g_pre, w_gu, w_down, g_post)


def _mix_out(ys_ref, yr_ref, yg_ref, yh_ref, x_ref, wo_ref, gpost_ref):
    mix = _bdot(ys_ref[...], wo_ref[0:GROUP_W, :])
    mix = mix + _bdot(yr_ref[...], wo_ref[GROUP_W:2 * GROUP_W, :])
    mix = mix + _bdot(yg_ref[...], wo_ref[2 * GROUP_W:3 * GROUP_W, :])
    mix = mix + _bdot(yh_ref[...], wo_ref[3 * GROUP_W:4 * GROUP_W, :])
    return x_ref[...] + _rms(mix, gpost_ref[...])


def _xattn_ffn_kernel(ys_ref, yr_ref, yg_ref, yh_ref, x_ref, wo_ref, gmix_ref, k_ref, v_ref, gq_ref, wq_ref,
                      wxo_ref, gxa_ref, gpre_ref, wgu_ref, wdn_ref, gpost_ref, x3_ref, o_scr, *, d_ff):
    x1 = _mix_out(ys_ref, yr_ref, yg_ref, yh_ref, x_ref, wo_ref, gmix_ref)
    q = _bdot(_rms(x1, gq_ref[...]), wq_ref[...])
    scale = XA_HD ** -0.5
    for h in range(XA_H):
        sl = slice(h * XA_HD, (h + 1) * XA_HD)
        s = _bdot_nt(q[:, sl], k_ref[0, :, sl]) * scale
        o_scr[:, sl] = _softmax_pv(s, v_ref[0, :, sl])
    x3_ref[...] = _ffn_tail(o_scr[...], x1, wxo_ref, gxa_ref, gpre_ref, wgu_ref, wdn_ref, gpost_ref, d_ff)


def _xattn_ffn(ys, x, w_out, g_mix, k, v, g_q, w_xq, w_xo, g_xa, g_pre, w_gu, w_down, g_post, seq_len, layer):
    n, d = x.shape
    n_seq, n_mem, _ = k.shape
    d_ff = w_down.shape[-2]
    nt = seq_len // ROW_TILE
    row = lambda wd: pl.BlockSpec((ROW_TILE, wd), lambda b, t: (b * nt + t, 0))
    kv = pl.BlockSpec((1, n_mem, d), lambda b, t: (b, 0, 0))
    vec = _const_spec((1, d))
    return pl.pallas_call(
        functools.partial(_xattn_ffn_kernel, d_ff=d_ff),
        grid=(n_seq, nt),
        in_specs=[row(GROUP_W)] * 4 + [row(d), _wspec(w_out, layer), vec, kv, kv, vec, _wspec(w_xq, layer),
                                       _wspec(w_xo, layer), vec, vec, _wspec(w_gu, layer),
                                       _wspec(w_down, layer), vec],
        out_specs=row(d),
        out_shape=jax.ShapeDtypeStruct((n, d), F32),
        scratch_shapes=[pltpu.VMEM((ROW_TILE, d), F32)],
        compiler_params=_params(("parallel", "parallel")),
        name="xattn_ffn",
    )(*ys, x, w_out, g_mix, k, v, g_q, w_xq, w_xo, g_xa, g_pre, w_gu, w_down, g_post)


def _softmax_pv(s, v):
    m = jnp.max(s, axis=-1, keepdims=True)
    p = jnp.exp(s - m)
    return _bdot(p, v) / jnp.sum(p, axis=-1, keepdims=True)


def _attn_sample_kernel(q_ref, k_ref, v_ref, o_ref, *, seqs, seg):
    scale = XA_HD ** -0.5
    rows = seqs * seg
    n_mem = k_ref.shape[2]
    wq = jnp.concatenate([q_ref[:, h * XA_HD:(h + 1) * XA_HD] for h in range(XA_H)], axis=0)
    wshape = (XA_H * rows, n_mem * XA_H)
    valid = (_iota(wshape, 0) >> (rows.bit_length() - 1)) == (_iota(wshape, 1) & (XA_H - 1))
    row_seq = _iota((rows, XA_HD), 0) >> (seg.bit_length() - 1)
    accs = [jnp.zeros((rows, XA_HD), F32) for _ in range(XA_H)]
    for i in range(seqs):
        k2 = k_ref[0, i].reshape(n_mem * XA_H, XA_HD)
        v2 = v_ref[0, i].reshape(n_mem * XA_H, XA_HD)
        s = jnp.where(valid, _bdot_nt(wq, k2) * scale, -jnp.inf)
        o = _softmax_pv(s, v2)
        for h in range(XA_H):
            accs[h] = jnp.where(row_seq == i, o[h * rows:(h + 1) * rows], accs[h])
    for h in range(XA_H):
        o_ref[:, h * XA_HD:(h + 1) * XA_HD] = accs[h]


def _attention_sample(q, cache_k, cache_v, layer, seq_len):
    n, d = q.shape
    _, n_seq, n_mem, heads, hd = cache_k.shape
    kv = pl.BlockSpec((1, ATT_SEQS, n_mem, heads, hd), lambda i: (layer, i, 0, 0, 0))
    row = pl.BlockSpec((ATT_SEQS * seq_len, d), lambda i: (i, 0))
    return pl.pallas_call(
        functools.partial(_attn_sample_kernel, seqs=ATT_SEQS, seg=seq_len),
        grid=(n_seq // ATT_SEQS,),
        in_specs=[row, kv, kv],
        out_specs=row,
        out_shape=jax.ShapeDtypeStruct((n, d), F32),
        compiler_params=_params(("parallel",)),
        name="xattn_sample",
    )(q, cache_k, cache_v)


def _ssd_intra(q, k, v, cum_x, cum_t, mask):
    sc = _bdot_nt(q, k)
    outs = []
    for hh in range(2):
        col = cum_x[:, hh * HEAD:(hh + 1) * HEAD]
        dec = jnp.exp(jnp.where(mask, col - cum_t[hh:hh + 1, :], -jnp.inf))
        outs.append(_bdot(sc * dec, v[:, hh * HEAD:(hh + 1) * HEAD]))
    return jnp.concatenate(outs, axis=-1)


def _ssd_common(u, dtraw_ref, dtb_ref, alog_ref, conv, seg_shift):
    rows = u.shape[0]
    xbc = _silu(conv)
    xs = xbc[:, 0:GROUP_W]
    bm = xbc[:, GROUP_W:2 * GROUP_W]
    cm = xbc[:, 2 * GROUP_W:3 * GROUP_W]
    dt = _softplus(dtraw_ref[...] + dtb_ref[...])
    la = -jnp.exp(alog_ref[...]) * dt
    cum = _seg_cumsum(la, seg_shift)
    expand = _head_expand_mat()
    dt_x = _dot_sel(dt, expand)
    cum_x = _dot_sel(cum, expand)
    sel8 = (_iota((8, 128), 0) == _iota((8, 128), 1)).astype(BF16)
    cum_t = _sel_dot_nt(sel8, cum)
    expand_full = ((_iota((128, 4 * 128), 1) >> 7) == _iota((128, 4 * 128), 0)).astype(BF16)
    cum_full = _dot_sel(cum, expand_full)
    return xs, bm, cm, dt_x, cum_x, cum_t, cum_full


def _ssd_finish(y, xs, z, d_ref, g_ref):
    y = y + d_ref[...] * xs
    return _rms(y * _silu(z), g_ref[...])


def _ssd_prompt_kernel(u_ref, dtraw_ref, cw_ref, cb_ref, dtb_ref, alog_ref, d_ref, g_ref,
                       y_ref, s_out_ref, buf_out_ref, s_ref, xx_ref, y_scr):
    t = pl.program_id(1)
    rows = u_ref.shape[0]

    @pl.when(t == 0)
    def _():
        s_ref[...] = jnp.zeros(s_ref.shape, F32)

    u = u_ref[...]
    conv = _conv_taps_carry(u[:, GROUP_W:], xx_ref, cw_ref, cb_ref, t == 0)
    xs, bm, cm, dt_x, cum_x, _, cum_full = _ssd_common(u, dtraw_ref, dtb_ref, alog_ref, conv, 6)
    v = xs * dt_x
    ecum_x = jnp.exp(cum_x)
    hmask01 = _head_block_mask().astype(BF16)
    gmask01 = ((_iota((GROUP_W, GROUP_W), 0) >> 7) == (_iota((GROUP_W, GROUP_W), 1) >> 7)).astype(BF16)
    wrow = _iota((CHUNK, GROUP_W), 0)
    wcol = _iota((CHUNK, GROUP_W), 1) & (CHUNK - 1)
    n_chunks = rows // CHUNK
    groups = [slice(g * 128, (g + 1) * 128) for g in range(SSD_G)]
    a_wide, ds_t = [], []
    for c in range(n_chunks):
        rs = slice(c * CHUNK, (c + 1) * CHUNK)
        cum_c = cum_x[rs]
        last_x = cum_x[c * CHUNK + CHUNK - 1:(c + 1) * CHUNK, :]
        vend = v[rs] * jnp.exp(last_x - cum_c)
        cum_row = jnp.sum(jnp.where(wrow == wcol, cum_c, 0.0), axis=0, keepdims=True)
        dec = jnp.exp(jnp.where(wcol <= wrow, cum_c - cum_row, -jnp.inf))
        a_wide.append(_bdot_nt(cm[rs], _block_diag_rows(bm[rs], gmask01)) * dec)
        ds_t.append([_bdot_tn(vend[:, ls], bm[rs, ls]) for ls in groups])
    s_in = []
    s_cur = [s_ref[ls, :] for ls in groups]
    for c in range(n_chunks):
        last_full = cum_full[c * CHUNK + CHUNK - 1:(c + 1) * CHUNK, :]
        s_in.append([x.astype(BF16) for x in s_cur])
        for g in range(SSD_G):
            decay = jnp.concatenate(
                [jnp.broadcast_to(jnp.exp(last_full[:, h * 128:(h + 1) * 128]), (HEAD, 128))
                 for h in (2 * g, 2 * g + 1)], axis=0)
            s_cur[g] = decay * s_cur[g] + ds_t[c][g]
    for g, ls in enumerate(groups):
        s_ref[ls, :] = s_cur[g]
    for c in range(n_chunks):
        rs = slice(c * CHUNK, (c + 1) * CHUNK)
        y_intra = _bdot(a_wide[c], _block_diag_rows(v[rs], hmask01))
        y_inter = [_bdot_nt(cm[rs, ls], s_in[c][g]) for g, ls in enumerate(groups)]
        y_scr[rs, :] = y_intra + jnp.concatenate(y_inter, axis=-1) * ecum_x[rs]
    y_ref[...] = _ssd_finish(y_scr[...], xs, u[:, 0:GROUP_W], d_ref, g_ref)

    @pl.when(t == pl.num_programs(1) - 1)
    def _():
        for h in range(N_HEADS):
            s_out_ref[0, h] = s_ref[h * HEAD:(h + 1) * HEAD, :]
        buf_out_ref[0] = xx_ref[pl.ds(8 - (CONV_W - 1), CONV_W - 1), :]


def _ssd_sample_kernel(u_ref, dtraw_ref, s0_ref, buf_ref, cw_ref, cb_ref, dtb_ref, alog_ref, d_ref,
                       g_ref, y_ref, s_out_ref, buf_out_ref, *, seqs):
    rows = u_ref.shape[0]
    u = u_ref[...]
    conv, new_buf = _conv_taps_seq4(u[:, GROUP_W:], buf_ref[...], cw_ref, cb_ref)
    buf_out_ref[...] = new_buf
    mask = _seg_causal_mask(rows, 2)
    xs, bm, cm, dt_x, cum_x, cum_t, cum_full = _ssd_common(u, dtraw_ref, dtb_ref, alog_ref, conv, 2)
    v = xs * dt_x
    ecum_x = jnp.exp(cum_x)
    pick_last = (_iota((rows, rows), 1) == (_iota((rows, rows), 0) | 3)).astype(BF16)
    last_x = _sel_dot(pick_last, cum_x)
    vend = v * jnp.exp(last_x - cum_x)
    row_seq = _iota((rows, 128), 0) >> 2
    ys = []
    for g in range(SSD_G):
        ls = slice(g * 128, (g + 1) * 128)
        q = cm[:, ls]
        k = bm[:, ls]
        y_g = _ssd_intra(q, k, v[:, ls], cum_x[:, ls], cum_t[2 * g:2 * g + 2, :], mask)
        y_int = jnp.zeros((rows, 128), F32)
        for i in range(seqs):
            in_seq = row_seq == i
            s_prev = jnp.concatenate([s0_ref[i, 2 * g], s0_ref[i, 2 * g + 1]], axis=0)
            y_int = jnp.where(in_seq, _bdot_nt(q, s_prev), y_int)
            ds = _bdot_tn(jnp.where(in_seq, vend[:, ls], 0.0), k)
            for hh in range(2):
                h = 2 * g + hh
                decay = jnp.exp(cum_full[4 * i + 3:4 * i + 4, h * 128:(h + 1) * 128])
                s_out_ref[i, h] = decay * s0_ref[i, h] + ds[hh * HEAD:(hh + 1) * HEAD, :]
        ys.append(y_g + y_int * ecum_x[:, ls])
    y_ref[...] = _ssd_finish(jnp.concatenate(ys, axis=-1), xs, u[:, 0:GROUP_W], d_ref, g_ref)


def _ssd(u, dtraw, state, buf, cw, cb, dtb, alog, d_x, gain, n_seq, seq_len, layer):
    n = u.shape[0]
    cch = cw.shape[1]
    vec = lambda wd: _const_spec((1, wd))
    common_in = [_const_spec(cw.shape), vec(cch), vec(128), vec(128), vec(GROUP_W), vec(GROUP_W)]
    if state is None:
        nt = seq_len // MIX_TILE
        row = lambda wd: pl.BlockSpec((MIX_TILE, wd), lambda b, t: (b * nt + t, 0))
        return pl.pallas_call(
            _ssd_prompt_kernel,
            grid=(n_seq, nt),
            in_specs=[row(u.shape[1]), row(128)] + common_in,
            out_specs=[row(GROUP_W),
                       pl.BlockSpec((1, N_HEADS, HEAD, SSD_N), lambda b, t: (b, 0, 0, 0)),
                       pl.BlockSpec((1, CONV_W - 1, cch), lambda b, t: (b, 0, 0))],
            out_shape=[jax.ShapeDtypeStruct((n, GROUP_W), F32),
                       jax.ShapeDtypeStruct((n_seq, N_HEADS, HEAD, SSD_N), F32),
                       jax.ShapeDtypeStruct((n_seq, CONV_W - 1, cch), F32)],
            scratch_shapes=[pltpu.VMEM((GROUP_W, SSD_N), F32), pltpu.VMEM((MIX_TILE + 8, cch), F32),
                            pltpu.VMEM((MIX_TILE, GROUP_W), F32)],
            compiler_params=_params(("parallel", "arbitrary")),
            name="ssd_prompt",
        )(u, dtraw, cw, cb, dtb, alog, d_x, gain)
    rows = SAMPLE_SEQS * seq_len
    row = lambda wd: pl.BlockSpec((rows, wd), lambda i: (i, 0))
    st = pl.BlockSpec((SAMPLE_SEQS, N_HEADS, HEAD, SSD_N), lambda i: (i, 0, 0, 0))
    st_in = pl.BlockSpec((None, SAMPLE_SEQS, N_HEADS, HEAD, SSD_N), lambda i: (layer, i, 0, 0, 0))
    return pl.pallas_call(
        functools.partial(_ssd_sample_kernel, seqs=SAMPLE_SEQS),
        grid=(n_seq // SAMPLE_SEQS,),
        in_specs=[row(u.shape[1]), row(128), st_in, row(cch)] + common_in,
        out_specs=[row(GROUP_W), st, row(cch)],
        out_shape=[jax.ShapeDtypeStruct((n, GROUP_W), F32),
                   jax.ShapeDtypeStruct(state.shape[1:], F32),
                   jax.ShapeDtypeStruct((n, cch), F32)],
        compiler_params=_params(("parallel",)),
        name="ssd_sample",
    )(u, dtraw, state, buf, cw, cb, dtb, alog, d_x, gain)


def _rope_split(x, cos, sin_signed):
    swapped = jnp.concatenate([x[:, 128:], x[:, :128]], axis=1)
    return x * cos + swapped * sin_signed


def _ret_qkv(u_ref, cos_ref, sin_ref):
    u = u_ref[...]
    q = _rope_split(u[:, 0:GROUP_W], cos_ref[...], sin_ref[...])
    k = _rope_split(u[:, GROUP_W:2 * GROUP_W], cos_ref[...], sin_ref[...]) * (HEAD ** -0.5)
    return q, k, u[:, 2 * GROUP_W:3 * GROUP_W], u[:, 3 * GROUP_W:4 * GROUP_W]


def _ret_prompt_kernel(u_ref, cos_ref, sin_ref, ecum_ref, eend_ref, elast_ref, dec_ref, g_ref,
                       y_ref, s_out_ref, s_ref, y_scr):
    t = pl.program_id(1)
    rows = u_ref.shape[0]

    @pl.when(t == 0)
    def _():
        s_ref[...] = jnp.zeros(s_ref.shape, F32)

    q, k, v, gate = _ret_qkv(u_ref, cos_ref, sin_ref)
    r_i = _iota((GROUP_W, GROUP_W), 0)
    c_i = _iota((GROUP_W, GROUP_W), 1)
    kmask01 = ((r_i >> 6) == ((c_i & 127) >> 5)).astype(BF16)
    vmask01 = _head_block_mask().astype(BF16)
    smask = ((r_i & 127) >> 5) == (c_i >> 6)
    chunks = [slice(c * CHUNK, (c + 1) * CHUNK) for c in range(rows // CHUNK)]
    a_wide = [_bdot_nt(q[rs], _block_diag_rows(k[rs], kmask01)) * dec_ref[...] for rs in chunks]
    ds = [jnp.where(smask, _bdot_tn(k[rs], v[rs] * eend_ref[...]), 0.0) for rs in chunks]
    s = s_ref[...]
    s_in = []
    for d in ds:
        s_in.append(s.astype(BF16))
        s = elast_ref[...] * s + d
    s_ref[...] = s
    for rs, a, s_c in zip(chunks, a_wide, s_in):
        y_intra = _bdot(a, _block_diag_rows(v[rs], vmask01))
        y_scr[rs, :] = y_intra + _bdot(q[rs], s_c) * ecum_ref[...]
    y_ref[...] = _silu(gate) * (_head_stat(y_scr[...], True) * g_ref[...])

    @pl.when(t == pl.num_programs(1) - 1)
    def _():
        half = HEAD // 2
        for h in range(N_HEADS):
            vs = slice(h * HEAD, (h + 1) * HEAD)
            s_out_ref[0, h, 0:half, :] = s_ref[h * half:(h + 1) * half, vs]
            s_out_ref[0, h, half:HEAD, :] = s_ref[128 + h * half:128 + (h + 1) * half, vs]


def _ret_tables(block, seg):
    log_gamma = jnp.log(1.0 - jnp.exp2(-5.0 - jnp.arange(N_HEADS, dtype=F32)))
    pos = (jnp.arange(block) % seg).astype(F32)
    cum = (pos[:, None] + 1.0) * log_gamma[None, :]
    last = seg * log_gamma
    rep = lambda a: jnp.repeat(a, HEAD, axis=-1)
    same = (jnp.arange(block)[:, None] // seg) == (jnp.arange(block)[None, :] // seg)
    mask = same & (jnp.arange(block)[None, :] <= jnp.arange(block)[:, None])
    dec = jnp.exp(jnp.where(mask[None], cum.T[:, :, None] - cum.T[:, None, :], -jnp.inf))
    return (rep(jnp.exp(cum)), rep(jnp.exp(last[None, :] - cum)), rep(jnp.exp(last)[None, :]), dec)


def _ret(u, cos, sin_signed, gain, n_seq, seq_len):
    n = u.shape[0]
    vec = _const_spec((1, GROUP_W))
    nt = seq_len // MIX_TILE
    ecum, eend, elast, dec = _ret_tables(CHUNK, CHUNK)
    dec = dec.transpose(1, 0, 2).reshape(CHUNK, N_HEADS * CHUNK)
    row = lambda wd: pl.BlockSpec((MIX_TILE, wd), lambda b, t: (b * nt + t, 0))
    pos = pl.BlockSpec((MIX_TILE, GROUP_W), lambda b, t: (t, 0))
    return pl.pallas_call(
        _ret_prompt_kernel,
        grid=(n_seq, nt),
        in_specs=[row(u.shape[1]), pos, pos, _const_spec(ecum.shape), _const_spec(eend.shape), vec,
                  _const_spec(dec.shape), vec],
        out_specs=[row(GROUP_W), pl.BlockSpec((1, N_HEADS, HEAD, HEAD), lambda b, t: (b, 0, 0, 0))],
        out_shape=[jax.ShapeDtypeStruct((n, GROUP_W), F32),
                   jax.ShapeDtypeStruct((n_seq, N_HEADS, HEAD, HEAD), F32)],
        scratch_shapes=[pltpu.VMEM((GROUP_W, GROUP_W), F32), pltpu.VMEM((MIX_TILE, GROUP_W), F32)],
        compiler_params=_params(("parallel", "arbitrary")),
        name="ret_prompt",
    )(u, cos, sin_signed, ecum, eend, elast, dec, gain)


def _rg_gates(xr, wa_ref, ba_ref, wx_ref, bx_ref, lam_ref):
    r_gate = _sigmoid(_bdot(xr, wa_ref[...]) + ba_ref[...])
    i_gate = _sigmoid(_bdot(xr, wx_ref[...]) + bx_ref[...])
    log_a = (-RG_C * _softplus(-lam_ref[...])) * r_gate
    a = jnp.exp(log_a)
    b = jnp.sqrt(-jnp.tanh(log_a) * (jnp.exp(2.0 * log_a) + 1.0)) * (i_gate * xr)
    return a, b


def _rg_scan(a, b, seg):
    pos = _iota(a.shape, 0) & (seg - 1)
    d = 1
    while d < seg:
        ok = pos >= d
        a_sh = jnp.where(ok, pltpu.roll(a, d, 0), 1.0)
        b_sh = jnp.where(ok, pltpu.roll(b, d, 0), 0.0)
        b = a * b_sh + b
        a = a * a_sh
        d *= 2
    return a, b


def _rg_scan_carry(a, b, h0):
    a, b = _rg_scan(a, b, 8)
    outs = []
    carry = h0
    for g in range(a.shape[0] // 8):
        h_g = b[g * 8:(g + 1) * 8] + a[g * 8:(g + 1) * 8] * carry
        outs.append(h_g)
        carry = h_g[7:8]
    return jnp.concatenate(outs, axis=0)


def _rg_prompt_kernel(u_ref, cw_ref, cb_ref, wa_ref, ba_ref, wx_ref, bx_ref, lam_ref,
                      y_ref, h_out_ref, buf_out_ref, h_ref, xx_ref):
    t = pl.program_id(1)
    rows = u_ref.shape[0]

    @pl.when(t == 0)
    def _():
        h_ref[...] = jnp.zeros(h_ref.shape, F32)

    u = u_ref[...]
    xr = _conv_taps_carry(u[:, 0:GROUP_W], xx_ref, cw_ref, cb_ref, t == 0)
    a, b = _rg_gates(xr, wa_ref, ba_ref, wx_ref, bx_ref, lam_ref)
    hseq = _rg_scan_carry(a, b, h_ref[0:1, :])
    h_ref[...] = jnp.broadcast_to(hseq[rows - 1:rows, :], h_ref.shape)
    y_ref[...] = _gelu_tanh(u[:, GROUP_W:]) * hseq

    @pl.when(t == pl.num_programs(1) - 1)
    def _():
        h_out_ref[0] = hseq[rows - 1:rows, :]
        buf_out_ref[0] = xx_ref[pl.ds(8 - (CONV_W - 1), CONV_W - 1), :]


def _rg_sample_kernel(u_ref, h0_ref, buf_ref, cw_ref, cb_ref, wa_ref, ba_ref, wx_ref, bx_ref, lam_ref,
                      y_ref, h_out_ref, buf_out_ref):
    u = u_ref[...]
    xr, new_buf = _conv_taps_seq4(u[:, 0:GROUP_W], buf_ref[...], cw_ref, cb_ref)
    buf_out_ref[...] = new_buf
    a, b = _rg_gates(xr, wa_ref, ba_ref, wx_ref, bx_ref, lam_ref)
    a_cum, h_loc = _rg_scan(a, b, 4)
    hseq = h_loc + a_cum * h0_ref[...]
    h_out_ref[...] = hseq
    y_ref[...] = _gelu_tanh(u[:, GROUP_W:]) * hseq


def _rg(u, h0_rows, buf, cw, cb, wa, ba, wx, bx, lam, n_seq, seq_len):
    n = u.shape[0]
    vec = _const_spec((1, GROUP_W))
    common_in = [_const_spec(cw.shape), vec, _const_spec(wa.shape), vec, _const_spec(wx.shape), vec, vec]
    if h0_rows is None:
        nt = seq_len // RG_TILE
        row = lambda wd: pl.BlockSpec((RG_TILE, wd), lambda b, t: (b * nt + t, 0))
        return pl.pallas_call(
            _rg_prompt_kernel,
            grid=(n_seq, nt),
            in_specs=[row(u.shape[1])] + common_in,
            out_specs=[row(GROUP_W), pl.BlockSpec((1, 1, GROUP_W), lambda b, t: (b, 0, 0)),
                       pl.BlockSpec((1, CONV_W - 1, GROUP_W), lambda b, t: (b, 0, 0))],
            out_shape=[jax.ShapeDtypeStruct((n, GROUP_W), F32),
                       jax.ShapeDtypeStruct((n_seq, 1, GROUP_W), F32),
                       jax.ShapeDtypeStruct((n_seq, CONV_W - 1, GROUP_W), F32)],
            scratch_shapes=[pltpu.VMEM((8, GROUP_W), F32), pltpu.VMEM((RG_TILE + 8, GROUP_W), F32)],
            compiler_params=_params(("parallel", "arbitrary")),
            name="rglru_prompt",
        )(u, cw, cb, wa, ba, wx, bx, lam)
    rows = SAMPLE_SEQS * seq_len
    row = lambda wd: pl.BlockSpec((rows, wd), lambda i: (i, 0))
    return pl.pallas_call(
        _rg_sample_kernel,
        grid=(n_seq // SAMPLE_SEQS,),
        in_specs=[row(u.shape[1]), row(GROUP_W), row(GROUP_W)] + common_in,
        out_specs=[row(GROUP_W)] * 3,
        out_shape=[jax.ShapeDtypeStruct((n, GROUP_W), F32)] * 3,
        compiler_params=_params(("parallel",)),
        name="rglru_sample",
    )(u, h0_rows, buf, cw, cb, wa, ba, wx, bx, lam)


def _hg_inputs(u_ref, lb_ref, seg_shift):
    u = u_ref[...]
    rows = u.shape[0]
    lb = lb_ref[...]
    fg = lb + (1.0 - lb) * _sigmoid(u[:, GROUP_W:2 * GROUP_W])
    q = _silu(u[:, 0:GROUP_W])
    k = 1.0 - fg
    v = u[:, 2 * GROUP_W:3 * GROUP_W]
    cum = _seg_cumsum(jnp.log(fg), seg_shift)
    return q, k, v, cum, u[:, 3 * GROUP_W:4 * GROUP_W]


def _hg_prompt_kernel(u_ref, lb_ref, g_ref, y_ref, s_out_ref, st_ref, y_scr):
    t = pl.program_id(1)
    rows = u_ref.shape[0]

    @pl.when(t == 0)
    def _():
        st_ref[...] = jnp.zeros(st_ref.shape, F32)

    q, k, v, cum, gate = _hg_inputs(u_ref, lb_ref, 6)
    hmask = _head_block_mask()
    hmask01 = hmask.astype(BF16)
    ones_bd = hmask01
    y_scr[...] = jnp.dot((q * k).astype(BF16), ones_bd, preferred_element_type=F32) * v
    row = _iota((rows, GROUP_W), 0)
    trow = _iota((CHUNK, GROUP_W), 0)
    tcol = _iota((CHUNK, GROUP_W), 1) & (CHUNK - 1)
    cum = cum * LOG2_E
    levels = []
    blk_last = cum
    h = 1
    while h < CHUNK:
        upper = (row & h) != 0
        qn = jnp.where(upper, q * jnp.exp2(cum - pltpu.roll(blk_last, h, 0)), 0.0)
        kn = k * jnp.exp2(blk_last - cum)
        sh = (2 * h).bit_length() - 1
        pair = ((trow >> sh) == (tcol >> sh)) & ((tcol & h) == 0)
        levels.append((pair, qn, kn))
        blk_last = jnp.where(upper, blk_last, pltpu.roll(blk_last, rows - h, 0))
        h *= 2
    q_in = q * jnp.exp2(cum)
    k_end = k * jnp.exp2(blk_last - cum)
    n_chunks = rows // CHUNK
    a_wide, ds_t = [], []
    for c in range(n_chunks):
        rs = slice(c * CHUNK, (c + 1) * CHUNK)
        a_c = jnp.zeros((CHUNK, GROUP_W), F32)
        for pair, qn, kn in levels:
            g = _bdot_nt(qn[rs], _block_diag_rows(kn[rs], hmask01))
            a_c = a_c + jnp.where(pair, g, 0.0)
        a_wide.append(a_c)
        ds_t.append(jnp.where(hmask, _bdot_tn(v[rs], k_end[rs]), 0.0))
    st = st_ref[...]
    st_in = []
    for c in range(n_chunks):
        st_in.append(st.astype(BF16))
        st = jnp.exp2(blk_last[c * CHUNK:c * CHUNK + 1, :]) * st + ds_t[c]
    st_ref[...] = st
    for c in range(n_chunks):
        rs = slice(c * CHUNK, (c + 1) * CHUNK)
        y_c = _bdot(a_wide[c], _block_diag_rows(v[rs], hmask01)) + _bdot_nt(q_in[rs], st_in[c])
        y_scr[rs, :] = y_scr[rs, :] + y_c
    y_ref[...] = _silu(gate) * (_head_stat(y_scr[...], False) * g_ref[...])

    @pl.when(t == pl.num_programs(1) - 1)
    def _():
        eye = (_iota((HEAD, HEAD), 0) == _iota((HEAD, HEAD), 1)).astype(BF16)
        for h in range(N_HEADS):
            s_out_ref[0, h] = _sel_dot_nt(eye, st_ref[h * HEAD:(h + 1) * HEAD, h * HEAD:(h + 1) * HEAD])


def _hg(u, lb, gain, n_seq, seq_len):
    n = u.shape[0]
    vec = _const_spec((1, GROUP_W))
    nt = seq_len // MIX_TILE
    row = lambda wd: pl.BlockSpec((MIX_TILE, wd), lambda b, t: (b * nt + t, 0))
    return pl.pallas_call(
        _hg_prompt_kernel,
        grid=(n_seq, nt),
        in_specs=[row(u.shape[1]), vec, vec],
        out_specs=[row(GROUP_W), pl.BlockSpec((1, N_HEADS, HEAD, HEAD), lambda b, t: (b, 0, 0, 0))],
        out_shape=[jax.ShapeDtypeStruct((n, GROUP_W), F32),
                   jax.ShapeDtypeStruct((n_seq, N_HEADS, HEAD, HEAD), F32)],
        scratch_shapes=[pltpu.VMEM((GROUP_W, GROUP_W), F32), pltpu.VMEM((MIX_TILE, GROUP_W), F32)],
        compiler_params=_params(("parallel", "arbitrary")),
        name="hgrn_prompt",
    )(u, lb, gain)


def _t_recurrence(q_scr, k_scr, v, dec_row, dec_scr, s_ref, n_seq, seq_len):
    cols = [slice(t * n_seq, (t + 1) * n_seq) for t in range(seq_len)]
    v_t = [v[:, c] for c in cols]
    outs = [jnp.zeros((HEAD, n_seq), F32) for _ in cols]
    for k in range(HEAD):
        s = s_ref[k]
        for t, c in enumerate(cols):
            dec = dec_row if dec_scr is None else dec_scr[k:k + 1, c]
            s = dec * s + k_scr[k:k + 1, c] * v_t[t]
            outs[t] = outs[t] + q_scr[k:k + 1, c] * s
        s_ref[k] = s
    return jnp.concatenate(outs, axis=1)


def _t_head_norm(o, center):
    if center:
        o = o - jnp.mean(o, axis=0, keepdims=True)
    return o * lax.rsqrt(jnp.mean(o * o, axis=0, keepdims=True) + EPS)


def _ret_t_kernel(q_ref, k_ref, v_ref, g_ref, cos_ref, sin_ref, gam_ref, gain_ref, s0_ref,
                  y_ref, s_ref, q_scr, k_scr, *, n_seq, seq_len):
    half = HEAD // 2

    def rope(x):
        swapped = jnp.concatenate([x[half:], x[:half]], axis=0)
        return x * cos_ref[...] + swapped * sin_ref[...]

    q_scr[...] = rope(q_ref[...])
    k_scr[...] = rope(k_ref[...]) * (HEAD ** -0.5)
    s_ref[0] = s0_ref[0]
    o = _t_recurrence(q_scr, k_scr, v_ref[...], gam_ref[0], None, s_ref.at[0], n_seq, seq_len)
    gain = jnp.concatenate([gain_ref[...]] * seq_len, axis=1)
    parts = [_t_head_norm(o[:, t * n_seq:(t + 1) * n_seq], True) for t in range(seq_len)]
    y_ref[...] = _silu(g_ref[...]) * (jnp.concatenate(parts, axis=1) * gain)


def _hg_t_kernel(q_ref, f_ref, i_ref, g_ref, lb_ref, gain_ref, s0_ref, y_ref, s_ref, q_scr, k_scr, f_scr,
                 *, n_seq, seq_len):
    lb = jnp.concatenate([lb_ref[...]] * seq_len, axis=1)
    fg = lb + (1.0 - lb) * _sigmoid(f_ref[...])
    q_scr[...] = _silu(q_ref[...])
    k_scr[...] = 1.0 - fg
    f_scr[...] = fg
    s_ref[0] = s0_ref[0]
    o = _t_recurrence(q_scr, k_scr, i_ref[...], None, f_scr, s_ref.at[0], n_seq, seq_len)
    gain = jnp.concatenate([gain_ref[...]] * seq_len, axis=1)
    parts = [_t_head_norm(o[:, t * n_seq:(t + 1) * n_seq], False) for t in range(seq_len)]
    y_ref[...] = _silu(g_ref[...]) * (jnp.concatenate(parts, axis=1) * gain)


def _t_specs(n, n_seq):
    head_rows = lambda section: pl.BlockSpec((HEAD, n), lambda h, s=section: (N_HEADS * s + h, 0))
    per_head = pl.BlockSpec((HEAD, n_seq), lambda h: (h, 0))
    state_in = lambda layer: pl.BlockSpec((None, 1, HEAD, HEAD, n_seq), lambda h: (layer, h, 0, 0, 0))
    state_out = pl.BlockSpec((1, HEAD, HEAD, n_seq), lambda h: (h, 0, 0, 0))
    return head_rows, per_head, state_in, state_out


def _ret_t(u_t, cos_t, sin_t, gam, gain_b, state_t, n_seq, seq_len, layer):
    n = u_t.shape[1]
    head_rows, per_head, state_in, state_out = _t_specs(n, n_seq)
    return pl.pallas_call(
        functools.partial(_ret_t_kernel, n_seq=n_seq, seq_len=seq_len),
        grid=(N_HEADS,),
        in_specs=[head_rows(0), head_rows(1), head_rows(2), head_rows(3), _const_spec(cos_t.shape),
                  _const_spec(sin_t.shape), pl.BlockSpec((1, 1, n_seq), lambda h: (h, 0, 0)), per_head,
                  state_in(layer)],
        out_specs=[pl.BlockSpec((HEAD, n), lambda h: (h, 0)), state_out],
        out_shape=[jax.ShapeDtypeStruct((GROUP_W, n), F32), jax.ShapeDtypeStruct(state_t.shape[1:], F32)],
        scratch_shapes=[pltpu.VMEM((HEAD, n), F32)] * 2,
        compiler_params=_params(("parallel",)),
        name="ret_sample_t",
    )(u_t, u_t, u_t, u_t, cos_t, sin_t, gam, gain_b, state_t)


def _hg_t(u_t, lb_b, gain_b, state_t, n_seq, seq_len, layer):
    n = u_t.shape[1]
    head_rows, per_head, state_in, state_out = _t_specs(n, n_seq)
    return pl.pallas_call(
        functools.partial(_hg_t_kernel, n_seq=n_seq, seq_len=seq_len),
        grid=(N_HEADS,),
        in_specs=[head_rows(0), head_rows(1), head_rows(2), head_rows(3), per_head, per_head, state_in(layer)],
        out_specs=[pl.BlockSpec((HEAD, n), lambda h: (h, 0)), state_out],
        out_shape=[jax.ShapeDtypeStruct((GROUP_W, n), F32), jax.ShapeDtypeStruct(state_t.shape[1:], F32)],
        scratch_shapes=[pltpu.VMEM((HEAD, n), F32)] * 3,
        compiler_params=_params(("parallel",)),
        name="hgrn_sample_t",
    )(u_t, u_t, u_t, u_t, lb_b, gain_b, state_t)


def _rope_tables_t(pos0, seq_len, n_seq):
    half = HEAD // 2
    pos = pos0 + jnp.arange(seq_len, dtype=F32)
    inv = ROPE_BASE ** (-jnp.arange(half, dtype=F32) / half)
    ang = pos[:, None] * inv
    cos = jnp.tile(jnp.cos(ang).T, (2, 1))
    sin = jnp.sin(ang).T
    sin_signed = jnp.concatenate([-sin, sin], axis=0)
    return jnp.repeat(cos, n_seq, axis=1), jnp.repeat(sin_signed, n_seq, axis=1)


def _block_diag(w):
    h, i, j = w.shape
    eye = jnp.eye(h, dtype=w.dtype)
    return (eye[:, None, :, None] * w[:, :, None, :]).reshape(h * i, h * j)


def _rope_tables(pos0, seq_len):
    half = HEAD // 2
    pos = pos0 + jnp.arange(seq_len, dtype=F32)
    inv = ROPE_BASE ** (-jnp.arange(half, dtype=F32) / half)
    ang = pos[:, None] * inv
    cos = jnp.tile(jnp.cos(ang), (1, 2 * N_HEADS))
    sin = jnp.sin(ang)
    sin_signed = jnp.concatenate([jnp.tile(-sin, (1, N_HEADS)), jnp.tile(sin, (1, N_HEADS))], axis=-1)
    return cos, sin_signed


def _layer_weights(P, l):
    row = lambda a: a.reshape(1, -1)
    pad128 = lambda a: jnp.pad(a, (0, 128 - a.shape[0])).reshape(1, 128)
    lb_sm = jax.nn.softmax(P['hg_lb'].astype(F32), axis=0)
    lb = (jnp.cumsum(lb_sm, axis=0) - lb_sm[0])[l]
    return dict(
        w_in=P['w_in_prepped'], w_in_t=P['w_in_t'],
        ln_mix_pre=row(P['ln_mix_pre'][l]), ln_mix_post=row(P['ln_mix_post'][l]),
        ln_xa_pre=row(P['ln_xa_pre'][l]), ln_xa_post=row(P['ln_xa_post'][l]),
        ln_ffn_pre=row(P['ln_ffn_pre'][l]), ln_ffn_post=row(P['ln_ffn_post'][l]),
        ssd_cw=P['ssd_conv_w'][l], ssd_cb=row(P['ssd_conv_b'][l]),
        ssd_dtb=pad128(P['ssd_dt_bias'][l]), ssd_alog=pad128(P['ssd_A_log'][l]),
        ssd_d=row(jnp.repeat(P['ssd_D'][l], HEAD)), ssd_norm=row(P['ssd_norm'][l]),
        ret_norm=row(P['ret_norm'][l]),
        rg_cw=P['rg_conv_w'][l], rg_cb=row(P['rg_conv_b'][l]),
        rg_wa=_block_diag(P['rg_wa'][l]).astype(BF16), rg_ba=row(P['rg_ba'][l]),
        rg_wx=_block_diag(P['rg_wx'][l]).astype(BF16), rg_bx=row(P['rg_bx'][l]),
        rg_lam=row(P['rg_lambda'][l]),
        hg_lb=row(lb), hg_norm=row(P['hg_norm'][l]),
        w_out=P['w_out_bf16'], w_xq=P['w_xq_bf16'], w_xo=P['w_xo_bf16'], w_gu=P['w_gu_bf16'],
        w_down=P['w_down_bf16'], ln_mem=row(P['ln_mem'][l]), w_xkv=P['w_xkv_bf16'],
    )


def _layer(x, W, k_mem, v_mem, st, n_seq, seq_len, pos0, layer):
    if st is None:
        u_ssd, u_ret, u_rg, u_hg, u_dt = _norm_matmul(
            x, W['ln_mix_pre'], W['w_in'], (IN_COLS_SSD, IN_COLS_RET_SPLIT, IN_COLS_RG, IN_COLS_HG, IN_COLS_DT),
            "in_proj", layer=layer)
        cos, sin_signed = _rope_tables(pos0, seq_len)
        y_ssd, s_ssd, b_ssd = _ssd(u_ssd, u_dt, None, None, W['ssd_cw'], W['ssd_cb'], W['ssd_dtb'],
                                   W['ssd_alog'], W['ssd_d'], W['ssd_norm'], n_seq, seq_len, layer)
        y_ret, s_ret = _ret(u_ret, cos, sin_signed, W['ret_norm'], n_seq, seq_len)
        y_rg, h_rg, b_rg = _rg(u_rg, None, None, W['rg_cw'], W['rg_cb'], W['rg_wa'], W['rg_ba'],
                               W['rg_wx'], W['rg_bx'], W['rg_lam'], n_seq, seq_len)
        h_rg = h_rg.reshape(n_seq, GROUP_W)
        y_hg, s_hg = _hg(u_hg, W['hg_lb'], W['hg_norm'], n_seq, seq_len)
    else:
        ssd_s, ssd_buf, ret_s, rg_h, rg_buf, hg_s = st
        u_ssd, u_rg, u_dt = _norm_matmul(x, W['ln_mix_pre'], W['w_in'], (IN_COLS_SSD, IN_COLS_RG, IN_COLS_DT),
                                         "in_proj", layer=layer)
        to_tb = lambda a: a.reshape(n_seq, seq_len, -1).transpose(1, 0, 2).reshape(n_seq * seq_len, -1)
        from_t = lambda a: a.reshape(-1, seq_len, n_seq).transpose(2, 1, 0).reshape(n_seq * seq_len, -1)
        bcast = lambda g: jnp.broadcast_to(g.reshape(-1, 1), (g.size, n_seq))
        ut_ret, ut_hg = _in_proj_t(to_tb(x), W['ln_mix_pre'], W['w_in_t'], layer)
        cos_t, sin_t = _rope_tables_t(pos0, seq_len, n_seq)
        log_gamma = jnp.log(1.0 - jnp.exp2(-5.0 - jnp.arange(N_HEADS, dtype=F32)))
        gam = jnp.broadcast_to(jnp.exp(log_gamma)[:, None, None], (N_HEADS, 1, n_seq))
        yt_ret, s_ret = _ret_t(ut_ret, cos_t, sin_t, gam, bcast(W['ret_norm']), ret_s, n_seq, seq_len, layer)
        yt_hg, s_hg = _hg_t(ut_hg, bcast(W['hg_lb']), bcast(W['hg_norm']), hg_s, n_seq, seq_len, layer)
        y_ret, y_hg = from_t(yt_ret), from_t(yt_hg)
        pad_rows = lambda b: jnp.pad(b, ((0, 0), (0, 1), (0, 0))).reshape(n_seq * 4, b.shape[-1])
        y_ssd, s_ssd, b_ssd = _ssd(u_ssd, u_dt, ssd_s, pad_rows(ssd_buf), W['ssd_cw'], W['ssd_cb'],
                                   W['ssd_dtb'], W['ssd_alog'], W['ssd_d'], W['ssd_norm'], n_seq, seq_len, layer)
        b_ssd = b_ssd.reshape(n_seq, 4, -1)[:, :CONV_W - 1]
        y_rg, h_rows, b_rg = _rg(u_rg, jnp.repeat(rg_h, seq_len, axis=0), pad_rows(rg_buf), W['rg_cw'],
                                 W['rg_cb'], W['rg_wa'], W['rg_ba'], W['rg_wx'], W['rg_bx'], W['rg_lam'],
                                 n_seq, seq_len)
        h_rg = h_rows.reshape(n_seq, seq_len, GROUP_W)[:, seq_len - 1]
        b_rg = b_rg.reshape(n_seq, 4, -1)[:, :CONV_W - 1]
    ys = (y_ssd, y_ret, y_rg, y_hg)
    if st is None:
        x3 = _xattn_ffn(ys, x, W['w_out'], W['ln_mix_post'], k_mem, v_mem, W['ln_xa_pre'], W['w_xq'], W['w_xo'],
                        W['ln_xa_post'], W['ln_ffn_pre'], W['w_gu'], W['w_down'], W['ln_ffn_post'], seq_len,
                        layer)
    else:
        x1, q = _out_proj(ys, x, W['w_out'], W['ln_mix_post'], W['ln_xa_pre'], W['w_xq'], layer)
        o = _attention_sample(q, k_mem, v_mem, layer, seq_len)
        x3 = _ffn(o, x1, W['w_xo'], W['ln_xa_post'], W['ln_ffn_pre'], W['w_gu'], W['w_down'],
                  W['ln_ffn_post'], layer)
    return x3, (s_ssd, b_ssd, s_ret, h_rg, b_rg, s_hg)


def kernel(x_prompt, x_sample, state_ssd, state_ssd_conv, state_ret, state_rglru, state_rglru_conv, state_hgrn, cache_mem_k, cache_mem_v, mem_prompt, ln_mix_pre, ln_mix_post, ln_xa_pre, ln_xa_post, ln_ffn_pre, ln_ffn_post, w_in, ssd_conv_w, ssd_conv_b, ssd_dt_bias, ssd_A_log, ssd_D, ssd_norm, ret_norm, rg_conv_w, rg_conv_b, rg_wa, rg_ba, rg_wx, rg_bx, rg_lambda, hg_lb, hg_norm, w_out, ln_mem, w_xq, w_xkv, w_xo, w_gu, w_down):
    P = dict(ln_mix_pre=ln_mix_pre, ln_mix_post=ln_mix_post, ln_xa_pre=ln_xa_pre, ln_xa_post=ln_xa_post,
             ln_ffn_pre=ln_ffn_pre, ln_ffn_post=ln_ffn_post, w_in=w_in, ssd_conv_w=ssd_conv_w,
             ssd_conv_b=ssd_conv_b, ssd_dt_bias=ssd_dt_bias, ssd_A_log=ssd_A_log, ssd_D=ssd_D,
             ssd_norm=ssd_norm, ret_norm=ret_norm, rg_conv_w=rg_conv_w, rg_conv_b=rg_conv_b,
             rg_wa=rg_wa, rg_ba=rg_ba, rg_wx=rg_wx, rg_bx=rg_bx, rg_lambda=rg_lambda, hg_lb=hg_lb,
             hg_norm=hg_norm, w_out=w_out, ln_mem=ln_mem, w_xq=w_xq, w_xkv=w_xkv, w_xo=w_xo,
             w_gu=w_gu, w_down=w_down)
    P['w_in_prepped'], P['w_in_t'] = _in_weight_prep(w_in)
    for name in ('w_out', 'w_xq', 'w_xo', 'w_gu', 'w_down', 'w_xkv'):
        P[name + '_bf16'] = P[name].astype(BF16)
    ssd_t = jnp.swapaxes(state_ssd, -1, -2)
    ret_t = jnp.transpose(state_ret, (0, 2, 3, 4, 1))
    hgrn_t = jnp.transpose(state_hgrn, (0, 2, 3, 4, 1))
    depth = w_in.shape[0]
    bp, tp, d = x_prompt.shape
    bs, ts, _ = x_sample.shape
    n_mem = mem_prompt.shape[1]
    y_p = x_prompt.reshape(bp * tp, d)
    y_s = x_sample.reshape(bs * ts, d)
    mem = mem_prompt.reshape(bp * n_mem, d)
    p_st, s_st, p_mk, p_mv = [], [], [], []
    for l in range(depth):
        W = _layer_weights(P, l)
        mk, mv, mk5, mv5 = _mem_kv(mem, W['ln_mem'], W['w_xkv'], l, n_mem)
        mk = mk.reshape(bp, n_mem, d)
        mv = mv.reshape(bp, n_mem, d)
        y_p, st_p = _layer(y_p, W, mk, mv, None, bp, tp, 0.0, l)
        p_st.append(st_p)
        p_mk.append(mk5)
        p_mv.append(mv5)
        st_l = (ssd_t, state_ssd_conv[l], ret_t, state_rglru[l], state_rglru_conv[l], hgrn_t)
        y_s, st_s = _layer(y_s, W, cache_mem_k, cache_mem_v, st_l, bs, ts, float(PAST_LEN), l)
        s_st.append(st_s)
    stack = lambda sts, i: jnp.stack([s[i] for s in sts], axis=0)
    return (y_p.reshape(bp, tp, d), y_s.reshape(bs, ts, d),
            jnp.swapaxes(stack(p_st, 0), -1, -2), stack(p_st, 1), stack(p_st, 2), stack(p_st, 3), stack(p_st, 4),
            stack(p_st, 5),
            jnp.stack(p_mk, axis=0), jnp.stack(p_mv, axis=0),
            jnp.swapaxes(stack(s_st, 0), -1, -2), stack(s_st, 1), jnp.transpose(stack(s_st, 2), (0, 4, 1, 2, 3)),
            stack(s_st, 3), stack(s_st, 4), jnp.transpose(stack(s_st, 5), (0, 4, 1, 2, 3)))
```

```python
import functools
import math

import jax
import jax.numpy as jnp
from jax import lax
from jax.experimental import pallas as pl
from jax.experimental.pallas import tpu as pltpu

F32 = jnp.float32
BF16 = jnp.bfloat16
EPS = 1e-6
LOG2_E = 1.4426950408889634

GROUP_W = 256
HEAD = 64
N_HEADS = 4
SSD_N = 128
SSD_G = 2
CONV_W = 4
XA_H = 4
XA_HD = 256
ROPE_BASE = 10000.0
RG_C = 8.0
CHUNK = 64
PAST_LEN = 16384

ROW_TILE = 512
MIX_TILE = 1024
RG_TILE = 256
SAMPLE_SEQS = 16
ATT_SEQS = 8
FF_TILE = 256
VMEM_LIMIT = 56 * 1024 * 1024


def _bdot(a, b):
    return jnp.dot(a.astype(BF16), b.astype(BF16), preferred_element_type=F32)


def _bdot_nt(a, b):
    return lax.dot_general(a.astype(BF16), b.astype(BF16), (((1,), (1,)), ((), ())),
                           preferred_element_type=F32)


def _bdot_tn(a, b):
    return lax.dot_general(a.astype(BF16), b.astype(BF16), (((0,), (0,)), ((), ())),
                           preferred_element_type=F32)


def _split3(x):
    hi = x.astype(BF16)
    r = x - hi.astype(F32)
    mid = r.astype(BF16)
    lo = (r - mid.astype(F32)).astype(BF16)
    return hi, mid, lo


def _sel_dot(sel, x):
    hi, mid, lo = _split3(x)
    d = lambda y: jnp.dot(sel, y, preferred_element_type=F32)
    return (d(hi) + d(mid)) + d(lo)


def _dot_sel(x, sel):
    hi, mid, lo = _split3(x)
    d = lambda y: jnp.dot(y, sel, preferred_element_type=F32)
    return (d(hi) + d(mid)) + d(lo)


def _sel_dot_nt(sel, x):
    hi, mid, lo = _split3(x)
    d = lambda y: lax.dot_general(sel, y, (((1,), (1,)), ((), ())), preferred_element_type=F32)
    return (d(hi) + d(mid)) + d(lo)


def _rms(x, g):
    return x * lax.rsqrt(jnp.mean(x * x, axis=-1, keepdims=True) + EPS) * g


def _sigmoid(x):
    return jax.nn.sigmoid(x)


def _silu(x):
    return x * jax.nn.sigmoid(x)


def _softplus(x):
    return jnp.maximum(x, 0.0) + jnp.log1p(jnp.exp(-jnp.abs(x)))


def _gelu_tanh(x):
    c = math.sqrt(2.0 / math.pi)
    return 0.5 * x * (1.0 + jnp.tanh(c * (x + 0.044715 * (x * x * x))))


def _iota(shape, dim):
    return lax.broadcasted_iota(jnp.int32, shape, dim)


def _seg_causal_mask(n, seg_shift):
    r = _iota((n, n), 0)
    c = _iota((n, n), 1)
    return ((r >> seg_shift) == (c >> seg_shift)) & (c <= r)


def _seg_cumsum(x, seg_shift):
    rows = x.shape[0]
    blk = min(rows, 256)
    mask = _seg_causal_mask(blk, seg_shift).astype(BF16)
    return jnp.concatenate([_sel_dot(mask, x[i:i + blk]) for i in range(0, rows, blk)], axis=0)


def _head_expand_mat(width_in=128):
    r = _iota((width_in, GROUP_W), 0)
    c = _iota((width_in, GROUP_W), 1)
    return ((c >> 6) == r).astype(BF16)


def _head_block_mask():
    return (_iota((GROUP_W, GROUP_W), 0) >> 6) == (_iota((GROUP_W, GROUP_W), 1) >> 6)


def _block_diag_rows(x, mask01):
    return jnp.concatenate([x.astype(BF16)] * N_HEADS, axis=0) * mask01


def _head_stat(y, center):
    ones_bd = _head_block_mask().astype(BF16)
    if center:
        y = y - _dot_sel(y, ones_bd) * (1.0 / HEAD)
    return y * lax.rsqrt(_dot_sel(y * y, ones_bd) * (1.0 / HEAD) + EPS)


def _conv_taps_carry(x, xx_ref, w_ref, b_ref, first):
    rows = x.shape[0]

    @pl.when(first)
    def _():
        xx_ref[0:8, :] = jnp.zeros((8, x.shape[1]), F32)

    xx_ref[8:8 + rows, :] = x
    y = b_ref[...] + w_ref[CONV_W - 1:CONV_W, :] * x
    for j in range(CONV_W - 1):
        y = y + w_ref[j:j + 1, :] * xx_ref[pl.ds(8 - (CONV_W - 1) + j, rows), :]
    xx_ref[0:8, :] = xx_ref[rows:rows + 8, :]
    return y


def _conv_taps_seq4(x, buf4, w_ref, b_ref):
    rows = x.shape[0]
    r = _iota((rows, rows), 0)
    c = _iota((rows, rows), 1)
    t = r & 3
    y = b_ref[...] + w_ref[CONV_W - 1:CONV_W, :] * x
    for d in range(1, CONV_W):
        shift = ((c == r - d) & (t >= d)).astype(BF16)
        hist = ((t < d) & (c == (r - t) + (3 + t - d))).astype(BF16)
        y = y + w_ref[CONV_W - 1 - d:CONV_W - d, :] * (_sel_dot(shift, x) + _sel_dot(hist, buf4))
    new_sel = (((r & 3) < 3) & (c == r + 1)).astype(BF16)
    return y, _sel_dot(new_sel, x)


def _norm_matmul_kernel(x_ref, g_ref, w_ref, *o_refs, col_ranges):
    h = _rms(x_ref[...], g_ref[...]).astype(BF16)
    for o_ref, pieces in zip(o_refs, col_ranges):
        off = 0
        for a, b in pieces:
            o_ref[:, off:off + b - a] = jnp.dot(h, w_ref[:, a:b], preferred_element_type=F32)
            off += b - a


def _const_spec(shape):
    nd = len(shape)
    return pl.BlockSpec(shape, lambda *_: (0,) * nd, pipeline_mode=pl.Buffered(1))


def _layer_spec(shape, layer):
    nd = len(shape)
    return pl.BlockSpec((None,) + tuple(shape[1:]), lambda *_: (layer,) + (0,) * (nd - 1),
                        pipeline_mode=pl.Buffered(1))


def _stack_call(kernel, prevs, stacked, layer, *, in_specs, args, out_specs, out_shape, **kw):
    n_in = len(in_specs)
    out_specs = list(out_specs)
    order = sorted(stacked)
    for o in order:
        blk, idx = stacked[o]
        if prevs is None:
            depth = out_shape[o].shape[0]
            out_specs[o] = pl.BlockSpec((depth,) + tuple(blk), lambda *g, idx=idx: (0,) + tuple(idx(*g)))
        else:
            out_specs[o] = pl.BlockSpec((None,) + tuple(blk), lambda *g, idx=idx: (layer,) + tuple(idx(*g)))
    if prevs is None:
        def body(*refs):
            refs = list(refs)
            for o in order:
                full = refs[n_in + o]
                for d in range(full.shape[0]):
                    if d != layer:
                        full[d] = jnp.zeros(full.shape[1:], full.dtype)
                refs[n_in + o] = full.at[layer]
            return kernel(*refs)

        return pl.pallas_call(body, in_specs=in_specs, out_specs=out_specs, out_shape=out_shape, **kw)(*args)

    def body(*refs):
        return kernel(*refs[:n_in], *refs[n_in + len(prevs):])

    return pl.pallas_call(
        body,
        in_specs=list(in_specs) + [pl.BlockSpec(memory_space=pl.ANY)] * len(prevs),
        out_specs=out_specs, out_shape=out_shape,
        input_output_aliases={n_in + j: o for j, o in enumerate(order)},
        **kw)(*args, *prevs)


def _params(sem):
    return pltpu.CompilerParams(dimension_semantics=sem, vmem_limit_bytes=VMEM_LIMIT)


def _in_weight_prep_kernel(wt_ref, o_ref, ot_ref, *, blocks, dt_block, dt_row, dt_n, t_rows):
    off = 0
    for src, n in t_rows:
        ot_ref[off:off + n, :] = wt_ref[pl.ds(src, n), :].astype(BF16)
        off += n
    for j, pieces in enumerate(blocks):
        if j == dt_block:
            dt = wt_ref[pl.ds(dt_row, 8), :]
            dt = jnp.where(_iota(dt.shape, 0) < dt_n, dt, 0.0)
            blk = jnp.concatenate([dt, jnp.zeros((120, dt.shape[1]), F32)], axis=0)
        else:
            blk = jnp.concatenate([wt_ref[pl.ds(src, n), :] for src, n in pieces], axis=0)
        o_ref[:, j * 128:(j + 1) * 128] = blk.T.astype(BF16)


IN_COLS_SSD = ((0, 1024),)
IN_COLS_RG = ((2048, 2560),)
IN_COLS_HG = ((2560, 3584),)
IN_COLS_DT = ((3584, 3712),)
IN_COLS_RET_SPLIT = ((3712, 4224), (1536, 2048))


def _in_weight_prep(w_in):
    depth, d, d_in = w_in.shape
    z0 = GROUP_W + (GROUP_W + 2 * SSD_G * SSD_N)
    d0 = z0 + N_HEADS
    blocks = [((src, 128),) for src in tuple(range(0, z0, 128)) + tuple(range(d0, d_in, 128))]
    dt_block = len(blocks)
    blocks.append(())
    half = HEAD // 2
    for base in (d0, d0 + GROUP_W):
        for j in range(2):
            blocks.append(tuple((base + h * HEAD + j * half, half) for h in range(N_HEADS)))
    n_out = len(blocks) * 128
    t_rows = ((d0, 4 * GROUP_W), (d0 + 6 * GROUP_W, 4 * GROUP_W))
    n_t = sum(n for _, n in t_rows)
    wt = jnp.swapaxes(w_in, 1, 2)
    return pl.pallas_call(
        functools.partial(_in_weight_prep_kernel, blocks=tuple(blocks), dt_block=dt_block, dt_row=z0,
                          dt_n=N_HEADS, t_rows=t_rows),
        grid=(depth,),
        in_specs=[pl.BlockSpec((None, d_in, d), lambda l: (l, 0, 0), pipeline_mode=pl.Buffered(1))],
        out_specs=[pl.BlockSpec((None, d, n_out), lambda l: (l, 0, 0)),
                   pl.BlockSpec((None, n_t, d), lambda l: (l, 0, 0))],
        out_shape=[jax.ShapeDtypeStruct((depth, d, n_out), BF16), jax.ShapeDtypeStruct((depth, n_t, d), BF16)],
        compiler_params=_params(("parallel",)),
        name="in_weight_prep",
    )(wt)


def _in_proj_t_kernel(x_ref, g_ref, wt_ref, *o_refs):
    h = _rms(x_ref[...], g_ref[...]).astype(BF16)
    off = 0
    for o_ref in o_refs:
        f = o_ref.shape[0]
        o_ref[...] = lax.dot_general(wt_ref[off:off + f, :], h, (((1,), (1,)), ((), ())),
                                     preferred_element_type=F32)
        off += f


def _in_proj_t(x, g, wt, layer):
    n, d = x.shape
    f = wt.shape[1] // 2
    return pl.pallas_call(
        _in_proj_t_kernel,
        grid=(1,),
        in_specs=[_const_spec((n, d)), _const_spec((1, d)), _layer_spec(wt.shape, layer)],
        out_specs=[_const_spec((f, n))] * 2,
        out_shape=[jax.ShapeDtypeStruct((f, n), F32)] * 2,
        compiler_params=_params(("arbitrary",)),
        name="in_proj_t",
    )(x, g, wt)


def _mem_kv_kernel(x_ref, g_ref, w_ref, k_ref, v_ref, k5_ref, v5_ref):
    h = _rms(x_ref[...], g_ref[...]).astype(BF16)
    d = x_ref.shape[1]
    k = jnp.dot(h, w_ref[:, 0:d], preferred_element_type=F32)
    v = jnp.dot(h, w_ref[:, d:2 * d], preferred_element_type=F32)
    k_ref[...] = k
    v_ref[...] = v
    k5_ref[...] = k.reshape(k5_ref.shape)
    v5_ref[...] = v.reshape(v5_ref.shape)


def _mem_kv(mem, g, w, layer, n_mem, prev):
    n, d = mem.shape
    depth = w.shape[0]
    tm = min(ROW_TILE, n)
    seqs = tm // n_mem
    row = pl.BlockSpec((tm, d), lambda i: (i, 0))
    blk5 = ((seqs, n_mem, XA_H, XA_HD), lambda i: (i, 0, 0, 0))
    shape5 = jax.ShapeDtypeStruct((depth, n // n_mem, n_mem, XA_H, XA_HD), F32)
    return _stack_call(
        _mem_kv_kernel, prev, {2: blk5, 3: blk5}, layer,
        in_specs=[row, _const_spec((1, d)), _wspec(w, layer)],
        args=(mem, g, w),
        grid=(n // tm,),
        out_specs=[row, row, None, None],
        out_shape=[jax.ShapeDtypeStruct((n, d), F32)] * 2 + [shape5] * 2,
        compiler_params=_params(("parallel",)),
        name="mem_kv",
    )


def _wspec(w, layer):
    return _layer_spec(w.shape, layer) if w.ndim == 3 else _const_spec(w.shape)


def _norm_matmul(x, g, w, col_ranges, name, layer=None):
    n, d = x.shape
    tm = min(ROW_TILE, n)
    w_spec = _wspec(w, layer)
    widths = [sum(b - a for a, b in pieces) for pieces in col_ranges]
    return pl.pallas_call(
        functools.partial(_norm_matmul_kernel, col_ranges=col_ranges),
        grid=(n // tm,),
        in_specs=[pl.BlockSpec((tm, d), lambda i: (i, 0)), _const_spec((1, d)), w_spec],
        out_specs=[pl.BlockSpec((tm, wd), lambda i: (i, 0)) for wd in widths],
        out_shape=[jax.ShapeDtypeStruct((n, wd), F32) for wd in widths],
        compiler_params=_params(("parallel",)),
        name=name,
    )(x, g, w)


def _out_proj_kernel(ys_ref, yr_ref, yg_ref, yh_ref, x_ref, wo_ref, gpost_ref, gpre_ref, wq_ref, x1_ref, q_ref):
    x1 = _mix_out(ys_ref, yr_ref, yg_ref, yh_ref, x_ref, wo_ref, gpost_ref)
    x1_ref[...] = x1
    q_ref[...] = _bdot(_rms(x1, gpre_ref[...]), wq_ref[...])


def _out_proj(ys, x, w_out, g_post, g_pre, w_xq, layer):
    n, d = x.shape
    tm = min(ROW_TILE, n)
    row = lambda wd: pl.BlockSpec((tm, wd), lambda i: (i, 0))
    vec = _const_spec((1, d))
    return pl.pallas_call(
        _out_proj_kernel,
        grid=(n // tm,),
        in_specs=[row(GROUP_W)] * 4 + [row(d), _wspec(w_out, layer), vec, vec, _wspec(w_xq, layer)],
        out_specs=[row(d)] * 2,
        out_shape=[jax.ShapeDtypeStruct((n, d), F32)] * 2,
        compiler_params=_params(("parallel",)),
        name="out_proj",
    )(*ys, x, w_out, g_post, g_pre, w_xq)


def _ffn_tail(o, x1, wxo_ref, gxa_ref, gpre_ref, wgu_ref, wdn_ref, gpost_ref, d_ff):
    a = _bdot(o, wxo_ref[...])
    x2 = x1 + _rms(a, gxa_ref[...])
    h = _rms(x2, gpre_ref[...]).astype(BF16)
    acc = jnp.zeros(x2.shape, F32)
    for j in range(0, d_ff, FF_TILE):
        g = jnp.dot(h, wgu_ref[:, j:j + FF_TILE], preferred_element_type=F32)
        u = jnp.dot(h, wgu_ref[:, d_ff + j:d_ff + j + FF_TILE], preferred_element_type=F32)
        act = (_silu(g) * u).astype(BF16)
        acc = acc + jnp.dot(act, wdn_ref[j:j + FF_TILE, :], preferred_element_type=F32)
    return x2 + _rms(acc, gpost_ref[...])


def _ffn_kernel(o_ref, x1_ref, wxo_ref, gxa_ref, gpre_ref, wgu_ref, wdn_ref, gpost_ref, x3_ref, *, d_ff):
    x3_ref[...] = _ffn_tail(o_ref[...], x1_ref[...], wxo_ref, gxa_ref, gpre_ref, wgu_ref, wdn_ref, gpost_ref,
                            d_ff)


def _ffn(o, x1, w_xo, g_xa, g_pre, w_gu, w_down, g_post, layer):
    n, d = x1.shape
    d_ff = w_down.shape[-2]
    tm = min(ROW_TILE, n)
    row = pl.BlockSpec((tm, d), lambda i: (i, 0))
    vec = _const_spec((1, d))
    return pl.pallas_call(
        functools.partial(_ffn_kernel, d_ff=d_ff),
        grid=(n // tm,),
        in_specs=[row, row, _wspec(w_xo, layer), vec, vec, _wspec(w_gu, layer), _wspec(w_down, layer), vec],
        out_specs=row,
        out_shape=jax.ShapeDtypeStruct((n, d), F32),
        compiler_params=_params(("parallel",)),
        name="attn_out_ffn",
    )(o, x1, w_xo, g_xa, g_pre, w_gu, w_down, g_post)


def _mix_out(ys_ref, yr_ref, yg_ref, yh_ref, x_ref, wo_ref, gpost_ref):
    mix = _bdot(ys_ref[...], wo_ref[0:GROUP_W, :])
    mix = mix + _bdot(yr_ref[...], wo_ref[GROUP_W:2 * GROUP_W, :])
    mix = mix + _bdot(yg_ref[...], wo_ref[2 * GROUP_W:3 * GROUP_W, :])
    mix = mix + _bdot(yh_ref[...], wo_ref[3 * GROUP_W:4 * GROUP_W, :])
    return x_ref[...] + _rms(mix, gpost_ref[...])


def _xattn_ffn_kernel(ys_ref, yr_ref, yg_ref, yh_ref, x_ref, wo_ref, gmix_ref, k_ref, v_ref, gq_ref, wq_ref,
                      wxo_ref, gxa_ref, gpre_ref, wgu_ref, wdn_ref, gpost_ref, x3_ref, o_scr, *, d_ff):
    x1 = _mix_out(ys_ref, yr_ref, yg_ref, yh_ref, x_ref, wo_ref, gmix_ref)
    q = _bdot(_rms(x1, gq_ref[...]), wq_ref[...])
    scale = XA_HD ** -0.5
    for h in range(XA_H):
        sl = slice(h * XA_HD, (h + 1) * XA_HD)
        s = _bdot_nt(q[:, sl], k_ref[0, :, sl]) * scale
        o_scr[:, sl] = _softmax_pv(s, v_ref[0, :, sl])
    x3_ref[...] = _ffn_tail(o_scr[...], x1, wxo_ref, gxa_ref, gpre_ref, wgu_ref, wdn_ref, gpost_ref, d_ff)


def _xattn_ffn(ys, x, w_out, g_mix, k, v, g_q, w_xq, w_xo, g_xa, g_pre, w_gu, w_down, g_post, seq_len, layer):
    n, d = x.shape
    n_seq, n_mem, _ = k.shape
    d_ff = w_down.shape[-2]
    nt = seq_len // ROW_TILE
    row = lambda wd: pl.BlockSpec((ROW_TILE, wd), lambda b, t: (b * nt + t, 0))
    kv = pl.BlockSpec((1, n_mem, d), lambda b, t: (b, 0, 0))
    vec = _const_spec((1, d))
    return pl.pallas_call(
        functools.partial(_xattn_ffn_kernel, d_ff=d_ff),
        grid=(n_seq, nt),
        in_specs=[row(GROUP_W)] * 4 + [row(d), _wspec(w_out, layer), vec, kv, kv, vec, _wspec(w_xq, layer),
                                       _wspec(w_xo, layer), vec, vec, _wspec(w_gu, layer),
                                       _wspec(w_down, layer), vec],
        out_specs=row(d),
        out_shape=jax.ShapeDtypeStruct((n, d), F32),
        scratch_shapes=[pltpu.VMEM((ROW_TILE, d), F32)],
        compiler_params=_params(("parallel", "parallel")),
        name="xattn_ffn",
    )(*ys, x, w_out, g_mix, k, v, g_q, w_xq, w_xo, g_xa, g_pre, w_gu, w_down, g_post)


def _softmax_pv(s, v):
    m = jnp.max(s, axis=-1, keepdims=True)
    p = jnp.exp(s - m)
    return _bdot(p, v) / jnp.sum(p, axis=-1, keepdims=True)


def _attn_sample_kernel(q_ref, k_ref, v_ref, o_ref, *, seqs, seg):
    scale = XA_HD ** -0.5
    rows = seqs * seg
    n_mem = k_ref.shape[2]
    wq = jnp.concatenate([q_ref[:, h * XA_HD:(h + 1) * XA_HD] for h in range(XA_H)], axis=0)
    wshape = (XA_H * rows, n_mem * XA_H)
    valid = (_iota(wshape, 0) >> (rows.bit_length() - 1)) == (_iota(wshape, 1) & (XA_H - 1))
    row_seq = _iota((rows, XA_HD), 0) >> (seg.bit_length() - 1)
    accs = [jnp.zeros((rows, XA_HD), F32) for _ in range(XA_H)]
    for i in range(seqs):
        k2 = k_ref[0, i].reshape(n_mem * XA_H, XA_HD)
        v2 = v_ref[0, i].reshape(n_mem * XA_H, XA_HD)
        s = jnp.where(valid, _bdot_nt(wq, k2) * scale, -jnp.inf)
        o = _softmax_pv(s, v2)
        for h in range(XA_H):
            accs[h] = jnp.where(row_seq == i, o[h * rows:(h + 1) * rows], accs[h])
    for h in range(XA_H):
        o_ref[:, h * XA_HD:(h + 1) * XA_HD] = accs[h]


def _attention_sample(q, cache_k, cache_v, layer, seq_len):
    n, d = q.shape
    _, n_seq, n_mem, heads, hd = cache_k.shape
    kv = pl.BlockSpec((1, ATT_SEQS, n_mem, heads, hd), lambda i: (layer, i, 0, 0, 0))
    row = pl.BlockSpec((ATT_SEQS * seq_len, d), lambda i: (i, 0))
    return pl.pallas_call(
        functools.partial(_attn_sample_kernel, seqs=ATT_SEQS, seg=seq_len),
        grid=(n_seq // ATT_SEQS,),
        in_specs=[row, kv, kv],
        out_specs=row,
        out_shape=jax.ShapeDtypeStruct((n, d), F32),
        compiler_params=_params(("parallel",)),
        name="xattn_sample",
    )(q, cache_k, cache_v)


def _ssd_intra(q, k, v, cum_x, cum_t, mask):
    sc = _bdot_nt(q, k)
    outs = []
    for hh in range(2):
        col = cum_x[:, hh * HEAD:(hh + 1) * HEAD]
        dec = jnp.exp(jnp.where(mask, col - cum_t[hh:hh + 1, :], -jnp.inf))
        outs.append(_bdot(sc * dec, v[:, hh * HEAD:(hh + 1) * HEAD]))
    return jnp.concatenate(outs, axis=-1)


def _ssd_common(u, dtraw_ref, dtb_ref, alog_ref, conv, seg_shift):
    rows = u.shape[0]
    xbc = _silu(conv)
    xs = xbc[:, 0:GROUP_W]
    bm = xbc[:, GROUP_W:2 * GROUP_W]
    cm = xbc[:, 2 * GROUP_W:3 * GROUP_W]
    dt = _softplus(dtraw_ref[...] + dtb_ref[...])
    la = -jnp.exp(alog_ref[...]) * dt
    cum = _seg_cumsum(la, seg_shift)
    expand = _head_expand_mat()
    dt_x = _dot_sel(dt, expand)
    cum_x = _dot_sel(cum, expand)
    sel8 = (_iota((8, 128), 0) == _iota((8, 128), 1)).astype(BF16)
    cum_t = _sel_dot_nt(sel8, cum)
    expand_full = ((_iota((128, 4 * 128), 1) >> 7) == _iota((128, 4 * 128), 0)).astype(BF16)
    cum_full = _dot_sel(cum, expand_full)
    return xs, bm, cm, dt_x, cum_x, cum_t, cum_full


def _ssd_finish(y, xs, z, d_ref, g_ref):
    y = y + d_ref[...] * xs
    return _rms(y * _silu(z), g_ref[...])


def _ssd_prompt_kernel(u_ref, dtraw_ref, cw_ref, cb_ref, dtb_ref, alog_ref, d_ref, g_ref,
                       y_ref, s_out_ref, buf_out_ref, s_ref, xx_ref, y_scr):
    t = pl.program_id(1)
    rows = u_ref.shape[0]

    @pl.when(t == 0)
    def _():
        s_ref[...] = jnp.zeros(s_ref.shape, F32)

    u = u_ref[...]
    conv = _conv_taps_carry(u[:, GROUP_W:], xx_ref, cw_ref, cb_ref, t == 0)
    xs, bm, cm, dt_x, cum_x, _, cum_full = _ssd_common(u, dtraw_ref, dtb_ref, alog_ref, conv, 6)
    v = xs * dt_x
    ecum_x = jnp.exp(cum_x)
    hmask01 = _head_block_mask().astype(BF16)
    gmask01 = ((_iota((GROUP_W, GROUP_W), 0) >> 7) == (_iota((GROUP_W, GROUP_W), 1) >> 7)).astype(BF16)
    wrow = _iota((CHUNK, GROUP_W), 0)
    wcol = _iota((CHUNK, GROUP_W), 1) & (CHUNK - 1)
    n_chunks = rows // CHUNK
    groups = [slice(g * 128, (g + 1) * 128) for g in range(SSD_G)]
    a_wide, ds_t = [], []
    for c in range(n_chunks):
        rs = slice(c * CHUNK, (c + 1) * CHUNK)
        cum_c = cum_x[rs]
        last_x = cum_x[c * CHUNK + CHUNK - 1:(c + 1) * CHUNK, :]
        vend = v[rs] * jnp.exp(last_x - cum_c)
        cum_row = jnp.sum(jnp.where(wrow == wcol, cum_c, 0.0), axis=0, keepdims=True)
        dec = jnp.exp(jnp.where(wcol <= wrow, cum_c - cum_row, -jnp.inf))
        a_wide.append(_bdot_nt(cm[rs], _block_diag_rows(bm[rs], gmask01)) * dec)
        ds_t.append([_bdot_tn(vend[:, ls], bm[rs, ls]) for ls in groups])
    s_in = []
    s_cur = [s_ref[ls, :] for ls in groups]
    for c in range(n_chunks):
        last_full = cum_full[c * CHUNK + CHUNK - 1:(c + 1) * CHUNK, :]
        s_in.append([x.astype(BF16) for x in s_cur])
        for g in range(SSD_G):
            decay = jnp.concatenate(
                [jnp.broadcast_to(jnp.exp(last_full[:, h * 128:(h + 1) * 128]), (HEAD, 128))
                 for h in (2 * g, 2 * g + 1)], axis=0)
            s_cur[g] = decay * s_cur[g] + ds_t[c][g]
    for g, ls in enumerate(groups):
        s_ref[ls, :] = s_cur[g]
    for c in range(n_chunks):
        rs = slice(c * CHUNK, (c + 1) * CHUNK)
        y_intra = _bdot(a_wide[c], _block_diag_rows(v[rs], hmask01))
        y_inter = [_bdot_nt(cm[rs, ls], s_in[c][g]) for g, ls in enumerate(groups)]
        y_scr[rs, :] = y_intra + jnp.concatenate(y_inter, axis=-1) * ecum_x[rs]
    y_ref[...] = _ssd_finish(y_scr[...], xs, u[:, 0:GROUP_W], d_ref, g_ref)

    @pl.when(t == pl.num_programs(1) - 1)
    def _():
        for h in range(N_HEADS):
            s_out_ref[0, h] = s_ref[h * HEAD:(h + 1) * HEAD, :]
        buf_out_ref[0] = xx_ref[pl.ds(8 - (CONV_W - 1), CONV_W - 1), :]


def _ssd_sample_kernel(u_ref, dtraw_ref, s0_ref, buf_ref, cw_ref, cb_ref, dtb_ref, alog_ref, d_ref,
                       g_ref, y_ref, s_out_ref, buf_out_ref, *, seqs):
    rows = u_ref.shape[0]
    u = u_ref[...]
    conv, new_buf = _conv_taps_seq4(u[:, GROUP_W:], buf_ref[...], cw_ref, cb_ref)
    buf_out_ref[...] = new_buf
    mask = _seg_causal_mask(rows, 2)
    xs, bm, cm, dt_x, cum_x, cum_t, cum_full = _ssd_common(u, dtraw_ref, dtb_ref, alog_ref, conv, 2)
    v = xs * dt_x
    ecum_x = jnp.exp(cum_x)
    pick_last = (_iota((rows, rows), 1) == (_iota((rows, rows), 0) | 3)).astype(BF16)
    last_x = _sel_dot(pick_last, cum_x)
    vend = v * jnp.exp(last_x - cum_x)
    row_seq = _iota((rows, 128), 0) >> 2
    ys = []
    for g in range(SSD_G):
        ls = slice(g * 128, (g + 1) * 128)
        q = cm[:, ls]
        k = bm[:, ls]
        y_g = _ssd_intra(q, k, v[:, ls], cum_x[:, ls], cum_t[2 * g:2 * g + 2, :], mask)
        y_int = jnp.zeros((rows, 128), F32)
        for i in range(seqs):
            in_seq = row_seq == i
            s_prev = jnp.concatenate([s0_ref[i, 2 * g], s0_ref[i, 2 * g + 1]], axis=0)
            y_int = jnp.where(in_seq, _bdot_nt(q, s_prev), y_int)
            ds = _bdot_tn(jnp.where(in_seq, vend[:, ls], 0.0), k)
            for hh in range(2):
                h = 2 * g + hh
                decay = jnp.exp(cum_full[4 * i + 3:4 * i + 4, h * 128:(h + 1) * 128])
                s_out_ref[i, h] = decay * s0_ref[i, h] + ds[hh * HEAD:(hh + 1) * HEAD, :]
        ys.append(y_g + y_int * ecum_x[:, ls])
    y_ref[...] = _ssd_finish(jnp.concatenate(ys, axis=-1), xs, u[:, 0:GROUP_W], d_ref, g_ref)


def _ssd(u, dtraw, state, buf, cw, cb, dtb, alog, d_x, gain, n_seq, seq_len, layer, prev=None):
    n = u.shape[0]
    cch = cw.shape[1]
    vec = lambda wd: _const_spec((1, wd))
    common_in = [_const_spec(cw.shape), vec(cch), vec(128), vec(128), vec(GROUP_W), vec(GROUP_W)]
    if state is None:
        nt = seq_len // MIX_TILE
        row = lambda wd: pl.BlockSpec((MIX_TILE, wd), lambda b, t: (b * nt + t, 0))
        return pl.pallas_call(
            _ssd_prompt_kernel,
            grid=(n_seq, nt),
            in_specs=[row(u.shape[1]), row(128)] + common_in,
            out_specs=[row(GROUP_W),
                       pl.BlockSpec((1, N_HEADS, HEAD, SSD_N), lambda b, t: (b, 0, 0, 0)),
                       pl.BlockSpec((1, CONV_W - 1, cch), lambda b, t: (b, 0, 0))],
            out_shape=[jax.ShapeDtypeStruct((n, GROUP_W), F32),
                       jax.ShapeDtypeStruct((n_seq, N_HEADS, HEAD, SSD_N), F32),
                       jax.ShapeDtypeStruct((n_seq, CONV_W - 1, cch), F32)],
            scratch_shapes=[pltpu.VMEM((GROUP_W, SSD_N), F32), pltpu.VMEM((MIX_TILE + 8, cch), F32),
                            pltpu.VMEM((MIX_TILE, GROUP_W), F32)],
            compiler_params=_params(("parallel", "arbitrary")),
            name="ssd_prompt",
        )(u, dtraw, cw, cb, dtb, alog, d_x, gain)
    rows = SAMPLE_SEQS * seq_len
    row = lambda wd: pl.BlockSpec((rows, wd), lambda i: (i, 0))
    st = pl.BlockSpec((None, SAMPLE_SEQS, N_HEADS, HEAD, SSD_N), lambda i: (layer, i, 0, 0, 0))
    return _stack_call(
        functools.partial(_ssd_sample_kernel, seqs=SAMPLE_SEQS), prev,
        {1: ((SAMPLE_SEQS, N_HEADS, HEAD, SSD_N), lambda i: (i, 0, 0, 0))}, layer,
        in_specs=[row(u.shape[1]), row(128), st, row(cch)] + common_in,
        args=(u, dtraw, state, buf, cw, cb, dtb, alog, d_x, gain),
        grid=(n_seq // SAMPLE_SEQS,),
        out_specs=[row(GROUP_W), None, row(cch)],
        out_shape=[jax.ShapeDtypeStruct((n, GROUP_W), F32),
                   jax.ShapeDtypeStruct(state.shape, F32),
                   jax.ShapeDtypeStruct((n, cch), F32)],
        compiler_params=_params(("parallel",)),
        name="ssd_sample",
    )


def _rope_split(x, cos, sin_signed):
    swapped = jnp.concatenate([x[:, 128:], x[:, :128]], axis=1)
    return x * cos + swapped * sin_signed


def _ret_qkv(u_ref, cos_ref, sin_ref):
    u = u_ref[...]
    q = _rope_split(u[:, 0:GROUP_W], cos_ref[...], sin_ref[...])
    k = _rope_split(u[:, GROUP_W:2 * GROUP_W], cos_ref[...], sin_ref[...]) * (HEAD ** -0.5)
    return q, k, u[:, 2 * GROUP_W:3 * GROUP_W], u[:, 3 * GROUP_W:4 * GROUP_W]


def _ret_prompt_kernel(u_ref, cos_ref, sin_ref, ecum_ref, eend_ref, elast_ref, dec_ref, g_ref,
                       y_ref, s_out_ref, s_ref, y_scr):
    t = pl.program_id(1)
    rows = u_ref.shape[0]

    @pl.when(t == 0)
    def _():
        s_ref[...] = jnp.zeros(s_ref.shape, F32)

    q, k, v, gate = _ret_qkv(u_ref, cos_ref, sin_ref)
    r_i = _iota((GROUP_W, GROUP_W), 0)
    c_i = _iota((GROUP_W, GROUP_W), 1)
    kmask01 = ((r_i >> 6) == ((c_i & 127) >> 5)).astype(BF16)
    vmask01 = _head_block_mask().astype(BF16)
    smask = ((r_i & 127) >> 5) == (c_i >> 6)
    chunks = [slice(c * CHUNK, (c + 1) * CHUNK) for c in range(rows // CHUNK)]
    a_wide = [_bdot_nt(q[rs], _block_diag_rows(k[rs], kmask01)) * dec_ref[...] for rs in chunks]
    ds = [jnp.where(smask, _bdot_tn(k[rs], v[rs] * eend_ref[...]), 0.0) for rs in chunks]
    s = s_ref[...]
    s_in = []
    for d in ds:
        s_in.append(s.astype(BF16))
        s = elast_ref[...] * s + d
    s_ref[...] = s
    for rs, a, s_c in zip(chunks, a_wide, s_in):
        y_intra = _bdot(a, _block_diag_rows(v[rs], vmask01))
        y_scr[rs, :] = y_intra + _bdot(q[rs], s_c) * ecum_ref[...]
    y_ref[...] = _silu(gate) * (_head_stat(y_scr[...], True) * g_ref[...])

    @pl.when(t == pl.num_programs(1) - 1)
    def _():
        half = HEAD // 2
        for h in range(N_HEADS):
            vs = slice(h * HEAD, (h + 1) * HEAD)
            s_out_ref[0, h, 0:half, :] = s_ref[h * half:(h + 1) * half, vs]
            s_out_ref[0, h, half:HEAD, :] = s_ref[128 + h * half:128 + (h + 1) * half, vs]


def _ret_tables(block, seg):
    log_gamma = jnp.log(1.0 - jnp.exp2(-5.0 - jnp.arange(N_HEADS, dtype=F32)))
    pos = (jnp.arange(block) % seg).astype(F32)
    cum = (pos[:, None] + 1.0) * log_gamma[None, :]
    last = seg * log_gamma
    rep = lambda a: jnp.repeat(a, HEAD, axis=-1)
    same = (jnp.arange(block)[:, None] // seg) == (jnp.arange(block)[None, :] // seg)
    mask = same & (jnp.arange(block)[None, :] <= jnp.arange(block)[:, None])
    dec = jnp.exp(jnp.where(mask[None], cum.T[:, :, None] - cum.T[:, None, :], -jnp.inf))
    return (rep(jnp.exp(cum)), rep(jnp.exp(last[None, :] - cum)), rep(jnp.exp(last)[None, :]), dec)


def _ret(u, cos, sin_signed, gain, n_seq, seq_len):
    n = u.shape[0]
    vec = _const_spec((1, GROUP_W))
    nt = seq_len // MIX_TILE
    ecum, eend, elast, dec = _ret_tables(CHUNK, CHUNK)
    dec = dec.transpose(1, 0, 2).reshape(CHUNK, N_HEADS * CHUNK)
    row = lambda wd: pl.BlockSpec((MIX_TILE, wd), lambda b, t: (b * nt + t, 0))
    pos = pl.BlockSpec((MIX_TILE, GROUP_W), lambda b, t: (t, 0))
    return pl.pallas_call(
        _ret_prompt_kernel,
        grid=(n_seq, nt),
        in_specs=[row(u.shape[1]), pos, pos, _const_spec(ecum.shape), _const_spec(eend.shape), vec,
                  _const_spec(dec.shape), vec],
        out_specs=[row(GROUP_W), pl.BlockSpec((1, N_HEADS, HEAD, HEAD), lambda b, t: (b, 0, 0, 0))],
        out_shape=[jax.ShapeDtypeStruct((n, GROUP_W), F32),
                   jax.ShapeDtypeStruct((n_seq, N_HEADS, HEAD, HEAD), F32)],
        scratch_shapes=[pltpu.VMEM((GROUP_W, GROUP_W), F32), pltpu.VMEM((MIX_TILE, GROUP_W), F32)],
        compiler_params=_params(("parallel", "arbitrary")),
        name="ret_prompt",
    )(u, cos, sin_signed, ecum, eend, elast, dec, gain)


def _rg_gates(xr, wa_ref, ba_ref, wx_ref, bx_ref, lam_ref):
    r_gate = _sigmoid(_bdot(xr, wa_ref[...]) + ba_ref[...])
    i_gate = _sigmoid(_bdot(xr, wx_ref[...]) + bx_ref[...])
    log_a = (-RG_C * _softplus(-lam_ref[...])) * r_gate
    a = jnp.exp(log_a)
    b = jnp.sqrt(-jnp.tanh(log_a) * (jnp.exp(2.0 * log_a) + 1.0)) * (i_gate * xr)
    return a, b


def _rg_scan(a, b, seg):
    pos = _iota(a.shape, 0) & (seg - 1)
    d = 1
    while d < seg:
        ok = pos >= d
        a_sh = jnp.where(ok, pltpu.roll(a, d, 0), 1.0)
        b_sh = jnp.where(ok, pltpu.roll(b, d, 0), 0.0)
        b = a * b_sh + b
        a = a * a_sh
        d *= 2
    return a, b


def _rg_scan_carry(a, b, h0):
    a, b = _rg_scan(a, b, 8)
    outs = []
    carry = h0
    for g in range(a.shape[0] // 8):
        h_g = b[g * 8:(g + 1) * 8] + a[g * 8:(g + 1) * 8] * carry
        outs.append(h_g)
        carry = h_g[7:8]
    return jnp.concatenate(outs, axis=0)


def _rg_prompt_kernel(u_ref, cw_ref, cb_ref, wa_ref, ba_ref, wx_ref, bx_ref, lam_ref,
                      y_ref, h_out_ref, buf_out_ref, h_ref, xx_ref):
    t = pl.program_id(1)
    rows = u_ref.shape[0]

    @pl.when(t == 0)
    def _():
        h_ref[...] = jnp.zeros(h_ref.shape, F32)

    u = u_ref[...]
    xr = _conv_taps_carry(u[:, 0:GROUP_W], xx_ref, cw_ref, cb_ref, t == 0)
    a, b = _rg_gates(xr, wa_ref, ba_ref, wx_ref, bx_ref, lam_ref)
    hseq = _rg_scan_carry(a, b, h_ref[0:1, :])
    h_ref[...] = jnp.broadcast_to(hseq[rows - 1:rows, :], h_ref.shape)
    y_ref[...] = _gelu_tanh(u[:, GROUP_W:]) * hseq

    @pl.when(t == pl.num_programs(1) - 1)
    def _():
        h_out_ref[0] = hseq[rows - 1:rows, :]
        buf_out_ref[0] = xx_ref[pl.ds(8 - (CONV_W - 1), CONV_W - 1), :]


def _rg_sample_kernel(u_ref, h0_ref, buf_ref, cw_ref, cb_ref, wa_ref, ba_ref, wx_ref, bx_ref, lam_ref,
                      y_ref, h_out_ref, buf_out_ref):
    u = u_ref[...]
    xr, new_buf = _conv_taps_seq4(u[:, 0:GROUP_W], buf_ref[...], cw_ref, cb_ref)
    buf_out_ref[...] = new_buf
    a, b = _rg_gates(xr, wa_ref, ba_ref, wx_ref, bx_ref, lam_ref)
    a_cum, h_loc = _rg_scan(a, b, 4)
    hseq = h_loc + a_cum * h0_ref[...]
    h_out_ref[...] = hseq
    y_ref[...] = _gelu_tanh(u[:, GROUP_W:]) * hseq


def _rg(u, h0_rows, buf, cw, cb, wa, ba, wx, bx, lam, n_seq, seq_len):
    n = u.shape[0]
    vec = _const_spec((1, GROUP_W))
    common_in = [_const_spec(cw.shape), vec, _const_spec(wa.shape), vec, _const_spec(wx.shape), vec, vec]
    if h0_rows is None:
        nt = seq_len // RG_TILE
        row = lambda wd: pl.BlockSpec((RG_TILE, wd), lambda b, t: (b * nt + t, 0))
        return pl.pallas_call(
            _rg_prompt_kernel,
            grid=(n_seq, nt),
            in_specs=[row(u.shape[1])] + common_in,
            out_specs=[row(GROUP_W), pl.BlockSpec((1, 1, GROUP_W), lambda b, t: (b, 0, 0)),
                       pl.BlockSpec((1, CONV_W - 1, GROUP_W), lambda b, t: (b, 0, 0))],
            out_shape=[jax.ShapeDtypeStruct((n, GROUP_W), F32),
                       jax.ShapeDtypeStruct((n_seq, 1, GROUP_W), F32),
                       jax.ShapeDtypeStruct((n_seq, CONV_W - 1, GROUP_W), F32)],
            scratch_shapes=[pltpu.VMEM((8, GROUP_W), F32), pltpu.VMEM((RG_TILE + 8, GROUP_W), F32)],
            compiler_params=_params(("parallel", "arbitrary")),
            name="rglru_prompt",
        )(u, cw, cb, wa, ba, wx, bx, lam)
    rows = SAMPLE_SEQS * seq_len
    row = lambda wd: pl.BlockSpec((rows, wd), lambda i: (i, 0))
    return pl.pallas_call(
        _rg_sample_kernel,
        grid=(n_seq // SAMPLE_SEQS,),
        in_specs=[row(u.shape[1]), row(GROUP_W), row(GROUP_W)] + common_in,
        out_specs=[row(GROUP_W)] * 3,
        out_shape=[jax.ShapeDtypeStruct((n, GROUP_W), F32)] * 3,
        compiler_params=_params(("parallel",)),
        name="rglru_sample",
    )(u, h0_rows, buf, cw, cb, wa, ba, wx, bx, lam)


def _hg_inputs(u_ref, lb_ref, seg_shift):
    u = u_ref[...]
    rows = u.shape[0]
    lb = lb_ref[...]
    fg = lb + (1.0 - lb) * _sigmoid(u[:, GROUP_W:2 * GROUP_W])
    q = _silu(u[:, 0:GROUP_W])
    k = 1.0 - fg
    v = u[:, 2 * GROUP_W:3 * GROUP_W]
    cum = _seg_cumsum(jnp.log(fg), seg_shift)
    return q, k, v, cum, u[:, 3 * GROUP_W:4 * GROUP_W]


def _hg_prompt_kernel(u_ref, lb_ref, g_ref, y_ref, s_out_ref, st_ref, y_scr):
    t = pl.program_id(1)
    rows = u_ref.shape[0]

    @pl.when(t == 0)
    def _():
        st_ref[...] = jnp.zeros(st_ref.shape, F32)

    q, k, v, cum, gate = _hg_inputs(u_ref, lb_ref, 6)
    hmask = _head_block_mask()
    hmask01 = hmask.astype(BF16)
    ones_bd = hmask01
    y_scr[...] = jnp.dot((q * k).astype(BF16), ones_bd, preferred_element_type=F32) * v
    row = _iota((rows, GROUP_W), 0)
    trow = _iota((CHUNK, GROUP_W), 0)
    tcol = _iota((CHUNK, GROUP_W), 1) & (CHUNK - 1)
    cum = cum * LOG2_E
    levels = []
    blk_last = cum
    h = 1
    while h < CHUNK:
        upper = (row & h) != 0
        qn = jnp.where(upper, q * jnp.exp2(cum - pltpu.roll(blk_last, h, 0)), 0.0)
        kn = k * jnp.exp2(blk_last - cum)
        sh = (2 * h).bit_length() - 1
        pair = ((trow >> sh) == (tcol >> sh)) & ((tcol & h) == 0)
        levels.append((pair, qn, kn))
        blk_last = jnp.where(upper, blk_last, pltpu.roll(blk_last, rows - h, 0))
        h *= 2
    q_in = q * jnp.exp2(cum)
    k_end = k * jnp.exp2(blk_last - cum)
    n_chunks = rows // CHUNK
    a_wide, ds_t = [], []
    for c in range(n_chunks):
        rs = slice(c * CHUNK, (c + 1) * CHUNK)
        a_c = jnp.zeros((CHUNK, GROUP_W), F32)
        for pair, qn, kn in levels:
            g = _bdot_nt(qn[rs], _block_diag_rows(kn[rs], hmask01))
            a_c = a_c + jnp.where(pair, g, 0.0)
        a_wide.append(a_c)
        ds_t.append(jnp.where(hmask, _bdot_tn(v[rs], k_end[rs]), 0.0))
    st = st_ref[...]
    st_in = []
    for c in range(n_chunks):
        st_in.append(st.astype(BF16))
        st = jnp.exp2(blk_last[c * CHUNK:c * CHUNK + 1, :]) * st + ds_t[c]
    st_ref[...] = st
    for c in range(n_chunks):
        rs = slice(c * CHUNK, (c + 1) * CHUNK)
        y_c = _bdot(a_wide[c], _block_diag_rows(v[rs], hmask01)) + _bdot_nt(q_in[rs], st_in[c])
        y_scr[rs, :] = y_scr[rs, :] + y_c
    y_ref[...] = _silu(gate) * (_head_stat(y_scr[...], False) * g_ref[...])

    @pl.when(t == pl.num_programs(1) - 1)
    def _():
        eye = (_iota((HEAD, HEAD), 0) == _iota((HEAD, HEAD), 1)).astype(BF16)
        for h in range(N_HEADS):
            s_out_ref[0, h] = _sel_dot_nt(eye, st_ref[h * HEAD:(h + 1) * HEAD, h * HEAD:(h + 1) * HEAD])


def _hg(u, lb, gain, n_seq, seq_len):
    n = u.shape[0]
    vec = _const_spec((1, GROUP_W))
    nt = seq_len // MIX_TILE
    row = lambda wd: pl.BlockSpec((MIX_TILE, wd), lambda b, t: (b * nt + t, 0))
    return pl.pallas_call(
        _hg_prompt_kernel,
        grid=(n_seq, nt),
        in_specs=[row(u.shape[1]), vec, vec],
        out_specs=[row(GROUP_W), pl.BlockSpec((1, N_HEADS, HEAD, HEAD), lambda b, t: (b, 0, 0, 0))],
        out_shape=[jax.ShapeDtypeStruct((n, GROUP_W), F32),
                   jax.ShapeDtypeStruct((n_seq, N_HEADS, HEAD, HEAD), F32)],
        scratch_shapes=[pltpu.VMEM((GROUP_W, GROUP_W), F32), pltpu.VMEM((MIX_TILE, GROUP_W), F32)],
        compiler_params=_params(("parallel", "arbitrary")),
        name="hgrn_prompt",
    )(u, lb, gain)


def _t_recurrence(q_scr, k_scr, v, dec_row, dec_scr, s_ref, n_seq, seq_len):
    cols = [slice(t * n_seq, (t + 1) * n_seq) for t in range(seq_len)]
    v_t = [v[:, c] for c in cols]
    outs = [jnp.zeros((HEAD, n_seq), F32) for _ in cols]
    for k in range(HEAD):
        s = s_ref[k]
        for t, c in enumerate(cols):
            dec = dec_row if dec_scr is None else dec_scr[k:k + 1, c]
            s = dec * s + k_scr[k:k + 1, c] * v_t[t]
            outs[t] = outs[t] + q_scr[k:k + 1, c] * s
        s_ref[k] = s
    return jnp.concatenate(outs, axis=1)


def _t_head_norm(o, center):
    if center:
        o = o - jnp.mean(o, axis=0, keepdims=True)
    return o * lax.rsqrt(jnp.mean(o * o, axis=0, keepdims=True) + EPS)


def _ret_t_kernel(q_ref, k_ref, v_ref, g_ref, cos_ref, sin_ref, gam_ref, gain_ref, s0_ref,
                  y_ref, s_ref, q_scr, k_scr, *, n_seq, seq_len):
    half = HEAD // 2

    def rope(x):
        swapped = jnp.concatenate([x[half:], x[:half]], axis=0)
        return x * cos_ref[...] + swapped * sin_ref[...]

    q_scr[...] = rope(q_ref[...])
    k_scr[...] = rope(k_ref[...]) * (HEAD ** -0.5)
    s_ref[0] = s0_ref[0]
    o = _t_recurrence(q_scr, k_scr, v_ref[...], gam_ref[0], None, s_ref.at[0], n_seq, seq_len)
    gain = jnp.concatenate([gain_ref[...]] * seq_len, axis=1)
    parts = [_t_head_norm(o[:, t * n_seq:(t + 1) * n_seq], True) for t in range(seq_len)]
    y_ref[...] = _silu(g_ref[...]) * (jnp.concatenate(parts, axis=1) * gain)


def _hg_t_kernel(q_ref, f_ref, i_ref, g_ref, lb_ref, gain_ref, s0_ref, y_ref, s_ref, q_scr, k_scr, f_scr,
                 *, n_seq, seq_len):
    lb = jnp.concatenate([lb_ref[...]] * seq_len, axis=1)
    fg = lb + (1.0 - lb) * _sigmoid(f_ref[...])
    q_scr[...] = _silu(q_ref[...])
    k_scr[...] = 1.0 - fg
    f_scr[...] = fg
    s_ref[0] = s0_ref[0]
    o = _t_recurrence(q_scr, k_scr, i_ref[...], None, f_scr, s_ref.at[0], n_seq, seq_len)
    gain = jnp.concatenate([gain_ref[...]] * seq_len, axis=1)
    parts = [_t_head_norm(o[:, t * n_seq:(t + 1) * n_seq], False) for t in range(seq_len)]
    y_ref[...] = _silu(g_ref[...]) * (jnp.concatenate(parts, axis=1) * gain)


def _t_specs(n, n_seq):
    head_rows = lambda section: pl.BlockSpec((HEAD, n), lambda h, s=section: (N_HEADS * s + h, 0))
    per_head = pl.BlockSpec((HEAD, n_seq), lambda h: (h, 0))
    state_in = lambda layer: pl.BlockSpec((None, 1, HEAD, HEAD, n_seq), lambda h: (layer, h, 0, 0, 0))
    state_out = ((1, HEAD, HEAD, n_seq), lambda h: (h, 0, 0, 0))
    return head_rows, per_head, state_in, state_out


def _ret_t(u_t, cos_t, sin_t, gam, gain_b, state_t, n_seq, seq_len, layer, prev):
    n = u_t.shape[1]
    head_rows, per_head, state_in, state_out = _t_specs(n, n_seq)
    return _stack_call(
        functools.partial(_ret_t_kernel, n_seq=n_seq, seq_len=seq_len), prev, {1: state_out}, layer,
        in_specs=[head_rows(0), head_rows(1), head_rows(2), head_rows(3), _const_spec(cos_t.shape),
                  _const_spec(sin_t.shape), pl.BlockSpec((1, 1, n_seq), lambda h: (h, 0, 0)), per_head,
                  state_in(layer)],
        args=(u_t, u_t, u_t, u_t, cos_t, sin_t, gam, gain_b, state_t),
        grid=(N_HEADS,),
        out_specs=[pl.BlockSpec((HEAD, n), lambda h: (h, 0)), None],
        out_shape=[jax.ShapeDtypeStruct((GROUP_W, n), F32), jax.ShapeDtypeStruct(state_t.shape, F32)],
        scratch_shapes=[pltpu.VMEM((HEAD, n), F32)] * 2,
        compiler_params=_params(("parallel",)),
        name="ret_sample_t",
    )


def _hg_t(u_t, lb_b, gain_b, state_t, n_seq, seq_len, layer, prev):
    n = u_t.shape[1]
    head_rows, per_head, state_in, state_out = _t_specs(n, n_seq)
    return _stack_call(
        functools.partial(_hg_t_kernel, n_seq=n_seq, seq_len=seq_len), prev, {1: state_out}, layer,
        in_specs=[head_rows(0), head_rows(1), head_rows(2), head_rows(3), per_head, per_head, state_in(layer)],
        args=(u_t, u_t, u_t, u_t, lb_b, gain_b, state_t),
        grid=(N_HEADS,),
        out_specs=[pl.BlockSpec((HEAD, n), lambda h: (h, 0)), None],
        out_shape=[jax.ShapeDtypeStruct((GROUP_W, n), F32), jax.ShapeDtypeStruct(state_t.shape, F32)],
        scratch_shapes=[pltpu.VMEM((HEAD, n), F32)] * 3,
        compiler_params=_params(("parallel",)),
        name="hgrn_sample_t",
    )


def _rope_tables_t(pos0, seq_len, n_seq):
    half = HEAD // 2
    pos = pos0 + jnp.arange(seq_len, dtype=F32)
    inv = ROPE_BASE ** (-jnp.arange(half, dtype=F32) / half)
    ang = pos[:, None] * inv
    cos = jnp.tile(jnp.cos(ang).T, (2, 1))
    sin = jnp.sin(ang).T
    sin_signed = jnp.concatenate([-sin, sin], axis=0)
    return jnp.repeat(cos, n_seq, axis=1), jnp.repeat(sin_signed, n_seq, axis=1)


def _block_diag(w):
    h, i, j = w.shape
    eye = jnp.eye(h, dtype=w.dtype)
    return (eye[:, None, :, None] * w[:, :, None, :]).reshape(h * i, h * j)


def _rope_tables(pos0, seq_len):
    half = HEAD // 2
    pos = pos0 + jnp.arange(seq_len, dtype=F32)
    inv = ROPE_BASE ** (-jnp.arange(half, dtype=F32) / half)
    ang = pos[:, None] * inv
    cos = jnp.tile(jnp.cos(ang), (1, 2 * N_HEADS))
    sin = jnp.sin(ang)
    sin_signed = jnp.concatenate([jnp.tile(-sin, (1, N_HEADS)), jnp.tile(sin, (1, N_HEADS))], axis=-1)
    return cos, sin_signed


def _layer_weights(P, l):
    row = lambda a: a.reshape(1, -1)
    pad128 = lambda a: jnp.pad(a, (0, 128 - a.shape[0])).reshape(1, 128)
    lb_sm = jax.nn.softmax(P['hg_lb'].astype(F32), axis=0)
    lb = (jnp.cumsum(lb_sm, axis=0) - lb_sm[0])[l]
    return dict(
        w_in=P['w_in_prepped'], w_in_t=P['w_in_t'],
        ln_mix_pre=row(P['ln_mix_pre'][l]), ln_mix_post=row(P['ln_mix_post'][l]),
        ln_xa_pre=row(P['ln_xa_pre'][l]), ln_xa_post=row(P['ln_xa_post'][l]),
        ln_ffn_pre=row(P['ln_ffn_pre'][l]), ln_ffn_post=row(P['ln_ffn_post'][l]),
        ssd_cw=P['ssd_conv_w'][l], ssd_cb=row(P['ssd_conv_b'][l]),
        ssd_dtb=pad128(P['ssd_dt_bias'][l]), ssd_alog=pad128(P['ssd_A_log'][l]),
        ssd_d=row(jnp.repeat(P['ssd_D'][l], HEAD)), ssd_norm=row(P['ssd_norm'][l]),
        ret_norm=row(P['ret_norm'][l]),
        rg_cw=P['rg_conv_w'][l], rg_cb=row(P['rg_conv_b'][l]),
        rg_wa=_block_diag(P['rg_wa'][l]).astype(BF16), rg_ba=row(P['rg_ba'][l]),
        rg_wx=_block_diag(P['rg_wx'][l]).astype(BF16), rg_bx=row(P['rg_bx'][l]),
        rg_lam=row(P['rg_lambda'][l]),
        hg_lb=row(lb), hg_norm=row(P['hg_norm'][l]),
        w_out=P['w_out_bf16'], w_xq=P['w_xq_bf16'], w_xo=P['w_xo_bf16'], w_gu=P['w_gu_bf16'],
        w_down=P['w_down_bf16'], ln_mem=row(P['ln_mem'][l]), w_xkv=P['w_xkv_bf16'],
    )


def _layer(x, W, k_mem, v_mem, st, n_seq, seq_len, pos0, layer, prev=None):
    if st is None:
        u_ssd, u_ret, u_rg, u_hg, u_dt = _norm_matmul(
            x, W['ln_mix_pre'], W['w_in'], (IN_COLS_SSD, IN_COLS_RET_SPLIT, IN_COLS_RG, IN_COLS_HG, IN_COLS_DT),
            "in_proj", layer=layer)
        cos, sin_signed = _rope_tables(pos0, seq_len)
        y_ssd, s_ssd, b_ssd = _ssd(u_ssd, u_dt, None, None, W['ssd_cw'], W['ssd_cb'], W['ssd_dtb'],
                                   W['ssd_alog'], W['ssd_d'], W['ssd_norm'], n_seq, seq_len, layer)
        y_ret, s_ret = _ret(u_ret, cos, sin_signed, W['ret_norm'], n_seq, seq_len)
        y_rg, h_rg, b_rg = _rg(u_rg, None, None, W['rg_cw'], W['rg_cb'], W['rg_wa'], W['rg_ba'],
                               W['rg_wx'], W['rg_bx'], W['rg_lam'], n_seq, seq_len)
        h_rg = h_rg.reshape(n_seq, GROUP_W)
        y_hg, s_hg = _hg(u_hg, W['hg_lb'], W['hg_norm'], n_seq, seq_len)
    else:
        ssd_s, ssd_buf, ret_s, rg_h, rg_buf, hg_s = st
        u_ssd, u_rg, u_dt = _norm_matmul(x, W['ln_mix_pre'], W['w_in'], (IN_COLS_SSD, IN_COLS_RG, IN_COLS_DT),
                                         "in_proj", layer=layer)
        to_tb = lambda a: a.reshape(n_seq, seq_len, -1).transpose(1, 0, 2).reshape(n_seq * seq_len, -1)
        from_t = lambda a: a.reshape(-1, seq_len, n_seq).transpose(2, 1, 0).reshape(n_seq * seq_len, -1)
        bcast = lambda g: jnp.broadcast_to(g.reshape(-1, 1), (g.size, n_seq))
        ut_ret, ut_hg = _in_proj_t(to_tb(x), W['ln_mix_pre'], W['w_in_t'], layer)
        cos_t, sin_t = _rope_tables_t(pos0, seq_len, n_seq)
        log_gamma = jnp.log(1.0 - jnp.exp2(-5.0 - jnp.arange(N_HEADS, dtype=F32)))
        gam = jnp.broadcast_to(jnp.exp(log_gamma)[:, None, None], (N_HEADS, 1, n_seq))
        yt_ret, s_ret = _ret_t(ut_ret, cos_t, sin_t, gam, bcast(W['ret_norm']), ret_s, n_seq, seq_len, layer,
                               prev and (prev[1],))
        yt_hg, s_hg = _hg_t(ut_hg, bcast(W['hg_lb']), bcast(W['hg_norm']), hg_s, n_seq, seq_len, layer,
                            prev and (prev[2],))
        y_ret, y_hg = from_t(yt_ret), from_t(yt_hg)
        pad_rows = lambda b: jnp.pad(b, ((0, 0), (0, 1), (0, 0))).reshape(n_seq * 4, b.shape[-1])
        y_ssd, s_ssd, b_ssd = _ssd(u_ssd, u_dt, ssd_s, pad_rows(ssd_buf), W['ssd_cw'], W['ssd_cb'],
                                   W['ssd_dtb'], W['ssd_alog'], W['ssd_d'], W['ssd_norm'], n_seq, seq_len, layer,
                                   prev and (prev[0],))
        b_ssd = b_ssd.reshape(n_seq, 4, -1)[:, :CONV_W - 1]
        y_rg, h_rows, b_rg = _rg(u_rg, jnp.repeat(rg_h, seq_len, axis=0), pad_rows(rg_buf), W['rg_cw'],
                                 W['rg_cb'], W['rg_wa'], W['rg_ba'], W['rg_wx'], W['rg_bx'], W['rg_lam'],
                                 n_seq, seq_len)
        h_rg = h_rows.reshape(n_seq, seq_len, GROUP_W)[:, seq_len - 1]
        b_rg = b_rg.reshape(n_seq, 4, -1)[:, :CONV_W - 1]
    ys = (y_ssd, y_ret, y_rg, y_hg)
    if st is None:
        x3 = _xattn_ffn(ys, x, W['w_out'], W['ln_mix_post'], k_mem, v_mem, W['ln_xa_pre'], W['w_xq'], W['w_xo'],
                        W['ln_xa_post'], W['ln_ffn_pre'], W['w_gu'], W['w_down'], W['ln_ffn_post'], seq_len,
                        layer)
    else:
        x1, q = _out_proj(ys, x, W['w_out'], W['ln_mix_post'], W['ln_xa_pre'], W['w_xq'], layer)
        o = _attention_sample(q, k_mem, v_mem, layer, seq_len)
        x3 = _ffn(o, x1, W['w_xo'], W['ln_xa_post'], W['ln_ffn_pre'], W['w_gu'], W['w_down'],
                  W['ln_ffn_post'], layer)
    return x3, (s_ssd, b_ssd, s_ret, h_rg, b_rg, s_hg)


def kernel(x_prompt, x_sample, state_ssd, state_ssd_conv, state_ret, state_rglru, state_rglru_conv, state_hgrn, cache_mem_k, cache_mem_v, mem_prompt, ln_mix_pre, ln_mix_post, ln_xa_pre, ln_xa_post, ln_ffn_pre, ln_ffn_post, w_in, ssd_conv_w, ssd_conv_b, ssd_dt_bias, ssd_A_log, ssd_D, ssd_norm, ret_norm, rg_conv_w, rg_conv_b, rg_wa, rg_ba, rg_wx, rg_bx, rg_lambda, hg_lb, hg_norm, w_out, ln_mem, w_xq, w_xkv, w_xo, w_gu, w_down):
    P = dict(ln_mix_pre=ln_mix_pre, ln_mix_post=ln_mix_post, ln_xa_pre=ln_xa_pre, ln_xa_post=ln_xa_post,
             ln_ffn_pre=ln_ffn_pre, ln_ffn_post=ln_ffn_post, w_in=w_in, ssd_conv_w=ssd_conv_w,
             ssd_conv_b=ssd_conv_b, ssd_dt_bias=ssd_dt_bias, ssd_A_log=ssd_A_log, ssd_D=ssd_D,
             ssd_norm=ssd_norm, ret_norm=ret_norm, rg_conv_w=rg_conv_w, rg_conv_b=rg_conv_b,
             rg_wa=rg_wa, rg_ba=rg_ba, rg_wx=rg_wx, rg_bx=rg_bx, rg_lambda=rg_lambda, hg_lb=hg_lb,
             hg_norm=hg_norm, w_out=w_out, ln_mem=ln_mem, w_xq=w_xq, w_xkv=w_xkv, w_xo=w_xo,
             w_gu=w_gu, w_down=w_down)
    P['w_in_prepped'], P['w_in_t'] = _in_weight_prep(w_in)
    for name in ('w_out', 'w_xq', 'w_xo', 'w_gu', 'w_down', 'w_xkv'):
        P[name + '_bf16'] = P[name].astype(BF16)
    ssd_t = jnp.swapaxes(state_ssd, -1, -2)
    ret_t = jnp.transpose(state_ret, (0, 2, 3, 4, 1))
    hgrn_t = jnp.transpose(state_hgrn, (0, 2, 3, 4, 1))
    depth = w_in.shape[0]
    bp, tp, d = x_prompt.shape
    bs, ts, _ = x_sample.shape
    n_mem = mem_prompt.shape[1]
    y_p = x_prompt.reshape(bp * tp, d)
    y_s = x_sample.reshape(bs * ts, d)
    mem = mem_prompt.reshape(bp * n_mem, d)
    p_st, s_st = [], []
    mem_prev = samp_prev = None
    for l in range(depth):
        W = _layer_weights(P, l)
        mk, mv, mk5, mv5 = _mem_kv(mem, W['ln_mem'], W['w_xkv'], l, n_mem, mem_prev)
        mem_prev = (mk5, mv5)
        mk = mk.reshape(bp, n_mem, d)
        mv = mv.reshape(bp, n_mem, d)
        y_p, st_p = _layer(y_p, W, mk, mv, None, bp, tp, 0.0, l)
        p_st.append(st_p)
        st_l = (ssd_t, state_ssd_conv[l], ret_t, state_rglru[l], state_rglru_conv[l], hgrn_t)
        y_s, st_s = _layer(y_s, W, cache_mem_k, cache_mem_v, st_l, bs, ts, float(PAST_LEN), l, samp_prev)
        samp_prev = (st_s[0], st_s[2], st_s[5])
        s_st.append(st_s)
    stack = lambda sts, i: jnp.stack([s[i] for s in sts], axis=0)
    return (y_p.reshape(bp, tp, d), y_s.reshape(bs, ts, d),
            jnp.swapaxes(stack(p_st, 0), -1, -2), stack(p_st, 1), stack(p_st, 2), stack(p_st, 3), stack(p_st, 4),
            stack(p_st, 5),
            mem_prev[0], mem_prev[1],
            jnp.swapaxes(samp_prev[0], -1, -2), stack(s_st, 1), jnp.transpose(samp_prev[1], (0, 4, 1, 2, 3)),
            stack(s_st, 3), stack(s_st, 4), jnp.transpose(samp_prev[2], (0, 4, 1, 2, 3)))
```

```python
import functools
import math

import jax
import jax.numpy as jnp
from jax import lax
from jax.experimental import pallas as pl
from jax.experimental.pallas import tpu as pltpu

F32 = jnp.float32
BF16 = jnp.bfloat16
EPS = 1e-6
LOG2_E = 1.4426950408889634

GROUP_W = 256
HEAD = 64
N_HEADS = 4
SSD_N = 128
SSD_G = 2
CONV_W = 4
XA_H = 4
XA_HD = 256
ROPE_BASE = 10000.0
RG_C = 8.0
CHUNK = 64
PAST_LEN = 16384

ROW_TILE = 512
IN_TILE = 1024
MIX_TILE = 1024
RG_TILE = 1024
SAMPLE_SEQS = 16
ATT_SEQS = 8
FF_TILE = 256
VMEM_LIMIT = 56 * 1024 * 1024


def _bdot(a, b):
    return jnp.dot(a.astype(BF16), b.astype(BF16), preferred_element_type=F32)


def _bdot_nt(a, b):
    return lax.dot_general(a.astype(BF16), b.astype(BF16), (((1,), (1,)), ((), ())),
                           preferred_element_type=F32)


def _bdot_tn(a, b):
    return lax.dot_general(a.astype(BF16), b.astype(BF16), (((0,), (0,)), ((), ())),
                           preferred_element_type=F32)


def _split3(x):
    hi = x.astype(BF16)
    r = x - hi.astype(F32)
    mid = r.astype(BF16)
    lo = (r - mid.astype(F32)).astype(BF16)
    return hi, mid, lo


def _sel_dot(sel, x):
    hi, mid, lo = _split3(x)
    d = lambda y: jnp.dot(sel, y, preferred_element_type=F32)
    return (d(hi) + d(mid)) + d(lo)


def _dot_sel(x, sel):
    hi, mid, lo = _split3(x)
    d = lambda y: jnp.dot(y, sel, preferred_element_type=F32)
    return (d(hi) + d(mid)) + d(lo)


def _sel_dot_nt(sel, x):
    hi, mid, lo = _split3(x)
    d = lambda y: lax.dot_general(sel, y, (((1,), (1,)), ((), ())), preferred_element_type=F32)
    return (d(hi) + d(mid)) + d(lo)


def _rms(x, g):
    return x * lax.rsqrt(jnp.mean(x * x, axis=-1, keepdims=True) + EPS) * g


def _sigmoid(x):
    return jax.nn.sigmoid(x)


def _silu(x):
    return x * jax.nn.sigmoid(x)


def _softplus(x):
    return jnp.maximum(x, 0.0) + jnp.log1p(jnp.exp(-jnp.abs(x)))


def _gelu_tanh(x):
    c = math.sqrt(2.0 / math.pi)
    return 0.5 * x * (1.0 + jnp.tanh(c * (x + 0.044715 * (x * x * x))))


def _iota(shape, dim):
    return lax.broadcasted_iota(jnp.int32, shape, dim)


def _seg_causal_mask(n, seg_shift):
    r = _iota((n, n), 0)
    c = _iota((n, n), 1)
    return ((r >> seg_shift) == (c >> seg_shift)) & (c <= r)


def _seg_cumsum(x, seg_shift):
    rows = x.shape[0]
    blk = min(rows, 256)
    mask = _seg_causal_mask(blk, seg_shift).astype(BF16)
    return jnp.concatenate([_sel_dot(mask, x[i:i + blk]) for i in range(0, rows, blk)], axis=0)


def _head_expand_mat(width_in=128):
    r = _iota((width_in, GROUP_W), 0)
    c = _iota((width_in, GROUP_W), 1)
    return ((c >> 6) == r).astype(BF16)


def _head_block_mask():
    return (_iota((GROUP_W, GROUP_W), 0) >> 6) == (_iota((GROUP_W, GROUP_W), 1) >> 6)


def _block_diag_rows(x, mask01):
    return jnp.concatenate([x.astype(BF16)] * N_HEADS, axis=0) * mask01


def _head_stat(y, center):
    ones_bd = _head_block_mask().astype(BF16)
    if center:
        y = y - _dot_sel(y, ones_bd) * (1.0 / HEAD)
    return y * lax.rsqrt(_dot_sel(y * y, ones_bd) * (1.0 / HEAD) + EPS)


def _conv_taps_carry(x, xx_ref, w_ref, b_ref, first):
    rows = x.shape[0]

    @pl.when(first)
    def _():
        xx_ref[0:8, :] = jnp.zeros((8, x.shape[1]), F32)

    xx_ref[8:8 + rows, :] = x
    y = b_ref[...] + w_ref[CONV_W - 1:CONV_W, :] * x
    for j in range(CONV_W - 1):
        y = y + w_ref[j:j + 1, :] * xx_ref[pl.ds(8 - (CONV_W - 1) + j, rows), :]
    xx_ref[0:8, :] = xx_ref[rows:rows + 8, :]
    return y


def _conv_taps_seq4(x, buf4, w_ref, b_ref):
    rows = x.shape[0]
    r = _iota((rows, rows), 0)
    c = _iota((rows, rows), 1)
    t = r & 3
    y = b_ref[...] + w_ref[CONV_W - 1:CONV_W, :] * x
    for d in range(1, CONV_W):
        shift = ((c == r - d) & (t >= d)).astype(BF16)
        hist = ((t < d) & (c == (r - t) + (3 + t - d))).astype(BF16)
        y = y + w_ref[CONV_W - 1 - d:CONV_W - d, :] * (_sel_dot(shift, x) + _sel_dot(hist, buf4))
    new_sel = (((r & 3) < 3) & (c == r + 1)).astype(BF16)
    return y, _sel_dot(new_sel, x)


def _norm_matmul_kernel(x_ref, g_ref, w_ref, *o_refs, col_ranges):
    h = _rms(x_ref[...], g_ref[...]).astype(BF16)
    for o_ref, pieces in zip(o_refs, col_ranges):
        off = 0
        for a, b in pieces:
            o_ref[:, off:off + b - a] = jnp.dot(h, w_ref[:, a:b], preferred_element_type=F32)
            off += b - a


def _const_spec(shape):
    nd = len(shape)
    return pl.BlockSpec(shape, lambda *_: (0,) * nd, pipeline_mode=pl.Buffered(1))


def _layer_spec(shape, layer):
    nd = len(shape)
    return pl.BlockSpec((None,) + tuple(shape[1:]), lambda *_: (layer,) + (0,) * (nd - 1),
                        pipeline_mode=pl.Buffered(1))


def _stack_call(kernel, prevs, stacked, layer, *, in_specs, args, out_specs, out_shape, **kw):
    n_in = len(in_specs)
    out_specs = list(out_specs)
    order = sorted(stacked)
    for o in order:
        blk, idx = stacked[o]
        if prevs is None:
            depth = out_shape[o].shape[0]
            out_specs[o] = pl.BlockSpec((depth,) + tuple(blk), lambda *g, idx=idx: (0,) + tuple(idx(*g)))
        else:
            out_specs[o] = pl.BlockSpec((None,) + tuple(blk), lambda *g, idx=idx: (layer,) + tuple(idx(*g)))
    if prevs is None:
        def body(*refs):
            refs = list(refs)
            for o in order:
                full = refs[n_in + o]
                for d in range(full.shape[0]):
                    if d != layer:
                        full[d] = jnp.zeros(full.shape[1:], full.dtype)
                refs[n_in + o] = full.at[layer]
            return kernel(*refs)

        return pl.pallas_call(body, in_specs=in_specs, out_specs=out_specs, out_shape=out_shape, **kw)(*args)

    def body(*refs):
        return kernel(*refs[:n_in], *refs[n_in + len(prevs):])

    return pl.pallas_call(
        body,
        in_specs=list(in_specs) + [pl.BlockSpec(memory_space=pl.ANY)] * len(prevs),
        out_specs=out_specs, out_shape=out_shape,
        input_output_aliases={n_in + j: o for j, o in enumerate(order)},
        **kw)(*args, *prevs)


def _params(sem):
    return pltpu.CompilerParams(dimension_semantics=sem, vmem_limit_bytes=VMEM_LIMIT)


def _in_weight_prep_kernel(wt_ref, o_ref, ot_ref, *, blocks, dt_block, dt_row, dt_n, t_rows):
    off = 0
    for src, n in t_rows:
        ot_ref[off:off + n, :] = wt_ref[pl.ds(src, n), :].astype(BF16)
        off += n
    for j, pieces in enumerate(blocks):
        if j == dt_block:
            dt = wt_ref[pl.ds(dt_row, 8), :]
            dt = jnp.where(_iota(dt.shape, 0) < dt_n, dt, 0.0)
            blk = jnp.concatenate([dt, jnp.zeros((120, dt.shape[1]), F32)], axis=0)
        else:
            blk = jnp.concatenate([wt_ref[pl.ds(src, n), :] for src, n in pieces], axis=0)
        o_ref[:, j * 128:(j + 1) * 128] = blk.T.astype(BF16)


IN_COLS_SSD = ((0, 1024),)
IN_COLS_RG = ((2048, 2560),)
IN_COLS_HG = ((2560, 3584),)
IN_COLS_DT = ((3584, 3712),)
IN_COLS_RET_SPLIT = ((3712, 4224), (1536, 2048))


def _in_weight_prep(w_in):
    depth, d, d_in = w_in.shape
    z0 = GROUP_W + (GROUP_W + 2 * SSD_G * SSD_N)
    d0 = z0 + N_HEADS
    blocks = [((src, 128),) for src in tuple(range(0, z0, 128)) + tuple(range(d0, d_in, 128))]
    dt_block = len(blocks)
    blocks.append(())
    half = HEAD // 2
    for base in (d0, d0 + GROUP_W):
        for j in range(2):
            blocks.append(tuple((base + h * HEAD + j * half, half) for h in range(N_HEADS)))
    n_out = len(blocks) * 128
    t_rows = ((d0, 4 * GROUP_W), (d0 + 6 * GROUP_W, 4 * GROUP_W))
    n_t = sum(n for _, n in t_rows)
    wt = jnp.swapaxes(w_in, 1, 2)
    return pl.pallas_call(
        functools.partial(_in_weight_prep_kernel, blocks=tuple(blocks), dt_block=dt_block, dt_row=z0,
                          dt_n=N_HEADS, t_rows=t_rows),
        grid=(depth,),
        in_specs=[pl.BlockSpec((None, d_in, d), lambda l: (l, 0, 0), pipeline_mode=pl.Buffered(1))],
        out_specs=[pl.BlockSpec((None, d, n_out), lambda l: (l, 0, 0)),
                   pl.BlockSpec((None, n_t, d), lambda l: (l, 0, 0))],
        out_shape=[jax.ShapeDtypeStruct((depth, d, n_out), BF16), jax.ShapeDtypeStruct((depth, n_t, d), BF16)],
        compiler_params=_params(("parallel",)),
        name="in_weight_prep",
    )(wt)


def _in_proj_t_kernel(x_ref, g_ref, wt_ref, *o_refs):
    h = _rms(x_ref[...], g_ref[...]).astype(BF16)
    off = 0
    for o_ref in o_refs:
        f = o_ref.shape[0]
        o_ref[...] = lax.dot_general(wt_ref[off:off + f, :], h, (((1,), (1,)), ((), ())),
                                     preferred_element_type=F32)
        off += f


def _in_proj_t(x, g, wt, layer):
    n, d = x.shape
    f = wt.shape[1] // 2
    return pl.pallas_call(
        _in_proj_t_kernel,
        grid=(1,),
        in_specs=[_const_spec((n, d)), _const_spec((1, d)), _layer_spec(wt.shape, layer)],
        out_specs=[_const_spec((f, n))] * 2,
        out_shape=[jax.ShapeDtypeStruct((f, n), F32)] * 2,
        compiler_params=_params(("arbitrary",)),
        name="in_proj_t",
    )(x, g, wt)


def _mem_kv_kernel(x_ref, g_ref, w_ref, k_ref, v_ref, k5_ref, v5_ref):
    h = _rms(x_ref[...], g_ref[...]).astype(BF16)
    d = x_ref.shape[1]
    k = jnp.dot(h, w_ref[:, 0:d], preferred_element_type=F32)
    v = jnp.dot(h, w_ref[:, d:2 * d], preferred_element_type=F32)
    k_ref[...] = k
    v_ref[...] = v
    k5_ref[...] = k.reshape(k5_ref.shape)
    v5_ref[...] = v.reshape(v5_ref.shape)


def _mem_kv(mem, g, w, layer, n_mem, prev):
    n, d = mem.shape
    depth = w.shape[0]
    tm = min(ROW_TILE, n)
    seqs = tm // n_mem
    row = pl.BlockSpec((tm, d), lambda i: (i, 0))
    blk5 = ((seqs, n_mem, XA_H, XA_HD), lambda i: (i, 0, 0, 0))
    shape5 = jax.ShapeDtypeStruct((depth, n // n_mem, n_mem, XA_H, XA_HD), F32)
    return _stack_call(
        _mem_kv_kernel, prev, {2: blk5, 3: blk5}, layer,
        in_specs=[row, _const_spec((1, d)), _wspec(w, layer)],
        args=(mem, g, w),
        grid=(n // tm,),
        out_specs=[row, row, None, None],
        out_shape=[jax.ShapeDtypeStruct((n, d), F32)] * 2 + [shape5] * 2,
        compiler_params=_params(("parallel",)),
        name="mem_kv",
    )


def _wspec(w, layer):
    return _layer_spec(w.shape, layer) if w.ndim == 3 else _const_spec(w.shape)


def _norm_matmul(x, g, w, col_ranges, name, layer=None):
    n, d = x.shape
    tm = min(IN_TILE, n)
    w_spec = _wspec(w, layer)
    widths = [sum(b - a for a, b in pieces) for pieces in col_ranges]
    return pl.pallas_call(
        functools.partial(_norm_matmul_kernel, col_ranges=col_ranges),
        grid=(n // tm,),
        in_specs=[pl.BlockSpec((tm, d), lambda i: (i, 0)), _const_spec((1, d)), w_spec],
        out_specs=[pl.BlockSpec((tm, wd), lambda i: (i, 0)) for wd in widths],
        out_shape=[jax.ShapeDtypeStruct((n, wd), F32) for wd in widths],
        compiler_params=_params(("parallel",)),
        name=name,
    )(x, g, w)


def _out_proj_kernel(ys_ref, yr_ref, yg_ref, yh_ref, x_ref, wo_ref, gpost_ref, gpre_ref, wq_ref, x1_ref, q_ref):
    x1 = _mix_out(ys_ref, yr_ref, yg_ref, yh_ref, x_ref, wo_ref, gpost_ref)
    x1_ref[...] = x1
    q_ref[...] = _bdot(_rms(x1, gpre_ref[...]), wq_ref[...])


def _out_proj(ys, x, w_out, g_post, g_pre, w_xq, layer):
    n, d = x.shape
    tm = min(ROW_TILE, n)
    row = lambda wd: pl.BlockSpec((tm, wd), lambda i: (i, 0))
    vec = _const_spec((1, d))
    return pl.pallas_call(
        _out_proj_kernel,
        grid=(n // tm,),
        in_specs=[row(GROUP_W)] * 4 + [row(d), _wspec(w_out, layer), vec, vec, _wspec(w_xq, layer)],
        out_specs=[row(d)] * 2,
        out_shape=[jax.ShapeDtypeStruct((n, d), F32)] * 2,
        compiler_params=_params(("parallel",)),
        name="out_proj",
    )(*ys, x, w_out, g_post, g_pre, w_xq)


def _ffn_tail(o, x1, wxo_ref, gxa_ref, gpre_ref, wgu_ref, wdn_ref, gpost_ref, d_ff):
    a = _bdot(o, wxo_ref[...])
    x2 = x1 + _rms(a, gxa_ref[...])
    h = _rms(x2, gpre_ref[...]).astype(BF16)
    acc = jnp.zeros(x2.shape, F32)
    for j in range(0, d_ff, FF_TILE):
        g = jnp.dot(h, wgu_ref[:, j:j + FF_TILE], preferred_element_type=F32)
        u = jnp.dot(h, wgu_ref[:, d_ff + j:d_ff + j + FF_TILE], preferred_element_type=F32)
        act = (_silu(g) * u).astype(BF16)
        acc = acc + jnp.dot(act, wdn_ref[j:j + FF_TILE, :], preferred_element_type=F32)
    return x2 + _rms(acc, gpost_ref[...])


def _ffn_kernel(o_ref, x1_ref, wxo_ref, gxa_ref, gpre_ref, wgu_ref, wdn_ref, gpost_ref, x3_ref, *, d_ff):
    x3_ref[...] = _ffn_tail(o_ref[...], x1_ref[...], wxo_ref, gxa_ref, gpre_ref, wgu_ref, wdn_ref, gpost_ref,
                            d_ff)


def _ffn(o, x1, w_xo, g_xa, g_pre, w_gu, w_down, g_post, layer):
    n, d = x1.shape
    d_ff = w_down.shape[-2]
    tm = min(ROW_TILE, n)
    row = pl.BlockSpec((tm, d), lambda i: (i, 0))
    vec = _const_spec((1, d))
    return pl.pallas_call(
        functools.partial(_ffn_kernel, d_ff=d_ff),
        grid=(n // tm,),
        in_specs=[row, row, _wspec(w_xo, layer), vec, vec, _wspec(w_gu, layer), _wspec(w_down, layer), vec],
        out_specs=row,
        out_shape=jax.ShapeDtypeStruct((n, d), F32),
        compiler_params=_params(("parallel",)),
        name="attn_out_ffn",
    )(o, x1, w_xo, g_xa, g_pre, w_gu, w_down, g_post)


def _mix_out(ys_ref, yr_ref, yg_ref, yh_ref, x_ref, wo_ref, gpost_ref):
    mix = _bdot(ys_ref[...], wo_ref[0:GROUP_W, :])
    mix = mix + _bdot(yr_ref[...], wo_ref[GROUP_W:2 * GROUP_W, :])
    mix = mix + _bdot(yg_ref[...], wo_ref[2 * GROUP_W:3 * GROUP_W, :])
    mix = mix + _bdot(yh_ref[...], wo_ref[3 * GROUP_W:4 * GROUP_W, :])
    return x_ref[...] + _rms(mix, gpost_ref[...])


def _xattn_ffn_kernel(ys_ref, yr_ref, yg_ref, yh_ref, x_ref, wo_ref, gmix_ref, k_ref, v_ref, gq_ref, wq_ref,
                      wxo_ref, gxa_ref, gpre_ref, wgu_ref, wdn_ref, gpost_ref, x3_ref, o_scr, *, d_ff):
    x1 = _mix_out(ys_ref, yr_ref, yg_ref, yh_ref, x_ref, wo_ref, gmix_ref)
    q = _bdot(_rms(x1, gq_ref[...]), wq_ref[...])
    scale = XA_HD ** -0.5
    for h in range(XA_H):
        sl = slice(h * XA_HD, (h + 1) * XA_HD)
        s = _bdot_nt(q[:, sl], k_ref[0, :, sl]) * scale
        o_scr[:, sl] = _softmax_pv(s, v_ref[0, :, sl])
    x3_ref[...] = _ffn_tail(o_scr[...], x1, wxo_ref, gxa_ref, gpre_ref, wgu_ref, wdn_ref, gpost_ref, d_ff)


def _xattn_ffn(ys, x, w_out, g_mix, k, v, g_q, w_xq, w_xo, g_xa, g_pre, w_gu, w_down, g_post, seq_len, layer):
    n, d = x.shape
    n_seq, n_mem, _ = k.shape
    d_ff = w_down.shape[-2]
    nt = seq_len // ROW_TILE
    row = lambda wd: pl.BlockSpec((ROW_TILE, wd), lambda b, t: (b * nt + t, 0))
    kv = pl.BlockSpec((1, n_mem, d), lambda b, t: (b, 0, 0))
    vec = _const_spec((1, d))
    return pl.pallas_call(
        functools.partial(_xattn_ffn_kernel, d_ff=d_ff),
        grid=(n_seq, nt),
        in_specs=[row(GROUP_W)] * 4 + [row(d), _wspec(w_out, layer), vec, kv, kv, vec, _wspec(w_xq, layer),
                                       _wspec(w_xo, layer), vec, vec, _wspec(w_gu, layer),
                                       _wspec(w_down, layer), vec],
        out_specs=row(d),
        out_shape=jax.ShapeDtypeStruct((n, d), F32),
        scratch_shapes=[pltpu.VMEM((ROW_TILE, d), F32)],
        compiler_params=_params(("parallel", "parallel")),
        name="xattn_ffn",
    )(*ys, x, w_out, g_mix, k, v, g_q, w_xq, w_xo, g_xa, g_pre, w_gu, w_down, g_post)


def _softmax_pv(s, v):
    m = jnp.max(s, axis=-1, keepdims=True)
    p = jnp.exp(s - m)
    return _bdot(p, v) / jnp.sum(p, axis=-1, keepdims=True)


def _attn_sample_kernel(q_ref, k_ref, v_ref, o_ref, *, seqs, seg):
    scale = XA_HD ** -0.5
    rows = seqs * seg
    n_mem = k_ref.shape[2]
    wq = jnp.concatenate([q_ref[:, h * XA_HD:(h + 1) * XA_HD] for h in range(XA_H)], axis=0)
    wshape = (XA_H * rows, n_mem * XA_H)
    valid = (_iota(wshape, 0) >> (rows.bit_length() - 1)) == (_iota(wshape, 1) & (XA_H - 1))
    row_seq = _iota((rows, XA_HD), 0) >> (seg.bit_length() - 1)
    accs = [jnp.zeros((rows, XA_HD), F32) for _ in range(XA_H)]
    for i in range(seqs):
        k2 = k_ref[0, i].reshape(n_mem * XA_H, XA_HD)
        v2 = v_ref[0, i].reshape(n_mem * XA_H, XA_HD)
        s = jnp.where(valid, _bdot_nt(wq, k2) * scale, -jnp.inf)
        o = _softmax_pv(s, v2)
        for h in range(XA_H):
            accs[h] = jnp.where(row_seq == i, o[h * rows:(h + 1) * rows], accs[h])
    for h in range(XA_H):
        o_ref[:, h * XA_HD:(h + 1) * XA_HD] = accs[h]


def _attention_sample(q, cache_k, cache_v, layer, seq_len):
    n, d = q.shape
    _, n_seq, n_mem, heads, hd = cache_k.shape
    kv = pl.BlockSpec((1, ATT_SEQS, n_mem, heads, hd), lambda i: (layer, i, 0, 0, 0))
    row = pl.BlockSpec((ATT_SEQS * seq_len, d), lambda i: (i, 0))
    return pl.pallas_call(
        functools.partial(_attn_sample_kernel, seqs=ATT_SEQS, seg=seq_len),
        grid=(n_seq // ATT_SEQS,),
        in_specs=[row, kv, kv],
        out_specs=row,
        out_shape=jax.ShapeDtypeStruct((n, d), F32),
        compiler_params=_params(("parallel",)),
        name="xattn_sample",
    )(q, cache_k, cache_v)


def _ssd_intra(q, k, v, cum_x, cum_t, mask):
    sc = _bdot_nt(q, k)
    outs = []
    for hh in range(2):
        col = cum_x[:, hh * HEAD:(hh + 1) * HEAD]
        dec = jnp.exp2(jnp.where(mask, col - cum_t[hh:hh + 1, :], -jnp.inf))
        outs.append(_bdot(sc * dec, v[:, hh * HEAD:(hh + 1) * HEAD]))
    return jnp.concatenate(outs, axis=-1)


def _ssd_common(u, dtraw_ref, dtb_ref, alog_ref, conv, seg_shift):
    rows = u.shape[0]
    xbc = _silu(conv)
    xs = xbc[:, 0:GROUP_W]
    bm = xbc[:, GROUP_W:2 * GROUP_W]
    cm = xbc[:, 2 * GROUP_W:3 * GROUP_W]
    dt = _softplus(dtraw_ref[...] + dtb_ref[...])
    la = (-jnp.exp(alog_ref[...]) * LOG2_E) * dt
    cum = _seg_cumsum(la, seg_shift)
    expand = _head_expand_mat()
    dt_x = _dot_sel(dt, expand)
    cum_x = _dot_sel(cum, expand)
    sel8 = (_iota((8, 128), 0) == _iota((8, 128), 1)).astype(BF16)
    cum_t = _sel_dot_nt(sel8, cum)
    expand_full = ((_iota((128, 4 * 128), 1) >> 7) == _iota((128, 4 * 128), 0)).astype(BF16)
    cum_full = _dot_sel(cum, expand_full)
    return xs, bm, cm, dt_x, cum_x, cum_t, cum_full


def _ssd_finish(y, xs, z, d_ref, g_ref):
    y = y + d_ref[...] * xs
    return _rms(y * _silu(z), g_ref[...])


def _ssd_prompt_kernel(u_ref, dtraw_ref, cw_ref, cb_ref, dtb_ref, alog_ref, d_ref, g_ref,
                       y_ref, s_out_ref, buf_out_ref, s_ref, xx_ref, y_scr):
    t = pl.program_id(1)
    rows = u_ref.shape[0]

    @pl.when(t == 0)
    def _():
        s_ref[...] = jnp.zeros(s_ref.shape, F32)

    u = u_ref[...]
    conv = _conv_taps_carry(u[:, GROUP_W:], xx_ref, cw_ref, cb_ref, t == 0)
    xs, bm, cm, dt_x, cum_x, _, cum_full = _ssd_common(u, dtraw_ref, dtb_ref, alog_ref, conv, 6)
    v = xs * dt_x
    ecum_x = jnp.exp2(cum_x)
    hmask01 = _head_block_mask().astype(BF16)
    gmask01 = ((_iota((GROUP_W, GROUP_W), 0) >> 7) == (_iota((GROUP_W, GROUP_W), 1) >> 7)).astype(BF16)
    wrow = _iota((CHUNK, GROUP_W), 0)
    wcol = _iota((CHUNK, GROUP_W), 1) & (CHUNK - 1)
    n_chunks = rows // CHUNK
    groups = [slice(g * 128, (g + 1) * 128) for g in range(SSD_G)]
    a_wide, ds_t = [], []
    for c in range(n_chunks):
        rs = slice(c * CHUNK, (c + 1) * CHUNK)
        cum_c = cum_x[rs]
        last_x = cum_x[c * CHUNK + CHUNK - 1:(c + 1) * CHUNK, :]
        vend = v[rs] * jnp.exp2(last_x - cum_c)
        cum_row = jnp.sum(jnp.where(wrow == wcol, cum_c, 0.0), axis=0, keepdims=True)
        dec = jnp.exp2(jnp.where(wcol <= wrow, cum_c - cum_row, -jnp.inf))
        a_wide.append(_bdot_nt(cm[rs], _block_diag_rows(bm[rs], gmask01)) * dec)
        ds_t.append([_bdot_tn(vend[:, ls], bm[rs, ls]) for ls in groups])
    s_in = []
    s_cur = [s_ref[ls, :] for ls in groups]
    for c in range(n_chunks):
        last_full = cum_full[c * CHUNK + CHUNK - 1:(c + 1) * CHUNK, :]
        s_in.append([x.astype(BF16) for x in s_cur])
        for g in range(SSD_G):
            decay = jnp.concatenate(
                [jnp.broadcast_to(jnp.exp2(last_full[:, h * 128:(h + 1) * 128]), (HEAD, 128))
                 for h in (2 * g, 2 * g + 1)], axis=0)
            s_cur[g] = decay * s_cur[g] + ds_t[c][g]
    for g, ls in enumerate(groups):
        s_ref[ls, :] = s_cur[g]
    for c in range(n_chunks):
        rs = slice(c * CHUNK, (c + 1) * CHUNK)
        y_intra = _bdot(a_wide[c], _block_diag_rows(v[rs], hmask01))
        y_inter = [_bdot_nt(cm[rs, ls], s_in[c][g]) for g, ls in enumerate(groups)]
        y_scr[rs, :] = y_intra + jnp.concatenate(y_inter, axis=-1) * ecum_x[rs]
    y_ref[...] = _ssd_finish(y_scr[...], xs, u[:, 0:GROUP_W], d_ref, g_ref)

    @pl.when(t == pl.num_programs(1) - 1)
    def _():
        for h in range(N_HEADS):
            s_out_ref[0, h] = s_ref[h * HEAD:(h + 1) * HEAD, :]
        buf_out_ref[0] = xx_ref[pl.ds(8 - (CONV_W - 1), CONV_W - 1), :]


def _ssd_sample_kernel(u_ref, dtraw_ref, s0_ref, buf_ref, cw_ref, cb_ref, dtb_ref, alog_ref, d_ref,
                       g_ref, y_ref, s_out_ref, buf_out_ref, *, seqs):
    rows = u_ref.shape[0]
    u = u_ref[...]
    conv, new_buf = _conv_taps_seq4(u[:, GROUP_W:], buf_ref[...], cw_ref, cb_ref)
    buf_out_ref[...] = new_buf
    mask = _seg_causal_mask(rows, 2)
    xs, bm, cm, dt_x, cum_x, cum_t, cum_full = _ssd_common(u, dtraw_ref, dtb_ref, alog_ref, conv, 2)
    v = xs * dt_x
    ecum_x = jnp.exp2(cum_x)
    pick_last = (_iota((rows, rows), 1) == (_iota((rows, rows), 0) | 3)).astype(BF16)
    last_x = _sel_dot(pick_last, cum_x)
    vend = v * jnp.exp2(last_x - cum_x)
    row_seq = _iota((rows, 128), 0) >> 2
    ys = []
    for g in range(SSD_G):
        ls = slice(g * 128, (g + 1) * 128)
        q = cm[:, ls]
        k = bm[:, ls]
        y_g = _ssd_intra(q, k, v[:, ls], cum_x[:, ls], cum_t[2 * g:2 * g + 2, :], mask)
        y_int = jnp.zeros((rows, 128), F32)
        for i in range(seqs):
            in_seq = row_seq == i
            s_prev = jnp.concatenate([s0_ref[i, 2 * g], s0_ref[i, 2 * g + 1]], axis=0)
            y_int = jnp.where(in_seq, _bdot_nt(q, s_prev), y_int)
            ds = _bdot_tn(jnp.where(in_seq, vend[:, ls], 0.0), k)
            for hh in range(2):
                h = 2 * g + hh
                decay = jnp.exp2(cum_full[4 * i + 3:4 * i + 4, h * 128:(h + 1) * 128])
                s_out_ref[i, h] = decay * s0_ref[i, h] + ds[hh * HEAD:(hh + 1) * HEAD, :]
        ys.append(y_g + y_int * ecum_x[:, ls])
    y_ref[...] = _ssd_finish(jnp.concatenate(ys, axis=-1), xs, u[:, 0:GROUP_W], d_ref, g_ref)


def _ssd(u, dtraw, state, buf, cw, cb, dtb, alog, d_x, gain, n_seq, seq_len, layer, prev=None):
    n = u.shape[0]
    cch = cw.shape[1]
    vec = lambda wd: _const_spec((1, wd))
    common_in = [_const_spec(cw.shape), vec(cch), vec(128), vec(128), vec(GROUP_W), vec(GROUP_W)]
    if state is None:
        nt = seq_len // MIX_TILE
        row = lambda wd: pl.BlockSpec((MIX_TILE, wd), lambda b, t: (b * nt + t, 0))
        return pl.pallas_call(
            _ssd_prompt_kernel,
            grid=(n_seq, nt),
            in_specs=[row(u.shape[1]), row(128)] + common_in,
            out_specs=[row(GROUP_W),
                       pl.BlockSpec((1, N_HEADS, HEAD, SSD_N), lambda b, t: (b, 0, 0, 0)),
                       pl.BlockSpec((1, CONV_W - 1, cch), lambda b, t: (b, 0, 0))],
            out_shape=[jax.ShapeDtypeStruct((n, GROUP_W), F32),
                       jax.ShapeDtypeStruct((n_seq, N_HEADS, HEAD, SSD_N), F32),
                       jax.ShapeDtypeStruct((n_seq, CONV_W - 1, cch), F32)],
            scratch_shapes=[pltpu.VMEM((GROUP_W, SSD_N), F32), pltpu.VMEM((MIX_TILE + 8, cch), F32),
                            pltpu.VMEM((MIX_TILE, GROUP_W), F32)],
            compiler_params=_params(("parallel", "arbitrary")),
            name="ssd_prompt",
        )(u, dtraw, cw, cb, dtb, alog, d_x, gain)
    rows = SAMPLE_SEQS * seq_len
    row = lambda wd: pl.BlockSpec((rows, wd), lambda i: (i, 0))
    st = pl.BlockSpec((None, SAMPLE_SEQS, N_HEADS, HEAD, SSD_N), lambda i: (layer, i, 0, 0, 0))
    return _stack_call(
        functools.partial(_ssd_sample_kernel, seqs=SAMPLE_SEQS), prev,
        {1: ((SAMPLE_SEQS, N_HEADS, HEAD, SSD_N), lambda i: (i, 0, 0, 0))}, layer,
        in_specs=[row(u.shape[1]), row(128), st, row(cch)] + common_in,
        args=(u, dtraw, state, buf, cw, cb, dtb, alog, d_x, gain),
        grid=(n_seq // SAMPLE_SEQS,),
        out_specs=[row(GROUP_W), None, row(cch)],
        out_shape=[jax.ShapeDtypeStruct((n, GROUP_W), F32),
                   jax.ShapeDtypeStruct(state.shape, F32),
                   jax.ShapeDtypeStruct((n, cch), F32)],
        compiler_params=_params(("parallel",)),
        name="ssd_sample",
    )


def _rope_split(x, cos, sin_signed):
    swapped = jnp.concatenate([x[:, 128:], x[:, :128]], axis=1)
    return x * cos + swapped * sin_signed


def _ret_qkv(u_ref, cos_ref, sin_ref):
    u = u_ref[...]
    q = _rope_split(u[:, 0:GROUP_W], cos_ref[...], sin_ref[...])
    k = _rope_split(u[:, GROUP_W:2 * GROUP_W], cos_ref[...], sin_ref[...]) * (HEAD ** -0.5)
    return q, k, u[:, 2 * GROUP_W:3 * GROUP_W], u[:, 3 * GROUP_W:4 * GROUP_W]


def _ret_prompt_kernel(u_ref, cos_ref, sin_ref, ecum_ref, eend_ref, elast_ref, dec_ref, g_ref,
                       y_ref, s_out_ref, s_ref, y_scr):
    t = pl.program_id(1)
    rows = u_ref.shape[0]

    @pl.when(t == 0)
    def _():
        s_ref[...] = jnp.zeros(s_ref.shape, F32)

    q, k, v, gate = _ret_qkv(u_ref, cos_ref, sin_ref)
    r_i = _iota((GROUP_W, GROUP_W), 0)
    c_i = _iota((GROUP_W, GROUP_W), 1)
    kmask01 = ((r_i >> 6) == ((c_i & 127) >> 5)).astype(BF16)
    vmask01 = _head_block_mask().astype(BF16)
    smask = ((r_i & 127) >> 5) == (c_i >> 6)
    chunks = [slice(c * CHUNK, (c + 1) * CHUNK) for c in range(rows // CHUNK)]
    a_wide = [_bdot_nt(q[rs], _block_diag_rows(k[rs], kmask01)) * dec_ref[...] for rs in chunks]
    ds = [jnp.where(smask, _bdot_tn(k[rs], v[rs] * eend_ref[...]), 0.0) for rs in chunks]
    s = s_ref[...]
    s_in = []
    for d in ds:
        s_in.append(s.astype(BF16))
        s = elast_ref[...] * s + d
    s_ref[...] = s
    for rs, a, s_c in zip(chunks, a_wide, s_in):
        y_intra = _bdot(a, _block_diag_rows(v[rs], vmask01))
        y_scr[rs, :] = y_intra + _bdot(q[rs], s_c) * ecum_ref[...]
    y_ref[...] = _silu(gate) * (_head_stat(y_scr[...], True) * g_ref[...])

    @pl.when(t == pl.num_programs(1) - 1)
    def _():
        half = HEAD // 2
        for h in range(N_HEADS):
            vs = slice(h * HEAD, (h + 1) * HEAD)
            s_out_ref[0, h, 0:half, :] = s_ref[h * half:(h + 1) * half, vs]
            s_out_ref[0, h, half:HEAD, :] = s_ref[128 + h * half:128 + (h + 1) * half, vs]


def _ret_tables(block, seg):
    log_gamma = jnp.log(1.0 - jnp.exp2(-5.0 - jnp.arange(N_HEADS, dtype=F32)))
    pos = (jnp.arange(block) % seg).astype(F32)
    cum = (pos[:, None] + 1.0) * log_gamma[None, :]
    last = seg * log_gamma
    rep = lambda a: jnp.repeat(a, HEAD, axis=-1)
    same = (jnp.arange(block)[:, None] // seg) == (jnp.arange(block)[None, :] // seg)
    mask = same & (jnp.arange(block)[None, :] <= jnp.arange(block)[:, None])
    dec = jnp.exp(jnp.where(mask[None], cum.T[:, :, None] - cum.T[:, None, :], -jnp.inf))
    return (rep(jnp.exp(cum)), rep(jnp.exp(last[None, :] - cum)), rep(jnp.exp(last)[None, :]), dec)


def _ret(u, cos, sin_signed, gain, n_seq, seq_len):
    n = u.shape[0]
    vec = _const_spec((1, GROUP_W))
    nt = seq_len // MIX_TILE
    ecum, eend, elast, dec = _ret_tables(CHUNK, CHUNK)
    dec = dec.transpose(1, 0, 2).reshape(CHUNK, N_HEADS * CHUNK)
    row = lambda wd: pl.BlockSpec((MIX_TILE, wd), lambda b, t: (b * nt + t, 0))
    pos = pl.BlockSpec((MIX_TILE, GROUP_W), lambda b, t: (t, 0))
    return pl.pallas_call(
        _ret_prompt_kernel,
        grid=(n_seq, nt),
        in_specs=[row(u.shape[1]), pos, pos, _const_spec(ecum.shape), _const_spec(eend.shape), vec,
                  _const_spec(dec.shape), vec],
        out_specs=[row(GROUP_W), pl.BlockSpec((1, N_HEADS, HEAD, HEAD), lambda b, t: (b, 0, 0, 0))],
        out_shape=[jax.ShapeDtypeStruct((n, GROUP_W), F32),
                   jax.ShapeDtypeStruct((n_seq, N_HEADS, HEAD, HEAD), F32)],
        scratch_shapes=[pltpu.VMEM((GROUP_W, GROUP_W), F32), pltpu.VMEM((MIX_TILE, GROUP_W), F32)],
        compiler_params=_params(("parallel", "arbitrary")),
        name="ret_prompt",
    )(u, cos, sin_signed, ecum, eend, elast, dec, gain)


def _rg_gates(xr, wa_ref, ba_ref, wx_ref, bx_ref, lam_ref):
    r_gate = _sigmoid(_bdot(xr, wa_ref[...]) + ba_ref[...])
    i_gate = _sigmoid(_bdot(xr, wx_ref[...]) + bx_ref[...])
    log_a = (-RG_C * _softplus(-lam_ref[...])) * r_gate
    a = jnp.exp(log_a)
    b = jnp.sqrt(-jnp.tanh(log_a) * (a * a + 1.0)) * (i_gate * xr)
    return a, b


def _rg_scan(a, b, seg):
    pos = _iota(a.shape, 0) & (seg - 1)
    d = 1
    while d < seg:
        ok = pos >= d
        a_sh = jnp.where(ok, pltpu.roll(a, d, 0), 1.0)
        b_sh = jnp.where(ok, pltpu.roll(b, d, 0), 0.0)
        b = a * b_sh + b
        a = a * a_sh
        d *= 2
    return a, b


def _rg_scan_carry(a, b, h0):
    a, b = _rg_scan(a, b, 8)
    outs = []
    carry = h0
    for g in range(a.shape[0] // 8):
        h_g = b[g * 8:(g + 1) * 8] + a[g * 8:(g + 1) * 8] * carry
        outs.append(h_g)
        carry = h_g[7:8]
    return jnp.concatenate(outs, axis=0)


def _rg_prompt_kernel(u_ref, cw_ref, cb_ref, wa_ref, ba_ref, wx_ref, bx_ref, lam_ref,
                      y_ref, h_out_ref, buf_out_ref, h_ref, xx_ref):
    t = pl.program_id(1)
    rows = u_ref.shape[0]

    @pl.when(t == 0)
    def _():
        h_ref[...] = jnp.zeros(h_ref.shape, F32)

    u = u_ref[...]
    xr = _conv_taps_carry(u[:, 0:GROUP_W], xx_ref, cw_ref, cb_ref, t == 0)
    a, b = _rg_gates(xr, wa_ref, ba_ref, wx_ref, bx_ref, lam_ref)
    hseq = _rg_scan_carry(a, b, h_ref[0:1, :])
    h_ref[...] = jnp.broadcast_to(hseq[rows - 1:rows, :], h_ref.shape)
    y_ref[...] = _gelu_tanh(u[:, GROUP_W:]) * hseq

    @pl.when(t == pl.num_programs(1) - 1)
    def _():
        h_out_ref[0] = hseq[rows - 1:rows, :]
        buf_out_ref[0] = xx_ref[pl.ds(8 - (CONV_W - 1), CONV_W - 1), :]


def _rg_sample_kernel(u_ref, h0_ref, buf_ref, cw_ref, cb_ref, wa_ref, ba_ref, wx_ref, bx_ref, lam_ref,
                      y_ref, h_out_ref, buf_out_ref):
    u = u_ref[...]
    xr, new_buf = _conv_taps_seq4(u[:, 0:GROUP_W], buf_ref[...], cw_ref, cb_ref)
    buf_out_ref[...] = new_buf
    a, b = _rg_gates(xr, wa_ref, ba_ref, wx_ref, bx_ref, lam_ref)
    a_cum, h_loc = _rg_scan(a, b, 4)
    hseq = h_loc + a_cum * h0_ref[...]
    h_out_ref[...] = hseq
    y_ref[...] = _gelu_tanh(u[:, GROUP_W:]) * hseq


def _rg(u, h0_rows, buf, cw, cb, wa, ba, wx, bx, lam, n_seq, seq_len):
    n = u.shape[0]
    vec = _const_spec((1, GROUP_W))
    common_in = [_const_spec(cw.shape), vec, _const_spec(wa.shape), vec, _const_spec(wx.shape), vec, vec]
    if h0_rows is None:
        nt = seq_len // RG_TILE
        row = lambda wd: pl.BlockSpec((RG_TILE, wd), lambda b, t: (b * nt + t, 0))
        return pl.pallas_call(
            _rg_prompt_kernel,
            grid=(n_seq, nt),
            in_specs=[row(u.shape[1])] + common_in,
            out_specs=[row(GROUP_W), pl.BlockSpec((1, 1, GROUP_W), lambda b, t: (b, 0, 0)),
                       pl.BlockSpec((1, CONV_W - 1, GROUP_W), lambda b, t: (b, 0, 0))],
            out_shape=[jax.ShapeDtypeStruct((n, GROUP_W), F32),
                       jax.ShapeDtypeStruct((n_seq, 1, GROUP_W), F32),
                       jax.ShapeDtypeStruct((n_seq, CONV_W - 1, GROUP_W), F32)],
            scratch_shapes=[pltpu.VMEM((8, GROUP_W), F32), pltpu.VMEM((RG_TILE + 8, GROUP_W), F32)],
            compiler_params=_params(("parallel", "arbitrary")),
            name="rglru_prompt",
        )(u, cw, cb, wa, ba, wx, bx, lam)
    rows = SAMPLE_SEQS * seq_len
    row = lambda wd: pl.BlockSpec((rows, wd), lambda i: (i, 0))
    return pl.pallas_call(
        _rg_sample_kernel,
        grid=(n_seq // SAMPLE_SEQS,),
        in_specs=[row(u.shape[1]), row(GROUP_W), row(GROUP_W)] + common_in,
        out_specs=[row(GROUP_W)] * 3,
        out_shape=[jax.ShapeDtypeStruct((n, GROUP_W), F32)] * 3,
        compiler_params=_params(("parallel",)),
        name="rglru_sample",
    )(u, h0_rows, buf, cw, cb, wa, ba, wx, bx, lam)


def _hg_inputs(u_ref, lb_ref, seg_shift):
    u = u_ref[...]
    rows = u.shape[0]
    lb = lb_ref[...]
    fg = lb + (1.0 - lb) * _sigmoid(u[:, GROUP_W:2 * GROUP_W])
    q = _silu(u[:, 0:GROUP_W])
    k = 1.0 - fg
    v = u[:, 2 * GROUP_W:3 * GROUP_W]
    cum = _seg_cumsum(jnp.log(fg), seg_shift)
    return q, k, v, cum, u[:, 3 * GROUP_W:4 * GROUP_W]


def _hg_prompt_kernel(u_ref, lb_ref, g_ref, y_ref, s_out_ref, st_ref, y_scr):
    t = pl.program_id(1)
    rows = u_ref.shape[0]

    @pl.when(t == 0)
    def _():
        st_ref[...] = jnp.zeros(st_ref.shape, F32)

    q, k, v, cum, gate = _hg_inputs(u_ref, lb_ref, 6)
    hmask = _head_block_mask()
    hmask01 = hmask.astype(BF16)
    ones_bd = hmask01
    y_scr[...] = jnp.dot((q * k).astype(BF16), ones_bd, preferred_element_type=F32) * v
    row = _iota((rows, GROUP_W), 0)
    trow = _iota((CHUNK, GROUP_W), 0)
    tcol = _iota((CHUNK, GROUP_W), 1) & (CHUNK - 1)
    cum = cum * LOG2_E
    levels = []
    blk_last = cum
    h = 1
    while h < CHUNK:
        upper = (row & h) != 0
        qn = jnp.where(upper, q * jnp.exp2(cum - pltpu.roll(blk_last, h, 0)), 0.0)
        kn = k * jnp.exp2(blk_last - cum)
        sh = (2 * h).bit_length() - 1
        pair = ((trow >> sh) == (tcol >> sh)) & ((tcol & h) == 0)
        levels.append((pair, qn, kn))
        blk_last = jnp.where(upper, blk_last, pltpu.roll(blk_last, rows - h, 0))
        h *= 2
    q_in = q * jnp.exp2(cum)
    k_end = k * jnp.exp2(blk_last - cum)
    n_chunks = rows // CHUNK
    a_wide, ds_t = [], []
    for c in range(n_chunks):
        rs = slice(c * CHUNK, (c + 1) * CHUNK)
        a_c = jnp.zeros((CHUNK, GROUP_W), F32)
        for pair, qn, kn in levels:
            g = _bdot_nt(qn[rs], _block_diag_rows(kn[rs], hmask01))
            a_c = a_c + jnp.where(pair, g, 0.0)
        a_wide.append(a_c)
        ds_t.append(jnp.where(hmask, _bdot_tn(v[rs], k_end[rs]), 0.0))
    st = st_ref[...]
    st_in = []
    for c in range(n_chunks):
        st_in.append(st.astype(BF16))
        st = jnp.exp2(blk_last[c * CHUNK:c * CHUNK + 1, :]) * st + ds_t[c]
    st_ref[...] = st
    for c in range(n_chunks):
        rs = slice(c * CHUNK, (c + 1) * CHUNK)
        y_c = _bdot(a_wide[c], _block_diag_rows(v[rs], hmask01)) + _bdot_nt(q_in[rs], st_in[c])
        y_scr[rs, :] = y_scr[rs, :] + y_c
    y_ref[...] = _silu(gate) * (_head_stat(y_scr[...], False) * g_ref[...])

    @pl.when(t == pl.num_programs(1) - 1)
    def _():
        eye = (_iota((HEAD, HEAD), 0) == _iota((HEAD, HEAD), 1)).astype(BF16)
        for h in range(N_HEADS):
            s_out_ref[0, h] = _sel_dot_nt(eye, st_ref[h * HEAD:(h + 1) * HEAD, h * HEAD:(h + 1) * HEAD])


def _hg(u, lb, gain, n_seq, seq_len):
    n = u.shape[0]
    vec = _const_spec((1, GROUP_W))
    nt = seq_len // MIX_TILE
    row = lambda wd: pl.BlockSpec((MIX_TILE, wd), lambda b, t: (b * nt + t, 0))
    return pl.pallas_call(
        _hg_prompt_kernel,
        grid=(n_seq, nt),
        in_specs=[row(u.shape[1]), vec, vec],
        out_specs=[row(GROUP_W), pl.BlockSpec((1, N_HEADS, HEAD, HEAD), lambda b, t: (b, 0, 0, 0))],
        out_shape=[jax.ShapeDtypeStruct((n, GROUP_W), F32),
                   jax.ShapeDtypeStruct((n_seq, N_HEADS, HEAD, HEAD), F32)],
        scratch_shapes=[pltpu.VMEM((GROUP_W, GROUP_W), F32), pltpu.VMEM((MIX_TILE, GROUP_W), F32)],
        compiler_params=_params(("parallel", "arbitrary")),
        name="hgrn_prompt",
    )(u, lb, gain)


def _t_recurrence(q_scr, k_scr, v, dec_row, dec_scr, s_ref, n_seq, seq_len):
    cols = [slice(t * n_seq, (t + 1) * n_seq) for t in range(seq_len)]
    v_t = [v[:, c] for c in cols]
    outs = [jnp.zeros((HEAD, n_seq), F32) for _ in cols]
    for k in range(HEAD):
        s = s_ref[k]
        for t, c in enumerate(cols):
            dec = dec_row if dec_scr is None else dec_scr[k:k + 1, c]
            s = dec * s + k_scr[k:k + 1, c] * v_t[t]
            outs[t] = outs[t] + q_scr[k:k + 1, c] * s
        s_ref[k] = s
    return jnp.concatenate(outs, axis=1)


def _t_head_norm(o, center):
    if center:
        o = o - jnp.mean(o, axis=0, keepdims=True)
    return o * lax.rsqrt(jnp.mean(o * o, axis=0, keepdims=True) + EPS)


def _ret_t_kernel(q_ref, k_ref, v_ref, g_ref, cos_ref, sin_ref, gam_ref, gain_ref, s0_ref,
                  y_ref, s_ref, q_scr, k_scr, *, n_seq, seq_len):
    half = HEAD // 2

    def rope(x):
        swapped = jnp.concatenate([x[half:], x[:half]], axis=0)
        return x * cos_ref[...] + swapped * sin_ref[...]

    q_scr[...] = rope(q_ref[...])
    k_scr[...] = rope(k_ref[...]) * (HEAD ** -0.5)
    s_ref[0] = s0_ref[0]
    o = _t_recurrence(q_scr, k_scr, v_ref[...], gam_ref[0], None, s_ref.at[0], n_seq, seq_len)
    gain = jnp.concatenate([gain_ref[...]] * seq_len, axis=1)
    parts = [_t_head_norm(o[:, t * n_seq:(t + 1) * n_seq], True) for t in range(seq_len)]
    y_ref[...] = _silu(g_ref[...]) * (jnp.concatenate(parts, axis=1) * gain)


def _hg_t_kernel(q_ref, f_ref, i_ref, g_ref, lb_ref, gain_ref, s0_ref, y_ref, s_ref, q_scr, k_scr, f_scr,
                 *, n_seq, seq_len):
    lb = jnp.concatenate([lb_ref[...]] * seq_len, axis=1)
    fg = lb + (1.0 - lb) * _sigmoid(f_ref[...])
    q_scr[...] = _silu(q_ref[...])
    k_scr[...] = 1.0 - fg
    f_scr[...] = fg
    s_ref[0] = s0_ref[0]
    o = _t_recurrence(q_scr, k_scr, i_ref[...], None, f_scr, s_ref.at[0], n_seq, seq_len)
    gain = jnp.concatenate([gain_ref[...]] * seq_len, axis=1)
    parts = [_t_head_norm(o[:, t * n_seq:(t + 1) * n_seq], False) for t in range(seq_len)]
    y_ref[...] = _silu(g_ref[...]) * (jnp.concatenate(parts, axis=1) * gain)


def _t_specs(n, n_seq):
    head_rows = lambda section: pl.BlockSpec((HEAD, n), lambda h, s=section: (N_HEADS * s + h, 0))
    per_head = pl.BlockSpec((HEAD, n_seq), lambda h: (h, 0))
    state_in = lambda layer: pl.BlockSpec((None, 1, HEAD, HEAD, n_seq), lambda h: (layer, h, 0, 0, 0))
    state_out = ((1, HEAD, HEAD, n_seq), lambda h: (h, 0, 0, 0))
    return head_rows, per_head, state_in, state_out


def _ret_t(u_t, cos_t, sin_t, gam, gain_b, state_t, n_seq, seq_len, layer, prev):
    n = u_t.shape[1]
    head_rows, per_head, state_in, state_out = _t_specs(n, n_seq)
    return _stack_call(
        functools.partial(_ret_t_kernel, n_seq=n_seq, seq_len=seq_len), prev, {1: state_out}, layer,
        in_specs=[head_rows(0), head_rows(1), head_rows(2), head_rows(3), _const_spec(cos_t.shape),
                  _const_spec(sin_t.shape), pl.BlockSpec((1, 1, n_seq), lambda h: (h, 0, 0)), per_head,
                  state_in(layer)],
        args=(u_t, u_t, u_t, u_t, cos_t, sin_t, gam, gain_b, state_t),
        grid=(N_HEADS,),
        out_specs=[pl.BlockSpec((HEAD, n), lambda h: (h, 0)), None],
        out_shape=[jax.ShapeDtypeStruct((GROUP_W, n), F32), jax.ShapeDtypeStruct(state_t.shape, F32)],
        scratch_shapes=[pltpu.VMEM((HEAD, n), F32)] * 2,
        compiler_params=_params(("parallel",)),
        name="ret_sample_t",
    )


def _hg_t(u_t, lb_b, gain_b, state_t, n_seq, seq_len, layer, prev):
    n = u_t.shape[1]
    head_rows, per_head, state_in, state_out = _t_specs(n, n_seq)
    return _stack_call(
        functools.partial(_hg_t_kernel, n_seq=n_seq, seq_len=seq_len), prev, {1: state_out}, layer,
        in_specs=[head_rows(0), head_rows(1), head_rows(2), head_rows(3), per_head, per_head, state_in(layer)],
        args=(u_t, u_t, u_t, u_t, lb_b, gain_b, state_t),
        grid=(N_HEADS,),
        out_specs=[pl.BlockSpec((HEAD, n), lambda h: (h, 0)), None],
        out_shape=[jax.ShapeDtypeStruct((GROUP_W, n), F32), jax.ShapeDtypeStruct(state_t.shape, F32)],
        scratch_shapes=[pltpu.VMEM((HEAD, n), F32)] * 3,
        compiler_params=_params(("parallel",)),
        name="hgrn_sample_t",
    )


def _rope_tables_t(pos0, seq_len, n_seq):
    half = HEAD // 2
    pos = pos0 + jnp.arange(seq_len, dtype=F32)
    inv = ROPE_BASE ** (-jnp.arange(half, dtype=F32) / half)
    ang = pos[:, None] * inv
    cos = jnp.tile(jnp.cos(ang).T, (2, 1))
    sin = jnp.sin(ang).T
    sin_signed = jnp.concatenate([-sin, sin], axis=0)
    return jnp.repeat(cos, n_seq, axis=1), jnp.repeat(sin_signed, n_seq, axis=1)


def _block_diag(w):
    h, i, j = w.shape
    eye = jnp.eye(h, dtype=w.dtype)
    return (eye[:, None, :, None] * w[:, :, None, :]).reshape(h * i, h * j)


def _rope_tables(pos0, seq_len):
    half = HEAD // 2
    pos = pos0 + jnp.arange(seq_len, dtype=F32)
    inv = ROPE_BASE ** (-jnp.arange(half, dtype=F32) / half)
    ang = pos[:, None] * inv
    cos = jnp.tile(jnp.cos(ang), (1, 2 * N_HEADS))
    sin = jnp.sin(ang)
    sin_signed = jnp.concatenate([jnp.tile(-sin, (1, N_HEADS)), jnp.tile(sin, (1, N_HEADS))], axis=-1)
    return cos, sin_signed


def _layer_weights(P, l):
    row = lambda a: a.reshape(1, -1)
    pad128 = lambda a: jnp.pad(a, (0, 128 - a.shape[0])).reshape(1, 128)
    lb_sm = jax.nn.softmax(P['hg_lb'].astype(F32), axis=0)
    lb = (jnp.cumsum(lb_sm, axis=0) - lb_sm[0])[l]
    return dict(
        w_in=P['w_in_prepped'], w_in_t=P['w_in_t'],
        ln_mix_pre=row(P['ln_mix_pre'][l]), ln_mix_post=row(P['ln_mix_post'][l]),
        ln_xa_pre=row(P['ln_xa_pre'][l]), ln_xa_post=row(P['ln_xa_post'][l]),
        ln_ffn_pre=row(P['ln_ffn_pre'][l]), ln_ffn_post=row(P['ln_ffn_post'][l]),
        ssd_cw=P['ssd_conv_w'][l], ssd_cb=row(P['ssd_conv_b'][l]),
        ssd_dtb=pad128(P['ssd_dt_bias'][l]), ssd_alog=pad128(P['ssd_A_log'][l]),
        ssd_d=row(jnp.repeat(P['ssd_D'][l], HEAD)), ssd_norm=row(P['ssd_norm'][l]),
        ret_norm=row(P['ret_norm'][l]),
        rg_cw=P['rg_conv_w'][l], rg_cb=row(P['rg_conv_b'][l]),
        rg_wa=_block_diag(P['rg_wa'][l]).astype(BF16), rg_ba=row(P['rg_ba'][l]),
        rg_wx=_block_diag(P['rg_wx'][l]).astype(BF16), rg_bx=row(P['rg_bx'][l]),
        rg_lam=row(P['rg_lambda'][l]),
        hg_lb=row(lb), hg_norm=row(P['hg_norm'][l]),
        w_out=P['w_out_bf16'], w_xq=P['w_xq_bf16'], w_xo=P['w_xo_bf16'], w_gu=P['w_gu_bf16'],
        w_down=P['w_down_bf16'], ln_mem=row(P['ln_mem'][l]), w_xkv=P['w_xkv_bf16'],
    )


def _layer(x, W, k_mem, v_mem, st, n_seq, seq_len, pos0, layer, prev=None):
    if st is None:
        u_ssd, u_ret, u_rg, u_hg, u_dt = _norm_matmul(
            x, W['ln_mix_pre'], W['w_in'], (IN_COLS_SSD, IN_COLS_RET_SPLIT, IN_COLS_RG, IN_COLS_HG, IN_COLS_DT),
            "in_proj", layer=layer)
        cos, sin_signed = _rope_tables(pos0, seq_len)
        y_ssd, s_ssd, b_ssd = _ssd(u_ssd, u_dt, None, None, W['ssd_cw'], W['ssd_cb'], W['ssd_dtb'],
                                   W['ssd_alog'], W['ssd_d'], W['ssd_norm'], n_seq, seq_len, layer)
        y_ret, s_ret = _ret(u_ret, cos, sin_signed, W['ret_norm'], n_seq, seq_len)
        y_rg, h_rg, b_rg = _rg(u_rg, None, None, W['rg_cw'], W['rg_cb'], W['rg_wa'], W['rg_ba'],
                               W['rg_wx'], W['rg_bx'], W['rg_lam'], n_seq, seq_len)
        h_rg = h_rg.reshape(n_seq, GROUP_W)
        y_hg, s_hg = _hg(u_hg, W['hg_lb'], W['hg_norm'], n_seq, seq_len)
    else:
        ssd_s, ssd_buf, ret_s, rg_h, rg_buf, hg_s = st
        u_ssd, u_rg, u_dt = _norm_matmul(x, W['ln_mix_pre'], W['w_in'], (IN_COLS_SSD, IN_COLS_RG, IN_COLS_DT),
                                         "in_proj", layer=layer)
        to_tb = lambda a: a.reshape(n_seq, seq_len, -1).transpose(1, 0, 2).reshape(n_seq * seq_len, -1)
        from_t = lambda a: a.reshape(-1, seq_len, n_seq).transpose(2, 1, 0).reshape(n_seq * seq_len, -1)
        bcast = lambda g: jnp.broadcast_to(g.reshape(-1, 1), (g.size, n_seq))
        ut_ret, ut_hg = _in_proj_t(to_tb(x), W['ln_mix_pre'], W['w_in_t'], layer)
        cos_t, sin_t = _rope_tables_t(pos0, seq_len, n_seq)
        log_gamma = jnp.log(1.0 - jnp.exp2(-5.0 - jnp.arange(N_HEADS, dtype=F32)))
        gam = jnp.broadcast_to(jnp.exp(log_gamma)[:, None, None], (N_HEADS, 1, n_seq))
        yt_ret, s_ret = _ret_t(ut_ret, cos_t, sin_t, gam, bcast(W['ret_norm']), ret_s, n_seq, seq_len, layer,
                               prev and (prev[1],))
        yt_hg, s_hg = _hg_t(ut_hg, bcast(W['hg_lb']), bcast(W['hg_norm']), hg_s, n_seq, seq_len, layer,
                            prev and (prev[2],))
        y_ret, y_hg = from_t(yt_ret), from_t(yt_hg)
        pad_rows = lambda b: jnp.pad(b, ((0, 0), (0, 1), (0, 0))).reshape(n_seq * 4, b.shape[-1])
        y_ssd, s_ssd, b_ssd = _ssd(u_ssd, u_dt, ssd_s, pad_rows(ssd_buf), W['ssd_cw'], W['ssd_cb'],
                                   W['ssd_dtb'], W['ssd_alog'], W['ssd_d'], W['ssd_norm'], n_seq, seq_len, layer,
                                   prev and (prev[0],))
        b_ssd = b_ssd.reshape(n_seq, 4, -1)[:, :CONV_W - 1]
        y_rg, h_rows, b_rg = _rg(u_rg, jnp.repeat(rg_h, seq_len, axis=0), pad_rows(rg_buf), W['rg_cw'],
                                 W['rg_cb'], W['rg_wa'], W['rg_ba'], W['rg_wx'], W['rg_bx'], W['rg_lam'],
                                 n_seq, seq_len)
        h_rg = h_rows.reshape(n_seq, seq_len, GROUP_W)[:, seq_len - 1]
        b_rg = b_rg.reshape(n_seq, 4, -1)[:, :CONV_W - 1]
    ys = (y_ssd, y_ret, y_rg, y_hg)
    if st is None:
        x3 = _xattn_ffn(ys, x, W['w_out'], W['ln_mix_post'], k_mem, v_mem, W['ln_xa_pre'], W['w_xq'], W['w_xo'],
                        W['ln_xa_post'], W['ln_ffn_pre'], W['w_gu'], W['w_down'], W['ln_ffn_post'], seq_len,
                        layer)
    else:
        x1, q = _out_proj(ys, x, W['w_out'], W['ln_mix_post'], W['ln_xa_pre'], W['w_xq'], layer)
        o = _attention_sample(q, k_mem, v_mem, layer, seq_len)
        x3 = _ffn(o, x1, W['w_xo'], W['ln_xa_post'], W['ln_ffn_pre'], W['w_gu'], W['w_down'],
                  W['ln_ffn_post'], layer)
    return x3, (s_ssd, b_ssd, s_ret, h_rg, b_rg, s_hg)


def kernel(x_prompt, x_sample, state_ssd, state_ssd_conv, state_ret, state_rglru, state_rglru_conv, state_hgrn, cache_mem_k, cache_mem_v, mem_prompt, ln_mix_pre, ln_mix_post, ln_xa_pre, ln_xa_post, ln_ffn_pre, ln_ffn_post, w_in, ssd_conv_w, ssd_conv_b, ssd_dt_bias, ssd_A_log, ssd_D, ssd_norm, ret_norm, rg_conv_w, rg_conv_b, rg_wa, rg_ba, rg_wx, rg_bx, rg_lambda, hg_lb, hg_norm, w_out, ln_mem, w_xq, w_xkv, w_xo, w_gu, w_down):
    P = dict(ln_mix_pre=ln_mix_pre, ln_mix_post=ln_mix_post, ln_xa_pre=ln_xa_pre, ln_xa_post=ln_xa_post,
             ln_ffn_pre=ln_ffn_pre, ln_ffn_post=ln_ffn_post, w_in=w_in, ssd_conv_w=ssd_conv_w,
             ssd_conv_b=ssd_conv_b, ssd_dt_bias=ssd_dt_bias, ssd_A_log=ssd_A_log, ssd_D=ssd_D,
             ssd_norm=ssd_norm, ret_norm=ret_norm, rg_conv_w=rg_conv_w, rg_conv_b=rg_conv_b,
             rg_wa=rg_wa, rg_ba=rg_ba, rg_wx=rg_wx, rg_bx=rg_bx, rg_lambda=rg_lambda, hg_lb=hg_lb,
             hg_norm=hg_norm, w_out=w_out, ln_mem=ln_mem, w_xq=w_xq, w_xkv=w_xkv, w_xo=w_xo,
             w_gu=w_gu, w_down=w_down)
    P['w_in_prepped'], P['w_in_t'] = _in_weight_prep(w_in)
    for name in ('w_out', 'w_xq', 'w_xo', 'w_gu', 'w_down', 'w_xkv'):
        P[name + '_bf16'] = P[name].astype(BF16)
    ssd_t = jnp.swapaxes(state_ssd, -1, -2)
    ret_t = jnp.transpose(state_ret, (0, 2, 3, 4, 1))
    hgrn_t = jnp.transpose(state_hgrn, (0, 2, 3, 4, 1))
    depth = w_in.shape[0]
    bp, tp, d = x_prompt.shape
    bs, ts, _ = x_sample.shape
    n_mem = mem_prompt.shape[1]
    y_p = x_prompt.reshape(bp * tp, d)
    y_s = x_sample.reshape(bs * ts, d)
    mem = mem_prompt.reshape(bp * n_mem, d)
    p_st, s_st = [], []
    mem_prev = samp_prev = None
    for l in range(depth):
        W = _layer_weights(P, l)
        mk, mv, mk5, mv5 = _mem_kv(mem, W['ln_mem'], W['w_xkv'], l, n_mem, mem_prev)
        mem_prev = (mk5, mv5)
        mk = mk.reshape(bp, n_mem, d)
        mv = mv.reshape(bp, n_mem, d)
        y_p, st_p = _layer(y_p, W, mk, mv, None, bp, tp, 0.0, l)
        p_st.append(st_p)
        st_l = (ssd_t, state_ssd_conv[l], ret_t, state_rglru[l], state_rglru_conv[l], hgrn_t)
        y_s, st_s = _layer(y_s, W, cache_mem_k, cache_mem_v, st_l, bs, ts, float(PAST_LEN), l, samp_prev)
        samp_prev = (st_s[0], st_s[2], st_s[5])
        s_st.append(st_s)
    stack = lambda sts, i: jnp.stack([s[i] for s in sts], axis=0)
    return (y_p.reshape(bp, tp, d), y_s.reshape(bs, ts, d),
            jnp.swapaxes(stack(p_st, 0), -1, -2), stack(p_st, 1), stack(p_st, 2), stack(p_st, 3), stack(p_st, 4),
            stack(p_st, 5),
            mem_prev[0], mem_prev[1],
            jnp.swapaxes(samp_prev[0], -1, -2), stack(s_st, 1), jnp.transpose(samp_prev[1], (0, 4, 1, 2, 3)),
            stack(s_st, 3), stack(s_st, 4), jnp.transpose(samp_prev[2], (0, 4, 1, 2, 3)))
```

```python
import functools
import math

import jax
import jax.numpy as jnp
from jax import lax
from jax.experimental import pallas as pl
from jax.experimental.pallas import tpu as pltpu

F32 = jnp.float32
BF16 = jnp.bfloat16
EPS = 1e-6
LOG2_E = 1.4426950408889634

GROUP_W = 256
HEAD = 64
N_HEADS = 4
SSD_N = 128
SSD_G = 2
CONV_W = 4
XA_H = 4
XA_HD = 256
ROPE_BASE = 10000.0
RG_C = 8.0
CHUNK = 64
PAST_LEN = 16384

ROW_TILE = 512
IN_TILE = 1024
MIX_TILE = 2048
RG_TILE = 2048
SAMPLE_SEQS = 16
ATT_SEQS = 8
FF_TILE = 256
VMEM_LIMIT = 56 * 1024 * 1024


def _bdot(a, b):
    return jnp.dot(a.astype(BF16), b.astype(BF16), preferred_element_type=F32)


def _bdot_nt(a, b):
    return lax.dot_general(a.astype(BF16), b.astype(BF16), (((1,), (1,)), ((), ())),
                           preferred_element_type=F32)


def _bdot_tn(a, b):
    return lax.dot_general(a.astype(BF16), b.astype(BF16), (((0,), (0,)), ((), ())),
                           preferred_element_type=F32)


def _split3(x):
    hi = x.astype(BF16)
    r = x - hi.astype(F32)
    mid = r.astype(BF16)
    lo = (r - mid.astype(F32)).astype(BF16)
    return hi, mid, lo


def _sel_dot(sel, x):
    hi, mid, lo = _split3(x)
    d = lambda y: jnp.dot(sel, y, preferred_element_type=F32)
    return (d(hi) + d(mid)) + d(lo)


def _dot_sel(x, sel):
    hi, mid, lo = _split3(x)
    d = lambda y: jnp.dot(y, sel, preferred_element_type=F32)
    return (d(hi) + d(mid)) + d(lo)


def _sel_dot_nt(sel, x):
    hi, mid, lo = _split3(x)
    d = lambda y: lax.dot_general(sel, y, (((1,), (1,)), ((), ())), preferred_element_type=F32)
    return (d(hi) + d(mid)) + d(lo)


def _rms(x, g):
    return x * lax.rsqrt(jnp.mean(x * x, axis=-1, keepdims=True) + EPS) * g


def _sigmoid(x):
    return jax.nn.sigmoid(x)


def _silu(x):
    return x * jax.nn.sigmoid(x)


def _softplus(x):
    return jnp.maximum(x, 0.0) + jnp.log1p(jnp.exp(-jnp.abs(x)))


def _gelu_tanh(x):
    c = math.sqrt(2.0 / math.pi)
    return 0.5 * x * (1.0 + jnp.tanh(c * (x + 0.044715 * (x * x * x))))


def _iota(shape, dim):
    return lax.broadcasted_iota(jnp.int32, shape, dim)


def _seg_causal_mask(n, seg_shift):
    r = _iota((n, n), 0)
    c = _iota((n, n), 1)
    return ((r >> seg_shift) == (c >> seg_shift)) & (c <= r)


def _seg_cumsum(x, seg_shift):
    rows = x.shape[0]
    blk = min(rows, 256)
    mask = _seg_causal_mask(blk, seg_shift).astype(BF16)
    return jnp.concatenate([_sel_dot(mask, x[i:i + blk]) for i in range(0, rows, blk)], axis=0)


def _head_expand_mat(width_in=128):
    r = _iota((width_in, GROUP_W), 0)
    c = _iota((width_in, GROUP_W), 1)
    return ((c >> 6) == r).astype(BF16)


def _head_block_mask():
    return (_iota((GROUP_W, GROUP_W), 0) >> 6) == (_iota((GROUP_W, GROUP_W), 1) >> 6)


def _block_diag_rows(x, mask01):
    return jnp.concatenate([x.astype(BF16)] * N_HEADS, axis=0) * mask01


def _head_stat(y, center):
    ones_bd = _head_block_mask().astype(BF16)
    if center:
        y = y - _dot_sel(y, ones_bd) * (1.0 / HEAD)
    return y * lax.rsqrt(_dot_sel(y * y, ones_bd) * (1.0 / HEAD) + EPS)


def _conv_taps_carry(x, xx_ref, w_ref, b_ref, first):
    rows = x.shape[0]

    @pl.when(first)
    def _():
        xx_ref[0:8, :] = jnp.zeros((8, x.shape[1]), F32)

    xx_ref[8:8 + rows, :] = x
    y = b_ref[...] + w_ref[CONV_W - 1:CONV_W, :] * x
    for j in range(CONV_W - 1):
        y = y + w_ref[j:j + 1, :] * xx_ref[pl.ds(8 - (CONV_W - 1) + j, rows), :]
    xx_ref[0:8, :] = xx_ref[rows:rows + 8, :]
    return y


def _conv_taps_seq4(x, buf4, w_ref, b_ref):
    rows = x.shape[0]
    r = _iota((rows, rows), 0)
    c = _iota((rows, rows), 1)
    t = r & 3
    y = b_ref[...] + w_ref[CONV_W - 1:CONV_W, :] * x
    for d in range(1, CONV_W):
        shift = ((c == r - d) & (t >= d)).astype(BF16)
        hist = ((t < d) & (c == (r - t) + (3 + t - d))).astype(BF16)
        y = y + w_ref[CONV_W - 1 - d:CONV_W - d, :] * (_sel_dot(shift, x) + _sel_dot(hist, buf4))
    new_sel = (((r & 3) < 3) & (c == r + 1)).astype(BF16)
    return y, _sel_dot(new_sel, x)


def _norm_matmul_kernel(x_ref, g_ref, w_ref, *o_refs, col_ranges):
    h = _rms(x_ref[...], g_ref[...]).astype(BF16)
    for o_ref, pieces in zip(o_refs, col_ranges):
        off = 0
        for a, b in pieces:
            o_ref[:, off:off + b - a] = jnp.dot(h, w_ref[:, a:b], preferred_element_type=F32)
            off += b - a


def _const_spec(shape):
    nd = len(shape)
    return pl.BlockSpec(shape, lambda *_: (0,) * nd, pipeline_mode=pl.Buffered(1))


def _layer_spec(shape, layer):
    nd = len(shape)
    return pl.BlockSpec((None,) + tuple(shape[1:]), lambda *_: (layer,) + (0,) * (nd - 1),
                        pipeline_mode=pl.Buffered(1))


def _stack_call(kernel, prevs, stacked, layer, *, in_specs, args, out_specs, out_shape, **kw):
    n_in = len(in_specs)
    out_specs = list(out_specs)
    order = sorted(stacked)
    for o in order:
        blk, idx = stacked[o]
        if prevs is None:
            depth = out_shape[o].shape[0]
            out_specs[o] = pl.BlockSpec((depth,) + tuple(blk), lambda *g, idx=idx: (0,) + tuple(idx(*g)))
        else:
            out_specs[o] = pl.BlockSpec((None,) + tuple(blk), lambda *g, idx=idx: (layer,) + tuple(idx(*g)))
    if prevs is None:
        def body(*refs):
            refs = list(refs)
            for o in order:
                full = refs[n_in + o]
                for d in range(full.shape[0]):
                    if d != layer:
                        full[d] = jnp.zeros(full.shape[1:], full.dtype)
                refs[n_in + o] = full.at[layer]
            return kernel(*refs)

        return pl.pallas_call(body, in_specs=in_specs, out_specs=out_specs, out_shape=out_shape, **kw)(*args)

    def body(*refs):
        return kernel(*refs[:n_in], *refs[n_in + len(prevs):])

    return pl.pallas_call(
        body,
        in_specs=list(in_specs) + [pl.BlockSpec(memory_space=pl.ANY)] * len(prevs),
        out_specs=out_specs, out_shape=out_shape,
        input_output_aliases={n_in + j: o for j, o in enumerate(order)},
        **kw)(*args, *prevs)


def _params(sem):
    return pltpu.CompilerParams(dimension_semantics=sem, vmem_limit_bytes=VMEM_LIMIT)


def _in_weight_prep_kernel(wt_ref, o_ref, ot_ref, *, blocks, dt_block, dt_row, dt_n, t_rows):
    off = 0
    for src, n in t_rows:
        ot_ref[off:off + n, :] = wt_ref[pl.ds(src, n), :].astype(BF16)
        off += n
    for j, pieces in enumerate(blocks):
        if j == dt_block:
            dt = wt_ref[pl.ds(dt_row, 8), :]
            dt = jnp.where(_iota(dt.shape, 0) < dt_n, dt, 0.0)
            blk = jnp.concatenate([dt, jnp.zeros((120, dt.shape[1]), F32)], axis=0)
        else:
            blk = jnp.concatenate([wt_ref[pl.ds(src, n), :] for src, n in pieces], axis=0)
        o_ref[:, j * 128:(j + 1) * 128] = blk.T.astype(BF16)


IN_COLS_SSD = ((0, 1024),)
IN_COLS_RG = ((2048, 2560),)
IN_COLS_HG = ((2560, 3584),)
IN_COLS_DT = ((3584, 3712),)
IN_COLS_RET_SPLIT = ((3712, 4224), (1536, 2048))


def _in_weight_prep(w_in):
    depth, d, d_in = w_in.shape
    z0 = GROUP_W + (GROUP_W + 2 * SSD_G * SSD_N)
    d0 = z0 + N_HEADS
    blocks = [((src, 128),) for src in tuple(range(0, z0, 128)) + tuple(range(d0, d_in, 128))]
    dt_block = len(blocks)
    blocks.append(())
    half = HEAD // 2
    for base in (d0, d0 + GROUP_W):
        for j in range(2):
            blocks.append(tuple((base + h * HEAD + j * half, half) for h in range(N_HEADS)))
    n_out = len(blocks) * 128
    t_rows = ((d0, 4 * GROUP_W), (d0 + 6 * GROUP_W, 4 * GROUP_W))
    n_t = sum(n for _, n in t_rows)
    wt = jnp.swapaxes(w_in, 1, 2)
    return pl.pallas_call(
        functools.partial(_in_weight_prep_kernel, blocks=tuple(blocks), dt_block=dt_block, dt_row=z0,
                          dt_n=N_HEADS, t_rows=t_rows),
        grid=(depth,),
        in_specs=[pl.BlockSpec((None, d_in, d), lambda l: (l, 0, 0), pipeline_mode=pl.Buffered(1))],
        out_specs=[pl.BlockSpec((None, d, n_out), lambda l: (l, 0, 0)),
                   pl.BlockSpec((None, n_t, d), lambda l: (l, 0, 0))],
        out_shape=[jax.ShapeDtypeStruct((depth, d, n_out), BF16), jax.ShapeDtypeStruct((depth, n_t, d), BF16)],
        compiler_params=_params(("parallel",)),
        name="in_weight_prep",
    )(wt)


def _in_proj_t_kernel(x_ref, g_ref, wt_ref, *o_refs):
    h = _rms(x_ref[...], g_ref[...]).astype(BF16)
    off = 0
    for o_ref in o_refs:
        f = o_ref.shape[0]
        o_ref[...] = lax.dot_general(wt_ref[off:off + f, :], h, (((1,), (1,)), ((), ())),
                                     preferred_element_type=F32)
        off += f


def _in_proj_t(x, g, wt, layer):
    n, d = x.shape
    f = wt.shape[1] // 2
    return pl.pallas_call(
        _in_proj_t_kernel,
        grid=(1,),
        in_specs=[_const_spec((n, d)), _const_spec((1, d)), _layer_spec(wt.shape, layer)],
        out_specs=[_const_spec((f, n))] * 2,
        out_shape=[jax.ShapeDtypeStruct((f, n), F32)] * 2,
        compiler_params=_params(("arbitrary",)),
        name="in_proj_t",
    )(x, g, wt)


def _mem_kv_kernel(x_ref, g_ref, w_ref, k_ref, v_ref, k5_ref, v5_ref):
    h = _rms(x_ref[...], g_ref[...]).astype(BF16)
    d = x_ref.shape[1]
    k = jnp.dot(h, w_ref[:, 0:d], preferred_element_type=F32)
    v = jnp.dot(h, w_ref[:, d:2 * d], preferred_element_type=F32)
    k_ref[...] = k
    v_ref[...] = v
    k5_ref[...] = k.reshape(k5_ref.shape)
    v5_ref[...] = v.reshape(v5_ref.shape)


def _mem_kv(mem, g, w, layer, n_mem, prev):
    n, d = mem.shape
    depth = w.shape[0]
    tm = min(ROW_TILE, n)
    seqs = tm // n_mem
    row = pl.BlockSpec((tm, d), lambda i: (i, 0))
    blk5 = ((seqs, n_mem, XA_H, XA_HD), lambda i: (i, 0, 0, 0))
    shape5 = jax.ShapeDtypeStruct((depth, n // n_mem, n_mem, XA_H, XA_HD), F32)
    return _stack_call(
        _mem_kv_kernel, prev, {2: blk5, 3: blk5}, layer,
        in_specs=[row, _const_spec((1, d)), _wspec(w, layer)],
        args=(mem, g, w),
        grid=(n // tm,),
        out_specs=[row, row, None, None],
        out_shape=[jax.ShapeDtypeStruct((n, d), F32)] * 2 + [shape5] * 2,
        compiler_params=_params(("parallel",)),
        name="mem_kv",
    )


def _wspec(w, layer):
    return _layer_spec(w.shape, layer) if w.ndim == 3 else _const_spec(w.shape)


def _norm_matmul(x, g, w, col_ranges, name, layer=None):
    n, d = x.shape
    tm = min(IN_TILE, n)
    w_spec = _wspec(w, layer)
    widths = [sum(b - a for a, b in pieces) for pieces in col_ranges]
    return pl.pallas_call(
        functools.partial(_norm_matmul_kernel, col_ranges=col_ranges),
        grid=(n // tm,),
        in_specs=[pl.BlockSpec((tm, d), lambda i: (i, 0)), _const_spec((1, d)), w_spec],
        out_specs=[pl.BlockSpec((tm, wd), lambda i: (i, 0)) for wd in widths],
        out_shape=[jax.ShapeDtypeStruct((n, wd), F32) for wd in widths],
        compiler_params=_params(("parallel",)),
        name=name,
    )(x, g, w)


def _out_proj_kernel(ys_ref, yr_ref, yg_ref, yh_ref, x_ref, wo_ref, gpost_ref, gpre_ref, wq_ref, x1_ref, q_ref):
    x1 = _mix_out(ys_ref, yr_ref, yg_ref, yh_ref, x_ref, wo_ref, gpost_ref)
    x1_ref[...] = x1
    q_ref[...] = _bdot(_rms(x1, gpre_ref[...]), wq_ref[...])


def _out_proj(ys, x, w_out, g_post, g_pre, w_xq, layer):
    n, d = x.shape
    tm = min(ROW_TILE, n)
    row = lambda wd: pl.BlockSpec((tm, wd), lambda i: (i, 0))
    vec = _const_spec((1, d))
    return pl.pallas_call(
        _out_proj_kernel,
        grid=(n // tm,),
        in_specs=[row(GROUP_W)] * 4 + [row(d), _wspec(w_out, layer), vec, vec, _wspec(w_xq, layer)],
        out_specs=[row(d)] * 2,
        out_shape=[jax.ShapeDtypeStruct((n, d), F32)] * 2,
        compiler_params=_params(("parallel",)),
        name="out_proj",
    )(*ys, x, w_out, g_post, g_pre, w_xq)


def _ffn_tail(o, x1, wxo_ref, gxa_ref, gpre_ref, wgu_ref, wdn_ref, gpost_ref, d_ff):
    a = _bdot(o, wxo_ref[...])
    x2 = x1 + _rms(a, gxa_ref[...])
    h = _rms(x2, gpre_ref[...]).astype(BF16)
    acc = jnp.zeros(x2.shape, F32)
    for j in range(0, d_ff, FF_TILE):
        g = jnp.dot(h, wgu_ref[:, j:j + FF_TILE], preferred_element_type=F32)
        u = jnp.dot(h, wgu_ref[:, d_ff + j:d_ff + j + FF_TILE], preferred_element_type=F32)
        act = (_silu(g) * u).astype(BF16)
        acc = acc + jnp.dot(act, wdn_ref[j:j + FF_TILE, :], preferred_element_type=F32)
    return x2 + _rms(acc, gpost_ref[...])


def _ffn_kernel(o_ref, x1_ref, wxo_ref, gxa_ref, gpre_ref, wgu_ref, wdn_ref, gpost_ref, x3_ref, *, d_ff):
    x3_ref[...] = _ffn_tail(o_ref[...], x1_ref[...], wxo_ref, gxa_ref, gpre_ref, wgu_ref, wdn_ref, gpost_ref,
                            d_ff)


def _ffn(o, x1, w_xo, g_xa, g_pre, w_gu, w_down, g_post, layer):
    n, d = x1.shape
    d_ff = w_down.shape[-2]
    tm = min(ROW_TILE, n)
    row = pl.BlockSpec((tm, d), lambda i: (i, 0))
    vec = _const_spec((1, d))
    return pl.pallas_call(
        functools.partial(_ffn_kernel, d_ff=d_ff),
        grid=(n // tm,),
        in_specs=[row, row, _wspec(w_xo, layer), vec, vec, _wspec(w_gu, layer), _wspec(w_down, layer), vec],
        out_specs=row,
        out_shape=jax.ShapeDtypeStruct((n, d), F32),
        compiler_params=_params(("parallel",)),
        name="attn_out_ffn",
    )(o, x1, w_xo, g_xa, g_pre, w_gu, w_down, g_post)


def _mix_out(ys_ref, yr_ref, yg_ref, yh_ref, x_ref, wo_ref, gpost_ref):
    mix = _bdot(ys_ref[...], wo_ref[0:GROUP_W, :])
    mix = mix + _bdot(yr_ref[...], wo_ref[GROUP_W:2 * GROUP_W, :])
    mix = mix + _bdot(yg_ref[...], wo_ref[2 * GROUP_W:3 * GROUP_W, :])
    mix = mix + _bdot(yh_ref[...], wo_ref[3 * GROUP_W:4 * GROUP_W, :])
    return x_ref[...] + _rms(mix, gpost_ref[...])


def _xattn_ffn_kernel(ys_ref, yr_ref, yg_ref, yh_ref, x_ref, wo_ref, gmix_ref, k_ref, v_ref, gq_ref, wq_ref,
                      wxo_ref, gxa_ref, gpre_ref, wgu_ref, wdn_ref, gpost_ref, x3_ref, o_scr, *, d_ff):
    x1 = _mix_out(ys_ref, yr_ref, yg_ref, yh_ref, x_ref, wo_ref, gmix_ref)
    q = _bdot(_rms(x1, gq_ref[...]), wq_ref[...])
    scale = XA_HD ** -0.5
    for h in range(XA_H):
        sl = slice(h * XA_HD, (h + 1) * XA_HD)
        s = _bdot_nt(q[:, sl], k_ref[0, :, sl]) * scale
        o_scr[:, sl] = _softmax_pv(s, v_ref[0, :, sl])
    x3_ref[...] = _ffn_tail(o_scr[...], x1, wxo_ref, gxa_ref, gpre_ref, wgu_ref, wdn_ref, gpost_ref, d_ff)


def _xattn_ffn(ys, x, w_out, g_mix, k, v, g_q, w_xq, w_xo, g_xa, g_pre, w_gu, w_down, g_post, seq_len, layer):
    n, d = x.shape
    n_seq, n_mem, _ = k.shape
    d_ff = w_down.shape[-2]
    nt = seq_len // ROW_TILE
    row = lambda wd: pl.BlockSpec((ROW_TILE, wd), lambda b, t: (b * nt + t, 0))
    kv = pl.BlockSpec((1, n_mem, d), lambda b, t: (b, 0, 0))
    vec = _const_spec((1, d))
    return pl.pallas_call(
        functools.partial(_xattn_ffn_kernel, d_ff=d_ff),
        grid=(n_seq, nt),
        in_specs=[row(GROUP_W)] * 4 + [row(d), _wspec(w_out, layer), vec, kv, kv, vec, _wspec(w_xq, layer),
                                       _wspec(w_xo, layer), vec, vec, _wspec(w_gu, layer),
                                       _wspec(w_down, layer), vec],
        out_specs=row(d),
        out_shape=jax.ShapeDtypeStruct((n, d), F32),
        scratch_shapes=[pltpu.VMEM((ROW_TILE, d), F32)],
        compiler_params=_params(("parallel", "parallel")),
        name="xattn_ffn",
    )(*ys, x, w_out, g_mix, k, v, g_q, w_xq, w_xo, g_xa, g_pre, w_gu, w_down, g_post)


def _softmax_pv(s, v):
    m = jnp.max(s, axis=-1, keepdims=True)
    p = jnp.exp(s - m)
    return _bdot(p, v) / jnp.sum(p, axis=-1, keepdims=True)


def _attn_sample_kernel(q_ref, k_ref, v_ref, o_ref, *, seqs, seg):
    scale = XA_HD ** -0.5
    rows = seqs * seg
    n_mem = k_ref.shape[2]
    wq = jnp.concatenate([q_ref[:, h * XA_HD:(h + 1) * XA_HD] for h in range(XA_H)], axis=0)
    wshape = (XA_H * rows, n_mem * XA_H)
    valid = (_iota(wshape, 0) >> (rows.bit_length() - 1)) == (_iota(wshape, 1) & (XA_H - 1))
    row_seq = _iota((rows, XA_HD), 0) >> (seg.bit_length() - 1)
    accs = [jnp.zeros((rows, XA_HD), F32) for _ in range(XA_H)]
    for i in range(seqs):
        k2 = k_ref[0, i].reshape(n_mem * XA_H, XA_HD)
        v2 = v_ref[0, i].reshape(n_mem * XA_H, XA_HD)
        s = jnp.where(valid, _bdot_nt(wq, k2) * scale, -jnp.inf)
        o = _softmax_pv(s, v2)
        for h in range(XA_H):
            accs[h] = jnp.where(row_seq == i, o[h * rows:(h + 1) * rows], accs[h])
    for h in range(XA_H):
        o_ref[:, h * XA_HD:(h + 1) * XA_HD] = accs[h]


def _attention_sample(q, cache_k, cache_v, layer, seq_len):
    n, d = q.shape
    _, n_seq, n_mem, heads, hd = cache_k.shape
    kv = pl.BlockSpec((1, ATT_SEQS, n_mem, heads, hd), lambda i: (layer, i, 0, 0, 0))
    row = pl.BlockSpec((ATT_SEQS * seq_len, d), lambda i: (i, 0))
    return pl.pallas_call(
        functools.partial(_attn_sample_kernel, seqs=ATT_SEQS, seg=seq_len),
        grid=(n_seq // ATT_SEQS,),
        in_specs=[row, kv, kv],
        out_specs=row,
        out_shape=jax.ShapeDtypeStruct((n, d), F32),
        compiler_params=_params(("parallel",)),
        name="xattn_sample",
    )(q, cache_k, cache_v)


def _ssd_intra(q, k, v, cum_x, cum_t, mask):
    sc = _bdot_nt(q, k)
    outs = []
    for hh in range(2):
        col = cum_x[:, hh * HEAD:(hh + 1) * HEAD]
        dec = jnp.exp2(jnp.where(mask, col - cum_t[hh:hh + 1, :], -jnp.inf))
        outs.append(_bdot(sc * dec, v[:, hh * HEAD:(hh + 1) * HEAD]))
    return jnp.concatenate(outs, axis=-1)


def _ssd_common(u, dtraw_ref, dtb_ref, alog_ref, conv, seg_shift):
    rows = u.shape[0]
    xbc = _silu(conv)
    xs = xbc[:, 0:GROUP_W]
    bm = xbc[:, GROUP_W:2 * GROUP_W]
    cm = xbc[:, 2 * GROUP_W:3 * GROUP_W]
    dt = _softplus(dtraw_ref[...] + dtb_ref[...])
    la = (-jnp.exp(alog_ref[...]) * LOG2_E) * dt
    cum = _seg_cumsum(la, seg_shift)
    expand = _head_expand_mat()
    dt_x = _dot_sel(dt, expand)
    cum_x = _dot_sel(cum, expand)
    sel8 = (_iota((8, 128), 0) == _iota((8, 128), 1)).astype(BF16)
    cum_t = _sel_dot_nt(sel8, cum)
    expand_full = ((_iota((128, 4 * 128), 1) >> 7) == _iota((128, 4 * 128), 0)).astype(BF16)
    cum_full = _dot_sel(cum, expand_full)
    return xs, bm, cm, dt_x, cum_x, cum_t, cum_full


def _ssd_finish(y, xs, z, d_ref, g_ref):
    y = y + d_ref[...] * xs
    return _rms(y * _silu(z), g_ref[...])


def _ssd_prompt_kernel(u_ref, dtraw_ref, cw_ref, cb_ref, dtb_ref, alog_ref, d_ref, g_ref,
                       y_ref, s_out_ref, buf_out_ref, s_ref, xx_ref, y_scr):
    t = pl.program_id(1)
    rows = u_ref.shape[0]

    @pl.when(t == 0)
    def _():
        s_ref[...] = jnp.zeros(s_ref.shape, F32)

    u = u_ref[...]
    conv = _conv_taps_carry(u[:, GROUP_W:], xx_ref, cw_ref, cb_ref, t == 0)
    xs, bm, cm, dt_x, cum_x, _, cum_full = _ssd_common(u, dtraw_ref, dtb_ref, alog_ref, conv, 6)
    v = xs * dt_x
    ecum_x = jnp.exp2(cum_x)
    hmask01 = _head_block_mask().astype(BF16)
    gmask01 = ((_iota((GROUP_W, GROUP_W), 0) >> 7) == (_iota((GROUP_W, GROUP_W), 1) >> 7)).astype(BF16)
    wrow = _iota((CHUNK, GROUP_W), 0)
    wcol = _iota((CHUNK, GROUP_W), 1) & (CHUNK - 1)
    n_chunks = rows // CHUNK
    groups = [slice(g * 128, (g + 1) * 128) for g in range(SSD_G)]
    a_wide, ds_t = [], []
    for c in range(n_chunks):
        rs = slice(c * CHUNK, (c + 1) * CHUNK)
        cum_c = cum_x[rs]
        last_x = cum_x[c * CHUNK + CHUNK - 1:(c + 1) * CHUNK, :]
        vend = v[rs] * jnp.exp2(last_x - cum_c)
        cum_row = jnp.sum(jnp.where(wrow == wcol, cum_c, 0.0), axis=0, keepdims=True)
        dec = jnp.exp2(jnp.where(wcol <= wrow, cum_c - cum_row, -jnp.inf))
        a_wide.append(_bdot_nt(cm[rs], _block_diag_rows(bm[rs], gmask01)) * dec)
        ds_t.append([_bdot_tn(vend[:, ls], bm[rs, ls]) for ls in groups])
    s_in = []
    s_cur = [s_ref[ls, :] for ls in groups]
    for c in range(n_chunks):
        last_full = cum_full[c * CHUNK + CHUNK - 1:(c + 1) * CHUNK, :]
        s_in.append([x.astype(BF16) for x in s_cur])
        for g in range(SSD_G):
            decay = jnp.concatenate(
                [jnp.broadcast_to(jnp.exp2(last_full[:, h * 128:(h + 1) * 128]), (HEAD, 128))
                 for h in (2 * g, 2 * g + 1)], axis=0)
            s_cur[g] = decay * s_cur[g] + ds_t[c][g]
    for g, ls in enumerate(groups):
        s_ref[ls, :] = s_cur[g]
    for c in range(n_chunks):
        rs = slice(c * CHUNK, (c + 1) * CHUNK)
        y_intra = _bdot(a_wide[c], _block_diag_rows(v[rs], hmask01))
        y_inter = [_bdot_nt(cm[rs, ls], s_in[c][g]) for g, ls in enumerate(groups)]
        y_scr[rs, :] = y_intra + jnp.concatenate(y_inter, axis=-1) * ecum_x[rs]
    y_ref[...] = _ssd_finish(y_scr[...], xs, u[:, 0:GROUP_W], d_ref, g_ref)

    @pl.when(t == pl.num_programs(1) - 1)
    def _():
        for h in range(N_HEADS):
            s_out_ref[0, h] = s_ref[h * HEAD:(h + 1) * HEAD, :]
        buf_out_ref[0] = xx_ref[pl.ds(8 - (CONV_W - 1), CONV_W - 1), :]


def _ssd_sample_kernel(u_ref, dtraw_ref, s0_ref, buf_ref, cw_ref, cb_ref, dtb_ref, alog_ref, d_ref,
                       g_ref, y_ref, s_out_ref, buf_out_ref, *, seqs):
    rows = u_ref.shape[0]
    u = u_ref[...]
    conv, new_buf = _conv_taps_seq4(u[:, GROUP_W:], buf_ref[...], cw_ref, cb_ref)
    buf_out_ref[...] = new_buf
    mask = _seg_causal_mask(rows, 2)
    xs, bm, cm, dt_x, cum_x, cum_t, cum_full = _ssd_common(u, dtraw_ref, dtb_ref, alog_ref, conv, 2)
    v = xs * dt_x
    ecum_x = jnp.exp2(cum_x)
    pick_last = (_iota((rows, rows), 1) == (_iota((rows, rows), 0) | 3)).astype(BF16)
    last_x = _sel_dot(pick_last, cum_x)
    vend = v * jnp.exp2(last_x - cum_x)
    row_seq = _iota((rows, 128), 0) >> 2
    ys = []
    for g in range(SSD_G):
        ls = slice(g * 128, (g + 1) * 128)
        q = cm[:, ls]
        k = bm[:, ls]
        y_g = _ssd_intra(q, k, v[:, ls], cum_x[:, ls], cum_t[2 * g:2 * g + 2, :], mask)
        y_int = jnp.zeros((rows, 128), F32)
        for i in range(seqs):
            in_seq = row_seq == i
            s_prev = jnp.concatenate([s0_ref[i, 2 * g], s0_ref[i, 2 * g + 1]], axis=0)
            y_int = jnp.where(in_seq, _bdot_nt(q, s_prev), y_int)
            ds = _bdot_tn(jnp.where(in_seq, vend[:, ls], 0.0), k)
            for hh in range(2):
                h = 2 * g + hh
                decay = jnp.exp2(cum_full[4 * i + 3:4 * i + 4, h * 128:(h + 1) * 128])
                s_out_ref[i, h] = decay * s0_ref[i, h] + ds[hh * HEAD:(hh + 1) * HEAD, :]
        ys.append(y_g + y_int * ecum_x[:, ls])
    y_ref[...] = _ssd_finish(jnp.concatenate(ys, axis=-1), xs, u[:, 0:GROUP_W], d_ref, g_ref)


def _ssd(u, dtraw, state, buf, cw, cb, dtb, alog, d_x, gain, n_seq, seq_len, layer, prev=None):
    n = u.shape[0]
    cch = cw.shape[1]
    vec = lambda wd: _const_spec((1, wd))
    common_in = [_const_spec(cw.shape), vec(cch), vec(128), vec(128), vec(GROUP_W), vec(GROUP_W)]
    if state is None:
        nt = seq_len // MIX_TILE
        row = lambda wd: pl.BlockSpec((MIX_TILE, wd), lambda b, t: (b * nt + t, 0))
        return pl.pallas_call(
            _ssd_prompt_kernel,
            grid=(n_seq, nt),
            in_specs=[row(u.shape[1]), row(128)] + common_in,
            out_specs=[row(GROUP_W),
                       pl.BlockSpec((1, N_HEADS, HEAD, SSD_N), lambda b, t: (b, 0, 0, 0)),
                       pl.BlockSpec((1, CONV_W - 1, cch), lambda b, t: (b, 0, 0))],
            out_shape=[jax.ShapeDtypeStruct((n, GROUP_W), F32),
                       jax.ShapeDtypeStruct((n_seq, N_HEADS, HEAD, SSD_N), F32),
                       jax.ShapeDtypeStruct((n_seq, CONV_W - 1, cch), F32)],
            scratch_shapes=[pltpu.VMEM((GROUP_W, SSD_N), F32), pltpu.VMEM((MIX_TILE + 8, cch), F32),
                            pltpu.VMEM((MIX_TILE, GROUP_W), F32)],
            compiler_params=_params(("parallel", "arbitrary")),
            name="ssd_prompt",
        )(u, dtraw, cw, cb, dtb, alog, d_x, gain)
    rows = SAMPLE_SEQS * seq_len
    row = lambda wd: pl.BlockSpec((rows, wd), lambda i: (i, 0))
    st = pl.BlockSpec((None, SAMPLE_SEQS, N_HEADS, HEAD, SSD_N), lambda i: (layer, i, 0, 0, 0))
    return _stack_call(
        functools.partial(_ssd_sample_kernel, seqs=SAMPLE_SEQS), prev,
        {1: ((SAMPLE_SEQS, N_HEADS, HEAD, SSD_N), lambda i: (i, 0, 0, 0))}, layer,
        in_specs=[row(u.shape[1]), row(128), st, row(cch)] + common_in,
        args=(u, dtraw, state, buf, cw, cb, dtb, alog, d_x, gain),
        grid=(n_seq // SAMPLE_SEQS,),
        out_specs=[row(GROUP_W), None, row(cch)],
        out_shape=[jax.ShapeDtypeStruct((n, GROUP_W), F32),
                   jax.ShapeDtypeStruct(state.shape, F32),
                   jax.ShapeDtypeStruct((n, cch), F32)],
        compiler_params=_params(("parallel",)),
        name="ssd_sample",
    )


def _rope_split(x, cos, sin_signed):
    swapped = jnp.concatenate([x[:, 128:], x[:, :128]], axis=1)
    return x * cos + swapped * sin_signed


def _ret_qkv(u_ref, cos_ref, sin_ref):
    u = u_ref[...]
    q = _rope_split(u[:, 0:GROUP_W], cos_ref[...], sin_ref[...])
    k = _rope_split(u[:, GROUP_W:2 * GROUP_W], cos_ref[...], sin_ref[...]) * (HEAD ** -0.5)
    return q, k, u[:, 2 * GROUP_W:3 * GROUP_W], u[:, 3 * GROUP_W:4 * GROUP_W]


def _ret_prompt_kernel(u_ref, cos_ref, sin_ref, ecum_ref, eend_ref, elast_ref, dec_ref, g_ref,
                       y_ref, s_out_ref, s_ref, y_scr):
    t = pl.program_id(1)
    rows = u_ref.shape[0]

    @pl.when(t == 0)
    def _():
        s_ref[...] = jnp.zeros(s_ref.shape, F32)

    q, k, v, gate = _ret_qkv(u_ref, cos_ref, sin_ref)
    r_i = _iota((GROUP_W, GROUP_W), 0)
    c_i = _iota((GROUP_W, GROUP_W), 1)
    kmask01 = ((r_i >> 6) == ((c_i & 127) >> 5)).astype(BF16)
    vmask01 = _head_block_mask().astype(BF16)
    smask = ((r_i & 127) >> 5) == (c_i >> 6)
    chunks = [slice(c * CHUNK, (c + 1) * CHUNK) for c in range(rows // CHUNK)]
    a_wide = [_bdot_nt(q[rs], _block_diag_rows(k[rs], kmask01)) * dec_ref[...] for rs in chunks]
    ds = [jnp.where(smask, _bdot_tn(k[rs], v[rs] * eend_ref[...]), 0.0) for rs in chunks]
    s = s_ref[...]
    s_in = []
    for d in ds:
        s_in.append(s.astype(BF16))
        s = elast_ref[...] * s + d
    s_ref[...] = s
    for rs, a, s_c in zip(chunks, a_wide, s_in):
        y_intra = _bdot(a, _block_diag_rows(v[rs], vmask01))
        y_scr[rs, :] = y_intra + _bdot(q[rs], s_c) * ecum_ref[...]
    y_ref[...] = _silu(gate) * (_head_stat(y_scr[...], True) * g_ref[...])

    @pl.when(t == pl.num_programs(1) - 1)
    def _():
        half = HEAD // 2
        for h in range(N_HEADS):
            vs = slice(h * HEAD, (h + 1) * HEAD)
            s_out_ref[0, h, 0:half, :] = s_ref[h * half:(h + 1) * half, vs]
            s_out_ref[0, h, half:HEAD, :] = s_ref[128 + h * half:128 + (h + 1) * half, vs]


def _ret_tables(block, seg):
    log_gamma = jnp.log(1.0 - jnp.exp2(-5.0 - jnp.arange(N_HEADS, dtype=F32)))
    pos = (jnp.arange(block) % seg).astype(F32)
    cum = (pos[:, None] + 1.0) * log_gamma[None, :]
    last = seg * log_gamma
    rep = lambda a: jnp.repeat(a, HEAD, axis=-1)
    same = (jnp.arange(block)[:, None] // seg) == (jnp.arange(block)[None, :] // seg)
    mask = same & (jnp.arange(block)[None, :] <= jnp.arange(block)[:, None])
    dec = jnp.exp(jnp.where(mask[None], cum.T[:, :, None] - cum.T[:, None, :], -jnp.inf))
    return (rep(jnp.exp(cum)), rep(jnp.exp(last[None, :] - cum)), rep(jnp.exp(last)[None, :]), dec)


def _ret(u, cos, sin_signed, gain, n_seq, seq_len):
    n = u.shape[0]
    vec = _const_spec((1, GROUP_W))
    nt = seq_len // MIX_TILE
    ecum, eend, elast, dec = _ret_tables(CHUNK, CHUNK)
    dec = dec.transpose(1, 0, 2).reshape(CHUNK, N_HEADS * CHUNK)
    row = lambda wd: pl.BlockSpec((MIX_TILE, wd), lambda b, t: (b * nt + t, 0))
    pos = pl.BlockSpec((MIX_TILE, GROUP_W), lambda b, t: (t, 0))
    return pl.pallas_call(
        _ret_prompt_kernel,
        grid=(n_seq, nt),
        in_specs=[row(u.shape[1]), pos, pos, _const_spec(ecum.shape), _const_spec(eend.shape), vec,
                  _const_spec(dec.shape), vec],
        out_specs=[row(GROUP_W), pl.BlockSpec((1, N_HEADS, HEAD, HEAD), lambda b, t: (b, 0, 0, 0))],
        out_shape=[jax.ShapeDtypeStruct((n, GROUP_W), F32),
                   jax.ShapeDtypeStruct((n_seq, N_HEADS, HEAD, HEAD), F32)],
        scratch_shapes=[pltpu.VMEM((GROUP_W, GROUP_W), F32), pltpu.VMEM((MIX_TILE, GROUP_W), F32)],
        compiler_params=_params(("parallel", "arbitrary")),
        name="ret_prompt",
    )(u, cos, sin_signed, ecum, eend, elast, dec, gain)


def _rg_gates(xr, wa_ref, ba_ref, wx_ref, bx_ref, lam_ref):
    r_gate = _sigmoid(_bdot(xr, wa_ref[...]) + ba_ref[...])
    i_gate = _sigmoid(_bdot(xr, wx_ref[...]) + bx_ref[...])
    log_a = (-RG_C * _softplus(-lam_ref[...])) * r_gate
    a = jnp.exp(log_a)
    b = jnp.sqrt(-jnp.tanh(log_a) * (a * a + 1.0)) * (i_gate * xr)
    return a, b


def _rg_scan(a, b, seg):
    pos = _iota(a.shape, 0) & (seg - 1)
    d = 1
    while d < seg:
        ok = pos >= d
        a_sh = jnp.where(ok, pltpu.roll(a, d, 0), 1.0)
        b_sh = jnp.where(ok, pltpu.roll(b, d, 0), 0.0)
        b = a * b_sh + b
        a = a * a_sh
        d *= 2
    return a, b


def _rg_scan_carry(a, b, h0):
    a, b = _rg_scan(a, b, 8)
    outs = []
    carry = h0
    for g in range(a.shape[0] // 8):
        h_g = b[g * 8:(g + 1) * 8] + a[g * 8:(g + 1) * 8] * carry
        outs.append(h_g)
        carry = h_g[7:8]
    return jnp.concatenate(outs, axis=0)


def _rg_prompt_kernel(u_ref, cw_ref, cb_ref, wa_ref, ba_ref, wx_ref, bx_ref, lam_ref,
                      y_ref, h_out_ref, buf_out_ref, h_ref, xx_ref):
    t = pl.program_id(1)
    rows = u_ref.shape[0]

    @pl.when(t == 0)
    def _():
        h_ref[...] = jnp.zeros(h_ref.shape, F32)

    u = u_ref[...]
    xr = _conv_taps_carry(u[:, 0:GROUP_W], xx_ref, cw_ref, cb_ref, t == 0)
    a, b = _rg_gates(xr, wa_ref, ba_ref, wx_ref, bx_ref, lam_ref)
    hseq = _rg_scan_carry(a, b, h_ref[0:1, :])
    h_ref[...] = jnp.broadcast_to(hseq[rows - 1:rows, :], h_ref.shape)
    y_ref[...] = _gelu_tanh(u[:, GROUP_W:]) * hseq

    @pl.when(t == pl.num_programs(1) - 1)
    def _():
        h_out_ref[0] = hseq[rows - 1:rows, :]
        buf_out_ref[0] = xx_ref[pl.ds(8 - (CONV_W - 1), CONV_W - 1), :]


def _rg_sample_kernel(u_ref, h0_ref, buf_ref, cw_ref, cb_ref, wa_ref, ba_ref, wx_ref, bx_ref, lam_ref,
                      y_ref, h_out_ref, buf_out_ref):
    u = u_ref[...]
    xr, new_buf = _conv_taps_seq4(u[:, 0:GROUP_W], buf_ref[...], cw_ref, cb_ref)
    buf_out_ref[...] = new_buf
    a, b = _rg_gates(xr, wa_ref, ba_ref, wx_ref, bx_ref, lam_ref)
    a_cum, h_loc = _rg_scan(a, b, 4)
    hseq = h_loc + a_cum * h0_ref[...]
    h_out_ref[...] = hseq
    y_ref[...] = _gelu_tanh(u[:, GROUP_W:]) * hseq


def _rg(u, h0_rows, buf, cw, cb, wa, ba, wx, bx, lam, n_seq, seq_len):
    n = u.shape[0]
    vec = _const_spec((1, GROUP_W))
    common_in = [_const_spec(cw.shape), vec, _const_spec(wa.shape), vec, _const_spec(wx.shape), vec, vec]
    if h0_rows is None:
        nt = seq_len // RG_TILE
        row = lambda wd: pl.BlockSpec((RG_TILE, wd), lambda b, t: (b * nt + t, 0))
        return pl.pallas_call(
            _rg_prompt_kernel,
            grid=(n_seq, nt),
            in_specs=[row(u.shape[1])] + common_in,
            out_specs=[row(GROUP_W), pl.BlockSpec((1, 1, GROUP_W), lambda b, t: (b, 0, 0)),
                       pl.BlockSpec((1, CONV_W - 1, GROUP_W), lambda b, t: (b, 0, 0))],
            out_shape=[jax.ShapeDtypeStruct((n, GROUP_W), F32),
                       jax.ShapeDtypeStruct((n_seq, 1, GROUP_W), F32),
                       jax.ShapeDtypeStruct((n_seq, CONV_W - 1, GROUP_W), F32)],
            scratch_shapes=[pltpu.VMEM((8, GROUP_W), F32), pltpu.VMEM((RG_TILE + 8, GROUP_W), F32)],
            compiler_params=_params(("parallel", "arbitrary")),
            name="rglru_prompt",
        )(u, cw, cb, wa, ba, wx, bx, lam)
    rows = SAMPLE_SEQS * seq_len
    row = lambda wd: pl.BlockSpec((rows, wd), lambda i: (i, 0))
    return pl.pallas_call(
        _rg_sample_kernel,
        grid=(n_seq // SAMPLE_SEQS,),
        in_specs=[row(u.shape[1]), row(GROUP_W), row(GROUP_W)] + common_in,
        out_specs=[row(GROUP_W)] * 3,
        out_shape=[jax.ShapeDtypeStruct((n, GROUP_W), F32)] * 3,
        compiler_params=_params(("parallel",)),
        name="rglru_sample",
    )(u, h0_rows, buf, cw, cb, wa, ba, wx, bx, lam)


def _hg_inputs(u_ref, lb_ref, seg_shift):
    u = u_ref[...]
    rows = u.shape[0]
    lb = lb_ref[...]
    fg = lb + (1.0 - lb) * _sigmoid(u[:, GROUP_W:2 * GROUP_W])
    q = _silu(u[:, 0:GROUP_W])
    k = 1.0 - fg
    v = u[:, 2 * GROUP_W:3 * GROUP_W]
    cum = _seg_cumsum(jnp.log(fg), seg_shift)
    return q, k, v, cum, u[:, 3 * GROUP_W:4 * GROUP_W]


def _hg_prompt_kernel(u_ref, lb_ref, g_ref, y_ref, s_out_ref, st_ref, y_scr):
    t = pl.program_id(1)
    rows = u_ref.shape[0]

    @pl.when(t == 0)
    def _():
        st_ref[...] = jnp.zeros(st_ref.shape, F32)

    q, k, v, cum, gate = _hg_inputs(u_ref, lb_ref, 6)
    hmask = _head_block_mask()
    hmask01 = hmask.astype(BF16)
    ones_bd = hmask01
    y_scr[...] = jnp.dot((q * k).astype(BF16), ones_bd, preferred_element_type=F32) * v
    row = _iota((rows, GROUP_W), 0)
    trow = _iota((CHUNK, GROUP_W), 0)
    tcol = _iota((CHUNK, GROUP_W), 1) & (CHUNK - 1)
    cum = cum * LOG2_E
    levels = []
    blk_last = cum
    h = 1
    while h < CHUNK:
        upper = (row & h) != 0
        qn = jnp.where(upper, q * jnp.exp2(cum - pltpu.roll(blk_last, h, 0)), 0.0)
        kn = k * jnp.exp2(blk_last - cum)
        sh = (2 * h).bit_length() - 1
        pair = ((trow >> sh) == (tcol >> sh)) & ((tcol & h) == 0)
        levels.append((pair, qn, kn))
        blk_last = jnp.where(upper, blk_last, pltpu.roll(blk_last, rows - h, 0))
        h *= 2
    q_in = q * jnp.exp2(cum)
    k_end = k * jnp.exp2(blk_last - cum)
    n_chunks = rows // CHUNK
    a_wide, ds_t = [], []
    for c in range(n_chunks):
        rs = slice(c * CHUNK, (c + 1) * CHUNK)
        a_c = jnp.zeros((CHUNK, GROUP_W), F32)
        for pair, qn, kn in levels:
            g = _bdot_nt(qn[rs], _block_diag_rows(kn[rs], hmask01))
            a_c = a_c + jnp.where(pair, g, 0.0)
        a_wide.append(a_c)
        ds_t.append(jnp.where(hmask, _bdot_tn(v[rs], k_end[rs]), 0.0))
    st = st_ref[...]
    st_in = []
    for c in range(n_chunks):
        st_in.append(st.astype(BF16))
        st = jnp.exp2(blk_last[c * CHUNK:c * CHUNK + 1, :]) * st + ds_t[c]
    st_ref[...] = st
    for c in range(n_chunks):
        rs = slice(c * CHUNK, (c + 1) * CHUNK)
        y_c = _bdot(a_wide[c], _block_diag_rows(v[rs], hmask01)) + _bdot_nt(q_in[rs], st_in[c])
        y_scr[rs, :] = y_scr[rs, :] + y_c
    y_ref[...] = _silu(gate) * (_head_stat(y_scr[...], False) * g_ref[...])

    @pl.when(t == pl.num_programs(1) - 1)
    def _():
        eye = (_iota((HEAD, HEAD), 0) == _iota((HEAD, HEAD), 1)).astype(BF16)
        for h in range(N_HEADS):
            s_out_ref[0, h] = _sel_dot_nt(eye, st_ref[h * HEAD:(h + 1) * HEAD, h * HEAD:(h + 1) * HEAD])


def _hg(u, lb, gain, n_seq, seq_len):
    n = u.shape[0]
    vec = _const_spec((1, GROUP_W))
    nt = seq_len // MIX_TILE
    row = lambda wd: pl.BlockSpec((MIX_TILE, wd), lambda b, t: (b * nt + t, 0))
    return pl.pallas_call(
        _hg_prompt_kernel,
        grid=(n_seq, nt),
        in_specs=[row(u.shape[1]), vec, vec],
        out_specs=[row(GROUP_W), pl.BlockSpec((1, N_HEADS, HEAD, HEAD), lambda b, t: (b, 0, 0, 0))],
        out_shape=[jax.ShapeDtypeStruct((n, GROUP_W), F32),
                   jax.ShapeDtypeStruct((n_seq, N_HEADS, HEAD, HEAD), F32)],
        scratch_shapes=[pltpu.VMEM((GROUP_W, GROUP_W), F32), pltpu.VMEM((MIX_TILE, GROUP_W), F32)],
        compiler_params=_params(("parallel", "arbitrary")),
        name="hgrn_prompt",
    )(u, lb, gain)


def _t_recurrence(q_scr, k_scr, v, dec_row, dec_scr, s_ref, n_seq, seq_len):
    cols = [slice(t * n_seq, (t + 1) * n_seq) for t in range(seq_len)]
    v_t = [v[:, c] for c in cols]
    outs = [jnp.zeros((HEAD, n_seq), F32) for _ in cols]
    for k in range(HEAD):
        s = s_ref[k]
        for t, c in enumerate(cols):
            dec = dec_row if dec_scr is None else dec_scr[k:k + 1, c]
            s = dec * s + k_scr[k:k + 1, c] * v_t[t]
            outs[t] = outs[t] + q_scr[k:k + 1, c] * s
        s_ref[k] = s
    return jnp.concatenate(outs, axis=1)


def _t_head_norm(o, center):
    if center:
        o = o - jnp.mean(o, axis=0, keepdims=True)
    return o * lax.rsqrt(jnp.mean(o * o, axis=0, keepdims=True) + EPS)


def _ret_t_kernel(q_ref, k_ref, v_ref, g_ref, cos_ref, sin_ref, gam_ref, gain_ref, s0_ref,
                  y_ref, s_ref, q_scr, k_scr, *, n_seq, seq_len):
    half = HEAD // 2

    def rope(x):
        swapped = jnp.concatenate([x[half:], x[:half]], axis=0)
        return x * cos_ref[...] + swapped * sin_ref[...]

    q_scr[...] = rope(q_ref[...])
    k_scr[...] = rope(k_ref[...]) * (HEAD ** -0.5)
    s_ref[0] = s0_ref[0]
    o = _t_recurrence(q_scr, k_scr, v_ref[...], gam_ref[0], None, s_ref.at[0], n_seq, seq_len)
    gain = jnp.concatenate([gain_ref[...]] * seq_len, axis=1)
    parts = [_t_head_norm(o[:, t * n_seq:(t + 1) * n_seq], True) for t in range(seq_len)]
    y_ref[...] = _silu(g_ref[...]) * (jnp.concatenate(parts, axis=1) * gain)


def _hg_t_kernel(q_ref, f_ref, i_ref, g_ref, lb_ref, gain_ref, s0_ref, y_ref, s_ref, q_scr, k_scr, f_scr,
                 *, n_seq, seq_len):
    lb = jnp.concatenate([lb_ref[...]] * seq_len, axis=1)
    fg = lb + (1.0 - lb) * _sigmoid(f_ref[...])
    q_scr[...] = _silu(q_ref[...])
    k_scr[...] = 1.0 - fg
    f_scr[...] = fg
    s_ref[0] = s0_ref[0]
    o = _t_recurrence(q_scr, k_scr, i_ref[...], None, f_scr, s_ref.at[0], n_seq, seq_len)
    gain = jnp.concatenate([gain_ref[...]] * seq_len, axis=1)
    parts = [_t_head_norm(o[:, t * n_seq:(t + 1) * n_seq], False) for t in range(seq_len)]
    y_ref[...] = _silu(g_ref[...]) * (jnp.concatenate(parts, axis=1) * gain)


def _t_specs(n, n_seq):
    head_rows = lambda section: pl.BlockSpec((HEAD, n), lambda h, s=section: (N_HEADS * s + h, 0))
    per_head = pl.BlockSpec((HEAD, n_seq), lambda h: (h, 0))
    state_in = lambda layer: pl.BlockSpec((None, 1, HEAD, HEAD, n_seq), lambda h: (layer, h, 0, 0, 0))
    state_out = ((1, HEAD, HEAD, n_seq), lambda h: (h, 0, 0, 0))
    return head_rows, per_head, state_in, state_out


def _ret_t(u_t, cos_t, sin_t, gam, gain_b, state_t, n_seq, seq_len, layer, prev):
    n = u_t.shape[1]
    head_rows, per_head, state_in, state_out = _t_specs(n, n_seq)
    return _stack_call(
        functools.partial(_ret_t_kernel, n_seq=n_seq, seq_len=seq_len), prev, {1: state_out}, layer,
        in_specs=[head_rows(0), head_rows(1), head_rows(2), head_rows(3), _const_spec(cos_t.shape),
                  _const_spec(sin_t.shape), pl.BlockSpec((1, 1, n_seq), lambda h: (h, 0, 0)), per_head,
                  state_in(layer)],
        args=(u_t, u_t, u_t, u_t, cos_t, sin_t, gam, gain_b, state_t),
        grid=(N_HEADS,),
        out_specs=[pl.BlockSpec((HEAD, n), lambda h: (h, 0)), None],
        out_shape=[jax.ShapeDtypeStruct((GROUP_W, n), F32), jax.ShapeDtypeStruct(state_t.shape, F32)],
        scratch_shapes=[pltpu.VMEM((HEAD, n), F32)] * 2,
        compiler_params=_params(("parallel",)),
        name="ret_sample_t",
    )


def _hg_t(u_t, lb_b, gain_b, state_t, n_seq, seq_len, layer, prev):
    n = u_t.shape[1]
    head_rows, per_head, state_in, state_out = _t_specs(n, n_seq)
    return _stack_call(
        functools.partial(_hg_t_kernel, n_seq=n_seq, seq_len=seq_len), prev, {1: state_out}, layer,
        in_specs=[head_rows(0), head_rows(1), head_rows(2), head_rows(3), per_head, per_head, state_in(layer)],
        args=(u_t, u_t, u_t, u_t, lb_b, gain_b, state_t),
        grid=(N_HEADS,),
        out_specs=[pl.BlockSpec((HEAD, n), lambda h: (h, 0)), None],
        out_shape=[jax.ShapeDtypeStruct((GROUP_W, n), F32), jax.ShapeDtypeStruct(state_t.shape, F32)],
        scratch_shapes=[pltpu.VMEM((HEAD, n), F32)] * 3,
        compiler_params=_params(("parallel",)),
        name="hgrn_sample_t",
    )


def _rope_tables_t(pos0, seq_len, n_seq):
    half = HEAD // 2
    pos = pos0 + jnp.arange(seq_len, dtype=F32)
    inv = ROPE_BASE ** (-jnp.arange(half, dtype=F32) / half)
    ang = pos[:, None] * inv
    cos = jnp.tile(jnp.cos(ang).T, (2, 1))
    sin = jnp.sin(ang).T
    sin_signed = jnp.concatenate([-sin, sin], axis=0)
    return jnp.repeat(cos, n_seq, axis=1), jnp.repeat(sin_signed, n_seq, axis=1)


def _block_diag(w):
    h, i, j = w.shape
    eye = jnp.eye(h, dtype=w.dtype)
    return (eye[:, None, :, None] * w[:, :, None, :]).reshape(h * i, h * j)


def _rope_tables(pos0, seq_len):
    half = HEAD // 2
    pos = pos0 + jnp.arange(seq_len, dtype=F32)
    inv = ROPE_BASE ** (-jnp.arange(half, dtype=F32) / half)
    ang = pos[:, None] * inv
    cos = jnp.tile(jnp.cos(ang), (1, 2 * N_HEADS))
    sin = jnp.sin(ang)
    sin_signed = jnp.concatenate([jnp.tile(-sin, (1, N_HEADS)), jnp.tile(sin, (1, N_HEADS))], axis=-1)
    return cos, sin_signed


def _layer_weights(P, l):
    row = lambda a: a.reshape(1, -1)
    pad128 = lambda a: jnp.pad(a, (0, 128 - a.shape[0])).reshape(1, 128)
    lb_sm = jax.nn.softmax(P['hg_lb'].astype(F32), axis=0)
    lb = (jnp.cumsum(lb_sm, axis=0) - lb_sm[0])[l]
    return dict(
        w_in=P['w_in_prepped'], w_in_t=P['w_in_t'],
        ln_mix_pre=row(P['ln_mix_pre'][l]), ln_mix_post=row(P['ln_mix_post'][l]),
        ln_xa_pre=row(P['ln_xa_pre'][l]), ln_xa_post=row(P['ln_xa_post'][l]),
        ln_ffn_pre=row(P['ln_ffn_pre'][l]), ln_ffn_post=row(P['ln_ffn_post'][l]),
        ssd_cw=P['ssd_conv_w'][l], ssd_cb=row(P['ssd_conv_b'][l]),
        ssd_dtb=pad128(P['ssd_dt_bias'][l]), ssd_alog=pad128(P['ssd_A_log'][l]),
        ssd_d=row(jnp.repeat(P['ssd_D'][l], HEAD)), ssd_norm=row(P['ssd_norm'][l]),
        ret_norm=row(P['ret_norm'][l]),
        rg_cw=P['rg_conv_w'][l], rg_cb=row(P['rg_conv_b'][l]),
        rg_wa=_block_diag(P['rg_wa'][l]).astype(BF16), rg_ba=row(P['rg_ba'][l]),
        rg_wx=_block_diag(P['rg_wx'][l]).astype(BF16), rg_bx=row(P['rg_bx'][l]),
        rg_lam=row(P['rg_lambda'][l]),
        hg_lb=row(lb), hg_norm=row(P['hg_norm'][l]),
        w_out=P['w_out_bf16'], w_xq=P['w_xq_bf16'], w_xo=P['w_xo_bf16'], w_gu=P['w_gu_bf16'],
        w_down=P['w_down_bf16'], ln_mem=row(P['ln_mem'][l]), w_xkv=P['w_xkv_bf16'],
    )


def _layer(x, W, k_mem, v_mem, st, n_seq, seq_len, pos0, layer, prev=None):
    if st is None:
        u_ssd, u_ret, u_rg, u_hg, u_dt = _norm_matmul(
            x, W['ln_mix_pre'], W['w_in'], (IN_COLS_SSD, IN_COLS_RET_SPLIT, IN_COLS_RG, IN_COLS_HG, IN_COLS_DT),
            "in_proj", layer=layer)
        cos, sin_signed = _rope_tables(pos0, seq_len)
        y_ssd, s_ssd, b_ssd = _ssd(u_ssd, u_dt, None, None, W['ssd_cw'], W['ssd_cb'], W['ssd_dtb'],
                                   W['ssd_alog'], W['ssd_d'], W['ssd_norm'], n_seq, seq_len, layer)
        y_ret, s_ret = _ret(u_ret, cos, sin_signed, W['ret_norm'], n_seq, seq_len)
        y_rg, h_rg, b_rg = _rg(u_rg, None, None, W['rg_cw'], W['rg_cb'], W['rg_wa'], W['rg_ba'],
                               W['rg_wx'], W['rg_bx'], W['rg_lam'], n_seq, seq_len)
        h_rg = h_rg.reshape(n_seq, GROUP_W)
        y_hg, s_hg = _hg(u_hg, W['hg_lb'], W['hg_norm'], n_seq, seq_len)
    else:
        ssd_s, ssd_buf, ret_s, rg_h, rg_buf, hg_s = st
        u_ssd, u_rg, u_dt = _norm_matmul(x, W['ln_mix_pre'], W['w_in'], (IN_COLS_SSD, IN_COLS_RG, IN_COLS_DT),
                                         "in_proj", layer=layer)
        to_tb = lambda a: a.reshape(n_seq, seq_len, -1).transpose(1, 0, 2).reshape(n_seq * seq_len, -1)
        from_t = lambda a: a.reshape(-1, seq_len, n_seq).transpose(2, 1, 0).reshape(n_seq * seq_len, -1)
        bcast = lambda g: jnp.broadcast_to(g.reshape(-1, 1), (g.size, n_seq))
        ut_ret, ut_hg = _in_proj_t(to_tb(x), W['ln_mix_pre'], W['w_in_t'], layer)
        cos_t, sin_t = _rope_tables_t(pos0, seq_len, n_seq)
        log_gamma = jnp.log(1.0 - jnp.exp2(-5.0 - jnp.arange(N_HEADS, dtype=F32)))
        gam = jnp.broadcast_to(jnp.exp(log_gamma)[:, None, None], (N_HEADS, 1, n_seq))
        yt_ret, s_ret = _ret_t(ut_ret, cos_t, sin_t, gam, bcast(W['ret_norm']), ret_s, n_seq, seq_len, layer,
                               prev and (prev[1],))
        yt_hg, s_hg = _hg_t(ut_hg, bcast(W['hg_lb']), bcast(W['hg_norm']), hg_s, n_seq, seq_len, layer,
                            prev and (prev[2],))
        y_ret, y_hg = from_t(yt_ret), from_t(yt_hg)
        pad_rows = lambda b: jnp.pad(b, ((0, 0), (0, 1), (0, 0))).reshape(n_seq * 4, b.shape[-1])
        y_ssd, s_ssd, b_ssd = _ssd(u_ssd, u_dt, ssd_s, pad_rows(ssd_buf), W['ssd_cw'], W['ssd_cb'],
                                   W['ssd_dtb'], W['ssd_alog'], W['ssd_d'], W['ssd_norm'], n_seq, seq_len, layer,
                                   prev and (prev[0],))
        b_ssd = b_ssd.reshape(n_seq, 4, -1)[:, :CONV_W - 1]
        y_rg, h_rows, b_rg = _rg(u_rg, jnp.repeat(rg_h, seq_len, axis=0), pad_rows(rg_buf), W['rg_cw'],
                                 W['rg_cb'], W['rg_wa'], W['rg_ba'], W['rg_wx'], W['rg_bx'], W['rg_lam'],
                                 n_seq, seq_len)
        h_rg = h_rows.reshape(n_seq, seq_len, GROUP_W)[:, seq_len - 1]
        b_rg = b_rg.reshape(n_seq, 4, -1)[:, :CONV_W - 1]
    ys = (y_ssd, y_ret, y_rg, y_hg)
    if st is None:
        x3 = _xattn_ffn(ys, x, W['w_out'], W['ln_mix_post'], k_mem, v_mem, W['ln_xa_pre'], W['w_xq'], W['w_xo'],
                        W['ln_xa_post'], W['ln_ffn_pre'], W['w_gu'], W['w_down'], W['ln_ffn_post'], seq_len,
                        layer)
    else:
        x1, q = _out_proj(ys, x, W['w_out'], W['ln_mix_post'], W['ln_xa_pre'], W['w_xq'], layer)
        o = _attention_sample(q, k_mem, v_mem, layer, seq_len)
        x3 = _ffn(o, x1, W['w_xo'], W['ln_xa_post'], W['ln_ffn_pre'], W['w_gu'], W['w_down'],
                  W['ln_ffn_post'], layer)
    return x3, (s_ssd, b_ssd, s_ret, h_rg, b_rg, s_hg)


def kernel(x_prompt, x_sample, state_ssd, state_ssd_conv, state_ret, state_rglru, state_rglru_conv, state_hgrn, cache_mem_k, cache_mem_v, mem_prompt, ln_mix_pre, ln_mix_post, ln_xa_pre, ln_xa_post, ln_ffn_pre, ln_ffn_post, w_in, ssd_conv_w, ssd_conv_b, ssd_dt_bias, ssd_A_log, ssd_D, ssd_norm, ret_norm, rg_conv_w, rg_conv_b, rg_wa, rg_ba, rg_wx, rg_bx, rg_lambda, hg_lb, hg_norm, w_out, ln_mem, w_xq, w_xkv, w_xo, w_gu, w_down):
    P = dict(ln_mix_pre=ln_mix_pre, ln_mix_post=ln_mix_post, ln_xa_pre=ln_xa_pre, ln_xa_post=ln_xa_post,
             ln_ffn_pre=ln_ffn_pre, ln_ffn_post=ln_ffn_post, w_in=w_in, ssd_conv_w=ssd_conv_w,
             ssd_conv_b=ssd_conv_b, ssd_dt_bias=ssd_dt_bias, ssd_A_log=ssd_A_log, ssd_D=ssd_D,
             ssd_norm=ssd_norm, ret_norm=ret_norm, rg_conv_w=rg_conv_w, rg_conv_b=rg_conv_b,
             rg_wa=rg_wa, rg_ba=rg_ba, rg_wx=rg_wx, rg_bx=rg_bx, rg_lambda=rg_lambda, hg_lb=hg_lb,
             hg_norm=hg_norm, w_out=w_out, ln_mem=ln_mem, w_xq=w_xq, w_xkv=w_xkv, w_xo=w_xo,
             w_gu=w_gu, w_down=w_down)
    P['w_in_prepped'], P['w_in_t'] = _in_weight_prep(w_in)
    for name in ('w_out', 'w_xq', 'w_xo', 'w_gu', 'w_down', 'w_xkv'):
        P[name + '_bf16'] = P[name].astype(BF16)
    ssd_t = jnp.swapaxes(state_ssd, -1, -2)
    ret_t = jnp.transpose(state_ret, (0, 2, 3, 4, 1))
    hgrn_t = jnp.transpose(state_hgrn, (0, 2, 3, 4, 1))
    depth = w_in.shape[0]
    bp, tp, d = x_prompt.shape
    bs, ts, _ = x_sample.shape
    n_mem = mem_prompt.shape[1]
    y_p = x_prompt.reshape(bp * tp, d)
    y_s = x_sample.reshape(bs * ts, d)
    mem = mem_prompt.reshape(bp * n_mem, d)
    p_st, s_st = [], []
    mem_prev = samp_prev = None
    for l in range(depth):
        W = _layer_weights(P, l)
        mk, mv, mk5, mv5 = _mem_kv(mem, W['ln_mem'], W['w_xkv'], l, n_mem, mem_prev)
        mem_prev = (mk5, mv5)
        mk = mk.reshape(bp, n_mem, d)
        mv = mv.reshape(bp, n_mem, d)
        y_p, st_p = _layer(y_p, W, mk, mv, None, bp, tp, 0.0, l)
        p_st.append(st_p)
        st_l = (ssd_t, state_ssd_conv[l], ret_t, state_rglru[l], state_rglru_conv[l], hgrn_t)
        y_s, st_s = _layer(y_s, W, cache_mem_k, cache_mem_v, st_l, bs, ts, float(PAST_LEN), l, samp_prev)
        samp_prev = (st_s[0], st_s[2], st_s[5])
        s_st.append(st_s)
    stack = lambda sts, i: jnp.stack([s[i] for s in sts], axis=0)
    return (y_p.reshape(bp, tp, d), y_s.reshape(bs, ts, d),
            jnp.swapaxes(stack(p_st, 0), -1, -2), stack(p_st, 1), stack(p_st, 2), stack(p_st, 3), stack(p_st, 4),
            stack(p_st, 5),
            mem_prev[0], mem_prev[1],
            jnp.swapaxes(samp_prev[0], -1, -2), stack(s_st, 1), jnp.transpose(samp_prev[1], (0, 4, 1, 2, 3)),
            stack(s_st, 3), stack(s_st, 4), jnp.transpose(samp_prev[2], (0, 4, 1, 2, 3)))
```

```python
import functools
import math

import jax
import jax.numpy as jnp
from jax import lax
from jax.experimental import pallas as pl
from jax.experimental.pallas import tpu as pltpu

F32 = jnp.float32
BF16 = jnp.bfloat16
EPS = 1e-6
LOG2_E = 1.4426950408889634

GROUP_W = 256
HEAD = 64
N_HEADS = 4
SSD_N = 128
SSD_G = 2
CONV_W = 4
XA_H = 4
XA_HD = 256
ROPE_BASE = 10000.0
RG_C = 8.0
CHUNK = 64
PAST_LEN = 16384

ROW_TILE = 512
XA_TILE = 1024
IN_TILE = 1024
MIX_TILE = 2048
RG_TILE = 2048
SAMPLE_SEQS = 16
ATT_SEQS = 8
FF_TILE = 256
VMEM_LIMIT = 56 * 1024 * 1024
XA_VMEM_LIMIT = 62 * 1024 * 1024


def _bdot(a, b):
    return jnp.dot(a.astype(BF16), b.astype(BF16), preferred_element_type=F32)


def _bdot_nt(a, b):
    return lax.dot_general(a.astype(BF16), b.astype(BF16), (((1,), (1,)), ((), ())),
                           preferred_element_type=F32)


def _bdot_tn(a, b):
    return lax.dot_general(a.astype(BF16), b.astype(BF16), (((0,), (0,)), ((), ())),
                           preferred_element_type=F32)


def _split3(x):
    hi = x.astype(BF16)
    r = x - hi.astype(F32)
    mid = r.astype(BF16)
    lo = (r - mid.astype(F32)).astype(BF16)
    return hi, mid, lo


def _sel_dot(sel, x):
    hi, mid, lo = _split3(x)
    d = lambda y: jnp.dot(sel, y, preferred_element_type=F32)
    return (d(hi) + d(mid)) + d(lo)


def _dot_sel(x, sel):
    hi, mid, lo = _split3(x)
    d = lambda y: jnp.dot(y, sel, preferred_element_type=F32)
    return (d(hi) + d(mid)) + d(lo)


def _sel_dot_nt(sel, x):
    hi, mid, lo = _split3(x)
    d = lambda y: lax.dot_general(sel, y, (((1,), (1,)), ((), ())), preferred_element_type=F32)
    return (d(hi) + d(mid)) + d(lo)


def _rms(x, g):
    return x * lax.rsqrt(jnp.mean(x * x, axis=-1, keepdims=True) + EPS) * g


def _sigmoid(x):
    return jax.nn.sigmoid(x)


def _silu(x):
    return x * jax.nn.sigmoid(x)


def _softplus(x):
    return jnp.maximum(x, 0.0) + jnp.log1p(jnp.exp(-jnp.abs(x)))


def _gelu_tanh(x):
    c = math.sqrt(2.0 / math.pi)
    return 0.5 * x * (1.0 + jnp.tanh(c * (x + 0.044715 * (x * x * x))))


def _iota(shape, dim):
    return lax.broadcasted_iota(jnp.int32, shape, dim)


def _seg_causal_mask(n, seg_shift):
    r = _iota((n, n), 0)
    c = _iota((n, n), 1)
    return ((r >> seg_shift) == (c >> seg_shift)) & (c <= r)


def _seg_cumsum(x, seg_shift):
    rows = x.shape[0]
    blk = min(rows, 256)
    mask = _seg_causal_mask(blk, seg_shift).astype(BF16)
    return jnp.concatenate([_sel_dot(mask, x[i:i + blk]) for i in range(0, rows, blk)], axis=0)


def _head_expand_mat(width_in=128):
    r = _iota((width_in, GROUP_W), 0)
    c = _iota((width_in, GROUP_W), 1)
    return ((c >> 6) == r).astype(BF16)


def _head_block_mask():
    return (_iota((GROUP_W, GROUP_W), 0) >> 6) == (_iota((GROUP_W, GROUP_W), 1) >> 6)


def _block_diag_rows(x, mask01):
    return jnp.concatenate([x.astype(BF16)] * N_HEADS, axis=0) * mask01


def _head_stat(y, center):
    ones_bd = _head_block_mask().astype(BF16)
    if center:
        y = y - _dot_sel(y, ones_bd) * (1.0 / HEAD)
    return y * lax.rsqrt(_dot_sel(y * y, ones_bd) * (1.0 / HEAD) + EPS)


def _conv_taps_carry(x, xx_ref, w_ref, b_ref, first):
    rows = x.shape[0]

    @pl.when(first)
    def _():
        xx_ref[0:8, :] = jnp.zeros((8, x.shape[1]), F32)

    xx_ref[8:8 + rows, :] = x
    y = b_ref[...] + w_ref[CONV_W - 1:CONV_W, :] * x
    for j in range(CONV_W - 1):
        y = y + w_ref[j:j + 1, :] * xx_ref[pl.ds(8 - (CONV_W - 1) + j, rows), :]
    xx_ref[0:8, :] = xx_ref[rows:rows + 8, :]
    return y


def _conv_taps_seq4(x, buf4, w_ref, b_ref):
    rows = x.shape[0]
    r = _iota((rows, rows), 0)
    c = _iota((rows, rows), 1)
    t = r & 3
    y = b_ref[...] + w_ref[CONV_W - 1:CONV_W, :] * x
    for d in range(1, CONV_W):
        shift = ((c == r - d) & (t >= d)).astype(BF16)
        hist = ((t < d) & (c == (r - t) + (3 + t - d))).astype(BF16)
        y = y + w_ref[CONV_W - 1 - d:CONV_W - d, :] * (_sel_dot(shift, x) + _sel_dot(hist, buf4))
    new_sel = (((r & 3) < 3) & (c == r + 1)).astype(BF16)
    return y, _sel_dot(new_sel, x)


def _norm_matmul_kernel(x_ref, g_ref, w_ref, *o_refs, col_ranges):
    h = _rms(x_ref[...], g_ref[...]).astype(BF16)
    for o_ref, pieces in zip(o_refs, col_ranges):
        off = 0
        for a, b in pieces:
            o_ref[:, off:off + b - a] = jnp.dot(h, w_ref[:, a:b], preferred_element_type=F32)
            off += b - a


def _const_spec(shape):
    nd = len(shape)
    return pl.BlockSpec(shape, lambda *_: (0,) * nd, pipeline_mode=pl.Buffered(1))


def _layer_spec(shape, layer):
    nd = len(shape)
    return pl.BlockSpec((None,) + tuple(shape[1:]), lambda *_: (layer,) + (0,) * (nd - 1),
                        pipeline_mode=pl.Buffered(1))


def _stack_call(kernel, prevs, stacked, layer, *, in_specs, args, out_specs, out_shape, **kw):
    n_in = len(in_specs)
    out_specs = list(out_specs)
    order = sorted(stacked)
    for o in order:
        blk, idx = stacked[o]
        if prevs is None:
            depth = out_shape[o].shape[0]
            out_specs[o] = pl.BlockSpec((depth,) + tuple(blk), lambda *g, idx=idx: (0,) + tuple(idx(*g)))
        else:
            out_specs[o] = pl.BlockSpec((None,) + tuple(blk), lambda *g, idx=idx: (layer,) + tuple(idx(*g)))
    if prevs is None:
        def body(*refs):
            refs = list(refs)
            for o in order:
                full = refs[n_in + o]
                for d in range(full.shape[0]):
                    if d != layer:
                        full[d] = jnp.zeros(full.shape[1:], full.dtype)
                refs[n_in + o] = full.at[layer]
            return kernel(*refs)

        return pl.pallas_call(body, in_specs=in_specs, out_specs=out_specs, out_shape=out_shape, **kw)(*args)

    def body(*refs):
        return kernel(*refs[:n_in], *refs[n_in + len(prevs):])

    return pl.pallas_call(
        body,
        in_specs=list(in_specs) + [pl.BlockSpec(memory_space=pl.ANY)] * len(prevs),
        out_specs=out_specs, out_shape=out_shape,
        input_output_aliases={n_in + j: o for j, o in enumerate(order)},
        **kw)(*args, *prevs)


def _params(sem, vmem=VMEM_LIMIT):
    return pltpu.CompilerParams(dimension_semantics=sem, vmem_limit_bytes=vmem)


def _in_weight_prep_kernel(wt_ref, o_ref, ot_ref, *, blocks, dt_block, dt_row, dt_n, t_rows):
    off = 0
    for src, n in t_rows:
        ot_ref[off:off + n, :] = wt_ref[pl.ds(src, n), :].astype(BF16)
        off += n
    for j, pieces in enumerate(blocks):
        if j == dt_block:
            dt = wt_ref[pl.ds(dt_row, 8), :]
            dt = jnp.where(_iota(dt.shape, 0) < dt_n, dt, 0.0)
            blk = jnp.concatenate([dt, jnp.zeros((120, dt.shape[1]), F32)], axis=0)
        else:
            blk = jnp.concatenate([wt_ref[pl.ds(src, n), :] for src, n in pieces], axis=0)
        o_ref[:, j * 128:(j + 1) * 128] = blk.T.astype(BF16)


IN_COLS_SSD = ((0, 1024),)
IN_COLS_RG = ((2048, 2560),)
IN_COLS_HG = ((2560, 3584),)
IN_COLS_DT = ((3584, 3712),)
IN_COLS_RET_SPLIT = ((3712, 4224), (1536, 2048))


def _in_weight_prep(w_in):
    depth, d, d_in = w_in.shape
    z0 = GROUP_W + (GROUP_W + 2 * SSD_G * SSD_N)
    d0 = z0 + N_HEADS
    blocks = [((src, 128),) for src in tuple(range(0, z0, 128)) + tuple(range(d0, d_in, 128))]
    dt_block = len(blocks)
    blocks.append(())
    half = HEAD // 2
    for base in (d0, d0 + GROUP_W):
        for j in range(2):
            blocks.append(tuple((base + h * HEAD + j * half, half) for h in range(N_HEADS)))
    n_out = len(blocks) * 128
    t_rows = ((d0, 4 * GROUP_W), (d0 + 6 * GROUP_W, 4 * GROUP_W))
    n_t = sum(n for _, n in t_rows)
    wt = jnp.swapaxes(w_in, 1, 2)
    return pl.pallas_call(
        functools.partial(_in_weight_prep_kernel, blocks=tuple(blocks), dt_block=dt_block, dt_row=z0,
                          dt_n=N_HEADS, t_rows=t_rows),
        grid=(depth,),
        in_specs=[pl.BlockSpec((None, d_in, d), lambda l: (l, 0, 0), pipeline_mode=pl.Buffered(1))],
        out_specs=[pl.BlockSpec((None, d, n_out), lambda l: (l, 0, 0)),
                   pl.BlockSpec((None, n_t, d), lambda l: (l, 0, 0))],
        out_shape=[jax.ShapeDtypeStruct((depth, d, n_out), BF16), jax.ShapeDtypeStruct((depth, n_t, d), BF16)],
        compiler_params=_params(("parallel",)),
        name="in_weight_prep",
    )(wt)


def _in_proj_t_kernel(x_ref, g_ref, wt_ref, *o_refs):
    h = _rms(x_ref[...], g_ref[...]).astype(BF16)
    off = 0
    for o_ref in o_refs:
        f = o_ref.shape[0]
        o_ref[...] = lax.dot_general(wt_ref[off:off + f, :], h, (((1,), (1,)), ((), ())),
                                     preferred_element_type=F32)
        off += f


def _in_proj_t(x, g, wt, layer):
    n, d = x.shape
    f = wt.shape[1] // 2
    return pl.pallas_call(
        _in_proj_t_kernel,
        grid=(1,),
        in_specs=[_const_spec((n, d)), _const_spec((1, d)), _layer_spec(wt.shape, layer)],
        out_specs=[_const_spec((f, n))] * 2,
        out_shape=[jax.ShapeDtypeStruct((f, n), F32)] * 2,
        compiler_params=_params(("arbitrary",)),
        name="in_proj_t",
    )(x, g, wt)


def _mem_kv_kernel(x_ref, g_ref, w_ref, k_ref, v_ref, k5_ref, v5_ref):
    h = _rms(x_ref[...], g_ref[...]).astype(BF16)
    d = x_ref.shape[1]
    k = jnp.dot(h, w_ref[:, 0:d], preferred_element_type=F32)
    v = jnp.dot(h, w_ref[:, d:2 * d], preferred_element_type=F32)
    k_ref[...] = k
    v_ref[...] = v
    k5_ref[...] = k.reshape(k5_ref.shape)
    v5_ref[...] = v.reshape(v5_ref.shape)


def _mem_kv(mem, g, w, layer, n_mem, prev):
    n, d = mem.shape
    depth = w.shape[0]
    tm = min(ROW_TILE, n)
    seqs = tm // n_mem
    row = pl.BlockSpec((tm, d), lambda i: (i, 0))
    blk5 = ((seqs, n_mem, XA_H, XA_HD), lambda i: (i, 0, 0, 0))
    shape5 = jax.ShapeDtypeStruct((depth, n // n_mem, n_mem, XA_H, XA_HD), F32)
    return _stack_call(
        _mem_kv_kernel, prev, {2: blk5, 3: blk5}, layer,
        in_specs=[row, _const_spec((1, d)), _wspec(w, layer)],
        args=(mem, g, w),
        grid=(n // tm,),
        out_specs=[row, row, None, None],
        out_shape=[jax.ShapeDtypeStruct((n, d), F32)] * 2 + [shape5] * 2,
        compiler_params=_params(("parallel",)),
        name="mem_kv",
    )


def _wspec(w, layer):
    return _layer_spec(w.shape, layer) if w.ndim == 3 else _const_spec(w.shape)


def _norm_matmul(x, g, w, col_ranges, name, layer=None):
    n, d = x.shape
    tm = min(IN_TILE, n)
    w_spec = _wspec(w, layer)
    widths = [sum(b - a for a, b in pieces) for pieces in col_ranges]
    return pl.pallas_call(
        functools.partial(_norm_matmul_kernel, col_ranges=col_ranges),
        grid=(n // tm,),
        in_specs=[pl.BlockSpec((tm, d), lambda i: (i, 0)), _const_spec((1, d)), w_spec],
        out_specs=[pl.BlockSpec((tm, wd), lambda i: (i, 0)) for wd in widths],
        out_shape=[jax.ShapeDtypeStruct((n, wd), F32) for wd in widths],
        compiler_params=_params(("parallel",)),
        name=name,
    )(x, g, w)


def _out_proj_kernel(ys_ref, yr_ref, yg_ref, yh_ref, x_ref, wo_ref, gpost_ref, gpre_ref, wq_ref, x1_ref, q_ref):
    x1 = _mix_out(ys_ref, yr_ref, yg_ref, yh_ref, x_ref, wo_ref, gpost_ref)
    x1_ref[...] = x1
    q_ref[...] = _bdot(_rms(x1, gpre_ref[...]), wq_ref[...])


def _out_proj(ys, x, w_out, g_post, g_pre, w_xq, layer):
    n, d = x.shape
    tm = min(ROW_TILE, n)
    row = lambda wd: pl.BlockSpec((tm, wd), lambda i: (i, 0))
    vec = _const_spec((1, d))
    return pl.pallas_call(
        _out_proj_kernel,
        grid=(n // tm,),
        in_specs=[row(GROUP_W)] * 4 + [row(d), _wspec(w_out, layer), vec, vec, _wspec(w_xq, layer)],
        out_specs=[row(d)] * 2,
        out_shape=[jax.ShapeDtypeStruct((n, d), F32)] * 2,
        compiler_params=_params(("parallel",)),
        name="out_proj",
    )(*ys, x, w_out, g_post, g_pre, w_xq)


def _ffn_tail(o, x1, wxo_ref, gxa_ref, gpre_ref, wgu_ref, wdn_ref, gpost_ref, d_ff):
    a = _bdot(o, wxo_ref[...])
    x2 = x1 + _rms(a, gxa_ref[...])
    h = _rms(x2, gpre_ref[...]).astype(BF16)
    acc = jnp.zeros(x2.shape, F32)
    for j in range(0, d_ff, FF_TILE):
        g = jnp.dot(h, wgu_ref[:, j:j + FF_TILE], preferred_element_type=F32)
        u = jnp.dot(h, wgu_ref[:, d_ff + j:d_ff + j + FF_TILE], preferred_element_type=F32)
        act = (_silu(g) * u).astype(BF16)
        acc = acc + jnp.dot(act, wdn_ref[j:j + FF_TILE, :], preferred_element_type=F32)
    return x2 + _rms(acc, gpost_ref[...])


def _ffn_kernel(o_ref, x1_ref, wxo_ref, gxa_ref, gpre_ref, wgu_ref, wdn_ref, gpost_ref, x3_ref, *, d_ff):
    x3_ref[...] = _ffn_tail(o_ref[...], x1_ref[...], wxo_ref, gxa_ref, gpre_ref, wgu_ref, wdn_ref, gpost_ref,
                            d_ff)


def _ffn(o, x1, w_xo, g_xa, g_pre, w_gu, w_down, g_post, layer):
    n, d = x1.shape
    d_ff = w_down.shape[-2]
    tm = min(ROW_TILE, n)
    row = pl.BlockSpec((tm, d), lambda i: (i, 0))
    vec = _const_spec((1, d))
    return pl.pallas_call(
        functools.partial(_ffn_kernel, d_ff=d_ff),
        grid=(n // tm,),
        in_specs=[row, row, _wspec(w_xo, layer), vec, vec, _wspec(w_gu, layer), _wspec(w_down, layer), vec],
        out_specs=row,
        out_shape=jax.ShapeDtypeStruct((n, d), F32),
        compiler_params=_params(("parallel",)),
        name="attn_out_ffn",
    )(o, x1, w_xo, g_xa, g_pre, w_gu, w_down, g_post)


def _mix_out(ys_ref, yr_ref, yg_ref, yh_ref, x_ref, wo_ref, gpost_ref):
    mix = _bdot(ys_ref[...], wo_ref[0:GROUP_W, :])
    mix = mix + _bdot(yr_ref[...], wo_ref[GROUP_W:2 * GROUP_W, :])
    mix = mix + _bdot(yg_ref[...], wo_ref[2 * GROUP_W:3 * GROUP_W, :])
    mix = mix + _bdot(yh_ref[...], wo_ref[3 * GROUP_W:4 * GROUP_W, :])
    return x_ref[...] + _rms(mix, gpost_ref[...])


def _xattn_ffn_kernel(ys_ref, yr_ref, yg_ref, yh_ref, x_ref, wo_ref, gmix_ref, k_ref, v_ref, gq_ref, wq_ref,
                      wxo_ref, gxa_ref, gpre_ref, wgu_ref, wdn_ref, gpost_ref, x3_ref, o_scr, *, d_ff):
    x1 = _mix_out(ys_ref, yr_ref, yg_ref, yh_ref, x_ref, wo_ref, gmix_ref)
    q = _bdot(_rms(x1, gq_ref[...]), wq_ref[...])
    scale = XA_HD ** -0.5
    for h in range(XA_H):
        sl = slice(h * XA_HD, (h + 1) * XA_HD)
        s = _bdot_nt(q[:, sl], k_ref[0, :, sl]) * scale
        o_scr[:, sl] = _softmax_pv(s, v_ref[0, :, sl]).astype(BF16)
    x3_ref[...] = _ffn_tail(o_scr[...], x1, wxo_ref, gxa_ref, gpre_ref, wgu_ref, wdn_ref, gpost_ref, d_ff)


def _xattn_ffn(ys, x, w_out, g_mix, k, v, g_q, w_xq, w_xo, g_xa, g_pre, w_gu, w_down, g_post, seq_len, layer):
    n, d = x.shape
    n_seq, n_mem, _ = k.shape
    d_ff = w_down.shape[-2]
    nt = seq_len // XA_TILE
    row = lambda wd: pl.BlockSpec((XA_TILE, wd), lambda b, t: (b * nt + t, 0))
    kv = pl.BlockSpec((1, n_mem, d), lambda b, t: (b, 0, 0))
    vec = _const_spec((1, d))
    return pl.pallas_call(
        functools.partial(_xattn_ffn_kernel, d_ff=d_ff),
        grid=(n_seq, nt),
        in_specs=[row(GROUP_W)] * 4 + [row(d), _wspec(w_out, layer), vec, kv, kv, vec, _wspec(w_xq, layer),
                                       _wspec(w_xo, layer), vec, vec, _wspec(w_gu, layer),
                                       _wspec(w_down, layer), vec],
        out_specs=row(d),
        out_shape=jax.ShapeDtypeStruct((n, d), F32),
        scratch_shapes=[pltpu.VMEM((XA_TILE, d), BF16)],
        compiler_params=_params(("parallel", "parallel"), XA_VMEM_LIMIT),
        name="xattn_ffn",
    )(*ys, x, w_out, g_mix, k, v, g_q, w_xq, w_xo, g_xa, g_pre, w_gu, w_down, g_post)


def _softmax_pv(s, v):
    m = jnp.max(s, axis=-1, keepdims=True)
    p = jnp.exp(s - m)
    return _bdot(p, v) / jnp.sum(p, axis=-1, keepdims=True)


def _attn_sample_kernel(q_ref, k_ref, v_ref, o_ref, *, seqs, seg):
    scale = XA_HD ** -0.5
    rows = seqs * seg
    n_mem = k_ref.shape[2]
    wq = jnp.concatenate([q_ref[:, h * XA_HD:(h + 1) * XA_HD] for h in range(XA_H)], axis=0)
    wshape = (XA_H * rows, n_mem * XA_H)
    valid = (_iota(wshape, 0) >> (rows.bit_length() - 1)) == (_iota(wshape, 1) & (XA_H - 1))
    row_seq = _iota((rows, XA_HD), 0) >> (seg.bit_length() - 1)
    accs = [jnp.zeros((rows, XA_HD), F32) for _ in range(XA_H)]
    for i in range(seqs):
        k2 = k_ref[0, i].reshape(n_mem * XA_H, XA_HD)
        v2 = v_ref[0, i].reshape(n_mem * XA_H, XA_HD)
        s = jnp.where(valid, _bdot_nt(wq, k2) * scale, -jnp.inf)
        o = _softmax_pv(s, v2)
        for h in range(XA_H):
            accs[h] = jnp.where(row_seq == i, o[h * rows:(h + 1) * rows], accs[h])
    for h in range(XA_H):
        o_ref[:, h * XA_HD:(h + 1) * XA_HD] = accs[h]


def _attention_sample(q, cache_k, cache_v, layer, seq_len):
    n, d = q.shape
    _, n_seq, n_mem, heads, hd = cache_k.shape
    kv = pl.BlockSpec((1, ATT_SEQS, n_mem, heads, hd), lambda i: (layer, i, 0, 0, 0))
    row = pl.BlockSpec((ATT_SEQS * seq_len, d), lambda i: (i, 0))
    return pl.pallas_call(
        functools.partial(_attn_sample_kernel, seqs=ATT_SEQS, seg=seq_len),
        grid=(n_seq // ATT_SEQS,),
        in_specs=[row, kv, kv],
        out_specs=row,
        out_shape=jax.ShapeDtypeStruct((n, d), F32),
        compiler_params=_params(("parallel",)),
        name="xattn_sample",
    )(q, cache_k, cache_v)


def _ssd_intra(q, k, v, cum_x, cum_t, mask):
    sc = _bdot_nt(q, k)
    outs = []
    for hh in range(2):
        col = cum_x[:, hh * HEAD:(hh + 1) * HEAD]
        dec = jnp.exp2(jnp.where(mask, col - cum_t[hh:hh + 1, :], -jnp.inf))
        outs.append(_bdot(sc * dec, v[:, hh * HEAD:(hh + 1) * HEAD]))
    return jnp.concatenate(outs, axis=-1)


def _ssd_common(u, dtraw_ref, dtb_ref, alog_ref, conv, seg_shift):
    rows = u.shape[0]
    xbc = _silu(conv)
    xs = xbc[:, 0:GROUP_W]
    bm = xbc[:, GROUP_W:2 * GROUP_W]
    cm = xbc[:, 2 * GROUP_W:3 * GROUP_W]
    dt = _softplus(dtraw_ref[...] + dtb_ref[...])
    la = (-jnp.exp(alog_ref[...]) * LOG2_E) * dt
    cum = _seg_cumsum(la, seg_shift)
    expand = _head_expand_mat()
    dt_x = _dot_sel(dt, expand)
    cum_x = _dot_sel(cum, expand)
    sel8 = (_iota((8, 128), 0) == _iota((8, 128), 1)).astype(BF16)
    cum_t = _sel_dot_nt(sel8, cum)
    expand_full = ((_iota((128, 4 * 128), 1) >> 7) == _iota((128, 4 * 128), 0)).astype(BF16)
    cum_full = _dot_sel(cum, expand_full)
    return xs, bm, cm, dt_x, cum_x, cum_t, cum_full


def _ssd_finish(y, xs, z, d_ref, g_ref):
    y = y + d_ref[...] * xs
    return _rms(y * _silu(z), g_ref[...])


def _ssd_prompt_kernel(u_ref, dtraw_ref, cw_ref, cb_ref, dtb_ref, alog_ref, d_ref, g_ref,
                       y_ref, s_out_ref, buf_out_ref, s_ref, xx_ref, y_scr):
    t = pl.program_id(1)
    rows = u_ref.shape[0]

    @pl.when(t == 0)
    def _():
        s_ref[...] = jnp.zeros(s_ref.shape, F32)

    u = u_ref[...]
    conv = _conv_taps_carry(u[:, GROUP_W:], xx_ref, cw_ref, cb_ref, t == 0)
    xs, bm, cm, dt_x, cum_x, _, cum_full = _ssd_common(u, dtraw_ref, dtb_ref, alog_ref, conv, 6)
    v = xs * dt_x
    ecum_x = jnp.exp2(cum_x)
    hmask01 = _head_block_mask().astype(BF16)
    gmask01 = ((_iota((GROUP_W, GROUP_W), 0) >> 7) == (_iota((GROUP_W, GROUP_W), 1) >> 7)).astype(BF16)
    wrow = _iota((CHUNK, GROUP_W), 0)
    wcol = _iota((CHUNK, GROUP_W), 1) & (CHUNK - 1)
    n_chunks = rows // CHUNK
    groups = [slice(g * 128, (g + 1) * 128) for g in range(SSD_G)]
    a_wide, ds_t = [], []
    for c in range(n_chunks):
        rs = slice(c * CHUNK, (c + 1) * CHUNK)
        cum_c = cum_x[rs]
        last_x = cum_x[c * CHUNK + CHUNK - 1:(c + 1) * CHUNK, :]
        vend = v[rs] * jnp.exp2(last_x - cum_c)
        cum_row = jnp.sum(jnp.where(wrow == wcol, cum_c, 0.0), axis=0, keepdims=True)
        dec = jnp.exp2(jnp.where(wcol <= wrow, cum_c - cum_row, -jnp.inf))
        a_wide.append(_bdot_nt(cm[rs], _block_diag_rows(bm[rs], gmask01)) * dec)
        ds_t.append([_bdot_tn(vend[:, ls], bm[rs, ls]) for ls in groups])
    s_in = []
    s_cur = [s_ref[ls, :] for ls in groups]
    for c in range(n_chunks):
        last_full = cum_full[c * CHUNK + CHUNK - 1:(c + 1) * CHUNK, :]
        s_in.append([x.astype(BF16) for x in s_cur])
        for g in range(SSD_G):
            decay = jnp.concatenate(
                [jnp.broadcast_to(jnp.exp2(last_full[:, h * 128:(h + 1) * 128]), (HEAD, 128))
                 for h in (2 * g, 2 * g + 1)], axis=0)
            s_cur[g] = decay * s_cur[g] + ds_t[c][g]
    for g, ls in enumerate(groups):
        s_ref[ls, :] = s_cur[g]
    for c in range(n_chunks):
        rs = slice(c * CHUNK, (c + 1) * CHUNK)
        y_intra = _bdot(a_wide[c], _block_diag_rows(v[rs], hmask01))
        y_inter = [_bdot_nt(cm[rs, ls], s_in[c][g]) for g, ls in enumerate(groups)]
        y_scr[rs, :] = y_intra + jnp.concatenate(y_inter, axis=-1) * ecum_x[rs]
    y_ref[...] = _ssd_finish(y_scr[...], xs, u[:, 0:GROUP_W], d_ref, g_ref)

    @pl.when(t == pl.num_programs(1) - 1)
    def _():
        for h in range(N_HEADS):
            s_out_ref[0, h] = s_ref[h * HEAD:(h + 1) * HEAD, :]
        buf_out_ref[0] = xx_ref[pl.ds(8 - (CONV_W - 1), CONV_W - 1), :]


def _ssd_sample_kernel(u_ref, dtraw_ref, s0_ref, buf_ref, cw_ref, cb_ref, dtb_ref, alog_ref, d_ref,
                       g_ref, y_ref, s_out_ref, buf_out_ref, *, seqs):
    rows = u_ref.shape[0]
    u = u_ref[...]
    conv, new_buf = _conv_taps_seq4(u[:, GROUP_W:], buf_ref[...], cw_ref, cb_ref)
    buf_out_ref[...] = new_buf
    mask = _seg_causal_mask(rows, 2)
    xs, bm, cm, dt_x, cum_x, cum_t, cum_full = _ssd_common(u, dtraw_ref, dtb_ref, alog_ref, conv, 2)
    v = xs * dt_x
    ecum_x = jnp.exp2(cum_x)
    pick_last = (_iota((rows, rows), 1) == (_iota((rows, rows), 0) | 3)).astype(BF16)
    last_x = _sel_dot(pick_last, cum_x)
    vend = v * jnp.exp2(last_x - cum_x)
    row_seq = _iota((rows, 128), 0) >> 2
    ys = []
    for g in range(SSD_G):
        ls = slice(g * 128, (g + 1) * 128)
        q = cm[:, ls]
        k = bm[:, ls]
        y_g = _ssd_intra(q, k, v[:, ls], cum_x[:, ls], cum_t[2 * g:2 * g + 2, :], mask)
        y_int = jnp.zeros((rows, 128), F32)
        for i in range(seqs):
            in_seq = row_seq == i
            s_prev = jnp.concatenate([s0_ref[i, 2 * g], s0_ref[i, 2 * g + 1]], axis=0)
            y_int = jnp.where(in_seq, _bdot_nt(q, s_prev), y_int)
            ds = _bdot_tn(jnp.where(in_seq, vend[:, ls], 0.0), k)
            for hh in range(2):
                h = 2 * g + hh
                decay = jnp.exp2(cum_full[4 * i + 3:4 * i + 4, h * 128:(h + 1) * 128])
                s_out_ref[i, h] = decay * s0_ref[i, h] + ds[hh * HEAD:(hh + 1) * HEAD, :]
        ys.append(y_g + y_int * ecum_x[:, ls])
    y_ref[...] = _ssd_finish(jnp.concatenate(ys, axis=-1), xs, u[:, 0:GROUP_W], d_ref, g_ref)


def _ssd(u, dtraw, state, buf, cw, cb, dtb, alog, d_x, gain, n_seq, seq_len, layer, prev=None):
    n = u.shape[0]
    cch = cw.shape[1]
    vec = lambda wd: _const_spec((1, wd))
    common_in = [_const_spec(cw.shape), vec(cch), vec(128), vec(128), vec(GROUP_W), vec(GROUP_W)]
    if state is None:
        nt = seq_len // MIX_TILE
        row = lambda wd: pl.BlockSpec((MIX_TILE, wd), lambda b, t: (b * nt + t, 0))
        return pl.pallas_call(
            _ssd_prompt_kernel,
            grid=(n_seq, nt),
            in_specs=[row(u.shape[1]), row(128)] + common_in,
            out_specs=[row(GROUP_W),
                       pl.BlockSpec((1, N_HEADS, HEAD, SSD_N), lambda b, t: (b, 0, 0, 0)),
                       pl.BlockSpec((1, CONV_W - 1, cch), lambda b, t: (b, 0, 0))],
            out_shape=[jax.ShapeDtypeStruct((n, GROUP_W), F32),
                       jax.ShapeDtypeStruct((n_seq, N_HEADS, HEAD, SSD_N), F32),
                       jax.ShapeDtypeStruct((n_seq, CONV_W - 1, cch), F32)],
            scratch_shapes=[pltpu.VMEM((GROUP_W, SSD_N), F32), pltpu.VMEM((MIX_TILE + 8, cch), F32),
                            pltpu.VMEM((MIX_TILE, GROUP_W), F32)],
            compiler_params=_params(("parallel", "arbitrary")),
            name="ssd_prompt",
        )(u, dtraw, cw, cb, dtb, alog, d_x, gain)
    rows = SAMPLE_SEQS * seq_len
    row = lambda wd: pl.BlockSpec((rows, wd), lambda i: (i, 0))
    st = pl.BlockSpec((None, SAMPLE_SEQS, N_HEADS, HEAD, SSD_N), lambda i: (layer, i, 0, 0, 0))
    return _stack_call(
        functools.partial(_ssd_sample_kernel, seqs=SAMPLE_SEQS), prev,
        {1: ((SAMPLE_SEQS, N_HEADS, HEAD, SSD_N), lambda i: (i, 0, 0, 0))}, layer,
        in_specs=[row(u.shape[1]), row(128), st, row(cch)] + common_in,
        args=(u, dtraw, state, buf, cw, cb, dtb, alog, d_x, gain),
        grid=(n_seq // SAMPLE_SEQS,),
        out_specs=[row(GROUP_W), None, row(cch)],
        out_shape=[jax.ShapeDtypeStruct((n, GROUP_W), F32),
                   jax.ShapeDtypeStruct(state.shape, F32),
                   jax.ShapeDtypeStruct((n, cch), F32)],
        compiler_params=_params(("parallel",)),
        name="ssd_sample",
    )


def _rope_split(x, cos, sin_signed):
    swapped = jnp.concatenate([x[:, 128:], x[:, :128]], axis=1)
    return x * cos + swapped * sin_signed


def _ret_qkv(u_ref, cos_ref, sin_ref):
    u = u_ref[...]
    q = _rope_split(u[:, 0:GROUP_W], cos_ref[...], sin_ref[...])
    k = _rope_split(u[:, GROUP_W:2 * GROUP_W], cos_ref[...], sin_ref[...]) * (HEAD ** -0.5)
    return q, k, u[:, 2 * GROUP_W:3 * GROUP_W], u[:, 3 * GROUP_W:4 * GROUP_W]


def _ret_prompt_kernel(u_ref, cos_ref, sin_ref, ecum_ref, eend_ref, elast_ref, dec_ref, g_ref,
                       y_ref, s_out_ref, s_ref, y_scr):
    t = pl.program_id(1)
    rows = u_ref.shape[0]

    @pl.when(t == 0)
    def _():
        s_ref[...] = jnp.zeros(s_ref.shape, F32)

    q, k, v, gate = _ret_qkv(u_ref, cos_ref, sin_ref)
    r_i = _iota((GROUP_W, GROUP_W), 0)
    c_i = _iota((GROUP_W, GROUP_W), 1)
    kmask01 = ((r_i >> 6) == ((c_i & 127) >> 5)).astype(BF16)
    vmask01 = _head_block_mask().astype(BF16)
    smask = ((r_i & 127) >> 5) == (c_i >> 6)
    chunks = [slice(c * CHUNK, (c + 1) * CHUNK) for c in range(rows // CHUNK)]
    a_wide = [_bdot_nt(q[rs], _block_diag_rows(k[rs], kmask01)) * dec_ref[...] for rs in chunks]
    ds = [jnp.where(smask, _bdot_tn(k[rs], v[rs] * eend_ref[...]), 0.0) for rs in chunks]
    s = s_ref[...]
    s_in = []
    for d in ds:
        s_in.append(s.astype(BF16))
        s = elast_ref[...] * s + d
    s_ref[...] = s
    for rs, a, s_c in zip(chunks, a_wide, s_in):
        y_intra = _bdot(a, _block_diag_rows(v[rs], vmask01))
        y_scr[rs, :] = y_intra + _bdot(q[rs], s_c) * ecum_ref[...]
    y_ref[...] = _silu(gate) * (_head_stat(y_scr[...], True) * g_ref[...])

    @pl.when(t == pl.num_programs(1) - 1)
    def _():
        half = HEAD // 2
        for h in range(N_HEADS):
            vs = slice(h * HEAD, (h + 1) * HEAD)
            s_out_ref[0, h, 0:half, :] = s_ref[h * half:(h + 1) * half, vs]
            s_out_ref[0, h, half:HEAD, :] = s_ref[128 + h * half:128 + (h + 1) * half, vs]


def _ret_tables(block, seg):
    log_gamma = jnp.log(1.0 - jnp.exp2(-5.0 - jnp.arange(N_HEADS, dtype=F32)))
    pos = (jnp.arange(block) % seg).astype(F32)
    cum = (pos[:, None] + 1.0) * log_gamma[None, :]
    last = seg * log_gamma
    rep = lambda a: jnp.repeat(a, HEAD, axis=-1)
    same = (jnp.arange(block)[:, None] // seg) == (jnp.arange(block)[None, :] // seg)
    mask = same & (jnp.arange(block)[None, :] <= jnp.arange(block)[:, None])
    dec = jnp.exp(jnp.where(mask[None], cum.T[:, :, None] - cum.T[:, None, :], -jnp.inf))
    return (rep(jnp.exp(cum)), rep(jnp.exp(last[None, :] - cum)), rep(jnp.exp(last)[None, :]), dec)


def _ret(u, cos, sin_signed, gain, n_seq, seq_len):
    n = u.shape[0]
    vec = _const_spec((1, GROUP_W))
    nt = seq_len // MIX_TILE
    ecum, eend, elast, dec = _ret_tables(CHUNK, CHUNK)
    dec = dec.transpose(1, 0, 2).reshape(CHUNK, N_HEADS * CHUNK)
    row = lambda wd: pl.BlockSpec((MIX_TILE, wd), lambda b, t: (b * nt + t, 0))
    pos = pl.BlockSpec((MIX_TILE, GROUP_W), lambda b, t: (t, 0))
    return pl.pallas_call(
        _ret_prompt_kernel,
        grid=(n_seq, nt),
        in_specs=[row(u.shape[1]), pos, pos, _const_spec(ecum.shape), _const_spec(eend.shape), vec,
                  _const_spec(dec.shape), vec],
        out_specs=[row(GROUP_W), pl.BlockSpec((1, N_HEADS, HEAD, HEAD), lambda b, t: (b, 0, 0, 0))],
        out_shape=[jax.ShapeDtypeStruct((n, GROUP_W), F32),
                   jax.ShapeDtypeStruct((n_seq, N_HEADS, HEAD, HEAD), F32)],
        scratch_shapes=[pltpu.VMEM((GROUP_W, GROUP_W), F32), pltpu.VMEM((MIX_TILE, GROUP_W), F32)],
        compiler_params=_params(("parallel", "arbitrary")),
        name="ret_prompt",
    )(u, cos, sin_signed, ecum, eend, elast, dec, gain)


def _rg_gates(xr, wa_ref, ba_ref, wx_ref, bx_ref, lam_ref):
    r_gate = _sigmoid(_bdot(xr, wa_ref[...]) + ba_ref[...])
    i_gate = _sigmoid(_bdot(xr, wx_ref[...]) + bx_ref[...])
    log_a = (-RG_C * _softplus(-lam_ref[...])) * r_gate
    a = jnp.exp(log_a)
    b = jnp.sqrt(-jnp.tanh(log_a) * (a * a + 1.0)) * (i_gate * xr)
    return a, b


def _rg_scan(a, b, seg):
    pos = _iota(a.shape, 0) & (seg - 1)
    d = 1
    while d < seg:
        ok = pos >= d
        a_sh = jnp.where(ok, pltpu.roll(a, d, 0), 1.0)
        b_sh = jnp.where(ok, pltpu.roll(b, d, 0), 0.0)
        b = a * b_sh + b
        a = a * a_sh
        d *= 2
    return a, b


def _rg_scan_carry(a, b, h0):
    a, b = _rg_scan(a, b, 8)
    outs = []
    carry = h0
    for g in range(a.shape[0] // 8):
        h_g = b[g * 8:(g + 1) * 8] + a[g * 8:(g + 1) * 8] * carry
        outs.append(h_g)
        carry = h_g[7:8]
    return jnp.concatenate(outs, axis=0)


def _rg_prompt_kernel(u_ref, cw_ref, cb_ref, wa_ref, ba_ref, wx_ref, bx_ref, lam_ref,
                      y_ref, h_out_ref, buf_out_ref, h_ref, xx_ref):
    t = pl.program_id(1)
    rows = u_ref.shape[0]

    @pl.when(t == 0)
    def _():
        h_ref[...] = jnp.zeros(h_ref.shape, F32)

    u = u_ref[...]
    xr = _conv_taps_carry(u[:, 0:GROUP_W], xx_ref, cw_ref, cb_ref, t == 0)
    a, b = _rg_gates(xr, wa_ref, ba_ref, wx_ref, bx_ref, lam_ref)
    hseq = _rg_scan_carry(a, b, h_ref[0:1, :])
    h_ref[...] = jnp.broadcast_to(hseq[rows - 1:rows, :], h_ref.shape)
    y_ref[...] = _gelu_tanh(u[:, GROUP_W:]) * hseq

    @pl.when(t == pl.num_programs(1) - 1)
    def _():
        h_out_ref[0] = hseq[rows - 1:rows, :]
        buf_out_ref[0] = xx_ref[pl.ds(8 - (CONV_W - 1), CONV_W - 1), :]


def _rg_sample_kernel(u_ref, h0_ref, buf_ref, cw_ref, cb_ref, wa_ref, ba_ref, wx_ref, bx_ref, lam_ref,
                      y_ref, h_out_ref, buf_out_ref):
    u = u_ref[...]
    xr, new_buf = _conv_taps_seq4(u[:, 0:GROUP_W], buf_ref[...], cw_ref, cb_ref)
    buf_out_ref[...] = new_buf
    a, b = _rg_gates(xr, wa_ref, ba_ref, wx_ref, bx_ref, lam_ref)
    a_cum, h_loc = _rg_scan(a, b, 4)
    hseq = h_loc + a_cum * h0_ref[...]
    h_out_ref[...] = hseq
    y_ref[...] = _gelu_tanh(u[:, GROUP_W:]) * hseq


def _rg(u, h0_rows, buf, cw, cb, wa, ba, wx, bx, lam, n_seq, seq_len):
    n = u.shape[0]
    vec = _const_spec((1, GROUP_W))
    common_in = [_const_spec(cw.shape), vec, _const_spec(wa.shape), vec, _const_spec(wx.shape), vec, vec]
    if h0_rows is None:
        nt = seq_len // RG_TILE
        row = lambda wd: pl.BlockSpec((RG_TILE, wd), lambda b, t: (b * nt + t, 0))
        return pl.pallas_call(
            _rg_prompt_kernel,
            grid=(n_seq, nt),
            in_specs=[row(u.shape[1])] + common_in,
            out_specs=[row(GROUP_W), pl.BlockSpec((1, 1, GROUP_W), lambda b, t: (b, 0, 0)),
                       pl.BlockSpec((1, CONV_W - 1, GROUP_W), lambda b, t: (b, 0, 0))],
            out_shape=[jax.ShapeDtypeStruct((n, GROUP_W), F32),
                       jax.ShapeDtypeStruct((n_seq, 1, GROUP_W), F32),
                       jax.ShapeDtypeStruct((n_seq, CONV_W - 1, GROUP_W), F32)],
            scratch_shapes=[pltpu.VMEM((8, GROUP_W), F32), pltpu.VMEM((RG_TILE + 8, GROUP_W), F32)],
            compiler_params=_params(("parallel", "arbitrary")),
            name="rglru_prompt",
        )(u, cw, cb, wa, ba, wx, bx, lam)
    rows = SAMPLE_SEQS * seq_len
    row = lambda wd: pl.BlockSpec((rows, wd), lambda i: (i, 0))
    return pl.pallas_call(
        _rg_sample_kernel,
        grid=(n_seq // SAMPLE_SEQS,),
        in_specs=[row(u.shape[1]), row(GROUP_W), row(GROUP_W)] + common_in,
        out_specs=[row(GROUP_W)] * 3,
        out_shape=[jax.ShapeDtypeStruct((n, GROUP_W), F32)] * 3,
        compiler_params=_params(("parallel",)),
        name="rglru_sample",
    )(u, h0_rows, buf, cw, cb, wa, ba, wx, bx, lam)


def _hg_inputs(u_ref, lb_ref, seg_shift):
    u = u_ref[...]
    rows = u.shape[0]
    lb = lb_ref[...]
    fg = lb + (1.0 - lb) * _sigmoid(u[:, GROUP_W:2 * GROUP_W])
    q = _silu(u[:, 0:GROUP_W])
    k = 1.0 - fg
    v = u[:, 2 * GROUP_W:3 * GROUP_W]
    cum = _seg_cumsum(jnp.log(fg), seg_shift)
    return q, k, v, cum, u[:, 3 * GROUP_W:4 * GROUP_W]


def _hg_prompt_kernel(u_ref, lb_ref, g_ref, y_ref, s_out_ref, st_ref, y_scr):
    t = pl.program_id(1)
    rows = u_ref.shape[0]

    @pl.when(t == 0)
    def _():
        st_ref[...] = jnp.zeros(st_ref.shape, F32)

    q, k, v, cum, gate = _hg_inputs(u_ref, lb_ref, 6)
    hmask = _head_block_mask()
    hmask01 = hmask.astype(BF16)
    ones_bd = hmask01
    y_scr[...] = jnp.dot((q * k).astype(BF16), ones_bd, preferred_element_type=F32) * v
    row = _iota((rows, GROUP_W), 0)
    trow = _iota((CHUNK, GROUP_W), 0)
    tcol = _iota((CHUNK, GROUP_W), 1) & (CHUNK - 1)
    cum = cum * LOG2_E
    levels = []
    blk_last = cum
    h = 1
    while h < CHUNK:
        upper = (row & h) != 0
        qn = jnp.where(upper, q * jnp.exp2(cum - pltpu.roll(blk_last, h, 0)), 0.0)
        kn = k * jnp.exp2(blk_last - cum)
        sh = (2 * h).bit_length() - 1
        pair = ((trow >> sh) == (tcol >> sh)) & ((tcol & h) == 0)
        levels.append((pair, qn, kn))
        blk_last = jnp.where(upper, blk_last, pltpu.roll(blk_last, rows - h, 0))
        h *= 2
    q_in = q * jnp.exp2(cum)
    k_end = k * jnp.exp2(blk_last - cum)
    n_chunks = rows // CHUNK
    a_wide, ds_t = [], []
    for c in range(n_chunks):
        rs = slice(c * CHUNK, (c + 1) * CHUNK)
        a_c = jnp.zeros((CHUNK, GROUP_W), F32)
        for pair, qn, kn in levels:
            g = _bdot_nt(qn[rs], _block_diag_rows(kn[rs], hmask01))
            a_c = a_c + jnp.where(pair, g, 0.0)
        a_wide.append(a_c)
        ds_t.append(jnp.where(hmask, _bdot_tn(v[rs], k_end[rs]), 0.0))
    st = st_ref[...]
    st_in = []
    for c in range(n_chunks):
        st_in.append(st.astype(BF16))
        st = jnp.exp2(blk_last[c * CHUNK:c * CHUNK + 1, :]) * st + ds_t[c]
    st_ref[...] = st
    for c in range(n_chunks):
        rs = slice(c * CHUNK, (c + 1) * CHUNK)
        y_c = _bdot(a_wide[c], _block_diag_rows(v[rs], hmask01)) + _bdot_nt(q_in[rs], st_in[c])
        y_scr[rs, :] = y_scr[rs, :] + y_c
    y_ref[...] = _silu(gate) * (_head_stat(y_scr[...], False) * g_ref[...])

    @pl.when(t == pl.num_programs(1) - 1)
    def _():
        eye = (_iota((HEAD, HEAD), 0) == _iota((HEAD, HEAD), 1)).astype(BF16)
        for h in range(N_HEADS):
            s_out_ref[0, h] = _sel_dot_nt(eye, st_ref[h * HEAD:(h + 1) * HEAD, h * HEAD:(h + 1) * HEAD])


def _hg(u, lb, gain, n_seq, seq_len):
    n = u.shape[0]
    vec = _const_spec((1, GROUP_W))
    nt = seq_len // MIX_TILE
    row = lambda wd: pl.BlockSpec((MIX_TILE, wd), lambda b, t: (b * nt + t, 0))
    return pl.pallas_call(
        _hg_prompt_kernel,
        grid=(n_seq, nt),
        in_specs=[row(u.shape[1]), vec, vec],
        out_specs=[row(GROUP_W), pl.BlockSpec((1, N_HEADS, HEAD, HEAD), lambda b, t: (b, 0, 0, 0))],
        out_shape=[jax.ShapeDtypeStruct((n, GROUP_W), F32),
                   jax.ShapeDtypeStruct((n_seq, N_HEADS, HEAD, HEAD), F32)],
        scratch_shapes=[pltpu.VMEM((GROUP_W, GROUP_W), F32), pltpu.VMEM((MIX_TILE, GROUP_W), F32)],
        compiler_params=_params(("parallel", "arbitrary")),
        name="hgrn_prompt",
    )(u, lb, gain)


def _t_recurrence(q_scr, k_scr, v, dec_row, dec_scr, s_ref, n_seq, seq_len):
    cols = [slice(t * n_seq, (t + 1) * n_seq) for t in range(seq_len)]
    v_t = [v[:, c] for c in cols]
    outs = [jnp.zeros((HEAD, n_seq), F32) for _ in cols]
    for k in range(HEAD):
        s = s_ref[k]
        for t, c in enumerate(cols):
            dec = dec_row if dec_scr is None else dec_scr[k:k + 1, c]
            s = dec * s + k_scr[k:k + 1, c] * v_t[t]
            outs[t] = outs[t] + q_scr[k:k + 1, c] * s
        s_ref[k] = s
    return jnp.concatenate(outs, axis=1)


def _t_head_norm(o, center):
    if center:
        o = o - jnp.mean(o, axis=0, keepdims=True)
    return o * lax.rsqrt(jnp.mean(o * o, axis=0, keepdims=True) + EPS)


def _ret_t_kernel(q_ref, k_ref, v_ref, g_ref, cos_ref, sin_ref, gam_ref, gain_ref, s0_ref,
                  y_ref, s_ref, q_scr, k_scr, *, n_seq, seq_len):
    half = HEAD // 2

    def rope(x):
        swapped = jnp.concatenate([x[half:], x[:half]], axis=0)
        return x * cos_ref[...] + swapped * sin_ref[...]

    q_scr[...] = rope(q_ref[...])
    k_scr[...] = rope(k_ref[...]) * (HEAD ** -0.5)
    s_ref[0] = s0_ref[0]
    o = _t_recurrence(q_scr, k_scr, v_ref[...], gam_ref[0], None, s_ref.at[0], n_seq, seq_len)
    gain = jnp.concatenate([gain_ref[...]] * seq_len, axis=1)
    parts = [_t_head_norm(o[:, t * n_seq:(t + 1) * n_seq], True) for t in range(seq_len)]
    y_ref[...] = _silu(g_ref[...]) * (jnp.concatenate(parts, axis=1) * gain)


def _hg_t_kernel(q_ref, f_ref, i_ref, g_ref, lb_ref, gain_ref, s0_ref, y_ref, s_ref, q_scr, k_scr, f_scr,
                 *, n_seq, seq_len):
    lb = jnp.concatenate([lb_ref[...]] * seq_len, axis=1)
    fg = lb + (1.0 - lb) * _sigmoid(f_ref[...])
    q_scr[...] = _silu(q_ref[...])
    k_scr[...] = 1.0 - fg
    f_scr[...] = fg
    s_ref[0] = s0_ref[0]
    o = _t_recurrence(q_scr, k_scr, i_ref[...], None, f_scr, s_ref.at[0], n_seq, seq_len)
    gain = jnp.concatenate([gain_ref[...]] * seq_len, axis=1)
    parts = [_t_head_norm(o[:, t * n_seq:(t + 1) * n_seq], False) for t in range(seq_len)]
    y_ref[...] = _silu(g_ref[...]) * (jnp.concatenate(parts, axis=1) * gain)


def _t_specs(n, n_seq):
    head_rows = lambda section: pl.BlockSpec((HEAD, n), lambda h, s=section: (N_HEADS * s + h, 0))
    per_head = pl.BlockSpec((HEAD, n_seq), lambda h: (h, 0))
    state_in = lambda layer: pl.BlockSpec((None, 1, HEAD, HEAD, n_seq), lambda h: (layer, h, 0, 0, 0))
    state_out = ((1, HEAD, HEAD, n_seq), lambda h: (h, 0, 0, 0))
    return head_rows, per_head, state_in, state_out


def _ret_t(u_t, cos_t, sin_t, gam, gain_b, state_t, n_seq, seq_len, layer, prev):
    n = u_t.shape[1]
    head_rows, per_head, state_in, state_out = _t_specs(n, n_seq)
    return _stack_call(
        functools.partial(_ret_t_kernel, n_seq=n_seq, seq_len=seq_len), prev, {1: state_out}, layer,
        in_specs=[head_rows(0), head_rows(1), head_rows(2), head_rows(3), _const_spec(cos_t.shape),
                  _const_spec(sin_t.shape), pl.BlockSpec((1, 1, n_seq), lambda h: (h, 0, 0)), per_head,
                  state_in(layer)],
        args=(u_t, u_t, u_t, u_t, cos_t, sin_t, gam, gain_b, state_t),
        grid=(N_HEADS,),
        out_specs=[pl.BlockSpec((HEAD, n), lambda h: (h, 0)), None],
        out_shape=[jax.ShapeDtypeStruct((GROUP_W, n), F32), jax.ShapeDtypeStruct(state_t.shape, F32)],
        scratch_shapes=[pltpu.VMEM((HEAD, n), F32)] * 2,
        compiler_params=_params(("parallel",)),
        name="ret_sample_t",
    )


def _hg_t(u_t, lb_b, gain_b, state_t, n_seq, seq_len, layer, prev):
    n = u_t.shape[1]
    head_rows, per_head, state_in, state_out = _t_specs(n, n_seq)
    return _stack_call(
        functools.partial(_hg_t_kernel, n_seq=n_seq, seq_len=seq_len), prev, {1: state_out}, layer,
        in_specs=[head_rows(0), head_rows(1), head_rows(2), head_rows(3), per_head, per_head, state_in(layer)],
        args=(u_t, u_t, u_t, u_t, lb_b, gain_b, state_t),
        grid=(N_HEADS,),
        out_specs=[pl.BlockSpec((HEAD, n), lambda h: (h, 0)), None],
        out_shape=[jax.ShapeDtypeStruct((GROUP_W, n), F32), jax.ShapeDtypeStruct(state_t.shape, F32)],
        scratch_shapes=[pltpu.VMEM((HEAD, n), F32)] * 3,
        compiler_params=_params(("parallel",)),
        name="hgrn_sample_t",
    )


def _rope_tables_t(pos0, seq_len, n_seq):
    half = HEAD // 2
    pos = pos0 + jnp.arange(seq_len, dtype=F32)
    inv = ROPE_BASE ** (-jnp.arange(half, dtype=F32) / half)
    ang = pos[:, None] * inv
    cos = jnp.tile(jnp.cos(ang).T, (2, 1))
    sin = jnp.sin(ang).T
    sin_signed = jnp.concatenate([-sin, sin], axis=0)
    return jnp.repeat(cos, n_seq, axis=1), jnp.repeat(sin_signed, n_seq, axis=1)


def _block_diag(w):
    h, i, j = w.shape
    eye = jnp.eye(h, dtype=w.dtype)
    return (eye[:, None, :, None] * w[:, :, None, :]).reshape(h * i, h * j)


def _rope_tables(pos0, seq_len):
    half = HEAD // 2
    pos = pos0 + jnp.arange(seq_len, dtype=F32)
    inv = ROPE_BASE ** (-jnp.arange(half, dtype=F32) / half)
    ang = pos[:, None] * inv
    cos = jnp.tile(jnp.cos(ang), (1, 2 * N_HEADS))
    sin = jnp.sin(ang)
    sin_signed = jnp.concatenate([jnp.tile(-sin, (1, N_HEADS)), jnp.tile(sin, (1, N_HEADS))], axis=-1)
    return cos, sin_signed


def _layer_weights(P, l):
    row = lambda a: a.reshape(1, -1)
    pad128 = lambda a: jnp.pad(a, (0, 128 - a.shape[0])).reshape(1, 128)
    lb_sm = jax.nn.softmax(P['hg_lb'].astype(F32), axis=0)
    lb = (jnp.cumsum(lb_sm, axis=0) - lb_sm[0])[l]
    return dict(
        w_in=P['w_in_prepped'], w_in_t=P['w_in_t'],
        ln_mix_pre=row(P['ln_mix_pre'][l]), ln_mix_post=row(P['ln_mix_post'][l]),
        ln_xa_pre=row(P['ln_xa_pre'][l]), ln_xa_post=row(P['ln_xa_post'][l]),
        ln_ffn_pre=row(P['ln_ffn_pre'][l]), ln_ffn_post=row(P['ln_ffn_post'][l]),
        ssd_cw=P['ssd_conv_w'][l], ssd_cb=row(P['ssd_conv_b'][l]),
        ssd_dtb=pad128(P['ssd_dt_bias'][l]), ssd_alog=pad128(P['ssd_A_log'][l]),
        ssd_d=row(jnp.repeat(P['ssd_D'][l], HEAD)), ssd_norm=row(P['ssd_norm'][l]),
        ret_norm=row(P['ret_norm'][l]),
        rg_cw=P['rg_conv_w'][l], rg_cb=row(P['rg_conv_b'][l]),
        rg_wa=_block_diag(P['rg_wa'][l]).astype(BF16), rg_ba=row(P['rg_ba'][l]),
        rg_wx=_block_diag(P['rg_wx'][l]).astype(BF16), rg_bx=row(P['rg_bx'][l]),
        rg_lam=row(P['rg_lambda'][l]),
        hg_lb=row(lb), hg_norm=row(P['hg_norm'][l]),
        w_out=P['w_out_bf16'], w_xq=P['w_xq_bf16'], w_xo=P['w_xo_bf16'], w_gu=P['w_gu_bf16'],
        w_down=P['w_down_bf16'], ln_mem=row(P['ln_mem'][l]), w_xkv=P['w_xkv_bf16'],
    )


def _layer(x, W, k_mem, v_mem, st, n_seq, seq_len, pos0, layer, prev=None):
    if st is None:
        u_ssd, u_ret, u_rg, u_hg, u_dt = _norm_matmul(
            x, W['ln_mix_pre'], W['w_in'], (IN_COLS_SSD, IN_COLS_RET_SPLIT, IN_COLS_RG, IN_COLS_HG, IN_COLS_DT),
            "in_proj", layer=layer)
        cos, sin_signed = _rope_tables(pos0, seq_len)
        y_ssd, s_ssd, b_ssd = _ssd(u_ssd, u_dt, None, None, W['ssd_cw'], W['ssd_cb'], W['ssd_dtb'],
                                   W['ssd_alog'], W['ssd_d'], W['ssd_norm'], n_seq, seq_len, layer)
        y_ret, s_ret = _ret(u_ret, cos, sin_signed, W['ret_norm'], n_seq, seq_len)
        y_rg, h_rg, b_rg = _rg(u_rg, None, None, W['rg_cw'], W['rg_cb'], W['rg_wa'], W['rg_ba'],
                               W['rg_wx'], W['rg_bx'], W['rg_lam'], n_seq, seq_len)
        h_rg = h_rg.reshape(n_seq, GROUP_W)
        y_hg, s_hg = _hg(u_hg, W['hg_lb'], W['hg_norm'], n_seq, seq_len)
    else:
        ssd_s, ssd_buf, ret_s, rg_h, rg_buf, hg_s = st
        u_ssd, u_rg, u_dt = _norm_matmul(x, W['ln_mix_pre'], W['w_in'], (IN_COLS_SSD, IN_COLS_RG, IN_COLS_DT),
                                         "in_proj", layer=layer)
        to_tb = lambda a: a.reshape(n_seq, seq_len, -1).transpose(1, 0, 2).reshape(n_seq * seq_len, -1)
        from_t = lambda a: a.reshape(-1, seq_len, n_seq).transpose(2, 1, 0).reshape(n_seq * seq_len, -1)
        bcast = lambda g: jnp.broadcast_to(g.reshape(-1, 1), (g.size, n_seq))
        ut_ret, ut_hg = _in_proj_t(to_tb(x), W['ln_mix_pre'], W['w_in_t'], layer)
        cos_t, sin_t = _rope_tables_t(pos0, seq_len, n_seq)
        log_gamma = jnp.log(1.0 - jnp.exp2(-5.0 - jnp.arange(N_HEADS, dtype=F32)))
        gam = jnp.broadcast_to(jnp.exp(log_gamma)[:, None, None], (N_HEADS, 1, n_seq))
        yt_ret, s_ret = _ret_t(ut_ret, cos_t, sin_t, gam, bcast(W['ret_norm']), ret_s, n_seq, seq_len, layer,
                               prev and (prev[1],))
        yt_hg, s_hg = _hg_t(ut_hg, bcast(W['hg_lb']), bcast(W['hg_norm']), hg_s, n_seq, seq_len, layer,
                            prev and (prev[2],))
        y_ret, y_hg = from_t(yt_ret), from_t(yt_hg)
        pad_rows = lambda b: jnp.pad(b, ((0, 0), (0, 1), (0, 0))).reshape(n_seq * 4, b.shape[-1])
        y_ssd, s_ssd, b_ssd = _ssd(u_ssd, u_dt, ssd_s, pad_rows(ssd_buf), W['ssd_cw'], W['ssd_cb'],
                                   W['ssd_dtb'], W['ssd_alog'], W['ssd_d'], W['ssd_norm'], n_seq, seq_len, layer,
                                   prev and (prev[0],))
        b_ssd = b_ssd.reshape(n_seq, 4, -1)[:, :CONV_W - 1]
        y_rg, h_rows, b_rg = _rg(u_rg, jnp.repeat(rg_h, seq_len, axis=0), pad_rows(rg_buf), W['rg_cw'],
                                 W['rg_cb'], W['rg_wa'], W['rg_ba'], W['rg_wx'], W['rg_bx'], W['rg_lam'],
                                 n_seq, seq_len)
        h_rg = h_rows.reshape(n_seq, seq_len, GROUP_W)[:, seq_len - 1]
        b_rg = b_rg.reshape(n_seq, 4, -1)[:, :CONV_W - 1]
    ys = (y_ssd, y_ret, y_rg, y_hg)
    if st is None:
        x3 = _xattn_ffn(ys, x, W['w_out'], W['ln_mix_post'], k_mem, v_mem, W['ln_xa_pre'], W['w_xq'], W['w_xo'],
                        W['ln_xa_post'], W['ln_ffn_pre'], W['w_gu'], W['w_down'], W['ln_ffn_post'], seq_len,
                        layer)
    else:
        x1, q = _out_proj(ys, x, W['w_out'], W['ln_mix_post'], W['ln_xa_pre'], W['w_xq'], layer)
        o = _attention_sample(q, k_mem, v_mem, layer, seq_len)
        x3 = _ffn(o, x1, W['w_xo'], W['ln_xa_post'], W['ln_ffn_pre'], W['w_gu'], W['w_down'],
                  W['ln_ffn_post'], layer)
    return x3, (s_ssd, b_ssd, s_ret, h_rg, b_rg, s_hg)


def kernel(x_prompt, x_sample, state_ssd, state_ssd_conv, state_ret, state_rglru, state_rglru_conv, state_hgrn, cache_mem_k, cache_mem_v, mem_prompt, ln_mix_pre, ln_mix_post, ln_xa_pre, ln_xa_post, ln_ffn_pre, ln_ffn_post, w_in, ssd_conv_w, ssd_conv_b, ssd_dt_bias, ssd_A_log, ssd_D, ssd_norm, ret_norm, rg_conv_w, rg_conv_b, rg_wa, rg_ba, rg_wx, rg_bx, rg_lambda, hg_lb, hg_norm, w_out, ln_mem, w_xq, w_xkv, w_xo, w_gu, w_down):
    P = dict(ln_mix_pre=ln_mix_pre, ln_mix_post=ln_mix_post, ln_xa_pre=ln_xa_pre, ln_xa_post=ln_xa_post,
             ln_ffn_pre=ln_ffn_pre, ln_ffn_post=ln_ffn_post, w_in=w_in, ssd_conv_w=ssd_conv_w,
             ssd_conv_b=ssd_conv_b, ssd_dt_bias=ssd_dt_bias, ssd_A_log=ssd_A_log, ssd_D=ssd_D,
             ssd_norm=ssd_norm, ret_norm=ret_norm, rg_conv_w=rg_conv_w, rg_conv_b=rg_conv_b,
             rg_wa=rg_wa, rg_ba=rg_ba, rg_wx=rg_wx, rg_bx=rg_bx, rg_lambda=rg_lambda, hg_lb=hg_lb,
             hg_norm=hg_norm, w_out=w_out, ln_mem=ln_mem, w_xq=w_xq, w_xkv=w_xkv, w_xo=w_xo,
             w_gu=w_gu, w_down=w_down)
    P['w_in_prepped'], P['w_in_t'] = _in_weight_prep(w_in)
    for name in ('w_out', 'w_xq', 'w_xo', 'w_gu', 'w_down', 'w_xkv'):
        P[name + '_bf16'] = P[name].astype(BF16)
    ssd_t = jnp.swapaxes(state_ssd, -1, -2)
    ret_t = jnp.transpose(state_ret, (0, 2, 3, 4, 1))
    hgrn_t = jnp.transpose(state_hgrn, (0, 2, 3, 4, 1))
    depth = w_in.shape[0]
    bp, tp, d = x_prompt.shape
    bs, ts, _ = x_sample.shape
    n_mem = mem_prompt.shape[1]
    y_p = x_prompt.reshape(bp * tp, d)
    y_s = x_sample.reshape(bs * ts, d)
    mem = mem_prompt.reshape(bp * n_mem, d)
    p_st, s_st = [], []
    mem_prev = samp_prev = None
    for l in range(depth):
        W = _layer_weights(P, l)
        mk, mv, mk5, mv5 = _mem_kv(mem, W['ln_mem'], W['w_xkv'], l, n_mem, mem_prev)
        mem_prev = (mk5, mv5)
        mk = mk.reshape(bp, n_mem, d)
        mv = mv.reshape(bp, n_mem, d)
        y_p, st_p = _layer(y_p, W, mk, mv, None, bp, tp, 0.0, l)
        p_st.append(st_p)
        st_l = (ssd_t, state_ssd_conv[l], ret_t, state_rglru[l], state_rglru_conv[l], hgrn_t)
        y_s, st_s = _layer(y_s, W, cache_mem_k, cache_mem_v, st_l, bs, ts, float(PAST_LEN), l, samp_prev)
        samp_prev = (st_s[0], st_s[2], st_s[5])
        s_st.append(st_s)
    stack = lambda sts, i: jnp.stack([s[i] for s in sts], axis=0)
    return (y_p.reshape(bp, tp, d), y_s.reshape(bs, ts, d),
            jnp.swapaxes(stack(p_st, 0), -1, -2), stack(p_st, 1), stack(p_st, 2), stack(p_st, 3), stack(p_st, 4),
            stack(p_st, 5),
            mem_prev[0], mem_prev[1],
            jnp.swapaxes(samp_prev[0], -1, -2), stack(s_st, 1), jnp.transpose(samp_prev[1], (0, 4, 1, 2, 3)),
            stack(s_st, 3), stack(s_st, 4), jnp.transpose(samp_prev[2], (0, 4, 1, 2, 3)))
```

```python
import functools
import math

import jax
import jax.numpy as jnp
from jax import lax
from jax.experimental import pallas as pl
from jax.experimental.pallas import tpu as pltpu

F32 = jnp.float32
BF16 = jnp.bfloat16
EPS = 1e-6
LOG2_E = 1.4426950408889634

GROUP_W = 256
HEAD = 64
N_HEADS = 4
SSD_N = 128
SSD_G = 2
CONV_W = 4
XA_H = 4
XA_HD = 256
ROPE_BASE = 10000.0
RG_C = 8.0
CHUNK = 64
PAST_LEN = 16384

ROW_TILE = 512
XA_TILE = 1024
IN_TILE = 1024
MIX_TILE = 2048
RG_TILE = 2048
SAMPLE_SEQS = 16
ATT_SEQS = 8
FF_TILE = 256
VMEM_LIMIT = 56 * 1024 * 1024
XA_VMEM_LIMIT = 62 * 1024 * 1024


def _bdot(a, b):
    return jnp.dot(a.astype(BF16), b.astype(BF16), preferred_element_type=F32)


def _bdot_nt(a, b):
    return lax.dot_general(a.astype(BF16), b.astype(BF16), (((1,), (1,)), ((), ())),
                           preferred_element_type=F32)


def _bdot_tn(a, b):
    return lax.dot_general(a.astype(BF16), b.astype(BF16), (((0,), (0,)), ((), ())),
                           preferred_element_type=F32)


def _split3(x):
    hi = x.astype(BF16)
    r = x - hi.astype(F32)
    mid = r.astype(BF16)
    lo = (r - mid.astype(F32)).astype(BF16)
    return hi, mid, lo


def _sel_dot(sel, x):
    hi, mid, lo = _split3(x)
    d = lambda y: jnp.dot(sel, y, preferred_element_type=F32)
    return (d(hi) + d(mid)) + d(lo)


def _dot_sel(x, sel):
    hi, mid, lo = _split3(x)
    d = lambda y: jnp.dot(y, sel, preferred_element_type=F32)
    return (d(hi) + d(mid)) + d(lo)


def _sel_dot_nt(sel, x):
    hi, mid, lo = _split3(x)
    d = lambda y: lax.dot_general(sel, y, (((1,), (1,)), ((), ())), preferred_element_type=F32)
    return (d(hi) + d(mid)) + d(lo)


def _rms(x, g):
    return x * lax.rsqrt(jnp.mean(x * x, axis=-1, keepdims=True) + EPS) * g


def _sigmoid(x):
    return jax.nn.sigmoid(x)


def _silu(x):
    return x * jax.nn.sigmoid(x)


def _softplus(x):
    return jnp.maximum(x, 0.0) + jnp.log1p(jnp.exp(-jnp.abs(x)))


def _gelu_tanh(x):
    c = math.sqrt(2.0 / math.pi)
    return 0.5 * x * (1.0 + jnp.tanh(c * (x + 0.044715 * (x * x * x))))


def _iota(shape, dim):
    return lax.broadcasted_iota(jnp.int32, shape, dim)


def _seg_causal_mask(n, seg_shift):
    r = _iota((n, n), 0)
    c = _iota((n, n), 1)
    return ((r >> seg_shift) == (c >> seg_shift)) & (c <= r)


def _seg_cumsum(x, seg_shift):
    rows = x.shape[0]
    blk = min(rows, 256)
    mask = _seg_causal_mask(blk, seg_shift).astype(BF16)
    return jnp.concatenate([_sel_dot(mask, x[i:i + blk]) for i in range(0, rows, blk)], axis=0)


def _head_expand_mat(width_in=128):
    r = _iota((width_in, GROUP_W), 0)
    c = _iota((width_in, GROUP_W), 1)
    return ((c >> 6) == r).astype(BF16)


def _head_block_mask():
    return (_iota((GROUP_W, GROUP_W), 0) >> 6) == (_iota((GROUP_W, GROUP_W), 1) >> 6)


def _block_diag_rows(x, mask01):
    return jnp.concatenate([x.astype(BF16)] * N_HEADS, axis=0) * mask01


def _head_stat(y, center):
    ones_bd = _head_block_mask().astype(BF16)
    if center:
        y = y - _dot_sel(y, ones_bd) * (1.0 / HEAD)
    return y * lax.rsqrt(_dot_sel(y * y, ones_bd) * (1.0 / HEAD) + EPS)


def _conv_taps_carry(x, xx_ref, w_ref, b_ref, first):
    rows = x.shape[0]

    @pl.when(first)
    def _():
        xx_ref[0:8, :] = jnp.zeros((8, x.shape[1]), F32)

    xx_ref[8:8 + rows, :] = x
    y = b_ref[...] + w_ref[CONV_W - 1:CONV_W, :] * x
    for j in range(CONV_W - 1):
        y = y + w_ref[j:j + 1, :] * xx_ref[pl.ds(8 - (CONV_W - 1) + j, rows), :]
    xx_ref[0:8, :] = xx_ref[rows:rows + 8, :]
    return y


def _conv_taps_seq4(x, buf4, w_ref, b_ref):
    rows = x.shape[0]
    r = _iota((rows, rows), 0)
    c = _iota((rows, rows), 1)
    t = r & 3
    y = b_ref[...] + w_ref[CONV_W - 1:CONV_W, :] * x
    for d in range(1, CONV_W):
        shift = ((c == r - d) & (t >= d)).astype(BF16)
        hist = ((t < d) & (c == (r - t) + (3 + t - d))).astype(BF16)
        y = y + w_ref[CONV_W - 1 - d:CONV_W - d, :] * (_sel_dot(shift, x) + _sel_dot(hist, buf4))
    new_sel = (((r & 3) < 3) & (c == r + 1)).astype(BF16)
    return y, _sel_dot(new_sel, x)


def _norm_matmul_kernel(x_ref, g_ref, w_ref, *o_refs, col_ranges):
    h = _rms(x_ref[...], g_ref[...]).astype(BF16)
    for o_ref, pieces in zip(o_refs, col_ranges):
        off = 0
        for a, b in pieces:
            o_ref[:, off:off + b - a] = jnp.dot(h, w_ref[:, a:b], preferred_element_type=F32)
            off += b - a


def _const_spec(shape):
    nd = len(shape)
    return pl.BlockSpec(shape, lambda *_: (0,) * nd, pipeline_mode=pl.Buffered(1))


def _layer_spec(shape, layer):
    nd = len(shape)
    return pl.BlockSpec((None,) + tuple(shape[1:]), lambda *_: (layer,) + (0,) * (nd - 1),
                        pipeline_mode=pl.Buffered(1))


def _stack_call(kernel, prevs, stacked, layer, *, in_specs, args, out_specs, out_shape, **kw):
    n_in = len(in_specs)
    out_specs = list(out_specs)
    order = sorted(stacked)
    for o in order:
        blk, idx = stacked[o]
        if prevs is None:
            depth = out_shape[o].shape[0]
            out_specs[o] = pl.BlockSpec((depth,) + tuple(blk), lambda *g, idx=idx: (0,) + tuple(idx(*g)))
        else:
            out_specs[o] = pl.BlockSpec((None,) + tuple(blk), lambda *g, idx=idx: (layer,) + tuple(idx(*g)))
    if prevs is None:
        def body(*refs):
            refs = list(refs)
            for o in order:
                full = refs[n_in + o]
                for d in range(full.shape[0]):
                    if d != layer:
                        full[d] = jnp.zeros(full.shape[1:], full.dtype)
                refs[n_in + o] = full.at[layer]
            return kernel(*refs)

        return pl.pallas_call(body, in_specs=in_specs, out_specs=out_specs, out_shape=out_shape, **kw)(*args)

    def body(*refs):
        return kernel(*refs[:n_in], *refs[n_in + len(prevs):])

    return pl.pallas_call(
        body,
        in_specs=list(in_specs) + [pl.BlockSpec(memory_space=pl.ANY)] * len(prevs),
        out_specs=out_specs, out_shape=out_shape,
        input_output_aliases={n_in + j: o for j, o in enumerate(order)},
        **kw)(*args, *prevs)


def _params(sem, vmem=VMEM_LIMIT):
    return pltpu.CompilerParams(dimension_semantics=sem, vmem_limit_bytes=vmem)


def _in_weight_prep_kernel(wt_ref, o_ref, ot_ref, *, blocks, dt_block, dt_row, dt_n, t_rows):
    off = 0
    for src, n in t_rows:
        ot_ref[off:off + n, :] = wt_ref[pl.ds(src, n), :].astype(BF16)
        off += n
    for j, pieces in enumerate(blocks):
        if j == dt_block:
            dt = wt_ref[pl.ds(dt_row, 8), :]
            dt = jnp.where(_iota(dt.shape, 0) < dt_n, dt, 0.0)
            blk = jnp.concatenate([dt, jnp.zeros((120, dt.shape[1]), F32)], axis=0)
        else:
            blk = jnp.concatenate([wt_ref[pl.ds(src, n), :] for src, n in pieces], axis=0)
        o_ref[:, j * 128:(j + 1) * 128] = blk.T.astype(BF16)


IN_COLS_SSD = ((0, 1024),)
IN_COLS_RG = ((2048, 2560),)
IN_COLS_HG = ((2560, 3584),)
IN_COLS_DT = ((3584, 3712),)
IN_COLS_RET_SPLIT = ((3712, 4224), (1536, 2048))


def _in_weight_prep(w_in):
    depth, d, d_in = w_in.shape
    z0 = GROUP_W + (GROUP_W + 2 * SSD_G * SSD_N)
    d0 = z0 + N_HEADS
    blocks = [((src, 128),) for src in tuple(range(0, z0, 128)) + tuple(range(d0, d_in, 128))]
    dt_block = len(blocks)
    blocks.append(())
    half = HEAD // 2
    for base in (d0, d0 + GROUP_W):
        for j in range(2):
            blocks.append(tuple((base + h * HEAD + j * half, half) for h in range(N_HEADS)))
    n_out = len(blocks) * 128
    t_rows = ((d0, 4 * GROUP_W), (d0 + 6 * GROUP_W, 4 * GROUP_W))
    n_t = sum(n for _, n in t_rows)
    wt = jnp.swapaxes(w_in, 1, 2)
    return pl.pallas_call(
        functools.partial(_in_weight_prep_kernel, blocks=tuple(blocks), dt_block=dt_block, dt_row=z0,
                          dt_n=N_HEADS, t_rows=t_rows),
        grid=(depth,),
        in_specs=[pl.BlockSpec((None, d_in, d), lambda l: (l, 0, 0), pipeline_mode=pl.Buffered(1))],
        out_specs=[pl.BlockSpec((None, d, n_out), lambda l: (l, 0, 0)),
                   pl.BlockSpec((None, n_t, d), lambda l: (l, 0, 0))],
        out_shape=[jax.ShapeDtypeStruct((depth, d, n_out), BF16), jax.ShapeDtypeStruct((depth, n_t, d), BF16)],
        compiler_params=_params(("parallel",)),
        name="in_weight_prep",
    )(wt)


def _in_proj_t_kernel(x_ref, g_ref, wt_ref, *o_refs):
    h = _rms(x_ref[...], g_ref[...]).astype(BF16)
    off = 0
    for o_ref in o_refs:
        f = o_ref.shape[0]
        o_ref[...] = lax.dot_general(wt_ref[off:off + f, :], h, (((1,), (1,)), ((), ())),
                                     preferred_element_type=F32)
        off += f


def _in_proj_t(x, g, wt, layer):
    n, d = x.shape
    f = wt.shape[1] // 2
    return pl.pallas_call(
        _in_proj_t_kernel,
        grid=(1,),
        in_specs=[_const_spec((n, d)), _const_spec((1, d)), _layer_spec(wt.shape, layer)],
        out_specs=[_const_spec((f, n))] * 2,
        out_shape=[jax.ShapeDtypeStruct((f, n), F32)] * 2,
        compiler_params=_params(("arbitrary",)),
        name="in_proj_t",
    )(x, g, wt)


def _mem_kv_kernel(x_ref, g_ref, w_ref, k_ref, v_ref, k5_ref, v5_ref):
    h = _rms(x_ref[...], g_ref[...]).astype(BF16)
    d = x_ref.shape[1]
    k = jnp.dot(h, w_ref[:, 0:d].astype(BF16), preferred_element_type=F32)
    v = jnp.dot(h, w_ref[:, d:2 * d].astype(BF16), preferred_element_type=F32)
    k_ref[...] = k
    v_ref[...] = v
    k5_ref[...] = k.reshape(k5_ref.shape)
    v5_ref[...] = v.reshape(v5_ref.shape)


def _mem_kv(mem, g, w, layer, n_mem, prev):
    n, d = mem.shape
    depth = w.shape[0]
    tm = min(ROW_TILE, n)
    seqs = tm // n_mem
    row = pl.BlockSpec((tm, d), lambda i: (i, 0))
    blk5 = ((seqs, n_mem, XA_H, XA_HD), lambda i: (i, 0, 0, 0))
    shape5 = jax.ShapeDtypeStruct((depth, n // n_mem, n_mem, XA_H, XA_HD), F32)
    return _stack_call(
        _mem_kv_kernel, prev, {2: blk5, 3: blk5}, layer,
        in_specs=[row, _const_spec((1, d)), _wspec(w, layer)],
        args=(mem, g, w),
        grid=(n // tm,),
        out_specs=[row, row, None, None],
        out_shape=[jax.ShapeDtypeStruct((n, d), F32)] * 2 + [shape5] * 2,
        compiler_params=_params(("parallel",)),
        name="mem_kv",
    )


def _wspec(w, layer):
    return _layer_spec(w.shape, layer) if w.ndim == 3 else _const_spec(w.shape)


def _norm_matmul(x, g, w, col_ranges, name, layer=None):
    n, d = x.shape
    tm = min(IN_TILE, n)
    w_spec = _wspec(w, layer)
    widths = [sum(b - a for a, b in pieces) for pieces in col_ranges]
    return pl.pallas_call(
        functools.partial(_norm_matmul_kernel, col_ranges=col_ranges),
        grid=(n // tm,),
        in_specs=[pl.BlockSpec((tm, d), lambda i: (i, 0)), _const_spec((1, d)), w_spec],
        out_specs=[pl.BlockSpec((tm, wd), lambda i: (i, 0)) for wd in widths],
        out_shape=[jax.ShapeDtypeStruct((n, wd), F32) for wd in widths],
        compiler_params=_params(("parallel",)),
        name=name,
    )(x, g, w)


def _out_proj_kernel(ys_ref, yr_ref, yg_ref, yh_ref, x_ref, wo_ref, gpost_ref, gpre_ref, wq_ref, x1_ref, q_ref):
    x1 = _mix_out(ys_ref, yr_ref, yg_ref, yh_ref, x_ref, wo_ref, gpost_ref)
    x1_ref[...] = x1
    q_ref[...] = _bdot(_rms(x1, gpre_ref[...]), wq_ref[...])


def _out_proj(ys, x, w_out, g_post, g_pre, w_xq, layer):
    n, d = x.shape
    tm = min(ROW_TILE, n)
    row = lambda wd: pl.BlockSpec((tm, wd), lambda i: (i, 0))
    vec = _const_spec((1, d))
    return pl.pallas_call(
        _out_proj_kernel,
        grid=(n // tm,),
        in_specs=[row(GROUP_W)] * 4 + [row(d), _wspec(w_out, layer), vec, vec, _wspec(w_xq, layer)],
        out_specs=[row(d)] * 2,
        out_shape=[jax.ShapeDtypeStruct((n, d), F32)] * 2,
        compiler_params=_params(("parallel",)),
        name="out_proj",
    )(*ys, x, w_out, g_post, g_pre, w_xq)


def _ffn_tail(o, x1, wxo_ref, gxa_ref, gpre_ref, wgu_ref, wdn_ref, gpost_ref, d_ff):
    a = _bdot(o, wxo_ref[...])
    x2 = x1 + _rms(a, gxa_ref[...])
    h = _rms(x2, gpre_ref[...]).astype(BF16)
    acc = jnp.zeros(x2.shape, F32)
    for j in range(0, d_ff, FF_TILE):
        g = jnp.dot(h, wgu_ref[:, j:j + FF_TILE], preferred_element_type=F32)
        u = jnp.dot(h, wgu_ref[:, d_ff + j:d_ff + j + FF_TILE], preferred_element_type=F32)
        act = (_silu(g) * u).astype(BF16)
        acc = acc + jnp.dot(act, wdn_ref[j:j + FF_TILE, :], preferred_element_type=F32)
    return x2 + _rms(acc, gpost_ref[...])


def _ffn_kernel(o_ref, x1_ref, wxo_ref, gxa_ref, gpre_ref, wgu_ref, wdn_ref, gpost_ref, x3_ref, *, d_ff):
    x3_ref[...] = _ffn_tail(o_ref[...], x1_ref[...], wxo_ref, gxa_ref, gpre_ref, wgu_ref, wdn_ref, gpost_ref,
                            d_ff)


def _ffn(o, x1, w_xo, g_xa, g_pre, w_gu, w_down, g_post, layer):
    n, d = x1.shape
    d_ff = w_down.shape[-2]
    tm = min(ROW_TILE, n)
    row = pl.BlockSpec((tm, d), lambda i: (i, 0))
    vec = _const_spec((1, d))
    return pl.pallas_call(
        functools.partial(_ffn_kernel, d_ff=d_ff),
        grid=(n // tm,),
        in_specs=[row, row, _wspec(w_xo, layer), vec, vec, _wspec(w_gu, layer), _wspec(w_down, layer), vec],
        out_specs=row,
        out_shape=jax.ShapeDtypeStruct((n, d), F32),
        compiler_params=_params(("parallel",)),
        name="attn_out_ffn",
    )(o, x1, w_xo, g_xa, g_pre, w_gu, w_down, g_post)


def _mix_out(ys_ref, yr_ref, yg_ref, yh_ref, x_ref, wo_ref, gpost_ref):
    mix = _bdot(ys_ref[...], wo_ref[0:GROUP_W, :])
    mix = mix + _bdot(yr_ref[...], wo_ref[GROUP_W:2 * GROUP_W, :])
    mix = mix + _bdot(yg_ref[...], wo_ref[2 * GROUP_W:3 * GROUP_W, :])
    mix = mix + _bdot(yh_ref[...], wo_ref[3 * GROUP_W:4 * GROUP_W, :])
    return x_ref[...] + _rms(mix, gpost_ref[...])


def _xattn_ffn_kernel(ys_ref, yr_ref, yg_ref, yh_ref, x_ref, wo_ref, gmix_ref, k_ref, v_ref, gq_ref, wq_ref,
                      wxo_ref, gxa_ref, gpre_ref, wgu_ref, wdn_ref, gpost_ref, x3_ref, o_scr, *, d_ff):
    x1 = _mix_out(ys_ref, yr_ref, yg_ref, yh_ref, x_ref, wo_ref, gmix_ref)
    q = _bdot(_rms(x1, gq_ref[...]), wq_ref[...])
    scale = XA_HD ** -0.5
    for h in range(XA_H):
        sl = slice(h * XA_HD, (h + 1) * XA_HD)
        s = _bdot_nt(q[:, sl], k_ref[0, :, sl]) * scale
        o_scr[:, sl] = _softmax_pv(s, v_ref[0, :, sl]).astype(BF16)
    x3_ref[...] = _ffn_tail(o_scr[...], x1, wxo_ref, gxa_ref, gpre_ref, wgu_ref, wdn_ref, gpost_ref, d_ff)


def _xattn_ffn(ys, x, w_out, g_mix, k, v, g_q, w_xq, w_xo, g_xa, g_pre, w_gu, w_down, g_post, seq_len, layer):
    n, d = x.shape
    n_seq, n_mem, _ = k.shape
    d_ff = w_down.shape[-2]
    nt = seq_len // XA_TILE
    row = lambda wd: pl.BlockSpec((XA_TILE, wd), lambda b, t: (b * nt + t, 0))
    kv = pl.BlockSpec((1, n_mem, d), lambda b, t: (b, 0, 0))
    vec = _const_spec((1, d))
    return pl.pallas_call(
        functools.partial(_xattn_ffn_kernel, d_ff=d_ff),
        grid=(n_seq, nt),
        in_specs=[row(GROUP_W)] * 4 + [row(d), _wspec(w_out, layer), vec, kv, kv, vec, _wspec(w_xq, layer),
                                       _wspec(w_xo, layer), vec, vec, _wspec(w_gu, layer),
                                       _wspec(w_down, layer), vec],
        out_specs=row(d),
        out_shape=jax.ShapeDtypeStruct((n, d), F32),
        scratch_shapes=[pltpu.VMEM((XA_TILE, d), BF16)],
        compiler_params=_params(("parallel", "parallel"), XA_VMEM_LIMIT),
        name="xattn_ffn",
    )(*ys, x, w_out, g_mix, k, v, g_q, w_xq, w_xo, g_xa, g_pre, w_gu, w_down, g_post)


def _softmax_pv(s, v):
    m = jnp.max(s, axis=-1, keepdims=True)
    p = jnp.exp(s - m)
    return _bdot(p, v) / jnp.sum(p, axis=-1, keepdims=True)


def _attn_sample_kernel(q_ref, k_ref, v_ref, o_ref, *, seqs, seg):
    scale = XA_HD ** -0.5
    rows = seqs * seg
    n_mem = k_ref.shape[2]
    wq = jnp.concatenate([q_ref[:, h * XA_HD:(h + 1) * XA_HD] for h in range(XA_H)], axis=0)
    wshape = (XA_H * rows, n_mem * XA_H)
    valid = (_iota(wshape, 0) >> (rows.bit_length() - 1)) == (_iota(wshape, 1) & (XA_H - 1))
    row_seq = _iota((rows, XA_HD), 0) >> (seg.bit_length() - 1)
    accs = [jnp.zeros((rows, XA_HD), F32) for _ in range(XA_H)]
    for i in range(seqs):
        k2 = k_ref[0, i].reshape(n_mem * XA_H, XA_HD)
        v2 = v_ref[0, i].reshape(n_mem * XA_H, XA_HD)
        s = jnp.where(valid, _bdot_nt(wq, k2) * scale, -jnp.inf)
        o = _softmax_pv(s, v2)
        for h in range(XA_H):
            accs[h] = jnp.where(row_seq == i, o[h * rows:(h + 1) * rows], accs[h])
    for h in range(XA_H):
        o_ref[:, h * XA_HD:(h + 1) * XA_HD] = accs[h]


def _attention_sample(q, cache_k, cache_v, layer, seq_len):
    n, d = q.shape
    _, n_seq, n_mem, heads, hd = cache_k.shape
    kv = pl.BlockSpec((1, ATT_SEQS, n_mem, heads, hd), lambda i: (layer, i, 0, 0, 0))
    row = pl.BlockSpec((ATT_SEQS * seq_len, d), lambda i: (i, 0))
    return pl.pallas_call(
        functools.partial(_attn_sample_kernel, seqs=ATT_SEQS, seg=seq_len),
        grid=(n_seq // ATT_SEQS,),
        in_specs=[row, kv, kv],
        out_specs=row,
        out_shape=jax.ShapeDtypeStruct((n, d), F32),
        compiler_params=_params(("parallel",)),
        name="xattn_sample",
    )(q, cache_k, cache_v)


def _ssd_intra(q, k, v, cum_x, cum_t, mask):
    sc = _bdot_nt(q, k)
    outs = []
    for hh in range(2):
        col = cum_x[:, hh * HEAD:(hh + 1) * HEAD]
        dec = jnp.exp2(jnp.where(mask, col - cum_t[hh:hh + 1, :], -jnp.inf))
        outs.append(_bdot(sc * dec, v[:, hh * HEAD:(hh + 1) * HEAD]))
    return jnp.concatenate(outs, axis=-1)


def _ssd_common(u, dtraw_ref, dtb_ref, alog_ref, conv, seg_shift):
    rows = u.shape[0]
    xbc = _silu(conv)
    xs = xbc[:, 0:GROUP_W]
    bm = xbc[:, GROUP_W:2 * GROUP_W]
    cm = xbc[:, 2 * GROUP_W:3 * GROUP_W]
    dt = _softplus(dtraw_ref[...] + dtb_ref[...])
    la = (-jnp.exp(alog_ref[...]) * LOG2_E) * dt
    cum = _seg_cumsum(la, seg_shift)
    expand = _head_expand_mat()
    dt_x = _dot_sel(dt, expand)
    cum_x = _dot_sel(cum, expand)
    sel8 = (_iota((8, 128), 0) == _iota((8, 128), 1)).astype(BF16)
    cum_t = _sel_dot_nt(sel8, cum)
    expand_full = ((_iota((128, 4 * 128), 1) >> 7) == _iota((128, 4 * 128), 0)).astype(BF16)
    cum_full = _dot_sel(cum, expand_full)
    return xs, bm, cm, dt_x, cum_x, cum_t, cum_full


def _ssd_finish(y, xs, z, d_ref, g_ref):
    y = y + d_ref[...] * xs
    return _rms(y * _silu(z), g_ref[...])


def _ssd_prompt_kernel(u_ref, dtraw_ref, cw_ref, cb_ref, dtb_ref, alog_ref, d_ref, g_ref,
                       y_ref, s_out_ref, buf_out_ref, s_ref, xx_ref, y_scr):
    t = pl.program_id(1)
    rows = u_ref.shape[0]

    @pl.when(t == 0)
    def _():
        s_ref[...] = jnp.zeros(s_ref.shape, F32)

    u = u_ref[...]
    conv = _conv_taps_carry(u[:, GROUP_W:], xx_ref, cw_ref, cb_ref, t == 0)
    xs, bm, cm, dt_x, cum_x, _, cum_full = _ssd_common(u, dtraw_ref, dtb_ref, alog_ref, conv, 6)
    v = xs * dt_x
    ecum_x = jnp.exp2(cum_x)
    hmask01 = _head_block_mask().astype(BF16)
    gmask01 = ((_iota((GROUP_W, GROUP_W), 0) >> 7) == (_iota((GROUP_W, GROUP_W), 1) >> 7)).astype(BF16)
    wrow = _iota((CHUNK, GROUP_W), 0)
    wcol = _iota((CHUNK, GROUP_W), 1) & (CHUNK - 1)
    n_chunks = rows // CHUNK
    groups = [slice(g * 128, (g + 1) * 128) for g in range(SSD_G)]
    a_wide, ds_t = [], []
    for c in range(n_chunks):
        rs = slice(c * CHUNK, (c + 1) * CHUNK)
        cum_c = cum_x[rs]
        last_x = cum_x[c * CHUNK + CHUNK - 1:(c + 1) * CHUNK, :]
        vend = v[rs] * jnp.exp2(last_x - cum_c)
        cum_row = jnp.sum(jnp.where(wrow == wcol, cum_c, 0.0), axis=0, keepdims=True)
        dec = jnp.exp2(jnp.where(wcol <= wrow, cum_c - cum_row, -jnp.inf))
        a_wide.append(_bdot_nt(cm[rs], _block_diag_rows(bm[rs], gmask01)) * dec)
        ds_t.append([_bdot_tn(vend[:, ls], bm[rs, ls]) for ls in groups])
    s_in = []
    s_cur = [s_ref[ls, :] for ls in groups]
    for c in range(n_chunks):
        last_full = cum_full[c * CHUNK + CHUNK - 1:(c + 1) * CHUNK, :]
        s_in.append([x.astype(BF16) for x in s_cur])
        for g in range(SSD_G):
            decay = jnp.concatenate(
                [jnp.broadcast_to(jnp.exp2(last_full[:, h * 128:(h + 1) * 128]), (HEAD, 128))
                 for h in (2 * g, 2 * g + 1)], axis=0)
            s_cur[g] = decay * s_cur[g] + ds_t[c][g]
    for g, ls in enumerate(groups):
        s_ref[ls, :] = s_cur[g]
    for c in range(n_chunks):
        rs = slice(c * CHUNK, (c + 1) * CHUNK)
        y_intra = _bdot(a_wide[c], _block_diag_rows(v[rs], hmask01))
        y_inter = [_bdot_nt(cm[rs, ls], s_in[c][g]) for g, ls in enumerate(groups)]
        y_scr[rs, :] = y_intra + jnp.concatenate(y_inter, axis=-1) * ecum_x[rs]
    y_ref[...] = _ssd_finish(y_scr[...], xs, u[:, 0:GROUP_W], d_ref, g_ref)

    @pl.when(t == pl.num_programs(1) - 1)
    def _():
        for h in range(N_HEADS):
            s_out_ref[0, h] = s_ref[h * HEAD:(h + 1) * HEAD, :]
        buf_out_ref[0] = xx_ref[pl.ds(8 - (CONV_W - 1), CONV_W - 1), :]


def _ssd_sample_kernel(u_ref, dtraw_ref, s0_ref, buf_ref, cw_ref, cb_ref, dtb_ref, alog_ref, d_ref,
                       g_ref, y_ref, s_out_ref, buf_out_ref, *, seqs):
    rows = u_ref.shape[0]
    u = u_ref[...]
    conv, new_buf = _conv_taps_seq4(u[:, GROUP_W:], buf_ref[...], cw_ref, cb_ref)
    buf_out_ref[...] = new_buf
    mask = _seg_causal_mask(rows, 2)
    xs, bm, cm, dt_x, cum_x, cum_t, cum_full = _ssd_common(u, dtraw_ref, dtb_ref, alog_ref, conv, 2)
    v = xs * dt_x
    ecum_x = jnp.exp2(cum_x)
    pick_last = (_iota((rows, rows), 1) == (_iota((rows, rows), 0) | 3)).astype(BF16)
    last_x = _sel_dot(pick_last, cum_x)
    vend = v * jnp.exp2(last_x - cum_x)
    row_seq = _iota((rows, 128), 0) >> 2
    ys = []
    for g in range(SSD_G):
        ls = slice(g * 128, (g + 1) * 128)
        q = cm[:, ls]
        k = bm[:, ls]
        y_g = _ssd_intra(q, k, v[:, ls], cum_x[:, ls], cum_t[2 * g:2 * g + 2, :], mask)
        y_int = jnp.zeros((rows, 128), F32)
        for i in range(seqs):
            in_seq = row_seq == i
            s_prev = jnp.concatenate([s0_ref[i, 2 * g], s0_ref[i, 2 * g + 1]], axis=0)
            y_int = jnp.where(in_seq, _bdot_nt(q, s_prev), y_int)
            ds = _bdot_tn(jnp.where(in_seq, vend[:, ls], 0.0), k)
            for hh in range(2):
                h = 2 * g + hh
                decay = jnp.exp2(cum_full[4 * i + 3:4 * i + 4, h * 128:(h + 1) * 128])
                s_out_ref[i, h] = decay * s0_ref[i, h] + ds[hh * HEAD:(hh + 1) * HEAD, :]
        ys.append(y_g + y_int * ecum_x[:, ls])
    y_ref[...] = _ssd_finish(jnp.concatenate(ys, axis=-1), xs, u[:, 0:GROUP_W], d_ref, g_ref)


def _ssd(u, dtraw, state, buf, cw, cb, dtb, alog, d_x, gain, n_seq, seq_len, layer, prev=None):
    n = u.shape[0]
    cch = cw.shape[1]
    vec = lambda wd: _const_spec((1, wd))
    common_in = [_const_spec(cw.shape), vec(cch), vec(128), vec(128), vec(GROUP_W), vec(GROUP_W)]
    if state is None:
        nt = seq_len // MIX_TILE
        row = lambda wd: pl.BlockSpec((MIX_TILE, wd), lambda b, t: (b * nt + t, 0))
        return pl.pallas_call(
            _ssd_prompt_kernel,
            grid=(n_seq, nt),
            in_specs=[row(u.shape[1]), row(128)] + common_in,
            out_specs=[row(GROUP_W),
                       pl.BlockSpec((1, N_HEADS, HEAD, SSD_N), lambda b, t: (b, 0, 0, 0)),
                       pl.BlockSpec((1, CONV_W - 1, cch), lambda b, t: (b, 0, 0))],
            out_shape=[jax.ShapeDtypeStruct((n, GROUP_W), F32),
                       jax.ShapeDtypeStruct((n_seq, N_HEADS, HEAD, SSD_N), F32),
                       jax.ShapeDtypeStruct((n_seq, CONV_W - 1, cch), F32)],
            scratch_shapes=[pltpu.VMEM((GROUP_W, SSD_N), F32), pltpu.VMEM((MIX_TILE + 8, cch), F32),
                            pltpu.VMEM((MIX_TILE, GROUP_W), F32)],
            compiler_params=_params(("parallel", "arbitrary")),
            name="ssd_prompt",
        )(u, dtraw, cw, cb, dtb, alog, d_x, gain)
    rows = SAMPLE_SEQS * seq_len
    row = lambda wd: pl.BlockSpec((rows, wd), lambda i: (i, 0))
    st = pl.BlockSpec((None, SAMPLE_SEQS, N_HEADS, HEAD, SSD_N), lambda i: (layer, i, 0, 0, 0))
    return _stack_call(
        functools.partial(_ssd_sample_kernel, seqs=SAMPLE_SEQS), prev,
        {1: ((SAMPLE_SEQS, N_HEADS, HEAD, SSD_N), lambda i: (i, 0, 0, 0))}, layer,
        in_specs=[row(u.shape[1]), row(128), st, row(cch)] + common_in,
        args=(u, dtraw, state, buf, cw, cb, dtb, alog, d_x, gain),
        grid=(n_seq // SAMPLE_SEQS,),
        out_specs=[row(GROUP_W), None, row(cch)],
        out_shape=[jax.ShapeDtypeStruct((n, GROUP_W), F32),
                   jax.ShapeDtypeStruct(state.shape, F32),
                   jax.ShapeDtypeStruct((n, cch), F32)],
        compiler_params=_params(("parallel",)),
        name="ssd_sample",
    )


def _rope_split(x, cos, sin_signed):
    swapped = jnp.concatenate([x[:, 128:], x[:, :128]], axis=1)
    return x * cos + swapped * sin_signed


def _ret_qkv(u_ref, cos_ref, sin_ref):
    u = u_ref[...]
    q = _rope_split(u[:, 0:GROUP_W], cos_ref[...], sin_ref[...])
    k = _rope_split(u[:, GROUP_W:2 * GROUP_W], cos_ref[...], sin_ref[...]) * (HEAD ** -0.5)
    return q, k, u[:, 2 * GROUP_W:3 * GROUP_W], u[:, 3 * GROUP_W:4 * GROUP_W]


def _ret_prompt_kernel(u_ref, cos_ref, sin_ref, ecum_ref, eend_ref, elast_ref, dec_ref, g_ref,
                       y_ref, s_out_ref, s_ref, y_scr):
    t = pl.program_id(1)
    rows = u_ref.shape[0]

    @pl.when(t == 0)
    def _():
        s_ref[...] = jnp.zeros(s_ref.shape, F32)

    q, k, v, gate = _ret_qkv(u_ref, cos_ref, sin_ref)
    r_i = _iota((GROUP_W, GROUP_W), 0)
    c_i = _iota((GROUP_W, GROUP_W), 1)
    kmask01 = ((r_i >> 6) == ((c_i & 127) >> 5)).astype(BF16)
    vmask01 = _head_block_mask().astype(BF16)
    smask = ((r_i & 127) >> 5) == (c_i >> 6)
    chunks = [slice(c * CHUNK, (c + 1) * CHUNK) for c in range(rows // CHUNK)]
    a_wide = [_bdot_nt(q[rs], _block_diag_rows(k[rs], kmask01)) * dec_ref[...] for rs in chunks]
    ds = [jnp.where(smask, _bdot_tn(k[rs], v[rs] * eend_ref[...]), 0.0) for rs in chunks]
    s = s_ref[...]
    s_in = []
    for d in ds:
        s_in.append(s.astype(BF16))
        s = elast_ref[...] * s + d
    s_ref[...] = s
    for rs, a, s_c in zip(chunks, a_wide, s_in):
        y_intra = _bdot(a, _block_diag_rows(v[rs], vmask01))
        y_scr[rs, :] = y_intra + _bdot(q[rs], s_c) * ecum_ref[...]
    y_ref[...] = _silu(gate) * (_head_stat(y_scr[...], True) * g_ref[...])

    @pl.when(t == pl.num_programs(1) - 1)
    def _():
        half = HEAD // 2
        for h in range(N_HEADS):
            vs = slice(h * HEAD, (h + 1) * HEAD)
            s_out_ref[0, h, 0:half, :] = s_ref[h * half:(h + 1) * half, vs]
            s_out_ref[0, h, half:HEAD, :] = s_ref[128 + h * half:128 + (h + 1) * half, vs]


def _ret_tables(block, seg):
    log_gamma = jnp.log(1.0 - jnp.exp2(-5.0 - jnp.arange(N_HEADS, dtype=F32)))
    pos = (jnp.arange(block) % seg).astype(F32)
    cum = (pos[:, None] + 1.0) * log_gamma[None, :]
    last = seg * log_gamma
    rep = lambda a: jnp.repeat(a, HEAD, axis=-1)
    same = (jnp.arange(block)[:, None] // seg) == (jnp.arange(block)[None, :] // seg)
    mask = same & (jnp.arange(block)[None, :] <= jnp.arange(block)[:, None])
    dec = jnp.exp(jnp.where(mask[None], cum.T[:, :, None] - cum.T[:, None, :], -jnp.inf))
    return (rep(jnp.exp(cum)), rep(jnp.exp(last[None, :] - cum)), rep(jnp.exp(last)[None, :]), dec)


def _ret(u, cos, sin_signed, gain, n_seq, seq_len):
    n = u.shape[0]
    vec = _const_spec((1, GROUP_W))
    nt = seq_len // MIX_TILE
    ecum, eend, elast, dec = _ret_tables(CHUNK, CHUNK)
    dec = dec.transpose(1, 0, 2).reshape(CHUNK, N_HEADS * CHUNK)
    row = lambda wd: pl.BlockSpec((MIX_TILE, wd), lambda b, t: (b * nt + t, 0))
    pos = pl.BlockSpec((MIX_TILE, GROUP_W), lambda b, t: (t, 0))
    return pl.pallas_call(
        _ret_prompt_kernel,
        grid=(n_seq, nt),
        in_specs=[row(u.shape[1]), pos, pos, _const_spec(ecum.shape), _const_spec(eend.shape), vec,
                  _const_spec(dec.shape), vec],
        out_specs=[row(GROUP_W), pl.BlockSpec((1, N_HEADS, HEAD, HEAD), lambda b, t: (b, 0, 0, 0))],
        out_shape=[jax.ShapeDtypeStruct((n, GROUP_W), F32),
                   jax.ShapeDtypeStruct((n_seq, N_HEADS, HEAD, HEAD), F32)],
        scratch_shapes=[pltpu.VMEM((GROUP_W, GROUP_W), F32), pltpu.VMEM((MIX_TILE, GROUP_W), F32)],
        compiler_params=_params(("parallel", "arbitrary")),
        name="ret_prompt",
    )(u, cos, sin_signed, ecum, eend, elast, dec, gain)


def _rg_gates(xr, wa_ref, ba_ref, wx_ref, bx_ref, lam_ref):
    r_gate = _sigmoid(_bdot(xr, wa_ref[...]) + ba_ref[...])
    i_gate = _sigmoid(_bdot(xr, wx_ref[...]) + bx_ref[...])
    log_a = (-RG_C * _softplus(-lam_ref[...])) * r_gate
    a = jnp.exp(log_a)
    b = jnp.sqrt(-jnp.tanh(log_a) * (a * a + 1.0)) * (i_gate * xr)
    return a, b


def _rg_scan(a, b, seg):
    pos = _iota(a.shape, 0) & (seg - 1)
    d = 1
    while d < seg:
        ok = pos >= d
        a_sh = jnp.where(ok, pltpu.roll(a, d, 0), 1.0)
        b_sh = jnp.where(ok, pltpu.roll(b, d, 0), 0.0)
        b = a * b_sh + b
        a = a * a_sh
        d *= 2
    return a, b


def _rg_scan_carry(a, b, h0):
    a, b = _rg_scan(a, b, 8)
    outs = []
    carry = h0
    for g in range(a.shape[0] // 8):
        h_g = b[g * 8:(g + 1) * 8] + a[g * 8:(g + 1) * 8] * carry
        outs.append(h_g)
        carry = h_g[7:8]
    return jnp.concatenate(outs, axis=0)


def _rg_prompt_kernel(u_ref, cw_ref, cb_ref, wa_ref, ba_ref, wx_ref, bx_ref, lam_ref,
                      y_ref, h_out_ref, buf_out_ref, h_ref, xx_ref):
    t = pl.program_id(1)
    rows = u_ref.shape[0]

    @pl.when(t == 0)
    def _():
        h_ref[...] = jnp.zeros(h_ref.shape, F32)

    u = u_ref[...]
    xr = _conv_taps_carry(u[:, 0:GROUP_W], xx_ref, cw_ref, cb_ref, t == 0)
    a, b = _rg_gates(xr, wa_ref, ba_ref, wx_ref, bx_ref, lam_ref)
    hseq = _rg_scan_carry(a, b, h_ref[0:1, :])
    h_ref[...] = jnp.broadcast_to(hseq[rows - 1:rows, :], h_ref.shape)
    y_ref[...] = _gelu_tanh(u[:, GROUP_W:]) * hseq

    @pl.when(t == pl.num_programs(1) - 1)
    def _():
        h_out_ref[0] = hseq[rows - 1:rows, :]
        buf_out_ref[0] = xx_ref[pl.ds(8 - (CONV_W - 1), CONV_W - 1), :]


def _rg_sample_kernel(u_ref, h0_ref, buf_ref, cw_ref, cb_ref, wa_ref, ba_ref, wx_ref, bx_ref, lam_ref,
                      y_ref, h_out_ref, buf_out_ref):
    u = u_ref[...]
    xr, new_buf = _conv_taps_seq4(u[:, 0:GROUP_W], buf_ref[...], cw_ref, cb_ref)
    buf_out_ref[...] = new_buf
    a, b = _rg_gates(xr, wa_ref, ba_ref, wx_ref, bx_ref, lam_ref)
    a_cum, h_loc = _rg_scan(a, b, 4)
    hseq = h_loc + a_cum * h0_ref[...]
    h_out_ref[...] = hseq
    y_ref[...] = _gelu_tanh(u[:, GROUP_W:]) * hseq


def _rg(u, h0_rows, buf, cw, cb, wa, ba, wx, bx, lam, n_seq, seq_len):
    n = u.shape[0]
    vec = _const_spec((1, GROUP_W))
    common_in = [_const_spec(cw.shape), vec, _const_spec(wa.shape), vec, _const_spec(wx.shape), vec, vec]
    if h0_rows is None:
        nt = seq_len // RG_TILE
        row = lambda wd: pl.BlockSpec((RG_TILE, wd), lambda b, t: (b * nt + t, 0))
        return pl.pallas_call(
            _rg_prompt_kernel,
            grid=(n_seq, nt),
            in_specs=[row(u.shape[1])] + common_in,
            out_specs=[row(GROUP_W), pl.BlockSpec((1, 1, GROUP_W), lambda b, t: (b, 0, 0)),
                       pl.BlockSpec((1, CONV_W - 1, GROUP_W), lambda b, t: (b, 0, 0))],
            out_shape=[jax.ShapeDtypeStruct((n, GROUP_W), F32),
                       jax.ShapeDtypeStruct((n_seq, 1, GROUP_W), F32),
                       jax.ShapeDtypeStruct((n_seq, CONV_W - 1, GROUP_W), F32)],
            scratch_shapes=[pltpu.VMEM((8, GROUP_W), F32), pltpu.VMEM((RG_TILE + 8, GROUP_W), F32)],
            compiler_params=_params(("parallel", "arbitrary")),
            name="rglru_prompt",
        )(u, cw, cb, wa, ba, wx, bx, lam)
    rows = SAMPLE_SEQS * seq_len
    row = lambda wd: pl.BlockSpec((rows, wd), lambda i: (i, 0))
    return pl.pallas_call(
        _rg_sample_kernel,
        grid=(n_seq // SAMPLE_SEQS,),
        in_specs=[row(u.shape[1]), row(GROUP_W), row(GROUP_W)] + common_in,
        out_specs=[row(GROUP_W)] * 3,
        out_shape=[jax.ShapeDtypeStruct((n, GROUP_W), F32)] * 3,
        compiler_params=_params(("parallel",)),
        name="rglru_sample",
    )(u, h0_rows, buf, cw, cb, wa, ba, wx, bx, lam)


def _hg_inputs(u_ref, lb_ref, seg_shift):
    u = u_ref[...]
    rows = u.shape[0]
    lb = lb_ref[...]
    fg = lb + (1.0 - lb) * _sigmoid(u[:, GROUP_W:2 * GROUP_W])
    q = _silu(u[:, 0:GROUP_W])
    k = 1.0 - fg
    v = u[:, 2 * GROUP_W:3 * GROUP_W]
    cum = _seg_cumsum(jnp.log(fg), seg_shift)
    return q, k, v, cum, u[:, 3 * GROUP_W:4 * GROUP_W]


def _hg_prompt_kernel(u_ref, lb_ref, g_ref, y_ref, s_out_ref, st_ref, y_scr):
    t = pl.program_id(1)
    rows = u_ref.shape[0]

    @pl.when(t == 0)
    def _():
        st_ref[...] = jnp.zeros(st_ref.shape, F32)

    q, k, v, cum, gate = _hg_inputs(u_ref, lb_ref, 6)
    hmask = _head_block_mask()
    hmask01 = hmask.astype(BF16)
    ones_bd = hmask01
    y_scr[...] = jnp.dot((q * k).astype(BF16), ones_bd, preferred_element_type=F32) * v
    row = _iota((rows, GROUP_W), 0)
    trow = _iota((CHUNK, GROUP_W), 0)
    tcol = _iota((CHUNK, GROUP_W), 1) & (CHUNK - 1)
    cum = cum * LOG2_E
    levels = []
    blk_last = cum
    h = 1
    while h < CHUNK:
        upper = (row & h) != 0
        qn = jnp.where(upper, q * jnp.exp2(cum - pltpu.roll(blk_last, h, 0)), 0.0)
        kn = k * jnp.exp2(blk_last - cum)
        sh = (2 * h).bit_length() - 1
        pair = ((trow >> sh) == (tcol >> sh)) & ((tcol & h) == 0)
        levels.append((pair, qn, kn))
        blk_last = jnp.where(upper, blk_last, pltpu.roll(blk_last, rows - h, 0))
        h *= 2
    q_in = q * jnp.exp2(cum)
    k_end = k * jnp.exp2(blk_last - cum)
    n_chunks = rows // CHUNK
    a_wide, ds_t = [], []
    for c in range(n_chunks):
        rs = slice(c * CHUNK, (c + 1) * CHUNK)
        a_c = jnp.zeros((CHUNK, GROUP_W), F32)
        for pair, qn, kn in levels:
            g = _bdot_nt(qn[rs], _block_diag_rows(kn[rs], hmask01))
            a_c = a_c + jnp.where(pair, g, 0.0)
        a_wide.append(a_c)
        ds_t.append(jnp.where(hmask, _bdot_tn(v[rs], k_end[rs]), 0.0))
    st = st_ref[...]
    st_in = []
    for c in range(n_chunks):
        st_in.append(st.astype(BF16))
        st = jnp.exp2(blk_last[c * CHUNK:c * CHUNK + 1, :]) * st + ds_t[c]
    st_ref[...] = st
    for c in range(n_chunks):
        rs = slice(c * CHUNK, (c + 1) * CHUNK)
        y_c = _bdot(a_wide[c], _block_diag_rows(v[rs], hmask01)) + _bdot_nt(q_in[rs], st_in[c])
        y_scr[rs, :] = y_scr[rs, :] + y_c
    y_ref[...] = _silu(gate) * (_head_stat(y_scr[...], False) * g_ref[...])

    @pl.when(t == pl.num_programs(1) - 1)
    def _():
        eye = (_iota((HEAD, HEAD), 0) == _iota((HEAD, HEAD), 1)).astype(BF16)
        for h in range(N_HEADS):
            s_out_ref[0, h] = _sel_dot_nt(eye, st_ref[h * HEAD:(h + 1) * HEAD, h * HEAD:(h + 1) * HEAD])


def _hg(u, lb, gain, n_seq, seq_len):
    n = u.shape[0]
    vec = _const_spec((1, GROUP_W))
    nt = seq_len // MIX_TILE
    row = lambda wd: pl.BlockSpec((MIX_TILE, wd), lambda b, t: (b * nt + t, 0))
    return pl.pallas_call(
        _hg_prompt_kernel,
        grid=(n_seq, nt),
        in_specs=[row(u.shape[1]), vec, vec],
        out_specs=[row(GROUP_W), pl.BlockSpec((1, N_HEADS, HEAD, HEAD), lambda b, t: (b, 0, 0, 0))],
        out_shape=[jax.ShapeDtypeStruct((n, GROUP_W), F32),
                   jax.ShapeDtypeStruct((n_seq, N_HEADS, HEAD, HEAD), F32)],
        scratch_shapes=[pltpu.VMEM((GROUP_W, GROUP_W), F32), pltpu.VMEM((MIX_TILE, GROUP_W), F32)],
        compiler_params=_params(("parallel", "arbitrary")),
        name="hgrn_prompt",
    )(u, lb, gain)


def _t_recurrence(q_scr, k_scr, v, dec_row, dec_scr, s_ref, n_seq, seq_len):
    cols = [slice(t * n_seq, (t + 1) * n_seq) for t in range(seq_len)]
    v_t = [v[:, c] for c in cols]
    outs = [jnp.zeros((HEAD, n_seq), F32) for _ in cols]
    for k in range(HEAD):
        s = s_ref[k]
        for t, c in enumerate(cols):
            dec = dec_row if dec_scr is None else dec_scr[k:k + 1, c]
            s = dec * s + k_scr[k:k + 1, c] * v_t[t]
            outs[t] = outs[t] + q_scr[k:k + 1, c] * s
        s_ref[k] = s
    return jnp.concatenate(outs, axis=1)


def _t_head_norm(o, center):
    if center:
        o = o - jnp.mean(o, axis=0, keepdims=True)
    return o * lax.rsqrt(jnp.mean(o * o, axis=0, keepdims=True) + EPS)


def _ret_t_kernel(q_ref, k_ref, v_ref, g_ref, cos_ref, sin_ref, gam_ref, gain_ref, s0_ref,
                  y_ref, s_ref, q_scr, k_scr, *, n_seq, seq_len):
    half = HEAD // 2

    def rope(x):
        swapped = jnp.concatenate([x[half:], x[:half]], axis=0)
        return x * cos_ref[...] + swapped * sin_ref[...]

    q_scr[...] = rope(q_ref[...])
    k_scr[...] = rope(k_ref[...]) * (HEAD ** -0.5)
    s_ref[0] = s0_ref[0]
    o = _t_recurrence(q_scr, k_scr, v_ref[...], gam_ref[0], None, s_ref.at[0], n_seq, seq_len)
    gain = jnp.concatenate([gain_ref[...]] * seq_len, axis=1)
    parts = [_t_head_norm(o[:, t * n_seq:(t + 1) * n_seq], True) for t in range(seq_len)]
    y_ref[...] = _silu(g_ref[...]) * (jnp.concatenate(parts, axis=1) * gain)


def _hg_t_kernel(q_ref, f_ref, i_ref, g_ref, lb_ref, gain_ref, s0_ref, y_ref, s_ref, q_scr, k_scr, f_scr,
                 *, n_seq, seq_len):
    lb = jnp.concatenate([lb_ref[...]] * seq_len, axis=1)
    fg = lb + (1.0 - lb) * _sigmoid(f_ref[...])
    q_scr[...] = _silu(q_ref[...])
    k_scr[...] = 1.0 - fg
    f_scr[...] = fg
    s_ref[0] = s0_ref[0]
    o = _t_recurrence(q_scr, k_scr, i_ref[...], None, f_scr, s_ref.at[0], n_seq, seq_len)
    gain = jnp.concatenate([gain_ref[...]] * seq_len, axis=1)
    parts = [_t_head_norm(o[:, t * n_seq:(t + 1) * n_seq], False) for t in range(seq_len)]
    y_ref[...] = _silu(g_ref[...]) * (jnp.concatenate(parts, axis=1) * gain)


def _t_specs(n, n_seq):
    head_rows = lambda section: pl.BlockSpec((HEAD, n), lambda h, s=section: (N_HEADS * s + h, 0))
    per_head = pl.BlockSpec((HEAD, n_seq), lambda h: (h, 0))
    state_in = lambda layer: pl.BlockSpec((None, 1, HEAD, HEAD, n_seq), lambda h: (layer, h, 0, 0, 0))
    state_out = ((1, HEAD, HEAD, n_seq), lambda h: (h, 0, 0, 0))
    return head_rows, per_head, state_in, state_out


def _ret_t(u_t, cos_t, sin_t, gam, gain_b, state_t, n_seq, seq_len, layer, prev):
    n = u_t.shape[1]
    head_rows, per_head, state_in, state_out = _t_specs(n, n_seq)
    return _stack_call(
        functools.partial(_ret_t_kernel, n_seq=n_seq, seq_len=seq_len), prev, {1: state_out}, layer,
        in_specs=[head_rows(0), head_rows(1), head_rows(2), head_rows(3), _const_spec(cos_t.shape),
                  _const_spec(sin_t.shape), pl.BlockSpec((1, 1, n_seq), lambda h: (h, 0, 0)), per_head,
                  state_in(layer)],
        args=(u_t, u_t, u_t, u_t, cos_t, sin_t, gam, gain_b, state_t),
        grid=(N_HEADS,),
        out_specs=[pl.BlockSpec((HEAD, n), lambda h: (h, 0)), None],
        out_shape=[jax.ShapeDtypeStruct((GROUP_W, n), F32), jax.ShapeDtypeStruct(state_t.shape, F32)],
        scratch_shapes=[pltpu.VMEM((HEAD, n), F32)] * 2,
        compiler_params=_params(("parallel",)),
        name="ret_sample_t",
    )


def _hg_t(u_t, lb_b, gain_b, state_t, n_seq, seq_len, layer, prev):
    n = u_t.shape[1]
    head_rows, per_head, state_in, state_out = _t_specs(n, n_seq)
    return _stack_call(
        functools.partial(_hg_t_kernel, n_seq=n_seq, seq_len=seq_len), prev, {1: state_out}, layer,
        in_specs=[head_rows(0), head_rows(1), head_rows(2), head_rows(3), per_head, per_head, state_in(layer)],
        args=(u_t, u_t, u_t, u_t, lb_b, gain_b, state_t),
        grid=(N_HEADS,),
        out_specs=[pl.BlockSpec((HEAD, n), lambda h: (h, 0)), None],
        out_shape=[jax.ShapeDtypeStruct((GROUP_W, n), F32), jax.ShapeDtypeStruct(state_t.shape, F32)],
        scratch_shapes=[pltpu.VMEM((HEAD, n), F32)] * 3,
        compiler_params=_params(("parallel",)),
        name="hgrn_sample_t",
    )


def _rope_tables_t(pos0, seq_len, n_seq):
    half = HEAD // 2
    pos = pos0 + jnp.arange(seq_len, dtype=F32)
    inv = ROPE_BASE ** (-jnp.arange(half, dtype=F32) / half)
    ang = pos[:, None] * inv
    cos = jnp.tile(jnp.cos(ang).T, (2, 1))
    sin = jnp.sin(ang).T
    sin_signed = jnp.concatenate([-sin, sin], axis=0)
    return jnp.repeat(cos, n_seq, axis=1), jnp.repeat(sin_signed, n_seq, axis=1)


def _block_diag(w):
    h, i, j = w.shape
    eye = jnp.eye(h, dtype=w.dtype)
    return (eye[:, None, :, None] * w[:, :, None, :]).reshape(h * i, h * j)


def _rope_tables(pos0, seq_len):
    half = HEAD // 2
    pos = pos0 + jnp.arange(seq_len, dtype=F32)
    inv = ROPE_BASE ** (-jnp.arange(half, dtype=F32) / half)
    ang = pos[:, None] * inv
    cos = jnp.tile(jnp.cos(ang), (1, 2 * N_HEADS))
    sin = jnp.sin(ang)
    sin_signed = jnp.concatenate([jnp.tile(-sin, (1, N_HEADS)), jnp.tile(sin, (1, N_HEADS))], axis=-1)
    return cos, sin_signed


def _layer_weights(P, l):
    row = lambda a: a.reshape(1, -1)
    pad128 = lambda a: jnp.pad(a, (0, 128 - a.shape[0])).reshape(1, 128)
    lb_sm = jax.nn.softmax(P['hg_lb'].astype(F32), axis=0)
    lb = (jnp.cumsum(lb_sm, axis=0) - lb_sm[0])[l]
    return dict(
        w_in=P['w_in_prepped'], w_in_t=P['w_in_t'],
        ln_mix_pre=row(P['ln_mix_pre'][l]), ln_mix_post=row(P['ln_mix_post'][l]),
        ln_xa_pre=row(P['ln_xa_pre'][l]), ln_xa_post=row(P['ln_xa_post'][l]),
        ln_ffn_pre=row(P['ln_ffn_pre'][l]), ln_ffn_post=row(P['ln_ffn_post'][l]),
        ssd_cw=P['ssd_conv_w'][l], ssd_cb=row(P['ssd_conv_b'][l]),
        ssd_dtb=pad128(P['ssd_dt_bias'][l]), ssd_alog=pad128(P['ssd_A_log'][l]),
        ssd_d=row(jnp.repeat(P['ssd_D'][l], HEAD)), ssd_norm=row(P['ssd_norm'][l]),
        ret_norm=row(P['ret_norm'][l]),
        rg_cw=P['rg_conv_w'][l], rg_cb=row(P['rg_conv_b'][l]),
        rg_wa=_block_diag(P['rg_wa'][l]).astype(BF16), rg_ba=row(P['rg_ba'][l]),
        rg_wx=_block_diag(P['rg_wx'][l]).astype(BF16), rg_bx=row(P['rg_bx'][l]),
        rg_lam=row(P['rg_lambda'][l]),
        hg_lb=row(lb), hg_norm=row(P['hg_norm'][l]),
        w_out=P['w_out_bf16'], w_xq=P['w_xq_bf16'], w_xo=P['w_xo_bf16'], w_gu=P['w_gu_bf16'],
        w_down=P['w_down_bf16'], ln_mem=row(P['ln_mem'][l]), w_xkv=P['w_xkv'],
    )


def _layer(x, W, k_mem, v_mem, st, n_seq, seq_len, pos0, layer, prev=None):
    if st is None:
        u_ssd, u_ret, u_rg, u_hg, u_dt = _norm_matmul(
            x, W['ln_mix_pre'], W['w_in'], (IN_COLS_SSD, IN_COLS_RET_SPLIT, IN_COLS_RG, IN_COLS_HG, IN_COLS_DT),
            "in_proj", layer=layer)
        cos, sin_signed = _rope_tables(pos0, seq_len)
        y_ssd, s_ssd, b_ssd = _ssd(u_ssd, u_dt, None, None, W['ssd_cw'], W['ssd_cb'], W['ssd_dtb'],
                                   W['ssd_alog'], W['ssd_d'], W['ssd_norm'], n_seq, seq_len, layer)
        y_ret, s_ret = _ret(u_ret, cos, sin_signed, W['ret_norm'], n_seq, seq_len)
        y_rg, h_rg, b_rg = _rg(u_rg, None, None, W['rg_cw'], W['rg_cb'], W['rg_wa'], W['rg_ba'],
                               W['rg_wx'], W['rg_bx'], W['rg_lam'], n_seq, seq_len)
        h_rg = h_rg.reshape(n_seq, GROUP_W)
        y_hg, s_hg = _hg(u_hg, W['hg_lb'], W['hg_norm'], n_seq, seq_len)
    else:
        ssd_s, ssd_buf, ret_s, rg_h, rg_buf, hg_s = st
        u_ssd, u_rg, u_dt = _norm_matmul(x, W['ln_mix_pre'], W['w_in'], (IN_COLS_SSD, IN_COLS_RG, IN_COLS_DT),
                                         "in_proj", layer=layer)
        to_tb = lambda a: a.reshape(n_seq, seq_len, -1).transpose(1, 0, 2).reshape(n_seq * seq_len, -1)
        from_t = lambda a: a.reshape(-1, seq_len, n_seq).transpose(2, 1, 0).reshape(n_seq * seq_len, -1)
        bcast = lambda g: jnp.broadcast_to(g.reshape(-1, 1), (g.size, n_seq))
        ut_ret, ut_hg = _in_proj_t(to_tb(x), W['ln_mix_pre'], W['w_in_t'], layer)
        cos_t, sin_t = _rope_tables_t(pos0, seq_len, n_seq)
        log_gamma = jnp.log(1.0 - jnp.exp2(-5.0 - jnp.arange(N_HEADS, dtype=F32)))
        gam = jnp.broadcast_to(jnp.exp(log_gamma)[:, None, None], (N_HEADS, 1, n_seq))
        yt_ret, s_ret = _ret_t(ut_ret, cos_t, sin_t, gam, bcast(W['ret_norm']), ret_s, n_seq, seq_len, layer,
                               prev and (prev[1],))
        yt_hg, s_hg = _hg_t(ut_hg, bcast(W['hg_lb']), bcast(W['hg_norm']), hg_s, n_seq, seq_len, layer,
                            prev and (prev[2],))
        y_ret, y_hg = from_t(yt_ret), from_t(yt_hg)
        pad_rows = lambda b: jnp.pad(b, ((0, 0), (0, 1), (0, 0))).reshape(n_seq * 4, b.shape[-1])
        y_ssd, s_ssd, b_ssd = _ssd(u_ssd, u_dt, ssd_s, pad_rows(ssd_buf), W['ssd_cw'], W['ssd_cb'],
                                   W['ssd_dtb'], W['ssd_alog'], W['ssd_d'], W['ssd_norm'], n_seq, seq_len, layer,
                                   prev and (prev[0],))
        b_ssd = b_ssd.reshape(n_seq, 4, -1)[:, :CONV_W - 1]
        y_rg, h_rows, b_rg = _rg(u_rg, jnp.repeat(rg_h, seq_len, axis=0), pad_rows(rg_buf), W['rg_cw'],
                                 W['rg_cb'], W['rg_wa'], W['rg_ba'], W['rg_wx'], W['rg_bx'], W['rg_lam'],
                                 n_seq, seq_len)
        h_rg = h_rows.reshape(n_seq, seq_len, GROUP_W)[:, seq_len - 1]
        b_rg = b_rg.reshape(n_seq, 4, -1)[:, :CONV_W - 1]
    ys = (y_ssd, y_ret, y_rg, y_hg)
    if st is None:
        x3 = _xattn_ffn(ys, x, W['w_out'], W['ln_mix_post'], k_mem, v_mem, W['ln_xa_pre'], W['w_xq'], W['w_xo'],
                        W['ln_xa_post'], W['ln_ffn_pre'], W['w_gu'], W['w_down'], W['ln_ffn_post'], seq_len,
                        layer)
    else:
        x1, q = _out_proj(ys, x, W['w_out'], W['ln_mix_post'], W['ln_xa_pre'], W['w_xq'], layer)
        o = _attention_sample(q, k_mem, v_mem, layer, seq_len)
        x3 = _ffn(o, x1, W['w_xo'], W['ln_xa_post'], W['ln_ffn_pre'], W['w_gu'], W['w_down'],
                  W['ln_ffn_post'], layer)
    return x3, (s_ssd, b_ssd, s_ret, h_rg, b_rg, s_hg)


def kernel(x_prompt, x_sample, state_ssd, state_ssd_conv, state_ret, state_rglru, state_rglru_conv, state_hgrn, cache_mem_k, cache_mem_v, mem_prompt, ln_mix_pre, ln_mix_post, ln_xa_pre, ln_xa_post, ln_ffn_pre, ln_ffn_post, w_in, ssd_conv_w, ssd_conv_b, ssd_dt_bias, ssd_A_log, ssd_D, ssd_norm, ret_norm, rg_conv_w, rg_conv_b, rg_wa, rg_ba, rg_wx, rg_bx, rg_lambda, hg_lb, hg_norm, w_out, ln_mem, w_xq, w_xkv, w_xo, w_gu, w_down):
    P = dict(ln_mix_pre=ln_mix_pre, ln_mix_post=ln_mix_post, ln_xa_pre=ln_xa_pre, ln_xa_post=ln_xa_post,
             ln_ffn_pre=ln_ffn_pre, ln_ffn_post=ln_ffn_post, w_in=w_in, ssd_conv_w=ssd_conv_w,
             ssd_conv_b=ssd_conv_b, ssd_dt_bias=ssd_dt_bias, ssd_A_log=ssd_A_log, ssd_D=ssd_D,
             ssd_norm=ssd_norm, ret_norm=ret_norm, rg_conv_w=rg_conv_w, rg_conv_b=rg_conv_b,
             rg_wa=rg_wa, rg_ba=rg_ba, rg_wx=rg_wx, rg_bx=rg_bx, rg_lambda=rg_lambda, hg_lb=hg_lb,
             hg_norm=hg_norm, w_out=w_out, ln_mem=ln_mem, w_xq=w_xq, w_xkv=w_xkv, w_xo=w_xo,
             w_gu=w_gu, w_down=w_down)
    P['w_in_prepped'], P['w_in_t'] = _in_weight_prep(w_in)
    for name in ('w_out', 'w_xq', 'w_xo', 'w_gu', 'w_down'):
        P[name + '_bf16'] = P[name].astype(BF16)
    ssd_t = jnp.swapaxes(state_ssd, -1, -2)
    ret_t = jnp.transpose(state_ret, (0, 2, 3, 4, 1))
    hgrn_t = jnp.transpose(state_hgrn, (0, 2, 3, 4, 1))
    depth = w_in.shape[0]
    bp, tp, d = x_prompt.shape
    bs, ts, _ = x_sample.shape
    n_mem = mem_prompt.shape[1]
    y_p = x_prompt.reshape(bp * tp, d)
    y_s = x_sample.reshape(bs * ts, d)
    mem = mem_prompt.reshape(bp * n_mem, d)
    p_st, s_st = [], []
    mem_prev = samp_prev = None
    for l in range(depth):
        W = _layer_weights(P, l)
        mk, mv, mk5, mv5 = _mem_kv(mem, W['ln_mem'], W['w_xkv'], l, n_mem, mem_prev)
        mem_prev = (mk5, mv5)
        mk = mk.reshape(bp, n_mem, d)
        mv = mv.reshape(bp, n_mem, d)
        y_p, st_p = _layer(y_p, W, mk, mv, None, bp, tp, 0.0, l)
        p_st.append(st_p)
        st_l = (ssd_t, state_ssd_conv[l], ret_t, state_rglru[l], state_rglru_conv[l], hgrn_t)
        y_s, st_s = _layer(y_s, W, cache_mem_k, cache_mem_v, st_l, bs, ts, float(PAST_LEN), l, samp_prev)
        samp_prev = (st_s[0], st_s[2], st_s[5])
        s_st.append(st_s)
    stack = lambda sts, i: jnp.stack([s[i] for s in sts], axis=0)
    return (y_p.reshape(bp, tp, d), y_s.reshape(bs, ts, d),
            jnp.swapaxes(stack(p_st, 0), -1, -2), stack(p_st, 1), stack(p_st, 2), stack(p_st, 3), stack(p_st, 4),
            stack(p_st, 5),
            mem_prev[0], mem_prev[1],
            jnp.swapaxes(samp_prev[0], -1, -2), stack(s_st, 1), jnp.transpose(samp_prev[1], (0, 4, 1, 2, 3)),
            stack(s_st, 3), stack(s_st, 4), jnp.transpose(samp_prev[2], (0, 4, 1, 2, 3)))
```

```python
import functools
import math

import jax
import jax.numpy as jnp
from jax import lax
from jax.experimental import pallas as pl
from jax.experimental.pallas import tpu as pltpu

F32 = jnp.float32
BF16 = jnp.bfloat16
EPS = 1e-6
LOG2_E = 1.4426950408889634

GROUP_W = 256
HEAD = 64
N_HEADS = 4
SSD_N = 128
SSD_G = 2
CONV_W = 4
XA_H = 4
XA_HD = 256
ROPE_BASE = 10000.0
RG_C = 8.0
CHUNK = 64
PAST_LEN = 16384

ROW_TILE = 512
XA_TILE = 1024
IN_TILE = 1024
MIX_TILE = 2048
RG_TILE = 2048
SAMPLE_SEQS = 16
ATT_SEQS = 8
ATT_BUFS = 3
FF_TILE = 256
VMEM_LIMIT = 56 * 1024 * 1024
XA_VMEM_LIMIT = 62 * 1024 * 1024


def _bdot(a, b):
    return jnp.dot(a.astype(BF16), b.astype(BF16), preferred_element_type=F32)


def _bdot_nt(a, b):
    return lax.dot_general(a.astype(BF16), b.astype(BF16), (((1,), (1,)), ((), ())),
                           preferred_element_type=F32)


def _bdot_tn(a, b):
    return lax.dot_general(a.astype(BF16), b.astype(BF16), (((0,), (0,)), ((), ())),
                           preferred_element_type=F32)


def _split3(x):
    hi = x.astype(BF16)
    r = x - hi.astype(F32)
    mid = r.astype(BF16)
    lo = (r - mid.astype(F32)).astype(BF16)
    return hi, mid, lo


def _sel_dot(sel, x):
    hi, mid, lo = _split3(x)
    d = lambda y: jnp.dot(sel, y, preferred_element_type=F32)
    return (d(hi) + d(mid)) + d(lo)


def _dot_sel(x, sel):
    hi, mid, lo = _split3(x)
    d = lambda y: jnp.dot(y, sel, preferred_element_type=F32)
    return (d(hi) + d(mid)) + d(lo)


def _sel_dot_nt(sel, x):
    hi, mid, lo = _split3(x)
    d = lambda y: lax.dot_general(sel, y, (((1,), (1,)), ((), ())), preferred_element_type=F32)
    return (d(hi) + d(mid)) + d(lo)


def _rms(x, g):
    return x * lax.rsqrt(jnp.mean(x * x, axis=-1, keepdims=True) + EPS) * g


def _sigmoid(x):
    return jax.nn.sigmoid(x)


def _silu(x):
    return x * jax.nn.sigmoid(x)


def _softplus(x):
    return jnp.maximum(x, 0.0) + jnp.log1p(jnp.exp(-jnp.abs(x)))


def _gelu_tanh(x):
    c = math.sqrt(2.0 / math.pi)
    return 0.5 * x * (1.0 + jnp.tanh(c * (x + 0.044715 * (x * x * x))))


def _iota(shape, dim):
    return lax.broadcasted_iota(jnp.int32, shape, dim)


def _seg_causal_mask(n, seg_shift):
    r = _iota((n, n), 0)
    c = _iota((n, n), 1)
    return ((r >> seg_shift) == (c >> seg_shift)) & (c <= r)


def _seg_cumsum(x, seg_shift):
    rows = x.shape[0]
    blk = min(rows, 256)
    mask = _seg_causal_mask(blk, seg_shift).astype(BF16)
    return jnp.concatenate([_sel_dot(mask, x[i:i + blk]) for i in range(0, rows, blk)], axis=0)


def _head_expand_mat(width_in=128):
    r = _iota((width_in, GROUP_W), 0)
    c = _iota((width_in, GROUP_W), 1)
    return ((c >> 6) == r).astype(BF16)


def _head_block_mask():
    return (_iota((GROUP_W, GROUP_W), 0) >> 6) == (_iota((GROUP_W, GROUP_W), 1) >> 6)


def _block_diag_rows(x, mask01):
    return jnp.concatenate([x.astype(BF16)] * N_HEADS, axis=0) * mask01


def _head_stat(y, center):
    ones_bd = _head_block_mask().astype(BF16)
    if center:
        y = y - _dot_sel(y, ones_bd) * (1.0 / HEAD)
    return y * lax.rsqrt(_dot_sel(y * y, ones_bd) * (1.0 / HEAD) + EPS)


def _conv_taps_carry(x, xx_ref, w_ref, b_ref, first):
    rows = x.shape[0]

    @pl.when(first)
    def _():
        xx_ref[0:8, :] = jnp.zeros((8, x.shape[1]), F32)

    xx_ref[8:8 + rows, :] = x
    y = b_ref[...] + w_ref[CONV_W - 1:CONV_W, :] * x
    for j in range(CONV_W - 1):
        y = y + w_ref[j:j + 1, :] * xx_ref[pl.ds(8 - (CONV_W - 1) + j, rows), :]
    xx_ref[0:8, :] = xx_ref[rows:rows + 8, :]
    return y


def _conv_taps_seq4(x, buf4, w_ref, b_ref):
    rows = x.shape[0]
    r = _iota((rows, rows), 0)
    c = _iota((rows, rows), 1)
    t = r & 3
    y = b_ref[...] + w_ref[CONV_W - 1:CONV_W, :] * x
    for d in range(1, CONV_W):
        shift = ((c == r - d) & (t >= d)).astype(BF16)
        hist = ((t < d) & (c == (r - t) + (3 + t - d))).astype(BF16)
        y = y + w_ref[CONV_W - 1 - d:CONV_W - d, :] * (_sel_dot(shift, x) + _sel_dot(hist, buf4))
    new_sel = (((r & 3) < 3) & (c == r + 1)).astype(BF16)
    return y, _sel_dot(new_sel, x)


def _norm_matmul_kernel(x_ref, g_ref, w_ref, *o_refs, col_ranges):
    h = _rms(x_ref[...], g_ref[...]).astype(BF16)
    for o_ref, pieces in zip(o_refs, col_ranges):
        off = 0
        for a, b in pieces:
            o_ref[:, off:off + b - a] = jnp.dot(h, w_ref[:, a:b], preferred_element_type=F32)
            off += b - a


def _const_spec(shape):
    nd = len(shape)
    return pl.BlockSpec(shape, lambda *_: (0,) * nd, pipeline_mode=pl.Buffered(1))


def _layer_spec(shape, layer):
    nd = len(shape)
    return pl.BlockSpec((None,) + tuple(shape[1:]), lambda *_: (layer,) + (0,) * (nd - 1),
                        pipeline_mode=pl.Buffered(1))


def _stack_call(kernel, prevs, stacked, layer, *, in_specs, args, out_specs, out_shape, **kw):
    n_in = len(in_specs)
    out_specs = list(out_specs)
    order = sorted(stacked)
    for o in order:
        blk, idx = stacked[o]
        if prevs is None:
            depth = out_shape[o].shape[0]
            out_specs[o] = pl.BlockSpec((depth,) + tuple(blk), lambda *g, idx=idx: (0,) + tuple(idx(*g)))
        else:
            out_specs[o] = pl.BlockSpec((None,) + tuple(blk), lambda *g, idx=idx: (layer,) + tuple(idx(*g)))
    if prevs is None:
        def body(*refs):
            refs = list(refs)
            for o in order:
                full = refs[n_in + o]
                for d in range(full.shape[0]):
                    if d != layer:
                        full[d] = jnp.zeros(full.shape[1:], full.dtype)
                refs[n_in + o] = full.at[layer]
            return kernel(*refs)

        return pl.pallas_call(body, in_specs=in_specs, out_specs=out_specs, out_shape=out_shape, **kw)(*args)

    def body(*refs):
        return kernel(*refs[:n_in], *refs[n_in + len(prevs):])

    return pl.pallas_call(
        body,
        in_specs=list(in_specs) + [pl.BlockSpec(memory_space=pl.ANY)] * len(prevs),
        out_specs=out_specs, out_shape=out_shape,
        input_output_aliases={n_in + j: o for j, o in enumerate(order)},
        **kw)(*args, *prevs)


def _params(sem, vmem=VMEM_LIMIT):
    return pltpu.CompilerParams(dimension_semantics=sem, vmem_limit_bytes=vmem)


def _in_weight_prep_kernel(wt_ref, o_ref, ot_ref, *, blocks, dt_block, dt_row, dt_n, t_rows):
    off = 0
    for src, n in t_rows:
        ot_ref[off:off + n, :] = wt_ref[pl.ds(src, n), :].astype(BF16)
        off += n
    for j, pieces in enumerate(blocks):
        if j == dt_block:
            dt = wt_ref[pl.ds(dt_row, 8), :]
            dt = jnp.where(_iota(dt.shape, 0) < dt_n, dt, 0.0)
            blk = jnp.concatenate([dt, jnp.zeros((120, dt.shape[1]), F32)], axis=0)
        else:
            blk = jnp.concatenate([wt_ref[pl.ds(src, n), :] for src, n in pieces], axis=0)
        o_ref[:, j * 128:(j + 1) * 128] = blk.T.astype(BF16)


IN_COLS_SSD = ((0, 1024),)
IN_COLS_RG = ((2048, 2560),)
IN_COLS_HG = ((2560, 3584),)
IN_COLS_DT = ((3584, 3712),)
IN_COLS_RET_SPLIT = ((3712, 4224), (1536, 2048))


def _in_weight_prep(w_in):
    depth, d, d_in = w_in.shape
    z0 = GROUP_W + (GROUP_W + 2 * SSD_G * SSD_N)
    d0 = z0 + N_HEADS
    blocks = [((src, 128),) for src in tuple(range(0, z0, 128)) + tuple(range(d0, d_in, 128))]
    dt_block = len(blocks)
    blocks.append(())
    half = HEAD // 2
    for base in (d0, d0 + GROUP_W):
        for j in range(2):
            blocks.append(tuple((base + h * HEAD + j * half, half) for h in range(N_HEADS)))
    n_out = len(blocks) * 128
    t_rows = ((d0, 4 * GROUP_W), (d0 + 6 * GROUP_W, 4 * GROUP_W))
    n_t = sum(n for _, n in t_rows)
    wt = jnp.swapaxes(w_in, 1, 2)
    return pl.pallas_call(
        functools.partial(_in_weight_prep_kernel, blocks=tuple(blocks), dt_block=dt_block, dt_row=z0,
                          dt_n=N_HEADS, t_rows=t_rows),
        grid=(depth,),
        in_specs=[pl.BlockSpec((None, d_in, d), lambda l: (l, 0, 0), pipeline_mode=pl.Buffered(1))],
        out_specs=[pl.BlockSpec((None, d, n_out), lambda l: (l, 0, 0)),
                   pl.BlockSpec((None, n_t, d), lambda l: (l, 0, 0))],
        out_shape=[jax.ShapeDtypeStruct((depth, d, n_out), BF16), jax.ShapeDtypeStruct((depth, n_t, d), BF16)],
        compiler_params=_params(("parallel",)),
        name="in_weight_prep",
    )(wt)


def _in_proj_t_kernel(x_ref, g_ref, wt_ref, *o_refs):
    h = _rms(x_ref[...], g_ref[...]).astype(BF16)
    off = 0
    for o_ref in o_refs:
        f = o_ref.shape[0]
        o_ref[...] = lax.dot_general(wt_ref[off:off + f, :], h, (((1,), (1,)), ((), ())),
                                     preferred_element_type=F32)
        off += f


def _in_proj_t(x, g, wt, layer):
    n, d = x.shape
    f = wt.shape[1] // 2
    return pl.pallas_call(
        _in_proj_t_kernel,
        grid=(1,),
        in_specs=[_const_spec((n, d)), _const_spec((1, d)), _layer_spec(wt.shape, layer)],
        out_specs=[_const_spec((f, n))] * 2,
        out_shape=[jax.ShapeDtypeStruct((f, n), F32)] * 2,
        compiler_params=_params(("arbitrary",)),
        name="in_proj_t",
    )(x, g, wt)


def _mem_kv_kernel(x_ref, g_ref, w_ref, k_ref, v_ref, k5_ref, v5_ref):
    h = _rms(x_ref[...], g_ref[...]).astype(BF16)
    d = x_ref.shape[1]
    k = jnp.dot(h, w_ref[:, 0:d].astype(BF16), preferred_element_type=F32)
    v = jnp.dot(h, w_ref[:, d:2 * d].astype(BF16), preferred_element_type=F32)
    k_ref[...] = k
    v_ref[...] = v
    k5_ref[...] = k.reshape(k5_ref.shape)
    v5_ref[...] = v.reshape(v5_ref.shape)


def _mem_kv(mem, g, w, layer, n_mem, prev):
    n, d = mem.shape
    depth = w.shape[0]
    tm = min(ROW_TILE, n)
    seqs = tm // n_mem
    row = pl.BlockSpec((tm, d), lambda i: (i, 0))
    blk5 = ((seqs, n_mem, XA_H, XA_HD), lambda i: (i, 0, 0, 0))
    shape5 = jax.ShapeDtypeStruct((depth, n // n_mem, n_mem, XA_H, XA_HD), F32)
    return _stack_call(
        _mem_kv_kernel, prev, {2: blk5, 3: blk5}, layer,
        in_specs=[row, _const_spec((1, d)), _wspec(w, layer)],
        args=(mem, g, w),
        grid=(n // tm,),
        out_specs=[row, row, None, None],
        out_shape=[jax.ShapeDtypeStruct((n, d), F32)] * 2 + [shape5] * 2,
        compiler_params=_params(("parallel",)),
        name="mem_kv",
    )


def _wspec(w, layer):
    return _layer_spec(w.shape, layer) if w.ndim == 3 else _const_spec(w.shape)


def _norm_matmul(x, g, w, col_ranges, name, layer=None):
    n, d = x.shape
    tm = min(IN_TILE, n)
    w_spec = _wspec(w, layer)
    widths = [sum(b - a for a, b in pieces) for pieces in col_ranges]
    return pl.pallas_call(
        functools.partial(_norm_matmul_kernel, col_ranges=col_ranges),
        grid=(n // tm,),
        in_specs=[pl.BlockSpec((tm, d), lambda i: (i, 0)), _const_spec((1, d)), w_spec],
        out_specs=[pl.BlockSpec((tm, wd), lambda i: (i, 0)) for wd in widths],
        out_shape=[jax.ShapeDtypeStruct((n, wd), F32) for wd in widths],
        compiler_params=_params(("parallel",)),
        name=name,
    )(x, g, w)


def _out_proj_kernel(ys_ref, yr_ref, yg_ref, yh_ref, x_ref, wo_ref, gpost_ref, gpre_ref, wq_ref, x1_ref, q_ref):
    x1 = _mix_out(ys_ref, yr_ref, yg_ref, yh_ref, x_ref, wo_ref, gpost_ref)
    x1_ref[...] = x1
    q_ref[...] = _bdot(_rms(x1, gpre_ref[...]), wq_ref[...])


def _out_proj(ys, x, w_out, g_post, g_pre, w_xq, layer):
    n, d = x.shape
    tm = min(ROW_TILE, n)
    row = lambda wd: pl.BlockSpec((tm, wd), lambda i: (i, 0))
    vec = _const_spec((1, d))
    return pl.pallas_call(
        _out_proj_kernel,
        grid=(n // tm,),
        in_specs=[row(GROUP_W)] * 4 + [row(d), _wspec(w_out, layer), vec, vec, _wspec(w_xq, layer)],
        out_specs=[row(d)] * 2,
        out_shape=[jax.ShapeDtypeStruct((n, d), F32)] * 2,
        compiler_params=_params(("parallel",)),
        name="out_proj",
    )(*ys, x, w_out, g_post, g_pre, w_xq)


def _ffn_tail(o, x1, wxo_ref, gxa_ref, gpre_ref, wgu_ref, wdn_ref, gpost_ref, d_ff):
    a = _bdot(o, wxo_ref[...])
    x2 = x1 + _rms(a, gxa_ref[...])
    h = _rms(x2, gpre_ref[...]).astype(BF16)
    acc = jnp.zeros(x2.shape, F32)
    for j in range(0, d_ff, FF_TILE):
        g = jnp.dot(h, wgu_ref[:, j:j + FF_TILE], preferred_element_type=F32)
        u = jnp.dot(h, wgu_ref[:, d_ff + j:d_ff + j + FF_TILE], preferred_element_type=F32)
        act = (_silu(g) * u).astype(BF16)
        acc = acc + jnp.dot(act, wdn_ref[j:j + FF_TILE, :], preferred_element_type=F32)
    return x2 + _rms(acc, gpost_ref[...])


def _ffn_kernel(o_ref, x1_ref, wxo_ref, gxa_ref, gpre_ref, wgu_ref, wdn_ref, gpost_ref, x3_ref, *, d_ff):
    x3_ref[...] = _ffn_tail(o_ref[...], x1_ref[...], wxo_ref, gxa_ref, gpre_ref, wgu_ref, wdn_ref, gpost_ref,
                            d_ff)


def _ffn(o, x1, w_xo, g_xa, g_pre, w_gu, w_down, g_post, layer):
    n, d = x1.shape
    d_ff = w_down.shape[-2]
    tm = min(ROW_TILE, n)
    row = pl.BlockSpec((tm, d), lambda i: (i, 0))
    vec = _const_spec((1, d))
    return pl.pallas_call(
        functools.partial(_ffn_kernel, d_ff=d_ff),
        grid=(n // tm,),
        in_specs=[row, row, _wspec(w_xo, layer), vec, vec, _wspec(w_gu, layer), _wspec(w_down, layer), vec],
        out_specs=row,
        out_shape=jax.ShapeDtypeStruct((n, d), F32),
        compiler_params=_params(("parallel",)),
        name="attn_out_ffn",
    )(o, x1, w_xo, g_xa, g_pre, w_gu, w_down, g_post)


def _mix_out(ys_ref, yr_ref, yg_ref, yh_ref, x_ref, wo_ref, gpost_ref):
    mix = _bdot(ys_ref[...], wo_ref[0:GROUP_W, :])
    mix = mix + _bdot(yr_ref[...], wo_ref[GROUP_W:2 * GROUP_W, :])
    mix = mix + _bdot(yg_ref[...], wo_ref[2 * GROUP_W:3 * GROUP_W, :])
    mix = mix + _bdot(yh_ref[...], wo_ref[3 * GROUP_W:4 * GROUP_W, :])
    return x_ref[...] + _rms(mix, gpost_ref[...])


def _xattn_ffn_kernel(ys_ref, yr_ref, yg_ref, yh_ref, x_ref, wo_ref, gmix_ref, k_ref, v_ref, gq_ref, wq_ref,
                      wxo_ref, gxa_ref, gpre_ref, wgu_ref, wdn_ref, gpost_ref, x3_ref, o_scr, *, d_ff):
    x1 = _mix_out(ys_ref, yr_ref, yg_ref, yh_ref, x_ref, wo_ref, gmix_ref)
    q = _bdot(_rms(x1, gq_ref[...]), wq_ref[...])
    scale = XA_HD ** -0.5
    for h in range(XA_H):
        sl = slice(h * XA_HD, (h + 1) * XA_HD)
        s = _bdot_nt(q[:, sl], k_ref[0, :, sl]) * scale
        o_scr[:, sl] = _softmax_pv(s, v_ref[0, :, sl]).astype(BF16)
    x3_ref[...] = _ffn_tail(o_scr[...], x1, wxo_ref, gxa_ref, gpre_ref, wgu_ref, wdn_ref, gpost_ref, d_ff)


def _xattn_ffn(ys, x, w_out, g_mix, k, v, g_q, w_xq, w_xo, g_xa, g_pre, w_gu, w_down, g_post, seq_len, layer):
    n, d = x.shape
    n_seq, n_mem, _ = k.shape
    d_ff = w_down.shape[-2]
    nt = seq_len // XA_TILE
    row = lambda wd: pl.BlockSpec((XA_TILE, wd), lambda b, t: (b * nt + t, 0))
    kv = pl.BlockSpec((1, n_mem, d), lambda b, t: (b, 0, 0))
    vec = _const_spec((1, d))
    return pl.pallas_call(
        functools.partial(_xattn_ffn_kernel, d_ff=d_ff),
        grid=(n_seq, nt),
        in_specs=[row(GROUP_W)] * 4 + [row(d), _wspec(w_out, layer), vec, kv, kv, vec, _wspec(w_xq, layer),
                                       _wspec(w_xo, layer), vec, vec, _wspec(w_gu, layer),
                                       _wspec(w_down, layer), vec],
        out_specs=row(d),
        out_shape=jax.ShapeDtypeStruct((n, d), F32),
        scratch_shapes=[pltpu.VMEM((XA_TILE, d), BF16)],
        compiler_params=_params(("parallel", "parallel"), XA_VMEM_LIMIT),
        name="xattn_ffn",
    )(*ys, x, w_out, g_mix, k, v, g_q, w_xq, w_xo, g_xa, g_pre, w_gu, w_down, g_post)


def _softmax_pv(s, v):
    m = jnp.max(s, axis=-1, keepdims=True)
    p = jnp.exp(s - m)
    return _bdot(p, v) / jnp.sum(p, axis=-1, keepdims=True)


def _attn_sample_kernel(q_ref, k_hbm, v_hbm, o_ref, kbuf, vbuf, sem, *, seqs, seg, layer):
    step = pl.program_id(0)
    n_steps = pl.num_programs(0)

    def copies(blk, slot):
        src = pl.ds(blk * seqs, seqs)
        return (pltpu.make_async_copy(k_hbm.at[layer, src], kbuf.at[slot], sem.at[0, slot]),
                pltpu.make_async_copy(v_hbm.at[layer, src], vbuf.at[slot], sem.at[1, slot]))

    @pl.when(step == 0)
    def _():
        for j in range(ATT_BUFS - 1):
            for c in copies(j, j):
                c.start()

    ahead = step + (ATT_BUFS - 1)

    @pl.when(ahead < n_steps)
    def _():
        for c in copies(ahead, ahead % ATT_BUFS):
            c.start()

    slot = step % ATT_BUFS
    for c in copies(step, slot):
        c.wait()
    k_ref = kbuf.at[slot]
    v_ref = vbuf.at[slot]
    scale = XA_HD ** -0.5
    rows = seqs * seg
    n_mem = kbuf.shape[2]
    wq = jnp.concatenate([q_ref[:, h * XA_HD:(h + 1) * XA_HD] for h in range(XA_H)], axis=0)
    wshape = (XA_H * rows, n_mem * XA_H)
    valid = (_iota(wshape, 0) >> (rows.bit_length() - 1)) == (_iota(wshape, 1) & (XA_H - 1))
    row_seq = _iota((rows, XA_HD), 0) >> (seg.bit_length() - 1)
    accs = [jnp.zeros((rows, XA_HD), F32) for _ in range(XA_H)]
    for i in range(seqs):
        k2 = k_ref[i].reshape(n_mem * XA_H, XA_HD)
        v2 = v_ref[i].reshape(n_mem * XA_H, XA_HD)
        s = jnp.where(valid, _bdot_nt(wq, k2) * scale, -jnp.inf)
        o = _softmax_pv(s, v2)
        for h in range(XA_H):
            accs[h] = jnp.where(row_seq == i, o[h * rows:(h + 1) * rows], accs[h])
    for h in range(XA_H):
        o_ref[:, h * XA_HD:(h + 1) * XA_HD] = accs[h]


def _attention_sample(q, cache_k, cache_v, layer, seq_len):
    n, d = q.shape
    _, n_seq, n_mem, heads, hd = cache_k.shape
    assert n_seq // ATT_SEQS >= ATT_BUFS - 1
    kv = pl.BlockSpec(memory_space=pl.ANY)
    row = pl.BlockSpec((ATT_SEQS * seq_len, d), lambda i: (i, 0))
    buf = pltpu.VMEM((ATT_BUFS, ATT_SEQS, n_mem, heads, hd), F32)
    return pl.pallas_call(
        functools.partial(_attn_sample_kernel, seqs=ATT_SEQS, seg=seq_len, layer=layer),
        grid=(n_seq // ATT_SEQS,),
        in_specs=[row, kv, kv],
        out_specs=row,
        out_shape=jax.ShapeDtypeStruct((n, d), F32),
        scratch_shapes=[buf, buf, pltpu.SemaphoreType.DMA((2, ATT_BUFS))],
        compiler_params=_params(("arbitrary",)),
        name="xattn_sample",
    )(q, cache_k, cache_v)


def _ssd_intra(q, k, v, cum_x, cum_t, mask):
    sc = _bdot_nt(q, k)
    outs = []
    for hh in range(2):
        col = cum_x[:, hh * HEAD:(hh + 1) * HEAD]
        dec = jnp.exp2(jnp.where(mask, col - cum_t[hh:hh + 1, :], -jnp.inf))
        outs.append(_bdot(sc * dec, v[:, hh * HEAD:(hh + 1) * HEAD]))
    return jnp.concatenate(outs, axis=-1)


def _ssd_common(u, dtraw_ref, dtb_ref, alog_ref, conv, seg_shift):
    rows = u.shape[0]
    xbc = _silu(conv)
    xs = xbc[:, 0:GROUP_W]
    bm = xbc[:, GROUP_W:2 * GROUP_W]
    cm = xbc[:, 2 * GROUP_W:3 * GROUP_W]
    dt = _softplus(dtraw_ref[...] + dtb_ref[...])
    la = (-jnp.exp(alog_ref[...]) * LOG2_E) * dt
    cum = _seg_cumsum(la, seg_shift)
    expand = _head_expand_mat()
    dt_x = _dot_sel(dt, expand)
    cum_x = _dot_sel(cum, expand)
    sel8 = (_iota((8, 128), 0) == _iota((8, 128), 1)).astype(BF16)
    cum_t = _sel_dot_nt(sel8, cum)
    expand_full = ((_iota((128, 4 * 128), 1) >> 7) == _iota((128, 4 * 128), 0)).astype(BF16)
    cum_full = _dot_sel(cum, expand_full)
    return xs, bm, cm, dt_x, cum_x, cum_t, cum_full


def _ssd_finish(y, xs, z, d_ref, g_ref):
    y = y + d_ref[...] * xs
    return _rms(y * _silu(z), g_ref[...])


def _ssd_prompt_kernel(u_ref, dtraw_ref, cw_ref, cb_ref, dtb_ref, alog_ref, d_ref, g_ref,
                       y_ref, s_out_ref, buf_out_ref, s_ref, xx_ref, y_scr):
    t = pl.program_id(1)
    rows = u_ref.shape[0]

    @pl.when(t == 0)
    def _():
        s_ref[...] = jnp.zeros(s_ref.shape, F32)

    u = u_ref[...]
    conv = _conv_taps_carry(u[:, GROUP_W:], xx_ref, cw_ref, cb_ref, t == 0)
    xs, bm, cm, dt_x, cum_x, _, cum_full = _ssd_common(u, dtraw_ref, dtb_ref, alog_ref, conv, 6)
    v = xs * dt_x
    ecum_x = jnp.exp2(cum_x)
    hmask01 = _head_block_mask().astype(BF16)
    gmask01 = ((_iota((GROUP_W, GROUP_W), 0) >> 7) == (_iota((GROUP_W, GROUP_W), 1) >> 7)).astype(BF16)
    wrow = _iota((CHUNK, GROUP_W), 0)
    wcol = _iota((CHUNK, GROUP_W), 1) & (CHUNK - 1)
    n_chunks = rows // CHUNK
    groups = [slice(g * 128, (g + 1) * 128) for g in range(SSD_G)]
    a_wide, ds_t = [], []
    for c in range(n_chunks):
        rs = slice(c * CHUNK, (c + 1) * CHUNK)
        cum_c = cum_x[rs]
        last_x = cum_x[c * CHUNK + CHUNK - 1:(c + 1) * CHUNK, :]
        vend = v[rs] * jnp.exp2(last_x - cum_c)
        cum_row = jnp.sum(jnp.where(wrow == wcol, cum_c, 0.0), axis=0, keepdims=True)
        dec = jnp.exp2(jnp.where(wcol <= wrow, cum_c - cum_row, -jnp.inf))
        a_wide.append(_bdot_nt(cm[rs], _block_diag_rows(bm[rs], gmask01)) * dec)
        ds_t.append([_bdot_tn(vend[:, ls], bm[rs, ls]) for ls in groups])
    s_in = []
    s_cur = [s_ref[ls, :] for ls in groups]
    for c in range(n_chunks):
        last_full = cum_full[c * CHUNK + CHUNK - 1:(c + 1) * CHUNK, :]
        s_in.append([x.astype(BF16) for x in s_cur])
        for g in range(SSD_G):
            decay = jnp.concatenate(
                [jnp.broadcast_to(jnp.exp2(last_full[:, h * 128:(h + 1) * 128]), (HEAD, 128))
                 for h in (2 * g, 2 * g + 1)], axis=0)
            s_cur[g] = decay * s_cur[g] + ds_t[c][g]
    for g, ls in enumerate(groups):
        s_ref[ls, :] = s_cur[g]
    for c in range(n_chunks):
        rs = slice(c * CHUNK, (c + 1) * CHUNK)
        y_intra = _bdot(a_wide[c], _block_diag_rows(v[rs], hmask01))
        y_inter = [_bdot_nt(cm[rs, ls], s_in[c][g]) for g, ls in enumerate(groups)]
        y_scr[rs, :] = y_intra + jnp.concatenate(y_inter, axis=-1) * ecum_x[rs]
    y_ref[...] = _ssd_finish(y_scr[...], xs, u[:, 0:GROUP_W], d_ref, g_ref)

    @pl.when(t == pl.num_programs(1) - 1)
    def _():
        for h in range(N_HEADS):
            s_out_ref[0, h] = s_ref[h * HEAD:(h + 1) * HEAD, :]
        buf_out_ref[0] = xx_ref[pl.ds(8 - (CONV_W - 1), CONV_W - 1), :]


def _ssd_sample_kernel(u_ref, dtraw_ref, s0_ref, buf_ref, cw_ref, cb_ref, dtb_ref, alog_ref, d_ref,
                       g_ref, y_ref, s_out_ref, buf_out_ref, *, seqs):
    rows = u_ref.shape[0]
    u = u_ref[...]
    conv, new_buf = _conv_taps_seq4(u[:, GROUP_W:], buf_ref[...], cw_ref, cb_ref)
    buf_out_ref[...] = new_buf
    mask = _seg_causal_mask(rows, 2)
    xs, bm, cm, dt_x, cum_x, cum_t, cum_full = _ssd_common(u, dtraw_ref, dtb_ref, alog_ref, conv, 2)
    v = xs * dt_x
    ecum_x = jnp.exp2(cum_x)
    pick_last = (_iota((rows, rows), 1) == (_iota((rows, rows), 0) | 3)).astype(BF16)
    last_x = _sel_dot(pick_last, cum_x)
    vend = v * jnp.exp2(last_x - cum_x)
    row_seq = _iota((rows, 128), 0) >> 2
    ys = []
    for g in range(SSD_G):
        ls = slice(g * 128, (g + 1) * 128)
        q = cm[:, ls]
        k = bm[:, ls]
        y_g = _ssd_intra(q, k, v[:, ls], cum_x[:, ls], cum_t[2 * g:2 * g + 2, :], mask)
        y_int = jnp.zeros((rows, 128), F32)
        for i in range(seqs):
            in_seq = row_seq == i
            s_prev = jnp.concatenate([s0_ref[i, 2 * g], s0_ref[i, 2 * g + 1]], axis=0)
            y_int = jnp.where(in_seq, _bdot_nt(q, s_prev), y_int)
            ds = _bdot_tn(jnp.where(in_seq, vend[:, ls], 0.0), k)
            for hh in range(2):
                h = 2 * g + hh
                decay = jnp.exp2(cum_full[4 * i + 3:4 * i + 4, h * 128:(h + 1) * 128])
                s_out_ref[i, h] = decay * s0_ref[i, h] + ds[hh * HEAD:(hh + 1) * HEAD, :]
        ys.append(y_g + y_int * ecum_x[:, ls])
    y_ref[...] = _ssd_finish(jnp.concatenate(ys, axis=-1), xs, u[:, 0:GROUP_W], d_ref, g_ref)


def _ssd(u, dtraw, state, buf, cw, cb, dtb, alog, d_x, gain, n_seq, seq_len, layer, prev=None):
    n = u.shape[0]
    cch = cw.shape[1]
    vec = lambda wd: _const_spec((1, wd))
    common_in = [_const_spec(cw.shape), vec(cch), vec(128), vec(128), vec(GROUP_W), vec(GROUP_W)]
    if state is None:
        nt = seq_len // MIX_TILE
        row = lambda wd: pl.BlockSpec((MIX_TILE, wd), lambda b, t: (b * nt + t, 0))
        return pl.pallas_call(
            _ssd_prompt_kernel,
            grid=(n_seq, nt),
            in_specs=[row(u.shape[1]), row(128)] + common_in,
            out_specs=[row(GROUP_W),
                       pl.BlockSpec((1, N_HEADS, HEAD, SSD_N), lambda b, t: (b, 0, 0, 0)),
                       pl.BlockSpec((1, CONV_W - 1, cch), lambda b, t: (b, 0, 0))],
            out_shape=[jax.ShapeDtypeStruct((n, GROUP_W), F32),
                       jax.ShapeDtypeStruct((n_seq, N_HEADS, HEAD, SSD_N), F32),
                       jax.ShapeDtypeStruct((n_seq, CONV_W - 1, cch), F32)],
            scratch_shapes=[pltpu.VMEM((GROUP_W, SSD_N), F32), pltpu.VMEM((MIX_TILE + 8, cch), F32),
                            pltpu.VMEM((MIX_TILE, GROUP_W), F32)],
            compiler_params=_params(("parallel", "arbitrary")),
            name="ssd_prompt",
        )(u, dtraw, cw, cb, dtb, alog, d_x, gain)
    rows = SAMPLE_SEQS * seq_len
    row = lambda wd: pl.BlockSpec((rows, wd), lambda i: (i, 0))
    st = pl.BlockSpec((None, SAMPLE_SEQS, N_HEADS, HEAD, SSD_N), lambda i: (layer, i, 0, 0, 0))
    return _stack_call(
        functools.partial(_ssd_sample_kernel, seqs=SAMPLE_SEQS), prev,
        {1: ((SAMPLE_SEQS, N_HEADS, HEAD, SSD_N), lambda i: (i, 0, 0, 0))}, layer,
        in_specs=[row(u.shape[1]), row(128), st, row(cch)] + common_in,
        args=(u, dtraw, state, buf, cw, cb, dtb, alog, d_x, gain),
        grid=(n_seq // SAMPLE_SEQS,),
        out_specs=[row(GROUP_W), None, row(cch)],
        out_shape=[jax.ShapeDtypeStruct((n, GROUP_W), F32),
                   jax.ShapeDtypeStruct(state.shape, F32),
                   jax.ShapeDtypeStruct((n, cch), F32)],
        compiler_params=_params(("parallel",)),
        name="ssd_sample",
    )


def _rope_split(x, cos, sin_signed):
    swapped = jnp.concatenate([x[:, 128:], x[:, :128]], axis=1)
    return x * cos + swapped * sin_signed


def _ret_qkv(u_ref, cos_ref, sin_ref):
    u = u_ref[...]
    q = _rope_split(u[:, 0:GROUP_W], cos_ref[...], sin_ref[...])
    k = _rope_split(u[:, GROUP_W:2 * GROUP_W], cos_ref[...], sin_ref[...]) * (HEAD ** -0.5)
    return q, k, u[:, 2 * GROUP_W:3 * GROUP_W], u[:, 3 * GROUP_W:4 * GROUP_W]


def _ret_prompt_kernel(u_ref, cos_ref, sin_ref, ecum_ref, eend_ref, elast_ref, dec_ref, g_ref,
                       y_ref, s_out_ref, s_ref, y_scr):
    t = pl.program_id(1)
    rows = u_ref.shape[0]

    @pl.when(t == 0)
    def _():
        s_ref[...] = jnp.zeros(s_ref.shape, F32)

    q, k, v, gate = _ret_qkv(u_ref, cos_ref, sin_ref)
    r_i = _iota((GROUP_W, GROUP_W), 0)
    c_i = _iota((GROUP_W, GROUP_W), 1)
    kmask01 = ((r_i >> 6) == ((c_i & 127) >> 5)).astype(BF16)
    vmask01 = _head_block_mask().astype(BF16)
    smask = ((r_i & 127) >> 5) == (c_i >> 6)
    chunks = [slice(c * CHUNK, (c + 1) * CHUNK) for c in range(rows // CHUNK)]
    a_wide = [_bdot_nt(q[rs], _block_diag_rows(k[rs], kmask01)) * dec_ref[...] for rs in chunks]
    ds = [jnp.where(smask, _bdot_tn(k[rs], v[rs] * eend_ref[...]), 0.0) for rs in chunks]
    s = s_ref[...]
    s_in = []
    for d in ds:
        s_in.append(s.astype(BF16))
        s = elast_ref[...] * s + d
    s_ref[...] = s
    for rs, a, s_c in zip(chunks, a_wide, s_in):
        y_intra = _bdot(a, _block_diag_rows(v[rs], vmask01))
        y_scr[rs, :] = y_intra + _bdot(q[rs], s_c) * ecum_ref[...]
    y_ref[...] = _silu(gate) * (_head_stat(y_scr[...], True) * g_ref[...])

    @pl.when(t == pl.num_programs(1) - 1)
    def _():
        half = HEAD // 2
        for h in range(N_HEADS):
            vs = slice(h * HEAD, (h + 1) * HEAD)
            s_out_ref[0, h, 0:half, :] = s_ref[h * half:(h + 1) * half, vs]
            s_out_ref[0, h, half:HEAD, :] = s_ref[128 + h * half:128 + (h + 1) * half, vs]


def _ret_tables(block, seg):
    log_gamma = jnp.log(1.0 - jnp.exp2(-5.0 - jnp.arange(N_HEADS, dtype=F32)))
    pos = (jnp.arange(block) % seg).astype(F32)
    cum = (pos[:, None] + 1.0) * log_gamma[None, :]
    last = seg * log_gamma
    rep = lambda a: jnp.repeat(a, HEAD, axis=-1)
    same = (jnp.arange(block)[:, None] // seg) == (jnp.arange(block)[None, :] // seg)
    mask = same & (jnp.arange(block)[None, :] <= jnp.arange(block)[:, None])
    dec = jnp.exp(jnp.where(mask[None], cum.T[:, :, None] - cum.T[:, None, :], -jnp.inf))
    return (rep(jnp.exp(cum)), rep(jnp.exp(last[None, :] - cum)), rep(jnp.exp(last)[None, :]), dec)


def _ret(u, cos, sin_signed, gain, n_seq, seq_len):
    n = u.shape[0]
    vec = _const_spec((1, GROUP_W))
    nt = seq_len // MIX_TILE
    ecum, eend, elast, dec = _ret_tables(CHUNK, CHUNK)
    dec = dec.transpose(1, 0, 2).reshape(CHUNK, N_HEADS * CHUNK)
    row = lambda wd: pl.BlockSpec((MIX_TILE, wd), lambda b, t: (b * nt + t, 0))
    pos = pl.BlockSpec((MIX_TILE, GROUP_W), lambda b, t: (t, 0))
    return pl.pallas_call(
        _ret_prompt_kernel,
        grid=(n_seq, nt),
        in_specs=[row(u.shape[1]), pos, pos, _const_spec(ecum.shape), _const_spec(eend.shape), vec,
                  _const_spec(dec.shape), vec],
        out_specs=[row(GROUP_W), pl.BlockSpec((1, N_HEADS, HEAD, HEAD), lambda b, t: (b, 0, 0, 0))],
        out_shape=[jax.ShapeDtypeStruct((n, GROUP_W), F32),
                   jax.ShapeDtypeStruct((n_seq, N_HEADS, HEAD, HEAD), F32)],
        scratch_shapes=[pltpu.VMEM((GROUP_W, GROUP_W), F32), pltpu.VMEM((MIX_TILE, GROUP_W), F32)],
        compiler_params=_params(("parallel", "arbitrary")),
        name="ret_prompt",
    )(u, cos, sin_signed, ecum, eend, elast, dec, gain)


def _rg_gates(xr, wa_ref, ba_ref, wx_ref, bx_ref, lam_ref):
    r_gate = _sigmoid(_bdot(xr, wa_ref[...]) + ba_ref[...])
    i_gate = _sigmoid(_bdot(xr, wx_ref[...]) + bx_ref[...])
    log_a = (-RG_C * _softplus(-lam_ref[...])) * r_gate
    a = jnp.exp(log_a)
    b = jnp.sqrt(-jnp.tanh(log_a) * (a * a + 1.0)) * (i_gate * xr)
    return a, b


def _rg_scan(a, b, seg):
    pos = _iota(a.shape, 0) & (seg - 1)
    d = 1
    while d < seg:
        ok = pos >= d
        a_sh = jnp.where(ok, pltpu.roll(a, d, 0), 1.0)
        b_sh = jnp.where(ok, pltpu.roll(b, d, 0), 0.0)
        b = a * b_sh + b
        a = a * a_sh
        d *= 2
    return a, b


def _rg_scan_carry(a, b, h0):
    a, b = _rg_scan(a, b, 8)
    outs = []
    carry = h0
    for g in range(a.shape[0] // 8):
        h_g = b[g * 8:(g + 1) * 8] + a[g * 8:(g + 1) * 8] * carry
        outs.append(h_g)
        carry = h_g[7:8]
    return jnp.concatenate(outs, axis=0)


def _rg_prompt_kernel(u_ref, cw_ref, cb_ref, wa_ref, ba_ref, wx_ref, bx_ref, lam_ref,
                      y_ref, h_out_ref, buf_out_ref, h_ref, xx_ref):
    t = pl.program_id(1)
    rows = u_ref.shape[0]

    @pl.when(t == 0)
    def _():
        h_ref[...] = jnp.zeros(h_ref.shape, F32)

    u = u_ref[...]
    xr = _conv_taps_carry(u[:, 0:GROUP_W], xx_ref, cw_ref, cb_ref, t == 0)
    a, b = _rg_gates(xr, wa_ref, ba_ref, wx_ref, bx_ref, lam_ref)
    hseq = _rg_scan_carry(a, b, h_ref[0:1, :])
    h_ref[...] = jnp.broadcast_to(hseq[rows - 1:rows, :], h_ref.shape)
    y_ref[...] = _gelu_tanh(u[:, GROUP_W:]) * hseq

    @pl.when(t == pl.num_programs(1) - 1)
    def _():
        h_out_ref[0] = hseq[rows - 1:rows, :]
        buf_out_ref[0] = xx_ref[pl.ds(8 - (CONV_W - 1), CONV_W - 1), :]


def _rg_sample_kernel(u_ref, h0_ref, buf_ref, cw_ref, cb_ref, wa_ref, ba_ref, wx_ref, bx_ref, lam_ref,
                      y_ref, h_out_ref, buf_out_ref):
    u = u_ref[...]
    xr, new_buf = _conv_taps_seq4(u[:, 0:GROUP_W], buf_ref[...], cw_ref, cb_ref)
    buf_out_ref[...] = new_buf
    a, b = _rg_gates(xr, wa_ref, ba_ref, wx_ref, bx_ref, lam_ref)
    a_cum, h_loc = _rg_scan(a, b, 4)
    hseq = h_loc + a_cum * h0_ref[...]
    h_out_ref[...] = hseq
    y_ref[...] = _gelu_tanh(u[:, GROUP_W:]) * hseq


def _rg(u, h0_rows, buf, cw, cb, wa, ba, wx, bx, lam, n_seq, seq_len):
    n = u.shape[0]
    vec = _const_spec((1, GROUP_W))
    common_in = [_const_spec(cw.shape), vec, _const_spec(wa.shape), vec, _const_spec(wx.shape), vec, vec]
    if h0_rows is None:
        nt = seq_len // RG_TILE
        row = lambda wd: pl.BlockSpec((RG_TILE, wd), lambda b, t: (b * nt + t, 0))
        return pl.pallas_call(
            _rg_prompt_kernel,
            grid=(n_seq, nt),
            in_specs=[row(u.shape[1])] + common_in,
            out_specs=[row(GROUP_W), pl.BlockSpec((1, 1, GROUP_W), lambda b, t: (b, 0, 0)),
                       pl.BlockSpec((1, CONV_W - 1, GROUP_W), lambda b, t: (b, 0, 0))],
            out_shape=[jax.ShapeDtypeStruct((n, GROUP_W), F32),
                       jax.ShapeDtypeStruct((n_seq, 1, GROUP_W), F32),
                       jax.ShapeDtypeStruct((n_seq, CONV_W - 1, GROUP_W), F32)],
            scratch_shapes=[pltpu.VMEM((8, GROUP_W), F32), pltpu.VMEM((RG_TILE + 8, GROUP_W), F32)],
            compiler_params=_params(("parallel", "arbitrary")),
            name="rglru_prompt",
        )(u, cw, cb, wa, ba, wx, bx, lam)
    rows = SAMPLE_SEQS * seq_len
    row = lambda wd: pl.BlockSpec((rows, wd), lambda i: (i, 0))
    return pl.pallas_call(
        _rg_sample_kernel,
        grid=(n_seq // SAMPLE_SEQS,),
        in_specs=[row(u.shape[1]), row(GROUP_W), row(GROUP_W)] + common_in,
        out_specs=[row(GROUP_W)] * 3,
        out_shape=[jax.ShapeDtypeStruct((n, GROUP_W), F32)] * 3,
        compiler_params=_params(("parallel",)),
        name="rglru_sample",
    )(u, h0_rows, buf, cw, cb, wa, ba, wx, bx, lam)


def _hg_inputs(u_ref, lb_ref, seg_shift):
    u = u_ref[...]
    rows = u.shape[0]
    lb = lb_ref[...]
    fg = lb + (1.0 - lb) * _sigmoid(u[:, GROUP_W:2 * GROUP_W])
    q = _silu(u[:, 0:GROUP_W])
    k = 1.0 - fg
    v = u[:, 2 * GROUP_W:3 * GROUP_W]
    cum = _seg_cumsum(jnp.log(fg), seg_shift)
    return q, k, v, cum, u[:, 3 * GROUP_W:4 * GROUP_W]


def _hg_prompt_kernel(u_ref, lb_ref, g_ref, y_ref, s_out_ref, st_ref, y_scr):
    t = pl.program_id(1)
    rows = u_ref.shape[0]

    @pl.when(t == 0)
    def _():
        st_ref[...] = jnp.zeros(st_ref.shape, F32)

    q, k, v, cum, gate = _hg_inputs(u_ref, lb_ref, 6)
    hmask = _head_block_mask()
    hmask01 = hmask.astype(BF16)
    ones_bd = hmask01
    y_scr[...] = jnp.dot((q * k).astype(BF16), ones_bd, preferred_element_type=F32) * v
    row = _iota((rows, GROUP_W), 0)
    trow = _iota((CHUNK, GROUP_W), 0)
    tcol = _iota((CHUNK, GROUP_W), 1) & (CHUNK - 1)
    cum = cum * LOG2_E
    levels = []
    blk_last = cum
    h = 1
    while h < CHUNK:
        upper = (row & h) != 0
        qn = jnp.where(upper, q * jnp.exp2(cum - pltpu.roll(blk_last, h, 0)), 0.0)
        kn = k * jnp.exp2(blk_last - cum)
        sh = (2 * h).bit_length() - 1
        pair = ((trow >> sh) == (tcol >> sh)) & ((tcol & h) == 0)
        levels.append((pair, qn, kn))
        blk_last = jnp.where(upper, blk_last, pltpu.roll(blk_last, rows - h, 0))
        h *= 2
    q_in = q * jnp.exp2(cum)
    k_end = k * jnp.exp2(blk_last - cum)
    n_chunks = rows // CHUNK
    a_wide, ds_t = [], []
    for c in range(n_chunks):
        rs = slice(c * CHUNK, (c + 1) * CHUNK)
        a_c = jnp.zeros((CHUNK, GROUP_W), F32)
        for pair, qn, kn in levels:
            g = _bdot_nt(qn[rs], _block_diag_rows(kn[rs], hmask01))
            a_c = a_c + jnp.where(pair, g, 0.0)
        a_wide.append(a_c)
        ds_t.append(jnp.where(hmask, _bdot_tn(v[rs], k_end[rs]), 0.0))
    st = st_ref[...]
    st_in = []
    for c in range(n_chunks):
        st_in.append(st.astype(BF16))
        st = jnp.exp2(blk_last[c * CHUNK:c * CHUNK + 1, :]) * st + ds_t[c]
    st_ref[...] = st
    for c in range(n_chunks):
        rs = slice(c * CHUNK, (c + 1) * CHUNK)
        y_c = _bdot(a_wide[c], _block_diag_rows(v[rs], hmask01)) + _bdot_nt(q_in[rs], st_in[c])
        y_scr[rs, :] = y_scr[rs, :] + y_c
    y_ref[...] = _silu(gate) * (_head_stat(y_scr[...], False) * g_ref[...])

    @pl.when(t == pl.num_programs(1) - 1)
    def _():
        eye = (_iota((HEAD, HEAD), 0) == _iota((HEAD, HEAD), 1)).astype(BF16)
        for h in range(N_HEADS):
            s_out_ref[0, h] = _sel_dot_nt(eye, st_ref[h * HEAD:(h + 1) * HEAD, h * HEAD:(h + 1) * HEAD])


def _hg(u, lb, gain, n_seq, seq_len):
    n = u.shape[0]
    vec = _const_spec((1, GROUP_W))
    nt = seq_len // MIX_TILE
    row = lambda wd: pl.BlockSpec((MIX_TILE, wd), lambda b, t: (b * nt + t, 0))
    return pl.pallas_call(
        _hg_prompt_kernel,
        grid=(n_seq, nt),
        in_specs=[row(u.shape[1]), vec, vec],
        out_specs=[row(GROUP_W), pl.BlockSpec((1, N_HEADS, HEAD, HEAD), lambda b, t: (b, 0, 0, 0))],
        out_shape=[jax.ShapeDtypeStruct((n, GROUP_W), F32),
                   jax.ShapeDtypeStruct((n_seq, N_HEADS, HEAD, HEAD), F32)],
        scratch_shapes=[pltpu.VMEM((GROUP_W, GROUP_W), F32), pltpu.VMEM((MIX_TILE, GROUP_W), F32)],
        compiler_params=_params(("parallel", "arbitrary")),
        name="hgrn_prompt",
    )(u, lb, gain)


def _t_recurrence(q_scr, k_scr, v, dec_row, dec_scr, s_ref, n_seq, seq_len):
    cols = [slice(t * n_seq, (t + 1) * n_seq) for t in range(seq_len)]
    v_t = [v[:, c] for c in cols]
    outs = [jnp.zeros((HEAD, n_seq), F32) for _ in cols]
    for k in range(HEAD):
        s = s_ref[k]
        for t, c in enumerate(cols):
            dec = dec_row if dec_scr is None else dec_scr[k:k + 1, c]
            s = dec * s + k_scr[k:k + 1, c] * v_t[t]
            outs[t] = outs[t] + q_scr[k:k + 1, c] * s
        s_ref[k] = s
    return jnp.concatenate(outs, axis=1)


def _t_head_norm(o, center):
    if center:
        o = o - jnp.mean(o, axis=0, keepdims=True)
    return o * lax.rsqrt(jnp.mean(o * o, axis=0, keepdims=True) + EPS)


def _ret_t_kernel(q_ref, k_ref, v_ref, g_ref, cos_ref, sin_ref, gam_ref, gain_ref, s0_ref,
                  y_ref, s_ref, q_scr, k_scr, *, n_seq, seq_len):
    half = HEAD // 2

    def rope(x):
        swapped = jnp.concatenate([x[half:], x[:half]], axis=0)
        return x * cos_ref[...] + swapped * sin_ref[...]

    q_scr[...] = rope(q_ref[...])
    k_scr[...] = rope(k_ref[...]) * (HEAD ** -0.5)
    s_ref[0] = s0_ref[0]
    o = _t_recurrence(q_scr, k_scr, v_ref[...], gam_ref[0], None, s_ref.at[0], n_seq, seq_len)
    gain = jnp.concatenate([gain_ref[...]] * seq_len, axis=1)
    parts = [_t_head_norm(o[:, t * n_seq:(t + 1) * n_seq], True) for t in range(seq_len)]
    y_ref[...] = _silu(g_ref[...]) * (jnp.concatenate(parts, axis=1) * gain)


def _hg_t_kernel(q_ref, f_ref, i_ref, g_ref, lb_ref, gain_ref, s0_ref, y_ref, s_ref, q_scr, k_scr, f_scr,
                 *, n_seq, seq_len):
    lb = jnp.concatenate([lb_ref[...]] * seq_len, axis=1)
    fg = lb + (1.0 - lb) * _sigmoid(f_ref[...])
    q_scr[...] = _silu(q_ref[...])
    k_scr[...] = 1.0 - fg
    f_scr[...] = fg
    s_ref[0] = s0_ref[0]
    o = _t_recurrence(q_scr, k_scr, i_ref[...], None, f_scr, s_ref.at[0], n_seq, seq_len)
    gain = jnp.concatenate([gain_ref[...]] * seq_len, axis=1)
    parts = [_t_head_norm(o[:, t * n_seq:(t + 1) * n_seq], False) for t in range(seq_len)]
    y_ref[...] = _silu(g_ref[...]) * (jnp.concatenate(parts, axis=1) * gain)


def _t_specs(n, n_seq):
    head_rows = lambda section: pl.BlockSpec((HEAD, n), lambda h, s=section: (N_HEADS * s + h, 0))
    per_head = pl.BlockSpec((HEAD, n_seq), lambda h: (h, 0))
    state_in = lambda layer: pl.BlockSpec((None, 1, HEAD, HEAD, n_seq), lambda h: (layer, h, 0, 0, 0))
    state_out = ((1, HEAD, HEAD, n_seq), lambda h: (h, 0, 0, 0))
    return head_rows, per_head, state_in, state_out


def _ret_t(u_t, cos_t, sin_t, gam, gain_b, state_t, n_seq, seq_len, layer, prev):
    n = u_t.shape[1]
    head_rows, per_head, state_in, state_out = _t_specs(n, n_seq)
    return _stack_call(
        functools.partial(_ret_t_kernel, n_seq=n_seq, seq_len=seq_len), prev, {1: state_out}, layer,
        in_specs=[head_rows(0), head_rows(1), head_rows(2), head_rows(3), _const_spec(cos_t.shape),
                  _const_spec(sin_t.shape), pl.BlockSpec((1, 1, n_seq), lambda h: (h, 0, 0)), per_head,
                  state_in(layer)],
        args=(u_t, u_t, u_t, u_t, cos_t, sin_t, gam, gain_b, state_t),
        grid=(N_HEADS,),
        out_specs=[pl.BlockSpec((HEAD, n), lambda h: (h, 0)), None],
        out_shape=[jax.ShapeDtypeStruct((GROUP_W, n), F32), jax.ShapeDtypeStruct(state_t.shape, F32)],
        scratch_shapes=[pltpu.VMEM((HEAD, n), F32)] * 2,
        compiler_params=_params(("parallel",)),
        name="ret_sample_t",
    )


def _hg_t(u_t, lb_b, gain_b, state_t, n_seq, seq_len, layer, prev):
    n = u_t.shape[1]
    head_rows, per_head, state_in, state_out = _t_specs(n, n_seq)
    return _stack_call(
        functools.partial(_hg_t_kernel, n_seq=n_seq, seq_len=seq_len), prev, {1: state_out}, layer,
        in_specs=[head_rows(0), head_rows(1), head_rows(2), head_rows(3), per_head, per_head, state_in(layer)],
        args=(u_t, u_t, u_t, u_t, lb_b, gain_b, state_t),
        grid=(N_HEADS,),
        out_specs=[pl.BlockSpec((HEAD, n), lambda h: (h, 0)), None],
        out_shape=[jax.ShapeDtypeStruct((GROUP_W, n), F32), jax.ShapeDtypeStruct(state_t.shape, F32)],
        scratch_shapes=[pltpu.VMEM((HEAD, n), F32)] * 3,
        compiler_params=_params(("parallel",)),
        name="hgrn_sample_t",
    )


def _rope_tables_t(pos0, seq_len, n_seq):
    half = HEAD // 2
    pos = pos0 + jnp.arange(seq_len, dtype=F32)
    inv = ROPE_BASE ** (-jnp.arange(half, dtype=F32) / half)
    ang = pos[:, None] * inv
    cos = jnp.tile(jnp.cos(ang).T, (2, 1))
    sin = jnp.sin(ang).T
    sin_signed = jnp.concatenate([-sin, sin], axis=0)
    return jnp.repeat(cos, n_seq, axis=1), jnp.repeat(sin_signed, n_seq, axis=1)


def _block_diag(w):
    h, i, j = w.shape
    eye = jnp.eye(h, dtype=w.dtype)
    return (eye[:, None, :, None] * w[:, :, None, :]).reshape(h * i, h * j)


def _rope_tables(pos0, seq_len):
    half = HEAD // 2
    pos = pos0 + jnp.arange(seq_len, dtype=F32)
    inv = ROPE_BASE ** (-jnp.arange(half, dtype=F32) / half)
    ang = pos[:, None] * inv
    cos = jnp.tile(jnp.cos(ang), (1, 2 * N_HEADS))
    sin = jnp.sin(ang)
    sin_signed = jnp.concatenate([jnp.tile(-sin, (1, N_HEADS)), jnp.tile(sin, (1, N_HEADS))], axis=-1)
    return cos, sin_signed


def _layer_weights(P, l):
    row = lambda a: a.reshape(1, -1)
    pad128 = lambda a: jnp.pad(a, (0, 128 - a.shape[0])).reshape(1, 128)
    lb_sm = jax.nn.softmax(P['hg_lb'].astype(F32), axis=0)
    lb = (jnp.cumsum(lb_sm, axis=0) - lb_sm[0])[l]
    return dict(
        w_in=P['w_in_prepped'], w_in_t=P['w_in_t'],
        ln_mix_pre=row(P['ln_mix_pre'][l]), ln_mix_post=row(P['ln_mix_post'][l]),
        ln_xa_pre=row(P['ln_xa_pre'][l]), ln_xa_post=row(P['ln_xa_post'][l]),
        ln_ffn_pre=row(P['ln_ffn_pre'][l]), ln_ffn_post=row(P['ln_ffn_post'][l]),
        ssd_cw=P['ssd_conv_w'][l], ssd_cb=row(P['ssd_conv_b'][l]),
        ssd_dtb=pad128(P['ssd_dt_bias'][l]), ssd_alog=pad128(P['ssd_A_log'][l]),
        ssd_d=row(jnp.repeat(P['ssd_D'][l], HEAD)), ssd_norm=row(P['ssd_norm'][l]),
        ret_norm=row(P['ret_norm'][l]),
        rg_cw=P['rg_conv_w'][l], rg_cb=row(P['rg_conv_b'][l]),
        rg_wa=_block_diag(P['rg_wa'][l]).astype(BF16), rg_ba=row(P['rg_ba'][l]),
        rg_wx=_block_diag(P['rg_wx'][l]).astype(BF16), rg_bx=row(P['rg_bx'][l]),
        rg_lam=row(P['rg_lambda'][l]),
        hg_lb=row(lb), hg_norm=row(P['hg_norm'][l]),
        w_out=P['w_out_bf16'], w_xq=P['w_xq_bf16'], w_xo=P['w_xo_bf16'], w_gu=P['w_gu_bf16'],
        w_down=P['w_down_bf16'], ln_mem=row(P['ln_mem'][l]), w_xkv=P['w_xkv'],
    )


def _layer(x, W, k_mem, v_mem, st, n_seq, seq_len, pos0, layer, prev=None):
    if st is None:
        u_ssd, u_ret, u_rg, u_hg, u_dt = _norm_matmul(
            x, W['ln_mix_pre'], W['w_in'], (IN_COLS_SSD, IN_COLS_RET_SPLIT, IN_COLS_RG, IN_COLS_HG, IN_COLS_DT),
            "in_proj", layer=layer)
        cos, sin_signed = _rope_tables(pos0, seq_len)
        y_ssd, s_ssd, b_ssd = _ssd(u_ssd, u_dt, None, None, W['ssd_cw'], W['ssd_cb'], W['ssd_dtb'],
                                   W['ssd_alog'], W['ssd_d'], W['ssd_norm'], n_seq, seq_len, layer)
        y_ret, s_ret = _ret(u_ret, cos, sin_signed, W['ret_norm'], n_seq, seq_len)
        y_rg, h_rg, b_rg = _rg(u_rg, None, None, W['rg_cw'], W['rg_cb'], W['rg_wa'], W['rg_ba'],
                               W['rg_wx'], W['rg_bx'], W['rg_lam'], n_seq, seq_len)
        h_rg = h_rg.reshape(n_seq, GROUP_W)
        y_hg, s_hg = _hg(u_hg, W['hg_lb'], W['hg_norm'], n_seq, seq_len)
    else:
        ssd_s, ssd_buf, ret_s, rg_h, rg_buf, hg_s = st
        u_ssd, u_rg, u_dt = _norm_matmul(x, W['ln_mix_pre'], W['w_in'], (IN_COLS_SSD, IN_COLS_RG, IN_COLS_DT),
                                         "in_proj", layer=layer)
        to_tb = lambda a: a.reshape(n_seq, seq_len, -1).transpose(1, 0, 2).reshape(n_seq * seq_len, -1)
        from_t = lambda a: a.reshape(-1, seq_len, n_seq).transpose(2, 1, 0).reshape(n_seq * seq_len, -1)
        bcast = lambda g: jnp.broadcast_to(g.reshape(-1, 1), (g.size, n_seq))
        ut_ret, ut_hg = _in_proj_t(to_tb(x), W['ln_mix_pre'], W['w_in_t'], layer)
        cos_t, sin_t = _rope_tables_t(pos0, seq_len, n_seq)
        log_gamma = jnp.log(1.0 - jnp.exp2(-5.0 - jnp.arange(N_HEADS, dtype=F32)))
        gam = jnp.broadcast_to(jnp.exp(log_gamma)[:, None, None], (N_HEADS, 1, n_seq))
        yt_ret, s_ret = _ret_t(ut_ret, cos_t, sin_t, gam, bcast(W['ret_norm']), ret_s, n_seq, seq_len, layer,
                               prev and (prev[1],))
        yt_hg, s_hg = _hg_t(ut_hg, bcast(W['hg_lb']), bcast(W['hg_norm']), hg_s, n_seq, seq_len, layer,
                            prev and (prev[2],))
        y_ret, y_hg = from_t(yt_ret), from_t(yt_hg)
        pad_rows = lambda b: jnp.pad(b, ((0, 0), (0, 1), (0, 0))).reshape(n_seq * 4, b.shape[-1])
        y_ssd, s_ssd, b_ssd = _ssd(u_ssd, u_dt, ssd_s, pad_rows(ssd_buf), W['ssd_cw'], W['ssd_cb'],
                                   W['ssd_dtb'], W['ssd_alog'], W['ssd_d'], W['ssd_norm'], n_seq, seq_len, layer,
                                   prev and (prev[0],))
        b_ssd = b_ssd.reshape(n_seq, 4, -1)[:, :CONV_W - 1]
        y_rg, h_rows, b_rg = _rg(u_rg, jnp.repeat(rg_h, seq_len, axis=0), pad_rows(rg_buf), W['rg_cw'],
                                 W['rg_cb'], W['rg_wa'], W['rg_ba'], W['rg_wx'], W['rg_bx'], W['rg_lam'],
                                 n_seq, seq_len)
        h_rg = h_rows.reshape(n_seq, seq_len, GROUP_W)[:, seq_len - 1]
        b_rg = b_rg.reshape(n_seq, 4, -1)[:, :CONV_W - 1]
    ys = (y_ssd, y_ret, y_rg, y_hg)
    if st is None:
        x3 = _xattn_ffn(ys, x, W['w_out'], W['ln_mix_post'], k_mem, v_mem, W['ln_xa_pre'], W['w_xq'], W['w_xo'],
                        W['ln_xa_post'], W['ln_ffn_pre'], W['w_gu'], W['w_down'], W['ln_ffn_post'], seq_len,
                        layer)
    else:
        x1, q = _out_proj(ys, x, W['w_out'], W['ln_mix_post'], W['ln_xa_pre'], W['w_xq'], layer)
        o = _attention_sample(q, k_mem, v_mem, layer, seq_len)
        x3 = _ffn(o, x1, W['w_xo'], W['ln_xa_post'], W['ln_ffn_pre'], W['w_gu'], W['w_down'],
                  W['ln_ffn_post'], layer)
    return x3, (s_ssd, b_ssd, s_ret, h_rg, b_rg, s_hg)


def kernel(x_prompt, x_sample, state_ssd, state_ssd_conv, state_ret, state_rglru, state_rglru_conv, state_hgrn, cache_mem_k, cache_mem_v, mem_prompt, ln_mix_pre, ln_mix_post, ln_xa_pre, ln_xa_post, ln_ffn_pre, ln_ffn_post, w_in, ssd_conv_w, ssd_conv_b, ssd_dt_bias, ssd_A_log, ssd_D, ssd_norm, ret_norm, rg_conv_w, rg_conv_b, rg_wa, rg_ba, rg_wx, rg_bx, rg_lambda, hg_lb, hg_norm, w_out, ln_mem, w_xq, w_xkv, w_xo, w_gu, w_down):
    P = dict(ln_mix_pre=ln_mix_pre, ln_mix_post=ln_mix_post, ln_xa_pre=ln_xa_pre, ln_xa_post=ln_xa_post,
             ln_ffn_pre=ln_ffn_pre, ln_ffn_post=ln_ffn_post, w_in=w_in, ssd_conv_w=ssd_conv_w,
             ssd_conv_b=ssd_conv_b, ssd_dt_bias=ssd_dt_bias, ssd_A_log=ssd_A_log, ssd_D=ssd_D,
             ssd_norm=ssd_norm, ret_norm=ret_norm, rg_conv_w=rg_conv_w, rg_conv_b=rg_conv_b,
             rg_wa=rg_wa, rg_ba=rg_ba, rg_wx=rg_wx, rg_bx=rg_bx, rg_lambda=rg_lambda, hg_lb=hg_lb,
             hg_norm=hg_norm, w_out=w_out, ln_mem=ln_mem, w_xq=w_xq, w_xkv=w_xkv, w_xo=w_xo,
             w_gu=w_gu, w_down=w_down)
    P['w_in_prepped'], P['w_in_t'] = _in_weight_prep(w_in)
    for name in ('w_out', 'w_xq', 'w_xo', 'w_gu', 'w_down'):
        P[name + '_bf16'] = P[name].astype(BF16)
    ssd_t = jnp.swapaxes(state_ssd, -1, -2)
    ret_t = jnp.transpose(state_ret, (0, 2, 3, 4, 1))
    hgrn_t = jnp.transpose(state_hgrn, (0, 2, 3, 4, 1))
    depth = w_in.shape[0]
    bp, tp, d = x_prompt.shape
    bs, ts, _ = x_sample.shape
    n_mem = mem_prompt.shape[1]
    y_p = x_prompt.reshape(bp * tp, d)
    y_s = x_sample.reshape(bs * ts, d)
    mem = mem_prompt.reshape(bp * n_mem, d)
    p_st, s_st = [], []
    mem_prev = samp_prev = None
    for l in range(depth):
        W = _layer_weights(P, l)
        mk, mv, mk5, mv5 = _mem_kv(mem, W['ln_mem'], W['w_xkv'], l, n_mem, mem_prev)
        mem_prev = (mk5, mv5)
        mk = mk.reshape(bp, n_mem, d)
        mv = mv.reshape(bp, n_mem, d)
        y_p, st_p = _layer(y_p, W, mk, mv, None, bp, tp, 0.0, l)
        p_st.append(st_p)
        st_l = (ssd_t, state_ssd_conv[l], ret_t, state_rglru[l], state_rglru_conv[l], hgrn_t)
        y_s, st_s = _layer(y_s, W, cache_mem_k, cache_mem_v, st_l, bs, ts, float(PAST_LEN), l, samp_prev)
        samp_prev = (st_s[0], st_s[2], st_s[5])
        s_st.append(st_s)
    stack = lambda sts, i: jnp.stack([s[i] for s in sts], axis=0)
    return (y_p.reshape(bp, tp, d), y_s.reshape(bs, ts, d),
            jnp.swapaxes(stack(p_st, 0), -1, -2), stack(p_st, 1), stack(p_st, 2), stack(p_st, 3), stack(p_st, 4),
            stack(p_st, 5),
            mem_prev[0], mem_prev[1],
            jnp.swapaxes(samp_prev[0], -1, -2), stack(s_st, 1), jnp.transpose(samp_prev[1], (0, 4, 1, 2, 3)),
            stack(s_st, 3), stack(s_st, 4), jnp.transpose(samp_prev[2], (0, 4, 1, 2, 3)))
```

```python
import functools
import math

import jax
import jax.numpy as jnp
from jax import lax
from jax.experimental import pallas as pl
from jax.experimental.pallas import tpu as pltpu

F32 = jnp.float32
BF16 = jnp.bfloat16
EPS = 1e-6
LOG2_E = 1.4426950408889634

GROUP_W = 256
HEAD = 64
N_HEADS = 4
SSD_N = 128
SSD_G = 2
CONV_W = 4
XA_H = 4
XA_HD = 256
ROPE_BASE = 10000.0
RG_C = 8.0
CHUNK = 64
PAST_LEN = 16384

ROW_TILE = 512
XA_TILE = 1024
IN_TILE = 1024
MIX_TILE = 2048
RG_TILE = 2048
SAMPLE_SEQS = 16
ATT_SEQS = 8
FF_TILE = 256
FFN_CHUNKS = 2
VMEM_LIMIT = 56 * 1024 * 1024
XA_VMEM_LIMIT = 62 * 1024 * 1024


def _bdot(a, b):
    return jnp.dot(a.astype(BF16), b.astype(BF16), preferred_element_type=F32)


def _bdot_nt(a, b):
    return lax.dot_general(a.astype(BF16), b.astype(BF16), (((1,), (1,)), ((), ())),
                           preferred_element_type=F32)


def _bdot_tn(a, b):
    return lax.dot_general(a.astype(BF16), b.astype(BF16), (((0,), (0,)), ((), ())),
                           preferred_element_type=F32)


def _split3(x):
    hi = x.astype(BF16)
    r = x - hi.astype(F32)
    mid = r.astype(BF16)
    lo = (r - mid.astype(F32)).astype(BF16)
    return hi, mid, lo


def _sel_dot(sel, x):
    hi, mid, lo = _split3(x)
    d = lambda y: jnp.dot(sel, y, preferred_element_type=F32)
    return (d(hi) + d(mid)) + d(lo)


def _dot_sel(x, sel):
    hi, mid, lo = _split3(x)
    d = lambda y: jnp.dot(y, sel, preferred_element_type=F32)
    return (d(hi) + d(mid)) + d(lo)


def _sel_dot_nt(sel, x):
    hi, mid, lo = _split3(x)
    d = lambda y: lax.dot_general(sel, y, (((1,), (1,)), ((), ())), preferred_element_type=F32)
    return (d(hi) + d(mid)) + d(lo)


def _rms(x, g):
    return x * lax.rsqrt(jnp.mean(x * x, axis=-1, keepdims=True) + EPS) * g


def _sigmoid(x):
    return jax.nn.sigmoid(x)


def _silu(x):
    return x * jax.nn.sigmoid(x)


def _softplus(x):
    return jnp.maximum(x, 0.0) + jnp.log1p(jnp.exp(-jnp.abs(x)))


def _gelu_tanh(x):
    c = math.sqrt(2.0 / math.pi)
    return 0.5 * x * (1.0 + jnp.tanh(c * (x + 0.044715 * (x * x * x))))


def _iota(shape, dim):
    return lax.broadcasted_iota(jnp.int32, shape, dim)


def _seg_causal_mask(n, seg_shift):
    r = _iota((n, n), 0)
    c = _iota((n, n), 1)
    return ((r >> seg_shift) == (c >> seg_shift)) & (c <= r)


def _seg_cumsum(x, seg_shift):
    rows = x.shape[0]
    blk = min(rows, 256)
    mask = _seg_causal_mask(blk, seg_shift).astype(BF16)
    return jnp.concatenate([_sel_dot(mask, x[i:i + blk]) for i in range(0, rows, blk)], axis=0)


def _head_expand_mat(width_in=128):
    r = _iota((width_in, GROUP_W), 0)
    c = _iota((width_in, GROUP_W), 1)
    return ((c >> 6) == r).astype(BF16)


def _head_block_mask():
    return (_iota((GROUP_W, GROUP_W), 0) >> 6) == (_iota((GROUP_W, GROUP_W), 1) >> 6)


def _block_diag_rows(x, mask01):
    return jnp.concatenate([x.astype(BF16)] * N_HEADS, axis=0) * mask01


def _head_stat(y, center):
    ones_bd = _head_block_mask().astype(BF16)
    if center:
        y = y - _dot_sel(y, ones_bd) * (1.0 / HEAD)
    return y * lax.rsqrt(_dot_sel(y * y, ones_bd) * (1.0 / HEAD) + EPS)


def _conv_taps_carry(x, xx_ref, w_ref, b_ref, first):
    rows = x.shape[0]

    @pl.when(first)
    def _():
        xx_ref[0:8, :] = jnp.zeros((8, x.shape[1]), F32)

    xx_ref[8:8 + rows, :] = x
    y = b_ref[...] + w_ref[CONV_W - 1:CONV_W, :] * x
    for j in range(CONV_W - 1):
        y = y + w_ref[j:j + 1, :] * xx_ref[pl.ds(8 - (CONV_W - 1) + j, rows), :]
    xx_ref[0:8, :] = xx_ref[rows:rows + 8, :]
    return y


def _conv_taps_seq4(x, buf4, w_ref, b_ref):
    rows = x.shape[0]
    r = _iota((rows, rows), 0)
    c = _iota((rows, rows), 1)
    t = r & 3
    y = b_ref[...] + w_ref[CONV_W - 1:CONV_W, :] * x
    for d in range(1, CONV_W):
        shift = ((c == r - d) & (t >= d)).astype(BF16)
        hist = ((t < d) & (c == (r - t) + (3 + t - d))).astype(BF16)
        y = y + w_ref[CONV_W - 1 - d:CONV_W - d, :] * (_sel_dot(shift, x) + _sel_dot(hist, buf4))
    new_sel = (((r & 3) < 3) & (c == r + 1)).astype(BF16)
    return y, _sel_dot(new_sel, x)


def _norm_matmul_kernel(x_ref, g_ref, w_ref, *o_refs, col_ranges):
    h = _rms(x_ref[...], g_ref[...]).astype(BF16)
    for o_ref, pieces in zip(o_refs, col_ranges):
        off = 0
        for a, b in pieces:
            o_ref[:, off:off + b - a] = jnp.dot(h, w_ref[:, a:b], preferred_element_type=F32)
            off += b - a


def _const_spec(shape):
    nd = len(shape)
    return pl.BlockSpec(shape, lambda *_: (0,) * nd, pipeline_mode=pl.Buffered(1))


def _layer_spec(shape, layer):
    nd = len(shape)
    return pl.BlockSpec((None,) + tuple(shape[1:]), lambda *_: (layer,) + (0,) * (nd - 1),
                        pipeline_mode=pl.Buffered(1))


def _stack_call(kernel, prevs, stacked, layer, *, in_specs, args, out_specs, out_shape, **kw):
    n_in = len(in_specs)
    out_specs = list(out_specs)
    order = sorted(stacked)
    for o in order:
        blk, idx = stacked[o]
        if prevs is None:
            depth = out_shape[o].shape[0]
            out_specs[o] = pl.BlockSpec((depth,) + tuple(blk), lambda *g, idx=idx: (0,) + tuple(idx(*g)))
        else:
            out_specs[o] = pl.BlockSpec((None,) + tuple(blk), lambda *g, idx=idx: (layer,) + tuple(idx(*g)))
    if prevs is None:
        def body(*refs):
            refs = list(refs)
            for o in order:
                full = refs[n_in + o]
                for d in range(full.shape[0]):
                    if d != layer:
                        full[d] = jnp.zeros(full.shape[1:], full.dtype)
                refs[n_in + o] = full.at[layer]
            return kernel(*refs)

        return pl.pallas_call(body, in_specs=in_specs, out_specs=out_specs, out_shape=out_shape, **kw)(*args)

    def body(*refs):
        return kernel(*refs[:n_in], *refs[n_in + len(prevs):])

    return pl.pallas_call(
        body,
        in_specs=list(in_specs) + [pl.BlockSpec(memory_space=pl.ANY)] * len(prevs),
        out_specs=out_specs, out_shape=out_shape,
        input_output_aliases={n_in + j: o for j, o in enumerate(order)},
        **kw)(*args, *prevs)


def _params(sem, vmem=VMEM_LIMIT):
    return pltpu.CompilerParams(dimension_semantics=sem, vmem_limit_bytes=vmem)


def _in_weight_prep_kernel(wt_ref, o_ref, ot_ref, *, blocks, dt_block, dt_row, dt_n, t_rows):
    off = 0
    for src, n in t_rows:
        ot_ref[off:off + n, :] = wt_ref[pl.ds(src, n), :].astype(BF16)
        off += n
    for j, pieces in enumerate(blocks):
        if j == dt_block:
            dt = wt_ref[pl.ds(dt_row, 8), :]
            dt = jnp.where(_iota(dt.shape, 0) < dt_n, dt, 0.0)
            blk = jnp.concatenate([dt, jnp.zeros((120, dt.shape[1]), F32)], axis=0)
        else:
            blk = jnp.concatenate([wt_ref[pl.ds(src, n), :] for src, n in pieces], axis=0)
        o_ref[:, j * 128:(j + 1) * 128] = blk.T.astype(BF16)


IN_COLS_SSD = ((0, 1024),)
IN_COLS_RG = ((2048, 2560),)
IN_COLS_HG = ((2560, 3584),)
IN_COLS_DT = ((3584, 3712),)
IN_COLS_RET_SPLIT = ((3712, 4224), (1536, 2048))


def _in_weight_prep(w_in):
    depth, d, d_in = w_in.shape
    z0 = GROUP_W + (GROUP_W + 2 * SSD_G * SSD_N)
    d0 = z0 + N_HEADS
    blocks = [((src, 128),) for src in tuple(range(0, z0, 128)) + tuple(range(d0, d_in, 128))]
    dt_block = len(blocks)
    blocks.append(())
    half = HEAD // 2
    for base in (d0, d0 + GROUP_W):
        for j in range(2):
            blocks.append(tuple((base + h * HEAD + j * half, half) for h in range(N_HEADS)))
    n_out = len(blocks) * 128
    t_rows = ((d0, 4 * GROUP_W), (d0 + 6 * GROUP_W, 4 * GROUP_W))
    n_t = sum(n for _, n in t_rows)
    wt = jnp.swapaxes(w_in, 1, 2)
    return pl.pallas_call(
        functools.partial(_in_weight_prep_kernel, blocks=tuple(blocks), dt_block=dt_block, dt_row=z0,
                          dt_n=N_HEADS, t_rows=t_rows),
        grid=(depth,),
        in_specs=[pl.BlockSpec((None, d_in, d), lambda l: (l, 0, 0), pipeline_mode=pl.Buffered(1))],
        out_specs=[pl.BlockSpec((None, d, n_out), lambda l: (l, 0, 0)),
                   pl.BlockSpec((None, n_t, d), lambda l: (l, 0, 0))],
        out_shape=[jax.ShapeDtypeStruct((depth, d, n_out), BF16), jax.ShapeDtypeStruct((depth, n_t, d), BF16)],
        compiler_params=_params(("parallel",)),
        name="in_weight_prep",
    )(wt)


def _in_proj_t_kernel(x_ref, g_ref, wt_ref, *o_refs):
    h = _rms(x_ref[...], g_ref[...]).astype(BF16)
    off = 0
    for o_ref in o_refs:
        f = o_ref.shape[0]
        o_ref[...] = lax.dot_general(wt_ref[off:off + f, :], h, (((1,), (1,)), ((), ())),
                                     preferred_element_type=F32)
        off += f


def _in_proj_t(x, g, wt, layer):
    n, d = x.shape
    f = wt.shape[1] // 2
    return pl.pallas_call(
        _in_proj_t_kernel,
        grid=(1,),
        in_specs=[_const_spec((n, d)), _const_spec((1, d)), _layer_spec(wt.shape, layer)],
        out_specs=[_const_spec((f, n))] * 2,
        out_shape=[jax.ShapeDtypeStruct((f, n), F32)] * 2,
        compiler_params=_params(("arbitrary",)),
        name="in_proj_t",
    )(x, g, wt)


def _mem_kv_kernel(x_ref, g_ref, w_ref, k_ref, v_ref, k5_ref, v5_ref):
    h = _rms(x_ref[...], g_ref[...]).astype(BF16)
    d = x_ref.shape[1]
    k = jnp.dot(h, w_ref[:, 0:d].astype(BF16), preferred_element_type=F32)
    v = jnp.dot(h, w_ref[:, d:2 * d].astype(BF16), preferred_element_type=F32)
    k_ref[...] = k
    v_ref[...] = v
    k5_ref[...] = k.reshape(k5_ref.shape)
    v5_ref[...] = v.reshape(v5_ref.shape)


def _mem_kv(mem, g, w, layer, n_mem, prev):
    n, d = mem.shape
    depth = w.shape[0]
    tm = min(ROW_TILE, n)
    seqs = tm // n_mem
    row = pl.BlockSpec((tm, d), lambda i: (i, 0))
    blk5 = ((seqs, n_mem, XA_H, XA_HD), lambda i: (i, 0, 0, 0))
    shape5 = jax.ShapeDtypeStruct((depth, n // n_mem, n_mem, XA_H, XA_HD), F32)
    return _stack_call(
        _mem_kv_kernel, prev, {2: blk5, 3: blk5}, layer,
        in_specs=[row, _const_spec((1, d)), _wspec(w, layer)],
        args=(mem, g, w),
        grid=(n // tm,),
        out_specs=[row, row, None, None],
        out_shape=[jax.ShapeDtypeStruct((n, d), F32)] * 2 + [shape5] * 2,
        compiler_params=_params(("parallel",)),
        name="mem_kv",
    )


def _wspec(w, layer):
    return _layer_spec(w.shape, layer) if w.ndim == 3 else _const_spec(w.shape)


def _norm_matmul(x, g, w, col_ranges, name, layer=None):
    n, d = x.shape
    tm = min(IN_TILE, n)
    w_spec = _wspec(w, layer)
    widths = [sum(b - a for a, b in pieces) for pieces in col_ranges]
    return pl.pallas_call(
        functools.partial(_norm_matmul_kernel, col_ranges=col_ranges),
        grid=(n // tm,),
        in_specs=[pl.BlockSpec((tm, d), lambda i: (i, 0)), _const_spec((1, d)), w_spec],
        out_specs=[pl.BlockSpec((tm, wd), lambda i: (i, 0)) for wd in widths],
        out_shape=[jax.ShapeDtypeStruct((n, wd), F32) for wd in widths],
        compiler_params=_params(("parallel",)),
        name=name,
    )(x, g, w)


def _out_proj_kernel(ys_ref, yr_ref, yg_ref, yh_ref, x_ref, wo_ref, gpost_ref, gpre_ref, wq_ref, x1_ref, q_ref):
    x1 = _mix_out(ys_ref, yr_ref, yg_ref, yh_ref, x_ref, wo_ref, gpost_ref)
    x1_ref[...] = x1
    q_ref[...] = _bdot(_rms(x1, gpre_ref[...]), wq_ref[...])


def _out_proj(ys, x, w_out, g_post, g_pre, w_xq, layer):
    n, d = x.shape
    tm = min(ROW_TILE, n)
    row = lambda wd: pl.BlockSpec((tm, wd), lambda i: (i, 0))
    vec = _const_spec((1, d))
    return pl.pallas_call(
        _out_proj_kernel,
        grid=(n // tm,),
        in_specs=[row(GROUP_W)] * 4 + [row(d), _wspec(w_out, layer), vec, vec, _wspec(w_xq, layer)],
        out_specs=[row(d)] * 2,
        out_shape=[jax.ShapeDtypeStruct((n, d), F32)] * 2,
        compiler_params=_params(("parallel",)),
        name="out_proj",
    )(*ys, x, w_out, g_post, g_pre, w_xq)


def _ffn_tail(o, x1, wxo_ref, gxa_ref, gpre_ref, wgu_ref, wdn_ref, gpost_ref, d_ff):
    a = _bdot(o, wxo_ref[...])
    x2 = x1 + _rms(a, gxa_ref[...])
    h = _rms(x2, gpre_ref[...]).astype(BF16)
    acc = jnp.zeros(x2.shape, F32)
    for j in range(0, d_ff, FF_TILE):
        g = jnp.dot(h, wgu_ref[:, j:j + FF_TILE], preferred_element_type=F32)
        u = jnp.dot(h, wgu_ref[:, d_ff + j:d_ff + j + FF_TILE], preferred_element_type=F32)
        act = (_silu(g) * u).astype(BF16)
        acc = acc + jnp.dot(act, wdn_ref[j:j + FF_TILE, :], preferred_element_type=F32)
    return x2 + _rms(acc, gpost_ref[...])


def _ffn_kernel(o_ref, x1_ref, wxo_ref, gxa_ref, gpre_ref, wg_ref, wu_ref, wdn_ref, gpost_ref, x3_ref,
                x2_scr, h_scr, acc_scr):
    j = pl.program_id(1)

    @pl.when(j == 0)
    def _():
        a = _bdot(o_ref[...], wxo_ref[...])
        x2 = x1_ref[...] + _rms(a, gxa_ref[...])
        x2_scr[...] = x2
        h_scr[...] = _rms(x2, gpre_ref[...]).astype(BF16)
        acc_scr[...] = jnp.zeros_like(acc_scr)

    h = h_scr[...]
    acc = acc_scr[...]
    chunk = wg_ref.shape[1]
    for c in range(0, chunk, FF_TILE):
        w = min(FF_TILE, chunk - c)
        g = jnp.dot(h, wg_ref[:, c:c + w], preferred_element_type=F32)
        u = jnp.dot(h, wu_ref[:, c:c + w], preferred_element_type=F32)
        act = (_silu(g) * u).astype(BF16)
        acc = acc + jnp.dot(act, wdn_ref[c:c + w, :], preferred_element_type=F32)
    acc_scr[...] = acc

    @pl.when(j == pl.num_programs(1) - 1)
    def _():
        x3_ref[...] = x2_scr[...] + _rms(acc, gpost_ref[...])


def _ffn(o, x1, w_xo, g_xa, g_pre, w_gu, w_down, g_post, layer):
    n, d = x1.shape
    d_ff = w_down.shape[-2]
    tm = min(ROW_TILE, n)
    row = pl.BlockSpec((tm, d), lambda i, j: (i, 0))
    vec = _const_spec((1, d))
    nc = FFN_CHUNKS
    ck = d_ff // nc
    gate = pl.BlockSpec((None, d, ck), lambda i, j: (layer, 0, j))
    up = pl.BlockSpec((None, d, ck), lambda i, j: (layer, 0, nc + j))
    down = pl.BlockSpec((None, ck, d), lambda i, j: (layer, j, 0))
    return pl.pallas_call(
        _ffn_kernel,
        grid=(n // tm, nc),
        in_specs=[row, row, _wspec(w_xo, layer), vec, vec, gate, up, down, vec],
        out_specs=row,
        out_shape=jax.ShapeDtypeStruct((n, d), F32),
        scratch_shapes=[pltpu.VMEM((tm, d), F32), pltpu.VMEM((tm, d), BF16), pltpu.VMEM((tm, d), F32)],
        compiler_params=_params(("parallel", "arbitrary")),
        name="attn_out_ffn",
    )(o, x1, w_xo, g_xa, g_pre, w_gu, w_gu, w_down, g_post)


def _mix_out(ys_ref, yr_ref, yg_ref, yh_ref, x_ref, wo_ref, gpost_ref):
    mix = _bdot(ys_ref[...], wo_ref[0:GROUP_W, :])
    mix = mix + _bdot(yr_ref[...], wo_ref[GROUP_W:2 * GROUP_W, :])
    mix = mix + _bdot(yg_ref[...], wo_ref[2 * GROUP_W:3 * GROUP_W, :])
    mix = mix + _bdot(yh_ref[...], wo_ref[3 * GROUP_W:4 * GROUP_W, :])
    return x_ref[...] + _rms(mix, gpost_ref[...])


def _xattn_ffn_kernel(ys_ref, yr_ref, yg_ref, yh_ref, x_ref, wo_ref, gmix_ref, k_ref, v_ref, gq_ref, wq_ref,
                      wxo_ref, gxa_ref, gpre_ref, wgu_ref, wdn_ref, gpost_ref, x3_ref, o_scr, *, d_ff):
    x1 = _mix_out(ys_ref, yr_ref, yg_ref, yh_ref, x_ref, wo_ref, gmix_ref)
    q = _bdot(_rms(x1, gq_ref[...]), wq_ref[...])
    scale = XA_HD ** -0.5
    for h in range(XA_H):
        sl = slice(h * XA_HD, (h + 1) * XA_HD)
        s = _bdot_nt(q[:, sl], k_ref[0, :, sl]) * scale
        o_scr[:, sl] = _softmax_pv(s, v_ref[0, :, sl]).astype(BF16)
    x3_ref[...] = _ffn_tail(o_scr[...], x1, wxo_ref, gxa_ref, gpre_ref, wgu_ref, wdn_ref, gpost_ref, d_ff)


def _xattn_ffn(ys, x, w_out, g_mix, k, v, g_q, w_xq, w_xo, g_xa, g_pre, w_gu, w_down, g_post, seq_len, layer):
    n, d = x.shape
    n_seq, n_mem, _ = k.shape
    d_ff = w_down.shape[-2]
    nt = seq_len // XA_TILE
    row = lambda wd: pl.BlockSpec((XA_TILE, wd), lambda b, t: (b * nt + t, 0))
    kv = pl.BlockSpec((1, n_mem, d), lambda b, t: (b, 0, 0))
    vec = _const_spec((1, d))
    return pl.pallas_call(
        functools.partial(_xattn_ffn_kernel, d_ff=d_ff),
        grid=(n_seq, nt),
        in_specs=[row(GROUP_W)] * 4 + [row(d), _wspec(w_out, layer), vec, kv, kv, vec, _wspec(w_xq, layer),
                                       _wspec(w_xo, layer), vec, vec, _wspec(w_gu, layer),
                                       _wspec(w_down, layer), vec],
        out_specs=row(d),
        out_shape=jax.ShapeDtypeStruct((n, d), F32),
        scratch_shapes=[pltpu.VMEM((XA_TILE, d), BF16)],
        compiler_params=_params(("parallel", "parallel"), XA_VMEM_LIMIT),
        name="xattn_ffn",
    )(*ys, x, w_out, g_mix, k, v, g_q, w_xq, w_xo, g_xa, g_pre, w_gu, w_down, g_post)


def _softmax_pv(s, v):
    m = jnp.max(s, axis=-1, keepdims=True)
    p = jnp.exp(s - m)
    return _bdot(p, v) / jnp.sum(p, axis=-1, keepdims=True)


def _attn_sample_kernel(q_ref, k_ref, v_ref, o_ref, *, seqs, seg):
    scale = XA_HD ** -0.5
    rows = seqs * seg
    n_mem = k_ref.shape[2]
    wq = jnp.concatenate([q_ref[:, h * XA_HD:(h + 1) * XA_HD] for h in range(XA_H)], axis=0)
    wshape = (XA_H * rows, n_mem * XA_H)
    valid = (_iota(wshape, 0) >> (rows.bit_length() - 1)) == (_iota(wshape, 1) & (XA_H - 1))
    row_seq = _iota((rows, XA_HD), 0) >> (seg.bit_length() - 1)
    accs = [jnp.zeros((rows, XA_HD), F32) for _ in range(XA_H)]
    for i in range(seqs):
        k2 = k_ref[0, i].reshape(n_mem * XA_H, XA_HD)
        v2 = v_ref[0, i].reshape(n_mem * XA_H, XA_HD)
        s = jnp.where(valid, _bdot_nt(wq, k2) * scale, -jnp.inf)
        o = _softmax_pv(s, v2)
        for h in range(XA_H):
            accs[h] = jnp.where(row_seq == i, o[h * rows:(h + 1) * rows], accs[h])
    for h in range(XA_H):
        o_ref[:, h * XA_HD:(h + 1) * XA_HD] = accs[h]


def _attention_sample(q, cache_k, cache_v, layer, seq_len):
    n, d = q.shape
    _, n_seq, n_mem, heads, hd = cache_k.shape
    kv = pl.BlockSpec((1, ATT_SEQS, n_mem, heads, hd), lambda i: (layer, i, 0, 0, 0))
    row = pl.BlockSpec((ATT_SEQS * seq_len, d), lambda i: (i, 0))
    return pl.pallas_call(
        functools.partial(_attn_sample_kernel, seqs=ATT_SEQS, seg=seq_len),
        grid=(n_seq // ATT_SEQS,),
        in_specs=[row, kv, kv],
        out_specs=row,
        out_shape=jax.ShapeDtypeStruct((n, d), F32),
        compiler_params=_params(("parallel",)),
        name="xattn_sample",
    )(q, cache_k, cache_v)


def _ssd_intra(q, k, v, cum_x, cum_t, mask):
    sc = _bdot_nt(q, k)
    outs = []
    for hh in range(2):
        col = cum_x[:, hh * HEAD:(hh + 1) * HEAD]
        dec = jnp.exp2(jnp.where(mask, col - cum_t[hh:hh + 1, :], -jnp.inf))
        outs.append(_bdot(sc * dec, v[:, hh * HEAD:(hh + 1) * HEAD]))
    return jnp.concatenate(outs, axis=-1)


def _ssd_common(u, dtraw_ref, dtb_ref, alog_ref, conv, seg_shift):
    rows = u.shape[0]
    xbc = _silu(conv)
    xs = xbc[:, 0:GROUP_W]
    bm = xbc[:, GROUP_W:2 * GROUP_W]
    cm = xbc[:, 2 * GROUP_W:3 * GROUP_W]
    dt = _softplus(dtraw_ref[...] + dtb_ref[...])
    la = (-jnp.exp(alog_ref[...]) * LOG2_E) * dt
    cum = _seg_cumsum(la, seg_shift)
    expand = _head_expand_mat()
    dt_x = _dot_sel(dt, expand)
    cum_x = _dot_sel(cum, expand)
    sel8 = (_iota((8, 128), 0) == _iota((8, 128), 1)).astype(BF16)
    cum_t = _sel_dot_nt(sel8, cum)
    expand_full = ((_iota((128, 4 * 128), 1) >> 7) == _iota((128, 4 * 128), 0)).astype(BF16)
    cum_full = _dot_sel(cum, expand_full)
    return xs, bm, cm, dt_x, cum_x, cum_t, cum_full


def _ssd_finish(y, xs, z, d_ref, g_ref):
    y = y + d_ref[...] * xs
    return _rms(y * _silu(z), g_ref[...])


def _ssd_prompt_kernel(u_ref, dtraw_ref, cw_ref, cb_ref, dtb_ref, alog_ref, d_ref, g_ref,
                       y_ref, s_out_ref, buf_out_ref, s_ref, xx_ref, y_scr):
    t = pl.program_id(1)
    rows = u_ref.shape[0]

    @pl.when(t == 0)
    def _():
        s_ref[...] = jnp.zeros(s_ref.shape, F32)

    u = u_ref[...]
    conv = _conv_taps_carry(u[:, GROUP_W:], xx_ref, cw_ref, cb_ref, t == 0)
    xs, bm, cm, dt_x, cum_x, _, cum_full = _ssd_common(u, dtraw_ref, dtb_ref, alog_ref, conv, 6)
    v = xs * dt_x
    ecum_x = jnp.exp2(cum_x)
    hmask01 = _head_block_mask().astype(BF16)
    gmask01 = ((_iota((GROUP_W, GROUP_W), 0) >> 7) == (_iota((GROUP_W, GROUP_W), 1) >> 7)).astype(BF16)
    wrow = _iota((CHUNK, GROUP_W), 0)
    wcol = _iota((CHUNK, GROUP_W), 1) & (CHUNK - 1)
    n_chunks = rows // CHUNK
    groups = [slice(g * 128, (g + 1) * 128) for g in range(SSD_G)]
    a_wide, ds_t = [], []
    for c in range(n_chunks):
        rs = slice(c * CHUNK, (c + 1) * CHUNK)
        cum_c = cum_x[rs]
        last_x = cum_x[c * CHUNK + CHUNK - 1:(c + 1) * CHUNK, :]
        vend = v[rs] * jnp.exp2(last_x - cum_c)
        cum_row = jnp.sum(jnp.where(wrow == wcol, cum_c, 0.0), axis=0, keepdims=True)
        dec = jnp.exp2(jnp.where(wcol <= wrow, cum_c - cum_row, -jnp.inf))
        a_wide.append(_bdot_nt(cm[rs], _block_diag_rows(bm[rs], gmask01)) * dec)
        ds_t.append([_bdot_tn(vend[:, ls], bm[rs, ls]) for ls in groups])
    s_in = []
    s_cur = [s_ref[ls, :] for ls in groups]
    for c in range(n_chunks):
        last_full = cum_full[c * CHUNK + CHUNK - 1:(c + 1) * CHUNK, :]
        s_in.append([x.astype(BF16) for x in s_cur])
        for g in range(SSD_G):
            decay = jnp.concatenate(
                [jnp.broadcast_to(jnp.exp2(last_full[:, h * 128:(h + 1) * 128]), (HEAD, 128))
                 for h in (2 * g, 2 * g + 1)], axis=0)
            s_cur[g] = decay * s_cur[g] + ds_t[c][g]
    for g, ls in enumerate(groups):
        s_ref[ls, :] = s_cur[g]
    for c in range(n_chunks):
        rs = slice(c * CHUNK, (c + 1) * CHUNK)
        y_intra = _bdot(a_wide[c], _block_diag_rows(v[rs], hmask01))
        y_inter = [_bdot_nt(cm[rs, ls], s_in[c][g]) for g, ls in enumerate(groups)]
        y_scr[rs, :] = y_intra + jnp.concatenate(y_inter, axis=-1) * ecum_x[rs]
    y_ref[...] = _ssd_finish(y_scr[...], xs, u[:, 0:GROUP_W], d_ref, g_ref)

    @pl.when(t == pl.num_programs(1) - 1)
    def _():
        for h in range(N_HEADS):
            s_out_ref[0, h] = s_ref[h * HEAD:(h + 1) * HEAD, :]
        buf_out_ref[0] = xx_ref[pl.ds(8 - (CONV_W - 1), CONV_W - 1), :]


def _ssd_sample_kernel(u_ref, dtraw_ref, s0_ref, buf_ref, cw_ref, cb_ref, dtb_ref, alog_ref, d_ref,
                       g_ref, y_ref, s_out_ref, buf_out_ref, *, seqs):
    rows = u_ref.shape[0]
    u = u_ref[...]
    conv, new_buf = _conv_taps_seq4(u[:, GROUP_W:], buf_ref[...], cw_ref, cb_ref)
    buf_out_ref[...] = new_buf
    mask = _seg_causal_mask(rows, 2)
    xs, bm, cm, dt_x, cum_x, cum_t, cum_full = _ssd_common(u, dtraw_ref, dtb_ref, alog_ref, conv, 2)
    v = xs * dt_x
    ecum_x = jnp.exp2(cum_x)
    pick_last = (_iota((rows, rows), 1) == (_iota((rows, rows), 0) | 3)).astype(BF16)
    last_x = _sel_dot(pick_last, cum_x)
    vend = v * jnp.exp2(last_x - cum_x)
    row_seq = _iota((rows, 128), 0) >> 2
    ys = []
    for g in range(SSD_G):
        ls = slice(g * 128, (g + 1) * 128)
        q = cm[:, ls]
        k = bm[:, ls]
        y_g = _ssd_intra(q, k, v[:, ls], cum_x[:, ls], cum_t[2 * g:2 * g + 2, :], mask)
        y_int = jnp.zeros((rows, 128), F32)
        for i in range(seqs):
            in_seq = row_seq == i
            s_prev = jnp.concatenate([s0_ref[i, 2 * g], s0_ref[i, 2 * g + 1]], axis=0)
            y_int = jnp.where(in_seq, _bdot_nt(q, s_prev), y_int)
            ds = _bdot_tn(jnp.where(in_seq, vend[:, ls], 0.0), k)
            for hh in range(2):
                h = 2 * g + hh
                decay = jnp.exp2(cum_full[4 * i + 3:4 * i + 4, h * 128:(h + 1) * 128])
                s_out_ref[i, h] = decay * s0_ref[i, h] + ds[hh * HEAD:(hh + 1) * HEAD, :]
        ys.append(y_g + y_int * ecum_x[:, ls])
    y_ref[...] = _ssd_finish(jnp.concatenate(ys, axis=-1), xs, u[:, 0:GROUP_W], d_ref, g_ref)


def _ssd(u, dtraw, state, buf, cw, cb, dtb, alog, d_x, gain, n_seq, seq_len, layer, prev=None):
    n = u.shape[0]
    cch = cw.shape[1]
    vec = lambda wd: _const_spec((1, wd))
    common_in = [_const_spec(cw.shape), vec(cch), vec(128), vec(128), vec(GROUP_W), vec(GROUP_W)]
    if state is None:
        nt = seq_len // MIX_TILE
        row = lambda wd: pl.BlockSpec((MIX_TILE, wd), lambda b, t: (b * nt + t, 0))
        return pl.pallas_call(
            _ssd_prompt_kernel,
            grid=(n_seq, nt),
            in_specs=[row(u.shape[1]), row(128)] + common_in,
            out_specs=[row(GROUP_W),
                       pl.BlockSpec((1, N_HEADS, HEAD, SSD_N), lambda b, t: (b, 0, 0, 0)),
                       pl.BlockSpec((1, CONV_W - 1, cch), lambda b, t: (b, 0, 0))],
            out_shape=[jax.ShapeDtypeStruct((n, GROUP_W), F32),
                       jax.ShapeDtypeStruct((n_seq, N_HEADS, HEAD, SSD_N), F32),
                       jax.ShapeDtypeStruct((n_seq, CONV_W - 1, cch), F32)],
            scratch_shapes=[pltpu.VMEM((GROUP_W, SSD_N), F32), pltpu.VMEM((MIX_TILE + 8, cch), F32),
                            pltpu.VMEM((MIX_TILE, GROUP_W), F32)],
            compiler_params=_params(("parallel", "arbitrary")),
            name="ssd_prompt",
        )(u, dtraw, cw, cb, dtb, alog, d_x, gain)
    rows = SAMPLE_SEQS * seq_len
    row = lambda wd: pl.BlockSpec((rows, wd), lambda i: (i, 0))
    st = pl.BlockSpec((None, SAMPLE_SEQS, N_HEADS, HEAD, SSD_N), lambda i: (layer, i, 0, 0, 0))
    return _stack_call(
        functools.partial(_ssd_sample_kernel, seqs=SAMPLE_SEQS), prev,
        {1: ((SAMPLE_SEQS, N_HEADS, HEAD, SSD_N), lambda i: (i, 0, 0, 0))}, layer,
        in_specs=[row(u.shape[1]), row(128), st, row(cch)] + common_in,
        args=(u, dtraw, state, buf, cw, cb, dtb, alog, d_x, gain),
        grid=(n_seq // SAMPLE_SEQS,),
        out_specs=[row(GROUP_W), None, row(cch)],
        out_shape=[jax.ShapeDtypeStruct((n, GROUP_W), F32),
                   jax.ShapeDtypeStruct(state.shape, F32),
                   jax.ShapeDtypeStruct((n, cch), F32)],
        compiler_params=_params(("parallel",)),
        name="ssd_sample",
    )


def _rope_split(x, cos, sin_signed):
    swapped = jnp.concatenate([x[:, 128:], x[:, :128]], axis=1)
    return x * cos + swapped * sin_signed


def _ret_qkv(u_ref, cos_ref, sin_ref):
    u = u_ref[...]
    q = _rope_split(u[:, 0:GROUP_W], cos_ref[...], sin_ref[...])
    k = _rope_split(u[:, GROUP_W:2 * GROUP_W], cos_ref[...], sin_ref[...]) * (HEAD ** -0.5)
    return q, k, u[:, 2 * GROUP_W:3 * GROUP_W], u[:, 3 * GROUP_W:4 * GROUP_W]


def _ret_prompt_kernel(u_ref, cos_ref, sin_ref, ecum_ref, eend_ref, elast_ref, dec_ref, g_ref,
                       y_ref, s_out_ref, s_ref, y_scr):
    t = pl.program_id(1)
    rows = u_ref.shape[0]

    @pl.when(t == 0)
    def _():
        s_ref[...] = jnp.zeros(s_ref.shape, F32)

    q, k, v, gate = _ret_qkv(u_ref, cos_ref, sin_ref)
    r_i = _iota((GROUP_W, GROUP_W), 0)
    c_i = _iota((GROUP_W, GROUP_W), 1)
    kmask01 = ((r_i >> 6) == ((c_i & 127) >> 5)).astype(BF16)
    vmask01 = _head_block_mask().astype(BF16)
    smask = ((r_i & 127) >> 5) == (c_i >> 6)
    chunks = [slice(c * CHUNK, (c + 1) * CHUNK) for c in range(rows // CHUNK)]
    a_wide = [_bdot_nt(q[rs], _block_diag_rows(k[rs], kmask01)) * dec_ref[...] for rs in chunks]
    ds = [jnp.where(smask, _bdot_tn(k[rs], v[rs] * eend_ref[...]), 0.0) for rs in chunks]
    s = s_ref[...]
    s_in = []
    for d in ds:
        s_in.append(s.astype(BF16))
        s = elast_ref[...] * s + d
    s_ref[...] = s
    for rs, a, s_c in zip(chunks, a_wide, s_in):
        y_intra = _bdot(a, _block_diag_rows(v[rs], vmask01))
        y_scr[rs, :] = y_intra + _bdot(q[rs], s_c) * ecum_ref[...]
    y_ref[...] = _silu(gate) * (_head_stat(y_scr[...], True) * g_ref[...])

    @pl.when(t == pl.num_programs(1) - 1)
    def _():
        half = HEAD // 2
        for h in range(N_HEADS):
            vs = slice(h * HEAD, (h + 1) * HEAD)
            s_out_ref[0, h, 0:half, :] = s_ref[h * half:(h + 1) * half, vs]
            s_out_ref[0, h, half:HEAD, :] = s_ref[128 + h * half:128 + (h + 1) * half, vs]


def _ret_tables(block, seg):
    log_gamma = jnp.log(1.0 - jnp.exp2(-5.0 - jnp.arange(N_HEADS, dtype=F32)))
    pos = (jnp.arange(block) % seg).astype(F32)
    cum = (pos[:, None] + 1.0) * log_gamma[None, :]
    last = seg * log_gamma
    rep = lambda a: jnp.repeat(a, HEAD, axis=-1)
    same = (jnp.arange(block)[:, None] // seg) == (jnp.arange(block)[None, :] // seg)
    mask = same & (jnp.arange(block)[None, :] <= jnp.arange(block)[:, None])
    dec = jnp.exp(jnp.where(mask[None], cum.T[:, :, None] - cum.T[:, None, :], -jnp.inf))
    return (rep(jnp.exp(cum)), rep(jnp.exp(last[None, :] - cum)), rep(jnp.exp(last)[None, :]), dec)


def _ret(u, cos, sin_signed, gain, n_seq, seq_len):
    n = u.shape[0]
    vec = _const_spec((1, GROUP_W))
    nt = seq_len // MIX_TILE
    ecum, eend, elast, dec = _ret_tables(CHUNK, CHUNK)
    dec = dec.transpose(1, 0, 2).reshape(CHUNK, N_HEADS * CHUNK)
    row = lambda wd: pl.BlockSpec((MIX_TILE, wd), lambda b, t: (b * nt + t, 0))
    pos = pl.BlockSpec((MIX_TILE, GROUP_W), lambda b, t: (t, 0))
    return pl.pallas_call(
        _ret_prompt_kernel,
        grid=(n_seq, nt),
        in_specs=[row(u.shape[1]), pos, pos, _const_spec(ecum.shape), _const_spec(eend.shape), vec,
                  _const_spec(dec.shape), vec],
        out_specs=[row(GROUP_W), pl.BlockSpec((1, N_HEADS, HEAD, HEAD), lambda b, t: (b, 0, 0, 0))],
        out_shape=[jax.ShapeDtypeStruct((n, GROUP_W), F32),
                   jax.ShapeDtypeStruct((n_seq, N_HEADS, HEAD, HEAD), F32)],
        scratch_shapes=[pltpu.VMEM((GROUP_W, GROUP_W), F32), pltpu.VMEM((MIX_TILE, GROUP_W), F32)],
        compiler_params=_params(("parallel", "arbitrary")),
        name="ret_prompt",
    )(u, cos, sin_signed, ecum, eend, elast, dec, gain)


def _rg_gates(xr, wa_ref, ba_ref, wx_ref, bx_ref, lam_ref):
    r_gate = _sigmoid(_bdot(xr, wa_ref[...]) + ba_ref[...])
    i_gate = _sigmoid(_bdot(xr, wx_ref[...]) + bx_ref[...])
    log_a = (-RG_C * _softplus(-lam_ref[...])) * r_gate
    a = jnp.exp(log_a)
    b = jnp.sqrt(-jnp.tanh(log_a) * (a * a + 1.0)) * (i_gate * xr)
    return a, b


def _rg_scan(a, b, seg):
    pos = _iota(a.shape, 0) & (seg - 1)
    d = 1
    while d < seg:
        ok = pos >= d
        a_sh = jnp.where(ok, pltpu.roll(a, d, 0), 1.0)
        b_sh = jnp.where(ok, pltpu.roll(b, d, 0), 0.0)
        b = a * b_sh + b
        a = a * a_sh
        d *= 2
    return a, b


def _rg_scan_carry(a, b, h0):
    a, b = _rg_scan(a, b, 8)
    outs = []
    carry = h0
    for g in range(a.shape[0] // 8):
        h_g = b[g * 8:(g + 1) * 8] + a[g * 8:(g + 1) * 8] * carry
        outs.append(h_g)
        carry = h_g[7:8]
    return jnp.concatenate(outs, axis=0)


def _rg_prompt_kernel(u_ref, cw_ref, cb_ref, wa_ref, ba_ref, wx_ref, bx_ref, lam_ref,
                      y_ref, h_out_ref, buf_out_ref, h_ref, xx_ref):
    t = pl.program_id(1)
    rows = u_ref.shape[0]

    @pl.when(t == 0)
    def _():
        h_ref[...] = jnp.zeros(h_ref.shape, F32)

    u = u_ref[...]
    xr = _conv_taps_carry(u[:, 0:GROUP_W], xx_ref, cw_ref, cb_ref, t == 0)
    a, b = _rg_gates(xr, wa_ref, ba_ref, wx_ref, bx_ref, lam_ref)
    hseq = _rg_scan_carry(a, b, h_ref[0:1, :])
    h_ref[...] = jnp.broadcast_to(hseq[rows - 1:rows, :], h_ref.shape)
    y_ref[...] = _gelu_tanh(u[:, GROUP_W:]) * hseq

    @pl.when(t == pl.num_programs(1) - 1)
    def _():
        h_out_ref[0] = hseq[rows - 1:rows, :]
        buf_out_ref[0] = xx_ref[pl.ds(8 - (CONV_W - 1), CONV_W - 1), :]


def _rg_sample_kernel(u_ref, h0_ref, buf_ref, cw_ref, cb_ref, wa_ref, ba_ref, wx_ref, bx_ref, lam_ref,
                      y_ref, h_out_ref, buf_out_ref):
    u = u_ref[...]
    xr, new_buf = _conv_taps_seq4(u[:, 0:GROUP_W], buf_ref[...], cw_ref, cb_ref)
    buf_out_ref[...] = new_buf
    a, b = _rg_gates(xr, wa_ref, ba_ref, wx_ref, bx_ref, lam_ref)
    a_cum, h_loc = _rg_scan(a, b, 4)
    hseq = h_loc + a_cum * h0_ref[...]
    h_out_ref[...] = hseq
    y_ref[...] = _gelu_tanh(u[:, GROUP_W:]) * hseq


def _rg(u, h0_rows, buf, cw, cb, wa, ba, wx, bx, lam, n_seq, seq_len):
    n = u.shape[0]
    vec = _const_spec((1, GROUP_W))
    common_in = [_const_spec(cw.shape), vec, _const_spec(wa.shape), vec, _const_spec(wx.shape), vec, vec]
    if h0_rows is None:
        nt = seq_len // RG_TILE
        row = lambda wd: pl.BlockSpec((RG_TILE, wd), lambda b, t: (b * nt + t, 0))
        return pl.pallas_call(
            _rg_prompt_kernel,
            grid=(n_seq, nt),
            in_specs=[row(u.shape[1])] + common_in,
            out_specs=[row(GROUP_W), pl.BlockSpec((1, 1, GROUP_W), lambda b, t: (b, 0, 0)),
                       pl.BlockSpec((1, CONV_W - 1, GROUP_W), lambda b, t: (b, 0, 0))],
            out_shape=[jax.ShapeDtypeStruct((n, GROUP_W), F32),
                       jax.ShapeDtypeStruct((n_seq, 1, GROUP_W), F32),
                       jax.ShapeDtypeStruct((n_seq, CONV_W - 1, GROUP_W), F32)],
            scratch_shapes=[pltpu.VMEM((8, GROUP_W), F32), pltpu.VMEM((RG_TILE + 8, GROUP_W), F32)],
            compiler_params=_params(("parallel", "arbitrary")),
            name="rglru_prompt",
        )(u, cw, cb, wa, ba, wx, bx, lam)
    rows = SAMPLE_SEQS * seq_len
    row = lambda wd: pl.BlockSpec((rows, wd), lambda i: (i, 0))
    return pl.pallas_call(
        _rg_sample_kernel,
        grid=(n_seq // SAMPLE_SEQS,),
        in_specs=[row(u.shape[1]), row(GROUP_W), row(GROUP_W)] + common_in,
        out_specs=[row(GROUP_W)] * 3,
        out_shape=[jax.ShapeDtypeStruct((n, GROUP_W), F32)] * 3,
        compiler_params=_params(("parallel",)),
        name="rglru_sample",
    )(u, h0_rows, buf, cw, cb, wa, ba, wx, bx, lam)


def _hg_inputs(u_ref, lb_ref, seg_shift):
    u = u_ref[...]
    rows = u.shape[0]
    lb = lb_ref[...]
    fg = lb + (1.0 - lb) * _sigmoid(u[:, GROUP_W:2 * GROUP_W])
    q = _silu(u[:, 0:GROUP_W])
    k = 1.0 - fg
    v = u[:, 2 * GROUP_W:3 * GROUP_W]
    cum = _seg_cumsum(jnp.log(fg), seg_shift)
    return q, k, v, cum, u[:, 3 * GROUP_W:4 * GROUP_W]


def _hg_prompt_kernel(u_ref, lb_ref, g_ref, y_ref, s_out_ref, st_ref, y_scr):
    t = pl.program_id(1)
    rows = u_ref.shape[0]

    @pl.when(t == 0)
    def _():
        st_ref[...] = jnp.zeros(st_ref.shape, F32)

    q, k, v, cum, gate = _hg_inputs(u_ref, lb_ref, 6)
    hmask = _head_block_mask()
    hmask01 = hmask.astype(BF16)
    ones_bd = hmask01
    y_scr[...] = jnp.dot((q * k).astype(BF16), ones_bd, preferred_element_type=F32) * v
    row = _iota((rows, GROUP_W), 0)
    trow = _iota((CHUNK, GROUP_W), 0)
    tcol = _iota((CHUNK, GROUP_W), 1) & (CHUNK - 1)
    cum = cum * LOG2_E
    levels = []
    blk_last = cum
    h = 1
    while h < CHUNK:
        upper = (row & h) != 0
        qn = jnp.where(upper, q * jnp.exp2(cum - pltpu.roll(blk_last, h, 0)), 0.0)
        kn = k * jnp.exp2(blk_last - cum)
        sh = (2 * h).bit_length() - 1
        pair = ((trow >> sh) == (tcol >> sh)) & ((tcol & h) == 0)
        levels.append((pair, qn, kn))
        blk_last = jnp.where(upper, blk_last, pltpu.roll(blk_last, rows - h, 0))
        h *= 2
    q_in = q * jnp.exp2(cum)
    k_end = k * jnp.exp2(blk_last - cum)
    n_chunks = rows // CHUNK
    a_wide, ds_t = [], []
    for c in range(n_chunks):
        rs = slice(c * CHUNK, (c + 1) * CHUNK)
        a_c = jnp.zeros((CHUNK, GROUP_W), F32)
        for pair, qn, kn in levels:
            g = _bdot_nt(qn[rs], _block_diag_rows(kn[rs], hmask01))
            a_c = a_c + jnp.where(pair, g, 0.0)
        a_wide.append(a_c)
        ds_t.append(jnp.where(hmask, _bdot_tn(v[rs], k_end[rs]), 0.0))
    st = st_ref[...]
    st_in = []
    for c in range(n_chunks):
        st_in.append(st.astype(BF16))
        st = jnp.exp2(blk_last[c * CHUNK:c * CHUNK + 1, :]) * st + ds_t[c]
    st_ref[...] = st
    for c in range(n_chunks):
        rs = slice(c * CHUNK, (c + 1) * CHUNK)
        y_c = _bdot(a_wide[c], _block_diag_rows(v[rs], hmask01)) + _bdot_nt(q_in[rs], st_in[c])
        y_scr[rs, :] = y_scr[rs, :] + y_c
    y_ref[...] = _silu(gate) * (_head_stat(y_scr[...], False) * g_ref[...])

    @pl.when(t == pl.num_programs(1) - 1)
    def _():
        eye = (_iota((HEAD, HEAD), 0) == _iota((HEAD, HEAD), 1)).astype(BF16)
        for h in range(N_HEADS):
            s_out_ref[0, h] = _sel_dot_nt(eye, st_ref[h * HEAD:(h + 1) * HEAD, h * HEAD:(h + 1) * HEAD])


def _hg(u, lb, gain, n_seq, seq_len):
    n = u.shape[0]
    vec = _const_spec((1, GROUP_W))
    nt = seq_len // MIX_TILE
    row = lambda wd: pl.BlockSpec((MIX_TILE, wd), lambda b, t: (b * nt + t, 0))
    return pl.pallas_call(
        _hg_prompt_kernel,
        grid=(n_seq, nt),
        in_specs=[row(u.shape[1]), vec, vec],
        out_specs=[row(GROUP_W), pl.BlockSpec((1, N_HEADS, HEAD, HEAD), lambda b, t: (b, 0, 0, 0))],
        out_shape=[jax.ShapeDtypeStruct((n, GROUP_W), F32),
                   jax.ShapeDtypeStruct((n_seq, N_HEADS, HEAD, HEAD), F32)],
        scratch_shapes=[pltpu.VMEM((GROUP_W, GROUP_W), F32), pltpu.VMEM((MIX_TILE, GROUP_W), F32)],
        compiler_params=_params(("parallel", "arbitrary")),
        name="hgrn_prompt",
    )(u, lb, gain)


def _t_recurrence(q_scr, k_scr, v, dec_row, dec_scr, s_ref, n_seq, seq_len):
    cols = [slice(t * n_seq, (t + 1) * n_seq) for t in range(seq_len)]
    v_t = [v[:, c] for c in cols]
    outs = [jnp.zeros((HEAD, n_seq), F32) for _ in cols]
    for k in range(HEAD):
        s = s_ref[k]
        for t, c in enumerate(cols):
            dec = dec_row if dec_scr is None else dec_scr[k:k + 1, c]
            s = dec * s + k_scr[k:k + 1, c] * v_t[t]
            outs[t] = outs[t] + q_scr[k:k + 1, c] * s
        s_ref[k] = s
    return jnp.concatenate(outs, axis=1)


def _t_head_norm(o, center):
    if center:
        o = o - jnp.mean(o, axis=0, keepdims=True)
    return o * lax.rsqrt(jnp.mean(o * o, axis=0, keepdims=True) + EPS)


def _ret_t_kernel(q_ref, k_ref, v_ref, g_ref, cos_ref, sin_ref, gam_ref, gain_ref, s0_ref,
                  y_ref, s_ref, q_scr, k_scr, *, n_seq, seq_len):
    half = HEAD // 2

    def rope(x):
        swapped = jnp.concatenate([x[half:], x[:half]], axis=0)
        return x * cos_ref[...] + swapped * sin_ref[...]

    q_scr[...] = rope(q_ref[...])
    k_scr[...] = rope(k_ref[...]) * (HEAD ** -0.5)
    s_ref[0] = s0_ref[0]
    o = _t_recurrence(q_scr, k_scr, v_ref[...], gam_ref[0], None, s_ref.at[0], n_seq, seq_len)
    gain = jnp.concatenate([gain_ref[...]] * seq_len, axis=1)
    parts = [_t_head_norm(o[:, t * n_seq:(t + 1) * n_seq], True) for t in range(seq_len)]
    y_ref[...] = _silu(g_ref[...]) * (jnp.concatenate(parts, axis=1) * gain)


def _hg_t_kernel(q_ref, f_ref, i_ref, g_ref, lb_ref, gain_ref, s0_ref, y_ref, s_ref, q_scr, k_scr, f_scr,
                 *, n_seq, seq_len):
    lb = jnp.concatenate([lb_ref[...]] * seq_len, axis=1)
    fg = lb + (1.0 - lb) * _sigmoid(f_ref[...])
    q_scr[...] = _silu(q_ref[...])
    k_scr[...] = 1.0 - fg
    f_scr[...] = fg
    s_ref[0] = s0_ref[0]
    o = _t_recurrence(q_scr, k_scr, i_ref[...], None, f_scr, s_ref.at[0], n_seq, seq_len)
    gain = jnp.concatenate([gain_ref[...]] * seq_len, axis=1)
    parts = [_t_head_norm(o[:, t * n_seq:(t + 1) * n_seq], False) for t in range(seq_len)]
    y_ref[...] = _silu(g_ref[...]) * (jnp.concatenate(parts, axis=1) * gain)


def _t_specs(n, n_seq):
    head_rows = lambda section: pl.BlockSpec((HEAD, n), lambda h, s=section: (N_HEADS * s + h, 0))
    per_head = pl.BlockSpec((HEAD, n_seq), lambda h: (h, 0))
    state_in = lambda layer: pl.BlockSpec((None, 1, HEAD, HEAD, n_seq), lambda h: (layer, h, 0, 0, 0))
    state_out = ((1, HEAD, HEAD, n_seq), lambda h: (h, 0, 0, 0))
    return head_rows, per_head, state_in, state_out


def _ret_t(u_t, cos_t, sin_t, gam, gain_b, state_t, n_seq, seq_len, layer, prev):
    n = u_t.shape[1]
    head_rows, per_head, state_in, state_out = _t_specs(n, n_seq)
    return _stack_call(
        functools.partial(_ret_t_kernel, n_seq=n_seq, seq_len=seq_len), prev, {1: state_out}, layer,
        in_specs=[head_rows(0), head_rows(1), head_rows(2), head_rows(3), _const_spec(cos_t.shape),
                  _const_spec(sin_t.shape), pl.BlockSpec((1, 1, n_seq), lambda h: (h, 0, 0)), per_head,
                  state_in(layer)],
        args=(u_t, u_t, u_t, u_t, cos_t, sin_t, gam, gain_b, state_t),
        grid=(N_HEADS,),
        out_specs=[pl.BlockSpec((HEAD, n), lambda h: (h, 0)), None],
        out_shape=[jax.ShapeDtypeStruct((GROUP_W, n), F32), jax.ShapeDtypeStruct(state_t.shape, F32)],
        scratch_shapes=[pltpu.VMEM((HEAD, n), F32)] * 2,
        compiler_params=_params(("parallel",)),
        name="ret_sample_t",
    )


def _hg_t(u_t, lb_b, gain_b, state_t, n_seq, seq_len, layer, prev):
    n = u_t.shape[1]
    head_rows, per_head, state_in, state_out = _t_specs(n, n_seq)
    return _stack_call(
        functools.partial(_hg_t_kernel, n_seq=n_seq, seq_len=seq_len), prev, {1: state_out}, layer,
        in_specs=[head_rows(0), head_rows(1), head_rows(2), head_rows(3), per_head, per_head, state_in(layer)],
        args=(u_t, u_t, u_t, u_t, lb_b, gain_b, state_t),
        grid=(N_HEADS,),
        out_specs=[pl.BlockSpec((HEAD, n), lambda h: (h, 0)), None],
        out_shape=[jax.ShapeDtypeStruct((GROUP_W, n), F32), jax.ShapeDtypeStruct(state_t.shape, F32)],
        scratch_shapes=[pltpu.VMEM((HEAD, n), F32)] * 3,
        compiler_params=_params(("parallel",)),
        name="hgrn_sample_t",
    )


def _rope_tables_t(pos0, seq_len, n_seq):
    half = HEAD // 2
    pos = pos0 + jnp.arange(seq_len, dtype=F32)
    inv = ROPE_BASE ** (-jnp.arange(half, dtype=F32) / half)
    ang = pos[:, None] * inv
    cos = jnp.tile(jnp.cos(ang).T, (2, 1))
    sin = jnp.sin(ang).T
    sin_signed = jnp.concatenate([-sin, sin], axis=0)
    return jnp.repeat(cos, n_seq, axis=1), jnp.repeat(sin_signed, n_seq, axis=1)


def _block_diag(w):
    h, i, j = w.shape
    eye = jnp.eye(h, dtype=w.dtype)
    return (eye[:, None, :, None] * w[:, :, None, :]).reshape(h * i, h * j)


def _rope_tables(pos0, seq_len):
    half = HEAD // 2
    pos = pos0 + jnp.arange(seq_len, dtype=F32)
    inv = ROPE_BASE ** (-jnp.arange(half, dtype=F32) / half)
    ang = pos[:, None] * inv
    cos = jnp.tile(jnp.cos(ang), (1, 2 * N_HEADS))
    sin = jnp.sin(ang)
    sin_signed = jnp.concatenate([jnp.tile(-sin, (1, N_HEADS)), jnp.tile(sin, (1, N_HEADS))], axis=-1)
    return cos, sin_signed


def _layer_weights(P, l):
    row = lambda a: a.reshape(1, -1)
    pad128 = lambda a: jnp.pad(a, (0, 128 - a.shape[0])).reshape(1, 128)
    lb_sm = jax.nn.softmax(P['hg_lb'].astype(F32), axis=0)
    lb = (jnp.cumsum(lb_sm, axis=0) - lb_sm[0])[l]
    return dict(
        w_in=P['w_in_prepped'], w_in_t=P['w_in_t'],
        ln_mix_pre=row(P['ln_mix_pre'][l]), ln_mix_post=row(P['ln_mix_post'][l]),
        ln_xa_pre=row(P['ln_xa_pre'][l]), ln_xa_post=row(P['ln_xa_post'][l]),
        ln_ffn_pre=row(P['ln_ffn_pre'][l]), ln_ffn_post=row(P['ln_ffn_post'][l]),
        ssd_cw=P['ssd_conv_w'][l], ssd_cb=row(P['ssd_conv_b'][l]),
        ssd_dtb=pad128(P['ssd_dt_bias'][l]), ssd_alog=pad128(P['ssd_A_log'][l]),
        ssd_d=row(jnp.repeat(P['ssd_D'][l], HEAD)), ssd_norm=row(P['ssd_norm'][l]),
        ret_norm=row(P['ret_norm'][l]),
        rg_cw=P['rg_conv_w'][l], rg_cb=row(P['rg_conv_b'][l]),
        rg_wa=_block_diag(P['rg_wa'][l]).astype(BF16), rg_ba=row(P['rg_ba'][l]),
        rg_wx=_block_diag(P['rg_wx'][l]).astype(BF16), rg_bx=row(P['rg_bx'][l]),
        rg_lam=row(P['rg_lambda'][l]),
        hg_lb=row(lb), hg_norm=row(P['hg_norm'][l]),
        w_out=P['w_out_bf16'], w_xq=P['w_xq_bf16'], w_xo=P['w_xo_bf16'], w_gu=P['w_gu_bf16'],
        w_down=P['w_down_bf16'], ln_mem=row(P['ln_mem'][l]), w_xkv=P['w_xkv'],
    )


def _layer(x, W, k_mem, v_mem, st, n_seq, seq_len, pos0, layer, prev=None):
    if st is None:
        u_ssd, u_ret, u_rg, u_hg, u_dt = _norm_matmul(
            x, W['ln_mix_pre'], W['w_in'], (IN_COLS_SSD, IN_COLS_RET_SPLIT, IN_COLS_RG, IN_COLS_HG, IN_COLS_DT),
            "in_proj", layer=layer)
        cos, sin_signed = _rope_tables(pos0, seq_len)
        y_ssd, s_ssd, b_ssd = _ssd(u_ssd, u_dt, None, None, W['ssd_cw'], W['ssd_cb'], W['ssd_dtb'],
                                   W['ssd_alog'], W['ssd_d'], W['ssd_norm'], n_seq, seq_len, layer)
        y_ret, s_ret = _ret(u_ret, cos, sin_signed, W['ret_norm'], n_seq, seq_len)
        y_rg, h_rg, b_rg = _rg(u_rg, None, None, W['rg_cw'], W['rg_cb'], W['rg_wa'], W['rg_ba'],
                               W['rg_wx'], W['rg_bx'], W['rg_lam'], n_seq, seq_len)
        h_rg = h_rg.reshape(n_seq, GROUP_W)
        y_hg, s_hg = _hg(u_hg, W['hg_lb'], W['hg_norm'], n_seq, seq_len)
    else:
        ssd_s, ssd_buf, ret_s, rg_h, rg_buf, hg_s = st
        u_ssd, u_rg, u_dt = _norm_matmul(x, W['ln_mix_pre'], W['w_in'], (IN_COLS_SSD, IN_COLS_RG, IN_COLS_DT),
                                         "in_proj", layer=layer)
        to_tb = lambda a: a.reshape(n_seq, seq_len, -1).transpose(1, 0, 2).reshape(n_seq * seq_len, -1)
        from_t = lambda a: a.reshape(-1, seq_len, n_seq).transpose(2, 1, 0).reshape(n_seq * seq_len, -1)
        bcast = lambda g: jnp.broadcast_to(g.reshape(-1, 1), (g.size, n_seq))
        ut_ret, ut_hg = _in_proj_t(to_tb(x), W['ln_mix_pre'], W['w_in_t'], layer)
        cos_t, sin_t = _rope_tables_t(pos0, seq_len, n_seq)
        log_gamma = jnp.log(1.0 - jnp.exp2(-5.0 - jnp.arange(N_HEADS, dtype=F32)))
        gam = jnp.broadcast_to(jnp.exp(log_gamma)[:, None, None], (N_HEADS, 1, n_seq))
        yt_ret, s_ret = _ret_t(ut_ret, cos_t, sin_t, gam, bcast(W['ret_norm']), ret_s, n_seq, seq_len, layer,
                               prev and (prev[1],))
        yt_hg, s_hg = _hg_t(ut_hg, bcast(W['hg_lb']), bcast(W['hg_norm']), hg_s, n_seq, seq_len, layer,
                            prev and (prev[2],))
        y_ret, y_hg = from_t(yt_ret), from_t(yt_hg)
        pad_rows = lambda b: jnp.pad(b, ((0, 0), (0, 1), (0, 0))).reshape(n_seq * 4, b.shape[-1])
        y_ssd, s_ssd, b_ssd = _ssd(u_ssd, u_dt, ssd_s, pad_rows(ssd_buf), W['ssd_cw'], W['ssd_cb'],
                                   W['ssd_dtb'], W['ssd_alog'], W['ssd_d'], W['ssd_norm'], n_seq, seq_len, layer,
                                   prev and (prev[0],))
        b_ssd = b_ssd.reshape(n_seq, 4, -1)[:, :CONV_W - 1]
        y_rg, h_rows, b_rg = _rg(u_rg, jnp.repeat(rg_h, seq_len, axis=0), pad_rows(rg_buf), W['rg_cw'],
                                 W['rg_cb'], W['rg_wa'], W['rg_ba'], W['rg_wx'], W['rg_bx'], W['rg_lam'],
                                 n_seq, seq_len)
        h_rg = h_rows.reshape(n_seq, seq_len, GROUP_W)[:, seq_len - 1]
        b_rg = b_rg.reshape(n_seq, 4, -1)[:, :CONV_W - 1]
    ys = (y_ssd, y_ret, y_rg, y_hg)
    if st is None:
        x3 = _xattn_ffn(ys, x, W['w_out'], W['ln_mix_post'], k_mem, v_mem, W['ln_xa_pre'], W['w_xq'], W['w_xo'],
                        W['ln_xa_post'], W['ln_ffn_pre'], W['w_gu'], W['w_down'], W['ln_ffn_post'], seq_len,
                        layer)
    else:
        x1, q = _out_proj(ys, x, W['w_out'], W['ln_mix_post'], W['ln_xa_pre'], W['w_xq'], layer)
        o = _attention_sample(q, k_mem, v_mem, layer, seq_len)
        x3 = _ffn(o, x1, W['w_xo'], W['ln_xa_post'], W['ln_ffn_pre'], W['w_gu'], W['w_down'],
                  W['ln_ffn_post'], layer)
    return x3, (s_ssd, b_ssd, s_ret, h_rg, b_rg, s_hg)


def kernel(x_prompt, x_sample, state_ssd, state_ssd_conv, state_ret, state_rglru, state_rglru_conv, state_hgrn, cache_mem_k, cache_mem_v, mem_prompt, ln_mix_pre, ln_mix_post, ln_xa_pre, ln_xa_post, ln_ffn_pre, ln_ffn_post, w_in, ssd_conv_w, ssd_conv_b, ssd_dt_bias, ssd_A_log, ssd_D, ssd_norm, ret_norm, rg_conv_w, rg_conv_b, rg_wa, rg_ba, rg_wx, rg_bx, rg_lambda, hg_lb, hg_norm, w_out, ln_mem, w_xq, w_xkv, w_xo, w_gu, w_down):
    P = dict(ln_mix_pre=ln_mix_pre, ln_mix_post=ln_mix_post, ln_xa_pre=ln_xa_pre, ln_xa_post=ln_xa_post,
             ln_ffn_pre=ln_ffn_pre, ln_ffn_post=ln_ffn_post, w_in=w_in, ssd_conv_w=ssd_conv_w,
             ssd_conv_b=ssd_conv_b, ssd_dt_bias=ssd_dt_bias, ssd_A_log=ssd_A_log, ssd_D=ssd_D,
             ssd_norm=ssd_norm, ret_norm=ret_norm, rg_conv_w=rg_conv_w, rg_conv_b=rg_conv_b,
             rg_wa=rg_wa, rg_ba=rg_ba, rg_wx=rg_wx, rg_bx=rg_bx, rg_lambda=rg_lambda, hg_lb=hg_lb,
             hg_norm=hg_norm, w_out=w_out, ln_mem=ln_mem, w_xq=w_xq, w_xkv=w_xkv, w_xo=w_xo,
             w_gu=w_gu, w_down=w_down)
    P['w_in_prepped'], P['w_in_t'] = _in_weight_prep(w_in)
    for name in ('w_out', 'w_xq', 'w_xo', 'w_gu', 'w_down'):
        P[name + '_bf16'] = P[name].astype(BF16)
    ssd_t = jnp.swapaxes(state_ssd, -1, -2)
    ret_t = jnp.transpose(state_ret, (0, 2, 3, 4, 1))
    hgrn_t = jnp.transpose(state_hgrn, (0, 2, 3, 4, 1))
    depth = w_in.shape[0]
    bp, tp, d = x_prompt.shape
    bs, ts, _ = x_sample.shape
    n_mem = mem_prompt.shape[1]
    y_p = x_prompt.reshape(bp * tp, d)
    y_s = x_sample.reshape(bs * ts, d)
    mem = mem_prompt.reshape(bp * n_mem, d)
    p_st, s_st = [], []
    mem_prev = samp_prev = None
    for l in range(depth):
        W = _layer_weights(P, l)
        mk, mv, mk5, mv5 = _mem_kv(mem, W['ln_mem'], W['w_xkv'], l, n_mem, mem_prev)
        mem_prev = (mk5, mv5)
        mk = mk.reshape(bp, n_mem, d)
        mv = mv.reshape(bp, n_mem, d)
        y_p, st_p = _layer(y_p, W, mk, mv, None, bp, tp, 0.0, l)
        p_st.append(st_p)
        st_l = (ssd_t, state_ssd_conv[l], ret_t, state_rglru[l], state_rglru_conv[l], hgrn_t)
        y_s, st_s = _layer(y_s, W, cache_mem_k, cache_mem_v, st_l, bs, ts, float(PAST_LEN), l, samp_prev)
        samp_prev = (st_s[0], st_s[2], st_s[5])
        s_st.append(st_s)
    stack = lambda sts, i: jnp.stack([s[i] for s in sts], axis=0)
    return (y_p.reshape(bp, tp, d), y_s.reshape(bs, ts, d),
            jnp.swapaxes(stack(p_st, 0), -1, -2), stack(p_st, 1), stack(p_st, 2), stack(p_st, 3), stack(p_st, 4),
            stack(p_st, 5),
            mem_prev[0], mem_prev[1],
            jnp.swapaxes(samp_prev[0], -1, -2), stack(s_st, 1), jnp.transpose(samp_prev[1], (0, 4, 1, 2, 3)),
            stack(s_st, 3), stack(s_st, 4), jnp.transpose(samp_prev[2], (0, 4, 1, 2, 3)))
```
